```python
import math
import jax, jax.numpy as jnp
from jax import lax
import numpy as np

D_MODEL = 2048
BATCH = 8
SEQ = 2048
DEPTH = 4

HEAD_DIM = 64
N_Q_HEADS = D_MODEL // 2 // HEAD_DIM
N_KV_HEADS = N_Q_HEADS // 4
ATTN_WIDTH = N_Q_HEADS * HEAD_DIM
KV_WIDTH = N_KV_HEADS * HEAD_DIM
WINDOW = 128
ATTN_BLOCK = 128
ROPE_THETA = 10000.0
POOL_WINDOWS = (2, 4, 8, 16)
POOL_WIDTH = D_MODEL // 4
POOL_GROUP = POOL_WIDTH // len(POOL_WINDOWS)
SSM_WIDTH = D_MODEL // 4
SSM_GROUP = 16
SSM_N_GROUPS = SSM_WIDTH // SSM_GROUP
SSM_STATE = 64
MIX_WIDTH = ATTN_WIDTH + POOL_WIDTH + SSM_WIDTH
IN_WIDTH = ATTN_WIDTH + 2 * KV_WIDTH + POOL_WIDTH + SSM_WIDTH
D_FF = 5504
CONV_WIDTH = 3
LN_EPS = 1e-5
DEEPNORM_ALPHA = (2 * DEPTH) ** 0.25
DEEPNORM_BETA = (8 * DEPTH) ** -0.25

kernel_name = 'hybrid_swa_pool_s5_convffn'

F32 = jnp.float32


def layer_norm(x, g, b):
    xf = x.astype(F32)
    mu = xf.mean(-1, keepdims=True)
    var = jnp.square(xf - mu).mean(-1, keepdims=True)
    return ((xf - mu) * lax.rsqrt(var + LN_EPS) * g.astype(F32) + b.astype(F32)).astype(x.dtype)


def rope(t, pos):
    half = HEAD_DIM // 2
    inv = ROPE_THETA ** (-jnp.arange(half, dtype=F32) / half)
    ang = pos.astype(F32)[:, None] * inv[None, :]
    cos = jnp.cos(ang)[None, :, None, :]
    sin = jnp.sin(ang)[None, :, None, :]
    tf = t.astype(F32)
    t1, t2 = tf[..., :half], tf[..., half:]
    return jnp.concatenate([t1 * cos - t2 * sin, t2 * cos + t1 * sin], -1).astype(t.dtype)


def sliding_window_attention(q, k, v, sinks):
    bsz, s_len = q.shape[0], q.shape[1]
    nb = s_len // ATTN_BLOCK
    grp = N_Q_HEADS // N_KV_HEADS
    qb = q.reshape(bsz, nb, ATTN_BLOCK, N_KV_HEADS, grp, HEAD_DIM).astype(F32)

    def band(t):
        tb = t.reshape(bsz, nb, ATTN_BLOCK, N_KV_HEADS, HEAD_DIM)
        prev = jnp.concatenate([jnp.zeros_like(tb[:, :1]), tb[:, :-1]], axis=1)
        return jnp.concatenate([prev, tb], axis=2).astype(F32)

    kw, vw = band(k), band(v)
    s = jnp.einsum('bnqkgd,bnjkd->bnkgqj', qb, kw) * (HEAD_DIM ** -0.5)
    i = jnp.arange(ATTN_BLOCK)[:, None]
    j = jnp.arange(2 * ATTN_BLOCK)[None, :]
    dist = i + ATTN_BLOCK - j
    in_win = (dist >= 0) & (dist < WINDOW)
    blk = jnp.arange(nb)[:, None, None]
    valid = in_win[None] & ((blk > 0) | (j[None] >= ATTN_BLOCK))
    s = jnp.where(valid[None, :, None, None], s, -1e30)
    sink = sinks.astype(F32).reshape(N_KV_HEADS, grp)[None, None, :, :, None, None]
    m = jnp.maximum(s.max(-1, keepdims=True), sink)
    p = jnp.exp(s - m)
    denom = p.sum(-1, keepdims=True) + jnp.exp(sink - m)
    o = jnp.einsum('bnkgqj,bnjkd->bnqkgd', p / denom, vw)
    return o.reshape(bsz, s_len, ATTN_WIDTH).astype(q.dtype)


def multiscale_pool(u, pool_w, pool_scale):
    uf = u.astype(F32)
    s_len = u.shape[1]
    t = jnp.arange(s_len, dtype=F32)[None, :, None]
    outs = []
    for gi, w in enumerate(POOL_WINDOWS):
        ug = uf[..., gi * POOL_GROUP:(gi + 1) * POOL_GROUP]
        c = jnp.cumsum(ug, axis=1)
        c_shift = jnp.pad(c, ((0, 0), (w, 0), (0, 0)))[:, :s_len]
        mean = (c - c_shift) / jnp.minimum(t + 1.0, float(w))
        outs.append(jnp.einsum('bsc,cd->bsd', mean - ug, pool_w[gi].astype(F32)))
    return (jnp.concatenate(outs, -1) * pool_scale.astype(F32)).astype(u.dtype)


def s5_ssm(u, lam_re, lam_im, log_dt, b_re, b_im, c_re, c_im, d, glu_w):
    bsz, s_len = u.shape[0], u.shape[1]
    uf = u.astype(F32).reshape(bsz, s_len, SSM_N_GROUPS, SSM_GROUP)
    lr, li = lam_re.astype(F32), lam_im.astype(F32)
    dt = jnp.exp(log_dt.astype(F32))[:, None]
    mag = jnp.exp(lr * dt)
    ab_re, ab_im = mag * jnp.cos(li * dt), mag * jnp.sin(li * dt)
    nr, ni = ab_re - 1.0, ab_im
    den = lr * lr + li * li
    zr = (nr * lr + ni * li) / den
    zi = (ni * lr - nr * li) / den
    br, bi = b_re.astype(F32), b_im.astype(F32)
    bbr = zr[..., None] * br - zi[..., None] * bi
    bbi = zr[..., None] * bi + zi[..., None] * br
    xr = jnp.einsum('gph,bsgh->bsgp', bbr, uf)
    xi = jnp.einsum('gph,bsgh->bsgp', bbi, uf)
    ar = jnp.broadcast_to(ab_re, xr.shape)
    ai = jnp.broadcast_to(ab_im, xi.shape)

    def combine(e1, e2):
        a1r, a1i, x1r, x1i = e1
        a2r, a2i, x2r, x2i = e2
        return (a2r * a1r - a2i * a1i, a2r * a1i + a2i * a1r,
                a2r * x1r - a2i * x1i + x2r, a2r * x1i + a2i * x1r + x2i)

    _, _, sr, si = lax.associative_scan(combine, (ar, ai, xr, xi), axis=1)
    y = (jnp.einsum('ghp,bsgp->bsgh', c_re.astype(F32), sr)
         - jnp.einsum('ghp,bsgp->bsgh', c_im.astype(F32), si)
         + d.astype(F32) * uf)
    y = jax.nn.gelu(y.reshape(bsz, s_len, SSM_WIDTH))
    ab = jnp.einsum('bsc,ce->bse', y, glu_w.astype(F32))
    out = ab[..., :SSM_WIDTH] * jax.nn.sigmoid(ab[..., SSM_WIDTH:])
    return out.astype(u.dtype)


def conv_glu_ffn(x, w_up, conv_w, conv_b, w_down):
    s_len = x.shape[1]
    h = jnp.einsum('bsd,df->bsf', x, w_up)
    hp = jnp.pad(h, ((0, 0), (CONV_WIDTH - 1, 0), (0, 0)))
    hc = conv_b + hp[:, 0:s_len] * conv_w[0]
    for tap in range(1, CONV_WIDTH):
        hc = hc + hp[:, tap:tap + s_len] * conv_w[tap]
    val, gate = hc[..., :D_FF], hc[..., D_FF:]
    act = (jax.nn.silu(gate.astype(F32)) * val.astype(F32)).astype(x.dtype)
    return jnp.einsum('bsf,fd->bsd', act, w_down)


def _fwd_setup_inputs(seed: int = 0) -> dict:
    key = jax.random.key(seed)
    ks = jax.random.split(key, 32)
    L = DEPTH
    nrm = lambda k, shape, std: jax.random.normal(k, shape, F32) * std
    lam_im_base = math.pi * jnp.arange(SSM_STATE, dtype=F32)
    return {
        'x': nrm(ks[0], (BATCH, SEQ, D_MODEL), 1.0),
        'w_in': nrm(ks[1], (L, D_MODEL, IN_WIDTH), D_MODEL ** -0.5),
        'attn_sinks': nrm(ks[2], (L, N_Q_HEADS), 0.5),
        'pool_w': nrm(ks[3], (L, len(POOL_WINDOWS), POOL_GROUP, POOL_GROUP), POOL_GROUP ** -0.5),
        'pool_scale': 1.0 + nrm(ks[4], (L, POOL_WIDTH), 0.02),
        'ssm_lam_re': -0.5 + nrm(ks[5], (L, SSM_N_GROUPS, SSM_STATE), 0.01),
        'ssm_lam_im': lam_im_base + nrm(ks[6], (L, SSM_N_GROUPS, SSM_STATE), 0.01),
        'ssm_log_dt': jax.random.uniform(ks[7], (L, SSM_N_GROUPS), F32, math.log(1e-3), math.log(1e-1)),
        'ssm_b_re': nrm(ks[8], (L, SSM_N_GROUPS, SSM_STATE, SSM_GROUP), (2 * SSM_GROUP) ** -0.5),
        'ssm_b_im': nrm(ks[9], (L, SSM_N_GROUPS, SSM_STATE, SSM_GROUP), (2 * SSM_GROUP) ** -0.5),
        'ssm_c_re': nrm(ks[10], (L, SSM_N_GROUPS, SSM_GROUP, SSM_STATE), (2 * SSM_STATE) ** -0.5),
        'ssm_c_im': nrm(ks[11], (L, SSM_N_GROUPS, SSM_GROUP, SSM_STATE), (2 * SSM_STATE) ** -0.5),
        'ssm_d': nrm(ks[12], (L, SSM_N_GROUPS, SSM_GROUP), 1.0),
        'ssm_glu_w': nrm(ks[13], (L, SSM_WIDTH, 2 * SSM_WIDTH), SSM_WIDTH ** -0.5),
        'w_out': nrm(ks[14], (L, MIX_WIDTH, D_MODEL), MIX_WIDTH ** -0.5 * DEEPNORM_BETA),
        'ln1_g': 1.0 + nrm(ks[15], (L, D_MODEL), 0.02),
        'ln1_b': nrm(ks[16], (L, D_MODEL), 0.02),
        'ffn_w_up': nrm(ks[17], (L, D_MODEL, 2 * D_FF), D_MODEL ** -0.5),
        'ffn_conv_w': nrm(ks[18], (L, CONV_WIDTH, 2 * D_FF), CONV_WIDTH ** -0.5),
        'ffn_conv_b': nrm(ks[19], (L, 2 * D_FF), 0.01),
        'ffn_w_down': nrm(ks[20], (L, D_FF, D_MODEL), D_FF ** -0.5 * DEEPNORM_BETA),
        'ln2_g': 1.0 + nrm(ks[21], (L, D_MODEL), 0.02),
        'ln2_b': nrm(ks[22], (L, D_MODEL), 0.02),
    }


def _fwd_reference(x, w_in, attn_sinks, pool_w, pool_scale, ssm_lam_re, ssm_lam_im, ssm_log_dt,
              ssm_b_re, ssm_b_im, ssm_c_re, ssm_c_im, ssm_d, ssm_glu_w, w_out, ln1_g, ln1_b,
              ffn_w_up, ffn_conv_w, ffn_conv_b, ffn_w_down, ln2_g, ln2_b):
    bsz, s_len = x.shape[0], x.shape[1]
    pos = jnp.arange(s_len)
    o_k = ATTN_WIDTH
    o_v = o_k + KV_WIDTH
    o_p = o_v + KV_WIDTH
    o_s = o_p + POOL_WIDTH
    for l in range(DEPTH):
        h = jnp.einsum('bsd,de->bse', x, w_in[l])
        q = rope(h[..., :o_k].reshape(bsz, s_len, N_Q_HEADS, HEAD_DIM), pos)
        k = rope(h[..., o_k:o_v].reshape(bsz, s_len, N_KV_HEADS, HEAD_DIM), pos)
        v = h[..., o_v:o_p].reshape(bsz, s_len, N_KV_HEADS, HEAD_DIM)
        y_attn = sliding_window_attention(q, k, v, attn_sinks[l])
        y_pool = multiscale_pool(h[..., o_p:o_s], pool_w[l], pool_scale[l])
        y_ssm = s5_ssm(h[..., o_s:], ssm_lam_re[l], ssm_lam_im[l], ssm_log_dt[l],
                       ssm_b_re[l], ssm_b_im[l], ssm_c_re[l], ssm_c_im[l], ssm_d[l],
                       ssm_glu_w[l])
        mix = jnp.concatenate([y_attn, y_pool, y_ssm], -1)
        mix = jnp.einsum('bse,ed->bsd', mix, w_out[l])
        x = layer_norm(DEEPNORM_ALPHA * x + mix, ln1_g[l], ln1_b[l])
        f = conv_glu_ffn(x, ffn_w_up[l], ffn_conv_w[l], ffn_conv_b[l], ffn_w_down[l])
        x = layer_norm(DEEPNORM_ALPHA * x + f, ln2_g[l], ln2_b[l])
    return x


import jax as _jax
import jax.numpy as _jnp

TWIN_FORMAT = 'train_step'
FWD_PARAMS = ['x', 'w_in', 'attn_sinks', 'pool_w', 'pool_scale', 'ssm_lam_re', 'ssm_lam_im', 'ssm_log_dt', 'ssm_b_re', 'ssm_b_im', 'ssm_c_re', 'ssm_c_im', 'ssm_d', 'ssm_glu_w', 'w_out', 'ln1_g', 'ln1_b', 'ffn_w_up', 'ffn_conv_w', 'ffn_conv_b', 'ffn_w_down', 'ln2_g', 'ln2_b']
TWIN_WEIGHTS = ['w_in', 'attn_sinks', 'pool_w', 'pool_scale', 'ssm_lam_re', 'ssm_lam_im', 'ssm_log_dt', 'ssm_b_re', 'ssm_b_im', 'ssm_c_re', 'ssm_c_im', 'ssm_d', 'ssm_glu_w', 'w_out', 'ln1_g', 'ln1_b', 'ffn_w_up', 'ffn_conv_w', 'ffn_conv_b', 'ffn_w_down', 'ln2_g', 'ln2_b']
TWIN_DIFF_INPUT = 'x'
TWIN_INPUTS = ['x', 'w_in', 'attn_sinks', 'pool_w', 'pool_scale', 'ssm_lam_re', 'ssm_lam_im', 'ssm_log_dt', 'ssm_b_re', 'ssm_b_im', 'ssm_c_re', 'ssm_c_im', 'ssm_d', 'ssm_glu_w', 'w_out', 'ln1_g', 'ln1_b', 'ffn_w_up', 'ffn_conv_w', 'ffn_conv_b', 'ffn_w_down', 'ln2_g', 'ln2_b', 'loss_target', 'm_w_in', 'm_attn_sinks', 'm_pool_w', 'm_pool_scale', 'm_ssm_lam_re', 'm_ssm_lam_im', 'm_ssm_log_dt', 'm_ssm_b_re', 'm_ssm_b_im', 'm_ssm_c_re', 'm_ssm_c_im', 'm_ssm_d', 'm_ssm_glu_w', 'm_w_out', 'm_ln1_g', 'm_ln1_b', 'm_ffn_w_up', 'm_ffn_conv_w', 'm_ffn_conv_b', 'm_ffn_w_down', 'm_ln2_g', 'm_ln2_b', 'v_w_in', 'v_attn_sinks', 'v_pool_w', 'v_pool_scale', 'v_ssm_lam_re', 'v_ssm_lam_im', 'v_ssm_log_dt', 'v_ssm_b_re', 'v_ssm_b_im', 'v_ssm_c_re', 'v_ssm_c_im', 'v_ssm_d', 'v_ssm_glu_w', 'v_w_out', 'v_ln1_g', 'v_ln1_b', 'v_ffn_w_up', 'v_ffn_conv_w', 'v_ffn_conv_b', 'v_ffn_w_down', 'v_ln2_g', 'v_ln2_b']
TWIN_OUTPUTS = ['loss', 'grad_x', 'grad_w_in', 'grad_attn_sinks', 'grad_pool_w', 'grad_pool_scale', 'grad_ssm_lam_re', 'grad_ssm_lam_im', 'grad_ssm_log_dt', 'grad_ssm_b_re', 'grad_ssm_b_im', 'grad_ssm_c_re', 'grad_ssm_c_im', 'grad_ssm_d', 'grad_ssm_glu_w', 'grad_w_out', 'grad_ln1_g', 'grad_ln1_b', 'grad_ffn_w_up', 'grad_ffn_conv_w', 'grad_ffn_conv_b', 'grad_ffn_w_down', 'grad_ln2_g', 'grad_ln2_b', 'delta_w_in', 'delta_attn_sinks', 'delta_pool_w', 'delta_pool_scale', 'delta_ssm_lam_re', 'delta_ssm_lam_im', 'delta_ssm_log_dt', 'delta_ssm_b_re', 'delta_ssm_b_im', 'delta_ssm_c_re', 'delta_ssm_c_im', 'delta_ssm_d', 'delta_ssm_glu_w', 'delta_w_out', 'delta_ln1_g', 'delta_ln1_b', 'delta_ffn_w_up', 'delta_ffn_conv_w', 'delta_ffn_conv_b', 'delta_ffn_w_down', 'delta_ln2_g', 'delta_ln2_b', 'new_m_w_in', 'new_m_attn_sinks', 'new_m_pool_w', 'new_m_pool_scale', 'new_m_ssm_lam_re', 'new_m_ssm_lam_im', 'new_m_ssm_log_dt', 'new_m_ssm_b_re', 'new_m_ssm_b_im', 'new_m_ssm_c_re', 'new_m_ssm_c_im', 'new_m_ssm_d', 'new_m_ssm_glu_w', 'new_m_w_out', 'new_m_ln1_g', 'new_m_ln1_b', 'new_m_ffn_w_up', 'new_m_ffn_conv_w', 'new_m_ffn_conv_b', 'new_m_ffn_w_down', 'new_m_ln2_g', 'new_m_ln2_b', 'new_v_w_in', 'new_v_attn_sinks', 'new_v_pool_w', 'new_v_pool_scale', 'new_v_ssm_lam_re', 'new_v_ssm_lam_im', 'new_v_ssm_log_dt', 'new_v_ssm_b_re', 'new_v_ssm_b_im', 'new_v_ssm_c_re', 'new_v_ssm_c_im', 'new_v_ssm_d', 'new_v_ssm_glu_w', 'new_v_w_out', 'new_v_ln1_g', 'new_v_ln1_b', 'new_v_ffn_w_up', 'new_v_ffn_conv_w', 'new_v_ffn_conv_b', 'new_v_ffn_w_down', 'new_v_ln2_g', 'new_v_ln2_b']
TWIN_LEAF_KINDS = {'loss': 'loss', 'grad_x': 'grad_x', 'grad_w_in': 'grad_w', 'grad_attn_sinks': 'grad_w', 'grad_pool_w': 'grad_w', 'grad_pool_scale': 'grad_w', 'grad_ssm_lam_re': 'grad_w', 'grad_ssm_lam_im': 'grad_w', 'grad_ssm_log_dt': 'grad_w', 'grad_ssm_b_re': 'grad_w', 'grad_ssm_b_im': 'grad_w', 'grad_ssm_c_re': 'grad_w', 'grad_ssm_c_im': 'grad_w', 'grad_ssm_d': 'grad_w', 'grad_ssm_glu_w': 'grad_w', 'grad_w_out': 'grad_w', 'grad_ln1_g': 'grad_w', 'grad_ln1_b': 'grad_w', 'grad_ffn_w_up': 'grad_w', 'grad_ffn_conv_w': 'grad_w', 'grad_ffn_conv_b': 'grad_w', 'grad_ffn_w_down': 'grad_w', 'grad_ln2_g': 'grad_w', 'grad_ln2_b': 'grad_w', 'delta_w_in': 'delta_w', 'delta_attn_sinks': 'delta_w', 'delta_pool_w': 'delta_w', 'delta_pool_scale': 'delta_w', 'delta_ssm_lam_re': 'delta_w', 'delta_ssm_lam_im': 'delta_w', 'delta_ssm_log_dt': 'delta_w', 'delta_ssm_b_re': 'delta_w', 'delta_ssm_b_im': 'delta_w', 'delta_ssm_c_re': 'delta_w', 'delta_ssm_c_im': 'delta_w', 'delta_ssm_d': 'delta_w', 'delta_ssm_glu_w': 'delta_w', 'delta_w_out': 'delta_w', 'delta_ln1_g': 'delta_w', 'delta_ln1_b': 'delta_w', 'delta_ffn_w_up': 'delta_w', 'delta_ffn_conv_w': 'delta_w', 'delta_ffn_conv_b': 'delta_w', 'delta_ffn_w_down': 'delta_w', 'delta_ln2_g': 'delta_w', 'delta_ln2_b': 'delta_w', 'new_m_w_in': 'new_m', 'new_m_attn_sinks': 'new_m', 'new_m_pool_w': 'new_m', 'new_m_pool_scale': 'new_m', 'new_m_ssm_lam_re': 'new_m', 'new_m_ssm_lam_im': 'new_m', 'new_m_ssm_log_dt': 'new_m', 'new_m_ssm_b_re': 'new_m', 'new_m_ssm_b_im': 'new_m', 'new_m_ssm_c_re': 'new_m', 'new_m_ssm_c_im': 'new_m', 'new_m_ssm_d': 'new_m', 'new_m_ssm_glu_w': 'new_m', 'new_m_w_out': 'new_m', 'new_m_ln1_g': 'new_m', 'new_m_ln1_b': 'new_m', 'new_m_ffn_w_up': 'new_m', 'new_m_ffn_conv_w': 'new_m', 'new_m_ffn_conv_b': 'new_m', 'new_m_ffn_w_down': 'new_m', 'new_m_ln2_g': 'new_m', 'new_m_ln2_b': 'new_m', 'new_v_w_in': 'new_v', 'new_v_attn_sinks': 'new_v', 'new_v_pool_w': 'new_v', 'new_v_pool_scale': 'new_v', 'new_v_ssm_lam_re': 'new_v', 'new_v_ssm_lam_im': 'new_v', 'new_v_ssm_log_dt': 'new_v', 'new_v_ssm_b_re': 'new_v', 'new_v_ssm_b_im': 'new_v', 'new_v_ssm_c_re': 'new_v', 'new_v_ssm_c_im': 'new_v', 'new_v_ssm_d': 'new_v', 'new_v_ssm_glu_w': 'new_v', 'new_v_w_out': 'new_v', 'new_v_ln1_g': 'new_v', 'new_v_ln1_b': 'new_v', 'new_v_ffn_w_up': 'new_v', 'new_v_ffn_conv_w': 'new_v', 'new_v_ffn_conv_b': 'new_v', 'new_v_ffn_w_down': 'new_v', 'new_v_ln2_g': 'new_v', 'new_v_ln2_b': 'new_v'}


def _forward(args):
    return _fwd_reference(*[args[k] for k in FWD_PARAMS])


def _output_shape():
    out = _jax.eval_shape(lambda: _forward(_fwd_setup_inputs(0)))
    return out.shape, out.dtype

N_MICROBATCH = 1
ADAM_LR = 0.001
ADAM_B1 = 0.9
ADAM_B2 = 0.999
ADAM_EPS = 1e-08
ADAM_WD = 0.01
ADAM_STEP = 10
PER_EXAMPLE_BATCH_AXIS = {'x': 0, 'loss_target': 0}
SHARED_INPUTS = []
_WEIGHT_DTYPES = {'w_in': _jnp.float32, 'attn_sinks': _jnp.float32, 'pool_w': _jnp.float32, 'pool_scale': _jnp.float32, 'ssm_lam_re': _jnp.float32, 'ssm_lam_im': _jnp.float32, 'ssm_log_dt': _jnp.float32, 'ssm_b_re': _jnp.float32, 'ssm_b_im': _jnp.float32, 'ssm_c_re': _jnp.float32, 'ssm_c_im': _jnp.float32, 'ssm_d': _jnp.float32, 'ssm_glu_w': _jnp.float32, 'w_out': _jnp.float32, 'ln1_g': _jnp.float32, 'ln1_b': _jnp.float32, 'ffn_w_up': _jnp.float32, 'ffn_conv_w': _jnp.float32, 'ffn_conv_b': _jnp.float32, 'ffn_w_down': _jnp.float32, 'ln2_g': _jnp.float32, 'ln2_b': _jnp.float32}
MOMENT_SCALE = {'w_in': 7.544305e-03, 'attn_sinks': 2.646247e-03, 'pool_w': 1.402149e-02, 'pool_scale': 1.480870e-02, 'ssm_lam_re': 2.791293e-04, 'ssm_lam_im': 2.845344e-04, 'ssm_log_dt': 2.324382e-01, 'ssm_b_re': 1.844160e-04, 'ssm_b_im': 1.833934e-04, 'ssm_c_re': 3.684630e-04, 'ssm_c_im': 3.777896e-04, 'ssm_d': 7.978712e-03, 'ssm_glu_w': 5.246263e-03, 'w_out': 1.897375e-02, 'ln1_g': 2.750052e-01, 'ln1_b': 1.491740e-01, 'ffn_w_up': 5.969360e-03, 'ffn_conv_w': 5.976668e-03, 'ffn_conv_b': 6.396818e-03, 'ffn_w_down': 2.291829e-02, 'ln2_g': 4.022920e+00, 'ln2_b': 2.814499e-01}


def _to_microbatches(a, axis):
    t = _jnp.moveaxis(a, axis, 0)
    t = t.reshape((N_MICROBATCH, t.shape[0] // N_MICROBATCH) + t.shape[1:])
    return _jnp.moveaxis(t, 1, axis + 1)


def setup_inputs(seed: int = 0) -> dict:
    inp = _fwd_setup_inputs(seed)
    key = _jax.random.fold_in(_jax.random.key(seed), 7919)
    shape, _ = _output_shape()
    out = dict(inp)
    out["loss_target"] = _jax.random.normal(_jax.random.fold_in(key, 0), shape, _jnp.float32)
    for i, name in enumerate(TWIN_WEIGHTS):
        w = inp[name].astype(_jnp.float32)
        if MOMENT_SCALE is None:
            s = _jnp.sqrt(_jnp.mean(_jnp.square(w)) + 1e-30)
        else:
            s = MOMENT_SCALE[name]
        km, kv = _jax.random.split(_jax.random.fold_in(key, i + 1))
        out[name] = w
        out["m_" + name] = s * _jax.random.normal(km, w.shape, _jnp.float32)
        out["v_" + name] = (s * s) * _jax.random.uniform(kv, w.shape, _jnp.float32, 0.5, 1.5)
    if N_MICROBATCH > 1:
        for name, axis in PER_EXAMPLE_BATCH_AXIS.items():
            out[name] = _to_microbatches(out[name], axis)
    return {'x': out['x'], 'w_in': out['w_in'], 'attn_sinks': out['attn_sinks'], 'pool_w': out['pool_w'], 'pool_scale': out['pool_scale'], 'ssm_lam_re': out['ssm_lam_re'], 'ssm_lam_im': out['ssm_lam_im'], 'ssm_log_dt': out['ssm_log_dt'], 'ssm_b_re': out['ssm_b_re'], 'ssm_b_im': out['ssm_b_im'], 'ssm_c_re': out['ssm_c_re'], 'ssm_c_im': out['ssm_c_im'], 'ssm_d': out['ssm_d'], 'ssm_glu_w': out['ssm_glu_w'], 'w_out': out['w_out'], 'ln1_g': out['ln1_g'], 'ln1_b': out['ln1_b'], 'ffn_w_up': out['ffn_w_up'], 'ffn_conv_w': out['ffn_conv_w'], 'ffn_conv_b': out['ffn_conv_b'], 'ffn_w_down': out['ffn_w_down'], 'ln2_g': out['ln2_g'], 'ln2_b': out['ln2_b'], 'loss_target': out['loss_target'], 'm_w_in': out['m_w_in'], 'm_attn_sinks': out['m_attn_sinks'], 'm_pool_w': out['m_pool_w'], 'm_pool_scale': out['m_pool_scale'], 'm_ssm_lam_re': out['m_ssm_lam_re'], 'm_ssm_lam_im': out['m_ssm_lam_im'], 'm_ssm_log_dt': out['m_ssm_log_dt'], 'm_ssm_b_re': out['m_ssm_b_re'], 'm_ssm_b_im': out['m_ssm_b_im'], 'm_ssm_c_re': out['m_ssm_c_re'], 'm_ssm_c_im': out['m_ssm_c_im'], 'm_ssm_d': out['m_ssm_d'], 'm_ssm_glu_w': out['m_ssm_glu_w'], 'm_w_out': out['m_w_out'], 'm_ln1_g': out['m_ln1_g'], 'm_ln1_b': out['m_ln1_b'], 'm_ffn_w_up': out['m_ffn_w_up'], 'm_ffn_conv_w': out['m_ffn_conv_w'], 'm_ffn_conv_b': out['m_ffn_conv_b'], 'm_ffn_w_down': out['m_ffn_w_down'], 'm_ln2_g': out['m_ln2_g'], 'm_ln2_b': out['m_ln2_b'], 'v_w_in': out['v_w_in'], 'v_attn_sinks': out['v_attn_sinks'], 'v_pool_w': out['v_pool_w'], 'v_pool_scale': out['v_pool_scale'], 'v_ssm_lam_re': out['v_ssm_lam_re'], 'v_ssm_lam_im': out['v_ssm_lam_im'], 'v_ssm_log_dt': out['v_ssm_log_dt'], 'v_ssm_b_re': out['v_ssm_b_re'], 'v_ssm_b_im': out['v_ssm_b_im'], 'v_ssm_c_re': out['v_ssm_c_re'], 'v_ssm_c_im': out['v_ssm_c_im'], 'v_ssm_d': out['v_ssm_d'], 'v_ssm_glu_w': out['v_ssm_glu_w'], 'v_w_out': out['v_w_out'], 'v_ln1_g': out['v_ln1_g'], 'v_ln1_b': out['v_ln1_b'], 'v_ffn_w_up': out['v_ffn_w_up'], 'v_ffn_conv_w': out['v_ffn_conv_w'], 'v_ffn_conv_b': out['v_ffn_conv_b'], 'v_ffn_w_down': out['v_ffn_w_down'], 'v_ln2_g': out['v_ln2_g'], 'v_ln2_b': out['v_ln2_b']}


def _loss(weights, diff, rest, loss_target):
    with _jax.named_scope("forward"):
        args = {**rest, TWIN_DIFF_INPUT: diff, **{k: w.astype(_WEIGHT_DTYPES[k]) for k, w in weights.items()}}
        y = _forward(args)
    with _jax.named_scope("loss_head"):
        err = _jnp.square(y.astype(_jnp.float32) - loss_target)
        return 0.5 * _jnp.sum(_jnp.mean(err, axis=-1)) if err.ndim else 0.5 * err


def _adamw(w, g, m, v):
    m = ADAM_B1 * m + (1.0 - ADAM_B1) * g
    v = ADAM_B2 * v + (1.0 - ADAM_B2) * _jnp.square(g)
    m_hat = m / (1.0 - ADAM_B1 ** ADAM_STEP)
    v_hat = v / (1.0 - ADAM_B2 ** ADAM_STEP)
    delta = -ADAM_LR * (m_hat / (_jnp.sqrt(v_hat) + ADAM_EPS) + ADAM_WD * w)
    return delta, m, v


def reference(x, w_in, attn_sinks, pool_w, pool_scale, ssm_lam_re, ssm_lam_im, ssm_log_dt, ssm_b_re, ssm_b_im, ssm_c_re, ssm_c_im, ssm_d, ssm_glu_w, w_out, ln1_g, ln1_b, ffn_w_up, ffn_conv_w, ffn_conv_b, ffn_w_down, ln2_g, ln2_b, loss_target, m_w_in, m_attn_sinks, m_pool_w, m_pool_scale, m_ssm_lam_re, m_ssm_lam_im, m_ssm_log_dt, m_ssm_b_re, m_ssm_b_im, m_ssm_c_re, m_ssm_c_im, m_ssm_d, m_ssm_glu_w, m_w_out, m_ln1_g, m_ln1_b, m_ffn_w_up, m_ffn_conv_w, m_ffn_conv_b, m_ffn_w_down, m_ln2_g, m_ln2_b, v_w_in, v_attn_sinks, v_pool_w, v_pool_scale, v_ssm_lam_re, v_ssm_lam_im, v_ssm_log_dt, v_ssm_b_re, v_ssm_b_im, v_ssm_c_re, v_ssm_c_im, v_ssm_d, v_ssm_glu_w, v_w_out, v_ln1_g, v_ln1_b, v_ffn_w_up, v_ffn_conv_w, v_ffn_conv_b, v_ffn_w_down, v_ln2_g, v_ln2_b):
    given = dict(x=x, w_in=w_in, attn_sinks=attn_sinks, pool_w=pool_w, pool_scale=pool_scale, ssm_lam_re=ssm_lam_re, ssm_lam_im=ssm_lam_im, ssm_log_dt=ssm_log_dt, ssm_b_re=ssm_b_re, ssm_b_im=ssm_b_im, ssm_c_re=ssm_c_re, ssm_c_im=ssm_c_im, ssm_d=ssm_d, ssm_glu_w=ssm_glu_w, w_out=w_out, ln1_g=ln1_g, ln1_b=ln1_b, ffn_w_up=ffn_w_up, ffn_conv_w=ffn_conv_w, ffn_conv_b=ffn_conv_b, ffn_w_down=ffn_w_down, ln2_g=ln2_g, ln2_b=ln2_b, loss_target=loss_target, m_w_in=m_w_in, m_attn_sinks=m_attn_sinks, m_pool_w=m_pool_w, m_pool_scale=m_pool_scale, m_ssm_lam_re=m_ssm_lam_re, m_ssm_lam_im=m_ssm_lam_im, m_ssm_log_dt=m_ssm_log_dt, m_ssm_b_re=m_ssm_b_re, m_ssm_b_im=m_ssm_b_im, m_ssm_c_re=m_ssm_c_re, m_ssm_c_im=m_ssm_c_im, m_ssm_d=m_ssm_d, m_ssm_glu_w=m_ssm_glu_w, m_w_out=m_w_out, m_ln1_g=m_ln1_g, m_ln1_b=m_ln1_b, m_ffn_w_up=m_ffn_w_up, m_ffn_conv_w=m_ffn_conv_w, m_ffn_conv_b=m_ffn_conv_b, m_ffn_w_down=m_ffn_w_down, m_ln2_g=m_ln2_g, m_ln2_b=m_ln2_b, v_w_in=v_w_in, v_attn_sinks=v_attn_sinks, v_pool_w=v_pool_w, v_pool_scale=v_pool_scale, v_ssm_lam_re=v_ssm_lam_re, v_ssm_lam_im=v_ssm_lam_im, v_ssm_log_dt=v_ssm_log_dt, v_ssm_b_re=v_ssm_b_re, v_ssm_b_im=v_ssm_b_im, v_ssm_c_re=v_ssm_c_re, v_ssm_c_im=v_ssm_c_im, v_ssm_d=v_ssm_d, v_ssm_glu_w=v_ssm_glu_w, v_w_out=v_w_out, v_ln1_g=v_ln1_g, v_ln1_b=v_ln1_b, v_ffn_w_up=v_ffn_w_up, v_ffn_conv_w=v_ffn_conv_w, v_ffn_conv_b=v_ffn_conv_b, v_ffn_w_down=v_ffn_w_down, v_ln2_g=v_ln2_g, v_ln2_b=v_ln2_b)
    weights = {n: given[n] for n in TWIN_WEIGHTS}
    shared = {n: given[n] for n in SHARED_INPUTS}
    per_example = {n: given[n] for n in ['x']}
    grad_fn = _jax.value_and_grad(_loss, argnums=(0, 1))

    def one_microbatch(ex, loss_target):
        ex = dict(ex)
        diff = ex.pop(TWIN_DIFF_INPUT)
        return grad_fn(weights, diff, {**shared, **ex}, loss_target)

    if N_MICROBATCH == 1:
        loss, (grad_w, grad_x) = one_microbatch(per_example, given["loss_target"])
    else:
        def body(carry, xs):
            loss_sum, grad_sum = carry
            l_k, (gw_k, gx_k) = one_microbatch(xs[0], xs[1])
            with _jax.named_scope("update"):
                return (loss_sum + l_k, _jax.tree.map(_jnp.add, grad_sum, gw_k)), gx_k

        init = (_jnp.zeros((), _jnp.float32), _jax.tree.map(_jnp.zeros_like, weights))
        (loss, grad_w), grad_x = _jax.lax.scan(body, init, (per_example, given["loss_target"]))
    with _jax.named_scope("update"):
        delta_w, new_m, new_v = {}, {}, {}
        for n in TWIN_WEIGHTS:
            delta_w[n], new_m[n], new_v[n] = _adamw(weights[n], grad_w[n], given["m_" + n], given["v_" + n])
    return (loss, grad_x, *[grad_w[n] for n in TWIN_WEIGHTS], *[delta_w[n] for n in TWIN_WEIGHTS],
            *[new_m[n] for n in TWIN_WEIGHTS], *[new_v[n] for n in TWIN_WEIGHTS])
```

```python
import functools
import math

import jax
import jax.numpy as jnp
from jax import lax
from jax.experimental import pallas as pl
from jax.experimental.pallas import tpu as pltpu

F32 = jnp.float32
BF16 = jnp.bfloat16
MM_DTYPE = BF16

D_MODEL = 2048
SEQ = 2048
DEPTH = 4
D_FF = 5504
HEAD_DIM = 64
N_Q_HEADS = D_MODEL // 2 // HEAD_DIM
N_KV_HEADS = N_Q_HEADS // 4
ATTN_WIDTH = N_Q_HEADS * HEAD_DIM
KV_WIDTH = N_KV_HEADS * HEAD_DIM
ATTN_BLOCK = 128
ROPE_THETA = 10000.0
POOL_WINDOWS = (2, 4, 8, 16)
POOL_WIDTH = D_MODEL // 4
POOL_GROUP = POOL_WIDTH // len(POOL_WINDOWS)
SSM_WIDTH = D_MODEL // 4
SSM_GROUP = 16
SSM_N_GROUPS = SSM_WIDTH // SSM_GROUP
SSM_STATE = 64
SSM_CH = SSM_N_GROUPS * SSM_STATE
MIX_WIDTH = ATTN_WIDTH + POOL_WIDTH + SSM_WIDTH
IN_WIDTH = ATTN_WIDTH + 2 * KV_WIDTH + POOL_WIDTH + SSM_WIDTH
CONV_WIDTH = 3
LN_EPS = 1e-5
DEEPNORM_ALPHA = (2 * DEPTH) ** 0.25
ADAM_LR = 0.001
ADAM_B1 = 0.9
ADAM_B2 = 0.999
ADAM_EPS = 1e-08
ADAM_WD = 0.01
ADAM_STEP = 10

N_CHIPS = 4
N_DEV = 8
FS = 2 * D_FF // N_CHIPS
IN_S = IN_WIDTH // N_CHIPS
GLU_S = 2 * SSM_WIDTH // N_CHIPS
LANES = 128
SUBLANES = 8
SCAN_CW = 256
SCAN_NB = SSM_CH // SCAN_CW
VMEM_LIMIT = 56 * 1024 * 1024
COPY_BLOCK_BYTES = 6 * 1024 * 1024
NEG = -1e30

NN = (((1,), (0,)), ((), ()))
NT = (((1,), (1,)), ((), ()))
TN = (((0,), (0,)), ((), ()))
MESH = pl.DeviceIdType.MESH


def _tile(n, pref, mult=LANES):
    best = None
    for t in range(mult, min(n, pref) + 1, mult):
        if n % t == 0:
            best = t
    return n if best is None else best


def _cparams(sem):
    return pltpu.CompilerParams(dimension_semantics=sem, vmem_limit_bytes=VMEM_LIMIT)


def _sds(shape, dtype=F32):
    return jax.ShapeDtypeStruct(tuple(shape), dtype)


def _mm(name, a, b, out_shape, grid, a_spec, b_spec, o_spec, dims, acc_shape, out_dtype=F32):
    nk = grid[2]

    def body(a_ref, b_ref, o_ref, acc_ref):
        k = pl.program_id(2)

        @pl.when(k == 0)
        def _():
            acc_ref[...] = jnp.zeros_like(acc_ref)

        acc_ref[...] += lax.dot_general(a_ref[...].astype(MM_DTYPE), b_ref[...].astype(MM_DTYPE), dims,
                                        preferred_element_type=F32)

        @pl.when(k == nk - 1)
        def _():
            o_ref[...] = acc_ref[...].astype(o_ref.dtype)

    return pl.pallas_call(
        body, name=name, grid=grid, in_specs=[a_spec, b_spec], out_specs=o_spec,
        out_shape=_sds(out_shape, out_dtype), scratch_shapes=[pltpu.VMEM(acc_shape, F32)],
        compiler_params=_cparams(("parallel", "parallel", "arbitrary")))(a, b)


def _mm_nn(name, a, b, tm=512, tn=512, tk=512, out_dtype=F32):
    m, kk = a.shape
    n = b.shape[1]
    tm, tn, tk = _tile(m, tm), _tile(n, tn), _tile(kk, tk)
    return _mm(name, a, b, (m, n), (m // tm, n // tn, kk // tk),
               pl.BlockSpec((tm, tk), lambda i, j, k: (i, k)), pl.BlockSpec((tk, tn), lambda i, j, k: (k, j)),
               pl.BlockSpec((tm, tn), lambda i, j, k: (i, j)), NN, (tm, tn), out_dtype)


def _mm_nt(name, a, b, tm=512, tn=512, tk=512):
    m, kk = a.shape
    n = b.shape[0]
    tm, tn, tk = _tile(m, tm), _tile(n, tn), _tile(kk, tk)
    return _mm(name, a, b, (m, n), (m // tm, n // tn, kk // tk),
               pl.BlockSpec((tm, tk), lambda i, j, k: (i, k)), pl.BlockSpec((tn, tk), lambda i, j, k: (j, k)),
               pl.BlockSpec((tm, tn), lambda i, j, k: (i, j)), NT, (tm, tn))


def _mm_tn(name, a, b, tm=512, tn=512, ts=512):
    s, m = a.shape
    n = b.shape[1]
    tm, tn, ts = _tile(m, tm), _tile(n, tn), _tile(s, ts)
    return _mm(name, a, b, (m, n), (m // tm, n // tn, s // ts),
               pl.BlockSpec((ts, tm), lambda i, j, k: (k, i)), pl.BlockSpec((ts, tn), lambda i, j, k: (k, j)),
               pl.BlockSpec((tm, tn), lambda i, j, k: (i, j)), TN, (tm, tn))


def _mm_shard_cols(name, a, w, tm=512, tk=512):
    m, kk = a.shape
    nj, _, c = w.shape
    tm, tk = _tile(m, tm), _tile(kk, tk)
    return _mm(name, a, w, (m, nj * c), (m // tm, nj, kk // tk),
               pl.BlockSpec((tm, tk), lambda i, j, k: (i, k)), pl.BlockSpec((None, tk, c), lambda i, j, k: (j, k, 0)),
               pl.BlockSpec((tm, c), lambda i, j, k: (i, j)), NN, (tm, c))


def _mm_shard_cols_nt(name, d, w, tm=512, tn=512):
    m = d.shape[0]
    nj, n, c = w.shape
    tm, tn = _tile(m, tm), _tile(n, tn)
    return _mm(name, d, w, (m, n), (m // tm, n // tn, nj),
               pl.BlockSpec((tm, c), lambda i, j, k: (i, k)), pl.BlockSpec((None, tn, c), lambda i, j, k: (k, j, 0)),
               pl.BlockSpec((tm, tn), lambda i, j, k: (i, j)), NT, (tm, tn))


def _mm_shard_cols_tn(name, a, d, nj, tm=512, ts=512):
    s, m = a.shape
    c = d.shape[1] // nj
    tm, ts = _tile(m, tm), _tile(s, ts)
    return _mm(name, a, d, (nj, m, c), (nj, m // tm, s // ts),
               pl.BlockSpec((ts, tm), lambda j, i, k: (k, i)), pl.BlockSpec((ts, c), lambda j, i, k: (k, j)),
               pl.BlockSpec((None, tm, c), lambda j, i, k: (j, i, 0)), TN, (tm, c))


def _ffn_up(x1, w_up, tm=256, tk=512):
    s, d = x1.shape
    tm, tk = _tile(s, tm), _tile(d, tk)
    return _mm("ffn_up", x1, w_up, (N_CHIPS, s, FS), (s // tm, N_CHIPS, d // tk),
               pl.BlockSpec((tm, tk), lambda i, j, k: (i, k)), pl.BlockSpec((None, tk, FS), lambda i, j, k: (j, k, 0)),
               pl.BlockSpec((None, tm, FS), lambda i, j, k: (j, i, 0)), NN, (tm, FS))


def _ffn_down(act, w_down, tm=512, tn=512):
    _, s, _ = act.shape
    d = w_down.shape[2]
    tm, tn = _tile(s, tm), _tile(d, tn)
    return _mm("ffn_down", act, w_down, (s, d), (s // tm, d // tn, 2),
               pl.BlockSpec((None, tm, FS), lambda i, j, k: (k, i, 0)), pl.BlockSpec((None, FS, tn), lambda i, j, k: (k, 0, j)),
               pl.BlockSpec((tm, tn), lambda i, j, k: (i, j)), NN, (tm, tn))


def _ffn_down_dact(df, w_down, tm=256, tk=512):
    s, d = df.shape
    tm, tk = _tile(s, tm), _tile(d, tk)
    return _mm("ffn_down_dact", df, w_down, (2, s, FS), (s // tm, 2, d // tk),
               pl.BlockSpec((tm, tk), lambda i, j, k: (i, k)), pl.BlockSpec((None, FS, tk), lambda i, j, k: (j, 0, k)),
               pl.BlockSpec((None, tm, FS), lambda i, j, k: (j, i, 0)), NT, (tm, FS))


def _ffn_down_dw(act, df, tn=512, ts=512):
    _, s, _ = act.shape
    d = df.shape[1]
    tn, ts = _tile(d, tn), _tile(s, ts)
    return _mm("ffn_down_dw", act, df, (2, FS, d), (2, d // tn, s // ts),
               pl.BlockSpec((None, ts, FS), lambda p, j, k: (p, k, 0)), pl.BlockSpec((ts, tn), lambda p, j, k: (k, j)),
               pl.BlockSpec((None, FS, tn), lambda p, j, k: (p, 0, j)), TN, (FS, tn))


def _ffn_up_dx(dh, w_up, tm=512, tn=512):
    _, s, _ = dh.shape
    d = w_up.shape[1]
    tm, tn = _tile(s, tm), _tile(d, tn)
    return _mm("ffn_up_dx", dh, w_up, (s, d), (s // tm, d // tn, N_CHIPS),
               pl.BlockSpec((None, tm, FS), lambda i, j, k: (k, i, 0)), pl.BlockSpec((None, tn, FS), lambda i, j, k: (k, j, 0)),
               pl.BlockSpec((tm, tn), lambda i, j, k: (i, j)), NT, (tm, tn))


def _ffn_up_dw(x1, dh, tm=512, ts=512):
    s, d = x1.shape
    tm, ts = _tile(d, tm), _tile(s, ts)
    return _mm("ffn_up_dw", x1, dh, (N_CHIPS, d, FS), (N_CHIPS, d // tm, s // ts),
               pl.BlockSpec((ts, tm), lambda j, i, k: (k, i)), pl.BlockSpec((None, ts, FS), lambda j, i, k: (j, k, 0)),
               pl.BlockSpec((None, tm, FS), lambda j, i, k: (j, i, 0)), TN, (tm, FS))


def _rope_tables():
    half = HEAD_DIM // 2
    inv = ROPE_THETA ** (-jnp.arange(half, dtype=F32) / half)
    ang = jnp.arange(SEQ).astype(F32)[:, None] * inv[None, :]
    cos, sin = jnp.cos(ang), jnp.sin(ang)
    cos_t = jnp.tile(cos, (1, LANES // half))
    sin_t = jnp.tile(jnp.concatenate([-sin, sin], axis=1), (1, LANES // HEAD_DIM))
    return cos_t, sin_t


def _rotate_half(t):
    lane = lax.broadcasted_iota(jnp.int32, t.shape, 1)
    first = (lane % HEAD_DIM) < (HEAD_DIM // 2)
    return jnp.where(first, pltpu.roll(t, LANES - HEAD_DIM // 2, 1), pltpu.roll(t, HEAD_DIM // 2, 1))


def _rope(name, src, col_tile0, n_tiles, cos_t, sin_t, out_dtype):
    tm = _tile(SEQ, 512)

    def body(x_ref, c_ref, s_ref, o_ref):
        t = x_ref[...].astype(F32)
        o_ref[...] = (t * c_ref[...] + _rotate_half(t) * s_ref[...]).astype(o_ref.dtype)

    return pl.pallas_call(
        body, name=name, grid=(SEQ // tm, n_tiles),
        in_specs=[pl.BlockSpec((tm, LANES), lambda i, j: (i, j + col_tile0)),
                  pl.BlockSpec((tm, LANES), lambda i, j: (i, 0)), pl.BlockSpec((tm, LANES), lambda i, j: (i, 0))],
        out_specs=pl.BlockSpec((tm, LANES), lambda i, j: (i, j)),
        out_shape=_sds((SEQ, n_tiles * LANES), out_dtype),
        compiler_params=_cparams(("parallel", "parallel")))(src, cos_t, sin_t)


Q_TILES = ATTN_WIDTH // LANES
KV_TILES = KV_WIDTH // LANES
Q_PER_KV_TILE = Q_TILES // KV_TILES
HEADS_PER_KV_TILE = N_Q_HEADS // KV_TILES
K_TILE0 = ATTN_WIDTH // LANES
V_TILE0 = (ATTN_WIDTH + KV_WIDTH) // LANES
N_QBLK = SEQ // ATTN_BLOCK


def _dup_half(t, which):
    lane = lax.broadcasted_iota(jnp.int32, t.shape, 1)
    r = pltpu.roll(t, HEAD_DIM, 1)
    lo = lane < HEAD_DIM
    return jnp.where(lo, t, r) if which == 0 else jnp.where(lo, r, t)


def _attn_masks(n):
    row = lax.broadcasted_iota(jnp.int32, (ATTN_BLOCK, ATTN_BLOCK), 0)
    col = lax.broadcasted_iota(jnp.int32, (ATTN_BLOCK, ATTN_BLOCK), 1)
    return col <= row, (col > row) & (n > 0), col < HEAD_DIM


def _attn_probs(qm, k2c, k2p, cur_ok, prev_ok, sink):
    scale = HEAD_DIM ** -0.5
    sc = lax.dot_general(qm, k2c, NT, preferred_element_type=F32) * scale
    sp = lax.dot_general(qm, k2p, NT, preferred_element_type=F32) * scale
    sc = jnp.where(cur_ok, sc, NEG)
    sp = jnp.where(prev_ok, sp, NEG)
    m = jnp.maximum(jnp.maximum(sc.max(1, keepdims=True), sp.max(1, keepdims=True)), sink)
    pc, pp = jnp.exp(sc - m), jnp.exp(sp - m)
    esink = jnp.exp(sink - m)
    inv = 1.0 / (pc.sum(1, keepdims=True) + pp.sum(1, keepdims=True) + esink)
    return pc * inv, pp * inv, esink * inv


def _attn_specs():
    blk = (ATTN_BLOCK, LANES)
    wide = (ATTN_BLOCK, Q_PER_KV_TILE * LANES)
    prev = lambda n: jnp.maximum(n - 1, 0)
    q_spec = pl.BlockSpec(wide, lambda t, n: (n, t))
    kc = pl.BlockSpec(blk, lambda t, n: (n, K_TILE0 + t))
    kp = pl.BlockSpec(blk, lambda t, n: (prev(n), K_TILE0 + t))
    vc = pl.BlockSpec(blk, lambda t, n: (n, V_TILE0 + t))
    vp = pl.BlockSpec(blk, lambda t, n: (prev(n), V_TILE0 + t))
    return q_spec, kc, kp, vc, vp, pl.BlockSpec(memory_space=pltpu.SMEM)


def _attn_fwd(qk, h, sinks):
    q_spec, kc_s, kp_s, vc_s, vp_s, smem = _attn_specs()

    def body(sink_ref, q_ref, kc_ref, kp_ref, vc_ref, vp_ref, o_ref):
        t, n = pl.program_id(0), pl.program_id(1)
        cur_ok, prev_ok, lo = _attn_masks(n)
        kc, kp = kc_ref[...].astype(F32), kp_ref[...].astype(F32)
        vc, vp = vc_ref[...], vp_ref[...]
        for kvl in range(2):
            k2c, k2p = _dup_half(kc, kvl).astype(MM_DTYPE), _dup_half(kp, kvl).astype(MM_DTYPE)
            v2c, v2p = _dup_half(vc, kvl).astype(MM_DTYPE), _dup_half(vp, kvl).astype(MM_DTYPE)
            for a in (2 * kvl, 2 * kvl + 1):
                qt = q_ref[:, a * LANES:(a + 1) * LANES].astype(F32)
                outs = []
                for hs in range(2):
                    qm = jnp.where(lo == (hs == 0), qt, 0.0).astype(MM_DTYPE)
                    sink = sink_ref[t * HEADS_PER_KV_TILE + 2 * a + hs]
                    pc, pp, _ = _attn_probs(qm, k2c, k2p, cur_ok, prev_ok, sink)
                    outs.append(lax.dot_general(pc.astype(MM_DTYPE), v2c, NN, preferred_element_type=F32)
                                + lax.dot_general(pp.astype(MM_DTYPE), v2p, NN, preferred_element_type=F32))
                o_ref[:, a * LANES:(a + 1) * LANES] = jnp.where(lo, outs[0], outs[1])

    return pl.pallas_call(
        body, name="attn_fwd", grid=(KV_TILES, N_QBLK),
        in_specs=[smem, q_spec, kc_s, kp_s, vc_s, vp_s], out_specs=q_spec,
        out_shape=_sds((SEQ, ATTN_WIDTH)), compiler_params=_cparams(("parallel", "parallel")))(sinks, qk, qk, qk, h, h)


def _attn_bwd(qk, h, sinks, y, dy, dy_tile0):
    q_spec, kc_s, kp_s, vc_s, vp_s, smem = _attn_specs()
    blk = (ATTN_BLOCK, LANES)
    wide = (ATTN_BLOCK, Q_PER_KV_TILE * LANES)
    kv_out = pl.BlockSpec(blk, lambda t, n: (n, t))
    dy_spec = pl.BlockSpec(wide, lambda t, n: (n, t + dy_tile0))

    def body(sink_ref, q_ref, kc_ref, kp_ref, vc_ref, vp_ref, y_ref, dy_ref,
             dq_ref, dkc_ref, dkp_ref, dvc_ref, dvp_ref, dsk_ref):
        t, n = pl.program_id(0), pl.program_id(1)
        cur_ok, prev_ok, lo = _attn_masks(n)
        scale = HEAD_DIM ** -0.5
        kc, kp = kc_ref[...].astype(F32), kp_ref[...].astype(F32)
        vc, vp = vc_ref[...], vp_ref[...]
        hrow = lax.broadcasted_iota(jnp.int32, (HEADS_PER_KV_TILE, LANES), 0)
        dsk = jnp.zeros((HEADS_PER_KV_TILE, LANES), F32)
        folded = []
        for kvl in range(2):
            k2c, k2p = _dup_half(kc, kvl).astype(MM_DTYPE), _dup_half(kp, kvl).astype(MM_DTYPE)
            v2c, v2p = _dup_half(vc, kvl).astype(MM_DTYPE), _dup_half(vp, kvl).astype(MM_DTYPE)
            acc = [jnp.zeros(blk, F32) for _ in range(4)]
            for a in (2 * kvl, 2 * kvl + 1):
                sl = slice(a * LANES, (a + 1) * LANES)
                qt = q_ref[:, sl].astype(F32)
                dot_, yt = dy_ref[:, sl], y_ref[:, sl]
                dqs = []
                for hs in range(2):
                    hm = lo == (hs == 0)
                    qm = jnp.where(hm, qt, 0.0).astype(MM_DTYPE)
                    dom = jnp.where(hm, dot_, 0.0).astype(MM_DTYPE)
                    hl = 2 * a + hs
                    sink = sink_ref[t * HEADS_PER_KV_TILE + hl]
                    pc, pp, psink = _attn_probs(qm, k2c, k2p, cur_ok, prev_ok, sink)
                    delta = jnp.sum(jnp.where(hm, dot_ * yt, 0.0), axis=1, keepdims=True)
                    dpc = lax.dot_general(dom, v2c, NT, preferred_element_type=F32)
                    dpp = lax.dot_general(dom, v2p, NT, preferred_element_type=F32)
                    dsc = (pc * (dpc - delta) * scale).astype(MM_DTYPE)
                    dsp = (pp * (dpp - delta) * scale).astype(MM_DTYPE)
                    dqs.append(lax.dot_general(dsc, k2c, NN, preferred_element_type=F32)
                               + lax.dot_general(dsp, k2p, NN, preferred_element_type=F32))
                    acc[0] += lax.dot_general(dsc, qm, TN, preferred_element_type=F32)
                    acc[1] += lax.dot_general(dsp, qm, TN, preferred_element_type=F32)
                    acc[2] += lax.dot_general(pc.astype(MM_DTYPE), dom, TN, preferred_element_type=F32)
                    acc[3] += lax.dot_general(pp.astype(MM_DTYPE), dom, TN, preferred_element_type=F32)
                    dsk = dsk + jnp.where(hrow == hl, -jnp.sum(psink * delta), 0.0)
                dq_ref[:, sl] = jnp.where(lo, dqs[0], dqs[1])
            folded.append([x + pltpu.roll(x, HEAD_DIM, 1) for x in acc])
        for o_ref, i in ((dkc_ref, 0), (dkp_ref, 1), (dvc_ref, 2), (dvp_ref, 3)):
            o_ref[...] = jnp.where(lo, folded[0][i], folded[1][i])

        @pl.when(n == 0)
        def _():
            dsk_ref[...] = jnp.zeros_like(dsk_ref)

        dsk_ref[...] += dsk

    kv_shape = _sds((SEQ, KV_WIDTH))
    return pl.pallas_call(
        body, name="attn_bwd", grid=(KV_TILES, N_QBLK),
        in_specs=[smem, q_spec, kc_s, kp_s, vc_s, vp_s, q_spec, dy_spec],
        out_specs=[q_spec, kv_out, kv_out, kv_out, kv_out,
                   pl.BlockSpec((None, HEADS_PER_KV_TILE, LANES), lambda t, n: (t, 0, 0))],
        out_shape=[_sds((SEQ, ATTN_WIDTH)), kv_shape, kv_shape, kv_shape, kv_shape,
                   _sds((KV_TILES, HEADS_PER_KV_TILE, LANES))],
        compiler_params=_cparams(("parallel", "arbitrary")))(sinks, qk, qk, qk, h, h, y, dy)


def _attn_dh(dq, dkc, dkp, dvc, dvp, cos_t, nsin_t):
    n_tiles = Q_TILES + 2 * KV_TILES
    blk = (ATTN_BLOCK, LANES)
    nxt = lambda n: jnp.minimum(n + 1, N_QBLK - 1)

    def body(dq_ref, kc_ref, kp_ref, vc_ref, vp_ref, c_ref, s_ref, o_ref):
        n, j = pl.program_id(0), pl.program_id(1)
        has_next = n < N_QBLK - 1

        @pl.when(j < Q_TILES)
        def _():
            t = dq_ref[...]
            o_ref[...] = t * c_ref[...] + _rotate_half(t) * s_ref[...]

        @pl.when((j >= Q_TILES) & (j < Q_TILES + KV_TILES))
        def _():
            t = kc_ref[...] + jnp.where(has_next, kp_ref[...], 0.0)
            o_ref[...] = t * c_ref[...] + _rotate_half(t) * s_ref[...]

        @pl.when(j >= Q_TILES + KV_TILES)
        def _():
            o_ref[...] = vc_ref[...] + jnp.where(has_next, vp_ref[...], 0.0)

    qj = lambda j: jnp.minimum(j, Q_TILES - 1)
    kj = lambda j: jnp.clip(j - Q_TILES, 0, KV_TILES - 1)
    vj = lambda j: jnp.clip(j - Q_TILES - KV_TILES, 0, KV_TILES - 1)
    return pl.pallas_call(
        body, name="attn_dh", grid=(N_QBLK, n_tiles),
        in_specs=[pl.BlockSpec(blk, lambda n, j: (n, qj(j))),
                  pl.BlockSpec(blk, lambda n, j: (n, kj(j))), pl.BlockSpec(blk, lambda n, j: (nxt(n), kj(j))),
                  pl.BlockSpec(blk, lambda n, j: (n, vj(j))), pl.BlockSpec(blk, lambda n, j: (nxt(n), vj(j))),
                  pl.BlockSpec(blk, lambda n, j: (n, 0)), pl.BlockSpec(blk, lambda n, j: (n, 0))],
        out_specs=pl.BlockSpec(blk, lambda n, j: (n, j)),
        out_shape=_sds((SEQ, n_tiles * LANES)),
        compiler_params=_cparams(("parallel", "parallel")))(dq, dkc, dkp, dvc, dvp, cos_t, nsin_t)


POOL_TILE0 = (ATTN_WIDTH + 2 * KV_WIDTH) // POOL_WIDTH


def _shift_rows(x, d, down):
    n = x.shape[0]
    row = lax.broadcasted_iota(jnp.int32, x.shape, 0)
    if down:
        return jnp.where(row >= d, pltpu.roll(x, d, 0), 0.0)
    return jnp.where(row < n - d, pltpu.roll(x, n - d, 0), 0.0)


def _window_sum(x, w, down):
    d = 1
    while d < w:
        x = x + _shift_rows(x, d, down)
        d *= 2
    return x


def _pool_z(u, w):
    t = lax.broadcasted_iota(jnp.int32, u.shape, 0).astype(F32)
    cnt = jnp.minimum(t + 1.0, float(w))
    return _window_sum(u, w, True) / cnt - u, cnt


def _pool_fwd(h, pool_w, pool_scale):
    def body(u_ref, w_ref, s_ref, o_ref):
        for gi, w in enumerate(POOL_WINDOWS):
            sl = slice(gi * POOL_GROUP, (gi + 1) * POOL_GROUP)
            z, _ = _pool_z(u_ref[:, sl], w)
            o_ref[:, sl] = lax.dot_general(z.astype(MM_DTYPE), w_ref[gi].astype(MM_DTYPE), NN,
                                           preferred_element_type=F32) * s_ref[:, sl]

    return pl.pallas_call(
        body, name="pool_fwd", grid=(1,),
        in_specs=[pl.BlockSpec((SEQ, POOL_WIDTH), lambda i: (0, POOL_TILE0)),
                  pl.BlockSpec(pool_w.shape, lambda i: (0, 0, 0)), pl.BlockSpec((1, POOL_WIDTH), lambda i: (0, 0))],
        out_specs=pl.BlockSpec((SEQ, POOL_WIDTH), lambda i: (0, 0)),
        out_shape=_sds((SEQ, POOL_WIDTH)), compiler_params=_cparams(("arbitrary",)))(h, pool_w, pool_scale)


def _pool_bwd(h, pool_w, pool_scale, dmix, dy_tile0):
    def body(u_ref, w_ref, s_ref, dy_ref, du_ref, dw_ref, ds_ref):
        for gi, w in enumerate(POOL_WINDOWS):
            sl = slice(gi * POOL_GROUP, (gi + 1) * POOL_GROUP)
            z, cnt = _pool_z(u_ref[:, sl], w)
            zb, wb = z.astype(MM_DTYPE), w_ref[gi].astype(MM_DTYPE)
            dy = dy_ref[:, sl]
            zp = lax.dot_general(zb, wb, NN, preferred_element_type=F32)
            ds_ref[:, sl] = jnp.sum(dy * zp, axis=0, keepdims=True)
            dyo = (dy * s_ref[:, sl]).astype(MM_DTYPE)
            dw_ref[gi] = lax.dot_general(zb, dyo, TN, preferred_element_type=F32)
            dz = lax.dot_general(dyo, wb, NT, preferred_element_type=F32)
            du_ref[:, sl] = _window_sum(dz / cnt, w, False) - dz

    return pl.pallas_call(
        body, name="pool_bwd", grid=(1,),
        in_specs=[pl.BlockSpec((SEQ, POOL_WIDTH), lambda i: (0, POOL_TILE0)),
                  pl.BlockSpec(pool_w.shape, lambda i: (0, 0, 0)), pl.BlockSpec((1, POOL_WIDTH), lambda i: (0, 0)),
                  pl.BlockSpec((SEQ, POOL_WIDTH), lambda i: (0, dy_tile0))],
        out_specs=[pl.BlockSpec((SEQ, POOL_WIDTH), lambda i: (0, 0)), pl.BlockSpec(pool_w.shape, lambda i: (0, 0, 0)),
                   pl.BlockSpec((1, POOL_WIDTH), lambda i: (0, 0))],
        out_shape=[_sds((SEQ, POOL_WIDTH)), _sds(pool_w.shape), _sds((1, POOL_WIDTH))],
        compiler_params=_cparams(("arbitrary",)))(h, pool_w, pool_scale, dmix)


def _ssm_discretize(lr, li, ldt, br, bi):
    dt = jnp.exp(ldt)
    mag = jnp.exp(lr * dt)
    ar, ai = mag * jnp.cos(li * dt), mag * jnp.sin(li * dt)
    nr, ni = ar - 1.0, ai
    den = lr * lr + li * li
    zr = (nr * lr + ni * li) / den
    zi = (ni * lr - nr * li) / den
    return ar, ai, zr * br - zi * bi, zr * bi + zi * br


def _ssm_prep(lr, li, ldt, br, bi):
    def body(lr_ref, li_ref, ldt_ref, br_ref, bi_ref, ar_ref, ai_ref, bbr_ref, bbi_ref):
        outs = _ssm_discretize(lr_ref[...], li_ref[...], ldt_ref[...], br_ref[...], bi_ref[...])
        for o, v in zip((ar_ref, ai_ref, bbr_ref, bbi_ref), outs):
            o[...] = v

    row, mat = _sds((1, SSM_CH)), _sds((SSM_GROUP, SSM_CH))
    return pl.pallas_call(body, name="ssm_prep", out_shape=[row, row, mat, mat])(lr, li, ldt, br, bi)


def _ssm_prep_bwd(lr, li, ldt, br, bi, dar8, dai8, dbbr, dbbi):
    def body(lr_ref, li_ref, ldt_ref, br_ref, bi_ref, dar_ref, dai_ref, dbbr_ref, dbbi_ref, *outs):
        args = (lr_ref[...], li_ref[...], ldt_ref[...], br_ref[...], bi_ref[...])
        _, vjp = jax.vjp(_ssm_discretize, *args)
        cot = (jnp.sum(dar_ref[...], axis=0, keepdims=True), jnp.sum(dai_ref[...], axis=0, keepdims=True),
               dbbr_ref[...], dbbi_ref[...])
        for o, v in zip(outs, vjp(cot)):
            o[...] = v

    row, mat = _sds((1, SSM_CH)), _sds((SSM_GROUP, SSM_CH))
    return pl.pallas_call(body, name="ssm_prep_bwd", out_shape=[row, row, row, mat, mat])(
        lr, li, ldt, br, bi, dar8, dai8, dbbr, dbbi)


def _scan_layout(re, im):
    r = re.shape[0]
    return jnp.stack([re.reshape(r, SCAN_NB, SCAN_CW), im.reshape(r, SCAN_NB, SCAN_CW)], axis=2).reshape(r, 2 * SSM_CH)


def _scan_unlayout(x):
    r = x.shape[0]
    x = x.reshape(r, SCAN_NB, 2, SCAN_CW)
    return x[:, :, 0].reshape(r, SSM_CH), x[:, :, 1].reshape(r, SSM_CH)


def _time_permute(u):
    s, c = u.shape
    return u.reshape(SUBLANES, s // SUBLANES, c).transpose(1, 0, 2).reshape(s, c)


def _time_unpermute(u):
    s, c = u.shape
    return u.reshape(s // SUBLANES, SUBLANES, c).transpose(1, 0, 2).reshape(s, c)


def _ssm_scan(name, a_vec, x, reverse, s_prev=None):
    nsteps = SEQ // SUBLANES
    cw = SCAN_CW
    with_da = s_prev is not None

    def body(a_ref, x_ref, *rest):
        if with_da:
            s_ref, o_ref, da_ref = rest
        else:
            o_ref, = rest
        ar = jnp.broadcast_to(a_ref[:, :cw], (SUBLANES, cw))
        ai = jnp.broadcast_to(a_ref[:, cw:], (SUBLANES, cw))
        seg = lax.broadcasted_iota(jnp.int32, (SUBLANES, cw), 0)

        def toward(v):
            if reverse:
                return jnp.where(seg < SUBLANES - 1, pltpu.roll(v, SUBLANES - 1, 0), 0.0)
            return jnp.where(seg >= 1, pltpu.roll(v, 1, 0), 0.0)

        def rows(j):
            jj = nsteps - 1 - j if reverse else j
            return pl.ds(pl.multiple_of(jj * SUBLANES, SUBLANES), SUBLANES)

        def cmul(pr, pi, qr, qi):
            return pr * qr - pi * qi, pr * qi + pi * qr

        def local(j, c):
            sr, si = c
            r = rows(j)
            mr, mi = cmul(ar, ai, sr, si)
            return mr + x_ref[r, :cw], mi + x_ref[r, cw:]

        zero = jnp.zeros((SUBLANES, cw), F32)
        fr, fi = lax.fori_loop(0, nsteps, local, (zero, zero))

        def power(_, c):
            return cmul(ar, ai, *c)

        pr, pi = lax.fori_loop(0, nsteps - 1, power, (ar, ai))
        tr, ti = fr, fi
        for _ in range(SUBLANES - 1):
            mr, mi = cmul(pr, pi, toward(tr), toward(ti))
            tr, ti = fr + mr, fi + mi
        init = (toward(tr), toward(ti))

        def full(j, c):
            r = rows(j)
            if with_da:
                sr, si, dar, dai = c
            else:
                sr, si = c
            mr, mi = cmul(ar, ai, sr, si)
            sr, si = mr + x_ref[r, :cw], mi + x_ref[r, cw:]
            o_ref[r, :cw] = sr
            o_ref[r, cw:] = si
            if not with_da:
                return sr, si
            jj = nsteps - 1 - j
            rp = pl.ds(pl.multiple_of(jnp.maximum(jj - 1, 0) * SUBLANES, SUBLANES), SUBLANES)
            last = pl.ds((nsteps - 1) * SUBLANES, SUBLANES)
            first = jj == 0
            spr = jnp.where(first, jnp.where(seg >= 1, pltpu.roll(s_ref[last, :cw], 1, 0), 0.0), s_ref[rp, :cw])
            spi = jnp.where(first, jnp.where(seg >= 1, pltpu.roll(s_ref[last, cw:], 1, 0), 0.0), s_ref[rp, cw:])
            return sr, si, dar + sr * spr + si * spi, dai + si * spr - sr * spi

        if with_da:
            _, _, dar, dai = lax.fori_loop(0, nsteps, full, init + (zero, zero))
            da_ref[:, :cw] = dar
            da_ref[:, cw:] = dai
        else:
            lax.fori_loop(0, nsteps, full, init)

    blk = pl.BlockSpec((SEQ, 2 * cw), lambda b: (0, b))
    a_spec = pl.BlockSpec((1, 2 * cw), lambda b: (0, b))
    in_specs, args = [a_spec, blk], [a_vec, x]
    out_specs, out_shape = blk, _sds((SEQ, 2 * SSM_CH))
    if with_da:
        in_specs, args = in_specs + [blk], args + [s_prev]
        out_specs = [blk, pl.BlockSpec((SUBLANES, 2 * cw), lambda b: (0, b))]
        out_shape = [out_shape, _sds((SUBLANES, 2 * SSM_CH))]
    return pl.pallas_call(body, name=name, grid=(SCAN_NB,), in_specs=in_specs, out_specs=out_specs,
                          out_shape=out_shape, compiler_params=_cparams(("parallel",)))(*args)


def _ssm_gelu(yp, up, dvec):
    tm = _tile(SEQ, 512)

    def body(y_ref, u_ref, d_ref, yf_ref, g_ref):
        yf = y_ref[...] + d_ref[...] * u_ref[...]
        yf_ref[...] = yf
        g_ref[...] = jax.nn.gelu(yf)

    blk = pl.BlockSpec((tm, SSM_WIDTH), lambda i: (i, 0))
    row = pl.BlockSpec((1, SSM_WIDTH), lambda i: (0, 0))
    return pl.pallas_call(body, name="ssm_gelu", grid=(SEQ // tm,), in_specs=[blk, blk, row], out_specs=[blk, blk],
                          out_shape=[_sds((SEQ, SSM_WIDTH))] * 2, compiler_params=_cparams(("parallel",)))(yp, up, dvec)


def _ssm_gelu_bwd(yf, dgy, up, dvec):
    tm = _tile(SEQ, 512)

    def body(yf_ref, dg_ref, u_ref, d_ref, dyf_ref, du_ref, dd_ref):
        _, vjp = jax.vjp(jax.nn.gelu, yf_ref[...])
        dyf, = vjp(dg_ref[...])
        dyf_ref[...] = dyf
        du_ref[...] = d_ref[...] * dyf

        @pl.when(pl.program_id(0) == 0)
        def _():
            dd_ref[...] = jnp.zeros_like(dd_ref)

        dd_ref[...] += jnp.sum(dyf * u_ref[...], axis=0, keepdims=True)

    blk = pl.BlockSpec((tm, SSM_WIDTH), lambda i: (i, 0))
    row = pl.BlockSpec((1, SSM_WIDTH), lambda i: (0, 0))
    return pl.pallas_call(body, name="ssm_gelu_bwd", grid=(SEQ // tm,), in_specs=[blk, blk, blk, row],
                          out_specs=[blk, blk, row], out_shape=[_sds((SEQ, SSM_WIDTH))] * 2 + [_sds((1, SSM_WIDTH))],
                          compiler_params=_cparams(("arbitrary",)))(yf, dgy, up, dvec)


def _glu(ab):
    return ab[:, :SSM_WIDTH] * jax.nn.sigmoid(ab[:, SSM_WIDTH:])


def _ssm_glu(ab):
    tm = _tile(SEQ, 512)

    def body(ab_ref, o_ref):
        o_ref[...] = _glu(ab_ref[...])

    return pl.pallas_call(body, name="ssm_glu", grid=(SEQ // tm,),
                          in_specs=[pl.BlockSpec((tm, 2 * SSM_WIDTH), lambda i: (i, 0))],
                          out_specs=pl.BlockSpec((tm, SSM_WIDTH), lambda i: (i, 0)),
                          out_shape=_sds((SEQ, SSM_WIDTH)), compiler_params=_cparams(("parallel",)))(ab)


def _ssm_glu_bwd(ab, dout):
    tm = _tile(SEQ, 512)

    def body(ab_ref, do_ref, dab_ref):
        _, vjp = jax.vjp(_glu, ab_ref[...])
        dab_ref[...], = vjp(do_ref[...])

    return pl.pallas_call(body, name="ssm_glu_bwd", grid=(SEQ // tm,),
                          in_specs=[pl.BlockSpec((tm, 2 * SSM_WIDTH), lambda i: (i, 0)),
                                    pl.BlockSpec((tm, SSM_WIDTH), lambda i: (i, 0))],
                          out_specs=pl.BlockSpec((tm, 2 * SSM_WIDTH), lambda i: (i, 0)),
                          out_shape=_sds((SEQ, 2 * SSM_WIDTH)), compiler_params=_cparams(("parallel",)))(ab, dout)


def _add2(name, a, b):
    tm = _tile(a.shape[0], 512)

    def body(a_ref, b_ref, o_ref):
        o_ref[...] = a_ref[...] + b_ref[...]

    blk = pl.BlockSpec((tm, a.shape[1]), lambda i: (i, 0))
    return pl.pallas_call(body, name=name, grid=(a.shape[0] // tm,), in_specs=[blk, blk], out_specs=blk,
                          out_shape=_sds(a.shape), compiler_params=_cparams(("parallel",)))(a, b)


def _layer_norm(r, g, b):
    mu = r.mean(-1, keepdims=True)
    var = jnp.square(r - mu).mean(-1, keepdims=True)
    return (r - mu) * lax.rsqrt(var + LN_EPS) * g + b


def _ln_fwd(name, x, y, g, b):
    tm = _tile(SEQ, 256)

    def body(x_ref, y_ref, g_ref, b_ref, r_ref, o_ref):
        r = DEEPNORM_ALPHA * x_ref[...] + y_ref[...]
        r_ref[...] = r
        o_ref[...] = _layer_norm(r, g_ref[...], b_ref[...])

    blk = pl.BlockSpec((tm, D_MODEL), lambda i: (i, 0))
    row = pl.BlockSpec((1, D_MODEL), lambda i: (0, 0))
    return pl.pallas_call(body, name=name, grid=(SEQ // tm,), in_specs=[blk, blk, row, row], out_specs=[blk, blk],
                          out_shape=[_sds((SEQ, D_MODEL))] * 2, compiler_params=_cparams(("parallel",)))(x, y, g, b)


def _ln_bwd(name, r, g, b, da, db=None):
    tm = _tile(SEQ, 256)
    two = db is not None

    def body(r_ref, g_ref, b_ref, da_ref, *rest):
        if two:
            db_ref, dr_ref, dg_ref, dbeta_ref = rest
            dout = DEEPNORM_ALPHA * da_ref[...] + db_ref[...]
        else:
            dr_ref, dg_ref, dbeta_ref = rest
            dout = da_ref[...]
        _, vjp = jax.vjp(_layer_norm, r_ref[...], g_ref[...], b_ref[...])
        dr, dg, dbeta = vjp(dout)
        dr_ref[...] = dr

        @pl.when(pl.program_id(0) == 0)
        def _():
            dg_ref[...] = jnp.zeros_like(dg_ref)
            dbeta_ref[...] = jnp.zeros_like(dbeta_ref)

        dg_ref[...] += dg
        dbeta_ref[...] += dbeta

    blk = pl.BlockSpec((tm, D_MODEL), lambda i: (i, 0))
    row = pl.BlockSpec((1, D_MODEL), lambda i: (0, 0))
    args = [r, g, b, da] + ([db] if two else [])
    return pl.pallas_call(body, name=name, grid=(SEQ // tm,), in_specs=[blk, row, row, blk] + ([blk] if two else []),
                          out_specs=[blk, row, row], out_shape=[_sds((SEQ, D_MODEL)), _sds((1, D_MODEL)), _sds((1, D_MODEL))],
                          compiler_params=_cparams(("arbitrary",)))(*args)


FFN_TM = 128
HALO = SUBLANES


def _conv_taps(cur, halo):
    row = lax.broadcasted_iota(jnp.int32, cur.shape, 0)
    h1 = jnp.where(row == 0, halo[HALO - 1:HALO, :], pltpu.roll(cur, 1, 0))
    h2 = jnp.where(row == 0, halo[HALO - 2:HALO - 1, :], jnp.where(row == 1, halo[HALO - 1:HALO, :], pltpu.roll(cur, 2, 0)))
    return h1, h2


def _conv_fwd(cur, halo, w_ref, b_ref):
    h1, h2 = _conv_taps(cur, halo)
    return b_ref[...] + h2 * w_ref[0:1, :] + h1 * w_ref[1:2, :] + cur * w_ref[2:3, :], h1, h2


def _gate(val, gate):
    return jax.nn.silu(gate) * val


def _ffn_specs(tm):
    nb = tm // HALO
    cur = lambda off: pl.BlockSpec((None, tm, FS), lambda p, i: (p + off, i, 0))
    halo = lambda off: pl.BlockSpec((None, HALO, FS), lambda p, i: (p + off, jnp.maximum(i * nb - 1, 0), 0))
    cw = lambda off: pl.BlockSpec((None, CONV_WIDTH, FS), lambda p, i: (p + off, 0, 0))
    cb = lambda off: pl.BlockSpec((None, 1, FS), lambda p, i: (p + off, 0, 0))
    return cur, halo, cw, cb


def _ffn_act(hf, conv_w, conv_b):
    tm = _tile(SEQ, FFN_TM, SUBLANES)
    cur, halo, cw, cb = _ffn_specs(tm)

    def body(v_ref, vh_ref, g_ref, gh_ref, wv_ref, wg_ref, bv_ref, bg_ref, o_ref):
        live = pl.program_id(1) > 0
        vh = jnp.where(live, vh_ref[...], 0.0)
        gh = jnp.where(live, gh_ref[...], 0.0)
        val, _, _ = _conv_fwd(v_ref[...], vh, wv_ref, bv_ref)
        gate, _, _ = _conv_fwd(g_ref[...], gh, wg_ref, bg_ref)
        o_ref[...] = _gate(val, gate).astype(o_ref.dtype)

    return pl.pallas_call(
        body, name="ffn_act", grid=(2, SEQ // tm),
        in_specs=[cur(0), halo(0), cur(2), halo(2), cw(0), cw(2), cb(0), cb(2)],
        out_specs=pl.BlockSpec((None, tm, FS), lambda p, i: (p, i, 0)),
        out_shape=_sds((2, SEQ, FS), MM_DTYPE), compiler_params=_cparams(("parallel", "parallel")))(
            hf, hf, hf, hf, conv_w, conv_w, conv_b, conv_b)


def _ffn_act_bwd(hf, conv_w, conv_b, dact):
    tm = _tile(SEQ, FFN_TM, SUBLANES)
    cur, halo, cw, cb = _ffn_specs(tm)

    def body(v_ref, vh_ref, g_ref, gh_ref, wv_ref, wg_ref, bv_ref, bg_ref, da_ref,
             dv_ref, dg_ref, dwv_ref, dwg_ref, dbv_ref, dbg_ref):
        i = pl.program_id(1)
        live = i > 0
        vh = jnp.where(live, vh_ref[...], 0.0)
        gh = jnp.where(live, gh_ref[...], 0.0)
        vcur, gcur = v_ref[...], g_ref[...]
        val, v1, v2 = _conv_fwd(vcur, vh, wv_ref, bv_ref)
        gate, g1, g2 = _conv_fwd(gcur, gh, wg_ref, bg_ref)
        _, vjp = jax.vjp(_gate, val, gate)
        dval, dgate = vjp(da_ref[...])
        dv_ref[...] = dval
        dg_ref[...] = dgate

        @pl.when(i == 0)
        def _():
            for r in (dwv_ref, dwg_ref, dbv_ref, dbg_ref):
                r[...] = jnp.zeros_like(r)

        for d, taps, dw_ref, dbias_ref in ((dval, (v2, v1, vcur), dwv_ref, dbv_ref), (dgate, (g2, g1, gcur), dwg_ref, dbg_ref)):
            for k in range(CONV_WIDTH):
                dw_ref[k:k + 1, :] += jnp.sum(d * taps[k], axis=0, keepdims=True)
            dbias_ref[...] += jnp.sum(d, axis=0, keepdims=True)

    out = lambda off: pl.BlockSpec((None, tm, FS), lambda p, i: (p + off, i, 0))
    return pl.pallas_call(
        body, name="ffn_act_bwd", grid=(2, SEQ // tm),
        in_specs=[cur(0), halo(0), cur(2), halo(2), cw(0), cw(2), cb(0), cb(2),
                  pl.BlockSpec((None, tm, FS), lambda p, i: (p, i, 0))],
        out_specs=[out(0), out(0), cw(0), cw(0), cb(0), cb(0)],
        out_shape=[_sds((2, SEQ, FS)), _sds((2, SEQ, FS)), _sds((2, CONV_WIDTH, FS)), _sds((2, CONV_WIDTH, FS)),
                   _sds((2, 1, FS)), _sds((2, 1, FS))],
        compiler_params=_cparams(("parallel", "arbitrary")))(hf, hf, hf, hf, conv_w, conv_w, conv_b, conv_b, dact)


def _conv_bwd_input(dhc, conv_w):
    tm = _tile(SEQ, FFN_TM, SUBLANES)
    nb = tm // HALO
    nblk = SEQ // tm

    def body(d_ref, nx_ref, w_ref, o_ref):
        cur = d_ref[...]
        nxt = jnp.where(pl.program_id(1) < nblk - 1, nx_ref[...], 0.0)
        row = lax.broadcasted_iota(jnp.int32, cur.shape, 0)
        d1 = jnp.where(row == tm - 1, nxt[0:1, :], pltpu.roll(cur, tm - 1, 0))
        d2 = jnp.where(row == tm - 1, nxt[1:2, :], jnp.where(row == tm - 2, nxt[0:1, :], pltpu.roll(cur, tm - 2, 0)))
        o_ref[...] = cur * w_ref[2:3, :] + d1 * w_ref[1:2, :] + d2 * w_ref[0:1, :]

    blk = pl.BlockSpec((None, tm, FS), lambda j, i: (j, i, 0))
    return pl.pallas_call(
        body, name="conv_bwd_input", grid=(N_CHIPS, nblk),
        in_specs=[blk, pl.BlockSpec((None, HALO, FS), lambda j, i: (j, jnp.minimum((i + 1) * nb, SEQ // HALO - 1), 0)),
                  pl.BlockSpec((None, CONV_WIDTH, FS), lambda j, i: (j, 0, 0))],
        out_specs=blk, out_shape=_sds((N_CHIPS, SEQ, FS)),
        compiler_params=_cparams(("parallel", "parallel")))(dhc, dhc, conv_w)


def _loss(y, target):
    tm = _tile(SEQ, 256)

    def body(y_ref, t_ref, dy_ref, l_ref):
        err = y_ref[...] - t_ref[...]
        dy_ref[...] = err * (1.0 / D_MODEL)

        @pl.when(pl.program_id(0) == 0)
        def _():
            l_ref[...] = jnp.zeros_like(l_ref)

        l_ref[...] += 0.5 * jnp.sum(jnp.mean(jnp.square(err), axis=-1))

    blk = pl.BlockSpec((tm, D_MODEL), lambda i: (i, 0))
    return pl.pallas_call(body, name="loss", grid=(SEQ // tm,), in_specs=[blk, blk],
                          out_specs=[blk, pl.BlockSpec((SUBLANES, LANES), lambda i: (0, 0))],
                          out_shape=[_sds((SEQ, D_MODEL)), _sds((SUBLANES, LANES))],
                          compiler_params=_cparams(("arbitrary",)))(y, target)


def _adamw(name, w, g, m, v):
    r, c = w.shape
    tr = _tile(r, max(SUBLANES, (1 << 20) // (4 * c) // SUBLANES * SUBLANES), SUBLANES)

    def body(w_ref, g_ref, m_ref, v_ref, d_ref, nm_ref, nv_ref):
        g_ = g_ref[...]
        nm = ADAM_B1 * m_ref[...] + (1.0 - ADAM_B1) * g_
        nv = ADAM_B2 * v_ref[...] + (1.0 - ADAM_B2) * jnp.square(g_)
        m_hat = nm / (1.0 - ADAM_B1 ** ADAM_STEP)
        v_hat = nv / (1.0 - ADAM_B2 ** ADAM_STEP)
        d_ref[...] = -ADAM_LR * (m_hat / (jnp.sqrt(v_hat) + ADAM_EPS) + ADAM_WD * w_ref[...])
        nm_ref[...] = nm
        nv_ref[...] = nv

    blk = pl.BlockSpec((tr, c), lambda i: (i, 0))
    return pl.pallas_call(body, name=name, grid=(r // tr,), in_specs=[blk] * 4, out_specs=[blk] * 3,
                          out_shape=[_sds((r, c))] * 3, compiler_params=_cparams(("parallel",)))(w, g, m, v)


ANY = pl.BlockSpec(memory_space=pl.ANY)


def _place():
    x, y, c = lax.axis_index("x"), lax.axis_index("y"), lax.axis_index("c")
    chips = [(1 - x, y), (x, 1 - y), (1 - x, 1 - y)]
    return x, y, c, chips


def _gather_weights(shards):
    n = len(shards)

    def body(*refs):
        ins, outs = refs[:n], refs[n:2 * n]
        send_sems, recv_sems, loc_sems = refs[2 * n:]
        x, y, c, chips = _place()
        me, sib = 2 * x + y, (x, y, 1 - c)

        def copy(i, k, chip, half, to):
            blk = outs[i].at[2 * chip[0] + chip[1], half]
            return pltpu.make_async_remote_copy(src_ref=blk, dst_ref=blk, send_sem=send_sems.at[6 * i + k],
                                                recv_sem=recv_sems.at[6 * i + k], device_id=to, device_id_type=MESH)

        local = [pltpu.make_async_copy(ins[i], outs[i].at[me], loc_sems.at[i]) for i in range(n)]
        for cp in local:
            cp.start()
        first = []
        for i in range(n):
            for k, chip in enumerate(chips):
                cp = pltpu.make_async_remote_copy(
                    src_ref=ins[i].at[c], dst_ref=outs[i].at[me, c], send_sem=send_sems.at[6 * i + k],
                    recv_sem=recv_sems.at[6 * i + k], device_id=(*chip, c), device_id_type=MESH)
                cp.start()
                first.append(cp)
        passed = []
        for i in range(n):
            for k, chip in enumerate(chips):
                copy(i, k, chip, c, (x, y, c)).wait_recv()
                cp = copy(i, 3 + k, chip, c, sib)
                cp.start()
                passed.append(cp)
        for i in range(n):
            for k, chip in enumerate(chips):
                copy(i, 3 + k, chip, 1 - c, (x, y, c)).wait_recv()
        for cp in first + passed:
            cp.wait_send()
        for cp in local:
            cp.wait()

    return pl.pallas_call(
        body, name="gather_weights", in_specs=[ANY] * n, out_specs=[ANY] * n,
        out_shape=[_sds((N_CHIPS,) + a.shape, a.dtype) for a in shards],
        scratch_shapes=[pltpu.SemaphoreType.DMA((6 * n,)), pltpu.SemaphoreType.DMA((6 * n,)), pltpu.SemaphoreType.DMA((n,))],
    )(*shards)


def _pair_swap(grads):
    n = len(grads)

    def body(*refs):
        ins, outs = refs[:n], refs[n:2 * n]
        send_sems, recv_sems = refs[2 * n:]
        x, y, c, _ = _place()
        cps = []
        for i in range(n):
            cp = pltpu.make_async_remote_copy(src_ref=ins[i].at[:, 1 - c], dst_ref=outs[i], send_sem=send_sems.at[i],
                                              recv_sem=recv_sems.at[i], device_id=(x, y, 1 - c), device_id_type=MESH)
            cp.start()
            cps.append(cp)
        for cp in cps:
            cp.wait()

    return pl.pallas_call(
        body, name="grad_pair_swap", in_specs=[ANY] * n, out_specs=[ANY] * n,
        out_shape=[_sds((g.shape[0],) + g.shape[2:], g.dtype) for g in grads],
        scratch_shapes=[pltpu.SemaphoreType.DMA((n,)), pltpu.SemaphoreType.DMA((n,))])(*grads)


def _pair_add(name, g, got, c_idx):
    nk, _, r, c = g.shape
    tr = _tile(r, max(2 * SUBLANES, COPY_BLOCK_BYTES // (4 * c)), 2 * SUBLANES)

    def body(c_ref, g_ref, x_ref, o_ref):
        o_ref[...] = (g_ref[...] + x_ref[...]).astype(o_ref.dtype)

    grid_spec = pltpu.PrefetchScalarGridSpec(
        num_scalar_prefetch=1, grid=(nk, r // tr),
        in_specs=[pl.BlockSpec((None, None, tr, c), lambda k, i, cr: (k, cr[0], i, 0)),
                  pl.BlockSpec((None, tr, c), lambda k, i, cr: (k, i, 0))],
        out_specs=pl.BlockSpec((None, tr, c), lambda k, i, cr: (k, i, 0)))
    return pl.pallas_call(body, name=name, grid_spec=grid_spec, out_shape=_sds((nk, r, c), BF16),
                          compiler_params=_cparams(("parallel", "parallel")))(c_idx, g, got)


def _chip_scatter(parts):
    n = len(parts)

    def body(*refs):
        ins, outs = refs[:n], refs[n:2 * n]
        send_sems, recv_sems, loc_sems = refs[2 * n:]
        x, y, c, chips = _place()
        me = 2 * x + y
        local = [pltpu.make_async_copy(ins[i].at[me], outs[i].at[me], loc_sems.at[i]) for i in range(n)]
        for cp in local:
            cp.start()
        cps = []
        for i in range(n):
            for k, chip in enumerate(chips):
                cp = pltpu.make_async_remote_copy(
                    src_ref=ins[i].at[2 * chip[0] + chip[1]], dst_ref=outs[i].at[me], send_sem=send_sems.at[3 * i + k],
                    recv_sem=recv_sems.at[3 * i + k], device_id=(*chip, c), device_id_type=MESH)
                cp.start()
                cps.append(cp)
        for cp in cps:
            cp.wait_send()
        for i in range(n):
            for k, chip in enumerate(chips):
                blk = outs[i].at[2 * chip[0] + chip[1]]
                pltpu.make_async_remote_copy(src_ref=blk, dst_ref=blk, send_sem=send_sems.at[3 * i + k],
                                             recv_sem=recv_sems.at[3 * i + k], device_id=(*chip, c),
                                             device_id_type=MESH).wait_recv()
        for cp in local:
            cp.wait()

    return pl.pallas_call(
        body, name="grad_chip_scatter", in_specs=[ANY] * n, out_specs=[ANY] * n,
        out_shape=[_sds(p.shape, p.dtype) for p in parts],
        scratch_shapes=[pltpu.SemaphoreType.DMA((3 * n,)), pltpu.SemaphoreType.DMA((3 * n,)), pltpu.SemaphoreType.DMA((n,))])(*parts)


def _sum_leading(name, x, out_dtype=F32):
    nk, r, c = x.shape
    tr = _tile(r, max(2 * SUBLANES, COPY_BLOCK_BYTES // (nk * c * x.dtype.itemsize)), 2 * SUBLANES)

    def body(x_ref, o_ref):
        acc = x_ref[0].astype(F32)
        for k in range(1, nk):
            acc = acc + x_ref[k].astype(F32)
        o_ref[...] = acc.astype(o_ref.dtype)

    return pl.pallas_call(body, name=name, grid=(r // tr,), in_specs=[pl.BlockSpec((nk, tr, c), lambda i: (0, i, 0))],
                          out_specs=pl.BlockSpec((tr, c), lambda i: (i, 0)), out_shape=_sds((r, c), out_dtype),
                          compiler_params=_cparams(("parallel",)))(x)


def _pair_exchange(halves):
    n = len(halves)

    def body(*refs):
        ins, outs = refs[:n], refs[n:2 * n]
        send_sems, recv_sems, loc_sems = refs[2 * n:]
        x, y, c, _ = _place()
        local = [pltpu.make_async_copy(ins[i], outs[i].at[c], loc_sems.at[i]) for i in range(n)]
        for cp in local:
            cp.start()
        cps = []
        for i in range(n):
            cp = pltpu.make_async_remote_copy(src_ref=ins[i], dst_ref=outs[i].at[c], send_sem=send_sems.at[i],
                                              recv_sem=recv_sems.at[i], device_id=(x, y, 1 - c), device_id_type=MESH)
            cp.start()
            cps.append(cp)
        for cp in cps:
            cp.wait_send()
        for i in range(n):
            blk = outs[i].at[1 - c]
            pltpu.make_async_remote_copy(src_ref=blk, dst_ref=blk, send_sem=send_sems.at[i], recv_sem=recv_sems.at[i],
                                         device_id=(x, y, 1 - c), device_id_type=MESH).wait_recv()
        for cp in local:
            cp.wait()

    return pl.pallas_call(
        body, name="grad_pair_exchange", in_specs=[ANY] * n, out_specs=[ANY] * n,
        out_shape=[_sds((2,) + h.shape, h.dtype) for h in halves],
        scratch_shapes=[pltpu.SemaphoreType.DMA((n,)), pltpu.SemaphoreType.DMA((n,)), pltpu.SemaphoreType.DMA((n,))])(*halves)


def _gather_all(part):
    def body(x_ref, out_ref, send_sems, recv_sems, local_sem):
        x, y, c, chips = _place()
        me, sibling = (x, y, c), (x, y, 1 - c)

        def rows(px, py, pc):
            return out_ref.at[4 * px + 2 * py + pc]

        def copy(k, block, to, src=None):
            return pltpu.make_async_remote_copy(src_ref=rows(*block) if src is None else src, dst_ref=rows(*block),
                                                send_sem=send_sems.at[k], recv_sem=recv_sems.at[k], device_id=to,
                                                device_id_type=MESH)

        mine = pltpu.make_async_copy(x_ref, rows(*me), local_sem)
        mine.start()
        first = [copy(0, me, sibling, src=x_ref)]
        first += [copy(1 + j, me, (*chip, c), src=x_ref) for j, chip in enumerate(chips)]
        for cp in first:
            cp.start()
        passed = [copy(4 + j, (*chip, c), sibling) for j, chip in enumerate(chips)]
        for j, chip in enumerate(chips):
            copy(1 + j, (*chip, c), me).wait_recv()
            passed[j].start()
        copy(0, sibling, me).wait_recv()
        for j, chip in enumerate(chips):
            copy(4 + j, (*chip, 1 - c), me).wait_recv()
        for cp in first + passed:
            cp.wait_send()
        mine.wait()

    return pl.pallas_call(
        body, name="gather_small_grads", in_specs=[ANY], out_specs=ANY, out_shape=_sds((N_DEV,) + part.shape, part.dtype),
        scratch_shapes=[pltpu.SemaphoreType.DMA((7,)), pltpu.SemaphoreType.DMA((7,)), pltpu.SemaphoreType.DMA])(part)


SMALL = ("attn_sinks", "pool_w", "pool_scale", "ssm_lam_re", "ssm_lam_im", "ssm_log_dt", "ssm_b_re", "ssm_b_im",
         "ssm_c_re", "ssm_c_im", "ssm_d", "ln1_g", "ln1_b", "ffn_conv_b", "ln2_g", "ln2_b")
BIG = ("w_in", "ssm_glu_w", "w_out", "ffn_w_up", "ffn_conv_w", "ffn_w_down")
ALL_W = ("w_in", "attn_sinks", "pool_w", "pool_scale", "ssm_lam_re", "ssm_lam_im", "ssm_log_dt", "ssm_b_re", "ssm_b_im",
         "ssm_c_re", "ssm_c_im", "ssm_d", "ssm_glu_w", "w_out", "ln1_g", "ln1_b", "ffn_w_up", "ffn_conv_w", "ffn_conv_b",
         "ffn_w_down", "ln2_g", "ln2_b")
PACK_UNIT = SUBLANES * LANES


def _padded(n):
    return -(-n // PACK_UNIT) * PACK_UNIT


def _pack(arrs):
    cols = []
    for name in SMALL:
        a = arrs[name].reshape(DEPTH, -1)
        cols.append(jnp.pad(a, ((0, 0), (0, _padded(a.shape[1]) - a.shape[1]))))
    return jnp.concatenate(cols, axis=1).reshape(-1, LANES)


def _unpack(packed, shapes):
    flat = packed.reshape(DEPTH, -1)
    out, off = {}, 0
    for name in SMALL:
        n = math.prod(shapes[name][1:])
        out[name] = flat[:, off:off + n].reshape(shapes[name])
        off += _padded(n)
    return out


def _b_rows(b):
    return b.transpose(2, 0, 1).reshape(SSM_GROUP, SSM_CH)


def _b_unrows(b):
    return b.reshape(SSM_GROUP, SSM_N_GROUPS, SSM_STATE).transpose(1, 2, 0)


def _block_diag_in(bb):
    eye = jnp.eye(SSM_N_GROUPS, dtype=F32)
    b3 = bb.reshape(SSM_GROUP, SSM_N_GROUPS, SSM_STATE)
    return jnp.einsum("hgp,gk->ghkp", b3, eye).reshape(SSM_WIDTH, SSM_CH)


def _block_diag_in_t(full):
    f4 = full.reshape(SSM_N_GROUPS, SSM_GROUP, SSM_N_GROUPS, SSM_STATE)
    return jnp.einsum("ghgp->hgp", f4).reshape(SSM_GROUP, SSM_CH)


def _block_diag_out(cc):
    eye = jnp.eye(SSM_N_GROUPS, dtype=F32)
    return jnp.einsum("ghp,gk->gpkh", cc, eye).reshape(SSM_CH, SSM_WIDTH)


def _block_diag_out_t(full):
    f4 = full.reshape(SSM_N_GROUPS, SSM_STATE, SSM_N_GROUPS, SSM_GROUP)
    return jnp.einsum("gpgh->ghp", f4)


def _rows_layout(re, im):
    n = re.shape[1]
    return jnp.stack([re.reshape(SCAN_NB, SCAN_CW, n), im.reshape(SCAN_NB, SCAN_CW, n)], axis=1).reshape(2 * SSM_CH, n)


def _rows_unlayout(x):
    n = x.shape[1]
    x = x.reshape(SCAN_NB, 2, SCAN_CW, n)
    return x[:, 0].reshape(SSM_CH, n), x[:, 1].reshape(SSM_CH, n)


H_POOL0 = ATTN_WIDTH + 2 * KV_WIDTH
H_SSM0 = H_POOL0 + POOL_WIDTH


def _ssm_params(p):
    lr = p["ssm_lam_re"].reshape(1, SSM_CH)
    li = p["ssm_lam_im"].reshape(1, SSM_CH)
    ldt = jnp.repeat(p["ssm_log_dt"], SSM_STATE).reshape(1, SSM_CH)
    return lr, li, ldt, _b_rows(p["ssm_b_re"]), _b_rows(p["ssm_b_im"])


def _layer_fwd(x, p, wg, rope_t):
    cos_t, sin_t = rope_t
    h = _mm_shard_cols("in_proj", x, wg["w_in"])
    qk = _rope("rope_fwd", h, 0, Q_TILES + KV_TILES, cos_t, sin_t, MM_DTYPE)
    y_attn = _attn_fwd(qk, h, p["attn_sinks"])
    y_pool = _pool_fwd(h, p["pool_w"], p["pool_scale"].reshape(1, POOL_WIDTH))
    ssm_in = _ssm_params(p)
    ar, ai, bbr, bbi = _ssm_prep(*ssm_in)
    bd = _scan_layout(_block_diag_in(bbr), _block_diag_in(bbi))
    cc = _rows_layout(_block_diag_out(p["ssm_c_re"]), -_block_diag_out(p["ssm_c_im"]))
    dvec = p["ssm_d"].reshape(1, SSM_WIDTH)
    up = _time_permute(h[:, H_SSM0:])
    xx = _mm_nn("ssm_bu", up, bd, tn=1024)
    ss = _ssm_scan("ssm_scan_fwd", _scan_layout(ar, ai), xx, False)
    yp = _mm_nn("ssm_cs", ss, cc, tk=1024)
    yf, gy = _ssm_gelu(yp, up, dvec)
    ab = _mm_shard_cols("ssm_glu_proj", gy, wg["ssm_glu_w"])
    y_ssm = _time_unpermute(_ssm_glu(ab))
    mix = jnp.concatenate([y_attn, y_pool, y_ssm], axis=1)
    mixo = _mm_nn("out_proj", mix, wg["w_out"].reshape(MIX_WIDTH, D_MODEL))
    r1, x1 = _ln_fwd("ln1_fwd", x, mixo, p["ln1_g"].reshape(1, D_MODEL), p["ln1_b"].reshape(1, D_MODEL))
    hf = _ffn_up(x1, wg["ffn_w_up"])
    conv_b = p["ffn_conv_b"].reshape(N_CHIPS, 1, FS)
    act = _ffn_act(hf, wg["ffn_conv_w"], conv_b)
    f = _ffn_down(act, wg["ffn_w_down"].reshape(2, FS, D_MODEL))
    r2, x2 = _ln_fwd("ln2_fwd", x1, f, p["ln2_g"].reshape(1, D_MODEL), p["ln2_b"].reshape(1, D_MODEL))
    saved = dict(x=x, h=h, qk=qk, y_attn=y_attn, ssm_in=ssm_in, ar=ar, ai=ai, bd=bd, cc=cc, dvec=dvec, up=up, ss=ss, yf=yf,
                 gy=gy, ab=ab, mix=mix, r1=r1, x1=x1, hf=hf, conv_b=conv_b, act=act, r2=r2)
    return x2, saved


def _layer_bwd(da, db, p, wg, sv, rope_t):
    cos_t, sin_t = rope_t
    small = {}
    dr2, dg, dbeta = _ln_bwd("ln2_bwd" if db is not None else "ln2_bwd_last", sv["r2"], p["ln2_g"].reshape(1, D_MODEL),
                             p["ln2_b"].reshape(1, D_MODEL), da, db)
    small["ln2_g"], small["ln2_b"] = dg, dbeta
    w_down = wg["ffn_w_down"].reshape(2, FS, D_MODEL)
    dact = _ffn_down_dact(dr2, w_down)
    dw_down = _ffn_down_dw(sv["act"], dr2)
    dval, dgate, dwv, dwg, dbv, dbg = _ffn_act_bwd(sv["hf"], wg["ffn_conv_w"], sv["conv_b"], dact)
    dhc = jnp.concatenate([dval, dgate], axis=0)
    dconv_w = jnp.concatenate([dwv, dwg], axis=0)
    small["ffn_conv_b"] = jnp.concatenate([dbv, dbg], axis=0)
    dh_ffn = _conv_bwd_input(dhc, wg["ffn_conv_w"])
    dx1_ffn = _ffn_up_dx(dh_ffn, wg["ffn_w_up"])
    dw_up = _ffn_up_dw(sv["x1"], dh_ffn)
    dr1, dg, dbeta = _ln_bwd("ln1_bwd", sv["r1"], p["ln1_g"].reshape(1, D_MODEL), p["ln1_b"].reshape(1, D_MODEL), dr2, dx1_ffn)
    small["ln1_g"], small["ln1_b"] = dg, dbeta
    w_out = wg["w_out"].reshape(MIX_WIDTH, D_MODEL)
    dmix = _mm_nt("out_proj_dx", dr1, w_out)
    dw_out = _mm_tn("out_proj_dw", sv["mix"], dr1)
    dq, dkc, dkp, dvc, dvp, dsk = _attn_bwd(sv["qk"], sv["h"], p["attn_sinks"], sv["y_attn"], dmix, 0)
    small["attn_sinks"] = dsk[:, :, 0]
    dh_attn = _attn_dh(dq, dkc, dkp, dvc, dvp, cos_t, -sin_t)
    dh_pool, dpw, dps = _pool_bwd(sv["h"], p["pool_w"], p["pool_scale"].reshape(1, POOL_WIDTH), dmix, ATTN_WIDTH // POOL_WIDTH)
    small["pool_w"], small["pool_scale"] = dpw, dps
    dout_p = _time_permute(dmix[:, ATTN_WIDTH + POOL_WIDTH:])
    dab = _ssm_glu_bwd(sv["ab"], dout_p)
    dgy = _mm_shard_cols_nt("ssm_glu_dx", dab, wg["ssm_glu_w"])
    dw_glu = _mm_shard_cols_tn("ssm_glu_dw", sv["gy"], dab, N_CHIPS)
    dyf, du1, dd = _ssm_gelu_bwd(sv["yf"], dgy, sv["up"], sv["dvec"])
    small["ssm_d"] = dd
    dss = _mm_nt("ssm_cs_dx", dyf, sv["cc"], tn=1024)
    dcc = _mm_tn("ssm_cs_dw", sv["ss"], dyf, tm=1024)
    dcre, dcim = _rows_unlayout(dcc)
    small["ssm_c_re"], small["ssm_c_im"] = _block_diag_out_t(dcre), -_block_diag_out_t(dcim)
    gg, da8 = _ssm_scan("ssm_scan_bwd", _scan_layout(sv["ar"], -sv["ai"]), dss, True, sv["ss"])
    du2 = _mm_nt("ssm_bu_dx", gg, sv["bd"], tk=1024)
    dbd = _mm_tn("ssm_bu_dw", sv["up"], gg, tn=1024)
    dbdr, dbdi = _scan_unlayout(dbd)
    dar8, dai8 = _scan_unlayout(da8)
    dlr, dli, dldt, dbr, dbi = _ssm_prep_bwd(*sv["ssm_in"], dar8, dai8, _block_diag_in_t(dbdr), _block_diag_in_t(dbdi))
    small["ssm_lam_re"], small["ssm_lam_im"] = dlr, dli
    small["ssm_log_dt"] = dldt.reshape(SSM_N_GROUPS, SSM_STATE).sum(axis=1)
    small["ssm_b_re"], small["ssm_b_im"] = _b_unrows(dbr), _b_unrows(dbi)
    dh_ssm = _time_unpermute(_add2("ssm_du", du1, du2))
    dh = jnp.concatenate([dh_attn, dh_pool, dh_ssm], axis=1)
    dx_in = _mm_shard_cols_nt("in_proj_dx", dh, wg["w_in"])
    dw_in = _mm_shard_cols_tn("in_proj_dw", sv["x"], dh, N_CHIPS)
    big = {"w_in": dw_in, "ssm_glu_w": dw_glu, "w_out": dw_out.reshape(N_CHIPS, MIX_WIDTH // N_CHIPS, D_MODEL),
           "ffn_w_up": dw_up, "ffn_conv_w": dconv_w, "ffn_w_down": dw_down.reshape(N_CHIPS, FS // 2, D_MODEL)}
    return dr1, dx_in, big, small


CONV_PAD = 2 * SUBLANES


def _halved(name, a):
    if name == "ffn_conv_w":
        return jnp.pad(a, ((0, CONV_PAD - CONV_WIDTH), (0, 0))).reshape(2, SUBLANES, FS)
    return a.reshape(2, a.shape[0] // 2, a.shape[1])


def _unhalved(name, a):
    a = a.reshape(a.shape[:-3] + (2 * a.shape[-2], a.shape[-1]))
    return a[..., :CONV_WIDTH, :] if name == "ffn_conv_w" else a


def _reduce_big(big, c_idx):
    g4 = []
    for name in BIG:
        g = big[name]
        if name == "ffn_conv_w":
            g = jnp.pad(g, ((0, 0), (0, CONV_PAD - CONV_WIDTH), (0, 0)))
        g4.append(g.reshape(N_CHIPS, 2, g.shape[1] // 2, g.shape[2]))
    got = _pair_swap(g4)
    parts = [_pair_add("grad_pair_add", g, x, c_idx) for g, x in zip(g4, got)]
    recv = _chip_scatter(parts)
    halves = [_sum_leading("grad_chip_sum", r) for r in recv]
    full = _pair_exchange(halves)
    return {name: _unhalved(name, f) for name, f in zip(BIG, full)}


def kernel(x, w_in, attn_sinks, pool_w, pool_scale, ssm_lam_re, ssm_lam_im, ssm_log_dt, ssm_b_re, ssm_b_im, ssm_c_re, ssm_c_im, ssm_d, ssm_glu_w, w_out, ln1_g, ln1_b, ffn_w_up, ffn_conv_w, ffn_conv_b, ffn_w_down, ln2_g, ln2_b, loss_target, m_w_in, m_attn_sinks, m_pool_w, m_pool_scale, m_ssm_lam_re, m_ssm_lam_im, m_ssm_log_dt, m_ssm_b_re, m_ssm_b_im, m_ssm_c_re, m_ssm_c_im, m_ssm_d, m_ssm_glu_w, m_w_out, m_ln1_g, m_ln1_b, m_ffn_w_up, m_ffn_conv_w, m_ffn_conv_b, m_ffn_w_down, m_ln2_g, m_ln2_b, v_w_in, v_attn_sinks, v_pool_w, v_pool_scale, v_ssm_lam_re, v_ssm_lam_im, v_ssm_log_dt, v_ssm_b_re, v_ssm_b_im, v_ssm_c_re, v_ssm_c_im, v_ssm_d, v_ssm_glu_w, v_w_out, v_ln1_g, v_ln1_b, v_ffn_w_up, v_ffn_conv_w, v_ffn_conv_b, v_ffn_w_down, v_ln2_g, v_ln2_b):
    w = dict(w_in=w_in, attn_sinks=attn_sinks, pool_w=pool_w, pool_scale=pool_scale, ssm_lam_re=ssm_lam_re,
             ssm_lam_im=ssm_lam_im, ssm_log_dt=ssm_log_dt, ssm_b_re=ssm_b_re, ssm_b_im=ssm_b_im, ssm_c_re=ssm_c_re,
             ssm_c_im=ssm_c_im, ssm_d=ssm_d, ssm_glu_w=ssm_glu_w, w_out=w_out, ln1_g=ln1_g, ln1_b=ln1_b, ffn_w_up=ffn_w_up,
             ffn_conv_w=ffn_conv_w, ffn_conv_b=ffn_conv_b, ffn_w_down=ffn_w_down, ln2_g=ln2_g, ln2_b=ln2_b)
    m = dict(w_in=m_w_in, attn_sinks=m_attn_sinks, pool_w=m_pool_w, pool_scale=m_pool_scale, ssm_lam_re=m_ssm_lam_re,
             ssm_lam_im=m_ssm_lam_im, ssm_log_dt=m_ssm_log_dt, ssm_b_re=m_ssm_b_re, ssm_b_im=m_ssm_b_im, ssm_c_re=m_ssm_c_re,
             ssm_c_im=m_ssm_c_im, ssm_d=m_ssm_d, ssm_glu_w=m_ssm_glu_w, w_out=m_w_out, ln1_g=m_ln1_g, ln1_b=m_ln1_b,
             ffn_w_up=m_ffn_w_up, ffn_conv_w=m_ffn_conv_w, ffn_conv_b=m_ffn_conv_b, ffn_w_down=m_ffn_w_down, ln2_g=m_ln2_g,
             ln2_b=m_ln2_b)
    v = dict(w_in=v_w_in, attn_sinks=v_attn_sinks, pool_w=v_pool_w, pool_scale=v_pool_scale, ssm_lam_re=v_ssm_lam_re,
             ssm_lam_im=v_ssm_lam_im, ssm_log_dt=v_ssm_log_dt, ssm_b_re=v_ssm_b_re, ssm_b_im=v_ssm_b_im, ssm_c_re=v_ssm_c_re,
             ssm_c_im=v_ssm_c_im, ssm_d=v_ssm_d, ssm_glu_w=v_ssm_glu_w, w_out=v_w_out, ln1_g=v_ln1_g, ln1_b=v_ln1_b,
             ffn_w_up=v_ffn_w_up, ffn_conv_w=v_ffn_conv_w, ffn_conv_b=v_ffn_conv_b, ffn_w_down=v_ffn_w_down, ln2_g=v_ln2_g,
             ln2_b=v_ln2_b)
    c_idx = lax.axis_index("c").astype(jnp.int32).reshape(1)
    rope_t = _rope_tables()
    xs = x.reshape(SEQ, D_MODEL)

    gathered, saved = [], []
    for l in range(DEPTH):
        shards = [_halved(n, w[n][l] if n == "ffn_conv_w" else w[n][l].astype(MM_DTYPE)) for n in BIG]
        wg = {n: _unhalved(n, g) for n, g in zip(BIG, _gather_weights(shards))}
        p = {n: w[n][l] for n in SMALL}
        xs, sv = _layer_fwd(xs, p, wg, rope_t)
        gathered.append(wg)
        saved.append(sv)
    dy, loss_tile = _loss(xs, loss_target.reshape(SEQ, D_MODEL))
    loss = lax.psum(loss_tile[0, 0], ("x", "y", "c"))

    big_out = {n: [None] * DEPTH for n in BIG}
    small_g = {n: [None] * DEPTH for n in SMALL}
    da, db = dy, None
    for l in reversed(range(DEPTH)):
        p = {n: w[n][l] for n in SMALL}
        da, db, big, small = _layer_bwd(da, db, p, gathered[l], saved[l], rope_t)
        for n in SMALL:
            small_g[n][l] = small[n].reshape(w[n].shape[1:])
        for n, g in _reduce_big(big, c_idx).items():
            shp = w[n].shape[1:]
            two_d = (shp[0], shp[1])
            g = g.reshape(two_d)
            big_out[n][l] = (g,) + tuple(_adamw("adamw_" + n, w[n][l].reshape(two_d), g, m[n][l].reshape(two_d),
                                                v[n][l].reshape(two_d)))
    grad_x = _ln_in_grad(da, db).reshape(x.shape)

    shapes = {n: w[n].shape for n in SMALL}
    part = _pack({n: jnp.stack(small_g[n]) for n in SMALL})
    g_small = _sum_leading("small_grad_sum", _gather_all(part))
    upd = _adamw("adamw_small", _pack(w), g_small, _pack(m), _pack(v))
    small_out = [_unpack(a, shapes) for a in (g_small,) + tuple(upd)]

    outs = [loss, grad_x]
    for kind in range(4):
        for n in ALL_W:
            if n in SMALL:
                outs.append(small_out[kind][n])
            else:
                outs.append(jnp.stack([big_out[n][l][kind] for l in range(DEPTH)]).reshape(w[n].shape))
    return tuple(outs)


def _ln_in_grad(dr1, dx_in):
    tm = _tile(SEQ, 512)

    def body(a_ref, b_ref, o_ref):
        o_ref[...] = DEEPNORM_ALPHA * a_ref[...] + b_ref[...]

    blk = pl.BlockSpec((tm, D_MODEL), lambda i: (i, 0))
    return pl.pallas_call(body, name="grad_x", grid=(SEQ // tm,), in_specs=[blk, blk], out_specs=blk,
                          out_shape=_sds((SEQ, D_MODEL)), compiler_params=_cparams(("parallel",)))(dr1, dx_in)
```

```python
import functools
import math

import jax
import jax.numpy as jnp
from jax import lax
from jax.experimental import pallas as pl
from jax.experimental.pallas import tpu as pltpu

F32 = jnp.float32
BF16 = jnp.bfloat16
MM_DTYPE = BF16

D_MODEL = 2048
SEQ = 2048
DEPTH = 4
D_FF = 5504
HEAD_DIM = 64
N_Q_HEADS = D_MODEL // 2 // HEAD_DIM
N_KV_HEADS = N_Q_HEADS // 4
ATTN_WIDTH = N_Q_HEADS * HEAD_DIM
KV_WIDTH = N_KV_HEADS * HEAD_DIM
ATTN_BLOCK = 128
ROPE_THETA = 10000.0
POOL_WINDOWS = (2, 4, 8, 16)
POOL_WIDTH = D_MODEL // 4
POOL_GROUP = POOL_WIDTH // len(POOL_WINDOWS)
SSM_WIDTH = D_MODEL // 4
SSM_GROUP = 16
SSM_N_GROUPS = SSM_WIDTH // SSM_GROUP
SSM_STATE = 64
SSM_CH = SSM_N_GROUPS * SSM_STATE
MIX_WIDTH = ATTN_WIDTH + POOL_WIDTH + SSM_WIDTH
IN_WIDTH = ATTN_WIDTH + 2 * KV_WIDTH + POOL_WIDTH + SSM_WIDTH
CONV_WIDTH = 3
LN_EPS = 1e-5
DEEPNORM_ALPHA = (2 * DEPTH) ** 0.25
ADAM_LR = 0.001
ADAM_B1 = 0.9
ADAM_B2 = 0.999
ADAM_EPS = 1e-08
ADAM_WD = 0.01
ADAM_STEP = 10

N_CHIPS = 4
N_DEV = 8
FS = 2 * D_FF // N_CHIPS
IN_S = IN_WIDTH // N_CHIPS
GLU_S = 2 * SSM_WIDTH // N_CHIPS
LANES = 128
SUBLANES = 8
SCAN_CW = 256
SCAN_NB = SSM_CH // SCAN_CW
VMEM_LIMIT = 56 * 1024 * 1024
COPY_BLOCK_BYTES = 6 * 1024 * 1024
NEG = -1e30

NN = (((1,), (0,)), ((), ()))
NT = (((1,), (1,)), ((), ()))
TN = (((0,), (0,)), ((), ()))
MESH = pl.DeviceIdType.MESH


def _tile(n, pref, mult=LANES):
    best = None
    for t in range(mult, min(n, pref) + 1, mult):
        if n % t == 0:
            best = t
    return n if best is None else best


def _cparams(sem):
    return pltpu.CompilerParams(dimension_semantics=sem, vmem_limit_bytes=VMEM_LIMIT)


def _sds(shape, dtype=F32):
    return jax.ShapeDtypeStruct(tuple(shape), dtype)


def _mm(name, a, b, out_shape, grid, a_spec, b_spec, o_spec, dims, acc_shape, out_dtype=F32):
    nk = grid[2]

    def body(a_ref, b_ref, o_ref, acc_ref):
        k = pl.program_id(2)

        @pl.when(k == 0)
        def _():
            acc_ref[...] = jnp.zeros_like(acc_ref)

        acc_ref[...] += lax.dot_general(a_ref[...].astype(MM_DTYPE), b_ref[...].astype(MM_DTYPE), dims,
                                        preferred_element_type=F32)

        @pl.when(k == nk - 1)
        def _():
            o_ref[...] = acc_ref[...].astype(o_ref.dtype)

    return pl.pallas_call(
        body, name=name, grid=grid, in_specs=[a_spec, b_spec], out_specs=o_spec,
        out_shape=_sds(out_shape, out_dtype), scratch_shapes=[pltpu.VMEM(acc_shape, F32)],
        compiler_params=_cparams(("parallel", "parallel", "arbitrary")))(a, b)


def _mm_nn(name, a, b, tm=2048, tn=512, tk=512, out_dtype=F32):
    m, kk = a.shape
    n = b.shape[1]
    tm, tn, tk = _tile(m, tm), _tile(n, tn), _tile(kk, tk)
    return _mm(name, a, b, (m, n), (m // tm, n // tn, kk // tk),
               pl.BlockSpec((tm, tk), lambda i, j, k: (i, k)), pl.BlockSpec((tk, tn), lambda i, j, k: (k, j)),
               pl.BlockSpec((tm, tn), lambda i, j, k: (i, j)), NN, (tm, tn), out_dtype)


def _mm_nt(name, a, b, tm=2048, tn=512, tk=512):
    m, kk = a.shape
    n = b.shape[0]
    tm, tn, tk = _tile(m, tm), _tile(n, tn), _tile(kk, tk)
    return _mm(name, a, b, (m, n), (m // tm, n // tn, kk // tk),
               pl.BlockSpec((tm, tk), lambda i, j, k: (i, k)), pl.BlockSpec((tn, tk), lambda i, j, k: (j, k)),
               pl.BlockSpec((tm, tn), lambda i, j, k: (i, j)), NT, (tm, tn))


def _mm_tn(name, a, b, tm=1024, tn=1024, ts=512):
    s, m = a.shape
    n = b.shape[1]
    tm, tn, ts = _tile(m, tm), _tile(n, tn), _tile(s, ts)
    return _mm(name, a, b, (m, n), (m // tm, n // tn, s // ts),
               pl.BlockSpec((ts, tm), lambda i, j, k: (k, i)), pl.BlockSpec((ts, tn), lambda i, j, k: (k, j)),
               pl.BlockSpec((tm, tn), lambda i, j, k: (i, j)), TN, (tm, tn))


def _mm_shard_cols(name, a, w, tm=2048, tk=512):
    m, kk = a.shape
    nj, _, c = w.shape
    tm, tk = _tile(m, tm), _tile(kk, tk)
    return _mm(name, a, w, (m, nj * c), (m // tm, nj, kk // tk),
               pl.BlockSpec((tm, tk), lambda i, j, k: (i, k)), pl.BlockSpec((None, tk, c), lambda i, j, k: (j, k, 0)),
               pl.BlockSpec((tm, c), lambda i, j, k: (i, j)), NN, (tm, c))


def _mm_shard_cols_nt(name, d, w, tm=2048, tn=512):
    m = d.shape[0]
    nj, n, c = w.shape
    tm, tn = _tile(m, tm), _tile(n, tn)
    return _mm(name, d, w, (m, n), (m // tm, n // tn, nj),
               pl.BlockSpec((tm, c), lambda i, j, k: (i, k)), pl.BlockSpec((None, tn, c), lambda i, j, k: (k, j, 0)),
               pl.BlockSpec((tm, tn), lambda i, j, k: (i, j)), NT, (tm, tn))


def _mm_shard_cols_tn(name, a, d, nj, tm=1024, ts=512):
    s, m = a.shape
    c = d.shape[1] // nj
    tm, ts = _tile(m, tm), _tile(s, ts)
    return _mm(name, a, d, (nj, m, c), (nj, m // tm, s // ts),
               pl.BlockSpec((ts, tm), lambda j, i, k: (k, i)), pl.BlockSpec((ts, c), lambda j, i, k: (k, j)),
               pl.BlockSpec((None, tm, c), lambda j, i, k: (j, i, 0)), TN, (tm, c))


def _ffn_up(x1, w_up, tm=1024, tk=512):
    s, d = x1.shape
    tm, tk = _tile(s, tm), _tile(d, tk)
    return _mm("ffn_up", x1, w_up, (N_CHIPS, s, FS), (s // tm, N_CHIPS, d // tk),
               pl.BlockSpec((tm, tk), lambda i, j, k: (i, k)), pl.BlockSpec((None, tk, FS), lambda i, j, k: (j, k, 0)),
               pl.BlockSpec((None, tm, FS), lambda i, j, k: (j, i, 0)), NN, (tm, FS))


def _ffn_down(act, w_down, tm=1024, tn=512):
    _, s, _ = act.shape
    d = w_down.shape[2]
    tm, tn = _tile(s, tm), _tile(d, tn)
    return _mm("ffn_down", act, w_down, (s, d), (s // tm, d // tn, 2),
               pl.BlockSpec((None, tm, FS), lambda i, j, k: (k, i, 0)), pl.BlockSpec((None, FS, tn), lambda i, j, k: (k, 0, j)),
               pl.BlockSpec((tm, tn), lambda i, j, k: (i, j)), NN, (tm, tn))


def _ffn_down_dact(df, w_down, tm=1024, tk=512):
    s, d = df.shape
    tm, tk = _tile(s, tm), _tile(d, tk)
    return _mm("ffn_down_dact", df, w_down, (2, s, FS), (s // tm, 2, d // tk),
               pl.BlockSpec((tm, tk), lambda i, j, k: (i, k)), pl.BlockSpec((None, FS, tk), lambda i, j, k: (j, 0, k)),
               pl.BlockSpec((None, tm, FS), lambda i, j, k: (j, i, 0)), NT, (tm, FS))


def _ffn_down_dw(act, df, tn=1024, ts=512):
    _, s, _ = act.shape
    d = df.shape[1]
    tn, ts = _tile(d, tn), _tile(s, ts)
    return _mm("ffn_down_dw", act, df, (2, FS, d), (2, d // tn, s // ts),
               pl.BlockSpec((None, ts, FS), lambda p, j, k: (p, k, 0)), pl.BlockSpec((ts, tn), lambda p, j, k: (k, j)),
               pl.BlockSpec((None, FS, tn), lambda p, j, k: (p, 0, j)), TN, (FS, tn))


def _ffn_up_dx(dh, w_up, tm=1024, tn=1024):
    s = dh.shape[2]
    d = w_up.shape[1]
    tm, tn = _tile(s, tm), _tile(d, tn)
    return _mm("ffn_up_dx", dh, w_up, (s, d), (s // tm, d // tn, N_CHIPS),
               pl.BlockSpec((None, None, tm, FS), lambda i, j, k: (k % 2, k // 2, i, 0)),
               pl.BlockSpec((None, tn, FS), lambda i, j, k: (k, j, 0)),
               pl.BlockSpec((tm, tn), lambda i, j, k: (i, j)), NT, (tm, tn))


def _ffn_up_dw(x1, dh, tm=1024, ts=512):
    s, d = x1.shape
    tm, ts = _tile(d, tm), _tile(s, ts)
    return _mm("ffn_up_dw", x1, dh, (N_CHIPS, d, FS), (N_CHIPS, d // tm, s // ts),
               pl.BlockSpec((ts, tm), lambda j, i, k: (k, i)),
               pl.BlockSpec((None, None, ts, FS), lambda j, i, k: (j % 2, j // 2, k, 0)),
               pl.BlockSpec((None, tm, FS), lambda j, i, k: (j, i, 0)), TN, (tm, FS))


def _rope_tables():
    half = HEAD_DIM // 2
    inv = ROPE_THETA ** (-jnp.arange(half, dtype=F32) / half)
    ang = jnp.arange(SEQ).astype(F32)[:, None] * inv[None, :]
    cos, sin = jnp.cos(ang), jnp.sin(ang)
    cos_t = jnp.tile(cos, (1, LANES // half))
    sin_t = jnp.tile(jnp.concatenate([-sin, sin], axis=1), (1, LANES // HEAD_DIM))
    return cos_t, sin_t


def _rotate_half(t):
    lane = lax.broadcasted_iota(jnp.int32, t.shape, 1)
    first = (lane % HEAD_DIM) < (HEAD_DIM // 2)
    return jnp.where(first, pltpu.roll(t, LANES - HEAD_DIM // 2, 1), pltpu.roll(t, HEAD_DIM // 2, 1))


def _rope(name, src, col_tile0, n_tiles, cos_t, sin_t, out_dtype):
    tm = _tile(SEQ, 512)

    def body(x_ref, c_ref, s_ref, o_ref):
        t = x_ref[...].astype(F32)
        o_ref[...] = (t * c_ref[...] + _rotate_half(t) * s_ref[...]).astype(o_ref.dtype)

    return pl.pallas_call(
        body, name=name, grid=(SEQ // tm, n_tiles),
        in_specs=[pl.BlockSpec((tm, LANES), lambda i, j: (i, j + col_tile0)),
                  pl.BlockSpec((tm, LANES), lambda i, j: (i, 0)), pl.BlockSpec((tm, LANES), lambda i, j: (i, 0))],
        out_specs=pl.BlockSpec((tm, LANES), lambda i, j: (i, j)),
        out_shape=_sds((SEQ, n_tiles * LANES), out_dtype),
        compiler_params=_cparams(("parallel", "parallel")))(src, cos_t, sin_t)


Q_TILES = ATTN_WIDTH // LANES
KV_TILES = KV_WIDTH // LANES
Q_PER_KV_TILE = Q_TILES // KV_TILES
HEADS_PER_KV_TILE = N_Q_HEADS // KV_TILES
K_TILE0 = ATTN_WIDTH // LANES
V_TILE0 = (ATTN_WIDTH + KV_WIDTH) // LANES
N_QBLK = SEQ // ATTN_BLOCK


def _dup_half(t, which):
    lane = lax.broadcasted_iota(jnp.int32, t.shape, 1)
    r = pltpu.roll(t, HEAD_DIM, 1)
    lo = lane < HEAD_DIM
    return jnp.where(lo, t, r) if which == 0 else jnp.where(lo, r, t)


def _attn_masks(n):
    row = lax.broadcasted_iota(jnp.int32, (ATTN_BLOCK, ATTN_BLOCK), 0)
    col = lax.broadcasted_iota(jnp.int32, (ATTN_BLOCK, ATTN_BLOCK), 1)
    return col <= row, (col > row) & (n > 0), col < HEAD_DIM


def _attn_probs(qm, k2c, k2p, cur_ok, prev_ok, sink):
    scale = HEAD_DIM ** -0.5
    sc = lax.dot_general(qm, k2c, NT, preferred_element_type=F32) * scale
    sp = lax.dot_general(qm, k2p, NT, preferred_element_type=F32) * scale
    sc = jnp.where(cur_ok, sc, NEG)
    sp = jnp.where(prev_ok, sp, NEG)
    m = jnp.maximum(jnp.maximum(sc.max(1, keepdims=True), sp.max(1, keepdims=True)), sink)
    pc, pp = jnp.exp(sc - m), jnp.exp(sp - m)
    esink = jnp.exp(sink - m)
    inv = 1.0 / (pc.sum(1, keepdims=True) + pp.sum(1, keepdims=True) + esink)
    return pc * inv, pp * inv, esink * inv


def _attn_specs():
    blk = (ATTN_BLOCK, LANES)
    wide = (ATTN_BLOCK, Q_PER_KV_TILE * LANES)
    prev = lambda n: jnp.maximum(n - 1, 0)
    q_spec = pl.BlockSpec(wide, lambda t, n: (n, t))
    kc = pl.BlockSpec(blk, lambda t, n: (n, K_TILE0 + t))
    kp = pl.BlockSpec(blk, lambda t, n: (prev(n), K_TILE0 + t))
    vc = pl.BlockSpec(blk, lambda t, n: (n, V_TILE0 + t))
    vp = pl.BlockSpec(blk, lambda t, n: (prev(n), V_TILE0 + t))
    return q_spec, kc, kp, vc, vp, pl.BlockSpec(memory_space=pltpu.SMEM)


def _attn_fwd(qk, h, sinks):
    q_spec, kc_s, kp_s, vc_s, vp_s, smem = _attn_specs()

    def body(sink_ref, q_ref, kc_ref, kp_ref, vc_ref, vp_ref, o_ref, ob_ref):
        t, n = pl.program_id(0), pl.program_id(1)
        cur_ok, prev_ok, lo = _attn_masks(n)
        kc, kp = kc_ref[...].astype(F32), kp_ref[...].astype(F32)
        vc, vp = vc_ref[...], vp_ref[...]
        for kvl in range(2):
            k2c, k2p = _dup_half(kc, kvl).astype(MM_DTYPE), _dup_half(kp, kvl).astype(MM_DTYPE)
            v2c, v2p = _dup_half(vc, kvl).astype(MM_DTYPE), _dup_half(vp, kvl).astype(MM_DTYPE)
            for a in (2 * kvl, 2 * kvl + 1):
                qt = q_ref[:, a * LANES:(a + 1) * LANES].astype(F32)
                outs = []
                for hs in range(2):
                    qm = jnp.where(lo == (hs == 0), qt, 0.0).astype(MM_DTYPE)
                    sink = sink_ref[t * HEADS_PER_KV_TILE + 2 * a + hs]
                    pc, pp, _ = _attn_probs(qm, k2c, k2p, cur_ok, prev_ok, sink)
                    outs.append(lax.dot_general(pc.astype(MM_DTYPE), v2c, NN, preferred_element_type=F32)
                                + lax.dot_general(pp.astype(MM_DTYPE), v2p, NN, preferred_element_type=F32))
                o = jnp.where(lo, outs[0], outs[1])
                o_ref[:, a * LANES:(a + 1) * LANES] = o
                ob_ref[:, a * LANES:(a + 1) * LANES] = o.astype(ob_ref.dtype)

    return pl.pallas_call(
        body, name="attn_fwd", grid=(KV_TILES, N_QBLK),
        in_specs=[smem, q_spec, kc_s, kp_s, vc_s, vp_s], out_specs=[q_spec, q_spec],
        out_shape=[_sds((SEQ, ATTN_WIDTH)), _sds((SEQ, ATTN_WIDTH), MM_DTYPE)],
        compiler_params=_cparams(("parallel", "parallel")))(sinks, qk, qk, qk, h, h)


def _attn_bwd(qk, h, sinks, y, dy, dy_tile0):
    q_spec, kc_s, kp_s, vc_s, vp_s, smem = _attn_specs()
    blk = (ATTN_BLOCK, LANES)
    wide = (ATTN_BLOCK, Q_PER_KV_TILE * LANES)
    kv_out = pl.BlockSpec(blk, lambda t, n: (n, t))
    dy_spec = pl.BlockSpec(wide, lambda t, n: (n, t + dy_tile0))

    def body(sink_ref, q_ref, kc_ref, kp_ref, vc_ref, vp_ref, y_ref, dy_ref,
             dq_ref, dkc_ref, dkp_ref, dvc_ref, dvp_ref, dsk_ref):
        t, n = pl.program_id(0), pl.program_id(1)
        cur_ok, prev_ok, lo = _attn_masks(n)
        scale = HEAD_DIM ** -0.5
        kc, kp = kc_ref[...].astype(F32), kp_ref[...].astype(F32)
        vc, vp = vc_ref[...], vp_ref[...]
        hrow = lax.broadcasted_iota(jnp.int32, (HEADS_PER_KV_TILE, LANES), 0)
        dsk = jnp.zeros((HEADS_PER_KV_TILE, LANES), F32)
        folded = []
        for kvl in range(2):
            k2c, k2p = _dup_half(kc, kvl).astype(MM_DTYPE), _dup_half(kp, kvl).astype(MM_DTYPE)
            v2c, v2p = _dup_half(vc, kvl).astype(MM_DTYPE), _dup_half(vp, kvl).astype(MM_DTYPE)
            acc = [jnp.zeros(blk, F32) for _ in range(4)]
            for a in (2 * kvl, 2 * kvl + 1):
                sl = slice(a * LANES, (a + 1) * LANES)
                qt = q_ref[:, sl].astype(F32)
                dot_, yt = dy_ref[:, sl], y_ref[:, sl]
                dqs = []
                for hs in range(2):
                    hm = lo == (hs == 0)
                    qm = jnp.where(hm, qt, 0.0).astype(MM_DTYPE)
                    dom = jnp.where(hm, dot_, 0.0).astype(MM_DTYPE)
                    hl = 2 * a + hs
                    sink = sink_ref[t * HEADS_PER_KV_TILE + hl]
                    pc, pp, psink = _attn_probs(qm, k2c, k2p, cur_ok, prev_ok, sink)
                    delta = jnp.sum(jnp.where(hm, dot_ * yt, 0.0), axis=1, keepdims=True)
                    dpc = lax.dot_general(dom, v2c, NT, preferred_element_type=F32)
                    dpp = lax.dot_general(dom, v2p, NT, preferred_element_type=F32)
                    dsc = (pc * (dpc - delta) * scale).astype(MM_DTYPE)
                    dsp = (pp * (dpp - delta) * scale).astype(MM_DTYPE)
                    dqs.append(lax.dot_general(dsc, k2c, NN, preferred_element_type=F32)
                               + lax.dot_general(dsp, k2p, NN, preferred_element_type=F32))
                    acc[0] += lax.dot_general(dsc, qm, TN, preferred_element_type=F32)
                    acc[1] += lax.dot_general(dsp, qm, TN, preferred_element_type=F32)
                    acc[2] += lax.dot_general(pc.astype(MM_DTYPE), dom, TN, preferred_element_type=F32)
                    acc[3] += lax.dot_general(pp.astype(MM_DTYPE), dom, TN, preferred_element_type=F32)
                    dsk = dsk + jnp.where(hrow == hl, -jnp.sum(psink * delta), 0.0)
                dq_ref[:, sl] = jnp.where(lo, dqs[0], dqs[1])
            folded.append([x + pltpu.roll(x, HEAD_DIM, 1) for x in acc])
        for o_ref, i in ((dkc_ref, 0), (dkp_ref, 1), (dvc_ref, 2), (dvp_ref, 3)):
            o_ref[...] = jnp.where(lo, folded[0][i], folded[1][i])

        @pl.when(n == 0)
        def _():
            dsk_ref[...] = jnp.zeros_like(dsk_ref)

        dsk_ref[...] += dsk

    kv_shape = _sds((SEQ, KV_WIDTH))
    return pl.pallas_call(
        body, name="attn_bwd", grid=(KV_TILES, N_QBLK),
        in_specs=[smem, q_spec, kc_s, kp_s, vc_s, vp_s, q_spec, dy_spec],
        out_specs=[q_spec, kv_out, kv_out, kv_out, kv_out,
                   pl.BlockSpec((None, HEADS_PER_KV_TILE, LANES), lambda t, n: (t, 0, 0))],
        out_shape=[_sds((SEQ, ATTN_WIDTH)), kv_shape, kv_shape, kv_shape, kv_shape,
                   _sds((KV_TILES, HEADS_PER_KV_TILE, LANES))],
        compiler_params=_cparams(("parallel", "arbitrary")))(sinks, qk, qk, qk, h, h, y, dy)


def _attn_dh(dq, dkc, dkp, dvc, dvp, cos_t, nsin_t):
    n_tiles = Q_TILES + 2 * KV_TILES
    nxt = lambda n: jnp.minimum(n + 1, N_QBLK - 1)

    def body(dq_ref, kc_ref, kp_ref, vc_ref, vp_ref, c_ref, s_ref, o_ref):
        has_next = pl.program_id(0) < N_QBLK - 1
        cos, sin = c_ref[...], s_ref[...]

        def unrope(t):
            return t * cos + _rotate_half(t) * sin

        for j in range(Q_TILES):
            sl = slice(j * LANES, (j + 1) * LANES)
            o_ref[:, sl] = unrope(dq_ref[:, sl]).astype(o_ref.dtype)
        for j in range(KV_TILES):
            sl = slice(j * LANES, (j + 1) * LANES)
            t = kc_ref[:, sl] + jnp.where(has_next, kp_ref[:, sl], 0.0)
            o_ref[:, ATTN_WIDTH + j * LANES:ATTN_WIDTH + (j + 1) * LANES] = unrope(t).astype(o_ref.dtype)
        o_ref[:, ATTN_WIDTH + KV_WIDTH:] = (vc_ref[...] + jnp.where(has_next, vp_ref[...], 0.0)).astype(o_ref.dtype)

    qb, kb, tb = (ATTN_BLOCK, ATTN_WIDTH), (ATTN_BLOCK, KV_WIDTH), (ATTN_BLOCK, LANES)
    return pl.pallas_call(
        body, name="attn_dh", grid=(N_QBLK,),
        in_specs=[pl.BlockSpec(qb, lambda n: (n, 0)),
                  pl.BlockSpec(kb, lambda n: (n, 0)), pl.BlockSpec(kb, lambda n: (nxt(n), 0)),
                  pl.BlockSpec(kb, lambda n: (n, 0)), pl.BlockSpec(kb, lambda n: (nxt(n), 0)),
                  pl.BlockSpec(tb, lambda n: (n, 0)), pl.BlockSpec(tb, lambda n: (n, 0))],
        out_specs=pl.BlockSpec((ATTN_BLOCK, n_tiles * LANES), lambda n: (n, 0)),
        out_shape=_sds((SEQ, n_tiles * LANES), MM_DTYPE),
        compiler_params=_cparams(("parallel",)))(dq, dkc, dkp, dvc, dvp, cos_t, nsin_t)


POOL_TILE0 = (ATTN_WIDTH + 2 * KV_WIDTH) // POOL_WIDTH


def _shift_rows(x, d, down):
    n = x.shape[0]
    row = lax.broadcasted_iota(jnp.int32, x.shape, 0)
    if down:
        return jnp.where(row >= d, pltpu.roll(x, d, 0), 0.0)
    return jnp.where(row < n - d, pltpu.roll(x, n - d, 0), 0.0)


def _window_sum(x, w, down):
    d = 1
    while d < w:
        x = x + _shift_rows(x, d, down)
        d *= 2
    return x


def _pool_z(u, w):
    t = lax.broadcasted_iota(jnp.int32, u.shape, 0).astype(F32)
    cnt = jnp.minimum(t + 1.0, float(w))
    return _window_sum(u, w, True) / cnt - u, cnt


def _pool_fwd(h, pool_w, pool_scale):
    def body(u_ref, w_ref, s_ref, o_ref):
        for gi, w in enumerate(POOL_WINDOWS):
            sl = slice(gi * POOL_GROUP, (gi + 1) * POOL_GROUP)
            z, _ = _pool_z(u_ref[:, sl], w)
            o_ref[:, sl] = (lax.dot_general(z.astype(MM_DTYPE), w_ref[gi].astype(MM_DTYPE), NN,
                                            preferred_element_type=F32) * s_ref[:, sl]).astype(o_ref.dtype)

    return pl.pallas_call(
        body, name="pool_fwd", grid=(1,),
        in_specs=[pl.BlockSpec((SEQ, POOL_WIDTH), lambda i: (0, POOL_TILE0)),
                  pl.BlockSpec(pool_w.shape, lambda i: (0, 0, 0)), pl.BlockSpec((1, POOL_WIDTH), lambda i: (0, 0))],
        out_specs=pl.BlockSpec((SEQ, POOL_WIDTH), lambda i: (0, 0)),
        out_shape=_sds((SEQ, POOL_WIDTH), MM_DTYPE), compiler_params=_cparams(("arbitrary",)))(h, pool_w, pool_scale)


def _pool_bwd(h, pool_w, pool_scale, dmix, dy_tile0):
    def body(u_ref, w_ref, s_ref, dy_ref, du_ref, dw_ref, ds_ref):
        for gi, w in enumerate(POOL_WINDOWS):
            sl = slice(gi * POOL_GROUP, (gi + 1) * POOL_GROUP)
            z, cnt = _pool_z(u_ref[:, sl], w)
            zb, wb = z.astype(MM_DTYPE), w_ref[gi].astype(MM_DTYPE)
            dy = dy_ref[:, sl]
            zp = lax.dot_general(zb, wb, NN, preferred_element_type=F32)
            ds_ref[:, sl] = jnp.sum(dy * zp, axis=0, keepdims=True)
            dyo = (dy * s_ref[:, sl]).astype(MM_DTYPE)
            dw_ref[gi] = lax.dot_general(zb, dyo, TN, preferred_element_type=F32)
            dz = lax.dot_general(dyo, wb, NT, preferred_element_type=F32)
            du_ref[:, sl] = (_window_sum(dz / cnt, w, False) - dz).astype(du_ref.dtype)

    return pl.pallas_call(
        body, name="pool_bwd", grid=(1,),
        in_specs=[pl.BlockSpec((SEQ, POOL_WIDTH), lambda i: (0, POOL_TILE0)),
                  pl.BlockSpec(pool_w.shape, lambda i: (0, 0, 0)), pl.BlockSpec((1, POOL_WIDTH), lambda i: (0, 0)),
                  pl.BlockSpec((SEQ, POOL_WIDTH), lambda i: (0, dy_tile0))],
        out_specs=[pl.BlockSpec((SEQ, POOL_WIDTH), lambda i: (0, 0)), pl.BlockSpec(pool_w.shape, lambda i: (0, 0, 0)),
                   pl.BlockSpec((1, POOL_WIDTH), lambda i: (0, 0))],
        out_shape=[_sds((SEQ, POOL_WIDTH), MM_DTYPE), _sds(pool_w.shape), _sds((1, POOL_WIDTH))],
        compiler_params=_cparams(("arbitrary",)))(h, pool_w, pool_scale, dmix)


def _ssm_discretize(lr, li, ldt, br, bi):
    dt = jnp.exp(ldt)
    mag = jnp.exp(lr * dt)
    ar, ai = mag * jnp.cos(li * dt), mag * jnp.sin(li * dt)
    nr, ni = ar - 1.0, ai
    den = lr * lr + li * li
    zr = (nr * lr + ni * li) / den
    zi = (ni * lr - nr * li) / den
    return ar, ai, zr * br - zi * bi, zr * bi + zi * br


def _ssm_prep(lr, li, ldt, br, bi):
    def body(lr_ref, li_ref, ldt_ref, br_ref, bi_ref, ar_ref, ai_ref, bbr_ref, bbi_ref):
        outs = _ssm_discretize(lr_ref[...], li_ref[...], ldt_ref[...], br_ref[...], bi_ref[...])
        for o, v in zip((ar_ref, ai_ref, bbr_ref, bbi_ref), outs):
            o[...] = v

    row, mat = _sds((1, SSM_CH)), _sds((SSM_GROUP, SSM_CH))
    return pl.pallas_call(body, name="ssm_prep", out_shape=[row, row, mat, mat])(lr, li, ldt, br, bi)


def _ssm_prep_bwd(lr, li, ldt, br, bi, dar8, dai8, dbbr, dbbi):
    def body(lr_ref, li_ref, ldt_ref, br_ref, bi_ref, dar_ref, dai_ref, dbbr_ref, dbbi_ref, *outs):
        args = (lr_ref[...], li_ref[...], ldt_ref[...], br_ref[...], bi_ref[...])
        _, vjp = jax.vjp(_ssm_discretize, *args)
        cot = (jnp.sum(dar_ref[...], axis=0, keepdims=True), jnp.sum(dai_ref[...], axis=0, keepdims=True),
               dbbr_ref[...], dbbi_ref[...])
        for o, v in zip(outs, vjp(cot)):
            o[...] = v

    row, mat = _sds((1, SSM_CH)), _sds((SSM_GROUP, SSM_CH))
    return pl.pallas_call(body, name="ssm_prep_bwd", out_shape=[row, row, row, mat, mat])(
        lr, li, ldt, br, bi, dar8, dai8, dbbr, dbbi)


def _scan_layout(re, im):
    r = re.shape[0]
    return jnp.stack([re.reshape(r, SCAN_NB, SCAN_CW), im.reshape(r, SCAN_NB, SCAN_CW)], axis=2).reshape(r, 2 * SSM_CH)


def _scan_unlayout(x):
    r = x.shape[0]
    x = x.reshape(r, SCAN_NB, 2, SCAN_CW)
    return x[:, :, 0].reshape(r, SSM_CH), x[:, :, 1].reshape(r, SSM_CH)


def _time_permute(u):
    s, c = u.shape
    return u.reshape(SUBLANES, s // SUBLANES, c).transpose(1, 0, 2).reshape(s, c)


def _time_unpermute(u):
    s, c = u.shape
    return u.reshape(s // SUBLANES, SUBLANES, c).transpose(1, 0, 2).reshape(s, c)


def _ssm_scan(name, a_vec, x, reverse, s_prev=None):
    nsteps = SEQ // SUBLANES
    cw = SCAN_CW
    with_da = s_prev is not None

    def body(a_ref, x_ref, *rest):
        if with_da:
            s_ref, o_ref, da_ref = rest
        else:
            o_ref, = rest
        ar = jnp.broadcast_to(a_ref[:, :cw], (SUBLANES, cw))
        ai = jnp.broadcast_to(a_ref[:, cw:], (SUBLANES, cw))
        seg = lax.broadcasted_iota(jnp.int32, (SUBLANES, cw), 0)

        def toward(v):
            if reverse:
                return jnp.where(seg < SUBLANES - 1, pltpu.roll(v, SUBLANES - 1, 0), 0.0)
            return jnp.where(seg >= 1, pltpu.roll(v, 1, 0), 0.0)

        def rows(j):
            jj = nsteps - 1 - j if reverse else j
            return pl.ds(pl.multiple_of(jj * SUBLANES, SUBLANES), SUBLANES)

        def cmul(pr, pi, qr, qi):
            return pr * qr - pi * qi, pr * qi + pi * qr

        def local(j, c):
            sr, si = c
            r = rows(j)
            mr, mi = cmul(ar, ai, sr, si)
            return mr + x_ref[r, :cw], mi + x_ref[r, cw:]

        zero = jnp.zeros((SUBLANES, cw), F32)
        fr, fi = lax.fori_loop(0, nsteps, local, (zero, zero))

        def power(_, c):
            return cmul(ar, ai, *c)

        pr, pi = lax.fori_loop(0, nsteps - 1, power, (ar, ai))
        tr, ti = fr, fi
        for _ in range(SUBLANES - 1):
            mr, mi = cmul(pr, pi, toward(tr), toward(ti))
            tr, ti = fr + mr, fi + mi
        init = (toward(tr), toward(ti))

        def full(j, c):
            r = rows(j)
            if with_da:
                sr, si, dar, dai = c
            else:
                sr, si = c
            mr, mi = cmul(ar, ai, sr, si)
            sr, si = mr + x_ref[r, :cw], mi + x_ref[r, cw:]
            o_ref[r, :cw] = sr
            o_ref[r, cw:] = si
            if not with_da:
                return sr, si
            jj = nsteps - 1 - j
            rp = pl.ds(pl.multiple_of(jnp.maximum(jj - 1, 0) * SUBLANES, SUBLANES), SUBLANES)
            last = pl.ds((nsteps - 1) * SUBLANES, SUBLANES)
            first = jj == 0
            spr = jnp.where(first, jnp.where(seg >= 1, pltpu.roll(s_ref[last, :cw], 1, 0), 0.0), s_ref[rp, :cw])
            spi = jnp.where(first, jnp.where(seg >= 1, pltpu.roll(s_ref[last, cw:], 1, 0), 0.0), s_ref[rp, cw:])
            return sr, si, dar + sr * spr + si * spi, dai + si * spr - sr * spi

        if with_da:
            _, _, dar, dai = lax.fori_loop(0, nsteps, full, init + (zero, zero))
            da_ref[:, :cw] = dar
            da_ref[:, cw:] = dai
        else:
            lax.fori_loop(0, nsteps, full, init)

    blk = pl.BlockSpec((SEQ, 2 * cw), lambda b: (0, b))
    a_spec = pl.BlockSpec((1, 2 * cw), lambda b: (0, b))
    in_specs, args = [a_spec, blk], [a_vec, x]
    out_specs, out_shape = blk, _sds((SEQ, 2 * SSM_CH))
    if with_da:
        in_specs, args = in_specs + [blk], args + [s_prev]
        out_specs = [blk, pl.BlockSpec((SUBLANES, 2 * cw), lambda b: (0, b))]
        out_shape = [out_shape, _sds((SUBLANES, 2 * SSM_CH))]
    return pl.pallas_call(body, name=name, grid=(SCAN_NB,), in_specs=in_specs, out_specs=out_specs,
                          out_shape=out_shape, compiler_params=_cparams(("parallel",)))(*args)


def _ssm_gelu(yp, up, dvec):
    tm = _tile(SEQ, 512)

    def body(y_ref, u_ref, d_ref, yf_ref, g_ref):
        yf = y_ref[...] + d_ref[...] * u_ref[...]
        yf_ref[...] = yf
        g_ref[...] = jax.nn.gelu(yf).astype(g_ref.dtype)

    blk = pl.BlockSpec((tm, SSM_WIDTH), lambda i: (i, 0))
    row = pl.BlockSpec((1, SSM_WIDTH), lambda i: (0, 0))
    return pl.pallas_call(body, name="ssm_gelu", grid=(SEQ // tm,), in_specs=[blk, blk, row], out_specs=[blk, blk],
                          out_shape=[_sds((SEQ, SSM_WIDTH)), _sds((SEQ, SSM_WIDTH), MM_DTYPE)],
                          compiler_params=_cparams(("parallel",)))(yp, up, dvec)


def _ssm_gelu_bwd(yf, dgy, up, dvec):
    tm = _tile(SEQ, 512)

    def body(yf_ref, dg_ref, u_ref, d_ref, dyf_ref, du_ref, dd_ref):
        _, vjp = jax.vjp(jax.nn.gelu, yf_ref[...])
        dyf, = vjp(dg_ref[...])
        dyf_ref[...] = dyf.astype(dyf_ref.dtype)
        du_ref[...] = d_ref[...] * dyf

        @pl.when(pl.program_id(0) == 0)
        def _():
            dd_ref[...] = jnp.zeros_like(dd_ref)

        dd_ref[...] += jnp.sum(dyf * u_ref[...], axis=0, keepdims=True)

    blk = pl.BlockSpec((tm, SSM_WIDTH), lambda i: (i, 0))
    row = pl.BlockSpec((1, SSM_WIDTH), lambda i: (0, 0))
    return pl.pallas_call(body, name="ssm_gelu_bwd", grid=(SEQ // tm,), in_specs=[blk, blk, blk, row],
                          out_specs=[blk, blk, row],
                          out_shape=[_sds((SEQ, SSM_WIDTH), MM_DTYPE), _sds((SEQ, SSM_WIDTH)), _sds((1, SSM_WIDTH))],
                          compiler_params=_cparams(("arbitrary",)))(yf, dgy, up, dvec)


def _glu(ab):
    return ab[:, :SSM_WIDTH] * jax.nn.sigmoid(ab[:, SSM_WIDTH:])


def _ssm_glu(ab):
    tm = _tile(SEQ, 512)

    def body(ab_ref, o_ref):
        o_ref[...] = _glu(ab_ref[...]).astype(o_ref.dtype)

    return pl.pallas_call(body, name="ssm_glu", grid=(SEQ // tm,),
                          in_specs=[pl.BlockSpec((tm, 2 * SSM_WIDTH), lambda i: (i, 0))],
                          out_specs=pl.BlockSpec((tm, SSM_WIDTH), lambda i: (i, 0)),
                          out_shape=_sds((SEQ, SSM_WIDTH), MM_DTYPE), compiler_params=_cparams(("parallel",)))(ab)


def _ssm_glu_bwd(ab, dout):
    tm = _tile(SEQ, 512)

    def body(ab_ref, do_ref, dab_ref):
        _, vjp = jax.vjp(_glu, ab_ref[...])
        dab, = vjp(do_ref[...])
        dab_ref[...] = dab.astype(dab_ref.dtype)

    return pl.pallas_call(body, name="ssm_glu_bwd", grid=(SEQ // tm,),
                          in_specs=[pl.BlockSpec((tm, 2 * SSM_WIDTH), lambda i: (i, 0)),
                                    pl.BlockSpec((tm, SSM_WIDTH), lambda i: (i, 0))],
                          out_specs=pl.BlockSpec((tm, 2 * SSM_WIDTH), lambda i: (i, 0)),
                          out_shape=_sds((SEQ, 2 * SSM_WIDTH), MM_DTYPE), compiler_params=_cparams(("parallel",)))(ab, dout)


def _add2(name, a, b, out_dtype):
    tm = _tile(a.shape[0], 512)

    def body(a_ref, b_ref, o_ref):
        o_ref[...] = (a_ref[...] + b_ref[...]).astype(o_ref.dtype)

    blk = pl.BlockSpec((tm, a.shape[1]), lambda i: (i, 0))
    return pl.pallas_call(body, name=name, grid=(a.shape[0] // tm,), in_specs=[blk, blk], out_specs=blk,
                          out_shape=_sds(a.shape, out_dtype), compiler_params=_cparams(("parallel",)))(a, b)


def _layer_norm(r, g, b):
    mu = r.mean(-1, keepdims=True)
    var = jnp.square(r - mu).mean(-1, keepdims=True)
    return (r - mu) * lax.rsqrt(var + LN_EPS) * g + b


def _ln_fwd(name, x, y, g, b):
    tm = _tile(SEQ, 256)

    def body(x_ref, y_ref, g_ref, b_ref, r_ref, o_ref, ob_ref):
        r = DEEPNORM_ALPHA * x_ref[...] + y_ref[...]
        r_ref[...] = r
        o = _layer_norm(r, g_ref[...], b_ref[...])
        o_ref[...] = o
        ob_ref[...] = o.astype(ob_ref.dtype)

    blk = pl.BlockSpec((tm, D_MODEL), lambda i: (i, 0))
    row = pl.BlockSpec((1, D_MODEL), lambda i: (0, 0))
    return pl.pallas_call(body, name=name, grid=(SEQ // tm,), in_specs=[blk, blk, row, row], out_specs=[blk, blk, blk],
                          out_shape=[_sds((SEQ, D_MODEL))] * 2 + [_sds((SEQ, D_MODEL), MM_DTYPE)],
                          compiler_params=_cparams(("parallel",)))(x, y, g, b)


def _ln_bwd(name, r, g, b, da, db=None):
    tm = _tile(SEQ, 256)
    two = db is not None

    def body(r_ref, g_ref, b_ref, da_ref, *rest):
        if two:
            db_ref, dr_ref, drb_ref, dg_ref, dbeta_ref = rest
            dout = DEEPNORM_ALPHA * da_ref[...] + db_ref[...]
        else:
            dr_ref, drb_ref, dg_ref, dbeta_ref = rest
            dout = da_ref[...]
        _, vjp = jax.vjp(_layer_norm, r_ref[...], g_ref[...], b_ref[...])
        dr, dg, dbeta = vjp(dout)
        dr_ref[...] = dr
        drb_ref[...] = dr.astype(drb_ref.dtype)

        @pl.when(pl.program_id(0) == 0)
        def _():
            dg_ref[...] = jnp.zeros_like(dg_ref)
            dbeta_ref[...] = jnp.zeros_like(dbeta_ref)

        dg_ref[...] += dg
        dbeta_ref[...] += dbeta

    blk = pl.BlockSpec((tm, D_MODEL), lambda i: (i, 0))
    row = pl.BlockSpec((1, D_MODEL), lambda i: (0, 0))
    args = [r, g, b, da] + ([db] if two else [])
    return pl.pallas_call(body, name=name, grid=(SEQ // tm,), in_specs=[blk, row, row, blk] + ([blk] if two else []),
                          out_specs=[blk, blk, row, row],
                          out_shape=[_sds((SEQ, D_MODEL)), _sds((SEQ, D_MODEL), MM_DTYPE), _sds((1, D_MODEL)), _sds((1, D_MODEL))],
                          compiler_params=_cparams(("arbitrary",)))(*args)


FFN_TM = 128
HALO = SUBLANES


def _conv_taps(cur, halo):
    row = lax.broadcasted_iota(jnp.int32, cur.shape, 0)
    h1 = jnp.where(row == 0, halo[HALO - 1:HALO, :], pltpu.roll(cur, 1, 0))
    h2 = jnp.where(row == 0, halo[HALO - 2:HALO - 1, :], jnp.where(row == 1, halo[HALO - 1:HALO, :], pltpu.roll(cur, 2, 0)))
    return h1, h2


def _conv_fwd(cur, halo, w_ref, b_ref):
    h1, h2 = _conv_taps(cur, halo)
    return b_ref[...] + h2 * w_ref[0:1, :] + h1 * w_ref[1:2, :] + cur * w_ref[2:3, :], h1, h2


def _gate(val, gate):
    return jax.nn.silu(gate) * val


def _ffn_specs(tm):
    nb = tm // HALO
    cur = lambda off: pl.BlockSpec((None, tm, FS), lambda p, i: (p + off, i, 0))
    halo = lambda off: pl.BlockSpec((None, HALO, FS), lambda p, i: (p + off, jnp.maximum(i * nb - 1, 0), 0))
    cw = lambda off: pl.BlockSpec((None, CONV_WIDTH, FS), lambda p, i: (p + off, 0, 0))
    cb = lambda off: pl.BlockSpec((None, 1, FS), lambda p, i: (p + off, 0, 0))
    return cur, halo, cw, cb


def _ffn_act(hf, conv_w, conv_b):
    tm = _tile(SEQ, FFN_TM, SUBLANES)
    cur, halo, cw, cb = _ffn_specs(tm)

    def body(v_ref, vh_ref, g_ref, gh_ref, wv_ref, wg_ref, bv_ref, bg_ref, o_ref):
        live = pl.program_id(1) > 0
        vh = jnp.where(live, vh_ref[...], 0.0)
        gh = jnp.where(live, gh_ref[...], 0.0)
        val, _, _ = _conv_fwd(v_ref[...], vh, wv_ref, bv_ref)
        gate, _, _ = _conv_fwd(g_ref[...], gh, wg_ref, bg_ref)
        o_ref[...] = _gate(val, gate).astype(o_ref.dtype)

    return pl.pallas_call(
        body, name="ffn_act", grid=(2, SEQ // tm),
        in_specs=[cur(0), halo(0), cur(2), halo(2), cw(0), cw(2), cb(0), cb(2)],
        out_specs=pl.BlockSpec((None, tm, FS), lambda p, i: (p, i, 0)),
        out_shape=_sds((2, SEQ, FS), MM_DTYPE), compiler_params=_cparams(("parallel", "parallel")))(
            hf, hf, hf, hf, conv_w, conv_w, conv_b, conv_b)


def _ffn_act_bwd(hf, conv_w, conv_b, dact):
    tm = _tile(SEQ, FFN_TM, SUBLANES)
    cur, halo, cw, cb = _ffn_specs(tm)

    def body(v_ref, vh_ref, g_ref, gh_ref, wv_ref, wg_ref, bv_ref, bg_ref, da_ref, dhc_ref, dw_ref, dbias_ref):
        dv_ref, dg_ref = dhc_ref.at[0], dhc_ref.at[1]
        dwv_ref, dwg_ref = dw_ref.at[0], dw_ref.at[1]
        dbv_ref, dbg_ref = dbias_ref.at[0], dbias_ref.at[1]
        i = pl.program_id(1)
        live = i > 0
        vh = jnp.where(live, vh_ref[...], 0.0)
        gh = jnp.where(live, gh_ref[...], 0.0)
        vcur, gcur = v_ref[...], g_ref[...]
        val, v1, v2 = _conv_fwd(vcur, vh, wv_ref, bv_ref)
        gate, g1, g2 = _conv_fwd(gcur, gh, wg_ref, bg_ref)
        _, vjp = jax.vjp(_gate, val, gate)
        dval, dgate = vjp(da_ref[...])
        dv_ref[...] = dval
        dg_ref[...] = dgate

        @pl.when(i == 0)
        def _():
            dw_ref[...] = jnp.zeros_like(dw_ref)
            dbias_ref[...] = jnp.zeros_like(dbias_ref)

        for d, taps, dwk_ref, dbk_ref in ((dval, (v2, v1, vcur), dwv_ref, dbv_ref), (dgate, (g2, g1, gcur), dwg_ref, dbg_ref)):
            for k in range(CONV_WIDTH):
                dwk_ref[k:k + 1, :] += jnp.sum(d * taps[k], axis=0, keepdims=True)
            dbk_ref[...] += jnp.sum(d, axis=0, keepdims=True)

    return pl.pallas_call(
        body, name="ffn_act_bwd", grid=(2, SEQ // tm),
        in_specs=[cur(0), halo(0), cur(2), halo(2), cw(0), cw(2), cb(0), cb(2),
                  pl.BlockSpec((None, tm, FS), lambda p, i: (p, i, 0))],
        out_specs=[pl.BlockSpec((None, 2, tm, FS), lambda p, i: (p, 0, i, 0)),
                   pl.BlockSpec((None, 2, CONV_WIDTH, FS), lambda p, i: (p, 0, 0, 0)),
                   pl.BlockSpec((None, 2, 1, FS), lambda p, i: (p, 0, 0, 0))],
        out_shape=[_sds((2, 2, SEQ, FS)), _sds((2, 2, CONV_WIDTH, FS)), _sds((2, 2, 1, FS))],
        compiler_params=_cparams(("parallel", "arbitrary")))(hf, hf, hf, hf, conv_w, conv_w, conv_b, conv_b, dact)


def _conv_bwd_input(dhc, conv_w):
    tm = _tile(SEQ, FFN_TM, SUBLANES)
    nb = tm // HALO
    nblk = SEQ // tm

    def body(d_ref, nx_ref, w_ref, o_ref):
        cur = d_ref[...]
        nxt = jnp.where(pl.program_id(2) < nblk - 1, nx_ref[...], 0.0)
        row = lax.broadcasted_iota(jnp.int32, cur.shape, 0)
        d1 = jnp.where(row == tm - 1, nxt[0:1, :], pltpu.roll(cur, tm - 1, 0))
        d2 = jnp.where(row == tm - 1, nxt[1:2, :], jnp.where(row == tm - 2, nxt[0:1, :], pltpu.roll(cur, tm - 2, 0)))
        o_ref[...] = (cur * w_ref[2:3, :] + d1 * w_ref[1:2, :] + d2 * w_ref[0:1, :]).astype(o_ref.dtype)

    blk = pl.BlockSpec((None, None, tm, FS), lambda p, kd, i: (p, kd, i, 0))
    return pl.pallas_call(
        body, name="conv_bwd_input", grid=(2, 2, nblk),
        in_specs=[blk, pl.BlockSpec((None, None, HALO, FS), lambda p, kd, i: (p, kd, jnp.minimum((i + 1) * nb, SEQ // HALO - 1), 0)),
                  pl.BlockSpec((None, CONV_WIDTH, FS), lambda p, kd, i: (2 * kd + p, 0, 0))],
        out_specs=blk, out_shape=_sds((2, 2, SEQ, FS), MM_DTYPE),
        compiler_params=_cparams(("parallel", "parallel", "parallel")))(dhc, dhc, conv_w)


def _loss(y, target):
    tm = _tile(SEQ, 256)

    def body(y_ref, t_ref, dy_ref, l_ref):
        err = y_ref[...] - t_ref[...]
        dy_ref[...] = err * (1.0 / D_MODEL)

        @pl.when(pl.program_id(0) == 0)
        def _():
            l_ref[...] = jnp.zeros_like(l_ref)

        l_ref[...] += 0.5 * jnp.sum(jnp.mean(jnp.square(err), axis=-1))

    blk = pl.BlockSpec((tm, D_MODEL), lambda i: (i, 0))
    return pl.pallas_call(body, name="loss", grid=(SEQ // tm,), in_specs=[blk, blk],
                          out_specs=[blk, pl.BlockSpec((SUBLANES, LANES), lambda i: (0, 0))],
                          out_shape=[_sds((SEQ, D_MODEL)), _sds((SUBLANES, LANES))],
                          compiler_params=_cparams(("arbitrary",)))(y, target)


ADAM_BLOCK_BYTES = 1 << 20


def _adamw_math(w, g, m, v):
    nm = ADAM_B1 * m + (1.0 - ADAM_B1) * g
    nv = ADAM_B2 * v + (1.0 - ADAM_B2) * jnp.square(g)
    m_hat = nm / (1.0 - ADAM_B1 ** ADAM_STEP)
    v_hat = nv / (1.0 - ADAM_B2 ** ADAM_STEP)
    return -ADAM_LR * (m_hat / (jnp.sqrt(v_hat) + ADAM_EPS) + ADAM_WD * w), nm, nv


def _adamw(name, w, g, m, v):
    r, c = w.shape
    tr = _tile(r, max(SUBLANES, ADAM_BLOCK_BYTES // (4 * c)), SUBLANES)

    def body(w_ref, g_ref, m_ref, v_ref, d_ref, nm_ref, nv_ref):
        d_ref[...], nm_ref[...], nv_ref[...] = _adamw_math(w_ref[...], g_ref[...], m_ref[...], v_ref[...])

    blk = pl.BlockSpec((tr, c), lambda i: (i, 0))
    return pl.pallas_call(body, name=name, grid=(r // tr,), in_specs=[blk] * 4, out_specs=[blk] * 3,
                          out_shape=[_sds((r, c))] * 3, compiler_params=_cparams(("parallel",)))(w, g, m, v)


def _adamw_big(name, l, c_idx, w, m, v, g_own, g_got, prev):
    depth, _, r, c = w.shape
    tr = _tile(r, max(SUBLANES, ADAM_BLOCK_BYTES // (4 * c)), SUBLANES)

    def body(c_ref, w_ref, m_ref, v_ref, own_ref, got_ref, *rest):
        g_ref, d_ref, nm_ref, nv_ref = rest[-4:]
        g = jnp.where(pl.program_id(0) == c_ref[0], own_ref[...], got_ref[...])
        g_ref[...] = g
        d_ref[...], nm_ref[...], nv_ref[...] = _adamw_math(w_ref[...], g, m_ref[...], v_ref[...])

    stacked = pl.BlockSpec((None, None, tr, c), lambda h, i, cr: (l, h, i, 0))
    own = pl.BlockSpec((tr, c), lambda h, i, cr: (jnp.where(h == cr[0], i, 0), 0))
    got = pl.BlockSpec((tr, c), lambda h, i, cr: (jnp.where(h == cr[0], 0, i), 0))
    grid_spec = pltpu.PrefetchScalarGridSpec(
        num_scalar_prefetch=1, grid=(2, r // tr),
        in_specs=[stacked] * 3 + [own, got] + ([ANY] * 4 if prev else []), out_specs=[stacked] * 4)
    return pl.pallas_call(
        body, name=name, grid_spec=grid_spec, out_shape=[_sds((depth, 2, r, c))] * 4,
        input_output_aliases={6 + k: k for k in range(4)} if prev else {},
        compiler_params=_cparams(("arbitrary", "arbitrary")))(c_idx, w, m, v, g_own, g_got, *(prev or ()))


ANY = pl.BlockSpec(memory_space=pl.ANY)


def _place():
    x, y, c = lax.axis_index("x"), lax.axis_index("y"), lax.axis_index("c")
    chips = [(1 - x, y), (x, 1 - y), (1 - x, 1 - y)]
    return x, y, c, chips


def _cast_place(name, w, l, me_idx, out_dtype):
    _, _, r, c = w.shape
    tr = _tile(r, max(2 * SUBLANES, COPY_BLOCK_BYTES // (4 * c)), 2 * SUBLANES)

    def body(me_ref, w_ref, o_ref):
        o_ref[...] = w_ref[...].astype(o_ref.dtype)

    grid_spec = pltpu.PrefetchScalarGridSpec(
        num_scalar_prefetch=1, grid=(2, r // tr),
        in_specs=[pl.BlockSpec((None, None, tr, c), lambda h, i, me: (l, h, i, 0))],
        out_specs=pl.BlockSpec((None, None, tr, c), lambda h, i, me: (me[0], h, i, 0)))
    return pl.pallas_call(body, name=name, grid_spec=grid_spec, out_shape=_sds((N_CHIPS, 2, r, c), out_dtype),
                          compiler_params=_cparams(("parallel", "parallel")))(me_idx, w)


def _gather_weights(bufs):
    n = len(bufs)

    def body(*refs):
        outs = refs[n:2 * n]
        send_sems, recv_sems = refs[2 * n:]
        x, y, c, chips = _place()
        me, sib = 2 * x + y, (x, y, 1 - c)

        def copy(i, k, chip_idx, half, to):
            blk = outs[i].at[chip_idx, half]
            return pltpu.make_async_remote_copy(src_ref=blk, dst_ref=blk, send_sem=send_sems.at[6 * i + k],
                                                recv_sem=recv_sems.at[6 * i + k], device_id=to, device_id_type=MESH)

        first = []
        for i in range(n):
            for k, chip in enumerate(chips):
                cp = copy(i, k, me, c, (*chip, c))
                cp.start()
                first.append(cp)
        passed = []
        for i in range(n):
            for k, chip in enumerate(chips):
                copy(i, k, 2 * chip[0] + chip[1], c, (x, y, c)).wait_recv()
                cp = copy(i, 3 + k, 2 * chip[0] + chip[1], c, sib)
                cp.start()
                passed.append(cp)
        for i in range(n):
            for k, chip in enumerate(chips):
                copy(i, 3 + k, 2 * chip[0] + chip[1], 1 - c, (x, y, c)).wait_recv()
        for cp in first + passed:
            cp.wait_send()

    return pl.pallas_call(
        body, name="gather_weights", in_specs=[ANY] * n, out_specs=[ANY] * n,
        out_shape=[_sds(a.shape, a.dtype) for a in bufs], input_output_aliases={i: i for i in range(n)},
        scratch_shapes=[pltpu.SemaphoreType.DMA((6 * n,)), pltpu.SemaphoreType.DMA((6 * n,))],
    )(*bufs)


def _pair_swap(grads):
    n = len(grads)

    def body(*refs):
        ins, outs = refs[:n], refs[n:2 * n]
        send_sems, recv_sems = refs[2 * n:]
        x, y, c, _ = _place()
        cps = []
        for i in range(n):
            cp = pltpu.make_async_remote_copy(src_ref=ins[i].at[:, 1 - c], dst_ref=outs[i], send_sem=send_sems.at[i],
                                              recv_sem=recv_sems.at[i], device_id=(x, y, 1 - c), device_id_type=MESH)
            cp.start()
            cps.append(cp)
        for cp in cps:
            cp.wait()

    return pl.pallas_call(
        body, name="grad_pair_swap", in_specs=[ANY] * n, out_specs=[ANY] * n,
        out_shape=[_sds((g.shape[0],) + g.shape[2:], g.dtype) for g in grads],
        scratch_shapes=[pltpu.SemaphoreType.DMA((n,)), pltpu.SemaphoreType.DMA((n,))])(*grads)


def _pair_add(name, g, got, cm_idx):
    nk, _, r, c = g.shape
    tr = _tile(r, max(2 * SUBLANES, COPY_BLOCK_BYTES // (4 * c)), 2 * SUBLANES)

    def body(cm_ref, g_ref, x_ref, o_ref, land_ref):
        s = (g_ref[...] + x_ref[...]).astype(o_ref.dtype)
        o_ref[...] = s

        @pl.when(pl.program_id(1) == cm_ref[1])
        def _():
            land_ref[...] = s

    grid_spec = pltpu.PrefetchScalarGridSpec(
        num_scalar_prefetch=1, grid=(r // tr, nk),
        in_specs=[pl.BlockSpec((None, None, tr, c), lambda i, k, cm: (k, cm[0], i, 0)),
                  pl.BlockSpec((None, tr, c), lambda i, k, cm: (k, i, 0))],
        out_specs=[pl.BlockSpec((None, tr, c), lambda i, k, cm: (k, i, 0)),
                   pl.BlockSpec((None, tr, c), lambda i, k, cm: (cm[1], i, 0))])
    return pl.pallas_call(body, name=name, grid_spec=grid_spec, out_shape=[_sds((nk, r, c), BF16)] * 2,
                          compiler_params=_cparams(("parallel", "arbitrary")))(cm_idx, g, got)


def _chip_scatter(parts, lands):
    n = len(parts)

    def body(*refs):
        ins, outs = refs[:n], refs[2 * n:3 * n]
        send_sems, recv_sems = refs[3 * n:]
        x, y, c, chips = _place()
        me = 2 * x + y
        cps = []
        for i in range(n):
            for k, chip in enumerate(chips):
                cp = pltpu.make_async_remote_copy(
                    src_ref=ins[i].at[2 * chip[0] + chip[1]], dst_ref=outs[i].at[me], send_sem=send_sems.at[3 * i + k],
                    recv_sem=recv_sems.at[3 * i + k], device_id=(*chip, c), device_id_type=MESH)
                cp.start()
                cps.append(cp)
        for cp in cps:
            cp.wait_send()
        for i in range(n):
            for k, chip in enumerate(chips):
                blk = outs[i].at[2 * chip[0] + chip[1]]
                pltpu.make_async_remote_copy(src_ref=blk, dst_ref=blk, send_sem=send_sems.at[3 * i + k],
                                             recv_sem=recv_sems.at[3 * i + k], device_id=(*chip, c),
                                             device_id_type=MESH).wait_recv()

    return pl.pallas_call(
        body, name="grad_chip_scatter", in_specs=[ANY] * (2 * n), out_specs=[ANY] * n,
        out_shape=[_sds(p.shape, p.dtype) for p in lands], input_output_aliases={n + i: i for i in range(n)},
        scratch_shapes=[pltpu.SemaphoreType.DMA((3 * n,)), pltpu.SemaphoreType.DMA((3 * n,))])(*parts, *lands)


def _sum_leading(name, x, out_dtype=F32):
    nk, r, c = x.shape
    tr = _tile(r, max(2 * SUBLANES, COPY_BLOCK_BYTES // (nk * c * x.dtype.itemsize)), 2 * SUBLANES)

    def body(x_ref, o_ref):
        acc = x_ref[0].astype(F32)
        for k in range(1, nk):
            acc = acc + x_ref[k].astype(F32)
        o_ref[...] = acc.astype(o_ref.dtype)

    return pl.pallas_call(body, name=name, grid=(r // tr,), in_specs=[pl.BlockSpec((nk, tr, c), lambda i: (0, i, 0))],
                          out_specs=pl.BlockSpec((tr, c), lambda i: (i, 0)), out_shape=_sds((r, c), out_dtype),
                          compiler_params=_cparams(("parallel",)))(x)


def _pair_exchange(halves):
    n = len(halves)

    def body(*refs):
        ins, outs = refs[:n], refs[n:2 * n]
        send_sems, recv_sems = refs[2 * n:]
        x, y, c, _ = _place()
        cps = []
        for i in range(n):
            cp = pltpu.make_async_remote_copy(src_ref=ins[i], dst_ref=outs[i], send_sem=send_sems.at[i],
                                              recv_sem=recv_sems.at[i], device_id=(x, y, 1 - c), device_id_type=MESH)
            cp.start()
            cps.append(cp)
        for cp in cps:
            cp.wait()

    return pl.pallas_call(
        body, name="grad_pair_exchange", in_specs=[ANY] * n, out_specs=[ANY] * n,
        out_shape=[_sds(h.shape, h.dtype) for h in halves],
        scratch_shapes=[pltpu.SemaphoreType.DMA((n,)), pltpu.SemaphoreType.DMA((n,))])(*halves)


def _gather_all(part):
    def body(x_ref, out_ref, send_sems, recv_sems, local_sem):
        x, y, c, chips = _place()
        me, sibling = (x, y, c), (x, y, 1 - c)

        def rows(px, py, pc):
            return out_ref.at[4 * px + 2 * py + pc]

        def copy(k, block, to, src=None):
            return pltpu.make_async_remote_copy(src_ref=rows(*block) if src is None else src, dst_ref=rows(*block),
                                                send_sem=send_sems.at[k], recv_sem=recv_sems.at[k], device_id=to,
                                                device_id_type=MESH)

        mine = pltpu.make_async_copy(x_ref, rows(*me), local_sem)
        mine.start()
        first = [copy(0, me, sibling, src=x_ref)]
        first += [copy(1 + j, me, (*chip, c), src=x_ref) for j, chip in enumerate(chips)]
        for cp in first:
            cp.start()
        passed = [copy(4 + j, (*chip, c), sibling) for j, chip in enumerate(chips)]
        for j, chip in enumerate(chips):
            copy(1 + j, (*chip, c), me).wait_recv()
            passed[j].start()
        copy(0, sibling, me).wait_recv()
        for j, chip in enumerate(chips):
            copy(4 + j, (*chip, 1 - c), me).wait_recv()
        for cp in first + passed:
            cp.wait_send()
        mine.wait()

    return pl.pallas_call(
        body, name="gather_small_grads", in_specs=[ANY], out_specs=ANY, out_shape=_sds((N_DEV,) + part.shape, part.dtype),
        scratch_shapes=[pltpu.SemaphoreType.DMA((7,)), pltpu.SemaphoreType.DMA((7,)), pltpu.SemaphoreType.DMA])(part)


SMALL = ("attn_sinks", "pool_w", "pool_scale", "ssm_lam_re", "ssm_lam_im", "ssm_log_dt", "ssm_b_re", "ssm_b_im",
         "ssm_c_re", "ssm_c_im", "ssm_d", "ln1_g", "ln1_b", "ffn_conv_b", "ln2_g", "ln2_b")
BIG = ("w_in", "ssm_glu_w", "w_out", "ffn_w_up", "ffn_conv_w", "ffn_w_down")
ALL_W = ("w_in", "attn_sinks", "pool_w", "pool_scale", "ssm_lam_re", "ssm_lam_im", "ssm_log_dt", "ssm_b_re", "ssm_b_im",
         "ssm_c_re", "ssm_c_im", "ssm_d", "ssm_glu_w", "w_out", "ln1_g", "ln1_b", "ffn_w_up", "ffn_conv_w", "ffn_conv_b",
         "ffn_w_down", "ln2_g", "ln2_b")
PACK_UNIT = SUBLANES * LANES


def _padded(n):
    return -(-n // PACK_UNIT) * PACK_UNIT


def _pack(arrs):
    cols = []
    for name in SMALL:
        a = arrs[name].reshape(DEPTH, -1)
        cols.append(jnp.pad(a, ((0, 0), (0, _padded(a.shape[1]) - a.shape[1]))))
    return jnp.concatenate(cols, axis=1).reshape(-1, LANES)


def _unpack(packed, shapes):
    flat = packed.reshape(DEPTH, -1)
    out, off = {}, 0
    for name in SMALL:
        n = math.prod(shapes[name][1:])
        out[name] = flat[:, off:off + n].reshape(shapes[name])
        off += _padded(n)
    return out


def _b_rows(b):
    return b.transpose(2, 0, 1).reshape(SSM_GROUP, SSM_CH)


def _b_unrows(b):
    return b.reshape(SSM_GROUP, SSM_N_GROUPS, SSM_STATE).transpose(1, 2, 0)


def _block_diag_in(bb):
    eye = jnp.eye(SSM_N_GROUPS, dtype=F32)
    b3 = bb.reshape(SSM_GROUP, SSM_N_GROUPS, SSM_STATE)
    return jnp.einsum("hgp,gk->ghkp", b3, eye).reshape(SSM_WIDTH, SSM_CH)


def _block_diag_in_t(full):
    f4 = full.reshape(SSM_N_GROUPS, SSM_GROUP, SSM_N_GROUPS, SSM_STATE)
    return jnp.einsum("ghgp->hgp", f4).reshape(SSM_GROUP, SSM_CH)


def _block_diag_out(cc):
    eye = jnp.eye(SSM_N_GROUPS, dtype=F32)
    return jnp.einsum("ghp,gk->gpkh", cc, eye).reshape(SSM_CH, SSM_WIDTH)


def _block_diag_out_t(full):
    f4 = full.reshape(SSM_N_GROUPS, SSM_STATE, SSM_N_GROUPS, SSM_GROUP)
    return jnp.einsum("gpgh->ghp", f4)


def _rows_layout(re, im):
    n = re.shape[1]
    return jnp.stack([re.reshape(SCAN_NB, SCAN_CW, n), im.reshape(SCAN_NB, SCAN_CW, n)], axis=1).reshape(2 * SSM_CH, n)


def _rows_unlayout(x):
    n = x.shape[1]
    x = x.reshape(SCAN_NB, 2, SCAN_CW, n)
    return x[:, 0].reshape(SSM_CH, n), x[:, 1].reshape(SSM_CH, n)


H_POOL0 = ATTN_WIDTH + 2 * KV_WIDTH
H_SSM0 = H_POOL0 + POOL_WIDTH


def _ssm_params(p):
    lr = p["ssm_lam_re"].reshape(1, SSM_CH)
    li = p["ssm_lam_im"].reshape(1, SSM_CH)
    ldt = jnp.repeat(p["ssm_log_dt"], SSM_STATE).reshape(1, SSM_CH)
    return lr, li, ldt, _b_rows(p["ssm_b_re"]), _b_rows(p["ssm_b_im"])


def _layer_fwd(x, xb, p, wg, rope_t):
    cos_t, sin_t = rope_t
    h = _mm_shard_cols("in_proj", xb, wg["w_in"])
    qk = _rope("rope_fwd", h, 0, Q_TILES + KV_TILES, cos_t, sin_t, MM_DTYPE)
    y_attn, y_attn_b = _attn_fwd(qk, h, p["attn_sinks"])
    y_pool = _pool_fwd(h, p["pool_w"], p["pool_scale"].reshape(1, POOL_WIDTH))
    ssm_in = _ssm_params(p)
    ar, ai, bbr, bbi = _ssm_prep(*ssm_in)
    bd = _scan_layout(_block_diag_in(bbr), _block_diag_in(bbi)).astype(MM_DTYPE)
    cc = _rows_layout(_block_diag_out(p["ssm_c_re"]), -_block_diag_out(p["ssm_c_im"])).astype(MM_DTYPE)
    dvec = p["ssm_d"].reshape(1, SSM_WIDTH)
    up = _time_permute(h[:, H_SSM0:])
    xx = _mm_nn("ssm_bu", up, bd, tn=1024)
    ss = _ssm_scan("ssm_scan_fwd", _scan_layout(ar, ai), xx, False)
    yp = _mm_nn("ssm_cs", ss, cc, tk=1024)
    yf, gy = _ssm_gelu(yp, up, dvec)
    ab = _mm_shard_cols("ssm_glu_proj", gy, wg["ssm_glu_w"])
    y_ssm = _time_unpermute(_ssm_glu(ab))
    mix = jnp.concatenate([y_attn_b, y_pool, y_ssm], axis=1)
    mixo = _mm_nn("out_proj", mix, wg["w_out"].reshape(MIX_WIDTH, D_MODEL))
    r1, x1, x1b = _ln_fwd("ln1_fwd", x, mixo, p["ln1_g"].reshape(1, D_MODEL), p["ln1_b"].reshape(1, D_MODEL))
    hf = _ffn_up(x1b, wg["ffn_w_up"])
    conv_b = p["ffn_conv_b"].reshape(N_CHIPS, 1, FS)
    act = _ffn_act(hf, wg["ffn_conv_w"], conv_b)
    f = _ffn_down(act, wg["ffn_w_down"].reshape(2, FS, D_MODEL))
    r2, x2, x2b = _ln_fwd("ln2_fwd", x1, f, p["ln2_g"].reshape(1, D_MODEL), p["ln2_b"].reshape(1, D_MODEL))
    saved = dict(xb=xb, h=h, qk=qk, y_attn=y_attn, ssm_in=ssm_in, ar=ar, ai=ai, bd=bd, cc=cc, dvec=dvec, up=up, ss=ss, yf=yf,
                 gy=gy, ab=ab, mix=mix, r1=r1, x1b=x1b, hf=hf, conv_b=conv_b, act=act, r2=r2)
    return x2, x2b, saved


def _layer_bwd(da, db, p, wg, sv, rope_t):
    cos_t, sin_t = rope_t
    small = {}
    dr2, dr2b, dg, dbeta = _ln_bwd("ln2_bwd" if db is not None else "ln2_bwd_last", sv["r2"], p["ln2_g"].reshape(1, D_MODEL),
                                   p["ln2_b"].reshape(1, D_MODEL), da, db)
    small["ln2_g"], small["ln2_b"] = dg, dbeta
    w_down = wg["ffn_w_down"].reshape(2, FS, D_MODEL)
    dact = _ffn_down_dact(dr2b, w_down)
    dw_down = _ffn_down_dw(sv["act"], dr2b)
    dhc, dcw, dcb = _ffn_act_bwd(sv["hf"], wg["ffn_conv_w"], sv["conv_b"], dact)
    dconv_w = dcw.transpose(1, 0, 2, 3).reshape(N_CHIPS, CONV_WIDTH, FS)
    small["ffn_conv_b"] = dcb.transpose(1, 0, 2, 3)
    dh_ffn = _conv_bwd_input(dhc, wg["ffn_conv_w"])
    dx1_ffn = _ffn_up_dx(dh_ffn, wg["ffn_w_up"])
    dw_up = _ffn_up_dw(sv["x1b"], dh_ffn)
    dr1, dr1b, dg, dbeta = _ln_bwd("ln1_bwd", sv["r1"], p["ln1_g"].reshape(1, D_MODEL), p["ln1_b"].reshape(1, D_MODEL), dr2,
                                   dx1_ffn)
    small["ln1_g"], small["ln1_b"] = dg, dbeta
    w_out = wg["w_out"].reshape(MIX_WIDTH, D_MODEL)
    dmix = _mm_nt("out_proj_dx", dr1b, w_out)
    dw_out = _mm_tn("out_proj_dw", sv["mix"], dr1b)
    dq, dkc, dkp, dvc, dvp, dsk = _attn_bwd(sv["qk"], sv["h"], p["attn_sinks"], sv["y_attn"], dmix, 0)
    small["attn_sinks"] = dsk[:, :, 0]
    dh_attn = _attn_dh(dq, dkc, dkp, dvc, dvp, cos_t, -sin_t)
    dh_pool, dpw, dps = _pool_bwd(sv["h"], p["pool_w"], p["pool_scale"].reshape(1, POOL_WIDTH), dmix, ATTN_WIDTH // POOL_WIDTH)
    small["pool_w"], small["pool_scale"] = dpw, dps
    dout_p = _time_permute(dmix[:, ATTN_WIDTH + POOL_WIDTH:])
    dab = _ssm_glu_bwd(sv["ab"], dout_p)
    dgy = _mm_shard_cols_nt("ssm_glu_dx", dab, wg["ssm_glu_w"])
    dw_glu = _mm_shard_cols_tn("ssm_glu_dw", sv["gy"], dab, N_CHIPS)
    dyf, du1, dd = _ssm_gelu_bwd(sv["yf"], dgy, sv["up"], sv["dvec"])
    small["ssm_d"] = dd
    dss = _mm_nt("ssm_cs_dx", dyf, sv["cc"], tn=1024)
    dcc = _mm_tn("ssm_cs_dw", sv["ss"], dyf, tm=1024)
    dcre, dcim = _rows_unlayout(dcc)
    small["ssm_c_re"], small["ssm_c_im"] = _block_diag_out_t(dcre), -_block_diag_out_t(dcim)
    gg, da8 = _ssm_scan("ssm_scan_bwd", _scan_layout(sv["ar"], -sv["ai"]), dss, True, sv["ss"])
    du2 = _mm_nt("ssm_bu_dx", gg, sv["bd"], tk=1024)
    dbd = _mm_tn("ssm_bu_dw", sv["up"], gg, tn=1024)
    dbdr, dbdi = _scan_unlayout(dbd)
    dar8, dai8 = _scan_unlayout(da8)
    dlr, dli, dldt, dbr, dbi = _ssm_prep_bwd(*sv["ssm_in"], dar8, dai8, _block_diag_in_t(dbdr), _block_diag_in_t(dbdi))
    small["ssm_lam_re"], small["ssm_lam_im"] = dlr, dli
    small["ssm_log_dt"] = dldt.reshape(SSM_N_GROUPS, SSM_STATE).sum(axis=1)
    small["ssm_b_re"], small["ssm_b_im"] = _b_unrows(dbr), _b_unrows(dbi)
    dh_ssm = _time_unpermute(_add2("ssm_du", du1, du2, MM_DTYPE))
    dh = jnp.concatenate([dh_attn, dh_pool, dh_ssm], axis=1)
    dx_in = _mm_shard_cols_nt("in_proj_dx", dh, wg["w_in"])
    dw_in = _mm_shard_cols_tn("in_proj_dw", sv["xb"], dh, N_CHIPS)
    big = {"w_in": dw_in, "ssm_glu_w": dw_glu, "w_out": dw_out.reshape(N_CHIPS, MIX_WIDTH // N_CHIPS, D_MODEL),
           "ffn_w_up": dw_up, "ffn_conv_w": dconv_w, "ffn_w_down": dw_down.reshape(N_CHIPS, FS // 2, D_MODEL)}
    return dr1, dx_in, big, small


CONV_PAD = 2 * SUBLANES


def _halved(name, a):
    if name == "ffn_conv_w":
        a = jnp.pad(a, ((0, 0), (0, CONV_PAD - CONV_WIDTH), (0, 0)))
    return a.reshape(a.shape[0], 2, a.shape[1] // 2, a.shape[2])


def _unhalved(name, a):
    a = a.reshape(a.shape[:-3] + (2 * a.shape[-2], a.shape[-1]))
    return a[..., :CONV_WIDTH, :] if name == "ffn_conv_w" else a


def _reduce_big(big, cm_idx):
    g4 = [_halved(name, big[name]) for name in BIG]
    got = _pair_swap(g4)
    parts, lands = zip(*[_pair_add("grad_pair_add", g, x, cm_idx) for g, x in zip(g4, got)])
    recv = _chip_scatter(parts, lands)
    halves = [_sum_leading("grad_chip_sum", r) for r in recv]
    return halves, _pair_exchange(halves)


def kernel(x, w_in, attn_sinks, pool_w, pool_scale, ssm_lam_re, ssm_lam_im, ssm_log_dt, ssm_b_re, ssm_b_im, ssm_c_re, ssm_c_im, ssm_d, ssm_glu_w, w_out, ln1_g, ln1_b, ffn_w_up, ffn_conv_w, ffn_conv_b, ffn_w_down, ln2_g, ln2_b, loss_target, m_w_in, m_attn_sinks, m_pool_w, m_pool_scale, m_ssm_lam_re, m_ssm_lam_im, m_ssm_log_dt, m_ssm_b_re, m_ssm_b_im, m_ssm_c_re, m_ssm_c_im, m_ssm_d, m_ssm_glu_w, m_w_out, m_ln1_g, m_ln1_b, m_ffn_w_up, m_ffn_conv_w, m_ffn_conv_b, m_ffn_w_down, m_ln2_g, m_ln2_b, v_w_in, v_attn_sinks, v_pool_w, v_pool_scale, v_ssm_lam_re, v_ssm_lam_im, v_ssm_log_dt, v_ssm_b_re, v_ssm_b_im, v_ssm_c_re, v_ssm_c_im, v_ssm_d, v_ssm_glu_w, v_w_out, v_ln1_g, v_ln1_b, v_ffn_w_up, v_ffn_conv_w, v_ffn_conv_b, v_ffn_w_down, v_ln2_g, v_ln2_b):
    w = dict(w_in=w_in, attn_sinks=attn_sinks, pool_w=pool_w, pool_scale=pool_scale, ssm_lam_re=ssm_lam_re,
             ssm_lam_im=ssm_lam_im, ssm_log_dt=ssm_log_dt, ssm_b_re=ssm_b_re, ssm_b_im=ssm_b_im, ssm_c_re=ssm_c_re,
             ssm_c_im=ssm_c_im, ssm_d=ssm_d, ssm_glu_w=ssm_glu_w, w_out=w_out, ln1_g=ln1_g, ln1_b=ln1_b, ffn_w_up=ffn_w_up,
             ffn_conv_w=ffn_conv_w, ffn_conv_b=ffn_conv_b, ffn_w_down=ffn_w_down, ln2_g=ln2_g, ln2_b=ln2_b)
    m = dict(w_in=m_w_in, attn_sinks=m_attn_sinks, pool_w=m_pool_w, pool_scale=m_pool_scale, ssm_lam_re=m_ssm_lam_re,
             ssm_lam_im=m_ssm_lam_im, ssm_log_dt=m_ssm_log_dt, ssm_b_re=m_ssm_b_re, ssm_b_im=m_ssm_b_im, ssm_c_re=m_ssm_c_re,
             ssm_c_im=m_ssm_c_im, ssm_d=m_ssm_d, ssm_glu_w=m_ssm_glu_w, w_out=m_w_out, ln1_g=m_ln1_g, ln1_b=m_ln1_b,
             ffn_w_up=m_ffn_w_up, ffn_conv_w=m_ffn_conv_w, ffn_conv_b=m_ffn_conv_b, ffn_w_down=m_ffn_w_down, ln2_g=m_ln2_g,
             ln2_b=m_ln2_b)
    v = dict(w_in=v_w_in, attn_sinks=v_attn_sinks, pool_w=v_pool_w, pool_scale=v_pool_scale, ssm_lam_re=v_ssm_lam_re,
             ssm_lam_im=v_ssm_lam_im, ssm_log_dt=v_ssm_log_dt, ssm_b_re=v_ssm_b_re, ssm_b_im=v_ssm_b_im, ssm_c_re=v_ssm_c_re,
             ssm_c_im=v_ssm_c_im, ssm_d=v_ssm_d, ssm_glu_w=v_ssm_glu_w, w_out=v_w_out, ln1_g=v_ln1_g, ln1_b=v_ln1_b,
             ffn_w_up=v_ffn_w_up, ffn_conv_w=v_ffn_conv_w, ffn_conv_b=v_ffn_conv_b, ffn_w_down=v_ffn_w_down, ln2_g=v_ln2_g,
             ln2_b=v_ln2_b)
    c_pos = lax.axis_index("c").astype(jnp.int32)
    chip = (2 * lax.axis_index("x") + lax.axis_index("y")).astype(jnp.int32)
    c_idx, chip_idx, cm_idx = c_pos.reshape(1), chip.reshape(1), jnp.stack([c_pos, chip])
    rope_t = _rope_tables()
    xs = x.reshape(SEQ, D_MODEL)
    xb = xs.astype(MM_DTYPE)
    wh, mh, vh = ({n: _halved(n, t[n]) for n in BIG} for t in (w, m, v))

    gathered, saved = [], []
    for l in range(DEPTH):
        bufs = [_cast_place("place_" + n, wh[n], l, chip_idx, F32 if n == "ffn_conv_w" else MM_DTYPE) for n in BIG]
        wg = {n: _unhalved(n, g) for n, g in zip(BIG, _gather_weights(bufs))}
        p = {n: w[n][l] for n in SMALL}
        xs, xb, sv = _layer_fwd(xs, xb, p, wg, rope_t)
        gathered.append(wg)
        saved.append(sv)
    dy, loss_tile = _loss(xs, loss_target.reshape(SEQ, D_MODEL))
    loss = lax.psum(loss_tile[0, 0], ("x", "y", "c"))

    big_out = {n: None for n in BIG}
    small_g = {n: [None] * DEPTH for n in SMALL}
    da, db = dy, None
    for l in reversed(range(DEPTH)):
        p = {n: w[n][l] for n in SMALL}
        da, db, big, small = _layer_bwd(da, db, p, gathered[l], saved[l], rope_t)
        for n in SMALL:
            small_g[n][l] = small[n].reshape(w[n].shape[1:])
        for n, own, got in zip(BIG, *_reduce_big(big, cm_idx)):
            big_out[n] = _adamw_big("adamw_" + n, l, c_idx, wh[n], mh[n], vh[n], own, got, big_out[n])
    grad_x = _ln_in_grad(da, db).reshape(x.shape)

    shapes = {n: w[n].shape for n in SMALL}
    part = _pack({n: jnp.stack(small_g[n]) for n in SMALL})
    g_small = _sum_leading("small_grad_sum", _gather_all(part))
    upd = _adamw("adamw_small", _pack(w), g_small, _pack(m), _pack(v))
    small_out = [_unpack(a, shapes) for a in (g_small,) + tuple(upd)]

    outs = [loss, grad_x]
    for kind in range(4):
        for n in ALL_W:
            if n in SMALL:
                outs.append(small_out[kind][n])
            else:
                outs.append(_unhalved(n, big_out[n][kind]))
    return tuple(outs)


def _ln_in_grad(dr1, dx_in):
    tm = _tile(SEQ, 512)

    def body(a_ref, b_ref, o_ref):
        o_ref[...] = DEEPNORM_ALPHA * a_ref[...] + b_ref[...]

    blk = pl.BlockSpec((tm, D_MODEL), lambda i: (i, 0))
    return pl.pallas_call(body, name="grad_x", grid=(SEQ // tm,), in_specs=[blk, blk], out_specs=blk,
                          out_shape=_sds((SEQ, D_MODEL)), compiler_params=_cparams(("parallel",)))(dr1, dx_in)
```

```python
import functools
import math

import jax
import jax.numpy as jnp
from jax import lax
from jax.experimental import pallas as pl
from jax.experimental.pallas import tpu as pltpu

F32 = jnp.float32
BF16 = jnp.bfloat16
MM_DTYPE = BF16

D_MODEL = 2048
SEQ = 2048
DEPTH = 4
D_FF = 5504
HEAD_DIM = 64
N_Q_HEADS = D_MODEL // 2 // HEAD_DIM
N_KV_HEADS = N_Q_HEADS // 4
ATTN_WIDTH = N_Q_HEADS * HEAD_DIM
KV_WIDTH = N_KV_HEADS * HEAD_DIM
ATTN_BLOCK = 128
ROPE_THETA = 10000.0
POOL_WINDOWS = (2, 4, 8, 16)
POOL_WIDTH = D_MODEL // 4
POOL_GROUP = POOL_WIDTH // len(POOL_WINDOWS)
SSM_WIDTH = D_MODEL // 4
SSM_GROUP = 16
SSM_N_GROUPS = SSM_WIDTH // SSM_GROUP
SSM_STATE = 64
SSM_CH = SSM_N_GROUPS * SSM_STATE
MIX_WIDTH = ATTN_WIDTH + POOL_WIDTH + SSM_WIDTH
IN_WIDTH = ATTN_WIDTH + 2 * KV_WIDTH + POOL_WIDTH + SSM_WIDTH
CONV_WIDTH = 3
LN_EPS = 1e-5
DEEPNORM_ALPHA = (2 * DEPTH) ** 0.25
ADAM_LR = 0.001
ADAM_B1 = 0.9
ADAM_B2 = 0.999
ADAM_EPS = 1e-08
ADAM_WD = 0.01
ADAM_STEP = 10

N_CHIPS = 4
N_DEV = 8
FS = 2 * D_FF // N_CHIPS
IN_S = IN_WIDTH // N_CHIPS
GLU_S = 2 * SSM_WIDTH // N_CHIPS
LANES = 128
SUBLANES = 8
SCAN_CW = 256
SCAN_NB = SSM_CH // SCAN_CW
VMEM_LIMIT = 56 * 1024 * 1024
COPY_BLOCK_BYTES = 6 * 1024 * 1024
NEG = -1e30

NN = (((1,), (0,)), ((), ()))
NT = (((1,), (1,)), ((), ()))
TN = (((0,), (0,)), ((), ()))
MESH = pl.DeviceIdType.MESH


def _tile(n, pref, mult=LANES):
    best = None
    for t in range(mult, min(n, pref) + 1, mult):
        if n % t == 0:
            best = t
    return n if best is None else best


def _cparams(sem):
    return pltpu.CompilerParams(dimension_semantics=sem, vmem_limit_bytes=VMEM_LIMIT)


def _sds(shape, dtype=F32):
    return jax.ShapeDtypeStruct(tuple(shape), dtype)


def _mm(name, a, b, out_shape, grid, a_spec, b_spec, o_spec, dims, acc_shape, out_dtype=F32, dep=None):
    nk = grid[2]
    deps = [] if dep is None else [dep]

    def body(a_ref, b_ref, *rest):
        o_ref, acc_ref = rest[-2:]
        k = pl.program_id(2)

        @pl.when(k == 0)
        def _():
            acc_ref[...] = jnp.zeros_like(acc_ref)

        acc_ref[...] += lax.dot_general(a_ref[...].astype(MM_DTYPE), b_ref[...].astype(MM_DTYPE), dims,
                                        preferred_element_type=F32)

        @pl.when(k == nk - 1)
        def _():
            o_ref[...] = acc_ref[...].astype(o_ref.dtype)

    return pl.pallas_call(
        body, name=name, grid=grid, in_specs=[a_spec, b_spec] + [ANY] * len(deps), out_specs=o_spec,
        out_shape=_sds(out_shape, out_dtype), scratch_shapes=[pltpu.VMEM(acc_shape, F32)],
        compiler_params=_cparams(("parallel", "parallel", "arbitrary")))(a, b, *deps)


def _mm_nn(name, a, b, tm=2048, tn=512, tk=512, out_dtype=F32):
    m, kk = a.shape
    n = b.shape[1]
    tm, tn, tk = _tile(m, tm), _tile(n, tn), _tile(kk, tk)
    return _mm(name, a, b, (m, n), (m // tm, n // tn, kk // tk),
               pl.BlockSpec((tm, tk), lambda i, j, k: (i, k)), pl.BlockSpec((tk, tn), lambda i, j, k: (k, j)),
               pl.BlockSpec((tm, tn), lambda i, j, k: (i, j)), NN, (tm, tn), out_dtype)


def _mm_nt(name, a, b, tm=2048, tn=512, tk=512):
    m, kk = a.shape
    n = b.shape[0]
    tm, tn, tk = _tile(m, tm), _tile(n, tn), _tile(kk, tk)
    return _mm(name, a, b, (m, n), (m // tm, n // tn, kk // tk),
               pl.BlockSpec((tm, tk), lambda i, j, k: (i, k)), pl.BlockSpec((tn, tk), lambda i, j, k: (j, k)),
               pl.BlockSpec((tm, tn), lambda i, j, k: (i, j)), NT, (tm, tn))


def _mm_tn(name, a, b, tm=1024, tn=1024, ts=512):
    s, m = a.shape
    n = b.shape[1]
    tm, tn, ts = _tile(m, tm), _tile(n, tn), _tile(s, ts)
    return _mm(name, a, b, (m, n), (m // tm, n // tn, s // ts),
               pl.BlockSpec((ts, tm), lambda i, j, k: (k, i)), pl.BlockSpec((ts, tn), lambda i, j, k: (k, j)),
               pl.BlockSpec((tm, tn), lambda i, j, k: (i, j)), TN, (tm, tn))


def _mm_shard_cols(name, a, w, tm=2048, tk=512, dep=None):
    m, kk = a.shape
    nj, _, c = w.shape
    tm, tk = _tile(m, tm), _tile(kk, tk)
    return _mm(name, a, w, (m, nj * c), (m // tm, nj, kk // tk),
               pl.BlockSpec((tm, tk), lambda i, j, k: (i, k)), pl.BlockSpec((None, tk, c), lambda i, j, k: (j, k, 0)),
               pl.BlockSpec((tm, c), lambda i, j, k: (i, j)), NN, (tm, c), dep=dep)


def _mm_shard_cols_nt(name, d, w, tm=2048, tn=512, dep=None):
    m = d.shape[0]
    nj, n, c = w.shape
    tm, tn = _tile(m, tm), _tile(n, tn)
    return _mm(name, d, w, (m, n), (m // tm, n // tn, nj),
               pl.BlockSpec((tm, c), lambda i, j, k: (i, k)), pl.BlockSpec((None, tn, c), lambda i, j, k: (k, j, 0)),
               pl.BlockSpec((tm, tn), lambda i, j, k: (i, j)), NT, (tm, tn), dep=dep)


def _mm_shard_cols_tn(name, a, d, nj, tm=1024, ts=512):
    s, m = a.shape
    c = d.shape[1] // nj
    tm, ts = _tile(m, tm), _tile(s, ts)
    return _mm(name, a, d, (nj, m, c), (nj, m // tm, s // ts),
               pl.BlockSpec((ts, tm), lambda j, i, k: (k, i)), pl.BlockSpec((ts, c), lambda j, i, k: (k, j)),
               pl.BlockSpec((None, tm, c), lambda j, i, k: (j, i, 0)), TN, (tm, c))


def _ffn_up(x1, w_up, tm=1024, tk=512, dep=None):
    s, d = x1.shape
    tm, tk = _tile(s, tm), _tile(d, tk)
    return _mm("ffn_up", x1, w_up, (N_CHIPS, s, FS), (s // tm, N_CHIPS, d // tk),
               pl.BlockSpec((tm, tk), lambda i, j, k: (i, k)), pl.BlockSpec((None, tk, FS), lambda i, j, k: (j, k, 0)),
               pl.BlockSpec((None, tm, FS), lambda i, j, k: (j, i, 0)), NN, (tm, FS), dep=dep)


def _ffn_down(act, w_down, tm=1024, tn=512):
    _, s, _ = act.shape
    d = w_down.shape[2]
    tm, tn = _tile(s, tm), _tile(d, tn)
    return _mm("ffn_down", act, w_down, (s, d), (s // tm, d // tn, 2),
               pl.BlockSpec((None, tm, FS), lambda i, j, k: (k, i, 0)), pl.BlockSpec((None, FS, tn), lambda i, j, k: (k, 0, j)),
               pl.BlockSpec((tm, tn), lambda i, j, k: (i, j)), NN, (tm, tn))


def _ffn_down_dact(df, w_down, tm=1024, tk=512):
    s, d = df.shape
    tm, tk = _tile(s, tm), _tile(d, tk)
    return _mm("ffn_down_dact", df, w_down, (2, s, FS), (s // tm, 2, d // tk),
               pl.BlockSpec((tm, tk), lambda i, j, k: (i, k)), pl.BlockSpec((None, FS, tk), lambda i, j, k: (j, 0, k)),
               pl.BlockSpec((None, tm, FS), lambda i, j, k: (j, i, 0)), NT, (tm, FS))


def _ffn_down_dw(act, df, tn=1024, ts=512):
    _, s, _ = act.shape
    d = df.shape[1]
    tn, ts = _tile(d, tn), _tile(s, ts)
    return _mm("ffn_down_dw", act, df, (2, FS, d), (2, d // tn, s // ts),
               pl.BlockSpec((None, ts, FS), lambda p, j, k: (p, k, 0)), pl.BlockSpec((ts, tn), lambda p, j, k: (k, j)),
               pl.BlockSpec((None, FS, tn), lambda p, j, k: (p, 0, j)), TN, (FS, tn))


def _ffn_up_dx(dh, w_up, tm=1024, tn=1024):
    s = dh.shape[2]
    d = w_up.shape[1]
    tm, tn = _tile(s, tm), _tile(d, tn)
    return _mm("ffn_up_dx", dh, w_up, (s, d), (s // tm, d // tn, N_CHIPS),
               pl.BlockSpec((None, None, tm, FS), lambda i, j, k: (k % 2, k // 2, i, 0)),
               pl.BlockSpec((None, tn, FS), lambda i, j, k: (k, j, 0)),
               pl.BlockSpec((tm, tn), lambda i, j, k: (i, j)), NT, (tm, tn))


def _ffn_up_dw(x1, dh, tm=1024, ts=512):
    s, d = x1.shape
    tm, ts = _tile(d, tm), _tile(s, ts)
    return _mm("ffn_up_dw", x1, dh, (N_CHIPS, d, FS), (N_CHIPS, d // tm, s // ts),
               pl.BlockSpec((ts, tm), lambda j, i, k: (k, i)),
               pl.BlockSpec((None, None, ts, FS), lambda j, i, k: (j % 2, j // 2, k, 0)),
               pl.BlockSpec((None, tm, FS), lambda j, i, k: (j, i, 0)), TN, (tm, FS))


def _rope_tables():
    half = HEAD_DIM // 2
    inv = ROPE_THETA ** (-jnp.arange(half, dtype=F32) / half)
    ang = jnp.arange(SEQ).astype(F32)[:, None] * inv[None, :]
    cos, sin = jnp.cos(ang), jnp.sin(ang)
    cos_t = jnp.tile(cos, (1, LANES // half))
    sin_t = jnp.tile(jnp.concatenate([-sin, sin], axis=1), (1, LANES // HEAD_DIM))
    return cos_t, sin_t


def _rotate_half(t):
    lane = lax.broadcasted_iota(jnp.int32, t.shape, 1)
    first = (lane % HEAD_DIM) < (HEAD_DIM // 2)
    return jnp.where(first, pltpu.roll(t, LANES - HEAD_DIM // 2, 1), pltpu.roll(t, HEAD_DIM // 2, 1))


def _rope(name, src, col_tile0, n_tiles, cos_t, sin_t, out_dtype):
    tm = _tile(SEQ, 512)

    def body(x_ref, c_ref, s_ref, o_ref):
        t = x_ref[...].astype(F32)
        o_ref[...] = (t * c_ref[...] + _rotate_half(t) * s_ref[...]).astype(o_ref.dtype)

    return pl.pallas_call(
        body, name=name, grid=(SEQ // tm, n_tiles),
        in_specs=[pl.BlockSpec((tm, LANES), lambda i, j: (i, j + col_tile0)),
                  pl.BlockSpec((tm, LANES), lambda i, j: (i, 0)), pl.BlockSpec((tm, LANES), lambda i, j: (i, 0))],
        out_specs=pl.BlockSpec((tm, LANES), lambda i, j: (i, j)),
        out_shape=_sds((SEQ, n_tiles * LANES), out_dtype),
        compiler_params=_cparams(("parallel", "parallel")))(src, cos_t, sin_t)


Q_TILES = ATTN_WIDTH // LANES
KV_TILES = KV_WIDTH // LANES
Q_PER_KV_TILE = Q_TILES // KV_TILES
HEADS_PER_KV_TILE = N_Q_HEADS // KV_TILES
K_TILE0 = ATTN_WIDTH // LANES
V_TILE0 = (ATTN_WIDTH + KV_WIDTH) // LANES
N_QBLK = SEQ // ATTN_BLOCK


def _dup_half(t, which):
    lane = lax.broadcasted_iota(jnp.int32, t.shape, 1)
    r = pltpu.roll(t, HEAD_DIM, 1)
    lo = lane < HEAD_DIM
    return jnp.where(lo, t, r) if which == 0 else jnp.where(lo, r, t)


def _attn_masks(n):
    row = lax.broadcasted_iota(jnp.int32, (ATTN_BLOCK, ATTN_BLOCK), 0)
    col = lax.broadcasted_iota(jnp.int32, (ATTN_BLOCK, ATTN_BLOCK), 1)
    return col <= row, (col > row) & (n > 0), col < HEAD_DIM


def _attn_probs(qm, k2c, k2p, cur_ok, prev_ok, sink):
    scale = HEAD_DIM ** -0.5
    sc = lax.dot_general(qm, k2c, NT, preferred_element_type=F32) * scale
    sp = lax.dot_general(qm, k2p, NT, preferred_element_type=F32) * scale
    sc = jnp.where(cur_ok, sc, NEG)
    sp = jnp.where(prev_ok, sp, NEG)
    m = jnp.maximum(jnp.maximum(sc.max(1, keepdims=True), sp.max(1, keepdims=True)), sink)
    pc, pp = jnp.exp(sc - m), jnp.exp(sp - m)
    esink = jnp.exp(sink - m)
    inv = 1.0 / (pc.sum(1, keepdims=True) + pp.sum(1, keepdims=True) + esink)
    return pc * inv, pp * inv, esink * inv


def _attn_specs():
    blk = (ATTN_BLOCK, LANES)
    wide = (ATTN_BLOCK, Q_PER_KV_TILE * LANES)
    prev = lambda n: jnp.maximum(n - 1, 0)
    q_spec = pl.BlockSpec(wide, lambda t, n: (n, t))
    kc = pl.BlockSpec(blk, lambda t, n: (n, K_TILE0 + t))
    kp = pl.BlockSpec(blk, lambda t, n: (prev(n), K_TILE0 + t))
    vc = pl.BlockSpec(blk, lambda t, n: (n, V_TILE0 + t))
    vp = pl.BlockSpec(blk, lambda t, n: (prev(n), V_TILE0 + t))
    return q_spec, kc, kp, vc, vp, pl.BlockSpec(memory_space=pltpu.SMEM)


def _attn_fwd(qk, h, sinks):
    q_spec, kc_s, kp_s, vc_s, vp_s, smem = _attn_specs()

    def body(sink_ref, q_ref, kc_ref, kp_ref, vc_ref, vp_ref, o_ref, ob_ref):
        t, n = pl.program_id(0), pl.program_id(1)
        cur_ok, prev_ok, lo = _attn_masks(n)
        kc, kp = kc_ref[...].astype(F32), kp_ref[...].astype(F32)
        vc, vp = vc_ref[...], vp_ref[...]
        for kvl in range(2):
            k2c, k2p = _dup_half(kc, kvl).astype(MM_DTYPE), _dup_half(kp, kvl).astype(MM_DTYPE)
            v2c, v2p = _dup_half(vc, kvl).astype(MM_DTYPE), _dup_half(vp, kvl).astype(MM_DTYPE)
            for a in (2 * kvl, 2 * kvl + 1):
                qt = q_ref[:, a * LANES:(a + 1) * LANES].astype(F32)
                outs = []
                for hs in range(2):
                    qm = jnp.where(lo == (hs == 0), qt, 0.0).astype(MM_DTYPE)
                    sink = sink_ref[t * HEADS_PER_KV_TILE + 2 * a + hs]
                    pc, pp, _ = _attn_probs(qm, k2c, k2p, cur_ok, prev_ok, sink)
                    outs.append(lax.dot_general(pc.astype(MM_DTYPE), v2c, NN, preferred_element_type=F32)
                                + lax.dot_general(pp.astype(MM_DTYPE), v2p, NN, preferred_element_type=F32))
                o = jnp.where(lo, outs[0], outs[1])
                o_ref[:, a * LANES:(a + 1) * LANES] = o
                ob_ref[:, a * LANES:(a + 1) * LANES] = o.astype(ob_ref.dtype)

    return pl.pallas_call(
        body, name="attn_fwd", grid=(KV_TILES, N_QBLK),
        in_specs=[smem, q_spec, kc_s, kp_s, vc_s, vp_s], out_specs=[q_spec, q_spec],
        out_shape=[_sds((SEQ, ATTN_WIDTH)), _sds((SEQ, ATTN_WIDTH), MM_DTYPE)],
        compiler_params=_cparams(("parallel", "parallel")))(sinks, qk, qk, qk, h, h)


def _attn_bwd(qk, h, sinks, y, dy, dy_tile0):
    q_spec, kc_s, kp_s, vc_s, vp_s, smem = _attn_specs()
    blk = (ATTN_BLOCK, LANES)
    wide = (ATTN_BLOCK, Q_PER_KV_TILE * LANES)
    kv_out = pl.BlockSpec(blk, lambda t, n: (n, t))
    dy_spec = pl.BlockSpec(wide, lambda t, n: (n, t + dy_tile0))

    def body(sink_ref, q_ref, kc_ref, kp_ref, vc_ref, vp_ref, y_ref, dy_ref,
             dq_ref, dkc_ref, dkp_ref, dvc_ref, dvp_ref, dsk_ref):
        t, n = pl.program_id(0), pl.program_id(1)
        cur_ok, prev_ok, lo = _attn_masks(n)
        scale = HEAD_DIM ** -0.5
        kc, kp = kc_ref[...].astype(F32), kp_ref[...].astype(F32)
        vc, vp = vc_ref[...], vp_ref[...]
        hrow = lax.broadcasted_iota(jnp.int32, (HEADS_PER_KV_TILE, LANES), 0)
        dsk = jnp.zeros((HEADS_PER_KV_TILE, LANES), F32)
        folded = []
        for kvl in range(2):
            k2c, k2p = _dup_half(kc, kvl).astype(MM_DTYPE), _dup_half(kp, kvl).astype(MM_DTYPE)
            v2c, v2p = _dup_half(vc, kvl).astype(MM_DTYPE), _dup_half(vp, kvl).astype(MM_DTYPE)
            acc = [jnp.zeros(blk, F32) for _ in range(4)]
            for a in (2 * kvl, 2 * kvl + 1):
                sl = slice(a * LANES, (a + 1) * LANES)
                qt = q_ref[:, sl].astype(F32)
                dot_, yt = dy_ref[:, sl], y_ref[:, sl]
                dqs = []
                for hs in range(2):
                    hm = lo == (hs == 0)
                    qm = jnp.where(hm, qt, 0.0).astype(MM_DTYPE)
                    dom = jnp.where(hm, dot_, 0.0).astype(MM_DTYPE)
                    hl = 2 * a + hs
                    sink = sink_ref[t * HEADS_PER_KV_TILE + hl]
                    pc, pp, psink = _attn_probs(qm, k2c, k2p, cur_ok, prev_ok, sink)
                    delta = jnp.sum(jnp.where(hm, dot_ * yt, 0.0), axis=1, keepdims=True)
                    dpc = lax.dot_general(dom, v2c, NT, preferred_element_type=F32)
                    dpp = lax.dot_general(dom, v2p, NT, preferred_element_type=F32)
                    dsc = (pc * (dpc - delta) * scale).astype(MM_DTYPE)
                    dsp = (pp * (dpp - delta) * scale).astype(MM_DTYPE)
                    dqs.append(lax.dot_general(dsc, k2c, NN, preferred_element_type=F32)
                               + lax.dot_general(dsp, k2p, NN, preferred_element_type=F32))
                    acc[0] += lax.dot_general(dsc, qm, TN, preferred_element_type=F32)
                    acc[1] += lax.dot_general(dsp, qm, TN, preferred_element_type=F32)
                    acc[2] += lax.dot_general(pc.astype(MM_DTYPE), dom, TN, preferred_element_type=F32)
                    acc[3] += lax.dot_general(pp.astype(MM_DTYPE), dom, TN, preferred_element_type=F32)
                    dsk = dsk + jnp.where(hrow == hl, -jnp.sum(psink * delta), 0.0)
                dq_ref[:, sl] = jnp.where(lo, dqs[0], dqs[1])
            folded.append([x + pltpu.roll(x, HEAD_DIM, 1) for x in acc])
        for o_ref, i in ((dkc_ref, 0), (dkp_ref, 1), (dvc_ref, 2), (dvp_ref, 3)):
            o_ref[...] = jnp.where(lo, folded[0][i], folded[1][i])

        @pl.when(n == 0)
        def _():
            dsk_ref[...] = jnp.zeros_like(dsk_ref)

        dsk_ref[...] += dsk

    kv_shape = _sds((SEQ, KV_WIDTH))
    return pl.pallas_call(
        body, name="attn_bwd", grid=(KV_TILES, N_QBLK),
        in_specs=[smem, q_spec, kc_s, kp_s, vc_s, vp_s, q_spec, dy_spec],
        out_specs=[q_spec, kv_out, kv_out, kv_out, kv_out,
                   pl.BlockSpec((None, HEADS_PER_KV_TILE, LANES), lambda t, n: (t, 0, 0))],
        out_shape=[_sds((SEQ, ATTN_WIDTH)), kv_shape, kv_shape, kv_shape, kv_shape,
                   _sds((KV_TILES, HEADS_PER_KV_TILE, LANES))],
        compiler_params=_cparams(("parallel", "arbitrary")))(sinks, qk, qk, qk, h, h, y, dy)


def _attn_dh(dq, dkc, dkp, dvc, dvp, cos_t, nsin_t):
    n_tiles = Q_TILES + 2 * KV_TILES
    nxt = lambda n: jnp.minimum(n + 1, N_QBLK - 1)

    def body(dq_ref, kc_ref, kp_ref, vc_ref, vp_ref, c_ref, s_ref, o_ref):
        has_next = pl.program_id(0) < N_QBLK - 1
        cos, sin = c_ref[...], s_ref[...]

        def unrope(t):
            return t * cos + _rotate_half(t) * sin

        for j in range(Q_TILES):
            sl = slice(j * LANES, (j + 1) * LANES)
            o_ref[:, sl] = unrope(dq_ref[:, sl]).astype(o_ref.dtype)
        for j in range(KV_TILES):
            sl = slice(j * LANES, (j + 1) * LANES)
            t = kc_ref[:, sl] + jnp.where(has_next, kp_ref[:, sl], 0.0)
            o_ref[:, ATTN_WIDTH + j * LANES:ATTN_WIDTH + (j + 1) * LANES] = unrope(t).astype(o_ref.dtype)
        o_ref[:, ATTN_WIDTH + KV_WIDTH:] = (vc_ref[...] + jnp.where(has_next, vp_ref[...], 0.0)).astype(o_ref.dtype)

    qb, kb, tb = (ATTN_BLOCK, ATTN_WIDTH), (ATTN_BLOCK, KV_WIDTH), (ATTN_BLOCK, LANES)
    return pl.pallas_call(
        body, name="attn_dh", grid=(N_QBLK,),
        in_specs=[pl.BlockSpec(qb, lambda n: (n, 0)),
                  pl.BlockSpec(kb, lambda n: (n, 0)), pl.BlockSpec(kb, lambda n: (nxt(n), 0)),
                  pl.BlockSpec(kb, lambda n: (n, 0)), pl.BlockSpec(kb, lambda n: (nxt(n), 0)),
                  pl.BlockSpec(tb, lambda n: (n, 0)), pl.BlockSpec(tb, lambda n: (n, 0))],
        out_specs=pl.BlockSpec((ATTN_BLOCK, n_tiles * LANES), lambda n: (n, 0)),
        out_shape=_sds((SEQ, n_tiles * LANES), MM_DTYPE),
        compiler_params=_cparams(("parallel",)))(dq, dkc, dkp, dvc, dvp, cos_t, nsin_t)


POOL_TILE0 = (ATTN_WIDTH + 2 * KV_WIDTH) // POOL_WIDTH


def _shift_rows(x, d, down):
    n = x.shape[0]
    row = lax.broadcasted_iota(jnp.int32, x.shape, 0)
    if down:
        return jnp.where(row >= d, pltpu.roll(x, d, 0), 0.0)
    return jnp.where(row < n - d, pltpu.roll(x, n - d, 0), 0.0)


def _window_sum(x, w, down):
    d = 1
    while d < w:
        x = x + _shift_rows(x, d, down)
        d *= 2
    return x


def _pool_z(u, w):
    t = lax.broadcasted_iota(jnp.int32, u.shape, 0).astype(F32)
    cnt = jnp.minimum(t + 1.0, float(w))
    return _window_sum(u, w, True) / cnt - u, cnt


def _pool_fwd(h, pool_w, pool_scale):
    def body(u_ref, w_ref, s_ref, o_ref):
        for gi, w in enumerate(POOL_WINDOWS):
            sl = slice(gi * POOL_GROUP, (gi + 1) * POOL_GROUP)
            z, _ = _pool_z(u_ref[:, sl], w)
            o_ref[:, sl] = (lax.dot_general(z.astype(MM_DTYPE), w_ref[gi].astype(MM_DTYPE), NN,
                                            preferred_element_type=F32) * s_ref[:, sl]).astype(o_ref.dtype)

    return pl.pallas_call(
        body, name="pool_fwd", grid=(1,),
        in_specs=[pl.BlockSpec((SEQ, POOL_WIDTH), lambda i: (0, POOL_TILE0)),
                  pl.BlockSpec(pool_w.shape, lambda i: (0, 0, 0)), pl.BlockSpec((1, POOL_WIDTH), lambda i: (0, 0))],
        out_specs=pl.BlockSpec((SEQ, POOL_WIDTH), lambda i: (0, 0)),
        out_shape=_sds((SEQ, POOL_WIDTH), MM_DTYPE), compiler_params=_cparams(("arbitrary",)))(h, pool_w, pool_scale)


def _pool_bwd(h, pool_w, pool_scale, dmix, dy_tile0):
    def body(u_ref, w_ref, s_ref, dy_ref, du_ref, dw_ref, ds_ref):
        for gi, w in enumerate(POOL_WINDOWS):
            sl = slice(gi * POOL_GROUP, (gi + 1) * POOL_GROUP)
            z, cnt = _pool_z(u_ref[:, sl], w)
            zb, wb = z.astype(MM_DTYPE), w_ref[gi].astype(MM_DTYPE)
            dy = dy_ref[:, sl]
            zp = lax.dot_general(zb, wb, NN, preferred_element_type=F32)
            ds_ref[:, sl] = jnp.sum(dy * zp, axis=0, keepdims=True)
            dyo = (dy * s_ref[:, sl]).astype(MM_DTYPE)
            dw_ref[gi] = lax.dot_general(zb, dyo, TN, preferred_element_type=F32)
            dz = lax.dot_general(dyo, wb, NT, preferred_element_type=F32)
            du_ref[:, sl] = (_window_sum(dz / cnt, w, False) - dz).astype(du_ref.dtype)

    return pl.pallas_call(
        body, name="pool_bwd", grid=(1,),
        in_specs=[pl.BlockSpec((SEQ, POOL_WIDTH), lambda i: (0, POOL_TILE0)),
                  pl.BlockSpec(pool_w.shape, lambda i: (0, 0, 0)), pl.BlockSpec((1, POOL_WIDTH), lambda i: (0, 0)),
                  pl.BlockSpec((SEQ, POOL_WIDTH), lambda i: (0, dy_tile0))],
        out_specs=[pl.BlockSpec((SEQ, POOL_WIDTH), lambda i: (0, 0)), pl.BlockSpec(pool_w.shape, lambda i: (0, 0, 0)),
                   pl.BlockSpec((1, POOL_WIDTH), lambda i: (0, 0))],
        out_shape=[_sds((SEQ, POOL_WIDTH), MM_DTYPE), _sds(pool_w.shape), _sds((1, POOL_WIDTH))],
        compiler_params=_cparams(("arbitrary",)))(h, pool_w, pool_scale, dmix)


def _ssm_discretize(lr, li, ldt, br, bi):
    dt = jnp.exp(ldt)
    mag = jnp.exp(lr * dt)
    ar, ai = mag * jnp.cos(li * dt), mag * jnp.sin(li * dt)
    nr, ni = ar - 1.0, ai
    den = lr * lr + li * li
    zr = (nr * lr + ni * li) / den
    zi = (ni * lr - nr * li) / den
    return ar, ai, zr * br - zi * bi, zr * bi + zi * br


def _ssm_prep(lr, li, ldt, br, bi):
    def body(lr_ref, li_ref, ldt_ref, br_ref, bi_ref, ar_ref, ai_ref, bbr_ref, bbi_ref):
        outs = _ssm_discretize(lr_ref[...], li_ref[...], ldt_ref[...], br_ref[...], bi_ref[...])
        for o, v in zip((ar_ref, ai_ref, bbr_ref, bbi_ref), outs):
            o[...] = v

    row, mat = _sds((1, SSM_CH)), _sds((SSM_GROUP, SSM_CH))
    return pl.pallas_call(body, name="ssm_prep", out_shape=[row, row, mat, mat])(lr, li, ldt, br, bi)


def _ssm_prep_bwd(lr, li, ldt, br, bi, dar8, dai8, dbbr, dbbi):
    def body(lr_ref, li_ref, ldt_ref, br_ref, bi_ref, dar_ref, dai_ref, dbbr_ref, dbbi_ref, *outs):
        args = (lr_ref[...], li_ref[...], ldt_ref[...], br_ref[...], bi_ref[...])
        _, vjp = jax.vjp(_ssm_discretize, *args)
        cot = (jnp.sum(dar_ref[...], axis=0, keepdims=True), jnp.sum(dai_ref[...], axis=0, keepdims=True),
               dbbr_ref[...], dbbi_ref[...])
        for o, v in zip(outs, vjp(cot)):
            o[...] = v

    row, mat = _sds((1, SSM_CH)), _sds((SSM_GROUP, SSM_CH))
    return pl.pallas_call(body, name="ssm_prep_bwd", out_shape=[row, row, row, mat, mat])(
        lr, li, ldt, br, bi, dar8, dai8, dbbr, dbbi)


def _scan_layout(re, im):
    r = re.shape[0]
    return jnp.stack([re.reshape(r, SCAN_NB, SCAN_CW), im.reshape(r, SCAN_NB, SCAN_CW)], axis=2).reshape(r, 2 * SSM_CH)


def _scan_unlayout(x):
    r = x.shape[0]
    x = x.reshape(r, SCAN_NB, 2, SCAN_CW)
    return x[:, :, 0].reshape(r, SSM_CH), x[:, :, 1].reshape(r, SSM_CH)


def _time_permute(u):
    s, c = u.shape
    return u.reshape(SUBLANES, s // SUBLANES, c).transpose(1, 0, 2).reshape(s, c)


def _time_unpermute(u):
    s, c = u.shape
    return u.reshape(s // SUBLANES, SUBLANES, c).transpose(1, 0, 2).reshape(s, c)


def _ssm_scan(name, a_vec, x, reverse, s_prev=None):
    nsteps = SEQ // SUBLANES
    cw = SCAN_CW
    with_da = s_prev is not None

    def body(a_ref, x_ref, *rest):
        if with_da:
            s_ref, o_ref, da_ref = rest
        else:
            o_ref, = rest
        ar = jnp.broadcast_to(a_ref[:, :cw], (SUBLANES, cw))
        ai = jnp.broadcast_to(a_ref[:, cw:], (SUBLANES, cw))
        seg = lax.broadcasted_iota(jnp.int32, (SUBLANES, cw), 0)

        def toward(v):
            if reverse:
                return jnp.where(seg < SUBLANES - 1, pltpu.roll(v, SUBLANES - 1, 0), 0.0)
            return jnp.where(seg >= 1, pltpu.roll(v, 1, 0), 0.0)

        def rows(j):
            jj = nsteps - 1 - j if reverse else j
            return pl.ds(pl.multiple_of(jj * SUBLANES, SUBLANES), SUBLANES)

        def cmul(pr, pi, qr, qi):
            return pr * qr - pi * qi, pr * qi + pi * qr

        def local(j, c):
            sr, si = c
            r = rows(j)
            mr, mi = cmul(ar, ai, sr, si)
            return mr + x_ref[r, :cw], mi + x_ref[r, cw:]

        zero = jnp.zeros((SUBLANES, cw), F32)
        fr, fi = lax.fori_loop(0, nsteps, local, (zero, zero))

        def power(_, c):
            return cmul(ar, ai, *c)

        pr, pi = lax.fori_loop(0, nsteps - 1, power, (ar, ai))
        tr, ti = fr, fi
        for _ in range(SUBLANES - 1):
            mr, mi = cmul(pr, pi, toward(tr), toward(ti))
            tr, ti = fr + mr, fi + mi
        init = (toward(tr), toward(ti))

        def full(j, c):
            r = rows(j)
            if with_da:
                sr, si, dar, dai = c
            else:
                sr, si = c
            mr, mi = cmul(ar, ai, sr, si)
            sr, si = mr + x_ref[r, :cw], mi + x_ref[r, cw:]
            o_ref[r, :cw] = sr
            o_ref[r, cw:] = si
            if not with_da:
                return sr, si
            jj = nsteps - 1 - j
            rp = pl.ds(pl.multiple_of(jnp.maximum(jj - 1, 0) * SUBLANES, SUBLANES), SUBLANES)
            last = pl.ds((nsteps - 1) * SUBLANES, SUBLANES)
            first = jj == 0
            spr = jnp.where(first, jnp.where(seg >= 1, pltpu.roll(s_ref[last, :cw], 1, 0), 0.0), s_ref[rp, :cw])
            spi = jnp.where(first, jnp.where(seg >= 1, pltpu.roll(s_ref[last, cw:], 1, 0), 0.0), s_ref[rp, cw:])
            return sr, si, dar + sr * spr + si * spi, dai + si * spr - sr * spi

        if with_da:
            _, _, dar, dai = lax.fori_loop(0, nsteps, full, init + (zero, zero))
            da_ref[:, :cw] = dar
            da_ref[:, cw:] = dai
        else:
            lax.fori_loop(0, nsteps, full, init)

    blk = pl.BlockSpec((SEQ, 2 * cw), lambda b: (0, b))
    a_spec = pl.BlockSpec((1, 2 * cw), lambda b: (0, b))
    in_specs, args = [a_spec, blk], [a_vec, x]
    out_specs, out_shape = blk, _sds((SEQ, 2 * SSM_CH))
    if with_da:
        in_specs, args = in_specs + [blk], args + [s_prev]
        out_specs = [blk, pl.BlockSpec((SUBLANES, 2 * cw), lambda b: (0, b))]
        out_shape = [out_shape, _sds((SUBLANES, 2 * SSM_CH))]
    return pl.pallas_call(body, name=name, grid=(SCAN_NB,), in_specs=in_specs, out_specs=out_specs,
                          out_shape=out_shape, compiler_params=_cparams(("parallel",)))(*args)


def _ssm_gelu(yp, up, dvec):
    tm = _tile(SEQ, 512)

    def body(y_ref, u_ref, d_ref, yf_ref, g_ref):
        yf = y_ref[...] + d_ref[...] * u_ref[...]
        yf_ref[...] = yf
        g_ref[...] = jax.nn.gelu(yf).astype(g_ref.dtype)

    blk = pl.BlockSpec((tm, SSM_WIDTH), lambda i: (i, 0))
    row = pl.BlockSpec((1, SSM_WIDTH), lambda i: (0, 0))
    return pl.pallas_call(body, name="ssm_gelu", grid=(SEQ // tm,), in_specs=[blk, blk, row], out_specs=[blk, blk],
                          out_shape=[_sds((SEQ, SSM_WIDTH)), _sds((SEQ, SSM_WIDTH), MM_DTYPE)],
                          compiler_params=_cparams(("parallel",)))(yp, up, dvec)


def _ssm_gelu_bwd(yf, dgy, up, dvec):
    tm = _tile(SEQ, 512)

    def body(yf_ref, dg_ref, u_ref, d_ref, dyf_ref, du_ref, dd_ref):
        _, vjp = jax.vjp(jax.nn.gelu, yf_ref[...])
        dyf, = vjp(dg_ref[...])
        dyf_ref[...] = dyf.astype(dyf_ref.dtype)
        du_ref[...] = d_ref[...] * dyf

        @pl.when(pl.program_id(0) == 0)
        def _():
            dd_ref[...] = jnp.zeros_like(dd_ref)

        dd_ref[...] += jnp.sum(dyf * u_ref[...], axis=0, keepdims=True)

    blk = pl.BlockSpec((tm, SSM_WIDTH), lambda i: (i, 0))
    row = pl.BlockSpec((1, SSM_WIDTH), lambda i: (0, 0))
    return pl.pallas_call(body, name="ssm_gelu_bwd", grid=(SEQ // tm,), in_specs=[blk, blk, blk, row],
                          out_specs=[blk, blk, row],
                          out_shape=[_sds((SEQ, SSM_WIDTH), MM_DTYPE), _sds((SEQ, SSM_WIDTH)), _sds((1, SSM_WIDTH))],
                          compiler_params=_cparams(("arbitrary",)))(yf, dgy, up, dvec)


def _glu(ab):
    return ab[:, :SSM_WIDTH] * jax.nn.sigmoid(ab[:, SSM_WIDTH:])


def _ssm_glu(ab):
    tm = _tile(SEQ, 512)

    def body(ab_ref, o_ref):
        o_ref[...] = _glu(ab_ref[...]).astype(o_ref.dtype)

    return pl.pallas_call(body, name="ssm_glu", grid=(SEQ // tm,),
                          in_specs=[pl.BlockSpec((tm, 2 * SSM_WIDTH), lambda i: (i, 0))],
                          out_specs=pl.BlockSpec((tm, SSM_WIDTH), lambda i: (i, 0)),
                          out_shape=_sds((SEQ, SSM_WIDTH), MM_DTYPE), compiler_params=_cparams(("parallel",)))(ab)


def _ssm_glu_bwd(ab, dout):
    tm = _tile(SEQ, 512)

    def body(ab_ref, do_ref, dab_ref):
        _, vjp = jax.vjp(_glu, ab_ref[...])
        dab, = vjp(do_ref[...])
        dab_ref[...] = dab.astype(dab_ref.dtype)

    return pl.pallas_call(body, name="ssm_glu_bwd", grid=(SEQ // tm,),
                          in_specs=[pl.BlockSpec((tm, 2 * SSM_WIDTH), lambda i: (i, 0)),
                                    pl.BlockSpec((tm, SSM_WIDTH), lambda i: (i, 0))],
                          out_specs=pl.BlockSpec((tm, 2 * SSM_WIDTH), lambda i: (i, 0)),
                          out_shape=_sds((SEQ, 2 * SSM_WIDTH), MM_DTYPE), compiler_params=_cparams(("parallel",)))(ab, dout)


def _add2(name, a, b, out_dtype):
    tm = _tile(a.shape[0], 512)

    def body(a_ref, b_ref, o_ref):
        o_ref[...] = (a_ref[...] + b_ref[...]).astype(o_ref.dtype)

    blk = pl.BlockSpec((tm, a.shape[1]), lambda i: (i, 0))
    return pl.pallas_call(body, name=name, grid=(a.shape[0] // tm,), in_specs=[blk, blk], out_specs=blk,
                          out_shape=_sds(a.shape, out_dtype), compiler_params=_cparams(("parallel",)))(a, b)


def _layer_norm(r, g, b):
    mu = r.mean(-1, keepdims=True)
    var = jnp.square(r - mu).mean(-1, keepdims=True)
    return (r - mu) * lax.rsqrt(var + LN_EPS) * g + b


def _ln_fwd(name, x, y, g, b):
    tm = _tile(SEQ, 256)

    def body(x_ref, y_ref, g_ref, b_ref, r_ref, o_ref, ob_ref):
        r = DEEPNORM_ALPHA * x_ref[...] + y_ref[...]
        r_ref[...] = r
        o = _layer_norm(r, g_ref[...], b_ref[...])
        o_ref[...] = o
        ob_ref[...] = o.astype(ob_ref.dtype)

    blk = pl.BlockSpec((tm, D_MODEL), lambda i: (i, 0))
    row = pl.BlockSpec((1, D_MODEL), lambda i: (0, 0))
    return pl.pallas_call(body, name=name, grid=(SEQ // tm,), in_specs=[blk, blk, row, row], out_specs=[blk, blk, blk],
                          out_shape=[_sds((SEQ, D_MODEL))] * 2 + [_sds((SEQ, D_MODEL), MM_DTYPE)],
                          compiler_params=_cparams(("parallel",)))(x, y, g, b)


def _ln_bwd(name, r, g, b, da, db=None, dep=None):
    tm = _tile(SEQ, 256)
    two = db is not None
    deps = [] if dep is None else [dep]

    def body(r_ref, g_ref, b_ref, da_ref, *rest):
        dr_ref, drb_ref, dg_ref, dbeta_ref = rest[-4:]
        dout = DEEPNORM_ALPHA * da_ref[...] + rest[0][...] if two else da_ref[...]
        _, vjp = jax.vjp(_layer_norm, r_ref[...], g_ref[...], b_ref[...])
        dr, dg, dbeta = vjp(dout)
        dr_ref[...] = dr
        drb_ref[...] = dr.astype(drb_ref.dtype)

        @pl.when(pl.program_id(0) == 0)
        def _():
            dg_ref[...] = jnp.zeros_like(dg_ref)
            dbeta_ref[...] = jnp.zeros_like(dbeta_ref)

        dg_ref[...] += dg
        dbeta_ref[...] += dbeta

    blk = pl.BlockSpec((tm, D_MODEL), lambda i: (i, 0))
    row = pl.BlockSpec((1, D_MODEL), lambda i: (0, 0))
    args = [r, g, b, da] + ([db] if two else []) + deps
    return pl.pallas_call(body, name=name, grid=(SEQ // tm,),
                          in_specs=[blk, row, row, blk] + ([blk] if two else []) + [ANY] * len(deps),
                          out_specs=[blk, blk, row, row],
                          out_shape=[_sds((SEQ, D_MODEL)), _sds((SEQ, D_MODEL), MM_DTYPE), _sds((1, D_MODEL)), _sds((1, D_MODEL))],
                          compiler_params=_cparams(("arbitrary",)))(*args)


FFN_TM = 128
HALO = SUBLANES


def _conv_taps(cur, halo):
    row = lax.broadcasted_iota(jnp.int32, cur.shape, 0)
    h1 = jnp.where(row == 0, halo[HALO - 1:HALO, :], pltpu.roll(cur, 1, 0))
    h2 = jnp.where(row == 0, halo[HALO - 2:HALO - 1, :], jnp.where(row == 1, halo[HALO - 1:HALO, :], pltpu.roll(cur, 2, 0)))
    return h1, h2


def _conv_fwd(cur, halo, w_ref, b_ref):
    h1, h2 = _conv_taps(cur, halo)
    return b_ref[...] + h2 * w_ref[0:1, :] + h1 * w_ref[1:2, :] + cur * w_ref[2:3, :], h1, h2


def _gate(val, gate):
    return jax.nn.silu(gate) * val


def _ffn_specs(tm):
    nb = tm // HALO
    cur = lambda off: pl.BlockSpec((None, tm, FS), lambda p, i: (p + off, i, 0))
    halo = lambda off: pl.BlockSpec((None, HALO, FS), lambda p, i: (p + off, jnp.maximum(i * nb - 1, 0), 0))
    cw = lambda off: pl.BlockSpec((None, CONV_WIDTH, FS), lambda p, i: (p + off, 0, 0))
    cb = lambda off: pl.BlockSpec((None, 1, FS), lambda p, i: (p + off, 0, 0))
    return cur, halo, cw, cb


def _ffn_act(hf, conv_w, conv_b):
    tm = _tile(SEQ, FFN_TM, SUBLANES)
    cur, halo, cw, cb = _ffn_specs(tm)

    def body(v_ref, vh_ref, g_ref, gh_ref, wv_ref, wg_ref, bv_ref, bg_ref, o_ref):
        live = pl.program_id(1) > 0
        vh = jnp.where(live, vh_ref[...], 0.0)
        gh = jnp.where(live, gh_ref[...], 0.0)
        val, _, _ = _conv_fwd(v_ref[...], vh, wv_ref, bv_ref)
        gate, _, _ = _conv_fwd(g_ref[...], gh, wg_ref, bg_ref)
        o_ref[...] = _gate(val, gate).astype(o_ref.dtype)

    return pl.pallas_call(
        body, name="ffn_act", grid=(2, SEQ // tm),
        in_specs=[cur(0), halo(0), cur(2), halo(2), cw(0), cw(2), cb(0), cb(2)],
        out_specs=pl.BlockSpec((None, tm, FS), lambda p, i: (p, i, 0)),
        out_shape=_sds((2, SEQ, FS), MM_DTYPE), compiler_params=_cparams(("parallel", "parallel")))(
            hf, hf, hf, hf, conv_w, conv_w, conv_b, conv_b)


def _ffn_act_bwd(hf, conv_w, conv_b, dact):
    tm = _tile(SEQ, FFN_TM, SUBLANES)
    cur, halo, cw, cb = _ffn_specs(tm)

    def body(v_ref, vh_ref, g_ref, gh_ref, wv_ref, wg_ref, bv_ref, bg_ref, da_ref, dhc_ref, dw_ref, dbias_ref):
        dv_ref, dg_ref = dhc_ref.at[0], dhc_ref.at[1]
        dwv_ref, dwg_ref = dw_ref.at[0], dw_ref.at[1]
        dbv_ref, dbg_ref = dbias_ref.at[0], dbias_ref.at[1]
        i = pl.program_id(1)
        live = i > 0
        vh = jnp.where(live, vh_ref[...], 0.0)
        gh = jnp.where(live, gh_ref[...], 0.0)
        vcur, gcur = v_ref[...], g_ref[...]
        val, v1, v2 = _conv_fwd(vcur, vh, wv_ref, bv_ref)
        gate, g1, g2 = _conv_fwd(gcur, gh, wg_ref, bg_ref)
        _, vjp = jax.vjp(_gate, val, gate)
        dval, dgate = vjp(da_ref[...])
        dv_ref[...] = dval
        dg_ref[...] = dgate

        @pl.when(i == 0)
        def _():
            dw_ref[...] = jnp.zeros_like(dw_ref)
            dbias_ref[...] = jnp.zeros_like(dbias_ref)

        for d, taps, dwk_ref, dbk_ref in ((dval, (v2, v1, vcur), dwv_ref, dbv_ref), (dgate, (g2, g1, gcur), dwg_ref, dbg_ref)):
            for k in range(CONV_WIDTH):
                dwk_ref[k:k + 1, :] += jnp.sum(d * taps[k], axis=0, keepdims=True)
            dbk_ref[...] += jnp.sum(d, axis=0, keepdims=True)

    return pl.pallas_call(
        body, name="ffn_act_bwd", grid=(2, SEQ // tm),
        in_specs=[cur(0), halo(0), cur(2), halo(2), cw(0), cw(2), cb(0), cb(2),
                  pl.BlockSpec((None, tm, FS), lambda p, i: (p, i, 0))],
        out_specs=[pl.BlockSpec((None, 2, tm, FS), lambda p, i: (p, 0, i, 0)),
                   pl.BlockSpec((None, 2, CONV_WIDTH, FS), lambda p, i: (p, 0, 0, 0)),
                   pl.BlockSpec((None, 2, 1, FS), lambda p, i: (p, 0, 0, 0))],
        out_shape=[_sds((2, 2, SEQ, FS)), _sds((2, 2, CONV_WIDTH, FS)), _sds((2, 2, 1, FS))],
        compiler_params=_cparams(("parallel", "arbitrary")))(hf, hf, hf, hf, conv_w, conv_w, conv_b, conv_b, dact)


def _conv_bwd_input(dhc, conv_w):
    tm = _tile(SEQ, FFN_TM, SUBLANES)
    nb = tm // HALO
    nblk = SEQ // tm

    def body(d_ref, nx_ref, w_ref, o_ref):
        cur = d_ref[...]
        nxt = jnp.where(pl.program_id(2) < nblk - 1, nx_ref[...], 0.0)
        row = lax.broadcasted_iota(jnp.int32, cur.shape, 0)
        d1 = jnp.where(row == tm - 1, nxt[0:1, :], pltpu.roll(cur, tm - 1, 0))
        d2 = jnp.where(row == tm - 1, nxt[1:2, :], jnp.where(row == tm - 2, nxt[0:1, :], pltpu.roll(cur, tm - 2, 0)))
        o_ref[...] = (cur * w_ref[2:3, :] + d1 * w_ref[1:2, :] + d2 * w_ref[0:1, :]).astype(o_ref.dtype)

    blk = pl.BlockSpec((None, None, tm, FS), lambda p, kd, i: (p, kd, i, 0))
    return pl.pallas_call(
        body, name="conv_bwd_input", grid=(2, 2, nblk),
        in_specs=[blk, pl.BlockSpec((None, None, HALO, FS), lambda p, kd, i: (p, kd, jnp.minimum((i + 1) * nb, SEQ // HALO - 1), 0)),
                  pl.BlockSpec((None, CONV_WIDTH, FS), lambda p, kd, i: (2 * kd + p, 0, 0))],
        out_specs=blk, out_shape=_sds((2, 2, SEQ, FS), MM_DTYPE),
        compiler_params=_cparams(("parallel", "parallel", "parallel")))(dhc, dhc, conv_w)


def _loss(y, target):
    tm = _tile(SEQ, 256)

    def body(y_ref, t_ref, dy_ref, l_ref):
        err = y_ref[...] - t_ref[...]
        dy_ref[...] = err * (1.0 / D_MODEL)

        @pl.when(pl.program_id(0) == 0)
        def _():
            l_ref[...] = jnp.zeros_like(l_ref)

        l_ref[...] += 0.5 * jnp.sum(jnp.mean(jnp.square(err), axis=-1))

    blk = pl.BlockSpec((tm, D_MODEL), lambda i: (i, 0))
    return pl.pallas_call(body, name="loss", grid=(SEQ // tm,), in_specs=[blk, blk],
                          out_specs=[blk, pl.BlockSpec((SUBLANES, LANES), lambda i: (0, 0))],
                          out_shape=[_sds((SEQ, D_MODEL)), _sds((SUBLANES, LANES))],
                          compiler_params=_cparams(("arbitrary",)))(y, target)


ADAM_BLOCK_BYTES = 1 << 20


def _adamw_math(w, g, m, v):
    nm = ADAM_B1 * m + (1.0 - ADAM_B1) * g
    nv = ADAM_B2 * v + (1.0 - ADAM_B2) * jnp.square(g)
    m_hat = nm / (1.0 - ADAM_B1 ** ADAM_STEP)
    v_hat = nv / (1.0 - ADAM_B2 ** ADAM_STEP)
    return -ADAM_LR * (m_hat / (jnp.sqrt(v_hat) + ADAM_EPS) + ADAM_WD * w), nm, nv


def _adamw(name, w, g, m, v):
    r, c = w.shape
    tr = _tile(r, max(SUBLANES, ADAM_BLOCK_BYTES // (4 * c)), SUBLANES)

    def body(w_ref, g_ref, m_ref, v_ref, d_ref, nm_ref, nv_ref):
        d_ref[...], nm_ref[...], nv_ref[...] = _adamw_math(w_ref[...], g_ref[...], m_ref[...], v_ref[...])

    blk = pl.BlockSpec((tr, c), lambda i: (i, 0))
    return pl.pallas_call(body, name=name, grid=(r // tr,), in_specs=[blk] * 4, out_specs=[blk] * 3,
                          out_shape=[_sds((r, c))] * 3, compiler_params=_cparams(("parallel",)))(w, g, m, v)


def _adamw_big(name, l, c_idx, w, m, v, g_own, g_got, prev):
    depth, _, r, c = w.shape
    tr = _tile(r, max(SUBLANES, ADAM_BLOCK_BYTES // (4 * c)), SUBLANES)

    def body(c_ref, w_ref, m_ref, v_ref, own_ref, got_ref, *rest):
        g_ref, d_ref, nm_ref, nv_ref = rest[-4:]
        g = jnp.where(pl.program_id(0) == c_ref[0], own_ref[...], got_ref[...])
        g_ref[...] = g
        d_ref[...], nm_ref[...], nv_ref[...] = _adamw_math(w_ref[...], g, m_ref[...], v_ref[...])

    stacked = pl.BlockSpec((None, None, tr, c), lambda h, i, cr: (l, h, i, 0))
    own = pl.BlockSpec((tr, c), lambda h, i, cr: (jnp.where(h == cr[0], i, 0), 0))
    got = pl.BlockSpec((tr, c), lambda h, i, cr: (jnp.where(h == cr[0], 0, i), 0))
    grid_spec = pltpu.PrefetchScalarGridSpec(
        num_scalar_prefetch=1, grid=(2, r // tr),
        in_specs=[stacked] * 3 + [own, got] + ([ANY] * 4 if prev else []), out_specs=[stacked] * 4)
    return pl.pallas_call(
        body, name=name, grid_spec=grid_spec, out_shape=[_sds((depth, 2, r, c))] * 4,
        input_output_aliases={6 + k: k for k in range(4)} if prev else {},
        compiler_params=_cparams(("arbitrary", "arbitrary")))(c_idx, w, m, v, g_own, g_got, *(prev or ()))


ANY = pl.BlockSpec(memory_space=pl.ANY)


def _place():
    x, y, c = lax.axis_index("x"), lax.axis_index("y"), lax.axis_index("c")
    chips = [(1 - x, y), (x, 1 - y), (1 - x, 1 - y)]
    return x, y, c, chips


def _cast_place(name, w, l, me_idx, out_dtype):
    _, _, r, c = w.shape
    tr = _tile(r, max(2 * SUBLANES, COPY_BLOCK_BYTES // (4 * c)), 2 * SUBLANES)

    def body(me_ref, w_ref, o_ref):
        o_ref[...] = w_ref[...].astype(o_ref.dtype)

    grid_spec = pltpu.PrefetchScalarGridSpec(
        num_scalar_prefetch=1, grid=(2, r // tr),
        in_specs=[pl.BlockSpec((None, None, tr, c), lambda h, i, me: (l, h, i, 0))],
        out_specs=pl.BlockSpec((None, None, tr, c), lambda h, i, me: (me[0], h, i, 0)))
    return pl.pallas_call(body, name=name, grid_spec=grid_spec, out_shape=_sds((N_CHIPS, 2, r, c), out_dtype),
                          compiler_params=_cparams(("parallel", "parallel")))(me_idx, w)


HBM = pl.BlockSpec(memory_space=pltpu.HBM)
SEM = pl.BlockSpec(memory_space=pltpu.SEMAPHORE)
TOKEN = (SUBLANES, LANES)


def _comm_call(name, body, hbm, sems_in=(), after=None, sems_out=(), token=False):
    n, k = len(hbm), len(sems_out)
    ins = [pltpu.with_memory_space_constraint(a, pltpu.HBM) for a in hbm] + list(sems_in)
    in_specs = [HBM] * n + [SEM] * len(sems_in)
    if after is not None:
        ins.append(after)
        in_specs.append(ANY)
    out_shape = [pltpu.SemaphoreType.DMA((s,)) for s in sems_out] + [pltpu.HBM(a.shape, a.dtype) for a in hbm]
    out_specs = [SEM] * k + [HBM] * n
    if token:
        out_shape.append(_sds(TOKEN))
        out_specs.append(pl.BlockSpec(memory_space=pltpu.VMEM))
    res = pl.pallas_call(
        body, name=name, in_specs=in_specs, out_specs=out_specs, out_shape=out_shape,
        input_output_aliases={i: k + i for i in range(n)},
        compiler_params=pltpu.CompilerParams(has_side_effects=pltpu.SideEffectType.DATAFLOW_SIDE_EFFECTING))(*ins)
    return list(res[:k]), list(res[k:k + n]), (res[k + n] if token else None)


def _remote(src, dst, send, recv, to):
    return pltpu.make_async_remote_copy(src_ref=src, dst_ref=dst, send_sem=send, recv_sem=recv, device_id=to,
                                        device_id_type=MESH)


def _gather_start(name, bufs):
    n = len(bufs)

    def body(*refs):
        ins, (send, recv), token = refs[:n], refs[n:n + 2], refs[-1]
        x, y, c, chips = _place()
        for i in range(n):
            mine = ins[i].at[2 * x + y, c]
            for k, chip in enumerate(chips):
                _remote(mine, mine, send.at[3 * i + k], recv.at[3 * i + k], (*chip, c)).start()
        token[...] = jnp.zeros(TOKEN, F32)

    return _comm_call(name, body, bufs, sems_out=(3 * n, 3 * n), token=True)


def _gather_forward(name, bufs, sems, after):
    n = len(bufs)
    o = n + 2 + (after is not None)

    def body(*refs):
        ins, (send, recv), (send2, recv2), token = refs[:n], refs[n:n + 2], refs[o:o + 2], refs[-1]
        x, y, c, chips = _place()
        for i in range(n):
            mine = ins[i].at[2 * x + y, c]
            for k, chip in enumerate(chips):
                land = ins[i].at[2 * chip[0] + chip[1], c]
                first = _remote(mine, land, send.at[3 * i + k], recv.at[3 * i + k], (*chip, c))
                first.wait_send()
                first.wait_recv()
                _remote(land, land, send2.at[3 * i + k], recv2.at[3 * i + k], (x, y, 1 - c)).start()
        token[...] = jnp.zeros(TOKEN, F32)

    return _comm_call(name, body, bufs, sems_in=sems, after=after, sems_out=(3 * n, 3 * n), token=True)


def _gather_finish(name, bufs, sems, after):
    n = len(bufs)

    def body(*refs):
        ins, (send, recv) = refs[:n], refs[n:n + 2]
        x, y, c, chips = _place()
        for i in range(n):
            for k, chip in enumerate(chips):
                idx = 2 * chip[0] + chip[1]
                cp = _remote(ins[i].at[idx, c], ins[i].at[idx, 1 - c], send.at[3 * i + k], recv.at[3 * i + k], (x, y, 1 - c))
                cp.wait_send()
                cp.wait_recv()

    return _comm_call(name, body, bufs, sems_in=sems, after=after)[1]


def _swap_start(name, grads):
    n = len(grads)
    lands = [lax.empty((g.shape[0],) + g.shape[2:], g.dtype) for g in grads]

    def body(*refs):
        ins, lnd, (send, recv), token = refs[:n], refs[n:2 * n], refs[2 * n:2 * n + 2], refs[-1]
        x, y, c, _ = _place()
        for i in range(n):
            _remote(ins[i].at[:, 1 - c], lnd[i], send.at[i], recv.at[i], (x, y, 1 - c)).start()
        token[...] = jnp.zeros(TOKEN, F32)

    return _comm_call(name, body, list(grads) + lands, sems_out=(n, n), token=True)


def _swap_wait(name, hbm, sems, after):
    n = len(hbm) // 2

    def body(*refs):
        ins, lnd, (send, recv) = refs[:n], refs[n:2 * n], refs[2 * n:2 * n + 2]
        x, y, c, _ = _place()
        for i in range(n):
            cp = _remote(ins[i].at[:, 1 - c], lnd[i], send.at[i], recv.at[i], (x, y, 1 - c))
            cp.wait_send()
            cp.wait_recv()

    out = _comm_call(name, body, hbm, sems_in=sems, after=after)[1]
    return out[:n], out[n:]


def _pair_add(name, g, got, cm_idx):
    nk, _, r, c = g.shape
    tr = _tile(r, max(2 * SUBLANES, COPY_BLOCK_BYTES // (4 * c)), 2 * SUBLANES)

    def body(cm_ref, g_ref, x_ref, o_ref, land_ref):
        s = (g_ref[...] + x_ref[...]).astype(o_ref.dtype)
        o_ref[...] = s

        @pl.when(pl.program_id(1) == cm_ref[1])
        def _():
            land_ref[...] = s

    grid_spec = pltpu.PrefetchScalarGridSpec(
        num_scalar_prefetch=1, grid=(r // tr, nk),
        in_specs=[pl.BlockSpec((None, None, tr, c), lambda i, k, cm: (k, cm[0], i, 0)),
                  pl.BlockSpec((None, tr, c), lambda i, k, cm: (k, i, 0))],
        out_specs=[pl.BlockSpec((None, tr, c), lambda i, k, cm: (k, i, 0)),
                   pl.BlockSpec((None, tr, c), lambda i, k, cm: (cm[1], i, 0))])
    return pl.pallas_call(body, name=name, grid_spec=grid_spec, out_shape=[_sds((nk, r, c), BF16)] * 2,
                          compiler_params=_cparams(("parallel", "arbitrary")))(cm_idx, g, got)


def _scatter_start(name, parts, lands):
    n = len(parts)

    def body(*refs):
        ins, lnd, (send, recv), token = refs[:n], refs[n:2 * n], refs[2 * n:2 * n + 2], refs[-1]
        x, y, c, chips = _place()
        for i in range(n):
            for k, chip in enumerate(chips):
                _remote(ins[i].at[2 * chip[0] + chip[1]], lnd[i].at[2 * x + y], send.at[3 * i + k], recv.at[3 * i + k],
                        (*chip, c)).start()
        token[...] = jnp.zeros(TOKEN, F32)

    return _comm_call(name, body, list(parts) + list(lands), sems_out=(3 * n, 3 * n), token=True)


def _scatter_wait(name, hbm, sems, after):
    n = len(hbm) // 2

    def body(*refs):
        ins, lnd, (send, recv) = refs[:n], refs[n:2 * n], refs[2 * n:2 * n + 2]
        x, y, c, chips = _place()
        for i in range(n):
            for k, chip in enumerate(chips):
                idx = 2 * chip[0] + chip[1]
                cp = _remote(ins[i].at[idx], lnd[i].at[idx], send.at[3 * i + k], recv.at[3 * i + k], (*chip, c))
                cp.wait_send()
                cp.wait_recv()

    out = _comm_call(name, body, hbm, sems_in=sems, after=after)[1]
    return out[:n], out[n:]


def _sum_leading(name, x, out_dtype=F32):
    nk, r, c = x.shape
    tr = _tile(r, max(2 * SUBLANES, COPY_BLOCK_BYTES // (nk * c * x.dtype.itemsize)), 2 * SUBLANES)

    def body(x_ref, o_ref):
        acc = x_ref[0].astype(F32)
        for k in range(1, nk):
            acc = acc + x_ref[k].astype(F32)
        o_ref[...] = acc.astype(o_ref.dtype)

    return pl.pallas_call(body, name=name, grid=(r // tr,), in_specs=[pl.BlockSpec((nk, tr, c), lambda i: (0, i, 0))],
                          out_specs=pl.BlockSpec((tr, c), lambda i: (i, 0)), out_shape=_sds((r, c), out_dtype),
                          compiler_params=_cparams(("parallel",)))(x)


def _exchange_start(name, halves):
    n = len(halves)
    lands = [lax.empty(h.shape, h.dtype) for h in halves]

    def body(*refs):
        ins, lnd, (send, recv), token = refs[:n], refs[n:2 * n], refs[2 * n:2 * n + 2], refs[-1]
        x, y, c, _ = _place()
        for i in range(n):
            _remote(ins[i], lnd[i], send.at[i], recv.at[i], (x, y, 1 - c)).start()
        token[...] = jnp.zeros(TOKEN, F32)

    return _comm_call(name, body, list(halves) + lands, sems_out=(n, n), token=True)


def _exchange_wait(name, hbm, sems, after):
    n = len(hbm) // 2

    def body(*refs):
        ins, lnd, (send, recv) = refs[:n], refs[n:2 * n], refs[2 * n:2 * n + 2]
        x, y, c, _ = _place()
        for i in range(n):
            cp = _remote(ins[i], lnd[i], send.at[i], recv.at[i], (x, y, 1 - c))
            cp.wait_send()
            cp.wait_recv()

    out = _comm_call(name, body, hbm, sems_in=sems, after=after)[1]
    return out[:n], out[n:]


def _gather_all(part):
    def body(x_ref, out_ref, send_sems, recv_sems, local_sem):
        x, y, c, chips = _place()
        me, sibling = (x, y, c), (x, y, 1 - c)

        def rows(px, py, pc):
            return out_ref.at[4 * px + 2 * py + pc]

        def copy(k, block, to, src=None):
            return pltpu.make_async_remote_copy(src_ref=rows(*block) if src is None else src, dst_ref=rows(*block),
                                                send_sem=send_sems.at[k], recv_sem=recv_sems.at[k], device_id=to,
                                                device_id_type=MESH)

        mine = pltpu.make_async_copy(x_ref, rows(*me), local_sem)
        mine.start()
        first = [copy(0, me, sibling, src=x_ref)]
        first += [copy(1 + j, me, (*chip, c), src=x_ref) for j, chip in enumerate(chips)]
        for cp in first:
            cp.start()
        passed = [copy(4 + j, (*chip, c), sibling) for j, chip in enumerate(chips)]
        for j, chip in enumerate(chips):
            copy(1 + j, (*chip, c), me).wait_recv()
            passed[j].start()
        copy(0, sibling, me).wait_recv()
        for j, chip in enumerate(chips):
            copy(4 + j, (*chip, 1 - c), me).wait_recv()
        for cp in first + passed:
            cp.wait_send()
        mine.wait()

    return pl.pallas_call(
        body, name="gather_small_grads", in_specs=[ANY], out_specs=ANY, out_shape=_sds((N_DEV,) + part.shape, part.dtype),
        scratch_shapes=[pltpu.SemaphoreType.DMA((7,)), pltpu.SemaphoreType.DMA((7,)), pltpu.SemaphoreType.DMA])(part)


SMALL = ("attn_sinks", "pool_w", "pool_scale", "ssm_lam_re", "ssm_lam_im", "ssm_log_dt", "ssm_b_re", "ssm_b_im",
         "ssm_c_re", "ssm_c_im", "ssm_d", "ln1_g", "ln1_b", "ffn_conv_b", "ln2_g", "ln2_b")
BIG = ("w_in", "ssm_glu_w", "w_out", "ffn_w_up", "ffn_conv_w", "ffn_w_down")
ALL_W = ("w_in", "attn_sinks", "pool_w", "pool_scale", "ssm_lam_re", "ssm_lam_im", "ssm_log_dt", "ssm_b_re", "ssm_b_im",
         "ssm_c_re", "ssm_c_im", "ssm_d", "ssm_glu_w", "w_out", "ln1_g", "ln1_b", "ffn_w_up", "ffn_conv_w", "ffn_conv_b",
         "ffn_w_down", "ln2_g", "ln2_b")
PACK_UNIT = SUBLANES * LANES


def _padded(n):
    return -(-n // PACK_UNIT) * PACK_UNIT


def _pack(arrs):
    cols = []
    for name in SMALL:
        a = arrs[name].reshape(DEPTH, -1)
        cols.append(jnp.pad(a, ((0, 0), (0, _padded(a.shape[1]) - a.shape[1]))))
    return jnp.concatenate(cols, axis=1).reshape(-1, LANES)


def _unpack(packed, shapes):
    flat = packed.reshape(DEPTH, -1)
    out, off = {}, 0
    for name in SMALL:
        n = math.prod(shapes[name][1:])
        out[name] = flat[:, off:off + n].reshape(shapes[name])
        off += _padded(n)
    return out


def _b_rows(b):
    return b.transpose(2, 0, 1).reshape(SSM_GROUP, SSM_CH)


def _b_unrows(b):
    return b.reshape(SSM_GROUP, SSM_N_GROUPS, SSM_STATE).transpose(1, 2, 0)


def _block_diag_in(bb):
    eye = jnp.eye(SSM_N_GROUPS, dtype=F32)
    b3 = bb.reshape(SSM_GROUP, SSM_N_GROUPS, SSM_STATE)
    return jnp.einsum("hgp,gk->ghkp", b3, eye).reshape(SSM_WIDTH, SSM_CH)


def _block_diag_in_t(full):
    f4 = full.reshape(SSM_N_GROUPS, SSM_GROUP, SSM_N_GROUPS, SSM_STATE)
    return jnp.einsum("ghgp->hgp", f4).reshape(SSM_GROUP, SSM_CH)


def _block_diag_out(cc):
    eye = jnp.eye(SSM_N_GROUPS, dtype=F32)
    return jnp.einsum("ghp,gk->gpkh", cc, eye).reshape(SSM_CH, SSM_WIDTH)


def _block_diag_out_t(full):
    f4 = full.reshape(SSM_N_GROUPS, SSM_STATE, SSM_N_GROUPS, SSM_GROUP)
    return jnp.einsum("gpgh->ghp", f4)


def _rows_layout(re, im):
    n = re.shape[1]
    return jnp.stack([re.reshape(SCAN_NB, SCAN_CW, n), im.reshape(SCAN_NB, SCAN_CW, n)], axis=1).reshape(2 * SSM_CH, n)


def _rows_unlayout(x):
    n = x.shape[1]
    x = x.reshape(SCAN_NB, 2, SCAN_CW, n)
    return x[:, 0].reshape(SSM_CH, n), x[:, 1].reshape(SSM_CH, n)


H_POOL0 = ATTN_WIDTH + 2 * KV_WIDTH
H_SSM0 = H_POOL0 + POOL_WIDTH


def _ssm_params(p):
    lr = p["ssm_lam_re"].reshape(1, SSM_CH)
    li = p["ssm_lam_im"].reshape(1, SSM_CH)
    ldt = jnp.repeat(p["ssm_log_dt"], SSM_STATE).reshape(1, SSM_CH)
    return lr, li, ldt, _b_rows(p["ssm_b_re"]), _b_rows(p["ssm_b_im"])


def _layer_fwd(x, xb, p, wg, rope_t, dep, mid):
    cos_t, sin_t = rope_t
    h = _mm_shard_cols("in_proj", xb, wg["w_in"], dep=dep)
    qk = _rope("rope_fwd", h, 0, Q_TILES + KV_TILES, cos_t, sin_t, MM_DTYPE)
    y_attn, y_attn_b = _attn_fwd(qk, h, p["attn_sinks"])
    y_pool = _pool_fwd(h, p["pool_w"], p["pool_scale"].reshape(1, POOL_WIDTH))
    ssm_in = _ssm_params(p)
    ar, ai, bbr, bbi = _ssm_prep(*ssm_in)
    bd = _scan_layout(_block_diag_in(bbr), _block_diag_in(bbi)).astype(MM_DTYPE)
    cc = _rows_layout(_block_diag_out(p["ssm_c_re"]), -_block_diag_out(p["ssm_c_im"])).astype(MM_DTYPE)
    dvec = p["ssm_d"].reshape(1, SSM_WIDTH)
    up = _time_permute(h[:, H_SSM0:])
    xx = _mm_nn("ssm_bu", up, bd, tn=1024)
    ss = _ssm_scan("ssm_scan_fwd", _scan_layout(ar, ai), xx, False)
    yp = _mm_nn("ssm_cs", ss, cc, tk=1024)
    yf, gy = _ssm_gelu(yp, up, dvec)
    ab = _mm_shard_cols("ssm_glu_proj", gy, wg["ssm_glu_w"])
    y_ssm = _time_unpermute(_ssm_glu(ab))
    mix = jnp.concatenate([y_attn_b, y_pool, y_ssm], axis=1)
    mixo = _mm_nn("out_proj", mix, wg["w_out"].reshape(MIX_WIDTH, D_MODEL))
    r1, x1, x1b = _ln_fwd("ln1_fwd", x, mixo, p["ln1_g"].reshape(1, D_MODEL), p["ln1_b"].reshape(1, D_MODEL))
    hf = _ffn_up(x1b, wg["ffn_w_up"], dep=mid(x1b))
    conv_b = p["ffn_conv_b"].reshape(N_CHIPS, 1, FS)
    act = _ffn_act(hf, wg["ffn_conv_w"], conv_b)
    f = _ffn_down(act, wg["ffn_w_down"].reshape(2, FS, D_MODEL))
    r2, x2, x2b = _ln_fwd("ln2_fwd", x1, f, p["ln2_g"].reshape(1, D_MODEL), p["ln2_b"].reshape(1, D_MODEL))
    saved = dict(xb=xb, h=h, qk=qk, y_attn=y_attn, ssm_in=ssm_in, ar=ar, ai=ai, bd=bd, cc=cc, dvec=dvec, up=up, ss=ss, yf=yf,
                 gy=gy, ab=ab, mix=mix, r1=r1, x1b=x1b, hf=hf, conv_b=conv_b, act=act, r2=r2)
    return x2, x2b, saved


def _layer_bwd(da, db, p, wg, sv, rope_t, dep, mid1, mid2):
    cos_t, sin_t = rope_t
    small = {}
    dr2, dr2b, dg, dbeta = _ln_bwd("ln2_bwd" if db is not None else "ln2_bwd_last", sv["r2"], p["ln2_g"].reshape(1, D_MODEL),
                                   p["ln2_b"].reshape(1, D_MODEL), da, db, dep=dep)
    small["ln2_g"], small["ln2_b"] = dg, dbeta
    w_down = wg["ffn_w_down"].reshape(2, FS, D_MODEL)
    dact = _ffn_down_dact(dr2b, w_down)
    dw_down = _ffn_down_dw(sv["act"], dr2b)
    dhc, dcw, dcb = _ffn_act_bwd(sv["hf"], wg["ffn_conv_w"], sv["conv_b"], dact)
    dconv_w = dcw.transpose(1, 0, 2, 3).reshape(N_CHIPS, CONV_WIDTH, FS)
    small["ffn_conv_b"] = dcb.transpose(1, 0, 2, 3)
    dh_ffn = _conv_bwd_input(dhc, wg["ffn_conv_w"])
    dx1_ffn = _ffn_up_dx(dh_ffn, wg["ffn_w_up"])
    dw_up = _ffn_up_dw(sv["x1b"], dh_ffn)
    dr1, dr1b, dg, dbeta = _ln_bwd("ln1_bwd", sv["r1"], p["ln1_g"].reshape(1, D_MODEL), p["ln1_b"].reshape(1, D_MODEL), dr2,
                                   dx1_ffn, dep=mid1(dx1_ffn))
    small["ln1_g"], small["ln1_b"] = dg, dbeta
    w_out = wg["w_out"].reshape(MIX_WIDTH, D_MODEL)
    dmix = _mm_nt("out_proj_dx", dr1b, w_out)
    dw_out = _mm_tn("out_proj_dw", sv["mix"], dr1b)
    dq, dkc, dkp, dvc, dvp, dsk = _attn_bwd(sv["qk"], sv["h"], p["attn_sinks"], sv["y_attn"], dmix, 0)
    small["attn_sinks"] = dsk[:, :, 0]
    dh_attn = _attn_dh(dq, dkc, dkp, dvc, dvp, cos_t, -sin_t)
    dh_pool, dpw, dps = _pool_bwd(sv["h"], p["pool_w"], p["pool_scale"].reshape(1, POOL_WIDTH), dmix, ATTN_WIDTH // POOL_WIDTH)
    small["pool_w"], small["pool_scale"] = dpw, dps
    dout_p = _time_permute(dmix[:, ATTN_WIDTH + POOL_WIDTH:])
    dab = _ssm_glu_bwd(sv["ab"], dout_p)
    dgy = _mm_shard_cols_nt("ssm_glu_dx", dab, wg["ssm_glu_w"])
    dw_glu = _mm_shard_cols_tn("ssm_glu_dw", sv["gy"], dab, N_CHIPS)
    dyf, du1, dd = _ssm_gelu_bwd(sv["yf"], dgy, sv["up"], sv["dvec"])
    small["ssm_d"] = dd
    dss = _mm_nt("ssm_cs_dx", dyf, sv["cc"], tn=1024)
    dcc = _mm_tn("ssm_cs_dw", sv["ss"], dyf, tm=1024)
    dcre, dcim = _rows_unlayout(dcc)
    small["ssm_c_re"], small["ssm_c_im"] = _block_diag_out_t(dcre), -_block_diag_out_t(dcim)
    gg, da8 = _ssm_scan("ssm_scan_bwd", _scan_layout(sv["ar"], -sv["ai"]), dss, True, sv["ss"])
    du2 = _mm_nt("ssm_bu_dx", gg, sv["bd"], tk=1024)
    dbd = _mm_tn("ssm_bu_dw", sv["up"], gg, tn=1024)
    dbdr, dbdi = _scan_unlayout(dbd)
    dar8, dai8 = _scan_unlayout(da8)
    dlr, dli, dldt, dbr, dbi = _ssm_prep_bwd(*sv["ssm_in"], dar8, dai8, _block_diag_in_t(dbdr), _block_diag_in_t(dbdi))
    small["ssm_lam_re"], small["ssm_lam_im"] = dlr, dli
    small["ssm_log_dt"] = dldt.reshape(SSM_N_GROUPS, SSM_STATE).sum(axis=1)
    small["ssm_b_re"], small["ssm_b_im"] = _b_unrows(dbr), _b_unrows(dbi)
    dh_ssm = _time_unpermute(_add2("ssm_du", du1, du2, MM_DTYPE))
    dh = jnp.concatenate([dh_attn, dh_pool, dh_ssm], axis=1)
    dx_in = _mm_shard_cols_nt("in_proj_dx", dh, wg["w_in"], dep=mid2(dh))
    dw_in = _mm_shard_cols_tn("in_proj_dw", sv["xb"], dh, N_CHIPS)
    big = {"w_in": dw_in, "ssm_glu_w": dw_glu, "w_out": dw_out.reshape(N_CHIPS, MIX_WIDTH // N_CHIPS, D_MODEL),
           "ffn_w_up": dw_up, "ffn_conv_w": dconv_w, "ffn_w_down": dw_down.reshape(N_CHIPS, FS // 2, D_MODEL)}
    return dr1, dx_in, big, small


CONV_PAD = 2 * SUBLANES


def _halved(name, a):
    if name == "ffn_conv_w":
        a = jnp.pad(a, ((0, 0), (0, CONV_PAD - CONV_WIDTH), (0, 0)))
    return a.reshape(a.shape[0], 2, a.shape[1] // 2, a.shape[2])


def _unhalved(name, a):
    a = a.reshape(a.shape[:-3] + (2 * a.shape[-2], a.shape[-1]))
    return a[..., :CONV_WIDTH, :] if name == "ffn_conv_w" else a


class _Reduce:
    def __init__(self, tag, big, cm_idx):
        self.tag, self.cm_idx = tag, cm_idx
        g4 = [_halved(name, big[name]) for name in BIG]
        self.sems, self.hbm, self.token = _swap_start("grad_swap_start_" + tag, g4)

    def swapped(self, after):
        g4, got = _swap_wait("grad_swap_wait_" + self.tag, self.hbm, self.sems, after)
        parts, lands = zip(*[_pair_add("grad_pair_add", g, x, self.cm_idx) for g, x in zip(g4, got)])
        self.sems, self.hbm, self.token = _scatter_start("grad_scatter_start_" + self.tag, parts, lands)
        return self.token

    def scattered(self, after):
        _, recv = _scatter_wait("grad_scatter_wait_" + self.tag, self.hbm, self.sems, after)
        halves = [_sum_leading("grad_chip_sum", r) for r in recv]
        self.sems, self.hbm, self.token = _exchange_start("grad_exchange_start_" + self.tag, halves)
        return self.token

    def finish(self, after):
        return _exchange_wait("grad_exchange_wait_" + self.tag, self.hbm, self.sems, after)


def kernel(x, w_in, attn_sinks, pool_w, pool_scale, ssm_lam_re, ssm_lam_im, ssm_log_dt, ssm_b_re, ssm_b_im, ssm_c_re, ssm_c_im, ssm_d, ssm_glu_w, w_out, ln1_g, ln1_b, ffn_w_up, ffn_conv_w, ffn_conv_b, ffn_w_down, ln2_g, ln2_b, loss_target, m_w_in, m_attn_sinks, m_pool_w, m_pool_scale, m_ssm_lam_re, m_ssm_lam_im, m_ssm_log_dt, m_ssm_b_re, m_ssm_b_im, m_ssm_c_re, m_ssm_c_im, m_ssm_d, m_ssm_glu_w, m_w_out, m_ln1_g, m_ln1_b, m_ffn_w_up, m_ffn_conv_w, m_ffn_conv_b, m_ffn_w_down, m_ln2_g, m_ln2_b, v_w_in, v_attn_sinks, v_pool_w, v_pool_scale, v_ssm_lam_re, v_ssm_lam_im, v_ssm_log_dt, v_ssm_b_re, v_ssm_b_im, v_ssm_c_re, v_ssm_c_im, v_ssm_d, v_ssm_glu_w, v_w_out, v_ln1_g, v_ln1_b, v_ffn_w_up, v_ffn_conv_w, v_ffn_conv_b, v_ffn_w_down, v_ln2_g, v_ln2_b):
    w = dict(w_in=w_in, attn_sinks=attn_sinks, pool_w=pool_w, pool_scale=pool_scale, ssm_lam_re=ssm_lam_re,
             ssm_lam_im=ssm_lam_im, ssm_log_dt=ssm_log_dt, ssm_b_re=ssm_b_re, ssm_b_im=ssm_b_im, ssm_c_re=ssm_c_re,
             ssm_c_im=ssm_c_im, ssm_d=ssm_d, ssm_glu_w=ssm_glu_w, w_out=w_out, ln1_g=ln1_g, ln1_b=ln1_b, ffn_w_up=ffn_w_up,
             ffn_conv_w=ffn_conv_w, ffn_conv_b=ffn_conv_b, ffn_w_down=ffn_w_down, ln2_g=ln2_g, ln2_b=ln2_b)
    m = dict(w_in=m_w_in, attn_sinks=m_attn_sinks, pool_w=m_pool_w, pool_scale=m_pool_scale, ssm_lam_re=m_ssm_lam_re,
             ssm_lam_im=m_ssm_lam_im, ssm_log_dt=m_ssm_log_dt, ssm_b_re=m_ssm_b_re, ssm_b_im=m_ssm_b_im, ssm_c_re=m_ssm_c_re,
             ssm_c_im=m_ssm_c_im, ssm_d=m_ssm_d, ssm_glu_w=m_ssm_glu_w, w_out=m_w_out, ln1_g=m_ln1_g, ln1_b=m_ln1_b,
             ffn_w_up=m_ffn_w_up, ffn_conv_w=m_ffn_conv_w, ffn_conv_b=m_ffn_conv_b, ffn_w_down=m_ffn_w_down, ln2_g=m_ln2_g,
             ln2_b=m_ln2_b)
    v = dict(w_in=v_w_in, attn_sinks=v_attn_sinks, pool_w=v_pool_w, pool_scale=v_pool_scale, ssm_lam_re=v_ssm_lam_re,
             ssm_lam_im=v_ssm_lam_im, ssm_log_dt=v_ssm_log_dt, ssm_b_re=v_ssm_b_re, ssm_b_im=v_ssm_b_im, ssm_c_re=v_ssm_c_re,
             ssm_c_im=v_ssm_c_im, ssm_d=v_ssm_d, ssm_glu_w=v_ssm_glu_w, w_out=v_w_out, ln1_g=v_ln1_g, ln1_b=v_ln1_b,
             ffn_w_up=v_ffn_w_up, ffn_conv_w=v_ffn_conv_w, ffn_conv_b=v_ffn_conv_b, ffn_w_down=v_ffn_w_down, ln2_g=v_ln2_g,
             ln2_b=v_ln2_b)
    c_pos = lax.axis_index("c").astype(jnp.int32)
    chip = (2 * lax.axis_index("x") + lax.axis_index("y")).astype(jnp.int32)
    c_idx, chip_idx, cm_idx = c_pos.reshape(1), chip.reshape(1), jnp.stack([c_pos, chip])
    rope_t = _rope_tables()
    xs = x.reshape(SEQ, D_MODEL)
    xb = xs.astype(MM_DTYPE)
    wh, mh, vh = ({n: _halved(n, t[n]) for n in BIG} for t in (w, m, v))

    def place(l):
        return [_cast_place("place_" + n, wh[n], l, chip_idx, F32 if n == "ffn_conv_w" else MM_DTYPE) for n in BIG]

    sems, bufs, _ = _gather_start("gather_start_0", place(0))
    sems, bufs, _ = _gather_forward("gather_forward_0", bufs, sems, None)
    bufs = _gather_finish("gather_finish_0", bufs, sems, None)
    gathered, saved = [], []
    for l in range(DEPTH):
        gathered.append({n: _unhalved(n, g) for n, g in zip(BIG, bufs)})
        nxt, dep, mid = {}, None, lambda after: None
        if l + 1 < DEPTH:
            nxt["sems"], nxt["bufs"], dep = _gather_start("gather_start_%d" % (l + 1), place(l + 1))

            def mid(after):
                nxt["sems"], nxt["bufs"], token = _gather_forward("gather_forward_%d" % (l + 1), nxt["bufs"], nxt["sems"], after)
                return token

        xs, xb, sv = _layer_fwd(xs, xb, {n: w[n][l] for n in SMALL}, gathered[l], rope_t, dep, mid)
        saved.append(sv)
        if l + 1 < DEPTH:
            bufs = _gather_finish("gather_finish_%d" % (l + 1), nxt["bufs"], nxt["sems"], xb)
    dy, loss_tile = _loss(xs, loss_target.reshape(SEQ, D_MODEL))
    loss = lax.psum(loss_tile[0, 0], ("x", "y", "c"))

    big_out = {n: None for n in BIG}
    small_g = {n: [None] * DEPTH for n in SMALL}

    def update(l, own, got):
        for n, o, g in zip(BIG, own, got):
            big_out[n] = _adamw_big("adamw_" + n, l, c_idx, wh[n], mh[n], vh[n], o, g, big_out[n])

    da, db, red = dy, None, None
    for l in reversed(range(DEPTH)):
        hooks = (None, lambda after: None, lambda after: None) if red is None else (red.token, red.swapped, red.scattered)
        da, db, big, small = _layer_bwd(da, db, {n: w[n][l] for n in SMALL}, gathered[l], saved[l], rope_t, *hooks)
        for n in SMALL:
            small_g[n][l] = small[n].reshape(w[n].shape[1:])
        if red is not None:
            update(l + 1, *red.finish(db))
        red = _Reduce(str(l), big, cm_idx)
    red.swapped(None)
    red.scattered(None)
    update(0, *red.finish(None))
    grad_x = _ln_in_grad(da, db).reshape(x.shape)

    shapes = {n: w[n].shape for n in SMALL}
    part = _pack({n: jnp.stack(small_g[n]) for n in SMALL})
    g_small = _sum_leading("small_grad_sum", _gather_all(part))
    upd = _adamw("adamw_small", _pack(w), g_small, _pack(m), _pack(v))
    small_out = [_unpack(a, shapes) for a in (g_small,) + tuple(upd)]

    outs = [loss, grad_x]
    for kind in range(4):
        for n in ALL_W:
            if n in SMALL:
                outs.append(small_out[kind][n])
            else:
                outs.append(_unhalved(n, big_out[n][kind]))
    return tuple(outs)


def _ln_in_grad(dr1, dx_in):
    tm = _tile(SEQ, 512)

    def body(a_ref, b_ref, o_ref):
        o_ref[...] = DEEPNORM_ALPHA * a_ref[...] + b_ref[...]

    blk = pl.BlockSpec((tm, D_MODEL), lambda i: (i, 0))
    return pl.pallas_call(body, name="grad_x", grid=(SEQ // tm,), in_specs=[blk, blk], out_specs=blk,
                          out_shape=_sds((SEQ, D_MODEL)), compiler_params=_cparams(("parallel",)))(dr1, dx_in)
```

```python
import functools
import math

import jax
import jax.numpy as jnp
from jax import lax
from jax.experimental import pallas as pl
from jax.experimental.pallas import tpu as pltpu

F32 = jnp.float32
BF16 = jnp.bfloat16
MM_DTYPE = BF16

D_MODEL = 2048
SEQ = 2048
DEPTH = 4
D_FF = 5504
HEAD_DIM = 64
N_Q_HEADS = D_MODEL // 2 // HEAD_DIM
N_KV_HEADS = N_Q_HEADS // 4
ATTN_WIDTH = N_Q_HEADS * HEAD_DIM
KV_WIDTH = N_KV_HEADS * HEAD_DIM
ATTN_BLOCK = 128
ROPE_THETA = 10000.0
POOL_WINDOWS = (2, 4, 8, 16)
POOL_WIDTH = D_MODEL // 4
POOL_GROUP = POOL_WIDTH // len(POOL_WINDOWS)
SSM_WIDTH = D_MODEL // 4
SSM_GROUP = 16
SSM_N_GROUPS = SSM_WIDTH // SSM_GROUP
SSM_STATE = 64
SSM_CH = SSM_N_GROUPS * SSM_STATE
MIX_WIDTH = ATTN_WIDTH + POOL_WIDTH + SSM_WIDTH
IN_WIDTH = ATTN_WIDTH + 2 * KV_WIDTH + POOL_WIDTH + SSM_WIDTH
CONV_WIDTH = 3
LN_EPS = 1e-5
DEEPNORM_ALPHA = (2 * DEPTH) ** 0.25
ADAM_LR = 0.001
ADAM_B1 = 0.9
ADAM_B2 = 0.999
ADAM_EPS = 1e-08
ADAM_WD = 0.01
ADAM_STEP = 10

N_CHIPS = 4
N_DEV = 8
FS = 2 * D_FF // N_CHIPS
IN_S = IN_WIDTH // N_CHIPS
GLU_S = 2 * SSM_WIDTH // N_CHIPS
LANES = 128
SUBLANES = 8
SCAN_CW = 256
SCAN_NB = SSM_CH // SCAN_CW
VMEM_LIMIT = 56 * 1024 * 1024
COPY_BLOCK_BYTES = 6 * 1024 * 1024
NEG = -1e30

NN = (((1,), (0,)), ((), ()))
NT = (((1,), (1,)), ((), ()))
TN = (((0,), (0,)), ((), ()))
MESH = pl.DeviceIdType.MESH


def _tile(n, pref, mult=LANES):
    best = None
    for t in range(mult, min(n, pref) + 1, mult):
        if n % t == 0:
            best = t
    return n if best is None else best


def _cparams(sem):
    return pltpu.CompilerParams(dimension_semantics=sem, vmem_limit_bytes=VMEM_LIMIT)


def _sds(shape, dtype=F32):
    return jax.ShapeDtypeStruct(tuple(shape), dtype)


def _as_list(x):
    return [] if x is None else list(x) if isinstance(x, (list, tuple)) else [x]


def _mm(name, a, b, out_shape, grid, a_spec, b_spec, o_spec, dims, acc_shape, out_dtype=F32, dep=None):
    nk = grid[2]
    deps = _as_list(dep)

    def product(a_ref, b_ref):
        return lax.dot_general(a_ref[...].astype(MM_DTYPE), b_ref[...].astype(MM_DTYPE), dims, preferred_element_type=F32)

    def body_one(a_ref, b_ref, *rest):
        rest[-1][...] = product(a_ref, b_ref).astype(rest[-1].dtype)

    def body(a_ref, b_ref, *rest):
        o_ref, acc_ref = rest[-2:]
        k = pl.program_id(2)

        @pl.when(k == 0)
        def _():
            acc_ref[...] = product(a_ref, b_ref)

        @pl.when(k > 0)
        def _():
            acc_ref[...] += product(a_ref, b_ref)

        @pl.when(k == nk - 1)
        def _():
            o_ref[...] = acc_ref[...].astype(o_ref.dtype)

    return pl.pallas_call(
        body_one if nk == 1 else body, name=name, grid=grid, in_specs=[a_spec, b_spec] + [ANY] * len(deps),
        out_specs=o_spec, out_shape=_sds(out_shape, out_dtype),
        scratch_shapes=[] if nk == 1 else [pltpu.VMEM(acc_shape, F32)],
        compiler_params=_cparams(("parallel", "parallel", "arbitrary")))(a, b, *deps)


def _mm_nn(name, a, b, tm=2048, tn=512, tk=2048, out_dtype=F32):
    m, kk = a.shape
    n = b.shape[1]
    tm, tn, tk = _tile(m, tm), _tile(n, tn), _tile(kk, tk)
    return _mm(name, a, b, (m, n), (m // tm, n // tn, kk // tk),
               pl.BlockSpec((tm, tk), lambda i, j, k: (i, k)), pl.BlockSpec((tk, tn), lambda i, j, k: (k, j)),
               pl.BlockSpec((tm, tn), lambda i, j, k: (i, j)), NN, (tm, tn), out_dtype)


def _mm_nt(name, a, b, tm=2048, tn=512, tk=2048):
    m, kk = a.shape
    n = b.shape[0]
    tm, tn, tk = _tile(m, tm), _tile(n, tn), _tile(kk, tk)
    return _mm(name, a, b, (m, n), (m // tm, n // tn, kk // tk),
               pl.BlockSpec((tm, tk), lambda i, j, k: (i, k)), pl.BlockSpec((tn, tk), lambda i, j, k: (j, k)),
               pl.BlockSpec((tm, tn), lambda i, j, k: (i, j)), NT, (tm, tn))


def _mm_tn(name, a, b, tm=1024, tn=1024, ts=2048):
    s, m = a.shape
    n = b.shape[1]
    tm, tn, ts = _tile(m, tm), _tile(n, tn), _tile(s, ts)
    return _mm(name, a, b, (m, n), (m // tm, n // tn, s // ts),
               pl.BlockSpec((ts, tm), lambda i, j, k: (k, i)), pl.BlockSpec((ts, tn), lambda i, j, k: (k, j)),
               pl.BlockSpec((tm, tn), lambda i, j, k: (i, j)), TN, (tm, tn))


def _mm_shard_cols(name, a, w, tm=2048, tk=2048, dep=None):
    m, kk = a.shape
    nj, _, c = w.shape
    tm, tk = _tile(m, tm), _tile(kk, tk)
    return _mm(name, a, w, (m, nj * c), (m // tm, nj, kk // tk),
               pl.BlockSpec((tm, tk), lambda i, j, k: (i, k)), pl.BlockSpec((None, tk, c), lambda i, j, k: (j, k, 0)),
               pl.BlockSpec((tm, c), lambda i, j, k: (i, j)), NN, (tm, c), dep=dep)


def _mm_shard_cols_nt(name, d, w, tm=2048, tn=512, dep=None):
    m = d.shape[0]
    nj, n, c = w.shape
    tm, tn = _tile(m, tm), _tile(n, tn)
    return _mm(name, d, w, (m, n), (m // tm, n // tn, nj),
               pl.BlockSpec((tm, c), lambda i, j, k: (i, k)), pl.BlockSpec((None, tn, c), lambda i, j, k: (k, j, 0)),
               pl.BlockSpec((tm, tn), lambda i, j, k: (i, j)), NT, (tm, tn), dep=dep)


def _mm_shard_cols_tn(name, a, d, nj, tm=1024, ts=2048):
    s, m = a.shape
    c = d.shape[1] // nj
    tm, ts = _tile(m, tm), _tile(s, ts)
    return _mm(name, a, d, (nj, m, c), (nj, m // tm, s // ts),
               pl.BlockSpec((ts, tm), lambda j, i, k: (k, i)), pl.BlockSpec((ts, c), lambda j, i, k: (k, j)),
               pl.BlockSpec((None, tm, c), lambda j, i, k: (j, i, 0)), TN, (tm, c))


def _ffn_up(x1, w_up_t, tm=512, tk=2048, dep=None):
    s, d = x1.shape
    tm, tk = _tile(s, tm), _tile(d, tk)
    return _mm("ffn_up", x1, w_up_t, (N_CHIPS, s, FS), (N_CHIPS, s // tm, d // tk),
               pl.BlockSpec((tm, tk), lambda j, i, k: (i, k)), pl.BlockSpec((None, FS, tk), lambda j, i, k: (j, 0, k)),
               pl.BlockSpec((None, tm, FS), lambda j, i, k: (j, i, 0)), NT, (tm, FS), dep=dep)


def _ffn_down(act, w_down, tm=1024, tn=512):
    _, s, _ = act.shape
    d = w_down.shape[2]
    tm, tn = _tile(s, tm), _tile(d, tn)
    return _mm("ffn_down", act, w_down, (s, d), (s // tm, d // tn, 2),
               pl.BlockSpec((None, tm, FS), lambda i, j, k: (k, i, 0)), pl.BlockSpec((None, FS, tn), lambda i, j, k: (k, 0, j)),
               pl.BlockSpec((tm, tn), lambda i, j, k: (i, j)), NN, (tm, tn))


def _ffn_down_dact(df, w_down, tm=512, tk=2048):
    s, d = df.shape
    tm, tk = _tile(s, tm), _tile(d, tk)
    return _mm("ffn_down_dact", df, w_down, (2, s, FS), (2, s // tm, d // tk),
               pl.BlockSpec((tm, tk), lambda j, i, k: (i, k)), pl.BlockSpec((None, FS, tk), lambda j, i, k: (j, 0, k)),
               pl.BlockSpec((None, tm, FS), lambda j, i, k: (j, i, 0)), NT, (tm, FS))


def _ffn_down_dw(act, df, tn=512, ts=2048):
    _, s, _ = act.shape
    d = df.shape[1]
    tn, ts = _tile(d, tn), _tile(s, ts)
    return _mm("ffn_down_dw", act, df, (2, FS, d), (2, d // tn, s // ts),
               pl.BlockSpec((None, ts, FS), lambda p, j, k: (p, k, 0)), pl.BlockSpec((ts, tn), lambda p, j, k: (k, j)),
               pl.BlockSpec((None, FS, tn), lambda p, j, k: (p, 0, j)), TN, (FS, tn))


def _ffn_up_dx(dh, w_up_t, tm=1024, tn=1024):
    s = dh.shape[2]
    d = w_up_t.shape[2]
    tm, tn = _tile(s, tm), _tile(d, tn)
    return _mm("ffn_up_dx", dh, w_up_t, (s, d), (s // tm, d // tn, N_CHIPS),
               pl.BlockSpec((None, None, tm, FS), lambda i, j, k: (k % 2, k // 2, i, 0)),
               pl.BlockSpec((None, FS, tn), lambda i, j, k: (k, 0, j)),
               pl.BlockSpec((tm, tn), lambda i, j, k: (i, j)), NN, (tm, tn))


def _ffn_up_dw(x1, dh, tn=512, ts=2048):
    s, d = x1.shape
    tn, ts = _tile(d, tn), _tile(s, ts)
    return _mm("ffn_up_dw", dh, x1, (N_CHIPS, FS, d), (N_CHIPS, d // tn, s // ts),
               pl.BlockSpec((None, None, ts, FS), lambda j, i, k: (j % 2, j // 2, k, 0)),
               pl.BlockSpec((ts, tn), lambda j, i, k: (k, i)),
               pl.BlockSpec((None, FS, tn), lambda j, i, k: (j, 0, i)), TN, (FS, tn))


def _rope_tables():
    half = HEAD_DIM // 2
    inv = ROPE_THETA ** (-jnp.arange(half, dtype=F32) / half)
    ang = jnp.arange(SEQ).astype(F32)[:, None] * inv[None, :]
    cos, sin = jnp.cos(ang), jnp.sin(ang)
    cos_t = jnp.tile(cos, (1, LANES // half))
    sin_t = jnp.tile(jnp.concatenate([-sin, sin], axis=1), (1, LANES // HEAD_DIM))
    return cos_t, sin_t


def _rotate_half(t):
    lane = lax.broadcasted_iota(jnp.int32, t.shape, 1)
    first = (lane % HEAD_DIM) < (HEAD_DIM // 2)
    return jnp.where(first, pltpu.roll(t, LANES - HEAD_DIM // 2, 1), pltpu.roll(t, HEAD_DIM // 2, 1))


def _rope(name, src, col_tile0, n_tiles, cos_t, sin_t, out_dtype):
    tm = _tile(SEQ, 512)

    def body(x_ref, c_ref, s_ref, o_ref):
        t = x_ref[...].astype(F32)
        o_ref[...] = (t * c_ref[...] + _rotate_half(t) * s_ref[...]).astype(o_ref.dtype)

    return pl.pallas_call(
        body, name=name, grid=(SEQ // tm, n_tiles),
        in_specs=[pl.BlockSpec((tm, LANES), lambda i, j: (i, j + col_tile0)),
                  pl.BlockSpec((tm, LANES), lambda i, j: (i, 0)), pl.BlockSpec((tm, LANES), lambda i, j: (i, 0))],
        out_specs=pl.BlockSpec((tm, LANES), lambda i, j: (i, j)),
        out_shape=_sds((SEQ, n_tiles * LANES), out_dtype),
        compiler_params=_cparams(("parallel", "parallel")))(src, cos_t, sin_t)


Q_TILES = ATTN_WIDTH // LANES
KV_TILES = KV_WIDTH // LANES
Q_PER_KV_TILE = Q_TILES // KV_TILES
HEADS_PER_KV_TILE = N_Q_HEADS // KV_TILES
K_TILE0 = ATTN_WIDTH // LANES
V_TILE0 = (ATTN_WIDTH + KV_WIDTH) // LANES
N_QBLK = SEQ // ATTN_BLOCK


def _dup_half(t, which):
    lane = lax.broadcasted_iota(jnp.int32, t.shape, 1)
    r = pltpu.roll(t, HEAD_DIM, 1)
    lo = lane < HEAD_DIM
    return jnp.where(lo, t, r) if which == 0 else jnp.where(lo, r, t)


def _attn_masks(n):
    row = lax.broadcasted_iota(jnp.int32, (ATTN_BLOCK, ATTN_BLOCK), 0)
    col = lax.broadcasted_iota(jnp.int32, (ATTN_BLOCK, ATTN_BLOCK), 1)
    return col <= row, (col > row) & (n > 0), col < HEAD_DIM


def _attn_probs(qm, k2c, k2p, cur_ok, prev_ok, sink):
    scale = HEAD_DIM ** -0.5
    sc = lax.dot_general(qm, k2c, NT, preferred_element_type=F32) * scale
    sp = lax.dot_general(qm, k2p, NT, preferred_element_type=F32) * scale
    sc = jnp.where(cur_ok, sc, NEG)
    sp = jnp.where(prev_ok, sp, NEG)
    m = jnp.maximum(jnp.maximum(sc.max(1, keepdims=True), sp.max(1, keepdims=True)), sink)
    pc, pp = jnp.exp(sc - m), jnp.exp(sp - m)
    esink = jnp.exp(sink - m)
    inv = 1.0 / (pc.sum(1, keepdims=True) + pp.sum(1, keepdims=True) + esink)
    return pc * inv, pp * inv, esink * inv


def _attn_specs():
    blk = (ATTN_BLOCK, LANES)
    wide = (ATTN_BLOCK, Q_PER_KV_TILE * LANES)
    prev = lambda n: jnp.maximum(n - 1, 0)
    q_spec = pl.BlockSpec(wide, lambda t, n: (n, t))
    kc = pl.BlockSpec(blk, lambda t, n: (n, K_TILE0 + t))
    kp = pl.BlockSpec(blk, lambda t, n: (prev(n), K_TILE0 + t))
    vc = pl.BlockSpec(blk, lambda t, n: (n, V_TILE0 + t))
    vp = pl.BlockSpec(blk, lambda t, n: (prev(n), V_TILE0 + t))
    return q_spec, kc, kp, vc, vp, pl.BlockSpec(memory_space=pltpu.SMEM)


def _attn_fwd(qk, h, sinks):
    q_spec, kc_s, kp_s, vc_s, vp_s, smem = _attn_specs()

    def body(sink_ref, q_ref, kc_ref, kp_ref, vc_ref, vp_ref, o_ref, ob_ref):
        t, n = pl.program_id(0), pl.program_id(1)
        cur_ok, prev_ok, lo = _attn_masks(n)
        kc, kp = kc_ref[...].astype(F32), kp_ref[...].astype(F32)
        vc, vp = vc_ref[...], vp_ref[...]
        for kvl in range(2):
            k2c, k2p = _dup_half(kc, kvl).astype(MM_DTYPE), _dup_half(kp, kvl).astype(MM_DTYPE)
            v2c, v2p = _dup_half(vc, kvl).astype(MM_DTYPE), _dup_half(vp, kvl).astype(MM_DTYPE)
            for a in (2 * kvl, 2 * kvl + 1):
                qt = q_ref[:, a * LANES:(a + 1) * LANES].astype(F32)
                outs = []
                for hs in range(2):
                    qm = jnp.where(lo == (hs == 0), qt, 0.0).astype(MM_DTYPE)
                    sink = sink_ref[t * HEADS_PER_KV_TILE + 2 * a + hs]
                    pc, pp, _ = _attn_probs(qm, k2c, k2p, cur_ok, prev_ok, sink)
                    outs.append(lax.dot_general(pc.astype(MM_DTYPE), v2c, NN, preferred_element_type=F32)
                                + lax.dot_general(pp.astype(MM_DTYPE), v2p, NN, preferred_element_type=F32))
                o = jnp.where(lo, outs[0], outs[1])
                o_ref[:, a * LANES:(a + 1) * LANES] = o
                ob_ref[:, a * LANES:(a + 1) * LANES] = o.astype(ob_ref.dtype)

    return pl.pallas_call(
        body, name="attn_fwd", grid=(KV_TILES, N_QBLK),
        in_specs=[smem, q_spec, kc_s, kp_s, vc_s, vp_s], out_specs=[q_spec, q_spec],
        out_shape=[_sds((SEQ, ATTN_WIDTH)), _sds((SEQ, ATTN_WIDTH), MM_DTYPE)],
        compiler_params=_cparams(("parallel", "parallel")))(sinks, qk, qk, qk, h, h)


def _attn_bwd(qk, h, sinks, y, dy, dy_tile0):
    q_spec, kc_s, kp_s, vc_s, vp_s, smem = _attn_specs()
    blk = (ATTN_BLOCK, LANES)
    wide = (ATTN_BLOCK, Q_PER_KV_TILE * LANES)
    kv_out = pl.BlockSpec(blk, lambda t, n: (n, t))
    dy_spec = pl.BlockSpec(wide, lambda t, n: (n, t + dy_tile0))

    def body(sink_ref, q_ref, kc_ref, kp_ref, vc_ref, vp_ref, y_ref, dy_ref,
             dq_ref, dkc_ref, dkp_ref, dvc_ref, dvp_ref, dsk_ref):
        t, n = pl.program_id(0), pl.program_id(1)
        cur_ok, prev_ok, lo = _attn_masks(n)
        scale = HEAD_DIM ** -0.5
        kc, kp = kc_ref[...].astype(F32), kp_ref[...].astype(F32)
        vc, vp = vc_ref[...], vp_ref[...]
        hrow = lax.broadcasted_iota(jnp.int32, (HEADS_PER_KV_TILE, LANES), 0)
        dsk = jnp.zeros((HEADS_PER_KV_TILE, LANES), F32)
        folded = []
        for kvl in range(2):
            k2c, k2p = _dup_half(kc, kvl).astype(MM_DTYPE), _dup_half(kp, kvl).astype(MM_DTYPE)
            v2c, v2p = _dup_half(vc, kvl).astype(MM_DTYPE), _dup_half(vp, kvl).astype(MM_DTYPE)
            acc = [jnp.zeros(blk, F32) for _ in range(4)]
            for a in (2 * kvl, 2 * kvl + 1):
                sl = slice(a * LANES, (a + 1) * LANES)
                qt = q_ref[:, sl].astype(F32)
                dot_, yt = dy_ref[:, sl], y_ref[:, sl]
                dqs = []
                for hs in range(2):
                    hm = lo == (hs == 0)
                    qm = jnp.where(hm, qt, 0.0).astype(MM_DTYPE)
                    dom = jnp.where(hm, dot_, 0.0).astype(MM_DTYPE)
                    hl = 2 * a + hs
                    sink = sink_ref[t * HEADS_PER_KV_TILE + hl]
                    pc, pp, psink = _attn_probs(qm, k2c, k2p, cur_ok, prev_ok, sink)
                    delta = jnp.sum(jnp.where(hm, dot_ * yt, 0.0), axis=1, keepdims=True)
                    dpc = lax.dot_general(dom, v2c, NT, preferred_element_type=F32)
                    dpp = lax.dot_general(dom, v2p, NT, preferred_element_type=F32)
                    dsc = (pc * (dpc - delta) * scale).astype(MM_DTYPE)
                    dsp = (pp * (dpp - delta) * scale).astype(MM_DTYPE)
                    dqs.append(lax.dot_general(dsc, k2c, NN, preferred_element_type=F32)
                               + lax.dot_general(dsp, k2p, NN, preferred_element_type=F32))
                    acc[0] += lax.dot_general(dsc, qm, TN, preferred_element_type=F32)
                    acc[1] += lax.dot_general(dsp, qm, TN, preferred_element_type=F32)
                    acc[2] += lax.dot_general(pc.astype(MM_DTYPE), dom, TN, preferred_element_type=F32)
                    acc[3] += lax.dot_general(pp.astype(MM_DTYPE), dom, TN, preferred_element_type=F32)
                    dsk = dsk + jnp.where(hrow == hl, -jnp.sum(psink * delta), 0.0)
                dq_ref[:, sl] = jnp.where(lo, dqs[0], dqs[1])
            folded.append([x + pltpu.roll(x, HEAD_DIM, 1) for x in acc])
        for o_ref, i in ((dkc_ref, 0), (dkp_ref, 1), (dvc_ref, 2), (dvp_ref, 3)):
            o_ref[...] = jnp.where(lo, folded[0][i], folded[1][i])

        @pl.when(n == 0)
        def _():
            dsk_ref[...] = jnp.zeros_like(dsk_ref)

        dsk_ref[...] += dsk

    kv_shape = _sds((SEQ, KV_WIDTH))
    return pl.pallas_call(
        body, name="attn_bwd", grid=(KV_TILES, N_QBLK),
        in_specs=[smem, q_spec, kc_s, kp_s, vc_s, vp_s, q_spec, dy_spec],
        out_specs=[q_spec, kv_out, kv_out, kv_out, kv_out,
                   pl.BlockSpec((None, HEADS_PER_KV_TILE, LANES), lambda t, n: (t, 0, 0))],
        out_shape=[_sds((SEQ, ATTN_WIDTH)), kv_shape, kv_shape, kv_shape, kv_shape,
                   _sds((KV_TILES, HEADS_PER_KV_TILE, LANES))],
        compiler_params=_cparams(("parallel", "arbitrary")))(sinks, qk, qk, qk, h, h, y, dy)


def _attn_dh(dq, dkc, dkp, dvc, dvp, cos_t, nsin_t):
    n_tiles = Q_TILES + 2 * KV_TILES
    nxt = lambda n: jnp.minimum(n + 1, N_QBLK - 1)

    def body(dq_ref, kc_ref, kp_ref, vc_ref, vp_ref, c_ref, s_ref, o_ref):
        has_next = pl.program_id(0) < N_QBLK - 1
        cos, sin = c_ref[...], s_ref[...]

        def unrope(t):
            return t * cos + _rotate_half(t) * sin

        for j in range(Q_TILES):
            sl = slice(j * LANES, (j + 1) * LANES)
            o_ref[:, sl] = unrope(dq_ref[:, sl]).astype(o_ref.dtype)
        for j in range(KV_TILES):
            sl = slice(j * LANES, (j + 1) * LANES)
            t = kc_ref[:, sl] + jnp.where(has_next, kp_ref[:, sl], 0.0)
            o_ref[:, ATTN_WIDTH + j * LANES:ATTN_WIDTH + (j + 1) * LANES] = unrope(t).astype(o_ref.dtype)
        o_ref[:, ATTN_WIDTH + KV_WIDTH:] = (vc_ref[...] + jnp.where(has_next, vp_ref[...], 0.0)).astype(o_ref.dtype)

    qb, kb, tb = (ATTN_BLOCK, ATTN_WIDTH), (ATTN_BLOCK, KV_WIDTH), (ATTN_BLOCK, LANES)
    return pl.pallas_call(
        body, name="attn_dh", grid=(N_QBLK,),
        in_specs=[pl.BlockSpec(qb, lambda n: (n, 0)),
                  pl.BlockSpec(kb, lambda n: (n, 0)), pl.BlockSpec(kb, lambda n: (nxt(n), 0)),
                  pl.BlockSpec(kb, lambda n: (n, 0)), pl.BlockSpec(kb, lambda n: (nxt(n), 0)),
                  pl.BlockSpec(tb, lambda n: (n, 0)), pl.BlockSpec(tb, lambda n: (n, 0))],
        out_specs=pl.BlockSpec((ATTN_BLOCK, n_tiles * LANES), lambda n: (n, 0)),
        out_shape=_sds((SEQ, n_tiles * LANES), MM_DTYPE),
        compiler_params=_cparams(("parallel",)))(dq, dkc, dkp, dvc, dvp, cos_t, nsin_t)


POOL_TILE0 = (ATTN_WIDTH + 2 * KV_WIDTH) // POOL_WIDTH


def _shift_rows(x, d, down):
    n = x.shape[0]
    row = lax.broadcasted_iota(jnp.int32, x.shape, 0)
    if down:
        return jnp.where(row >= d, pltpu.roll(x, d, 0), 0.0)
    return jnp.where(row < n - d, pltpu.roll(x, n - d, 0), 0.0)


def _window_sum(x, w, down):
    d = 1
    while d < w:
        x = x + _shift_rows(x, d, down)
        d *= 2
    return x


def _pool_z(u, w):
    t = lax.broadcasted_iota(jnp.int32, u.shape, 0).astype(F32)
    cnt = jnp.minimum(t + 1.0, float(w))
    return _window_sum(u, w, True) / cnt - u, cnt


def _pool_fwd(h, pool_w, pool_scale):
    def body(u_ref, w_ref, s_ref, o_ref):
        for gi, w in enumerate(POOL_WINDOWS):
            sl = slice(gi * POOL_GROUP, (gi + 1) * POOL_GROUP)
            z, _ = _pool_z(u_ref[:, sl], w)
            o_ref[:, sl] = (lax.dot_general(z.astype(MM_DTYPE), w_ref[gi].astype(MM_DTYPE), NN,
                                            preferred_element_type=F32) * s_ref[:, sl]).astype(o_ref.dtype)

    return pl.pallas_call(
        body, name="pool_fwd", grid=(1,),
        in_specs=[pl.BlockSpec((SEQ, POOL_WIDTH), lambda i: (0, POOL_TILE0)),
                  pl.BlockSpec(pool_w.shape, lambda i: (0, 0, 0)), pl.BlockSpec((1, POOL_WIDTH), lambda i: (0, 0))],
        out_specs=pl.BlockSpec((SEQ, POOL_WIDTH), lambda i: (0, 0)),
        out_shape=_sds((SEQ, POOL_WIDTH), MM_DTYPE), compiler_params=_cparams(("arbitrary",)))(h, pool_w, pool_scale)


def _pool_bwd(h, pool_w, pool_scale, dmix, dy_tile0):
    def body(u_ref, w_ref, s_ref, dy_ref, du_ref, dw_ref, ds_ref):
        for gi, w in enumerate(POOL_WINDOWS):
            sl = slice(gi * POOL_GROUP, (gi + 1) * POOL_GROUP)
            z, cnt = _pool_z(u_ref[:, sl], w)
            zb, wb = z.astype(MM_DTYPE), w_ref[gi].astype(MM_DTYPE)
            dy = dy_ref[:, sl]
            zp = lax.dot_general(zb, wb, NN, preferred_element_type=F32)
            ds_ref[:, sl] = jnp.sum(dy * zp, axis=0, keepdims=True)
            dyo = (dy * s_ref[:, sl]).astype(MM_DTYPE)
            dw_ref[gi] = lax.dot_general(zb, dyo, TN, preferred_element_type=F32)
            dz = lax.dot_general(dyo, wb, NT, preferred_element_type=F32)
            du_ref[:, sl] = (_window_sum(dz / cnt, w, False) - dz).astype(du_ref.dtype)

    return pl.pallas_call(
        body, name="pool_bwd", grid=(1,),
        in_specs=[pl.BlockSpec((SEQ, POOL_WIDTH), lambda i: (0, POOL_TILE0)),
                  pl.BlockSpec(pool_w.shape, lambda i: (0, 0, 0)), pl.BlockSpec((1, POOL_WIDTH), lambda i: (0, 0)),
                  pl.BlockSpec((SEQ, POOL_WIDTH), lambda i: (0, dy_tile0))],
        out_specs=[pl.BlockSpec((SEQ, POOL_WIDTH), lambda i: (0, 0)), pl.BlockSpec(pool_w.shape, lambda i: (0, 0, 0)),
                   pl.BlockSpec((1, POOL_WIDTH), lambda i: (0, 0))],
        out_shape=[_sds((SEQ, POOL_WIDTH), MM_DTYPE), _sds(pool_w.shape), _sds((1, POOL_WIDTH))],
        compiler_params=_cparams(("arbitrary",)))(h, pool_w, pool_scale, dmix)


def _ssm_discretize(lr, li, ldt, br, bi):
    dt = jnp.exp(ldt)
    mag = jnp.exp(lr * dt)
    ar, ai = mag * jnp.cos(li * dt), mag * jnp.sin(li * dt)
    nr, ni = ar - 1.0, ai
    den = lr * lr + li * li
    zr = (nr * lr + ni * li) / den
    zi = (ni * lr - nr * li) / den
    return ar, ai, zr * br - zi * bi, zr * bi + zi * br


def _ssm_prep(lr, li, ldt, br, bi):
    def body(lr_ref, li_ref, ldt_ref, br_ref, bi_ref, ar_ref, ai_ref, bbr_ref, bbi_ref):
        outs = _ssm_discretize(lr_ref[...], li_ref[...], ldt_ref[...], br_ref[...], bi_ref[...])
        for o, v in zip((ar_ref, ai_ref, bbr_ref, bbi_ref), outs):
            o[...] = v

    row, mat = _sds((1, SSM_CH)), _sds((SSM_GROUP, SSM_CH))
    return pl.pallas_call(body, name="ssm_prep", out_shape=[row, row, mat, mat])(lr, li, ldt, br, bi)


def _ssm_prep_bwd(lr, li, ldt, br, bi, dar8, dai8, dbbr, dbbi):
    def body(lr_ref, li_ref, ldt_ref, br_ref, bi_ref, dar_ref, dai_ref, dbbr_ref, dbbi_ref, *outs):
        args = (lr_ref[...], li_ref[...], ldt_ref[...], br_ref[...], bi_ref[...])
        _, vjp = jax.vjp(_ssm_discretize, *args)
        cot = (jnp.sum(dar_ref[...], axis=0, keepdims=True), jnp.sum(dai_ref[...], axis=0, keepdims=True),
               dbbr_ref[...], dbbi_ref[...])
        for o, v in zip(outs, vjp(cot)):
            o[...] = v

    row, mat = _sds((1, SSM_CH)), _sds((SSM_GROUP, SSM_CH))
    return pl.pallas_call(body, name="ssm_prep_bwd", out_shape=[row, row, row, mat, mat])(
        lr, li, ldt, br, bi, dar8, dai8, dbbr, dbbi)


def _scan_layout(re, im):
    r = re.shape[0]
    return jnp.stack([re.reshape(r, SCAN_NB, SCAN_CW), im.reshape(r, SCAN_NB, SCAN_CW)], axis=2).reshape(r, 2 * SSM_CH)


def _scan_unlayout(x):
    r = x.shape[0]
    x = x.reshape(r, SCAN_NB, 2, SCAN_CW)
    return x[:, :, 0].reshape(r, SSM_CH), x[:, :, 1].reshape(r, SSM_CH)


def _time_permute(u):
    s, c = u.shape
    return u.reshape(SUBLANES, s // SUBLANES, c).transpose(1, 0, 2).reshape(s, c)


def _time_unpermute(u):
    s, c = u.shape
    return u.reshape(s // SUBLANES, SUBLANES, c).transpose(1, 0, 2).reshape(s, c)


def _ssm_scan(name, a_vec, x, reverse, s_prev=None):
    nsteps = SEQ // SUBLANES
    cw = SCAN_CW
    with_da = s_prev is not None

    def body(a_ref, x_ref, *rest):
        if with_da:
            s_ref, o_ref, da_ref = rest
        else:
            o_ref, = rest
        ar = jnp.broadcast_to(a_ref[:, :cw], (SUBLANES, cw))
        ai = jnp.broadcast_to(a_ref[:, cw:], (SUBLANES, cw))
        seg = lax.broadcasted_iota(jnp.int32, (SUBLANES, cw), 0)

        def toward(v):
            if reverse:
                return jnp.where(seg < SUBLANES - 1, pltpu.roll(v, SUBLANES - 1, 0), 0.0)
            return jnp.where(seg >= 1, pltpu.roll(v, 1, 0), 0.0)

        def rows(j):
            jj = nsteps - 1 - j if reverse else j
            return pl.ds(pl.multiple_of(jj * SUBLANES, SUBLANES), SUBLANES)

        def cmul(pr, pi, qr, qi):
            return pr * qr - pi * qi, pr * qi + pi * qr

        def local(j, c):
            sr, si = c
            r = rows(j)
            mr, mi = cmul(ar, ai, sr, si)
            return mr + x_ref[r, :cw], mi + x_ref[r, cw:]

        zero = jnp.zeros((SUBLANES, cw), F32)
        fr, fi = lax.fori_loop(0, nsteps, local, (zero, zero))

        def power(_, c):
            return cmul(ar, ai, *c)

        pr, pi = lax.fori_loop(0, nsteps - 1, power, (ar, ai))
        tr, ti = fr, fi
        for _ in range(SUBLANES - 1):
            mr, mi = cmul(pr, pi, toward(tr), toward(ti))
            tr, ti = fr + mr, fi + mi
        init = (toward(tr), toward(ti))

        def full(j, c):
            r = rows(j)
            if with_da:
                sr, si, dar, dai = c
            else:
                sr, si = c
            mr, mi = cmul(ar, ai, sr, si)
            sr, si = mr + x_ref[r, :cw], mi + x_ref[r, cw:]
            o_ref[r, :cw] = sr
            o_ref[r, cw:] = si
            if not with_da:
                return sr, si
            jj = nsteps - 1 - j
            rp = pl.ds(pl.multiple_of(jnp.maximum(jj - 1, 0) * SUBLANES, SUBLANES), SUBLANES)
            last = pl.ds((nsteps - 1) * SUBLANES, SUBLANES)
            first = jj == 0
            spr = jnp.where(first, jnp.where(seg >= 1, pltpu.roll(s_ref[last, :cw], 1, 0), 0.0), s_ref[rp, :cw])
            spi = jnp.where(first, jnp.where(seg >= 1, pltpu.roll(s_ref[last, cw:], 1, 0), 0.0), s_ref[rp, cw:])
            return sr, si, dar + sr * spr + si * spi, dai + si * spr - sr * spi

        if with_da:
            _, _, dar, dai = lax.fori_loop(0, nsteps, full, init + (zero, zero))
            da_ref[:, :cw] = dar
            da_ref[:, cw:] = dai
        else:
            lax.fori_loop(0, nsteps, full, init)

    blk = pl.BlockSpec((SEQ, 2 * cw), lambda b: (0, b))
    a_spec = pl.BlockSpec((1, 2 * cw), lambda b: (0, b))
    in_specs, args = [a_spec, blk], [a_vec, x]
    out_specs, out_shape = blk, _sds((SEQ, 2 * SSM_CH))
    if with_da:
        in_specs, args = in_specs + [blk], args + [s_prev]
        out_specs = [blk, pl.BlockSpec((SUBLANES, 2 * cw), lambda b: (0, b))]
        out_shape = [out_shape, _sds((SUBLANES, 2 * SSM_CH))]
    return pl.pallas_call(body, name=name, grid=(SCAN_NB,), in_specs=in_specs, out_specs=out_specs,
                          out_shape=out_shape, compiler_params=_cparams(("parallel",)))(*args)


def _ssm_gelu(yp, up, dvec):
    tm = _tile(SEQ, 512)

    def body(y_ref, u_ref, d_ref, yf_ref, g_ref):
        yf = y_ref[...] + d_ref[...] * u_ref[...]
        yf_ref[...] = yf
        g_ref[...] = jax.nn.gelu(yf).astype(g_ref.dtype)

    blk = pl.BlockSpec((tm, SSM_WIDTH), lambda i: (i, 0))
    row = pl.BlockSpec((1, SSM_WIDTH), lambda i: (0, 0))
    return pl.pallas_call(body, name="ssm_gelu", grid=(SEQ // tm,), in_specs=[blk, blk, row], out_specs=[blk, blk],
                          out_shape=[_sds((SEQ, SSM_WIDTH)), _sds((SEQ, SSM_WIDTH), MM_DTYPE)],
                          compiler_params=_cparams(("parallel",)))(yp, up, dvec)


def _ssm_gelu_bwd(yf, dgy, up, dvec):
    tm = _tile(SEQ, 512)

    def body(yf_ref, dg_ref, u_ref, d_ref, dyf_ref, du_ref, dd_ref):
        _, vjp = jax.vjp(jax.nn.gelu, yf_ref[...])
        dyf, = vjp(dg_ref[...])
        dyf_ref[...] = dyf.astype(dyf_ref.dtype)
        du_ref[...] = d_ref[...] * dyf

        @pl.when(pl.program_id(0) == 0)
        def _():
            dd_ref[...] = jnp.zeros_like(dd_ref)

        dd_ref[...] += jnp.sum(dyf * u_ref[...], axis=0, keepdims=True)

    blk = pl.BlockSpec((tm, SSM_WIDTH), lambda i: (i, 0))
    row = pl.BlockSpec((1, SSM_WIDTH), lambda i: (0, 0))
    return pl.pallas_call(body, name="ssm_gelu_bwd", grid=(SEQ // tm,), in_specs=[blk, blk, blk, row],
                          out_specs=[blk, blk, row],
                          out_shape=[_sds((SEQ, SSM_WIDTH), MM_DTYPE), _sds((SEQ, SSM_WIDTH)), _sds((1, SSM_WIDTH))],
                          compiler_params=_cparams(("arbitrary",)))(yf, dgy, up, dvec)


def _glu(ab):
    return ab[:, :SSM_WIDTH] * jax.nn.sigmoid(ab[:, SSM_WIDTH:])


def _ssm_glu(ab):
    tm = _tile(SEQ, 512)

    def body(ab_ref, o_ref):
        o_ref[...] = _glu(ab_ref[...]).astype(o_ref.dtype)

    return pl.pallas_call(body, name="ssm_glu", grid=(SEQ // tm,),
                          in_specs=[pl.BlockSpec((tm, 2 * SSM_WIDTH), lambda i: (i, 0))],
                          out_specs=pl.BlockSpec((tm, SSM_WIDTH), lambda i: (i, 0)),
                          out_shape=_sds((SEQ, SSM_WIDTH), MM_DTYPE), compiler_params=_cparams(("parallel",)))(ab)


def _ssm_glu_bwd(ab, dout):
    tm = _tile(SEQ, 512)

    def body(ab_ref, do_ref, dab_ref):
        _, vjp = jax.vjp(_glu, ab_ref[...])
        dab, = vjp(do_ref[...])
        dab_ref[...] = dab.astype(dab_ref.dtype)

    return pl.pallas_call(body, name="ssm_glu_bwd", grid=(SEQ // tm,),
                          in_specs=[pl.BlockSpec((tm, 2 * SSM_WIDTH), lambda i: (i, 0)),
                                    pl.BlockSpec((tm, SSM_WIDTH), lambda i: (i, 0))],
                          out_specs=pl.BlockSpec((tm, 2 * SSM_WIDTH), lambda i: (i, 0)),
                          out_shape=_sds((SEQ, 2 * SSM_WIDTH), MM_DTYPE), compiler_params=_cparams(("parallel",)))(ab, dout)


def _add2(name, a, b, out_dtype):
    tm = _tile(a.shape[0], 512)

    def body(a_ref, b_ref, o_ref):
        o_ref[...] = (a_ref[...] + b_ref[...]).astype(o_ref.dtype)

    blk = pl.BlockSpec((tm, a.shape[1]), lambda i: (i, 0))
    return pl.pallas_call(body, name=name, grid=(a.shape[0] // tm,), in_specs=[blk, blk], out_specs=blk,
                          out_shape=_sds(a.shape, out_dtype), compiler_params=_cparams(("parallel",)))(a, b)


def _layer_norm(r, g, b):
    mu = r.mean(-1, keepdims=True)
    var = jnp.square(r - mu).mean(-1, keepdims=True)
    return (r - mu) * lax.rsqrt(var + LN_EPS) * g + b


def _ln_fwd(name, x, y, g, b):
    tm = _tile(SEQ, 256)

    def body(x_ref, y_ref, g_ref, b_ref, r_ref, o_ref, ob_ref):
        r = DEEPNORM_ALPHA * x_ref[...] + y_ref[...]
        r_ref[...] = r
        o = _layer_norm(r, g_ref[...], b_ref[...])
        o_ref[...] = o
        ob_ref[...] = o.astype(ob_ref.dtype)

    blk = pl.BlockSpec((tm, D_MODEL), lambda i: (i, 0))
    row = pl.BlockSpec((1, D_MODEL), lambda i: (0, 0))
    return pl.pallas_call(body, name=name, grid=(SEQ // tm,), in_specs=[blk, blk, row, row], out_specs=[blk, blk, blk],
                          out_shape=[_sds((SEQ, D_MODEL))] * 2 + [_sds((SEQ, D_MODEL), MM_DTYPE)],
                          compiler_params=_cparams(("parallel",)))(x, y, g, b)


def _ln_bwd(name, r, g, b, da, db=None, dep=None):
    tm = _tile(SEQ, 256)
    two = db is not None
    deps = _as_list(dep)

    def body(r_ref, g_ref, b_ref, da_ref, *rest):
        dr_ref, drb_ref, dg_ref, dbeta_ref = rest[-4:]
        dout = DEEPNORM_ALPHA * da_ref[...] + rest[0][...] if two else da_ref[...]
        _, vjp = jax.vjp(_layer_norm, r_ref[...], g_ref[...], b_ref[...])
        dr, dg, dbeta = vjp(dout)
        dr_ref[...] = dr
        drb_ref[...] = dr.astype(drb_ref.dtype)

        @pl.when(pl.program_id(0) == 0)
        def _():
            dg_ref[...] = jnp.zeros_like(dg_ref)
            dbeta_ref[...] = jnp.zeros_like(dbeta_ref)

        dg_ref[...] += dg
        dbeta_ref[...] += dbeta

    blk = pl.BlockSpec((tm, D_MODEL), lambda i: (i, 0))
    row = pl.BlockSpec((1, D_MODEL), lambda i: (0, 0))
    args = [r, g, b, da] + ([db] if two else []) + deps
    return pl.pallas_call(body, name=name, grid=(SEQ // tm,),
                          in_specs=[blk, row, row, blk] + ([blk] if two else []) + [ANY] * len(deps),
                          out_specs=[blk, blk, row, row],
                          out_shape=[_sds((SEQ, D_MODEL)), _sds((SEQ, D_MODEL), MM_DTYPE), _sds((1, D_MODEL)), _sds((1, D_MODEL))],
                          compiler_params=_cparams(("arbitrary",)))(*args)


FFN_TM = 128
HALO = SUBLANES


def _conv_taps(cur, halo):
    row = lax.broadcasted_iota(jnp.int32, cur.shape, 0)
    h1 = jnp.where(row == 0, halo[HALO - 1:HALO, :], pltpu.roll(cur, 1, 0))
    h2 = jnp.where(row == 0, halo[HALO - 2:HALO - 1, :], jnp.where(row == 1, halo[HALO - 1:HALO, :], pltpu.roll(cur, 2, 0)))
    return h1, h2


def _conv_fwd(cur, halo, w_ref, b_ref):
    h1, h2 = _conv_taps(cur, halo)
    return b_ref[...] + h2 * w_ref[0:1, :] + h1 * w_ref[1:2, :] + cur * w_ref[2:3, :], h1, h2


def _gate(val, gate):
    return jax.nn.silu(gate) * val


def _ffn_specs(tm):
    nb = tm // HALO
    cur = lambda off: pl.BlockSpec((None, tm, FS), lambda p, i: (p + off, i, 0))
    halo = lambda off: pl.BlockSpec((None, HALO, FS), lambda p, i: (p + off, jnp.maximum(i * nb - 1, 0), 0))
    cw = lambda off: pl.BlockSpec((None, CONV_WIDTH, FS), lambda p, i: (p + off, 0, 0))
    cb = lambda off: pl.BlockSpec((None, 1, FS), lambda p, i: (p + off, 0, 0))
    return cur, halo, cw, cb


def _ffn_act(hf, conv_w, conv_b):
    tm = _tile(SEQ, FFN_TM, SUBLANES)
    cur, halo, cw, cb = _ffn_specs(tm)

    def body(v_ref, vh_ref, g_ref, gh_ref, wv_ref, wg_ref, bv_ref, bg_ref, o_ref):
        live = pl.program_id(1) > 0
        vh = jnp.where(live, vh_ref[...], 0.0)
        gh = jnp.where(live, gh_ref[...], 0.0)
        val, _, _ = _conv_fwd(v_ref[...], vh, wv_ref, bv_ref)
        gate, _, _ = _conv_fwd(g_ref[...], gh, wg_ref, bg_ref)
        o_ref[...] = _gate(val, gate).astype(o_ref.dtype)

    return pl.pallas_call(
        body, name="ffn_act", grid=(2, SEQ // tm),
        in_specs=[cur(0), halo(0), cur(2), halo(2), cw(0), cw(2), cb(0), cb(2)],
        out_specs=pl.BlockSpec((None, tm, FS), lambda p, i: (p, i, 0)),
        out_shape=_sds((2, SEQ, FS), MM_DTYPE), compiler_params=_cparams(("parallel", "parallel")))(
            hf, hf, hf, hf, conv_w, conv_w, conv_b, conv_b)


def _ffn_act_bwd(hf, conv_w, conv_b, dact):
    tm = _tile(SEQ, FFN_TM, SUBLANES)
    cur, halo, cw, cb = _ffn_specs(tm)

    def body(v_ref, vh_ref, g_ref, gh_ref, wv_ref, wg_ref, bv_ref, bg_ref, da_ref, dhc_ref, dw_ref, dbias_ref):
        dv_ref, dg_ref = dhc_ref.at[0], dhc_ref.at[1]
        dwv_ref, dwg_ref = dw_ref.at[0], dw_ref.at[1]
        dbv_ref, dbg_ref = dbias_ref.at[0], dbias_ref.at[1]
        i = pl.program_id(1)
        live = i > 0
        vh = jnp.where(live, vh_ref[...], 0.0)
        gh = jnp.where(live, gh_ref[...], 0.0)
        vcur, gcur = v_ref[...], g_ref[...]
        val, v1, v2 = _conv_fwd(vcur, vh, wv_ref, bv_ref)
        gate, g1, g2 = _conv_fwd(gcur, gh, wg_ref, bg_ref)
        _, vjp = jax.vjp(_gate, val, gate)
        dval, dgate = vjp(da_ref[...])
        dv_ref[...] = dval
        dg_ref[...] = dgate

        @pl.when(i == 0)
        def _():
            dw_ref[...] = jnp.zeros_like(dw_ref)
            dbias_ref[...] = jnp.zeros_like(dbias_ref)

        for d, taps, dwk_ref, dbk_ref in ((dval, (v2, v1, vcur), dwv_ref, dbv_ref), (dgate, (g2, g1, gcur), dwg_ref, dbg_ref)):
            for k in range(CONV_WIDTH):
                dwk_ref[k:k + 1, :] += jnp.sum(d * taps[k], axis=0, keepdims=True)
            dbk_ref[...] += jnp.sum(d, axis=0, keepdims=True)

    return pl.pallas_call(
        body, name="ffn_act_bwd", grid=(2, SEQ // tm),
        in_specs=[cur(0), halo(0), cur(2), halo(2), cw(0), cw(2), cb(0), cb(2),
                  pl.BlockSpec((None, tm, FS), lambda p, i: (p, i, 0))],
        out_specs=[pl.BlockSpec((None, 2, tm, FS), lambda p, i: (p, 0, i, 0)),
                   pl.BlockSpec((None, 2, CONV_WIDTH, FS), lambda p, i: (p, 0, 0, 0)),
                   pl.BlockSpec((None, 2, 1, FS), lambda p, i: (p, 0, 0, 0))],
        out_shape=[_sds((2, 2, SEQ, FS)), _sds((2, 2, CONV_WIDTH, FS)), _sds((2, 2, 1, FS))],
        compiler_params=_cparams(("parallel", "arbitrary")))(hf, hf, hf, hf, conv_w, conv_w, conv_b, conv_b, dact)


def _conv_bwd_input(dhc, conv_w):
    tm = _tile(SEQ, FFN_TM, SUBLANES)
    nb = tm // HALO
    nblk = SEQ // tm

    def body(d_ref, nx_ref, w_ref, o_ref):
        cur = d_ref[...]
        nxt = jnp.where(pl.program_id(2) < nblk - 1, nx_ref[...], 0.0)
        row = lax.broadcasted_iota(jnp.int32, cur.shape, 0)
        d1 = jnp.where(row == tm - 1, nxt[0:1, :], pltpu.roll(cur, tm - 1, 0))
        d2 = jnp.where(row == tm - 1, nxt[1:2, :], jnp.where(row == tm - 2, nxt[0:1, :], pltpu.roll(cur, tm - 2, 0)))
        o_ref[...] = (cur * w_ref[2:3, :] + d1 * w_ref[1:2, :] + d2 * w_ref[0:1, :]).astype(o_ref.dtype)

    blk = pl.BlockSpec((None, None, tm, FS), lambda p, kd, i: (p, kd, i, 0))
    return pl.pallas_call(
        body, name="conv_bwd_input", grid=(2, 2, nblk),
        in_specs=[blk, pl.BlockSpec((None, None, HALO, FS), lambda p, kd, i: (p, kd, jnp.minimum((i + 1) * nb, SEQ // HALO - 1), 0)),
                  pl.BlockSpec((None, CONV_WIDTH, FS), lambda p, kd, i: (2 * kd + p, 0, 0))],
        out_specs=blk, out_shape=_sds((2, 2, SEQ, FS), MM_DTYPE),
        compiler_params=_cparams(("parallel", "parallel", "parallel")))(dhc, dhc, conv_w)


def _loss(y, target):
    tm = _tile(SEQ, 256)

    def body(y_ref, t_ref, dy_ref, l_ref):
        err = y_ref[...] - t_ref[...]
        dy_ref[...] = err * (1.0 / D_MODEL)

        @pl.when(pl.program_id(0) == 0)
        def _():
            l_ref[...] = jnp.zeros_like(l_ref)

        l_ref[...] += 0.5 * jnp.sum(jnp.mean(jnp.square(err), axis=-1))

    blk = pl.BlockSpec((tm, D_MODEL), lambda i: (i, 0))
    return pl.pallas_call(body, name="loss", grid=(SEQ // tm,), in_specs=[blk, blk],
                          out_specs=[blk, pl.BlockSpec((SUBLANES, LANES), lambda i: (0, 0))],
                          out_shape=[_sds((SEQ, D_MODEL)), _sds((SUBLANES, LANES))],
                          compiler_params=_cparams(("arbitrary",)))(y, target)


ADAM_BLOCK_BYTES = 1 << 20


def _adamw_math(w, g, m, v):
    nm = ADAM_B1 * m + (1.0 - ADAM_B1) * g
    nv = ADAM_B2 * v + (1.0 - ADAM_B2) * jnp.square(g)
    m_hat = nm / (1.0 - ADAM_B1 ** ADAM_STEP)
    v_hat = nv / (1.0 - ADAM_B2 ** ADAM_STEP)
    return -ADAM_LR * (m_hat / (jnp.sqrt(v_hat) + ADAM_EPS) + ADAM_WD * w), nm, nv


def _adamw(name, w, g, m, v):
    r, c = w.shape
    tr = _tile(r, max(SUBLANES, ADAM_BLOCK_BYTES // (4 * c)), SUBLANES)

    def body(w_ref, g_ref, m_ref, v_ref, d_ref, nm_ref, nv_ref):
        d_ref[...], nm_ref[...], nv_ref[...] = _adamw_math(w_ref[...], g_ref[...], m_ref[...], v_ref[...])

    blk = pl.BlockSpec((tr, c), lambda i: (i, 0))
    return pl.pallas_call(body, name=name, grid=(r // tr,), in_specs=[blk] * 4, out_specs=[blk] * 3,
                          out_shape=[_sds((r, c))] * 3, compiler_params=_cparams(("parallel",)))(w, g, m, v)


def _adamw_big(name, l, c_idx, w, m, v, g_own, g_got, prev):
    depth, _, r, c = w.shape
    tr = _tile(r, max(SUBLANES, ADAM_BLOCK_BYTES // (4 * c)), SUBLANES)

    def body(c_ref, w_ref, m_ref, v_ref, own_ref, got_ref, *rest):
        g_ref, d_ref, nm_ref, nv_ref = rest[-4:]
        g = jnp.where(pl.program_id(0) == c_ref[0], own_ref[...], got_ref[...])
        g_ref[...] = g
        d_ref[...], nm_ref[...], nv_ref[...] = _adamw_math(w_ref[...], g, m_ref[...], v_ref[...])

    stacked = pl.BlockSpec((None, None, tr, c), lambda h, i, cr: (l, h, i, 0))
    own = pl.BlockSpec((tr, c), lambda h, i, cr: (jnp.where(h == cr[0], i, 0), 0))
    got = pl.BlockSpec((tr, c), lambda h, i, cr: (jnp.where(h == cr[0], 0, i), 0))
    grid_spec = pltpu.PrefetchScalarGridSpec(
        num_scalar_prefetch=1, grid=(2, r // tr),
        in_specs=[stacked] * 3 + [own, got] + ([ANY] * 4 if prev else []), out_specs=[stacked] * 4)
    return pl.pallas_call(
        body, name=name, grid_spec=grid_spec, out_shape=[_sds((depth, 2, r, c))] * 4,
        input_output_aliases={6 + k: k for k in range(4)} if prev else {},
        compiler_params=_cparams(("arbitrary", "arbitrary")))(c_idx, w, m, v, g_own, g_got, *(prev or ()))


ANY = pl.BlockSpec(memory_space=pl.ANY)


def _place():
    x, y, c = lax.axis_index("x"), lax.axis_index("y"), lax.axis_index("c")
    chips = [(1 - x, y), (x, 1 - y), (1 - x, 1 - y)]
    return x, y, c, chips


def _cast_place(name, w, l, me_idx, out_dtype):
    _, _, r, c = w.shape
    tr = _tile(r, max(2 * SUBLANES, COPY_BLOCK_BYTES // (4 * c)), 2 * SUBLANES)

    def body(me_ref, w_ref, o_ref):
        o_ref[...] = w_ref[...].astype(o_ref.dtype)

    grid_spec = pltpu.PrefetchScalarGridSpec(
        num_scalar_prefetch=1, grid=(2, r // tr),
        in_specs=[pl.BlockSpec((None, None, tr, c), lambda h, i, me: (l, h, i, 0))],
        out_specs=pl.BlockSpec((None, None, tr, c), lambda h, i, me: (me[0], h, i, 0)))
    return pl.pallas_call(body, name=name, grid_spec=grid_spec, out_shape=_sds((N_CHIPS, 2, r, c), out_dtype),
                          compiler_params=_cparams(("parallel", "parallel")))(me_idx, w)


HBM = pl.BlockSpec(memory_space=pltpu.HBM)
SEM = pl.BlockSpec(memory_space=pltpu.SEMAPHORE)
TOKEN = (SUBLANES, LANES)


def _comm_call(name, body, hbm, sems_in=(), after=None, sems_out=(), token=False):
    n, k = len(hbm), len(sems_out)
    ins = [pltpu.with_memory_space_constraint(a, pltpu.HBM) for a in hbm] + list(sems_in)
    in_specs = [HBM] * n + [SEM] * len(sems_in)
    if after is not None:
        ins.append(after)
        in_specs.append(ANY)
    out_shape = [pltpu.SemaphoreType.DMA((s,)) for s in sems_out] + [pltpu.HBM(a.shape, a.dtype) for a in hbm]
    out_specs = [SEM] * k + [HBM] * n
    if token:
        out_shape.append(_sds(TOKEN))
        out_specs.append(pl.BlockSpec(memory_space=pltpu.VMEM))
    res = pl.pallas_call(
        body, name=name, in_specs=in_specs, out_specs=out_specs, out_shape=out_shape,
        input_output_aliases={i: k + i for i in range(n)},
        compiler_params=pltpu.CompilerParams(has_side_effects=pltpu.SideEffectType.DATAFLOW_SIDE_EFFECTING))(*ins)
    return list(res[:k]), list(res[k:k + n]), (res[k + n] if token else None)


def _remote(src, dst, send, recv, to):
    return pltpu.make_async_remote_copy(src_ref=src, dst_ref=dst, send_sem=send, recv_sem=recv, device_id=to,
                                        device_id_type=MESH)


def _gather_start(name, bufs):
    n = len(bufs)

    def body(*refs):
        ins, (send, recv), token = refs[:n], refs[n:n + 2], refs[-1]
        x, y, c, chips = _place()
        for i in range(n):
            mine = ins[i].at[2 * x + y, c]
            for k, chip in enumerate(chips):
                _remote(mine, mine, send.at[3 * i + k], recv.at[3 * i + k], (*chip, c)).start()
        token[...] = jnp.zeros(TOKEN, F32)

    return _comm_call(name, body, bufs, sems_out=(3 * n, 3 * n), token=True)


def _gather_forward(name, bufs, sems, after):
    n = len(bufs)
    o = n + 2 + (after is not None)

    def body(*refs):
        ins, (send, recv), (send2, recv2), token = refs[:n], refs[n:n + 2], refs[o:o + 2], refs[-1]
        x, y, c, chips = _place()
        for i in range(n):
            mine = ins[i].at[2 * x + y, c]
            for k, chip in enumerate(chips):
                land = ins[i].at[2 * chip[0] + chip[1], c]
                first = _remote(mine, land, send.at[3 * i + k], recv.at[3 * i + k], (*chip, c))
                first.wait_send()
                first.wait_recv()
                _remote(land, land, send2.at[3 * i + k], recv2.at[3 * i + k], (x, y, 1 - c)).start()
        token[...] = jnp.zeros(TOKEN, F32)

    return _comm_call(name, body, bufs, sems_in=sems, after=after, sems_out=(3 * n, 3 * n), token=True)


def _gather_finish(name, bufs, sems, after):
    n = len(bufs)

    def body(*refs):
        ins, (send, recv) = refs[:n], refs[n:n + 2]
        x, y, c, chips = _place()
        for i in range(n):
            for k, chip in enumerate(chips):
                idx = 2 * chip[0] + chip[1]
                cp = _remote(ins[i].at[idx, c], ins[i].at[idx, 1 - c], send.at[3 * i + k], recv.at[3 * i + k], (x, y, 1 - c))
                cp.wait_send()
                cp.wait_recv()

    return _comm_call(name, body, bufs, sems_in=sems, after=after)[1]


def _swap_start(name, grads):
    n = len(grads)
    lands = [lax.empty((g.shape[0],) + g.shape[2:], g.dtype) for g in grads]

    def body(*refs):
        ins, lnd, (send, recv), token = refs[:n], refs[n:2 * n], refs[2 * n:2 * n + 2], refs[-1]
        x, y, c, _ = _place()
        for i in range(n):
            _remote(ins[i].at[:, 1 - c], lnd[i], send.at[i], recv.at[i], (x, y, 1 - c)).start()
        token[...] = jnp.zeros(TOKEN, F32)

    return _comm_call(name, body, list(grads) + lands, sems_out=(n, n), token=True)


def _swap_wait(name, hbm, sems, after):
    n = len(hbm) // 2

    def body(*refs):
        ins, lnd, (send, recv) = refs[:n], refs[n:2 * n], refs[2 * n:2 * n + 2]
        x, y, c, _ = _place()
        for i in range(n):
            cp = _remote(ins[i].at[:, 1 - c], lnd[i], send.at[i], recv.at[i], (x, y, 1 - c))
            cp.wait_send()
            cp.wait_recv()

    out = _comm_call(name, body, hbm, sems_in=sems, after=after)[1]
    return out[:n], out[n:]


def _pair_add(name, g, got, cm_idx):
    nk, _, r, c = g.shape
    tr = _tile(r, max(2 * SUBLANES, COPY_BLOCK_BYTES // (4 * c)), 2 * SUBLANES)

    def body(cm_ref, g_ref, x_ref, o_ref, land_ref):
        s = (g_ref[...] + x_ref[...]).astype(o_ref.dtype)
        o_ref[...] = s

        @pl.when(pl.program_id(1) == cm_ref[1])
        def _():
            land_ref[...] = s

    grid_spec = pltpu.PrefetchScalarGridSpec(
        num_scalar_prefetch=1, grid=(r // tr, nk),
        in_specs=[pl.BlockSpec((None, None, tr, c), lambda i, k, cm: (k, cm[0], i, 0)),
                  pl.BlockSpec((None, tr, c), lambda i, k, cm: (k, i, 0))],
        out_specs=[pl.BlockSpec((None, tr, c), lambda i, k, cm: (k, i, 0)),
                   pl.BlockSpec((None, tr, c), lambda i, k, cm: (cm[1], i, 0))])
    return pl.pallas_call(body, name=name, grid_spec=grid_spec, out_shape=[_sds((nk, r, c), BF16)] * 2,
                          compiler_params=_cparams(("parallel", "arbitrary")))(cm_idx, g, got)


def _scatter_start(name, parts, lands):
    n = len(parts)

    def body(*refs):
        ins, lnd, (send, recv), token = refs[:n], refs[n:2 * n], refs[2 * n:2 * n + 2], refs[-1]
        x, y, c, chips = _place()
        for i in range(n):
            for k, chip in enumerate(chips):
                _remote(ins[i].at[2 * chip[0] + chip[1]], lnd[i].at[2 * x + y], send.at[3 * i + k], recv.at[3 * i + k],
                        (*chip, c)).start()
        token[...] = jnp.zeros(TOKEN, F32)

    return _comm_call(name, body, list(parts) + list(lands), sems_out=(3 * n, 3 * n), token=True)


def _scatter_wait(name, hbm, sems, after):
    n = len(hbm) // 2

    def body(*refs):
        ins, lnd, (send, recv) = refs[:n], refs[n:2 * n], refs[2 * n:2 * n + 2]
        x, y, c, chips = _place()
        for i in range(n):
            for k, chip in enumerate(chips):
                idx = 2 * chip[0] + chip[1]
                cp = _remote(ins[i].at[idx], lnd[i].at[idx], send.at[3 * i + k], recv.at[3 * i + k], (*chip, c))
                cp.wait_send()
                cp.wait_recv()

    out = _comm_call(name, body, hbm, sems_in=sems, after=after)[1]
    return out[:n], out[n:]


def _sum_leading(name, x, out_dtype=F32):
    nk, r, c = x.shape
    tr = _tile(r, max(2 * SUBLANES, COPY_BLOCK_BYTES // (nk * c * x.dtype.itemsize)), 2 * SUBLANES)

    def body(x_ref, o_ref):
        acc = x_ref[0].astype(F32)
        for k in range(1, nk):
            acc = acc + x_ref[k].astype(F32)
        o_ref[...] = acc.astype(o_ref.dtype)

    return pl.pallas_call(body, name=name, grid=(r // tr,), in_specs=[pl.BlockSpec((nk, tr, c), lambda i: (0, i, 0))],
                          out_specs=pl.BlockSpec((tr, c), lambda i: (i, 0)), out_shape=_sds((r, c), out_dtype),
                          compiler_params=_cparams(("parallel",)))(x)


def _exchange_start(name, halves):
    n = len(halves)
    lands = [lax.empty(h.shape, h.dtype) for h in halves]

    def body(*refs):
        ins, lnd, (send, recv), token = refs[:n], refs[n:2 * n], refs[2 * n:2 * n + 2], refs[-1]
        x, y, c, _ = _place()
        for i in range(n):
            _remote(ins[i], lnd[i], send.at[i], recv.at[i], (x, y, 1 - c)).start()
        token[...] = jnp.zeros(TOKEN, F32)

    return _comm_call(name, body, list(halves) + lands, sems_out=(n, n), token=True)


def _exchange_wait(name, hbm, sems, after):
    n = len(hbm) // 2

    def body(*refs):
        ins, lnd, (send, recv) = refs[:n], refs[n:2 * n], refs[2 * n:2 * n + 2]
        x, y, c, _ = _place()
        for i in range(n):
            cp = _remote(ins[i], lnd[i], send.at[i], recv.at[i], (x, y, 1 - c))
            cp.wait_send()
            cp.wait_recv()

    out = _comm_call(name, body, hbm, sems_in=sems, after=after)[1]
    return out[:n], out[n:]


def _gather_all(part):
    def body(x_ref, out_ref, send_sems, recv_sems, local_sem):
        x, y, c, chips = _place()
        me, sibling = (x, y, c), (x, y, 1 - c)

        def rows(px, py, pc):
            return out_ref.at[4 * px + 2 * py + pc]

        def copy(k, block, to, src=None):
            return pltpu.make_async_remote_copy(src_ref=rows(*block) if src is None else src, dst_ref=rows(*block),
                                                send_sem=send_sems.at[k], recv_sem=recv_sems.at[k], device_id=to,
                                                device_id_type=MESH)

        mine = pltpu.make_async_copy(x_ref, rows(*me), local_sem)
        mine.start()
        first = [copy(0, me, sibling, src=x_ref)]
        first += [copy(1 + j, me, (*chip, c), src=x_ref) for j, chip in enumerate(chips)]
        for cp in first:
            cp.start()
        passed = [copy(4 + j, (*chip, c), sibling) for j, chip in enumerate(chips)]
        for j, chip in enumerate(chips):
            copy(1 + j, (*chip, c), me).wait_recv()
            passed[j].start()
        copy(0, sibling, me).wait_recv()
        for j, chip in enumerate(chips):
            copy(4 + j, (*chip, 1 - c), me).wait_recv()
        for cp in first + passed:
            cp.wait_send()
        mine.wait()

    return pl.pallas_call(
        body, name="gather_small_grads", in_specs=[ANY], out_specs=ANY, out_shape=_sds((N_DEV,) + part.shape, part.dtype),
        scratch_shapes=[pltpu.SemaphoreType.DMA((7,)), pltpu.SemaphoreType.DMA((7,)), pltpu.SemaphoreType.DMA])(part)


SMALL = ("attn_sinks", "pool_w", "pool_scale", "ssm_lam_re", "ssm_lam_im", "ssm_log_dt", "ssm_b_re", "ssm_b_im",
         "ssm_c_re", "ssm_c_im", "ssm_d", "ln1_g", "ln1_b", "ffn_conv_b", "ln2_g", "ln2_b")
BIG = ("w_in", "ssm_glu_w", "w_out", "ffn_w_up", "ffn_conv_w", "ffn_w_down")
ALL_W = ("w_in", "attn_sinks", "pool_w", "pool_scale", "ssm_lam_re", "ssm_lam_im", "ssm_log_dt", "ssm_b_re", "ssm_b_im",
         "ssm_c_re", "ssm_c_im", "ssm_d", "ssm_glu_w", "w_out", "ln1_g", "ln1_b", "ffn_w_up", "ffn_conv_w", "ffn_conv_b",
         "ffn_w_down", "ln2_g", "ln2_b")
PACK_UNIT = SUBLANES * LANES


def _padded(n):
    return -(-n // PACK_UNIT) * PACK_UNIT


def _pack(arrs):
    cols = []
    for name in SMALL:
        a = arrs[name].reshape(DEPTH, -1)
        cols.append(jnp.pad(a, ((0, 0), (0, _padded(a.shape[1]) - a.shape[1]))))
    return jnp.concatenate(cols, axis=1).reshape(-1, LANES)


def _unpack(packed, shapes):
    flat = packed.reshape(DEPTH, -1)
    out, off = {}, 0
    for name in SMALL:
        n = math.prod(shapes[name][1:])
        out[name] = flat[:, off:off + n].reshape(shapes[name])
        off += _padded(n)
    return out


def _b_rows(b):
    return b.transpose(2, 0, 1).reshape(SSM_GROUP, SSM_CH)


def _b_unrows(b):
    return b.reshape(SSM_GROUP, SSM_N_GROUPS, SSM_STATE).transpose(1, 2, 0)


def _block_diag_in(bb):
    eye = jnp.eye(SSM_N_GROUPS, dtype=F32)
    b3 = bb.reshape(SSM_GROUP, SSM_N_GROUPS, SSM_STATE)
    return jnp.einsum("hgp,gk->ghkp", b3, eye).reshape(SSM_WIDTH, SSM_CH)


def _block_diag_in_t(full):
    f4 = full.reshape(SSM_N_GROUPS, SSM_GROUP, SSM_N_GROUPS, SSM_STATE)
    return jnp.einsum("ghgp->hgp", f4).reshape(SSM_GROUP, SSM_CH)


def _block_diag_out(cc):
    eye = jnp.eye(SSM_N_GROUPS, dtype=F32)
    return jnp.einsum("ghp,gk->gpkh", cc, eye).reshape(SSM_CH, SSM_WIDTH)


def _block_diag_out_t(full):
    f4 = full.reshape(SSM_N_GROUPS, SSM_STATE, SSM_N_GROUPS, SSM_GROUP)
    return jnp.einsum("gpgh->ghp", f4)


def _rows_layout(re, im):
    n = re.shape[1]
    return jnp.stack([re.reshape(SCAN_NB, SCAN_CW, n), im.reshape(SCAN_NB, SCAN_CW, n)], axis=1).reshape(2 * SSM_CH, n)


def _rows_unlayout(x):
    n = x.shape[1]
    x = x.reshape(SCAN_NB, 2, SCAN_CW, n)
    return x[:, 0].reshape(SSM_CH, n), x[:, 1].reshape(SSM_CH, n)


H_POOL0 = ATTN_WIDTH + 2 * KV_WIDTH
H_SSM0 = H_POOL0 + POOL_WIDTH


def _ssm_params(p):
    lr = p["ssm_lam_re"].reshape(1, SSM_CH)
    li = p["ssm_lam_im"].reshape(1, SSM_CH)
    ldt = jnp.repeat(p["ssm_log_dt"], SSM_STATE).reshape(1, SSM_CH)
    return lr, li, ldt, _b_rows(p["ssm_b_re"]), _b_rows(p["ssm_b_im"])


def _layer_fwd(x, xb, p, wg, rope_t, dep, mid):
    cos_t, sin_t = rope_t
    h = _mm_shard_cols("in_proj", xb, wg["w_in"], dep=dep)
    qk = _rope("rope_fwd", h, 0, Q_TILES + KV_TILES, cos_t, sin_t, MM_DTYPE)
    y_attn, y_attn_b = _attn_fwd(qk, h, p["attn_sinks"])
    y_pool = _pool_fwd(h, p["pool_w"], p["pool_scale"].reshape(1, POOL_WIDTH))
    ssm_in = _ssm_params(p)
    ar, ai, bbr, bbi = _ssm_prep(*ssm_in)
    bd = _scan_layout(_block_diag_in(bbr), _block_diag_in(bbi)).astype(MM_DTYPE)
    cc = _rows_layout(_block_diag_out(p["ssm_c_re"]), -_block_diag_out(p["ssm_c_im"])).astype(MM_DTYPE)
    dvec = p["ssm_d"].reshape(1, SSM_WIDTH)
    up = _time_permute(h[:, H_SSM0:])
    xx = _mm_nn("ssm_bu", up, bd, tn=1024)
    ss = _ssm_scan("ssm_scan_fwd", _scan_layout(ar, ai), xx, False)
    yp = _mm_nn("ssm_cs", ss, cc, tk=1024)
    yf, gy = _ssm_gelu(yp, up, dvec)
    ab = _mm_shard_cols("ssm_glu_proj", gy, wg["ssm_glu_w"])
    y_ssm = _time_unpermute(_ssm_glu(ab))
    mix = jnp.concatenate([y_attn_b, y_pool, y_ssm], axis=1)
    mixo = _mm_nn("out_proj", mix, wg["w_out"].reshape(MIX_WIDTH, D_MODEL))
    r1, x1, x1b = _ln_fwd("ln1_fwd", x, mixo, p["ln1_g"].reshape(1, D_MODEL), p["ln1_b"].reshape(1, D_MODEL))
    hf = _ffn_up(x1b, wg["ffn_w_up"], dep=mid(x1b))
    conv_b = p["ffn_conv_b"].reshape(N_CHIPS, 1, FS)
    act = _ffn_act(hf, wg["ffn_conv_w"], conv_b)
    f = _ffn_down(act, wg["ffn_w_down"].reshape(2, FS, D_MODEL))
    r2, x2, x2b = _ln_fwd("ln2_fwd", x1, f, p["ln2_g"].reshape(1, D_MODEL), p["ln2_b"].reshape(1, D_MODEL))
    saved = dict(xb=xb, h=h, qk=qk, y_attn=y_attn, ssm_in=ssm_in, ar=ar, ai=ai, bd=bd, cc=cc, dvec=dvec, up=up, ss=ss, yf=yf,
                 gy=gy, ab=ab, mix=mix, r1=r1, x1b=x1b, hf=hf, conv_b=conv_b, act=act, r2=r2)
    return x2, x2b, saved


def _layer_bwd(da, db, p, wg, sv, rope_t, dep, mid1, mid2):
    cos_t, sin_t = rope_t
    small = {}
    dr2, dr2b, dg, dbeta = _ln_bwd("ln2_bwd" if db is not None else "ln2_bwd_last", sv["r2"], p["ln2_g"].reshape(1, D_MODEL),
                                   p["ln2_b"].reshape(1, D_MODEL), da, db, dep=dep)
    small["ln2_g"], small["ln2_b"] = dg, dbeta
    w_down = wg["ffn_w_down"].reshape(2, FS, D_MODEL)
    dact = _ffn_down_dact(dr2b, w_down)
    dw_down = _ffn_down_dw(sv["act"], dr2b)
    dhc, dcw, dcb = _ffn_act_bwd(sv["hf"], wg["ffn_conv_w"], sv["conv_b"], dact)
    dconv_w = dcw.transpose(1, 0, 2, 3).reshape(N_CHIPS, CONV_WIDTH, FS)
    small["ffn_conv_b"] = dcb.transpose(1, 0, 2, 3)
    dh_ffn = _conv_bwd_input(dhc, wg["ffn_conv_w"])
    dx1_ffn = _ffn_up_dx(dh_ffn, wg["ffn_w_up"])
    dw_up = _ffn_up_dw(sv["x1b"], dh_ffn)
    dr1, dr1b, dg, dbeta = _ln_bwd("ln1_bwd", sv["r1"], p["ln1_g"].reshape(1, D_MODEL), p["ln1_b"].reshape(1, D_MODEL), dr2,
                                   dx1_ffn, dep=mid1(dx1_ffn))
    small["ln1_g"], small["ln1_b"] = dg, dbeta
    w_out = wg["w_out"].reshape(MIX_WIDTH, D_MODEL)
    dmix = _mm_nt("out_proj_dx", dr1b, w_out)
    dw_out = _mm_tn("out_proj_dw", sv["mix"], dr1b)
    dq, dkc, dkp, dvc, dvp, dsk = _attn_bwd(sv["qk"], sv["h"], p["attn_sinks"], sv["y_attn"], dmix, 0)
    small["attn_sinks"] = dsk[:, :, 0]
    dh_attn = _attn_dh(dq, dkc, dkp, dvc, dvp, cos_t, -sin_t)
    dh_pool, dpw, dps = _pool_bwd(sv["h"], p["pool_w"], p["pool_scale"].reshape(1, POOL_WIDTH), dmix, ATTN_WIDTH // POOL_WIDTH)
    small["pool_w"], small["pool_scale"] = dpw, dps
    dout_p = _time_permute(dmix[:, ATTN_WIDTH + POOL_WIDTH:])
    dab = _ssm_glu_bwd(sv["ab"], dout_p)
    dgy = _mm_shard_cols_nt("ssm_glu_dx", dab, wg["ssm_glu_w"])
    dw_glu = _mm_shard_cols_tn("ssm_glu_dw", sv["gy"], dab, N_CHIPS)
    dyf, du1, dd = _ssm_gelu_bwd(sv["yf"], dgy, sv["up"], sv["dvec"])
    small["ssm_d"] = dd
    dss = _mm_nt("ssm_cs_dx", dyf, sv["cc"], tn=1024)
    dcc = _mm_tn("ssm_cs_dw", sv["ss"], dyf, tm=1024)
    dcre, dcim = _rows_unlayout(dcc)
    small["ssm_c_re"], small["ssm_c_im"] = _block_diag_out_t(dcre), -_block_diag_out_t(dcim)
    gg, da8 = _ssm_scan("ssm_scan_bwd", _scan_layout(sv["ar"], -sv["ai"]), dss, True, sv["ss"])
    du2 = _mm_nt("ssm_bu_dx", gg, sv["bd"], tk=1024)
    dbd = _mm_tn("ssm_bu_dw", sv["up"], gg, tn=1024)
    dbdr, dbdi = _scan_unlayout(dbd)
    dar8, dai8 = _scan_unlayout(da8)
    dlr, dli, dldt, dbr, dbi = _ssm_prep_bwd(*sv["ssm_in"], dar8, dai8, _block_diag_in_t(dbdr), _block_diag_in_t(dbdi))
    small["ssm_lam_re"], small["ssm_lam_im"] = dlr, dli
    small["ssm_log_dt"] = dldt.reshape(SSM_N_GROUPS, SSM_STATE).sum(axis=1)
    small["ssm_b_re"], small["ssm_b_im"] = _b_unrows(dbr), _b_unrows(dbi)
    dh_ssm = _time_unpermute(_add2("ssm_du", du1, du2, MM_DTYPE))
    dh = jnp.concatenate([dh_attn, dh_pool, dh_ssm], axis=1)
    dx_in = _mm_shard_cols_nt("in_proj_dx", dh, wg["w_in"], dep=mid2(dh))
    dw_in = _mm_shard_cols_tn("in_proj_dw", sv["xb"], dh, N_CHIPS)
    big = {"w_in": dw_in, "ssm_glu_w": dw_glu, "w_out": dw_out.reshape(N_CHIPS, MIX_WIDTH // N_CHIPS, D_MODEL),
           "ffn_w_up": dw_up, "ffn_conv_w": dconv_w, "ffn_w_down": dw_down.reshape(N_CHIPS, FS // 2, D_MODEL)}
    return dr1, dx_in, big, small


CONV_PAD = 2 * SUBLANES


def _halved(name, a):
    if name == "ffn_conv_w":
        a = jnp.pad(a, ((0, 0), (0, CONV_PAD - CONV_WIDTH), (0, 0)))
    return a.reshape(a.shape[0], 2, a.shape[1] // 2, a.shape[2])


def _unhalved(name, a):
    a = a.reshape(a.shape[:-3] + (2 * a.shape[-2], a.shape[-1]))
    return a[..., :CONV_WIDTH, :] if name == "ffn_conv_w" else a


class _Reduce:
    def __init__(self, tag, big, cm_idx):
        self.tag, self.cm_idx = tag, cm_idx
        g4 = [_halved(name, big[name]) for name in BIG]
        self.sems, self.hbm, self.token = _swap_start("grad_swap_start_" + tag, g4)

    def swapped(self, after):
        g4, got = _swap_wait("grad_swap_wait_" + self.tag, self.hbm, self.sems, after)
        parts, lands = zip(*[_pair_add("grad_pair_add", g, x, self.cm_idx) for g, x in zip(g4, got)])
        self.sems, self.hbm, self.token = _scatter_start("grad_scatter_start_" + self.tag, parts, lands)
        return self.token

    def scattered(self, after):
        _, recv = _scatter_wait("grad_scatter_wait_" + self.tag, self.hbm, self.sems, after)
        halves = [_sum_leading("grad_chip_sum", r) for r in recv]
        self.sems, self.hbm, self.token = _exchange_start("grad_exchange_start_" + self.tag, halves)
        return self.token

    def finish(self, after):
        return _exchange_wait("grad_exchange_wait_" + self.tag, self.hbm, self.sems, after)


def kernel(x, w_in, attn_sinks, pool_w, pool_scale, ssm_lam_re, ssm_lam_im, ssm_log_dt, ssm_b_re, ssm_b_im, ssm_c_re, ssm_c_im, ssm_d, ssm_glu_w, w_out, ln1_g, ln1_b, ffn_w_up, ffn_conv_w, ffn_conv_b, ffn_w_down, ln2_g, ln2_b, loss_target, m_w_in, m_attn_sinks, m_pool_w, m_pool_scale, m_ssm_lam_re, m_ssm_lam_im, m_ssm_log_dt, m_ssm_b_re, m_ssm_b_im, m_ssm_c_re, m_ssm_c_im, m_ssm_d, m_ssm_glu_w, m_w_out, m_ln1_g, m_ln1_b, m_ffn_w_up, m_ffn_conv_w, m_ffn_conv_b, m_ffn_w_down, m_ln2_g, m_ln2_b, v_w_in, v_attn_sinks, v_pool_w, v_pool_scale, v_ssm_lam_re, v_ssm_lam_im, v_ssm_log_dt, v_ssm_b_re, v_ssm_b_im, v_ssm_c_re, v_ssm_c_im, v_ssm_d, v_ssm_glu_w, v_w_out, v_ln1_g, v_ln1_b, v_ffn_w_up, v_ffn_conv_w, v_ffn_conv_b, v_ffn_w_down, v_ln2_g, v_ln2_b):
    w = dict(w_in=w_in, attn_sinks=attn_sinks, pool_w=pool_w, pool_scale=pool_scale, ssm_lam_re=ssm_lam_re,
             ssm_lam_im=ssm_lam_im, ssm_log_dt=ssm_log_dt, ssm_b_re=ssm_b_re, ssm_b_im=ssm_b_im, ssm_c_re=ssm_c_re,
             ssm_c_im=ssm_c_im, ssm_d=ssm_d, ssm_glu_w=ssm_glu_w, w_out=w_out, ln1_g=ln1_g, ln1_b=ln1_b, ffn_w_up=ffn_w_up,
             ffn_conv_w=ffn_conv_w, ffn_conv_b=ffn_conv_b, ffn_w_down=ffn_w_down, ln2_g=ln2_g, ln2_b=ln2_b)
    m = dict(w_in=m_w_in, attn_sinks=m_attn_sinks, pool_w=m_pool_w, pool_scale=m_pool_scale, ssm_lam_re=m_ssm_lam_re,
             ssm_lam_im=m_ssm_lam_im, ssm_log_dt=m_ssm_log_dt, ssm_b_re=m_ssm_b_re, ssm_b_im=m_ssm_b_im, ssm_c_re=m_ssm_c_re,
             ssm_c_im=m_ssm_c_im, ssm_d=m_ssm_d, ssm_glu_w=m_ssm_glu_w, w_out=m_w_out, ln1_g=m_ln1_g, ln1_b=m_ln1_b,
             ffn_w_up=m_ffn_w_up, ffn_conv_w=m_ffn_conv_w, ffn_conv_b=m_ffn_conv_b, ffn_w_down=m_ffn_w_down, ln2_g=m_ln2_g,
             ln2_b=m_ln2_b)
    v = dict(w_in=v_w_in, attn_sinks=v_attn_sinks, pool_w=v_pool_w, pool_scale=v_pool_scale, ssm_lam_re=v_ssm_lam_re,
             ssm_lam_im=v_ssm_lam_im, ssm_log_dt=v_ssm_log_dt, ssm_b_re=v_ssm_b_re, ssm_b_im=v_ssm_b_im, ssm_c_re=v_ssm_c_re,
             ssm_c_im=v_ssm_c_im, ssm_d=v_ssm_d, ssm_glu_w=v_ssm_glu_w, w_out=v_w_out, ln1_g=v_ln1_g, ln1_b=v_ln1_b,
             ffn_w_up=v_ffn_w_up, ffn_conv_w=v_ffn_conv_w, ffn_conv_b=v_ffn_conv_b, ffn_w_down=v_ffn_w_down, ln2_g=v_ln2_g,
             ln2_b=v_ln2_b)
    c_pos = lax.axis_index("c").astype(jnp.int32)
    chip = (2 * lax.axis_index("x") + lax.axis_index("y")).astype(jnp.int32)
    c_idx, chip_idx, cm_idx = c_pos.reshape(1), chip.reshape(1), jnp.stack([c_pos, chip])
    rope_t = _rope_tables()
    xs = x.reshape(SEQ, D_MODEL)
    xb = xs.astype(MM_DTYPE)
    for t in (w, m, v):
        t["ffn_w_up"] = jnp.swapaxes(t["ffn_w_up"], 1, 2)
    wh, mh, vh = ({n: _halved(n, t[n]) for n in BIG} for t in (w, m, v))

    def place(l):
        return [_cast_place("place_" + n, wh[n], l, chip_idx, F32 if n == "ffn_conv_w" else MM_DTYPE) for n in BIG]

    sems, bufs, _ = _gather_start("gather_start_0", place(0))
    sems, bufs, _ = _gather_forward("gather_forward_0", bufs, sems, None)
    bufs = _gather_finish("gather_finish_0", bufs, sems, None)
    gathered, saved = [], []
    for l in range(DEPTH):
        gathered.append({n: _unhalved(n, g) for n, g in zip(BIG, bufs)})
        nxt, dep, mid = {}, None, lambda after: None
        if l + 1 < DEPTH:
            nxt["sems"], nxt["bufs"], dep = _gather_start("gather_start_%d" % (l + 1), place(l + 1))

            def mid(after):
                nxt["sems"], nxt["bufs"], token = _gather_forward("gather_forward_%d" % (l + 1), nxt["bufs"], nxt["sems"], after)
                return token

        xs, xb, sv = _layer_fwd(xs, xb, {n: w[n][l] for n in SMALL}, gathered[l], rope_t, dep, mid)
        saved.append(sv)
        if l + 1 < DEPTH:
            bufs = _gather_finish("gather_finish_%d" % (l + 1), nxt["bufs"], nxt["sems"], xb)
    dy, loss_tile = _loss(xs, loss_target.reshape(SEQ, D_MODEL))
    loss = lax.psum(loss_tile[0, 0], ("x", "y", "c"))

    big_out = {n: None for n in BIG}
    small_g = {n: [None] * DEPTH for n in SMALL}

    def update(l, own, got):
        for n, o, g in zip(BIG, own, got):
            big_out[n] = _adamw_big("adamw_" + n, l, c_idx, wh[n], mh[n], vh[n], o, g, big_out[n])

    da, db, red = dy, None, None
    for l in reversed(range(DEPTH)):
        hooks = (None, lambda after: None, lambda after: None) if red is None else (red.token, red.swapped, red.scattered)
        da, db, big, small = _layer_bwd(da, db, {n: w[n][l] for n in SMALL}, gathered[l], saved[l], rope_t, *hooks)
        for n in SMALL:
            small_g[n][l] = small[n].reshape(w[n].shape[1:])
        if red is not None:
            update(l + 1, *red.finish(db))
        red = _Reduce(str(l), big, cm_idx)
    red.swapped(None)
    red.scattered(None)
    update(0, *red.finish(None))
    grad_x = _ln_in_grad(da, db).reshape(x.shape)

    shapes = {n: w[n].shape for n in SMALL}
    part = _pack({n: jnp.stack(small_g[n]) for n in SMALL})
    g_small = _sum_leading("small_grad_sum", _gather_all(part))
    upd = _adamw("adamw_small", _pack(w), g_small, _pack(m), _pack(v))
    small_out = [_unpack(a, shapes) for a in (g_small,) + tuple(upd)]

    outs = [loss, grad_x]
    for kind in range(4):
        for n in ALL_W:
            if n in SMALL:
                outs.append(small_out[kind][n])
            else:
                o = _unhalved(n, big_out[n][kind])
                outs.append(jnp.swapaxes(o, 1, 2) if n == "ffn_w_up" else o)
    return tuple(outs)


def _ln_in_grad(dr1, dx_in):
    tm = _tile(SEQ, 512)

    def body(a_ref, b_ref, o_ref):
        o_ref[...] = DEEPNORM_ALPHA * a_ref[...] + b_ref[...]

    blk = pl.BlockSpec((tm, D_MODEL), lambda i: (i, 0))
    return pl.pallas_call(body, name="grad_x", grid=(SEQ // tm,), in_specs=[blk, blk], out_specs=blk,
                          out_shape=_sds((SEQ, D_MODEL)), compiler_params=_cparams(("parallel",)))(dr1, dx_in)
```

```python
import functools
import math

import jax
import jax.numpy as jnp
from jax import lax
from jax.experimental import pallas as pl
from jax.experimental.pallas import tpu as pltpu

F32 = jnp.float32
BF16 = jnp.bfloat16
MM_DTYPE = BF16

D_MODEL = 2048
SEQ = 2048
DEPTH = 4
D_FF = 5504
HEAD_DIM = 64
N_Q_HEADS = D_MODEL // 2 // HEAD_DIM
N_KV_HEADS = N_Q_HEADS // 4
ATTN_WIDTH = N_Q_HEADS * HEAD_DIM
KV_WIDTH = N_KV_HEADS * HEAD_DIM
ATTN_BLOCK = 128
ROPE_THETA = 10000.0
POOL_WINDOWS = (2, 4, 8, 16)
POOL_WIDTH = D_MODEL // 4
POOL_GROUP = POOL_WIDTH // len(POOL_WINDOWS)
SSM_WIDTH = D_MODEL // 4
SSM_GROUP = 16
SSM_N_GROUPS = SSM_WIDTH // SSM_GROUP
SSM_STATE = 64
SSM_CH = SSM_N_GROUPS * SSM_STATE
MIX_WIDTH = ATTN_WIDTH + POOL_WIDTH + SSM_WIDTH
IN_WIDTH = ATTN_WIDTH + 2 * KV_WIDTH + POOL_WIDTH + SSM_WIDTH
CONV_WIDTH = 3
LN_EPS = 1e-5
DEEPNORM_ALPHA = (2 * DEPTH) ** 0.25
ADAM_LR = 0.001
ADAM_B1 = 0.9
ADAM_B2 = 0.999
ADAM_EPS = 1e-08
ADAM_WD = 0.01
ADAM_STEP = 10

N_CHIPS = 4
N_DEV = 8
FS = 2 * D_FF // N_CHIPS
IN_S = IN_WIDTH // N_CHIPS
GLU_S = 2 * SSM_WIDTH // N_CHIPS
LANES = 128
SUBLANES = 8
SCAN_CW = 256
SCAN_NB = SSM_CH // SCAN_CW
VMEM_LIMIT = 56 * 1024 * 1024
COPY_BLOCK_BYTES = 6 * 1024 * 1024
NEG = -1e30

NN = (((1,), (0,)), ((), ()))
NT = (((1,), (1,)), ((), ()))
TN = (((0,), (0,)), ((), ()))
MESH = pl.DeviceIdType.MESH


def _tile(n, pref, mult=LANES):
    best = None
    for t in range(mult, min(n, pref) + 1, mult):
        if n % t == 0:
            best = t
    return n if best is None else best


def _cparams(sem):
    return pltpu.CompilerParams(dimension_semantics=sem, vmem_limit_bytes=VMEM_LIMIT)


def _sds(shape, dtype=F32):
    return jax.ShapeDtypeStruct(tuple(shape), dtype)


def _as_list(x):
    return [] if x is None else list(x) if isinstance(x, (list, tuple)) else [x]


def _mm(name, a, b, out_shape, grid, a_spec, b_spec, o_spec, dims, acc_shape, out_dtype=F32, dep=None):
    nk = grid[2]
    deps = _as_list(dep)

    def product(a_ref, b_ref):
        return lax.dot_general(a_ref[...].astype(MM_DTYPE), b_ref[...].astype(MM_DTYPE), dims, preferred_element_type=F32)

    def body_one(a_ref, b_ref, *rest):
        rest[-1][...] = product(a_ref, b_ref).astype(rest[-1].dtype)

    def body(a_ref, b_ref, *rest):
        o_ref, acc_ref = rest[-2:]
        k = pl.program_id(2)

        @pl.when(k == 0)
        def _():
            acc_ref[...] = product(a_ref, b_ref)

        @pl.when(k > 0)
        def _():
            acc_ref[...] += product(a_ref, b_ref)

        @pl.when(k == nk - 1)
        def _():
            o_ref[...] = acc_ref[...].astype(o_ref.dtype)

    return pl.pallas_call(
        body_one if nk == 1 else body, name=name, grid=grid, in_specs=[a_spec, b_spec] + [ANY] * len(deps),
        out_specs=o_spec, out_shape=_sds(out_shape, out_dtype),
        scratch_shapes=[] if nk == 1 else [pltpu.VMEM(acc_shape, F32)],
        compiler_params=_cparams(("parallel", "parallel", "arbitrary")))(a, b, *deps)


def _mm_nn(name, a, b, tm=2048, tn=512, tk=2048, out_dtype=F32):
    m, kk = a.shape
    n = b.shape[1]
    tm, tn, tk = _tile(m, tm), _tile(n, tn), _tile(kk, tk)
    return _mm(name, a, b, (m, n), (m // tm, n // tn, kk // tk),
               pl.BlockSpec((tm, tk), lambda i, j, k: (i, k)), pl.BlockSpec((tk, tn), lambda i, j, k: (k, j)),
               pl.BlockSpec((tm, tn), lambda i, j, k: (i, j)), NN, (tm, tn), out_dtype)


def _mm_nt(name, a, b, tm=2048, tn=512, tk=2048):
    m, kk = a.shape
    n = b.shape[0]
    tm, tn, tk = _tile(m, tm), _tile(n, tn), _tile(kk, tk)
    return _mm(name, a, b, (m, n), (m // tm, n // tn, kk // tk),
               pl.BlockSpec((tm, tk), lambda i, j, k: (i, k)), pl.BlockSpec((tn, tk), lambda i, j, k: (j, k)),
               pl.BlockSpec((tm, tn), lambda i, j, k: (i, j)), NT, (tm, tn))


def _mm_tn(name, a, b, tm=1024, tn=1024, ts=2048):
    s, m = a.shape
    n = b.shape[1]
    tm, tn, ts = _tile(m, tm), _tile(n, tn), _tile(s, ts)
    return _mm(name, a, b, (m, n), (m // tm, n // tn, s // ts),
               pl.BlockSpec((ts, tm), lambda i, j, k: (k, i)), pl.BlockSpec((ts, tn), lambda i, j, k: (k, j)),
               pl.BlockSpec((tm, tn), lambda i, j, k: (i, j)), TN, (tm, tn))


def _mm_shard_cols(name, a, w, tm=2048, tk=2048, dep=None):
    m, kk = a.shape
    nj, _, c = w.shape
    tm, tk = _tile(m, tm), _tile(kk, tk)
    return _mm(name, a, w, (m, nj * c), (m // tm, nj, kk // tk),
               pl.BlockSpec((tm, tk), lambda i, j, k: (i, k)), pl.BlockSpec((None, tk, c), lambda i, j, k: (j, k, 0)),
               pl.BlockSpec((tm, c), lambda i, j, k: (i, j)), NN, (tm, c), dep=dep)


def _mm_shard_cols_nt(name, d, w, tm=2048, tn=512, dep=None):
    m = d.shape[0]
    nj, n, c = w.shape
    tm, tn = _tile(m, tm), _tile(n, tn)
    return _mm(name, d, w, (m, n), (m // tm, n // tn, nj),
               pl.BlockSpec((tm, c), lambda i, j, k: (i, k)), pl.BlockSpec((None, tn, c), lambda i, j, k: (k, j, 0)),
               pl.BlockSpec((tm, tn), lambda i, j, k: (i, j)), NT, (tm, tn), dep=dep)


def _mm_shard_cols_tn(name, a, d, nj, tm=1024, ts=2048):
    s, m = a.shape
    c = d.shape[1] // nj
    tm, ts = _tile(m, tm), _tile(s, ts)
    return _mm(name, a, d, (nj, m, c), (nj, m // tm, s // ts),
               pl.BlockSpec((ts, tm), lambda j, i, k: (k, i)), pl.BlockSpec((ts, c), lambda j, i, k: (k, j)),
               pl.BlockSpec((None, tm, c), lambda j, i, k: (j, i, 0)), TN, (tm, c))


def _ffn_up(x1, w_up_t, tm=512, tk=2048, dep=None):
    s, d = x1.shape
    tm, tk = _tile(s, tm), _tile(d, tk)
    return _mm("ffn_up", x1, w_up_t, (N_CHIPS, s, FS), (N_CHIPS, s // tm, d // tk),
               pl.BlockSpec((tm, tk), lambda j, i, k: (i, k)), pl.BlockSpec((None, FS, tk), lambda j, i, k: (j, 0, k)),
               pl.BlockSpec((None, tm, FS), lambda j, i, k: (j, i, 0)), NT, (tm, FS), dep=dep)


def _ffn_down(act, w_down, tm=1024, tn=512):
    _, s, _ = act.shape
    d = w_down.shape[2]
    tm, tn = _tile(s, tm), _tile(d, tn)
    return _mm("ffn_down", act, w_down, (s, d), (s // tm, d // tn, 2),
               pl.BlockSpec((None, tm, FS), lambda i, j, k: (k, i, 0)), pl.BlockSpec((None, FS, tn), lambda i, j, k: (k, 0, j)),
               pl.BlockSpec((tm, tn), lambda i, j, k: (i, j)), NN, (tm, tn))


def _ffn_down_dact(df, w_down, tm=512, tk=2048):
    s, d = df.shape
    tm, tk = _tile(s, tm), _tile(d, tk)
    return _mm("ffn_down_dact", df, w_down, (2, s, FS), (2, s // tm, d // tk),
               pl.BlockSpec((tm, tk), lambda j, i, k: (i, k)), pl.BlockSpec((None, FS, tk), lambda j, i, k: (j, 0, k)),
               pl.BlockSpec((None, tm, FS), lambda j, i, k: (j, i, 0)), NT, (tm, FS))


def _ffn_down_dw(act, df, tn=512, ts=2048):
    _, s, _ = act.shape
    d = df.shape[1]
    tn, ts = _tile(d, tn), _tile(s, ts)
    return _mm("ffn_down_dw", act, df, (2, FS, d), (2, d // tn, s // ts),
               pl.BlockSpec((None, ts, FS), lambda p, j, k: (p, k, 0)), pl.BlockSpec((ts, tn), lambda p, j, k: (k, j)),
               pl.BlockSpec((None, FS, tn), lambda p, j, k: (p, 0, j)), TN, (FS, tn))


def _ffn_up_dx(dh, w_up_t, tm=1024, tn=1024):
    s = dh.shape[2]
    d = w_up_t.shape[2]
    tm, tn = _tile(s, tm), _tile(d, tn)
    return _mm("ffn_up_dx", dh, w_up_t, (s, d), (s // tm, d // tn, N_CHIPS),
               pl.BlockSpec((None, None, tm, FS), lambda i, j, k: (k % 2, k // 2, i, 0)),
               pl.BlockSpec((None, FS, tn), lambda i, j, k: (k, 0, j)),
               pl.BlockSpec((tm, tn), lambda i, j, k: (i, j)), NN, (tm, tn))


def _ffn_up_dw(x1, dh, tn=512, ts=2048):
    s, d = x1.shape
    tn, ts = _tile(d, tn), _tile(s, ts)
    return _mm("ffn_up_dw", dh, x1, (N_CHIPS, FS, d), (N_CHIPS, d // tn, s // ts),
               pl.BlockSpec((None, None, ts, FS), lambda j, i, k: (j % 2, j // 2, k, 0)),
               pl.BlockSpec((ts, tn), lambda j, i, k: (k, i)),
               pl.BlockSpec((None, FS, tn), lambda j, i, k: (j, 0, i)), TN, (FS, tn))


def _rope_tables():
    half = HEAD_DIM // 2
    inv = ROPE_THETA ** (-jnp.arange(half, dtype=F32) / half)
    ang = jnp.arange(SEQ).astype(F32)[:, None] * inv[None, :]
    cos, sin = jnp.cos(ang), jnp.sin(ang)
    cos_t = jnp.tile(cos, (1, LANES // half))
    sin_t = jnp.tile(jnp.concatenate([-sin, sin], axis=1), (1, LANES // HEAD_DIM))
    return cos_t, sin_t


def _rotate_half(t):
    lane = lax.broadcasted_iota(jnp.int32, t.shape, 1)
    first = (lane % HEAD_DIM) < (HEAD_DIM // 2)
    return jnp.where(first, pltpu.roll(t, LANES - HEAD_DIM // 2, 1), pltpu.roll(t, HEAD_DIM // 2, 1))


def _rope(name, src, col_tile0, n_tiles, cos_t, sin_t, out_dtype):
    tm = _tile(SEQ, 512)

    def body(x_ref, c_ref, s_ref, o_ref):
        t = x_ref[...].astype(F32)
        o_ref[...] = (t * c_ref[...] + _rotate_half(t) * s_ref[...]).astype(o_ref.dtype)

    return pl.pallas_call(
        body, name=name, grid=(SEQ // tm, n_tiles),
        in_specs=[pl.BlockSpec((tm, LANES), lambda i, j: (i, j + col_tile0)),
                  pl.BlockSpec((tm, LANES), lambda i, j: (i, 0)), pl.BlockSpec((tm, LANES), lambda i, j: (i, 0))],
        out_specs=pl.BlockSpec((tm, LANES), lambda i, j: (i, j)),
        out_shape=_sds((SEQ, n_tiles * LANES), out_dtype),
        compiler_params=_cparams(("parallel", "parallel")))(src, cos_t, sin_t)


Q_TILES = ATTN_WIDTH // LANES
KV_TILES = KV_WIDTH // LANES
Q_PER_KV_TILE = Q_TILES // KV_TILES
HEADS_PER_KV_TILE = N_Q_HEADS // KV_TILES
K_TILE0 = ATTN_WIDTH // LANES
V_TILE0 = (ATTN_WIDTH + KV_WIDTH) // LANES
N_QBLK = SEQ // ATTN_BLOCK


def _dup_half(t, which):
    lane = lax.broadcasted_iota(jnp.int32, t.shape, 1)
    r = pltpu.roll(t, HEAD_DIM, 1)
    lo = lane < HEAD_DIM
    return jnp.where(lo, t, r) if which == 0 else jnp.where(lo, r, t)


def _attn_masks(n):
    row = lax.broadcasted_iota(jnp.int32, (ATTN_BLOCK, ATTN_BLOCK), 0)
    col = lax.broadcasted_iota(jnp.int32, (ATTN_BLOCK, ATTN_BLOCK), 1)
    return col <= row, (col > row) & (n > 0), col < HEAD_DIM


def _attn_probs(qm, k2c, k2p, cur_ok, prev_ok, sink):
    scale = HEAD_DIM ** -0.5
    sc = lax.dot_general(qm, k2c, NT, preferred_element_type=F32) * scale
    sp = lax.dot_general(qm, k2p, NT, preferred_element_type=F32) * scale
    sc = jnp.where(cur_ok, sc, NEG)
    sp = jnp.where(prev_ok, sp, NEG)
    m = jnp.maximum(jnp.maximum(sc.max(1, keepdims=True), sp.max(1, keepdims=True)), sink)
    pc, pp = jnp.exp(sc - m), jnp.exp(sp - m)
    esink = jnp.exp(sink - m)
    inv = 1.0 / (pc.sum(1, keepdims=True) + pp.sum(1, keepdims=True) + esink)
    return pc * inv, pp * inv, esink * inv


def _attn_specs():
    blk = (ATTN_BLOCK, LANES)
    wide = (ATTN_BLOCK, Q_PER_KV_TILE * LANES)
    prev = lambda n: jnp.maximum(n - 1, 0)
    q_spec = pl.BlockSpec(wide, lambda t, n: (n, t))
    kc = pl.BlockSpec(blk, lambda t, n: (n, K_TILE0 + t))
    kp = pl.BlockSpec(blk, lambda t, n: (prev(n), K_TILE0 + t))
    vc = pl.BlockSpec(blk, lambda t, n: (n, V_TILE0 + t))
    vp = pl.BlockSpec(blk, lambda t, n: (prev(n), V_TILE0 + t))
    return q_spec, kc, kp, vc, vp, pl.BlockSpec(memory_space=pltpu.SMEM)


def _attn_fwd(qk, h, sinks):
    q_spec, kc_s, kp_s, vc_s, vp_s, smem = _attn_specs()

    def body(sink_ref, q_ref, kc_ref, kp_ref, vc_ref, vp_ref, o_ref, ob_ref):
        t, n = pl.program_id(0), pl.program_id(1)
        cur_ok, prev_ok, lo = _attn_masks(n)
        kc, kp = kc_ref[...].astype(F32), kp_ref[...].astype(F32)
        vc, vp = vc_ref[...], vp_ref[...]
        for kvl in range(2):
            k2c, k2p = _dup_half(kc, kvl).astype(MM_DTYPE), _dup_half(kp, kvl).astype(MM_DTYPE)
            v2c, v2p = _dup_half(vc, kvl).astype(MM_DTYPE), _dup_half(vp, kvl).astype(MM_DTYPE)
            for a in (2 * kvl, 2 * kvl + 1):
                qt = q_ref[:, a * LANES:(a + 1) * LANES].astype(F32)
                outs = []
                for hs in range(2):
                    qm = jnp.where(lo == (hs == 0), qt, 0.0).astype(MM_DTYPE)
                    sink = sink_ref[t * HEADS_PER_KV_TILE + 2 * a + hs]
                    pc, pp, _ = _attn_probs(qm, k2c, k2p, cur_ok, prev_ok, sink)
                    outs.append(lax.dot_general(pc.astype(MM_DTYPE), v2c, NN, preferred_element_type=F32)
                                + lax.dot_general(pp.astype(MM_DTYPE), v2p, NN, preferred_element_type=F32))
                o = jnp.where(lo, outs[0], outs[1])
                o_ref[:, a * LANES:(a + 1) * LANES] = o
                ob_ref[:, a * LANES:(a + 1) * LANES] = o.astype(ob_ref.dtype)

    return pl.pallas_call(
        body, name="attn_fwd", grid=(KV_TILES, N_QBLK),
        in_specs=[smem, q_spec, kc_s, kp_s, vc_s, vp_s], out_specs=[q_spec, q_spec],
        out_shape=[_sds((SEQ, ATTN_WIDTH)), _sds((SEQ, ATTN_WIDTH), MM_DTYPE)],
        compiler_params=_cparams(("parallel", "parallel")))(sinks, qk, qk, qk, h, h)


def _attn_bwd(qk, h, sinks, y, dy, dy_tile0, dep=None):
    deps = _as_list(dep)
    q_spec, kc_s, kp_s, vc_s, vp_s, smem = _attn_specs()
    blk = (ATTN_BLOCK, LANES)
    wide = (ATTN_BLOCK, Q_PER_KV_TILE * LANES)
    kv_out = pl.BlockSpec(blk, lambda t, n: (n, t))
    dy_spec = pl.BlockSpec(wide, lambda t, n: (n, t + dy_tile0))

    def body(sink_ref, q_ref, kc_ref, kp_ref, vc_ref, vp_ref, y_ref, dy_ref, *rest):
        dq_ref, dkc_ref, dkp_ref, dvc_ref, dvp_ref, dsk_ref = rest[-6:]
        t, n = pl.program_id(0), pl.program_id(1)
        cur_ok, prev_ok, lo = _attn_masks(n)
        scale = HEAD_DIM ** -0.5
        kc, kp = kc_ref[...].astype(F32), kp_ref[...].astype(F32)
        vc, vp = vc_ref[...], vp_ref[...]
        hrow = lax.broadcasted_iota(jnp.int32, (HEADS_PER_KV_TILE, LANES), 0)
        dsk = jnp.zeros((HEADS_PER_KV_TILE, LANES), F32)
        folded = []
        for kvl in range(2):
            k2c, k2p = _dup_half(kc, kvl).astype(MM_DTYPE), _dup_half(kp, kvl).astype(MM_DTYPE)
            v2c, v2p = _dup_half(vc, kvl).astype(MM_DTYPE), _dup_half(vp, kvl).astype(MM_DTYPE)
            acc = [jnp.zeros(blk, F32) for _ in range(4)]
            for a in (2 * kvl, 2 * kvl + 1):
                sl = slice(a * LANES, (a + 1) * LANES)
                qt = q_ref[:, sl].astype(F32)
                dot_, yt = dy_ref[:, sl], y_ref[:, sl]
                dqs = []
                for hs in range(2):
                    hm = lo == (hs == 0)
                    qm = jnp.where(hm, qt, 0.0).astype(MM_DTYPE)
                    dom = jnp.where(hm, dot_, 0.0).astype(MM_DTYPE)
                    hl = 2 * a + hs
                    sink = sink_ref[t * HEADS_PER_KV_TILE + hl]
                    pc, pp, psink = _attn_probs(qm, k2c, k2p, cur_ok, prev_ok, sink)
                    delta = jnp.sum(jnp.where(hm, dot_ * yt, 0.0), axis=1, keepdims=True)
                    dpc = lax.dot_general(dom, v2c, NT, preferred_element_type=F32)
                    dpp = lax.dot_general(dom, v2p, NT, preferred_element_type=F32)
                    dsc = (pc * (dpc - delta) * scale).astype(MM_DTYPE)
                    dsp = (pp * (dpp - delta) * scale).astype(MM_DTYPE)
                    dqs.append(lax.dot_general(dsc, k2c, NN, preferred_element_type=F32)
                               + lax.dot_general(dsp, k2p, NN, preferred_element_type=F32))
                    acc[0] += lax.dot_general(dsc, qm, TN, preferred_element_type=F32)
                    acc[1] += lax.dot_general(dsp, qm, TN, preferred_element_type=F32)
                    acc[2] += lax.dot_general(pc.astype(MM_DTYPE), dom, TN, preferred_element_type=F32)
                    acc[3] += lax.dot_general(pp.astype(MM_DTYPE), dom, TN, preferred_element_type=F32)
                    dsk = dsk + jnp.where(hrow == hl, -jnp.sum(psink * delta), 0.0)
                dq_ref[:, sl] = jnp.where(lo, dqs[0], dqs[1])
            folded.append([x + pltpu.roll(x, HEAD_DIM, 1) for x in acc])
        for o_ref, i in ((dkc_ref, 0), (dkp_ref, 1), (dvc_ref, 2), (dvp_ref, 3)):
            o_ref[...] = jnp.where(lo, folded[0][i], folded[1][i])

        @pl.when(n == 0)
        def _():
            dsk_ref[...] = jnp.zeros_like(dsk_ref)

        dsk_ref[...] += dsk

    kv_shape = _sds((SEQ, KV_WIDTH))
    return pl.pallas_call(
        body, name="attn_bwd", grid=(KV_TILES, N_QBLK),
        in_specs=[smem, q_spec, kc_s, kp_s, vc_s, vp_s, q_spec, dy_spec] + [ANY] * len(deps),
        out_specs=[q_spec, kv_out, kv_out, kv_out, kv_out,
                   pl.BlockSpec((None, HEADS_PER_KV_TILE, LANES), lambda t, n: (t, 0, 0))],
        out_shape=[_sds((SEQ, ATTN_WIDTH)), kv_shape, kv_shape, kv_shape, kv_shape,
                   _sds((KV_TILES, HEADS_PER_KV_TILE, LANES))],
        compiler_params=_cparams(("parallel", "arbitrary")))(sinks, qk, qk, qk, h, h, y, dy, *deps)


def _attn_dh(dq, dkc, dkp, dvc, dvp, cos_t, nsin_t):
    n_tiles = Q_TILES + 2 * KV_TILES
    nxt = lambda n: jnp.minimum(n + 1, N_QBLK - 1)

    def body(dq_ref, kc_ref, kp_ref, vc_ref, vp_ref, c_ref, s_ref, o_ref):
        has_next = pl.program_id(0) < N_QBLK - 1
        cos, sin = c_ref[...], s_ref[...]

        def unrope(t):
            return t * cos + _rotate_half(t) * sin

        for j in range(Q_TILES):
            sl = slice(j * LANES, (j + 1) * LANES)
            o_ref[:, sl] = unrope(dq_ref[:, sl]).astype(o_ref.dtype)
        for j in range(KV_TILES):
            sl = slice(j * LANES, (j + 1) * LANES)
            t = kc_ref[:, sl] + jnp.where(has_next, kp_ref[:, sl], 0.0)
            o_ref[:, ATTN_WIDTH + j * LANES:ATTN_WIDTH + (j + 1) * LANES] = unrope(t).astype(o_ref.dtype)
        o_ref[:, ATTN_WIDTH + KV_WIDTH:] = (vc_ref[...] + jnp.where(has_next, vp_ref[...], 0.0)).astype(o_ref.dtype)

    qb, kb, tb = (ATTN_BLOCK, ATTN_WIDTH), (ATTN_BLOCK, KV_WIDTH), (ATTN_BLOCK, LANES)
    return pl.pallas_call(
        body, name="attn_dh", grid=(N_QBLK,),
        in_specs=[pl.BlockSpec(qb, lambda n: (n, 0)),
                  pl.BlockSpec(kb, lambda n: (n, 0)), pl.BlockSpec(kb, lambda n: (nxt(n), 0)),
                  pl.BlockSpec(kb, lambda n: (n, 0)), pl.BlockSpec(kb, lambda n: (nxt(n), 0)),
                  pl.BlockSpec(tb, lambda n: (n, 0)), pl.BlockSpec(tb, lambda n: (n, 0))],
        out_specs=pl.BlockSpec((ATTN_BLOCK, n_tiles * LANES), lambda n: (n, 0)),
        out_shape=_sds((SEQ, n_tiles * LANES), MM_DTYPE),
        compiler_params=_cparams(("parallel",)))(dq, dkc, dkp, dvc, dvp, cos_t, nsin_t)


POOL_TILE0 = (ATTN_WIDTH + 2 * KV_WIDTH) // POOL_WIDTH


def _shift_rows(x, d, down):
    n = x.shape[0]
    row = lax.broadcasted_iota(jnp.int32, x.shape, 0)
    if down:
        return jnp.where(row >= d, pltpu.roll(x, d, 0), 0.0)
    return jnp.where(row < n - d, pltpu.roll(x, n - d, 0), 0.0)


def _window_sum(x, w, down):
    d = 1
    while d < w:
        x = x + _shift_rows(x, d, down)
        d *= 2
    return x


def _pool_z(u, w):
    t = lax.broadcasted_iota(jnp.int32, u.shape, 0).astype(F32)
    cnt = jnp.minimum(t + 1.0, float(w))
    return _window_sum(u, w, True) / cnt - u, cnt


def _pool_fwd(h, pool_w, pool_scale):
    def body(u_ref, w_ref, s_ref, o_ref):
        for gi, w in enumerate(POOL_WINDOWS):
            sl = slice(gi * POOL_GROUP, (gi + 1) * POOL_GROUP)
            z, _ = _pool_z(u_ref[:, sl], w)
            o_ref[:, sl] = (lax.dot_general(z.astype(MM_DTYPE), w_ref[gi].astype(MM_DTYPE), NN,
                                            preferred_element_type=F32) * s_ref[:, sl]).astype(o_ref.dtype)

    return pl.pallas_call(
        body, name="pool_fwd", grid=(1,),
        in_specs=[pl.BlockSpec((SEQ, POOL_WIDTH), lambda i: (0, POOL_TILE0)),
                  pl.BlockSpec(pool_w.shape, lambda i: (0, 0, 0)), pl.BlockSpec((1, POOL_WIDTH), lambda i: (0, 0))],
        out_specs=pl.BlockSpec((SEQ, POOL_WIDTH), lambda i: (0, 0)),
        out_shape=_sds((SEQ, POOL_WIDTH), MM_DTYPE), compiler_params=_cparams(("arbitrary",)))(h, pool_w, pool_scale)


def _pool_bwd(h, pool_w, pool_scale, dmix, dy_tile0):
    def body(u_ref, w_ref, s_ref, dy_ref, du_ref, dw_ref, ds_ref):
        for gi, w in enumerate(POOL_WINDOWS):
            sl = slice(gi * POOL_GROUP, (gi + 1) * POOL_GROUP)
            z, cnt = _pool_z(u_ref[:, sl], w)
            zb, wb = z.astype(MM_DTYPE), w_ref[gi].astype(MM_DTYPE)
            dy = dy_ref[:, sl]
            zp = lax.dot_general(zb, wb, NN, preferred_element_type=F32)
            ds_ref[:, sl] = jnp.sum(dy * zp, axis=0, keepdims=True)
            dyo = (dy * s_ref[:, sl]).astype(MM_DTYPE)
            dw_ref[gi] = lax.dot_general(zb, dyo, TN, preferred_element_type=F32)
            dz = lax.dot_general(dyo, wb, NT, preferred_element_type=F32)
            du_ref[:, sl] = (_window_sum(dz / cnt, w, False) - dz).astype(du_ref.dtype)

    return pl.pallas_call(
        body, name="pool_bwd", grid=(1,),
        in_specs=[pl.BlockSpec((SEQ, POOL_WIDTH), lambda i: (0, POOL_TILE0)),
                  pl.BlockSpec(pool_w.shape, lambda i: (0, 0, 0)), pl.BlockSpec((1, POOL_WIDTH), lambda i: (0, 0)),
                  pl.BlockSpec((SEQ, POOL_WIDTH), lambda i: (0, dy_tile0))],
        out_specs=[pl.BlockSpec((SEQ, POOL_WIDTH), lambda i: (0, 0)), pl.BlockSpec(pool_w.shape, lambda i: (0, 0, 0)),
                   pl.BlockSpec((1, POOL_WIDTH), lambda i: (0, 0))],
        out_shape=[_sds((SEQ, POOL_WIDTH), MM_DTYPE), _sds(pool_w.shape), _sds((1, POOL_WIDTH))],
        compiler_params=_cparams(("arbitrary",)))(h, pool_w, pool_scale, dmix)


def _ssm_discretize(lr, li, ldt, br, bi):
    dt = jnp.exp(ldt)
    mag = jnp.exp(lr * dt)
    ar, ai = mag * jnp.cos(li * dt), mag * jnp.sin(li * dt)
    nr, ni = ar - 1.0, ai
    den = lr * lr + li * li
    zr = (nr * lr + ni * li) / den
    zi = (ni * lr - nr * li) / den
    return ar, ai, zr * br - zi * bi, zr * bi + zi * br


def _ssm_prep(lr, li, ldt, br, bi):
    def body(lr_ref, li_ref, ldt_ref, br_ref, bi_ref, ar_ref, ai_ref, bbr_ref, bbi_ref):
        outs = _ssm_discretize(lr_ref[...], li_ref[...], ldt_ref[...], br_ref[...], bi_ref[...])
        for o, v in zip((ar_ref, ai_ref, bbr_ref, bbi_ref), outs):
            o[...] = v

    row, mat = _sds((1, SSM_CH)), _sds((SSM_GROUP, SSM_CH))
    return pl.pallas_call(body, name="ssm_prep", out_shape=[row, row, mat, mat])(lr, li, ldt, br, bi)


def _ssm_prep_bwd(lr, li, ldt, br, bi, dar8, dai8, dbbr, dbbi):
    def body(lr_ref, li_ref, ldt_ref, br_ref, bi_ref, dar_ref, dai_ref, dbbr_ref, dbbi_ref, *outs):
        args = (lr_ref[...], li_ref[...], ldt_ref[...], br_ref[...], bi_ref[...])
        _, vjp = jax.vjp(_ssm_discretize, *args)
        cot = (jnp.sum(dar_ref[...], axis=0, keepdims=True), jnp.sum(dai_ref[...], axis=0, keepdims=True),
               dbbr_ref[...], dbbi_ref[...])
        for o, v in zip(outs, vjp(cot)):
            o[...] = v

    row, mat = _sds((1, SSM_CH)), _sds((SSM_GROUP, SSM_CH))
    return pl.pallas_call(body, name="ssm_prep_bwd", out_shape=[row, row, row, mat, mat])(
        lr, li, ldt, br, bi, dar8, dai8, dbbr, dbbi)


def _ssm_diag(name, full):
    tiles = SCAN_CW // LANES
    groups = LANES // SSM_STATE
    col = lambda t, part: (t // tiles) * 2 * tiles + part * tiles + t % tiles

    def body(x_ref, o_ref):
        lane = lax.broadcasted_iota(jnp.int32, (SSM_GROUP, LANES), 1)
        out = x_ref[0:SSM_GROUP, :]
        for k in range(1, groups):
            out = jnp.where(lane >= k * SSM_STATE, x_ref[k * SSM_GROUP:(k + 1) * SSM_GROUP, :], out)
        o_ref[...] = out

    return pl.pallas_call(
        body, name=name, grid=(SSM_CH // LANES, 2),
        in_specs=[pl.BlockSpec((groups * SSM_GROUP, LANES), lambda t, part: (t, col(t, part)))],
        out_specs=pl.BlockSpec((SSM_GROUP, LANES), lambda t, part: (0, col(t, part))),
        out_shape=_sds((SSM_GROUP, 2 * SSM_CH)), compiler_params=_cparams(("parallel", "parallel")))(full)


def _scan_layout(re, im):
    r = re.shape[0]
    return jnp.stack([re.reshape(r, SCAN_NB, SCAN_CW), im.reshape(r, SCAN_NB, SCAN_CW)], axis=2).reshape(r, 2 * SSM_CH)


def _scan_unlayout(x):
    r = x.shape[0]
    x = x.reshape(r, SCAN_NB, 2, SCAN_CW)
    return x[:, :, 0].reshape(r, SSM_CH), x[:, :, 1].reshape(r, SSM_CH)


def _time_permute(u):
    s, c = u.shape
    return u.reshape(SUBLANES, s // SUBLANES, c).transpose(1, 0, 2).reshape(s, c)


def _time_unpermute(u):
    s, c = u.shape
    return u.reshape(s // SUBLANES, SUBLANES, c).transpose(1, 0, 2).reshape(s, c)


def _ssm_scan(name, a_vec, x, reverse, s_prev=None):
    nsteps = SEQ // SUBLANES
    cw = SCAN_CW
    with_da = s_prev is not None

    def body(a_ref, x_ref, *rest):
        if with_da:
            s_ref, o_ref, da_ref = rest
        else:
            o_ref, = rest
        ar = jnp.broadcast_to(a_ref[:, :cw], (SUBLANES, cw))
        ai = jnp.broadcast_to(a_ref[:, cw:], (SUBLANES, cw))
        seg = lax.broadcasted_iota(jnp.int32, (SUBLANES, cw), 0)

        def toward(v):
            if reverse:
                return jnp.where(seg < SUBLANES - 1, pltpu.roll(v, SUBLANES - 1, 0), 0.0)
            return jnp.where(seg >= 1, pltpu.roll(v, 1, 0), 0.0)

        def rows(j):
            jj = nsteps - 1 - j if reverse else j
            return pl.ds(pl.multiple_of(jj * SUBLANES, SUBLANES), SUBLANES)

        def cmul(pr, pi, qr, qi):
            return pr * qr - pi * qi, pr * qi + pi * qr

        def local(j, c):
            sr, si = c
            r = rows(j)
            mr, mi = cmul(ar, ai, sr, si)
            return mr + x_ref[r, :cw], mi + x_ref[r, cw:]

        zero = jnp.zeros((SUBLANES, cw), F32)
        fr, fi = lax.fori_loop(0, nsteps, local, (zero, zero))

        def power(_, c):
            return cmul(ar, ai, *c)

        pr, pi = lax.fori_loop(0, nsteps - 1, power, (ar, ai))
        tr, ti = fr, fi
        for _ in range(SUBLANES - 1):
            mr, mi = cmul(pr, pi, toward(tr), toward(ti))
            tr, ti = fr + mr, fi + mi
        init = (toward(tr), toward(ti))

        def full(j, c):
            r = rows(j)
            if with_da:
                sr, si, dar, dai = c
            else:
                sr, si = c
            mr, mi = cmul(ar, ai, sr, si)
            sr, si = mr + x_ref[r, :cw], mi + x_ref[r, cw:]
            o_ref[r, :cw] = sr
            o_ref[r, cw:] = si
            if not with_da:
                return sr, si
            jj = nsteps - 1 - j
            rp = pl.ds(pl.multiple_of(jnp.maximum(jj - 1, 0) * SUBLANES, SUBLANES), SUBLANES)
            last = pl.ds((nsteps - 1) * SUBLANES, SUBLANES)
            first = jj == 0
            spr = jnp.where(first, jnp.where(seg >= 1, pltpu.roll(s_ref[last, :cw], 1, 0), 0.0), s_ref[rp, :cw])
            spi = jnp.where(first, jnp.where(seg >= 1, pltpu.roll(s_ref[last, cw:], 1, 0), 0.0), s_ref[rp, cw:])
            return sr, si, dar + sr * spr + si * spi, dai + si * spr - sr * spi

        if with_da:
            _, _, dar, dai = lax.fori_loop(0, nsteps, full, init + (zero, zero))
            da_ref[:, :cw] = dar
            da_ref[:, cw:] = dai
        else:
            lax.fori_loop(0, nsteps, full, init)

    blk = pl.BlockSpec((SEQ, 2 * cw), lambda b: (0, b))
    a_spec = pl.BlockSpec((1, 2 * cw), lambda b: (0, b))
    in_specs, args = [a_spec, blk], [a_vec, x]
    out_specs, out_shape = blk, _sds((SEQ, 2 * SSM_CH))
    if with_da:
        in_specs, args = in_specs + [blk], args + [s_prev]
        out_specs = [blk, pl.BlockSpec((SUBLANES, 2 * cw), lambda b: (0, b))]
        out_shape = [out_shape, _sds((SUBLANES, 2 * SSM_CH))]
    return pl.pallas_call(body, name=name, grid=(SCAN_NB,), in_specs=in_specs, out_specs=out_specs,
                          out_shape=out_shape, compiler_params=_cparams(("parallel",)))(*args)


def _ssm_gelu(yp, up, dvec):
    tm = _tile(SEQ, 512)

    def body(y_ref, u_ref, d_ref, yf_ref, g_ref):
        yf = y_ref[...] + d_ref[...] * u_ref[...]
        yf_ref[...] = yf
        g_ref[...] = jax.nn.gelu(yf).astype(g_ref.dtype)

    blk = pl.BlockSpec((tm, SSM_WIDTH), lambda i: (i, 0))
    row = pl.BlockSpec((1, SSM_WIDTH), lambda i: (0, 0))
    return pl.pallas_call(body, name="ssm_gelu", grid=(SEQ // tm,), in_specs=[blk, blk, row], out_specs=[blk, blk],
                          out_shape=[_sds((SEQ, SSM_WIDTH)), _sds((SEQ, SSM_WIDTH), MM_DTYPE)],
                          compiler_params=_cparams(("parallel",)))(yp, up, dvec)


def _ssm_gelu_bwd(yf, dgy, up, dvec):
    tm = _tile(SEQ, 512)

    def body(yf_ref, dg_ref, u_ref, d_ref, dyf_ref, du_ref, dd_ref):
        _, vjp = jax.vjp(jax.nn.gelu, yf_ref[...])
        dyf, = vjp(dg_ref[...])
        dyf_ref[...] = dyf.astype(dyf_ref.dtype)
        du_ref[...] = d_ref[...] * dyf

        @pl.when(pl.program_id(0) == 0)
        def _():
            dd_ref[...] = jnp.zeros_like(dd_ref)

        dd_ref[...] += jnp.sum(dyf * u_ref[...], axis=0, keepdims=True)

    blk = pl.BlockSpec((tm, SSM_WIDTH), lambda i: (i, 0))
    row = pl.BlockSpec((1, SSM_WIDTH), lambda i: (0, 0))
    return pl.pallas_call(body, name="ssm_gelu_bwd", grid=(SEQ // tm,), in_specs=[blk, blk, blk, row],
                          out_specs=[blk, blk, row],
                          out_shape=[_sds((SEQ, SSM_WIDTH), MM_DTYPE), _sds((SEQ, SSM_WIDTH)), _sds((1, SSM_WIDTH))],
                          compiler_params=_cparams(("arbitrary",)))(yf, dgy, up, dvec)


def _glu(ab):
    return ab[:, :SSM_WIDTH] * jax.nn.sigmoid(ab[:, SSM_WIDTH:])


def _ssm_glu(ab):
    tm = _tile(SEQ, 512)

    def body(ab_ref, o_ref):
        o_ref[...] = _glu(ab_ref[...]).astype(o_ref.dtype)

    return pl.pallas_call(body, name="ssm_glu", grid=(SEQ // tm,),
                          in_specs=[pl.BlockSpec((tm, 2 * SSM_WIDTH), lambda i: (i, 0))],
                          out_specs=pl.BlockSpec((tm, SSM_WIDTH), lambda i: (i, 0)),
                          out_shape=_sds((SEQ, SSM_WIDTH), MM_DTYPE), compiler_params=_cparams(("parallel",)))(ab)


def _ssm_glu_bwd(ab, dout):
    tm = _tile(SEQ, 512)

    def body(ab_ref, do_ref, dab_ref):
        _, vjp = jax.vjp(_glu, ab_ref[...])
        dab, = vjp(do_ref[...])
        dab_ref[...] = dab.astype(dab_ref.dtype)

    return pl.pallas_call(body, name="ssm_glu_bwd", grid=(SEQ // tm,),
                          in_specs=[pl.BlockSpec((tm, 2 * SSM_WIDTH), lambda i: (i, 0)),
                                    pl.BlockSpec((tm, SSM_WIDTH), lambda i: (i, 0))],
                          out_specs=pl.BlockSpec((tm, 2 * SSM_WIDTH), lambda i: (i, 0)),
                          out_shape=_sds((SEQ, 2 * SSM_WIDTH), MM_DTYPE), compiler_params=_cparams(("parallel",)))(ab, dout)


def _add2(name, a, b, out_dtype):
    tm = _tile(a.shape[0], 512)

    def body(a_ref, b_ref, o_ref):
        o_ref[...] = (a_ref[...] + b_ref[...]).astype(o_ref.dtype)

    blk = pl.BlockSpec((tm, a.shape[1]), lambda i: (i, 0))
    return pl.pallas_call(body, name=name, grid=(a.shape[0] // tm,), in_specs=[blk, blk], out_specs=blk,
                          out_shape=_sds(a.shape, out_dtype), compiler_params=_cparams(("parallel",)))(a, b)


def _layer_norm(r, g, b):
    mu = r.mean(-1, keepdims=True)
    var = jnp.square(r - mu).mean(-1, keepdims=True)
    return (r - mu) * lax.rsqrt(var + LN_EPS) * g + b


def _ln_fwd(name, x, y, g, b):
    tm = _tile(SEQ, 256)

    def body(x_ref, y_ref, g_ref, b_ref, r_ref, o_ref, ob_ref):
        r = DEEPNORM_ALPHA * x_ref[...] + y_ref[...]
        r_ref[...] = r
        o = _layer_norm(r, g_ref[...], b_ref[...])
        o_ref[...] = o
        ob_ref[...] = o.astype(ob_ref.dtype)

    blk = pl.BlockSpec((tm, D_MODEL), lambda i: (i, 0))
    row = pl.BlockSpec((1, D_MODEL), lambda i: (0, 0))
    return pl.pallas_call(body, name=name, grid=(SEQ // tm,), in_specs=[blk, blk, row, row], out_specs=[blk, blk, blk],
                          out_shape=[_sds((SEQ, D_MODEL))] * 2 + [_sds((SEQ, D_MODEL), MM_DTYPE)],
                          compiler_params=_cparams(("parallel",)))(x, y, g, b)


def _ln_bwd(name, r, g, b, da, db=None, dep=None):
    tm = _tile(SEQ, 256)
    two = db is not None
    deps = _as_list(dep)

    def body(r_ref, g_ref, b_ref, da_ref, *rest):
        dr_ref, drb_ref, dg_ref, dbeta_ref = rest[-4:]
        dout = DEEPNORM_ALPHA * da_ref[...] + rest[0][...] if two else da_ref[...]
        _, vjp = jax.vjp(_layer_norm, r_ref[...], g_ref[...], b_ref[...])
        dr, dg, dbeta = vjp(dout)
        dr_ref[...] = dr
        drb_ref[...] = dr.astype(drb_ref.dtype)

        @pl.when(pl.program_id(0) == 0)
        def _():
            dg_ref[...] = jnp.zeros_like(dg_ref)
            dbeta_ref[...] = jnp.zeros_like(dbeta_ref)

        dg_ref[...] += dg
        dbeta_ref[...] += dbeta

    blk = pl.BlockSpec((tm, D_MODEL), lambda i: (i, 0))
    row = pl.BlockSpec((1, D_MODEL), lambda i: (0, 0))
    args = [r, g, b, da] + ([db] if two else []) + deps
    return pl.pallas_call(body, name=name, grid=(SEQ // tm,),
                          in_specs=[blk, row, row, blk] + ([blk] if two else []) + [ANY] * len(deps),
                          out_specs=[blk, blk, row, row],
                          out_shape=[_sds((SEQ, D_MODEL)), _sds((SEQ, D_MODEL), MM_DTYPE), _sds((1, D_MODEL)), _sds((1, D_MODEL))],
                          compiler_params=_cparams(("arbitrary",)))(*args)


FFN_TM = 128
HALO = SUBLANES


def _conv_taps(cur, halo):
    row = lax.broadcasted_iota(jnp.int32, cur.shape, 0)
    h1 = jnp.where(row == 0, halo[HALO - 1:HALO, :], pltpu.roll(cur, 1, 0))
    h2 = jnp.where(row == 0, halo[HALO - 2:HALO - 1, :], jnp.where(row == 1, halo[HALO - 1:HALO, :], pltpu.roll(cur, 2, 0)))
    return h1, h2


def _conv_fwd(cur, halo, w_ref, b_ref):
    h1, h2 = _conv_taps(cur, halo)
    return b_ref[...] + h2 * w_ref[0:1, :] + h1 * w_ref[1:2, :] + cur * w_ref[2:3, :], h1, h2


def _gate(val, gate):
    return jax.nn.silu(gate) * val


def _ffn_specs(tm):
    nb = tm // HALO
    cur = lambda off: pl.BlockSpec((None, tm, FS), lambda p, i: (p + off, i, 0))
    halo = lambda off: pl.BlockSpec((None, HALO, FS), lambda p, i: (p + off, jnp.maximum(i * nb - 1, 0), 0))
    cw = lambda off: pl.BlockSpec((None, CONV_WIDTH, FS), lambda p, i: (p + off, 0, 0))
    cb = lambda off: pl.BlockSpec((None, 1, FS), lambda p, i: (p + off, 0, 0))
    return cur, halo, cw, cb


def _ffn_act(hf, conv_w, conv_b):
    tm = _tile(SEQ, FFN_TM, SUBLANES)
    cur, halo, cw, cb = _ffn_specs(tm)

    def body(v_ref, vh_ref, g_ref, gh_ref, wv_ref, wg_ref, bv_ref, bg_ref, o_ref):
        live = pl.program_id(1) > 0
        vh = jnp.where(live, vh_ref[...], 0.0)
        gh = jnp.where(live, gh_ref[...], 0.0)
        val, _, _ = _conv_fwd(v_ref[...], vh, wv_ref, bv_ref)
        gate, _, _ = _conv_fwd(g_ref[...], gh, wg_ref, bg_ref)
        o_ref[...] = _gate(val, gate).astype(o_ref.dtype)

    return pl.pallas_call(
        body, name="ffn_act", grid=(2, SEQ // tm),
        in_specs=[cur(0), halo(0), cur(2), halo(2), cw(0), cw(2), cb(0), cb(2)],
        out_specs=pl.BlockSpec((None, tm, FS), lambda p, i: (p, i, 0)),
        out_shape=_sds((2, SEQ, FS), MM_DTYPE), compiler_params=_cparams(("parallel", "parallel")))(
            hf, hf, hf, hf, conv_w, conv_w, conv_b, conv_b)


def _ffn_act_bwd(hf, conv_w, conv_b, dact, dep=None):
    tm = _tile(SEQ, FFN_TM, SUBLANES)
    cur, halo, cw, cb = _ffn_specs(tm)
    deps = _as_list(dep)

    def body(v_ref, vh_ref, g_ref, gh_ref, wv_ref, wg_ref, bv_ref, bg_ref, da_ref, *rest):
        dhc_ref, dw_ref, dbias_ref = rest[-3:]
        dv_ref, dg_ref = dhc_ref.at[0], dhc_ref.at[1]
        dwv_ref, dwg_ref = dw_ref.at[0], dw_ref.at[1]
        dbv_ref, dbg_ref = dbias_ref.at[0], dbias_ref.at[1]
        i = pl.program_id(1)
        live = i > 0
        vh = jnp.where(live, vh_ref[...], 0.0)
        gh = jnp.where(live, gh_ref[...], 0.0)
        vcur, gcur = v_ref[...], g_ref[...]
        val, v1, v2 = _conv_fwd(vcur, vh, wv_ref, bv_ref)
        gate, g1, g2 = _conv_fwd(gcur, gh, wg_ref, bg_ref)
        _, vjp = jax.vjp(_gate, val, gate)
        dval, dgate = vjp(da_ref[...])
        dv_ref[...] = dval
        dg_ref[...] = dgate

        @pl.when(i == 0)
        def _():
            dw_ref[...] = jnp.zeros_like(dw_ref)
            dbias_ref[...] = jnp.zeros_like(dbias_ref)

        for d, taps, dwk_ref, dbk_ref in ((dval, (v2, v1, vcur), dwv_ref, dbv_ref), (dgate, (g2, g1, gcur), dwg_ref, dbg_ref)):
            for k in range(CONV_WIDTH):
                dwk_ref[k:k + 1, :] += jnp.sum(d * taps[k], axis=0, keepdims=True)
            dbk_ref[...] += jnp.sum(d, axis=0, keepdims=True)

    return pl.pallas_call(
        body, name="ffn_act_bwd", grid=(2, SEQ // tm),
        in_specs=[cur(0), halo(0), cur(2), halo(2), cw(0), cw(2), cb(0), cb(2),
                  pl.BlockSpec((None, tm, FS), lambda p, i: (p, i, 0))] + [ANY] * len(deps),
        out_specs=[pl.BlockSpec((None, 2, tm, FS), lambda p, i: (p, 0, i, 0)),
                   pl.BlockSpec((None, 2, CONV_WIDTH, FS), lambda p, i: (p, 0, 0, 0)),
                   pl.BlockSpec((None, 2, 1, FS), lambda p, i: (p, 0, 0, 0))],
        out_shape=[_sds((2, 2, SEQ, FS)), _sds((2, 2, CONV_WIDTH, FS)), _sds((2, 2, 1, FS))],
        compiler_params=_cparams(("parallel", "arbitrary")))(hf, hf, hf, hf, conv_w, conv_w, conv_b, conv_b, dact, *deps)


def _conv_bwd_input(dhc, conv_w):
    tm = _tile(SEQ, FFN_TM, SUBLANES)
    nb = tm // HALO
    nblk = SEQ // tm

    def body(d_ref, nx_ref, w_ref, o_ref):
        cur = d_ref[...]
        nxt = jnp.where(pl.program_id(2) < nblk - 1, nx_ref[...], 0.0)
        row = lax.broadcasted_iota(jnp.int32, cur.shape, 0)
        d1 = jnp.where(row == tm - 1, nxt[0:1, :], pltpu.roll(cur, tm - 1, 0))
        d2 = jnp.where(row == tm - 1, nxt[1:2, :], jnp.where(row == tm - 2, nxt[0:1, :], pltpu.roll(cur, tm - 2, 0)))
        o_ref[...] = (cur * w_ref[2:3, :] + d1 * w_ref[1:2, :] + d2 * w_ref[0:1, :]).astype(o_ref.dtype)

    blk = pl.BlockSpec((None, None, tm, FS), lambda p, kd, i: (p, kd, i, 0))
    return pl.pallas_call(
        body, name="conv_bwd_input", grid=(2, 2, nblk),
        in_specs=[blk, pl.BlockSpec((None, None, HALO, FS), lambda p, kd, i: (p, kd, jnp.minimum((i + 1) * nb, SEQ // HALO - 1), 0)),
                  pl.BlockSpec((None, CONV_WIDTH, FS), lambda p, kd, i: (2 * kd + p, 0, 0))],
        out_specs=blk, out_shape=_sds((2, 2, SEQ, FS), MM_DTYPE),
        compiler_params=_cparams(("parallel", "parallel", "parallel")))(dhc, dhc, conv_w)


def _loss(y, target):
    tm = _tile(SEQ, 256)

    def body(y_ref, t_ref, dy_ref, l_ref):
        err = y_ref[...] - t_ref[...]
        dy_ref[...] = err * (1.0 / D_MODEL)

        @pl.when(pl.program_id(0) == 0)
        def _():
            l_ref[...] = jnp.zeros_like(l_ref)

        l_ref[...] += 0.5 * jnp.sum(jnp.mean(jnp.square(err), axis=-1))

    blk = pl.BlockSpec((tm, D_MODEL), lambda i: (i, 0))
    return pl.pallas_call(body, name="loss", grid=(SEQ // tm,), in_specs=[blk, blk],
                          out_specs=[blk, pl.BlockSpec((SUBLANES, LANES), lambda i: (0, 0))],
                          out_shape=[_sds((SEQ, D_MODEL)), _sds((SUBLANES, LANES))],
                          compiler_params=_cparams(("arbitrary",)))(y, target)


ADAM_BLOCK_BYTES = 3 << 19
ELEMENTWISE_COLS = 1024


def _adamw_math(w, g, m, v):
    nm = ADAM_B1 * m + (1.0 - ADAM_B1) * g
    nv = ADAM_B2 * v + (1.0 - ADAM_B2) * jnp.square(g)
    m_hat = nm / (1.0 - ADAM_B1 ** ADAM_STEP)
    v_hat = nv / (1.0 - ADAM_B2 ** ADAM_STEP)
    return -ADAM_LR * (m_hat / (jnp.sqrt(v_hat) + ADAM_EPS) + ADAM_WD * w), nm, nv


def _adamw(name, w, g, m, v):
    r, c = w.shape
    tr = _tile(r, max(SUBLANES, ADAM_BLOCK_BYTES // (4 * c)), SUBLANES)

    def body(w_ref, g_ref, m_ref, v_ref, d_ref, nm_ref, nv_ref):
        d_ref[...], nm_ref[...], nv_ref[...] = _adamw_math(w_ref[...], g_ref[...], m_ref[...], v_ref[...])

    blk = pl.BlockSpec((tr, c), lambda i: (i, 0))
    return pl.pallas_call(body, name=name, grid=(r // tr,), in_specs=[blk] * 4, out_specs=[blk] * 3,
                          out_shape=[_sds((r, c))] * 3, compiler_params=_cparams(("parallel",)))(w, g, m, v)


def _adamw_big(name, l, c_idx, w, m, v, g_own, g_got, prev):
    depth, _, r, c = w.shape
    tc = _tile(c, ELEMENTWISE_COLS)
    tr = _tile(r, max(SUBLANES, ADAM_BLOCK_BYTES // (4 * tc)), SUBLANES)

    def body(c_ref, w_ref, m_ref, v_ref, own_ref, got_ref, *rest):
        g_ref, d_ref, nm_ref, nv_ref = rest[-4:]
        g = jnp.where(pl.program_id(0) == c_ref[0], own_ref[...], got_ref[...])
        g_ref[...] = g
        d_ref[...], nm_ref[...], nv_ref[...] = _adamw_math(w_ref[...], g, m_ref[...], v_ref[...])

    stacked = pl.BlockSpec((None, None, tr, tc), lambda h, i, j, cr: (l, h, i, j))
    own = pl.BlockSpec((tr, tc), lambda h, i, j, cr: (jnp.where(h == cr[0], i, 0), jnp.where(h == cr[0], j, 0)))
    got = pl.BlockSpec((tr, tc), lambda h, i, j, cr: (jnp.where(h == cr[0], 0, i), jnp.where(h == cr[0], 0, j)))
    grid_spec = pltpu.PrefetchScalarGridSpec(
        num_scalar_prefetch=1, grid=(2, r // tr, c // tc),
        in_specs=[stacked] * 3 + [own, got] + ([ANY] * 4 if prev else []), out_specs=[stacked] * 4)
    return pl.pallas_call(
        body, name=name, grid_spec=grid_spec, out_shape=[_sds((depth, 2, r, c))] * 4,
        input_output_aliases={6 + k: k for k in range(4)} if prev else {},
        compiler_params=_cparams(("arbitrary", "arbitrary", "arbitrary")))(c_idx, w, m, v, g_own, g_got, *(prev or ()))


ANY = pl.BlockSpec(memory_space=pl.ANY)


def _place():
    x, y, c = lax.axis_index("x"), lax.axis_index("y"), lax.axis_index("c")
    chips = [(1 - x, y), (x, 1 - y), (1 - x, 1 - y)]
    return x, y, c, chips


def _cast_place(name, w, l, me_idx, out_dtype):
    _, _, r, c = w.shape
    tr = _tile(r, max(2 * SUBLANES, COPY_BLOCK_BYTES // (4 * c)), 2 * SUBLANES)

    def body(me_ref, w_ref, o_ref):
        o_ref[...] = w_ref[...].astype(o_ref.dtype)

    grid_spec = pltpu.PrefetchScalarGridSpec(
        num_scalar_prefetch=1, grid=(2, r // tr),
        in_specs=[pl.BlockSpec((None, None, tr, c), lambda h, i, me: (l, h, i, 0))],
        out_specs=pl.BlockSpec((None, None, tr, c), lambda h, i, me: (me[0], h, i, 0)))
    return pl.pallas_call(body, name=name, grid_spec=grid_spec, out_shape=_sds((N_CHIPS, 2, r, c), out_dtype),
                          compiler_params=_cparams(("parallel", "parallel")))(me_idx, w)


HBM = pl.BlockSpec(memory_space=pltpu.HBM)
SEM = pl.BlockSpec(memory_space=pltpu.SEMAPHORE)
TOKEN = (SUBLANES, LANES)


def _comm_call(name, body, hbm, sems_in=(), after=None, sems_out=(), token=False):
    n, k = len(hbm), len(sems_out)
    ins = [pltpu.with_memory_space_constraint(a, pltpu.HBM) for a in hbm] + list(sems_in)
    in_specs = [HBM] * n + [SEM] * len(sems_in)
    if after is not None:
        ins.append(after)
        in_specs.append(ANY)
    out_shape = [pltpu.SemaphoreType.DMA((s,)) for s in sems_out] + [pltpu.HBM(a.shape, a.dtype) for a in hbm]
    out_specs = [SEM] * k + [HBM] * n
    if token:
        out_shape.append(_sds(TOKEN))
        out_specs.append(pl.BlockSpec(memory_space=pltpu.VMEM))
    res = pl.pallas_call(
        body, name=name, in_specs=in_specs, out_specs=out_specs, out_shape=out_shape,
        input_output_aliases={i: k + i for i in range(n)},
        compiler_params=pltpu.CompilerParams(has_side_effects=pltpu.SideEffectType.DATAFLOW_SIDE_EFFECTING))(*ins)
    return list(res[:k]), list(res[k:k + n]), (res[k + n] if token else None)


def _remote(src, dst, send, recv, to):
    return pltpu.make_async_remote_copy(src_ref=src, dst_ref=dst, send_sem=send, recv_sem=recv, device_id=to,
                                        device_id_type=MESH)


def _gather_start(name, bufs):
    n = len(bufs)

    def body(*refs):
        ins, (send, recv), token = refs[:n], refs[n:n + 2], refs[-1]
        x, y, c, chips = _place()
        for i in range(n):
            mine = ins[i].at[2 * x + y, c]
            for k, chip in enumerate(chips):
                _remote(mine, mine, send.at[3 * i + k], recv.at[3 * i + k], (*chip, c)).start()
        token[...] = jnp.zeros(TOKEN, F32)

    return _comm_call(name, body, bufs, sems_out=(3 * n, 3 * n), token=True)


def _gather_forward(name, bufs, sems, after):
    n = len(bufs)
    o = n + 2 + (after is not None)

    def body(*refs):
        ins, (send, recv), (send2, recv2), token = refs[:n], refs[n:n + 2], refs[o:o + 2], refs[-1]
        x, y, c, chips = _place()
        for i in range(n):
            mine = ins[i].at[2 * x + y, c]
            for k, chip in enumerate(chips):
                land = ins[i].at[2 * chip[0] + chip[1], c]
                first = _remote(mine, land, send.at[3 * i + k], recv.at[3 * i + k], (*chip, c))
                first.wait_send()
                first.wait_recv()
                _remote(land, land, send2.at[3 * i + k], recv2.at[3 * i + k], (x, y, 1 - c)).start()
        token[...] = jnp.zeros(TOKEN, F32)

    return _comm_call(name, body, bufs, sems_in=sems, after=after, sems_out=(3 * n, 3 * n), token=True)


def _gather_finish(name, bufs, sems, after):
    n = len(bufs)

    def body(*refs):
        ins, (send, recv) = refs[:n], refs[n:n + 2]
        x, y, c, chips = _place()
        for i in range(n):
            for k, chip in enumerate(chips):
                idx = 2 * chip[0] + chip[1]
                cp = _remote(ins[i].at[idx, c], ins[i].at[idx, 1 - c], send.at[3 * i + k], recv.at[3 * i + k], (x, y, 1 - c))
                cp.wait_send()
                cp.wait_recv()

    return _comm_call(name, body, bufs, sems_in=sems, after=after)[1]


def _swap_start(name, grads):
    n = len(grads)
    lands = [lax.empty((g.shape[0],) + g.shape[2:], g.dtype) for g in grads]

    def body(*refs):
        ins, lnd, (send, recv), token = refs[:n], refs[n:2 * n], refs[2 * n:2 * n + 2], refs[-1]
        x, y, c, _ = _place()
        for i in range(n):
            _remote(ins[i].at[:, 1 - c], lnd[i], send.at[i], recv.at[i], (x, y, 1 - c)).start()
        token[...] = jnp.zeros(TOKEN, F32)

    return _comm_call(name, body, list(grads) + lands, sems_out=(n, n), token=True)


def _swap_wait(name, hbm, sems, after):
    n = len(hbm) // 2

    def body(*refs):
        ins, lnd, (send, recv) = refs[:n], refs[n:2 * n], refs[2 * n:2 * n + 2]
        x, y, c, _ = _place()
        for i in range(n):
            cp = _remote(ins[i].at[:, 1 - c], lnd[i], send.at[i], recv.at[i], (x, y, 1 - c))
            cp.wait_send()
            cp.wait_recv()

    out = _comm_call(name, body, hbm, sems_in=sems, after=after)[1]
    return out[:n], out[n:]


def _pair_add(name, g, got, cm_idx):
    nk, _, r, c = g.shape
    tr = _tile(r, max(2 * SUBLANES, COPY_BLOCK_BYTES // (4 * c)), 2 * SUBLANES)

    def body(cm_ref, g_ref, x_ref, o_ref, land_ref):
        s = (g_ref[...] + x_ref[...]).astype(o_ref.dtype)
        o_ref[...] = s

        @pl.when(pl.program_id(1) == cm_ref[1])
        def _():
            land_ref[...] = s

    grid_spec = pltpu.PrefetchScalarGridSpec(
        num_scalar_prefetch=1, grid=(r // tr, nk),
        in_specs=[pl.BlockSpec((None, None, tr, c), lambda i, k, cm: (k, cm[0], i, 0)),
                  pl.BlockSpec((None, tr, c), lambda i, k, cm: (k, i, 0))],
        out_specs=[pl.BlockSpec((None, tr, c), lambda i, k, cm: (k, i, 0)),
                   pl.BlockSpec((None, tr, c), lambda i, k, cm: (cm[1], i, 0))])
    return pl.pallas_call(body, name=name, grid_spec=grid_spec, out_shape=[_sds((nk, r, c), BF16)] * 2,
                          compiler_params=_cparams(("parallel", "arbitrary")))(cm_idx, g, got)


def _scatter_start(name, parts, lands):
    n = len(parts)

    def body(*refs):
        ins, lnd, (send, recv), token = refs[:n], refs[n:2 * n], refs[2 * n:2 * n + 2], refs[-1]
        x, y, c, chips = _place()
        for i in range(n):
            for k, chip in enumerate(chips):
                _remote(ins[i].at[2 * chip[0] + chip[1]], lnd[i].at[2 * x + y], send.at[3 * i + k], recv.at[3 * i + k],
                        (*chip, c)).start()
        token[...] = jnp.zeros(TOKEN, F32)

    return _comm_call(name, body, list(parts) + list(lands), sems_out=(3 * n, 3 * n), token=True)


def _scatter_wait(name, hbm, sems, after):
    n = len(hbm) // 2

    def body(*refs):
        ins, lnd, (send, recv) = refs[:n], refs[n:2 * n], refs[2 * n:2 * n + 2]
        x, y, c, chips = _place()
        for i in range(n):
            for k, chip in enumerate(chips):
                idx = 2 * chip[0] + chip[1]
                cp = _remote(ins[i].at[idx], lnd[i].at[idx], send.at[3 * i + k], recv.at[3 * i + k], (*chip, c))
                cp.wait_send()
                cp.wait_recv()

    out = _comm_call(name, body, hbm, sems_in=sems, after=after)[1]
    return out[:n], out[n:]


def _sum_leading(name, x, out_dtype=F32):
    nk, r, c = x.shape
    tc = _tile(c, ELEMENTWISE_COLS)
    tr = _tile(r, max(2 * SUBLANES, COPY_BLOCK_BYTES // (nk * tc * x.dtype.itemsize)), 2 * SUBLANES)

    def body(x_ref, o_ref):
        acc = x_ref[0].astype(F32)
        for k in range(1, nk):
            acc = acc + x_ref[k].astype(F32)
        o_ref[...] = acc.astype(o_ref.dtype)

    return pl.pallas_call(body, name=name, grid=(r // tr, c // tc),
                          in_specs=[pl.BlockSpec((nk, tr, tc), lambda i, j: (0, i, j))],
                          out_specs=pl.BlockSpec((tr, tc), lambda i, j: (i, j)), out_shape=_sds((r, c), out_dtype),
                          compiler_params=_cparams(("parallel", "parallel")))(x)


def _exchange_start(name, halves):
    n = len(halves)
    lands = [lax.empty(h.shape, h.dtype) for h in halves]

    def body(*refs):
        ins, lnd, (send, recv), token = refs[:n], refs[n:2 * n], refs[2 * n:2 * n + 2], refs[-1]
        x, y, c, _ = _place()
        for i in range(n):
            _remote(ins[i], lnd[i], send.at[i], recv.at[i], (x, y, 1 - c)).start()
        token[...] = jnp.zeros(TOKEN, F32)

    return _comm_call(name, body, list(halves) + lands, sems_out=(n, n), token=True)


def _exchange_wait(name, hbm, sems, after):
    n = len(hbm) // 2

    def body(*refs):
        ins, lnd, (send, recv) = refs[:n], refs[n:2 * n], refs[2 * n:2 * n + 2]
        x, y, c, _ = _place()
        for i in range(n):
            cp = _remote(ins[i], lnd[i], send.at[i], recv.at[i], (x, y, 1 - c))
            cp.wait_send()
            cp.wait_recv()

    out = _comm_call(name, body, hbm, sems_in=sems, after=after)[1]
    return out[:n], out[n:]


def _gather_all(part):
    def body(x_ref, out_ref, send_sems, recv_sems, local_sem):
        x, y, c, chips = _place()
        me, sibling = (x, y, c), (x, y, 1 - c)

        def rows(px, py, pc):
            return out_ref.at[4 * px + 2 * py + pc]

        def copy(k, block, to, src=None):
            return pltpu.make_async_remote_copy(src_ref=rows(*block) if src is None else src, dst_ref=rows(*block),
                                                send_sem=send_sems.at[k], recv_sem=recv_sems.at[k], device_id=to,
                                                device_id_type=MESH)

        mine = pltpu.make_async_copy(x_ref, rows(*me), local_sem)
        mine.start()
        first = [copy(0, me, sibling, src=x_ref)]
        first += [copy(1 + j, me, (*chip, c), src=x_ref) for j, chip in enumerate(chips)]
        for cp in first:
            cp.start()
        passed = [copy(4 + j, (*chip, c), sibling) for j, chip in enumerate(chips)]
        for j, chip in enumerate(chips):
            copy(1 + j, (*chip, c), me).wait_recv()
            passed[j].start()
        copy(0, sibling, me).wait_recv()
        for j, chip in enumerate(chips):
            copy(4 + j, (*chip, 1 - c), me).wait_recv()
        for cp in first + passed:
            cp.wait_send()
        mine.wait()

    return pl.pallas_call(
        body, name="gather_small_grads", in_specs=[ANY], out_specs=ANY, out_shape=_sds((N_DEV,) + part.shape, part.dtype),
        scratch_shapes=[pltpu.SemaphoreType.DMA((7,)), pltpu.SemaphoreType.DMA((7,)), pltpu.SemaphoreType.DMA])(part)


SMALL = ("attn_sinks", "pool_w", "pool_scale", "ssm_lam_re", "ssm_lam_im", "ssm_log_dt", "ssm_b_re", "ssm_b_im",
         "ssm_c_re", "ssm_c_im", "ssm_d", "ln1_g", "ln1_b", "ffn_conv_b", "ln2_g", "ln2_b")
BIG = ("w_in", "ssm_glu_w", "w_out", "ffn_w_up", "ffn_conv_w", "ffn_w_down")
ALL_W = ("w_in", "attn_sinks", "pool_w", "pool_scale", "ssm_lam_re", "ssm_lam_im", "ssm_log_dt", "ssm_b_re", "ssm_b_im",
         "ssm_c_re", "ssm_c_im", "ssm_d", "ssm_glu_w", "w_out", "ln1_g", "ln1_b", "ffn_w_up", "ffn_conv_w", "ffn_conv_b",
         "ffn_w_down", "ln2_g", "ln2_b")
PACK_UNIT = SUBLANES * LANES


def _padded(n):
    return -(-n // PACK_UNIT) * PACK_UNIT


def _pack(arrs):
    cols = []
    for name in SMALL:
        a = arrs[name].reshape(DEPTH, -1)
        cols.append(jnp.pad(a, ((0, 0), (0, _padded(a.shape[1]) - a.shape[1]))))
    return jnp.concatenate(cols, axis=1).reshape(-1, LANES)


def _unpack(packed, shapes):
    flat = packed.reshape(DEPTH, -1)
    out, off = {}, 0
    for name in SMALL:
        n = math.prod(shapes[name][1:])
        out[name] = flat[:, off:off + n].reshape(shapes[name])
        off += _padded(n)
    return out


def _b_rows(b):
    return b.transpose(2, 0, 1).reshape(SSM_GROUP, SSM_CH)


def _b_unrows(b):
    return b.reshape(SSM_GROUP, SSM_N_GROUPS, SSM_STATE).transpose(1, 2, 0)


def _block_diag_in(bb):
    eye = jnp.eye(SSM_N_GROUPS, dtype=F32)
    b3 = bb.reshape(SSM_GROUP, SSM_N_GROUPS, SSM_STATE)
    return jnp.einsum("hgp,gk->ghkp", b3, eye).reshape(SSM_WIDTH, SSM_CH)


def _c_unrows(c):
    return c.reshape(SSM_GROUP, SSM_N_GROUPS, SSM_STATE).transpose(1, 0, 2)


def _block_diag_out(cc):
    eye = jnp.eye(SSM_N_GROUPS, dtype=F32)
    return jnp.einsum("ghp,gk->gpkh", cc, eye).reshape(SSM_CH, SSM_WIDTH)


def _rows_layout(re, im):
    n = re.shape[1]
    return jnp.stack([re.reshape(SCAN_NB, SCAN_CW, n), im.reshape(SCAN_NB, SCAN_CW, n)], axis=1).reshape(2 * SSM_CH, n)


H_POOL0 = ATTN_WIDTH + 2 * KV_WIDTH
H_SSM0 = H_POOL0 + POOL_WIDTH


def _ssm_params(p):
    lr = p["ssm_lam_re"].reshape(1, SSM_CH)
    li = p["ssm_lam_im"].reshape(1, SSM_CH)
    ldt = jnp.repeat(p["ssm_log_dt"], SSM_STATE).reshape(1, SSM_CH)
    return lr, li, ldt, _b_rows(p["ssm_b_re"]), _b_rows(p["ssm_b_im"])


def _layer_fwd(x, xb, p, wg, rope_t, dep, mid):
    cos_t, sin_t = rope_t
    h = _mm_shard_cols("in_proj", xb, wg["w_in"], dep=dep)
    qk = _rope("rope_fwd", h, 0, Q_TILES + KV_TILES, cos_t, sin_t, MM_DTYPE)
    y_attn, y_attn_b = _attn_fwd(qk, h, p["attn_sinks"])
    y_pool = _pool_fwd(h, p["pool_w"], p["pool_scale"].reshape(1, POOL_WIDTH))
    ssm_in = _ssm_params(p)
    ar, ai, bbr, bbi = _ssm_prep(*ssm_in)
    bd = _scan_layout(_block_diag_in(bbr), _block_diag_in(bbi)).astype(MM_DTYPE)
    cc = _rows_layout(_block_diag_out(p["ssm_c_re"]), -_block_diag_out(p["ssm_c_im"])).astype(MM_DTYPE)
    dvec = p["ssm_d"].reshape(1, SSM_WIDTH)
    up = _time_permute(h[:, H_SSM0:])
    xx = _mm_nn("ssm_bu", up, bd, tn=1024)
    ss = _ssm_scan("ssm_scan_fwd", _scan_layout(ar, ai), xx, False)
    yp = _mm_nn("ssm_cs", ss, cc, tk=1024)
    yf, gy = _ssm_gelu(yp, up, dvec)
    ab = _mm_shard_cols("ssm_glu_proj", gy, wg["ssm_glu_w"])
    y_ssm = _time_unpermute(_ssm_glu(ab))
    mix = jnp.concatenate([y_attn_b, y_pool, y_ssm], axis=1)
    mixo = _mm_nn("out_proj", mix, wg["w_out"].reshape(MIX_WIDTH, D_MODEL))
    r1, x1, x1b = _ln_fwd("ln1_fwd", x, mixo, p["ln1_g"].reshape(1, D_MODEL), p["ln1_b"].reshape(1, D_MODEL))
    tokens = mid(x1b)
    hf = _ffn_up(x1b, wg["ffn_w_up"], dep=tokens)
    conv_b = p["ffn_conv_b"].reshape(N_CHIPS, 1, FS)
    act = _ffn_act(hf, wg["ffn_conv_w"], conv_b)
    f = _ffn_down(act, wg["ffn_w_down"].reshape(2, FS, D_MODEL))
    r2, x2, x2b = _ln_fwd("ln2_fwd", x1, f, p["ln2_g"].reshape(1, D_MODEL), p["ln2_b"].reshape(1, D_MODEL))
    saved = dict(xb=xb, h=h, qk=qk, y_attn=y_attn, ssm_in=ssm_in, ar=ar, ai=ai, bd=bd, cc=cc, dvec=dvec, up=up, ss=ss, yf=yf,
                 gy=gy, ab=ab, mix=mix, r1=r1, x1b=x1b, hf=hf, conv_b=conv_b, act=act, r2=r2)
    return x2, x2b, saved


def _layer_bwd(da, db, p, wg, sv, rope_t, run, start):
    cos_t, sin_t = rope_t
    small = {}
    dr2, dr2b, dg, dbeta = _ln_bwd("ln2_bwd" if db is not None else "ln2_bwd_last", sv["r2"], p["ln2_g"].reshape(1, D_MODEL),
                                   p["ln2_b"].reshape(1, D_MODEL), da, db, dep=run("h0", None))
    small["ln2_g"], small["ln2_b"] = dg, dbeta
    w_down = wg["ffn_w_down"].reshape(2, FS, D_MODEL)
    dact = _ffn_down_dact(dr2b, w_down)
    dw_down = _ffn_down_dw(sv["act"], dr2b)
    dhc, dcw, dcb = _ffn_act_bwd(sv["hf"], wg["ffn_conv_w"], sv["conv_b"], dact, dep=run("h1", dw_down))
    dconv_w = dcw.transpose(1, 0, 2, 3).reshape(N_CHIPS, CONV_WIDTH, FS)
    small["ffn_conv_b"] = dcb.transpose(1, 0, 2, 3)
    dh_ffn = _conv_bwd_input(dhc, wg["ffn_conv_w"])
    dx1_ffn = _ffn_up_dx(dh_ffn, wg["ffn_w_up"])
    dw_up = _ffn_up_dw(sv["x1b"], dh_ffn)
    tok = [start("ffn", {"ffn_w_up": dw_up, "ffn_conv_w": dconv_w,
                         "ffn_w_down": dw_down.reshape(N_CHIPS, FS // 2, D_MODEL)})] + run("h2", dw_up)
    dr1, dr1b, dg, dbeta = _ln_bwd("ln1_bwd", sv["r1"], p["ln1_g"].reshape(1, D_MODEL), p["ln1_b"].reshape(1, D_MODEL), dr2,
                                   dx1_ffn, dep=tok)
    small["ln1_g"], small["ln1_b"] = dg, dbeta
    w_out = wg["w_out"].reshape(MIX_WIDTH, D_MODEL)
    dmix = _mm_nt("out_proj_dx", dr1b, w_out)
    dw_out = _mm_tn("out_proj_dw", sv["mix"], dr1b)
    dq, dkc, dkp, dvc, dvp, dsk = _attn_bwd(sv["qk"], sv["h"], p["attn_sinks"], sv["y_attn"], dmix, 0, dep=run("h3", dw_out))
    small["attn_sinks"] = dsk[:, :, 0]
    dh_attn = _attn_dh(dq, dkc, dkp, dvc, dvp, cos_t, -sin_t)
    dh_pool, dpw, dps = _pool_bwd(sv["h"], p["pool_w"], p["pool_scale"].reshape(1, POOL_WIDTH), dmix, ATTN_WIDTH // POOL_WIDTH)
    small["pool_w"], small["pool_scale"] = dpw, dps
    dout_p = _time_permute(dmix[:, ATTN_WIDTH + POOL_WIDTH:])
    dab = _ssm_glu_bwd(sv["ab"], dout_p)
    dgy = _mm_shard_cols_nt("ssm_glu_dx", dab, wg["ssm_glu_w"])
    dw_glu = _mm_shard_cols_tn("ssm_glu_dw", sv["gy"], dab, N_CHIPS)
    dyf, du1, dd = _ssm_gelu_bwd(sv["yf"], dgy, sv["up"], sv["dvec"])
    small["ssm_d"] = dd
    dss = _mm_nt("ssm_cs_dx", dyf, sv["cc"], tn=1024)
    dcre, dcim = _scan_unlayout(_ssm_diag("ssm_c_diag", _mm_tn("ssm_cs_dw", dyf, sv["ss"], tn=1024)))
    small["ssm_c_re"], small["ssm_c_im"] = _c_unrows(dcre), -_c_unrows(dcim)
    gg, da8 = _ssm_scan("ssm_scan_bwd", _scan_layout(sv["ar"], -sv["ai"]), dss, True, sv["ss"])
    du2 = _mm_nt("ssm_bu_dx", gg, sv["bd"], tk=1024)
    dbbr, dbbi = _scan_unlayout(_ssm_diag("ssm_b_diag", _mm_tn("ssm_bu_dw", sv["up"], gg, tn=1024)))
    dar8, dai8 = _scan_unlayout(da8)
    dlr, dli, dldt, dbr, dbi = _ssm_prep_bwd(*sv["ssm_in"], dar8, dai8, dbbr, dbbi)
    small["ssm_lam_re"], small["ssm_lam_im"] = dlr, dli
    small["ssm_log_dt"] = dldt.reshape(SSM_N_GROUPS, SSM_STATE).sum(axis=1)
    small["ssm_b_re"], small["ssm_b_im"] = _b_unrows(dbr), _b_unrows(dbi)
    dh_ssm = _time_unpermute(_add2("ssm_du", du1, du2, MM_DTYPE))
    dh = jnp.concatenate([dh_attn, dh_pool, dh_ssm], axis=1)
    dx_in = _mm_shard_cols_nt("in_proj_dx", dh, wg["w_in"], dep=run("h4", dh))
    dw_in = _mm_shard_cols_tn("in_proj_dw", sv["xb"], dh, N_CHIPS)
    start("mix", {"w_in": dw_in, "ssm_glu_w": dw_glu, "w_out": dw_out.reshape(N_CHIPS, MIX_WIDTH // N_CHIPS, D_MODEL)})
    return dr1, dx_in, small


CONV_PAD = 2 * SUBLANES


def _halved(name, a):
    if name == "ffn_conv_w":
        a = jnp.pad(a, ((0, 0), (0, CONV_PAD - CONV_WIDTH), (0, 0)))
    return a.reshape(a.shape[0], 2, a.shape[1] // 2, a.shape[2])


def _unhalved(name, a):
    a = a.reshape(a.shape[:-3] + (2 * a.shape[-2], a.shape[-1]))
    return a[..., :CONV_WIDTH, :] if name == "ffn_conv_w" else a


class _Reduce:
    def __init__(self, tag, grads, cm_idx):
        self.tag, self.cm_idx, self.names = tag, cm_idx, tuple(grads)
        g4 = [_halved(name, grads[name]) for name in self.names]
        self.sems, self.hbm, self.token = _swap_start("grad_swap_start_" + tag, g4)

    def swapped(self, after):
        g4, got = _swap_wait("grad_swap_wait_" + self.tag, self.hbm, self.sems, after)
        parts, lands = zip(*[_pair_add("grad_pair_add", g, x, self.cm_idx) for g, x in zip(g4, got)])
        self.sems, self.hbm, self.token = _scatter_start("grad_scatter_start_" + self.tag, parts, lands)
        return self.token

    def scattered(self, after):
        _, recv = _scatter_wait("grad_scatter_wait_" + self.tag, self.hbm, self.sems, after)
        halves = [_sum_leading("grad_chip_sum", r) for r in recv]
        self.sems, self.hbm, self.token = _exchange_start("grad_exchange_start_" + self.tag, halves)
        return self.token

    def finish(self, after):
        return _exchange_wait("grad_exchange_wait_" + self.tag, self.hbm, self.sems, after)


def kernel(x, w_in, attn_sinks, pool_w, pool_scale, ssm_lam_re, ssm_lam_im, ssm_log_dt, ssm_b_re, ssm_b_im, ssm_c_re, ssm_c_im, ssm_d, ssm_glu_w, w_out, ln1_g, ln1_b, ffn_w_up, ffn_conv_w, ffn_conv_b, ffn_w_down, ln2_g, ln2_b, loss_target, m_w_in, m_attn_sinks, m_pool_w, m_pool_scale, m_ssm_lam_re, m_ssm_lam_im, m_ssm_log_dt, m_ssm_b_re, m_ssm_b_im, m_ssm_c_re, m_ssm_c_im, m_ssm_d, m_ssm_glu_w, m_w_out, m_ln1_g, m_ln1_b, m_ffn_w_up, m_ffn_conv_w, m_ffn_conv_b, m_ffn_w_down, m_ln2_g, m_ln2_b, v_w_in, v_attn_sinks, v_pool_w, v_pool_scale, v_ssm_lam_re, v_ssm_lam_im, v_ssm_log_dt, v_ssm_b_re, v_ssm_b_im, v_ssm_c_re, v_ssm_c_im, v_ssm_d, v_ssm_glu_w, v_w_out, v_ln1_g, v_ln1_b, v_ffn_w_up, v_ffn_conv_w, v_ffn_conv_b, v_ffn_w_down, v_ln2_g, v_ln2_b):
    w = dict(w_in=w_in, attn_sinks=attn_sinks, pool_w=pool_w, pool_scale=pool_scale, ssm_lam_re=ssm_lam_re,
             ssm_lam_im=ssm_lam_im, ssm_log_dt=ssm_log_dt, ssm_b_re=ssm_b_re, ssm_b_im=ssm_b_im, ssm_c_re=ssm_c_re,
             ssm_c_im=ssm_c_im, ssm_d=ssm_d, ssm_glu_w=ssm_glu_w, w_out=w_out, ln1_g=ln1_g, ln1_b=ln1_b, ffn_w_up=ffn_w_up,
             ffn_conv_w=ffn_conv_w, ffn_conv_b=ffn_conv_b, ffn_w_down=ffn_w_down, ln2_g=ln2_g, ln2_b=ln2_b)
    m = dict(w_in=m_w_in, attn_sinks=m_attn_sinks, pool_w=m_pool_w, pool_scale=m_pool_scale, ssm_lam_re=m_ssm_lam_re,
             ssm_lam_im=m_ssm_lam_im, ssm_log_dt=m_ssm_log_dt, ssm_b_re=m_ssm_b_re, ssm_b_im=m_ssm_b_im, ssm_c_re=m_ssm_c_re,
             ssm_c_im=m_ssm_c_im, ssm_d=m_ssm_d, ssm_glu_w=m_ssm_glu_w, w_out=m_w_out, ln1_g=m_ln1_g, ln1_b=m_ln1_b,
             ffn_w_up=m_ffn_w_up, ffn_conv_w=m_ffn_conv_w, ffn_conv_b=m_ffn_conv_b, ffn_w_down=m_ffn_w_down, ln2_g=m_ln2_g,
             ln2_b=m_ln2_b)
    v = dict(w_in=v_w_in, attn_sinks=v_attn_sinks, pool_w=v_pool_w, pool_scale=v_pool_scale, ssm_lam_re=v_ssm_lam_re,
             ssm_lam_im=v_ssm_lam_im, ssm_log_dt=v_ssm_log_dt, ssm_b_re=v_ssm_b_re, ssm_b_im=v_ssm_b_im, ssm_c_re=v_ssm_c_re,
             ssm_c_im=v_ssm_c_im, ssm_d=v_ssm_d, ssm_glu_w=v_ssm_glu_w, w_out=v_w_out, ln1_g=v_ln1_g, ln1_b=v_ln1_b,
             ffn_w_up=v_ffn_w_up, ffn_conv_w=v_ffn_conv_w, ffn_conv_b=v_ffn_conv_b, ffn_w_down=v_ffn_w_down, ln2_g=v_ln2_g,
             ln2_b=v_ln2_b)
    c_pos = lax.axis_index("c").astype(jnp.int32)
    chip = (2 * lax.axis_index("x") + lax.axis_index("y")).astype(jnp.int32)
    c_idx, chip_idx, cm_idx = c_pos.reshape(1), chip.reshape(1), jnp.stack([c_pos, chip])
    rope_t = _rope_tables()
    xs = x.reshape(SEQ, D_MODEL)
    xb = xs.astype(MM_DTYPE)
    for t in (w, m, v):
        t["ffn_w_up"] = jnp.swapaxes(t["ffn_w_up"], 1, 2)
    wh, mh, vh = ({n: _halved(n, t[n]) for n in BIG} for t in (w, m, v))

    def place(l):
        return [_cast_place("place_" + n, wh[n], l, chip_idx, F32 if n == "ffn_conv_w" else MM_DTYPE) for n in BIG]

    n_mix = BIG.index("ffn_w_up")
    first = place(0)
    sems_a, bufs_a, _ = _gather_start("gather_start_0_mix", first[:n_mix])
    sems_b, bufs_b, _ = _gather_start("gather_start_0_ffn", first[n_mix:])
    sems_a, bufs_a, _ = _gather_forward("gather_forward_0_mix", bufs_a, sems_a, None)
    bufs_a = _gather_finish("gather_finish_0_mix", bufs_a, sems_a, None)
    gathered, saved = [{n: _unhalved(n, g) for n, g in zip(BIG, bufs_a)}], []
    for l in range(DEPTH):
        nxt, dep = {}, None
        if l + 1 < DEPTH:
            nxt["sems"], nxt["bufs"], dep = _gather_start("gather_start_%d" % (l + 1), place(l + 1))

        def mid(after):
            tokens = []
            if l == 0:
                sems, bufs, _ = _gather_forward("gather_forward_0_ffn", bufs_b, sems_b, after)
                bufs = _gather_finish("gather_finish_0_ffn", bufs, sems, after)
                gathered[0].update({n: _unhalved(n, g) for n, g in zip(BIG[n_mix:], bufs)})
            if l + 1 < DEPTH:
                nxt["sems"], nxt["bufs"], token = _gather_forward("gather_forward_%d" % (l + 1), nxt["bufs"], nxt["sems"], after)
                tokens.append(token)
            return tokens

        xs, xb, sv = _layer_fwd(xs, xb, {n: w[n][l] for n in SMALL}, gathered[l], rope_t, dep, mid)
        saved.append(sv)
        if l + 1 < DEPTH:
            bufs = _gather_finish("gather_finish_%d" % (l + 1), nxt["bufs"], nxt["sems"], xb)
            gathered.append({n: _unhalved(n, g) for n, g in zip(BIG, bufs)})
    dy, loss_tile = _loss(xs, loss_target.reshape(SEQ, D_MODEL))
    loss = lax.psum(loss_tile[0, 0], ("x", "y", "c"))

    big_out = {n: None for n in BIG}
    small_g = {n: [None] * DEPTH for n in SMALL}
    agenda = {}
    tail = []
    plan = {"ffn": (("h3", 0), ("end", 0), ("h1", -1)), "mix": (("h1", -1), ("h3", -1), ("h4", -1))}
    tail_rank = {("mix", 0): 0, ("ffn", 2): 1, ("mix", 1): 2, ("mix", 2): 3}
    started = []

    def book(l, group, red):
        def update(after):
            names, own, got = red.names, *red.finish(after)
            for n, o, g in zip(names, own, got):
                big_out[n] = _adamw_big("adamw_" + n, l, c_idx, wh[n], mh[n], vh[n], o, g, big_out[n])
            return [big_out[names[-1]][0]] if l == 0 else []

        steps = (lambda a: [red.swapped(a)], lambda a: [red.scattered(a)], update)
        for k, ((hook, dl), step) in enumerate(zip(plan[group], steps)):
            if l + dl >= 0:
                agenda.setdefault((l + dl, hook), []).append(step)
            else:
                tail.append((tail_rank[group, k], step))

    def run_at(l):
        return lambda hook, after: [t for step in agenda.pop((l, hook), []) for t in step(after)]

    def start_at(l):
        def start(group, grads):
            red = _Reduce("%s_%d" % (group, l), grads, cm_idx)
            book(l, group, red)
            started.append(red.token)
            return red.token
        return start

    da, db, carry = dy, None, []
    for l in reversed(range(DEPTH)):
        agenda.setdefault((l, "h0"), []).append(lambda after, carry=carry: carry)
        da, db, small = _layer_bwd(da, db, {n: w[n][l] for n in SMALL}, gathered[l], saved[l], rope_t, run_at(l), start_at(l))
        for n in SMALL:
            small_g[n][l] = small[n].reshape(w[n].shape[1:])
        carry = run_at(l)("end", db) + started[-1:]
    after = None
    for _, step in sorted(tail, key=lambda rs: rs[0]):
        after = (step(after) or [after])[-1]
    grad_x = _ln_in_grad(da, db).reshape(x.shape)

    shapes = {n: w[n].shape for n in SMALL}
    part = _pack({n: jnp.stack(small_g[n]) for n in SMALL})
    g_small = _sum_leading("small_grad_sum", _gather_all(part))
    upd = _adamw("adamw_small", _pack(w), g_small, _pack(m), _pack(v))
    small_out = [_unpack(a, shapes) for a in (g_small,) + tuple(upd)]

    outs = [loss, grad_x]
    for kind in range(4):
        for n in ALL_W:
            if n in SMALL:
                outs.append(small_out[kind][n])
            else:
                o = _unhalved(n, big_out[n][kind])
                outs.append(jnp.swapaxes(o, 1, 2) if n == "ffn_w_up" else o)
    return tuple(outs)


def _ln_in_grad(dr1, dx_in):
    tm = _tile(SEQ, 512)

    def body(a_ref, b_ref, o_ref):
        o_ref[...] = DEEPNORM_ALPHA * a_ref[...] + b_ref[...]

    blk = pl.BlockSpec((tm, D_MODEL), lambda i: (i, 0))
    return pl.pallas_call(body, name="grad_x", grid=(SEQ // tm,), in_specs=[blk, blk], out_specs=blk,
                          out_shape=_sds((SEQ, D_MODEL)), compiler_params=_cparams(("parallel",)))(dr1, dx_in)
```

```python
import functools
import math

import jax
import jax.numpy as jnp
from jax import lax
from jax.experimental import pallas as pl
from jax.experimental.pallas import tpu as pltpu

F32 = jnp.float32
BF16 = jnp.bfloat16
MM_DTYPE = BF16

D_MODEL = 2048
SEQ = 2048
DEPTH = 4
D_FF = 5504
HEAD_DIM = 64
N_Q_HEADS = D_MODEL // 2 // HEAD_DIM
N_KV_HEADS = N_Q_HEADS // 4
ATTN_WIDTH = N_Q_HEADS * HEAD_DIM
KV_WIDTH = N_KV_HEADS * HEAD_DIM
ATTN_BLOCK = 128
ROPE_THETA = 10000.0
POOL_WINDOWS = (2, 4, 8, 16)
POOL_WIDTH = D_MODEL // 4
POOL_GROUP = POOL_WIDTH // len(POOL_WINDOWS)
SSM_WIDTH = D_MODEL // 4
SSM_GROUP = 16
SSM_N_GROUPS = SSM_WIDTH // SSM_GROUP
SSM_STATE = 64
SSM_CH = SSM_N_GROUPS * SSM_STATE
MIX_WIDTH = ATTN_WIDTH + POOL_WIDTH + SSM_WIDTH
IN_WIDTH = ATTN_WIDTH + 2 * KV_WIDTH + POOL_WIDTH + SSM_WIDTH
CONV_WIDTH = 3
LN_EPS = 1e-5
DEEPNORM_ALPHA = (2 * DEPTH) ** 0.25
ADAM_LR = 0.001
ADAM_B1 = 0.9
ADAM_B2 = 0.999
ADAM_EPS = 1e-08
ADAM_WD = 0.01
ADAM_STEP = 10

N_CHIPS = 4
N_DEV = 8
FS = 2 * D_FF // N_CHIPS
IN_S = IN_WIDTH // N_CHIPS
GLU_S = 2 * SSM_WIDTH // N_CHIPS
LANES = 128
SUBLANES = 8
SCAN_CW = 256
SCAN_NB = SSM_CH // SCAN_CW
VMEM_LIMIT = 56 * 1024 * 1024
COPY_BLOCK_BYTES = 6 * 1024 * 1024
NEG = -1e30

NN = (((1,), (0,)), ((), ()))
NT = (((1,), (1,)), ((), ()))
TN = (((0,), (0,)), ((), ()))
MESH = pl.DeviceIdType.MESH


def _tile(n, pref, mult=LANES):
    best = None
    for t in range(mult, min(n, pref) + 1, mult):
        if n % t == 0:
            best = t
    return n if best is None else best


def _cparams(sem):
    return pltpu.CompilerParams(dimension_semantics=sem, vmem_limit_bytes=VMEM_LIMIT)


def _sds(shape, dtype=F32):
    return jax.ShapeDtypeStruct(tuple(shape), dtype)


def _as_list(x):
    return [] if x is None else list(x) if isinstance(x, (list, tuple)) else [x]


def _mm(name, a, b, out_shape, grid, a_spec, b_spec, o_spec, dims, acc_shape, out_dtype=F32, dep=None):
    nk = grid[2]
    deps = _as_list(dep)

    def product(a_ref, b_ref):
        return lax.dot_general(a_ref[...].astype(MM_DTYPE), b_ref[...].astype(MM_DTYPE), dims, preferred_element_type=F32)

    def body_one(a_ref, b_ref, *rest):
        rest[-1][...] = product(a_ref, b_ref).astype(rest[-1].dtype)

    def body(a_ref, b_ref, *rest):
        o_ref, acc_ref = rest[-2:]
        k = pl.program_id(2)

        @pl.when(k == 0)
        def _():
            acc_ref[...] = product(a_ref, b_ref)

        @pl.when(k > 0)
        def _():
            acc_ref[...] += product(a_ref, b_ref)

        @pl.when(k == nk - 1)
        def _():
            o_ref[...] = acc_ref[...].astype(o_ref.dtype)

    return pl.pallas_call(
        body_one if nk == 1 else body, name=name, grid=grid, in_specs=[a_spec, b_spec] + [ANY] * len(deps),
        out_specs=o_spec, out_shape=_sds(out_shape, out_dtype),
        scratch_shapes=[] if nk == 1 else [pltpu.VMEM(acc_shape, F32)],
        compiler_params=_cparams(("parallel", "parallel", "arbitrary")))(a, b, *deps)


def _mm_nn(name, a, b, tm=2048, tn=512, tk=2048, out_dtype=F32):
    m, kk = a.shape
    n = b.shape[1]
    tm, tn, tk = _tile(m, tm), _tile(n, tn), _tile(kk, tk)
    return _mm(name, a, b, (m, n), (m // tm, n // tn, kk // tk),
               pl.BlockSpec((tm, tk), lambda i, j, k: (i, k)), pl.BlockSpec((tk, tn), lambda i, j, k: (k, j)),
               pl.BlockSpec((tm, tn), lambda i, j, k: (i, j)), NN, (tm, tn), out_dtype)


def _mm_nt(name, a, b, tm=2048, tn=512, tk=2048, dep=None):
    m, kk = a.shape
    n = b.shape[0]
    tm, tn, tk = _tile(m, tm), _tile(n, tn), _tile(kk, tk)
    return _mm(name, a, b, (m, n), (m // tm, n // tn, kk // tk),
               pl.BlockSpec((tm, tk), lambda i, j, k: (i, k)), pl.BlockSpec((tn, tk), lambda i, j, k: (j, k)),
               pl.BlockSpec((tm, tn), lambda i, j, k: (i, j)), NT, (tm, tn), dep=dep)


def _mm_tn(name, a, b, tm=1024, tn=1024, ts=2048):
    s, m = a.shape
    n = b.shape[1]
    tm, tn, ts = _tile(m, tm), _tile(n, tn), _tile(s, ts)
    return _mm(name, a, b, (m, n), (m // tm, n // tn, s // ts),
               pl.BlockSpec((ts, tm), lambda i, j, k: (k, i)), pl.BlockSpec((ts, tn), lambda i, j, k: (k, j)),
               pl.BlockSpec((tm, tn), lambda i, j, k: (i, j)), TN, (tm, tn))


def _mm_shard_cols(name, a, w, tm=2048, tk=2048, dep=None):
    m, kk = a.shape
    nj, _, c = w.shape
    tm, tk = _tile(m, tm), _tile(kk, tk)
    return _mm(name, a, w, (m, nj * c), (m // tm, nj, kk // tk),
               pl.BlockSpec((tm, tk), lambda i, j, k: (i, k)), pl.BlockSpec((None, tk, c), lambda i, j, k: (j, k, 0)),
               pl.BlockSpec((tm, c), lambda i, j, k: (i, j)), NN, (tm, c), dep=dep)


def _mm_shard_cols_nt(name, d, w, tm=2048, tn=512, dep=None):
    m = d.shape[0]
    nj, n, c = w.shape
    tm, tn = _tile(m, tm), _tile(n, tn)
    return _mm(name, d, w, (m, n), (m // tm, n // tn, nj),
               pl.BlockSpec((tm, c), lambda i, j, k: (i, k)), pl.BlockSpec((None, tn, c), lambda i, j, k: (k, j, 0)),
               pl.BlockSpec((tm, tn), lambda i, j, k: (i, j)), NT, (tm, tn), dep=dep)


def _mm_shard_cols_tn(name, a, d, nj, tm=1024, ts=2048):
    s, m = a.shape
    c = d.shape[1] // nj
    tm, ts = _tile(m, tm), _tile(s, ts)
    return _mm(name, a, d, (nj, m, c), (nj, m // tm, s // ts),
               pl.BlockSpec((ts, tm), lambda j, i, k: (k, i)), pl.BlockSpec((ts, c), lambda j, i, k: (k, j)),
               pl.BlockSpec((None, tm, c), lambda j, i, k: (j, i, 0)), TN, (tm, c))


def _ffn_up(x1, w_up_t, tm=512, tk=2048, dep=None):
    s, d = x1.shape
    tm, tk = _tile(s, tm), _tile(d, tk)
    return _mm("ffn_up", x1, w_up_t, (N_CHIPS, s, FS), (N_CHIPS, s // tm, d // tk),
               pl.BlockSpec((tm, tk), lambda j, i, k: (i, k)), pl.BlockSpec((None, FS, tk), lambda j, i, k: (j, 0, k)),
               pl.BlockSpec((None, tm, FS), lambda j, i, k: (j, i, 0)), NT, (tm, FS), dep=dep)


def _ffn_down(act, w_down, tm=1024, tn=512):
    _, s, _ = act.shape
    d = w_down.shape[2]
    tm, tn = _tile(s, tm), _tile(d, tn)
    return _mm("ffn_down", act, w_down, (s, d), (s // tm, d // tn, 2),
               pl.BlockSpec((None, tm, FS), lambda i, j, k: (k, i, 0)), pl.BlockSpec((None, FS, tn), lambda i, j, k: (k, 0, j)),
               pl.BlockSpec((tm, tn), lambda i, j, k: (i, j)), NN, (tm, tn))


def _ffn_down_dact(df, w_down, tm=512, tk=2048):
    s, d = df.shape
    tm, tk = _tile(s, tm), _tile(d, tk)
    return _mm("ffn_down_dact", df, w_down, (2, s, FS), (2, s // tm, d // tk),
               pl.BlockSpec((tm, tk), lambda j, i, k: (i, k)), pl.BlockSpec((None, FS, tk), lambda j, i, k: (j, 0, k)),
               pl.BlockSpec((None, tm, FS), lambda j, i, k: (j, i, 0)), NT, (tm, FS))


def _ffn_down_dw(act, df, tn=512, ts=2048):
    _, s, _ = act.shape
    d = df.shape[1]
    tn, ts = _tile(d, tn), _tile(s, ts)
    return _mm("ffn_down_dw", act, df, (2, FS, d), (2, d // tn, s // ts),
               pl.BlockSpec((None, ts, FS), lambda p, j, k: (p, k, 0)), pl.BlockSpec((ts, tn), lambda p, j, k: (k, j)),
               pl.BlockSpec((None, FS, tn), lambda p, j, k: (p, 0, j)), TN, (FS, tn))


def _ffn_up_dx(dh, w_up_t, tm=1024, tn=1024, dep=None):
    s = dh.shape[2]
    d = w_up_t.shape[2]
    tm, tn = _tile(s, tm), _tile(d, tn)
    return _mm("ffn_up_dx", dh, w_up_t, (s, d), (s // tm, d // tn, N_CHIPS),
               pl.BlockSpec((None, None, tm, FS), lambda i, j, k: (k % 2, k // 2, i, 0)),
               pl.BlockSpec((None, FS, tn), lambda i, j, k: (k, 0, j)),
               pl.BlockSpec((tm, tn), lambda i, j, k: (i, j)), NN, (tm, tn), dep=dep)


def _ffn_up_dw(x1, dh, tn=512, ts=2048):
    s, d = x1.shape
    tn, ts = _tile(d, tn), _tile(s, ts)
    return _mm("ffn_up_dw", dh, x1, (N_CHIPS, FS, d), (N_CHIPS, d // tn, s // ts),
               pl.BlockSpec((None, None, ts, FS), lambda j, i, k: (j % 2, j // 2, k, 0)),
               pl.BlockSpec((ts, tn), lambda j, i, k: (k, i)),
               pl.BlockSpec((None, FS, tn), lambda j, i, k: (j, 0, i)), TN, (FS, tn))


def _rope_tables():
    half = HEAD_DIM // 2
    inv = ROPE_THETA ** (-jnp.arange(half, dtype=F32) / half)
    ang = jnp.arange(SEQ).astype(F32)[:, None] * inv[None, :]
    cos, sin = jnp.cos(ang), jnp.sin(ang)
    cos_t = jnp.tile(cos, (1, LANES // half))
    sin_t = jnp.tile(jnp.concatenate([-sin, sin], axis=1), (1, LANES // HEAD_DIM))
    return cos_t, sin_t


def _rotate_half(t):
    lane = lax.broadcasted_iota(jnp.int32, t.shape, 1)
    first = (lane % HEAD_DIM) < (HEAD_DIM // 2)
    return jnp.where(first, pltpu.roll(t, LANES - HEAD_DIM // 2, 1), pltpu.roll(t, HEAD_DIM // 2, 1))


def _rope(name, src, col_tile0, n_tiles, cos_t, sin_t, out_dtype):
    tm = _tile(SEQ, 512)

    def body(x_ref, c_ref, s_ref, o_ref):
        t = x_ref[...].astype(F32)
        o_ref[...] = (t * c_ref[...] + _rotate_half(t) * s_ref[...]).astype(o_ref.dtype)

    return pl.pallas_call(
        body, name=name, grid=(SEQ // tm, n_tiles),
        in_specs=[pl.BlockSpec((tm, LANES), lambda i, j: (i, j + col_tile0)),
                  pl.BlockSpec((tm, LANES), lambda i, j: (i, 0)), pl.BlockSpec((tm, LANES), lambda i, j: (i, 0))],
        out_specs=pl.BlockSpec((tm, LANES), lambda i, j: (i, j)),
        out_shape=_sds((SEQ, n_tiles * LANES), out_dtype),
        compiler_params=_cparams(("parallel", "parallel")))(src, cos_t, sin_t)


Q_TILES = ATTN_WIDTH // LANES
KV_TILES = KV_WIDTH // LANES
Q_PER_KV_TILE = Q_TILES // KV_TILES
HEADS_PER_KV_TILE = N_Q_HEADS // KV_TILES
K_TILE0 = ATTN_WIDTH // LANES
V_TILE0 = (ATTN_WIDTH + KV_WIDTH) // LANES
N_QBLK = SEQ // ATTN_BLOCK


def _dup_half(t, which):
    lane = lax.broadcasted_iota(jnp.int32, t.shape, 1)
    r = pltpu.roll(t, HEAD_DIM, 1)
    lo = lane < HEAD_DIM
    return jnp.where(lo, t, r) if which == 0 else jnp.where(lo, r, t)


def _attn_masks(n):
    row = lax.broadcasted_iota(jnp.int32, (ATTN_BLOCK, ATTN_BLOCK), 0)
    col = lax.broadcasted_iota(jnp.int32, (ATTN_BLOCK, ATTN_BLOCK), 1)
    return col <= row, (col > row) & (n > 0), col < HEAD_DIM


def _attn_probs(qm, k2c, k2p, cur_ok, prev_ok, sink):
    scale = HEAD_DIM ** -0.5
    sc = lax.dot_general(qm, k2c, NT, preferred_element_type=F32) * scale
    sp = lax.dot_general(qm, k2p, NT, preferred_element_type=F32) * scale
    sc = jnp.where(cur_ok, sc, NEG)
    sp = jnp.where(prev_ok, sp, NEG)
    m = jnp.maximum(jnp.maximum(sc.max(1, keepdims=True), sp.max(1, keepdims=True)), sink)
    pc, pp = jnp.exp(sc - m), jnp.exp(sp - m)
    esink = jnp.exp(sink - m)
    inv = 1.0 / (pc.sum(1, keepdims=True) + pp.sum(1, keepdims=True) + esink)
    return pc * inv, pp * inv, esink * inv


def _attn_specs():
    blk = (ATTN_BLOCK, LANES)
    wide = (ATTN_BLOCK, Q_PER_KV_TILE * LANES)
    prev = lambda n: jnp.maximum(n - 1, 0)
    q_spec = pl.BlockSpec(wide, lambda t, n: (n, t))
    kc = pl.BlockSpec(blk, lambda t, n: (n, K_TILE0 + t))
    kp = pl.BlockSpec(blk, lambda t, n: (prev(n), K_TILE0 + t))
    vc = pl.BlockSpec(blk, lambda t, n: (n, V_TILE0 + t))
    vp = pl.BlockSpec(blk, lambda t, n: (prev(n), V_TILE0 + t))
    return q_spec, kc, kp, vc, vp, pl.BlockSpec(memory_space=pltpu.SMEM)


def _attn_fwd(qk, h, sinks):
    q_spec, kc_s, kp_s, vc_s, vp_s, smem = _attn_specs()

    def body(sink_ref, q_ref, kc_ref, kp_ref, vc_ref, vp_ref, o_ref, ob_ref):
        t, n = pl.program_id(0), pl.program_id(1)
        cur_ok, prev_ok, lo = _attn_masks(n)
        kc, kp = kc_ref[...].astype(F32), kp_ref[...].astype(F32)
        vc, vp = vc_ref[...], vp_ref[...]
        for kvl in range(2):
            k2c, k2p = _dup_half(kc, kvl).astype(MM_DTYPE), _dup_half(kp, kvl).astype(MM_DTYPE)
            v2c, v2p = _dup_half(vc, kvl).astype(MM_DTYPE), _dup_half(vp, kvl).astype(MM_DTYPE)
            for a in (2 * kvl, 2 * kvl + 1):
                qt = q_ref[:, a * LANES:(a + 1) * LANES].astype(F32)
                outs = []
                for hs in range(2):
                    qm = jnp.where(lo == (hs == 0), qt, 0.0).astype(MM_DTYPE)
                    sink = sink_ref[t * HEADS_PER_KV_TILE + 2 * a + hs]
                    pc, pp, _ = _attn_probs(qm, k2c, k2p, cur_ok, prev_ok, sink)
                    outs.append(lax.dot_general(pc.astype(MM_DTYPE), v2c, NN, preferred_element_type=F32)
                                + lax.dot_general(pp.astype(MM_DTYPE), v2p, NN, preferred_element_type=F32))
                o = jnp.where(lo, outs[0], outs[1])
                o_ref[:, a * LANES:(a + 1) * LANES] = o
                ob_ref[:, a * LANES:(a + 1) * LANES] = o.astype(ob_ref.dtype)

    return pl.pallas_call(
        body, name="attn_fwd", grid=(KV_TILES, N_QBLK),
        in_specs=[smem, q_spec, kc_s, kp_s, vc_s, vp_s], out_specs=[q_spec, q_spec],
        out_shape=[_sds((SEQ, ATTN_WIDTH)), _sds((SEQ, ATTN_WIDTH), MM_DTYPE)],
        compiler_params=_cparams(("parallel", "parallel")))(sinks, qk, qk, qk, h, h)


def _attn_bwd(qk, h, sinks, y, dy, dy_tile0, dep=None):
    deps = _as_list(dep)
    q_spec, kc_s, kp_s, vc_s, vp_s, smem = _attn_specs()
    blk = (ATTN_BLOCK, LANES)
    wide = (ATTN_BLOCK, Q_PER_KV_TILE * LANES)
    kv_out = pl.BlockSpec(blk, lambda t, n: (n, t))
    dy_spec = pl.BlockSpec(wide, lambda t, n: (n, t + dy_tile0))

    def body(sink_ref, q_ref, kc_ref, kp_ref, vc_ref, vp_ref, y_ref, dy_ref, *rest):
        dq_ref, dkc_ref, dkp_ref, dvc_ref, dvp_ref, dsk_ref = rest[-6:]
        t, n = pl.program_id(0), pl.program_id(1)
        cur_ok, prev_ok, lo = _attn_masks(n)
        scale = HEAD_DIM ** -0.5
        kc, kp = kc_ref[...].astype(F32), kp_ref[...].astype(F32)
        vc, vp = vc_ref[...], vp_ref[...]
        hrow = lax.broadcasted_iota(jnp.int32, (HEADS_PER_KV_TILE, LANES), 0)
        dsk = jnp.zeros((HEADS_PER_KV_TILE, LANES), F32)
        folded = []
        for kvl in range(2):
            k2c, k2p = _dup_half(kc, kvl).astype(MM_DTYPE), _dup_half(kp, kvl).astype(MM_DTYPE)
            v2c, v2p = _dup_half(vc, kvl).astype(MM_DTYPE), _dup_half(vp, kvl).astype(MM_DTYPE)
            acc = [jnp.zeros(blk, F32) for _ in range(4)]
            for a in (2 * kvl, 2 * kvl + 1):
                sl = slice(a * LANES, (a + 1) * LANES)
                qt = q_ref[:, sl].astype(F32)
                dot_, yt = dy_ref[:, sl], y_ref[:, sl]
                dqs = []
                for hs in range(2):
                    hm = lo == (hs == 0)
                    qm = jnp.where(hm, qt, 0.0).astype(MM_DTYPE)
                    dom = jnp.where(hm, dot_, 0.0).astype(MM_DTYPE)
                    hl = 2 * a + hs
                    sink = sink_ref[t * HEADS_PER_KV_TILE + hl]
                    pc, pp, psink = _attn_probs(qm, k2c, k2p, cur_ok, prev_ok, sink)
                    delta = jnp.sum(jnp.where(hm, dot_ * yt, 0.0), axis=1, keepdims=True)
                    dpc = lax.dot_general(dom, v2c, NT, preferred_element_type=F32)
                    dpp = lax.dot_general(dom, v2p, NT, preferred_element_type=F32)
                    dsc = (pc * (dpc - delta) * scale).astype(MM_DTYPE)
                    dsp = (pp * (dpp - delta) * scale).astype(MM_DTYPE)
                    dqs.append(lax.dot_general(dsc, k2c, NN, preferred_element_type=F32)
                               + lax.dot_general(dsp, k2p, NN, preferred_element_type=F32))
                    acc[0] += lax.dot_general(dsc, qm, TN, preferred_element_type=F32)
                    acc[1] += lax.dot_general(dsp, qm, TN, preferred_element_type=F32)
                    acc[2] += lax.dot_general(pc.astype(MM_DTYPE), dom, TN, preferred_element_type=F32)
                    acc[3] += lax.dot_general(pp.astype(MM_DTYPE), dom, TN, preferred_element_type=F32)
                    dsk = dsk + jnp.where(hrow == hl, -jnp.sum(psink * delta), 0.0)
                dq_ref[:, sl] = jnp.where(lo, dqs[0], dqs[1])
            folded.append([x + pltpu.roll(x, HEAD_DIM, 1) for x in acc])
        for o_ref, i in ((dkc_ref, 0), (dkp_ref, 1), (dvc_ref, 2), (dvp_ref, 3)):
            o_ref[...] = jnp.where(lo, folded[0][i], folded[1][i])

        @pl.when(n == 0)
        def _():
            dsk_ref[...] = jnp.zeros_like(dsk_ref)

        dsk_ref[...] += dsk

    kv_shape = _sds((SEQ, KV_WIDTH))
    return pl.pallas_call(
        body, name="attn_bwd", grid=(KV_TILES, N_QBLK),
        in_specs=[smem, q_spec, kc_s, kp_s, vc_s, vp_s, q_spec, dy_spec] + [ANY] * len(deps),
        out_specs=[q_spec, kv_out, kv_out, kv_out, kv_out,
                   pl.BlockSpec((None, HEADS_PER_KV_TILE, LANES), lambda t, n: (t, 0, 0))],
        out_shape=[_sds((SEQ, ATTN_WIDTH)), kv_shape, kv_shape, kv_shape, kv_shape,
                   _sds((KV_TILES, HEADS_PER_KV_TILE, LANES))],
        compiler_params=_cparams(("parallel", "arbitrary")))(sinks, qk, qk, qk, h, h, y, dy, *deps)


def _attn_dh(dq, dkc, dkp, dvc, dvp, cos_t, nsin_t):
    n_tiles = Q_TILES + 2 * KV_TILES
    nxt = lambda n: jnp.minimum(n + 1, N_QBLK - 1)

    def body(dq_ref, kc_ref, kp_ref, vc_ref, vp_ref, c_ref, s_ref, o_ref):
        has_next = pl.program_id(0) < N_QBLK - 1
        cos, sin = c_ref[...], s_ref[...]

        def unrope(t):
            return t * cos + _rotate_half(t) * sin

        for j in range(Q_TILES):
            sl = slice(j * LANES, (j + 1) * LANES)
            o_ref[:, sl] = unrope(dq_ref[:, sl]).astype(o_ref.dtype)
        for j in range(KV_TILES):
            sl = slice(j * LANES, (j + 1) * LANES)
            t = kc_ref[:, sl] + jnp.where(has_next, kp_ref[:, sl], 0.0)
            o_ref[:, ATTN_WIDTH + j * LANES:ATTN_WIDTH + (j + 1) * LANES] = unrope(t).astype(o_ref.dtype)
        o_ref[:, ATTN_WIDTH + KV_WIDTH:] = (vc_ref[...] + jnp.where(has_next, vp_ref[...], 0.0)).astype(o_ref.dtype)

    qb, kb, tb = (ATTN_BLOCK, ATTN_WIDTH), (ATTN_BLOCK, KV_WIDTH), (ATTN_BLOCK, LANES)
    return pl.pallas_call(
        body, name="attn_dh", grid=(N_QBLK,),
        in_specs=[pl.BlockSpec(qb, lambda n: (n, 0)),
                  pl.BlockSpec(kb, lambda n: (n, 0)), pl.BlockSpec(kb, lambda n: (nxt(n), 0)),
                  pl.BlockSpec(kb, lambda n: (n, 0)), pl.BlockSpec(kb, lambda n: (nxt(n), 0)),
                  pl.BlockSpec(tb, lambda n: (n, 0)), pl.BlockSpec(tb, lambda n: (n, 0))],
        out_specs=pl.BlockSpec((ATTN_BLOCK, n_tiles * LANES), lambda n: (n, 0)),
        out_shape=_sds((SEQ, n_tiles * LANES), MM_DTYPE),
        compiler_params=_cparams(("parallel",)))(dq, dkc, dkp, dvc, dvp, cos_t, nsin_t)


POOL_TILE0 = (ATTN_WIDTH + 2 * KV_WIDTH) // POOL_WIDTH


def _shift_rows(x, d, down):
    n = x.shape[0]
    row = lax.broadcasted_iota(jnp.int32, x.shape, 0)
    if down:
        return jnp.where(row >= d, pltpu.roll(x, d, 0), 0.0)
    return jnp.where(row < n - d, pltpu.roll(x, n - d, 0), 0.0)


def _window_sum(x, w, down):
    d = 1
    while d < w:
        x = x + _shift_rows(x, d, down)
        d *= 2
    return x


def _pool_z(u, w):
    t = lax.broadcasted_iota(jnp.int32, u.shape, 0).astype(F32)
    cnt = jnp.minimum(t + 1.0, float(w))
    return _window_sum(u, w, True) / cnt - u, cnt


def _pool_fwd(h, pool_w, pool_scale):
    def body(u_ref, w_ref, s_ref, o_ref):
        for gi, w in enumerate(POOL_WINDOWS):
            sl = slice(gi * POOL_GROUP, (gi + 1) * POOL_GROUP)
            z, _ = _pool_z(u_ref[:, sl], w)
            o_ref[:, sl] = (lax.dot_general(z.astype(MM_DTYPE), w_ref[gi].astype(MM_DTYPE), NN,
                                            preferred_element_type=F32) * s_ref[:, sl]).astype(o_ref.dtype)

    return pl.pallas_call(
        body, name="pool_fwd", grid=(1,),
        in_specs=[pl.BlockSpec((SEQ, POOL_WIDTH), lambda i: (0, POOL_TILE0)),
                  pl.BlockSpec(pool_w.shape, lambda i: (0, 0, 0)), pl.BlockSpec((1, POOL_WIDTH), lambda i: (0, 0))],
        out_specs=pl.BlockSpec((SEQ, POOL_WIDTH), lambda i: (0, 0)),
        out_shape=_sds((SEQ, POOL_WIDTH), MM_DTYPE), compiler_params=_cparams(("arbitrary",)))(h, pool_w, pool_scale)


def _pool_bwd(h, pool_w, pool_scale, dmix, dy_tile0):
    def body(u_ref, w_ref, s_ref, dy_ref, du_ref, dw_ref, ds_ref):
        for gi, w in enumerate(POOL_WINDOWS):
            sl = slice(gi * POOL_GROUP, (gi + 1) * POOL_GROUP)
            z, cnt = _pool_z(u_ref[:, sl], w)
            zb, wb = z.astype(MM_DTYPE), w_ref[gi].astype(MM_DTYPE)
            dy = dy_ref[:, sl]
            zp = lax.dot_general(zb, wb, NN, preferred_element_type=F32)
            ds_ref[:, sl] = jnp.sum(dy * zp, axis=0, keepdims=True)
            dyo = (dy * s_ref[:, sl]).astype(MM_DTYPE)
            dw_ref[gi] = lax.dot_general(zb, dyo, TN, preferred_element_type=F32)
            dz = lax.dot_general(dyo, wb, NT, preferred_element_type=F32)
            du_ref[:, sl] = (_window_sum(dz / cnt, w, False) - dz).astype(du_ref.dtype)

    return pl.pallas_call(
        body, name="pool_bwd", grid=(1,),
        in_specs=[pl.BlockSpec((SEQ, POOL_WIDTH), lambda i: (0, POOL_TILE0)),
                  pl.BlockSpec(pool_w.shape, lambda i: (0, 0, 0)), pl.BlockSpec((1, POOL_WIDTH), lambda i: (0, 0)),
                  pl.BlockSpec((SEQ, POOL_WIDTH), lambda i: (0, dy_tile0))],
        out_specs=[pl.BlockSpec((SEQ, POOL_WIDTH), lambda i: (0, 0)), pl.BlockSpec(pool_w.shape, lambda i: (0, 0, 0)),
                   pl.BlockSpec((1, POOL_WIDTH), lambda i: (0, 0))],
        out_shape=[_sds((SEQ, POOL_WIDTH), MM_DTYPE), _sds(pool_w.shape), _sds((1, POOL_WIDTH))],
        compiler_params=_cparams(("arbitrary",)))(h, pool_w, pool_scale, dmix)


def _ssm_discretize(lr, li, ldt, br, bi):
    dt = jnp.exp(ldt)
    mag = jnp.exp(lr * dt)
    ar, ai = mag * jnp.cos(li * dt), mag * jnp.sin(li * dt)
    nr, ni = ar - 1.0, ai
    den = lr * lr + li * li
    zr = (nr * lr + ni * li) / den
    zi = (ni * lr - nr * li) / den
    return ar, ai, zr * br - zi * bi, zr * bi + zi * br


def _ssm_prep(lr, li, ldt, br, bi):
    def body(lr_ref, li_ref, ldt_ref, br_ref, bi_ref, ar_ref, ai_ref, bbr_ref, bbi_ref):
        outs = _ssm_discretize(lr_ref[...], li_ref[...], ldt_ref[...], br_ref[...], bi_ref[...])
        for o, v in zip((ar_ref, ai_ref, bbr_ref, bbi_ref), outs):
            o[...] = v

    row, mat = _sds((1, SSM_CH)), _sds((SSM_GROUP, SSM_CH))
    return pl.pallas_call(body, name="ssm_prep", out_shape=[row, row, mat, mat])(lr, li, ldt, br, bi)


def _ssm_prep_bwd(lr, li, ldt, br, bi, dar8, dai8, dbbr, dbbi):
    def body(lr_ref, li_ref, ldt_ref, br_ref, bi_ref, dar_ref, dai_ref, dbbr_ref, dbbi_ref, *outs):
        args = (lr_ref[...], li_ref[...], ldt_ref[...], br_ref[...], bi_ref[...])
        _, vjp = jax.vjp(_ssm_discretize, *args)
        cot = (jnp.sum(dar_ref[...], axis=0, keepdims=True), jnp.sum(dai_ref[...], axis=0, keepdims=True),
               dbbr_ref[...], dbbi_ref[...])
        for o, v in zip(outs, vjp(cot)):
            o[...] = v

    row, mat = _sds((1, SSM_CH)), _sds((SSM_GROUP, SSM_CH))
    return pl.pallas_call(body, name="ssm_prep_bwd", out_shape=[row, row, row, mat, mat])(
        lr, li, ldt, br, bi, dar8, dai8, dbbr, dbbi)


def _ssm_diag(name, full):
    tiles = SCAN_CW // LANES
    groups = LANES // SSM_STATE
    col = lambda t, part: (t // tiles) * 2 * tiles + part * tiles + t % tiles

    def body(x_ref, o_ref):
        lane = lax.broadcasted_iota(jnp.int32, (SSM_GROUP, LANES), 1)
        out = x_ref[0:SSM_GROUP, :]
        for k in range(1, groups):
            out = jnp.where(lane >= k * SSM_STATE, x_ref[k * SSM_GROUP:(k + 1) * SSM_GROUP, :], out)
        o_ref[...] = out

    return pl.pallas_call(
        body, name=name, grid=(SSM_CH // LANES, 2),
        in_specs=[pl.BlockSpec((groups * SSM_GROUP, LANES), lambda t, part: (t, col(t, part)))],
        out_specs=pl.BlockSpec((SSM_GROUP, LANES), lambda t, part: (0, col(t, part))),
        out_shape=_sds((SSM_GROUP, 2 * SSM_CH)), compiler_params=_cparams(("parallel", "parallel")))(full)


def _scan_layout(re, im):
    r = re.shape[0]
    return jnp.stack([re.reshape(r, SCAN_NB, SCAN_CW), im.reshape(r, SCAN_NB, SCAN_CW)], axis=2).reshape(r, 2 * SSM_CH)


def _scan_unlayout(x):
    r = x.shape[0]
    x = x.reshape(r, SCAN_NB, 2, SCAN_CW)
    return x[:, :, 0].reshape(r, SSM_CH), x[:, :, 1].reshape(r, SSM_CH)


def _time_permute(u):
    s, c = u.shape
    return u.reshape(SUBLANES, s // SUBLANES, c).transpose(1, 0, 2).reshape(s, c)


def _time_unpermute(u):
    s, c = u.shape
    return u.reshape(s // SUBLANES, SUBLANES, c).transpose(1, 0, 2).reshape(s, c)


def _ssm_scan(name, a_vec, x, reverse, s_prev=None):
    nsteps = SEQ // SUBLANES
    cw = SCAN_CW
    with_da = s_prev is not None

    def body(a_ref, x_ref, *rest):
        if with_da:
            s_ref, o_ref, da_ref = rest
        else:
            o_ref, = rest
        ar = jnp.broadcast_to(a_ref[:, :cw], (SUBLANES, cw))
        ai = jnp.broadcast_to(a_ref[:, cw:], (SUBLANES, cw))
        seg = lax.broadcasted_iota(jnp.int32, (SUBLANES, cw), 0)

        def toward(v):
            if reverse:
                return jnp.where(seg < SUBLANES - 1, pltpu.roll(v, SUBLANES - 1, 0), 0.0)
            return jnp.where(seg >= 1, pltpu.roll(v, 1, 0), 0.0)

        def rows(j):
            jj = nsteps - 1 - j if reverse else j
            return pl.ds(pl.multiple_of(jj * SUBLANES, SUBLANES), SUBLANES)

        def cmul(pr, pi, qr, qi):
            return pr * qr - pi * qi, pr * qi + pi * qr

        def local(j, c):
            sr, si = c
            r = rows(j)
            mr, mi = cmul(ar, ai, sr, si)
            return mr + x_ref[r, :cw], mi + x_ref[r, cw:]

        zero = jnp.zeros((SUBLANES, cw), F32)
        fr, fi = lax.fori_loop(0, nsteps, local, (zero, zero))

        def power(_, c):
            return cmul(ar, ai, *c)

        pr, pi = lax.fori_loop(0, nsteps - 1, power, (ar, ai))
        tr, ti = fr, fi
        for _ in range(SUBLANES - 1):
            mr, mi = cmul(pr, pi, toward(tr), toward(ti))
            tr, ti = fr + mr, fi + mi
        init = (toward(tr), toward(ti))

        def full(j, c):
            r = rows(j)
            if with_da:
                sr, si, dar, dai = c
            else:
                sr, si = c
            mr, mi = cmul(ar, ai, sr, si)
            sr, si = mr + x_ref[r, :cw], mi + x_ref[r, cw:]
            o_ref[r, :cw] = sr
            o_ref[r, cw:] = si
            if not with_da:
                return sr, si
            jj = nsteps - 1 - j
            rp = pl.ds(pl.multiple_of(jnp.maximum(jj - 1, 0) * SUBLANES, SUBLANES), SUBLANES)
            last = pl.ds((nsteps - 1) * SUBLANES, SUBLANES)
            first = jj == 0
            spr = jnp.where(first, jnp.where(seg >= 1, pltpu.roll(s_ref[last, :cw], 1, 0), 0.0), s_ref[rp, :cw])
            spi = jnp.where(first, jnp.where(seg >= 1, pltpu.roll(s_ref[last, cw:], 1, 0), 0.0), s_ref[rp, cw:])
            return sr, si, dar + sr * spr + si * spi, dai + si * spr - sr * spi

        if with_da:
            _, _, dar, dai = lax.fori_loop(0, nsteps, full, init + (zero, zero))
            da_ref[:, :cw] = dar
            da_ref[:, cw:] = dai
        else:
            lax.fori_loop(0, nsteps, full, init)

    blk = pl.BlockSpec((SEQ, 2 * cw), lambda b: (0, b))
    a_spec = pl.BlockSpec((1, 2 * cw), lambda b: (0, b))
    in_specs, args = [a_spec, blk], [a_vec, x]
    out_specs, out_shape = blk, _sds((SEQ, 2 * SSM_CH))
    if with_da:
        in_specs, args = in_specs + [blk], args + [s_prev]
        out_specs = [blk, pl.BlockSpec((SUBLANES, 2 * cw), lambda b: (0, b))]
        out_shape = [out_shape, _sds((SUBLANES, 2 * SSM_CH))]
    return pl.pallas_call(body, name=name, grid=(SCAN_NB,), in_specs=in_specs, out_specs=out_specs,
                          out_shape=out_shape, compiler_params=_cparams(("parallel",)))(*args)


def _ssm_gelu(yp, up, dvec):
    tm = _tile(SEQ, 512)

    def body(y_ref, u_ref, d_ref, yf_ref, g_ref):
        yf = y_ref[...] + d_ref[...] * u_ref[...]
        yf_ref[...] = yf
        g_ref[...] = jax.nn.gelu(yf).astype(g_ref.dtype)

    blk = pl.BlockSpec((tm, SSM_WIDTH), lambda i: (i, 0))
    row = pl.BlockSpec((1, SSM_WIDTH), lambda i: (0, 0))
    return pl.pallas_call(body, name="ssm_gelu", grid=(SEQ // tm,), in_specs=[blk, blk, row], out_specs=[blk, blk],
                          out_shape=[_sds((SEQ, SSM_WIDTH)), _sds((SEQ, SSM_WIDTH), MM_DTYPE)],
                          compiler_params=_cparams(("parallel",)))(yp, up, dvec)


def _ssm_gelu_bwd(yf, dgy, up, dvec):
    tm = _tile(SEQ, 512)

    def body(yf_ref, dg_ref, u_ref, d_ref, dyf_ref, du_ref, dd_ref):
        _, vjp = jax.vjp(jax.nn.gelu, yf_ref[...])
        dyf, = vjp(dg_ref[...])
        dyf_ref[...] = dyf.astype(dyf_ref.dtype)
        du_ref[...] = d_ref[...] * dyf

        @pl.when(pl.program_id(0) == 0)
        def _():
            dd_ref[...] = jnp.zeros_like(dd_ref)

        dd_ref[...] += jnp.sum(dyf * u_ref[...], axis=0, keepdims=True)

    blk = pl.BlockSpec((tm, SSM_WIDTH), lambda i: (i, 0))
    row = pl.BlockSpec((1, SSM_WIDTH), lambda i: (0, 0))
    return pl.pallas_call(body, name="ssm_gelu_bwd", grid=(SEQ // tm,), in_specs=[blk, blk, blk, row],
                          out_specs=[blk, blk, row],
                          out_shape=[_sds((SEQ, SSM_WIDTH), MM_DTYPE), _sds((SEQ, SSM_WIDTH)), _sds((1, SSM_WIDTH))],
                          compiler_params=_cparams(("arbitrary",)))(yf, dgy, up, dvec)


def _glu(ab):
    return ab[:, :SSM_WIDTH] * jax.nn.sigmoid(ab[:, SSM_WIDTH:])


def _ssm_glu(ab):
    tm = _tile(SEQ, 512)

    def body(ab_ref, o_ref):
        o_ref[...] = _glu(ab_ref[...]).astype(o_ref.dtype)

    return pl.pallas_call(body, name="ssm_glu", grid=(SEQ // tm,),
                          in_specs=[pl.BlockSpec((tm, 2 * SSM_WIDTH), lambda i: (i, 0))],
                          out_specs=pl.BlockSpec((tm, SSM_WIDTH), lambda i: (i, 0)),
                          out_shape=_sds((SEQ, SSM_WIDTH), MM_DTYPE), compiler_params=_cparams(("parallel",)))(ab)


def _ssm_glu_bwd(ab, dout):
    tm = _tile(SEQ, 512)

    def body(ab_ref, do_ref, dab_ref):
        _, vjp = jax.vjp(_glu, ab_ref[...])
        dab, = vjp(do_ref[...])
        dab_ref[...] = dab.astype(dab_ref.dtype)

    return pl.pallas_call(body, name="ssm_glu_bwd", grid=(SEQ // tm,),
                          in_specs=[pl.BlockSpec((tm, 2 * SSM_WIDTH), lambda i: (i, 0)),
                                    pl.BlockSpec((tm, SSM_WIDTH), lambda i: (i, 0))],
                          out_specs=pl.BlockSpec((tm, 2 * SSM_WIDTH), lambda i: (i, 0)),
                          out_shape=_sds((SEQ, 2 * SSM_WIDTH), MM_DTYPE), compiler_params=_cparams(("parallel",)))(ab, dout)


def _add2(name, a, b, out_dtype):
    tm = _tile(a.shape[0], 512)

    def body(a_ref, b_ref, o_ref):
        o_ref[...] = (a_ref[...] + b_ref[...]).astype(o_ref.dtype)

    blk = pl.BlockSpec((tm, a.shape[1]), lambda i: (i, 0))
    return pl.pallas_call(body, name=name, grid=(a.shape[0] // tm,), in_specs=[blk, blk], out_specs=blk,
                          out_shape=_sds(a.shape, out_dtype), compiler_params=_cparams(("parallel",)))(a, b)


def _layer_norm(r, g, b):
    mu = r.mean(-1, keepdims=True)
    var = jnp.square(r - mu).mean(-1, keepdims=True)
    return (r - mu) * lax.rsqrt(var + LN_EPS) * g + b


def _ln_fwd(name, x, y, g, b):
    tm = _tile(SEQ, 256)

    def body(x_ref, y_ref, g_ref, b_ref, r_ref, o_ref, ob_ref):
        r = DEEPNORM_ALPHA * x_ref[...] + y_ref[...]
        r_ref[...] = r
        o = _layer_norm(r, g_ref[...], b_ref[...])
        o_ref[...] = o
        ob_ref[...] = o.astype(ob_ref.dtype)

    blk = pl.BlockSpec((tm, D_MODEL), lambda i: (i, 0))
    row = pl.BlockSpec((1, D_MODEL), lambda i: (0, 0))
    return pl.pallas_call(body, name=name, grid=(SEQ // tm,), in_specs=[blk, blk, row, row], out_specs=[blk, blk, blk],
                          out_shape=[_sds((SEQ, D_MODEL))] * 2 + [_sds((SEQ, D_MODEL), MM_DTYPE)],
                          compiler_params=_cparams(("parallel",)))(x, y, g, b)


def _ln_bwd(name, r, g, b, da, db=None, dep=None):
    tm = _tile(SEQ, 256)
    two = db is not None
    deps = _as_list(dep)

    def body(r_ref, g_ref, b_ref, da_ref, *rest):
        dr_ref, drb_ref, dg_ref, dbeta_ref = rest[-4:]
        dout = DEEPNORM_ALPHA * da_ref[...] + rest[0][...] if two else da_ref[...]
        _, vjp = jax.vjp(_layer_norm, r_ref[...], g_ref[...], b_ref[...])
        dr, dg, dbeta = vjp(dout)
        dr_ref[...] = dr
        drb_ref[...] = dr.astype(drb_ref.dtype)

        @pl.when(pl.program_id(0) == 0)
        def _():
            dg_ref[...] = jnp.zeros_like(dg_ref)
            dbeta_ref[...] = jnp.zeros_like(dbeta_ref)

        dg_ref[...] += dg
        dbeta_ref[...] += dbeta

    blk = pl.BlockSpec((tm, D_MODEL), lambda i: (i, 0))
    row = pl.BlockSpec((1, D_MODEL), lambda i: (0, 0))
    args = [r, g, b, da] + ([db] if two else []) + deps
    return pl.pallas_call(body, name=name, grid=(SEQ // tm,),
                          in_specs=[blk, row, row, blk] + ([blk] if two else []) + [ANY] * len(deps),
                          out_specs=[blk, blk, row, row],
                          out_shape=[_sds((SEQ, D_MODEL)), _sds((SEQ, D_MODEL), MM_DTYPE), _sds((1, D_MODEL)), _sds((1, D_MODEL))],
                          compiler_params=_cparams(("arbitrary",)))(*args)


FFN_TM = 128
HALO = SUBLANES


def _conv_taps(cur, halo):
    row = lax.broadcasted_iota(jnp.int32, cur.shape, 0)
    h1 = jnp.where(row == 0, halo[HALO - 1:HALO, :], pltpu.roll(cur, 1, 0))
    h2 = jnp.where(row == 0, halo[HALO - 2:HALO - 1, :], jnp.where(row == 1, halo[HALO - 1:HALO, :], pltpu.roll(cur, 2, 0)))
    return h1, h2


def _conv_fwd(cur, halo, w_ref, b_ref):
    h1, h2 = _conv_taps(cur, halo)
    return b_ref[...] + h2 * w_ref[0:1, :] + h1 * w_ref[1:2, :] + cur * w_ref[2:3, :], h1, h2


def _gate(val, gate):
    return jax.nn.silu(gate) * val


def _ffn_specs(tm):
    nb = tm // HALO
    cur = lambda off: pl.BlockSpec((None, tm, FS), lambda p, i: (p + off, i, 0))
    halo = lambda off: pl.BlockSpec((None, HALO, FS), lambda p, i: (p + off, jnp.maximum(i * nb - 1, 0), 0))
    cw = lambda off: pl.BlockSpec((None, CONV_WIDTH, FS), lambda p, i: (p + off, 0, 0))
    cb = lambda off: pl.BlockSpec((None, 1, FS), lambda p, i: (p + off, 0, 0))
    return cur, halo, cw, cb


def _ffn_act(hf, conv_w, conv_b):
    tm = _tile(SEQ, FFN_TM, SUBLANES)
    cur, halo, cw, cb = _ffn_specs(tm)

    def body(v_ref, vh_ref, g_ref, gh_ref, wv_ref, wg_ref, bv_ref, bg_ref, o_ref):
        live = pl.program_id(1) > 0
        vh = jnp.where(live, vh_ref[...], 0.0)
        gh = jnp.where(live, gh_ref[...], 0.0)
        val, _, _ = _conv_fwd(v_ref[...], vh, wv_ref, bv_ref)
        gate, _, _ = _conv_fwd(g_ref[...], gh, wg_ref, bg_ref)
        o_ref[...] = _gate(val, gate).astype(o_ref.dtype)

    return pl.pallas_call(
        body, name="ffn_act", grid=(2, SEQ // tm),
        in_specs=[cur(0), halo(0), cur(2), halo(2), cw(0), cw(2), cb(0), cb(2)],
        out_specs=pl.BlockSpec((None, tm, FS), lambda p, i: (p, i, 0)),
        out_shape=_sds((2, SEQ, FS), MM_DTYPE), compiler_params=_cparams(("parallel", "parallel")))(
            hf, hf, hf, hf, conv_w, conv_w, conv_b, conv_b)


def _ffn_act_bwd(hf, conv_w, conv_b, dact, dep=None):
    tm = _tile(SEQ, FFN_TM, SUBLANES)
    cur, halo, cw, cb = _ffn_specs(tm)
    deps = _as_list(dep)

    def body(v_ref, vh_ref, g_ref, gh_ref, wv_ref, wg_ref, bv_ref, bg_ref, da_ref, *rest):
        dhc_ref, dw_ref, dbias_ref = rest[-3:]
        dv_ref, dg_ref = dhc_ref.at[0], dhc_ref.at[1]
        dwv_ref, dwg_ref = dw_ref.at[0], dw_ref.at[1]
        dbv_ref, dbg_ref = dbias_ref.at[0], dbias_ref.at[1]
        i = pl.program_id(1)
        live = i > 0
        vh = jnp.where(live, vh_ref[...], 0.0)
        gh = jnp.where(live, gh_ref[...], 0.0)
        vcur, gcur = v_ref[...], g_ref[...]
        val, v1, v2 = _conv_fwd(vcur, vh, wv_ref, bv_ref)
        gate, g1, g2 = _conv_fwd(gcur, gh, wg_ref, bg_ref)
        _, vjp = jax.vjp(_gate, val, gate)
        dval, dgate = vjp(da_ref[...])
        dv_ref[...] = dval
        dg_ref[...] = dgate

        @pl.when(i == 0)
        def _():
            dw_ref[...] = jnp.zeros_like(dw_ref)
            dbias_ref[...] = jnp.zeros_like(dbias_ref)

        for d, taps, dwk_ref, dbk_ref in ((dval, (v2, v1, vcur), dwv_ref, dbv_ref), (dgate, (g2, g1, gcur), dwg_ref, dbg_ref)):
            for k in range(CONV_WIDTH):
                dwk_ref[k:k + 1, :] += jnp.sum(d * taps[k], axis=0, keepdims=True)
            dbk_ref[...] += jnp.sum(d, axis=0, keepdims=True)

    return pl.pallas_call(
        body, name="ffn_act_bwd", grid=(2, SEQ // tm),
        in_specs=[cur(0), halo(0), cur(2), halo(2), cw(0), cw(2), cb(0), cb(2),
                  pl.BlockSpec((None, tm, FS), lambda p, i: (p, i, 0))] + [ANY] * len(deps),
        out_specs=[pl.BlockSpec((None, 2, tm, FS), lambda p, i: (p, 0, i, 0)),
                   pl.BlockSpec((None, 2, CONV_WIDTH, FS), lambda p, i: (p, 0, 0, 0)),
                   pl.BlockSpec((None, 2, 1, FS), lambda p, i: (p, 0, 0, 0))],
        out_shape=[_sds((2, 2, SEQ, FS)), _sds((2, 2, CONV_WIDTH, FS)), _sds((2, 2, 1, FS))],
        compiler_params=_cparams(("parallel", "arbitrary")))(hf, hf, hf, hf, conv_w, conv_w, conv_b, conv_b, dact, *deps)


def _conv_bwd_input(dhc, conv_w):
    tm = _tile(SEQ, FFN_TM, SUBLANES)
    nb = tm // HALO
    nblk = SEQ // tm

    def body(d_ref, nx_ref, w_ref, o_ref):
        cur = d_ref[...]
        nxt = jnp.where(pl.program_id(2) < nblk - 1, nx_ref[...], 0.0)
        row = lax.broadcasted_iota(jnp.int32, cur.shape, 0)
        d1 = jnp.where(row == tm - 1, nxt[0:1, :], pltpu.roll(cur, tm - 1, 0))
        d2 = jnp.where(row == tm - 1, nxt[1:2, :], jnp.where(row == tm - 2, nxt[0:1, :], pltpu.roll(cur, tm - 2, 0)))
        o_ref[...] = (cur * w_ref[2:3, :] + d1 * w_ref[1:2, :] + d2 * w_ref[0:1, :]).astype(o_ref.dtype)

    blk = pl.BlockSpec((None, None, tm, FS), lambda p, kd, i: (p, kd, i, 0))
    return pl.pallas_call(
        body, name="conv_bwd_input", grid=(2, 2, nblk),
        in_specs=[blk, pl.BlockSpec((None, None, HALO, FS), lambda p, kd, i: (p, kd, jnp.minimum((i + 1) * nb, SEQ // HALO - 1), 0)),
                  pl.BlockSpec((None, CONV_WIDTH, FS), lambda p, kd, i: (2 * kd + p, 0, 0))],
        out_specs=blk, out_shape=_sds((2, 2, SEQ, FS), MM_DTYPE),
        compiler_params=_cparams(("parallel", "parallel", "parallel")))(dhc, dhc, conv_w)


def _loss(y, target):
    tm = _tile(SEQ, 256)

    def body(y_ref, t_ref, dy_ref, l_ref):
        err = y_ref[...] - t_ref[...]
        dy_ref[...] = err * (1.0 / D_MODEL)

        @pl.when(pl.program_id(0) == 0)
        def _():
            l_ref[...] = jnp.zeros_like(l_ref)

        l_ref[...] += 0.5 * jnp.sum(jnp.mean(jnp.square(err), axis=-1))

    blk = pl.BlockSpec((tm, D_MODEL), lambda i: (i, 0))
    return pl.pallas_call(body, name="loss", grid=(SEQ // tm,), in_specs=[blk, blk],
                          out_specs=[blk, pl.BlockSpec((SUBLANES, LANES), lambda i: (0, 0))],
                          out_shape=[_sds((SEQ, D_MODEL)), _sds((SUBLANES, LANES))],
                          compiler_params=_cparams(("arbitrary",)))(y, target)


ADAM_BLOCK_BYTES = 3 << 19
ELEMENTWISE_COLS = 1024


def _adamw_math(w, g, m, v):
    nm = ADAM_B1 * m + (1.0 - ADAM_B1) * g
    nv = ADAM_B2 * v + (1.0 - ADAM_B2) * jnp.square(g)
    m_hat = nm / (1.0 - ADAM_B1 ** ADAM_STEP)
    v_hat = nv / (1.0 - ADAM_B2 ** ADAM_STEP)
    return -ADAM_LR * (m_hat / (jnp.sqrt(v_hat) + ADAM_EPS) + ADAM_WD * w), nm, nv


def _adamw(name, w, g, m, v):
    r, c = w.shape
    tr = _tile(r, max(SUBLANES, ADAM_BLOCK_BYTES // (4 * c)), SUBLANES)

    def body(w_ref, g_ref, m_ref, v_ref, d_ref, nm_ref, nv_ref):
        d_ref[...], nm_ref[...], nv_ref[...] = _adamw_math(w_ref[...], g_ref[...], m_ref[...], v_ref[...])

    blk = pl.BlockSpec((tr, c), lambda i: (i, 0))
    return pl.pallas_call(body, name=name, grid=(r // tr,), in_specs=[blk] * 4, out_specs=[blk] * 3,
                          out_shape=[_sds((r, c))] * 3, compiler_params=_cparams(("parallel",)))(w, g, m, v)


def _adamw_big(name, l, c_idx, w, m, v, g_own, g_got, prev):
    depth, _, r, c = w.shape
    tc = _tile(c, ELEMENTWISE_COLS)
    tr = _tile(r, max(SUBLANES, ADAM_BLOCK_BYTES // (4 * tc)), SUBLANES)

    def body(c_ref, w_ref, m_ref, v_ref, own_ref, got_ref, *rest):
        g_ref, d_ref, nm_ref, nv_ref = rest[-4:]
        g = jnp.where(pl.program_id(0) == c_ref[0], own_ref[...], got_ref[...])
        g_ref[...] = g
        d_ref[...], nm_ref[...], nv_ref[...] = _adamw_math(w_ref[...], g, m_ref[...], v_ref[...])

    stacked = pl.BlockSpec((None, None, tr, tc), lambda h, i, j, cr: (l, h, i, j))
    own = pl.BlockSpec((tr, tc), lambda h, i, j, cr: (jnp.where(h == cr[0], i, 0), jnp.where(h == cr[0], j, 0)))
    got = pl.BlockSpec((tr, tc), lambda h, i, j, cr: (jnp.where(h == cr[0], 0, i), jnp.where(h == cr[0], 0, j)))
    grid_spec = pltpu.PrefetchScalarGridSpec(
        num_scalar_prefetch=1, grid=(2, r // tr, c // tc),
        in_specs=[stacked] * 3 + [own, got] + ([ANY] * 4 if prev else []), out_specs=[stacked] * 4)
    return pl.pallas_call(
        body, name=name, grid_spec=grid_spec, out_shape=[_sds((depth, 2, r, c))] * 4,
        input_output_aliases={6 + k: k for k in range(4)} if prev else {},
        compiler_params=_cparams(("arbitrary", "arbitrary", "arbitrary")))(c_idx, w, m, v, g_own, g_got, *(prev or ()))


ANY = pl.BlockSpec(memory_space=pl.ANY)


def _place():
    x, y, c = lax.axis_index("x"), lax.axis_index("y"), lax.axis_index("c")
    chips = [(1 - x, y), (x, 1 - y), (1 - x, 1 - y)]
    return x, y, c, chips


def _cast_place(name, w, l, me_idx, out_dtype):
    _, _, r, c = w.shape
    tr = _tile(r, max(2 * SUBLANES, COPY_BLOCK_BYTES // (4 * c)), 2 * SUBLANES)

    def body(me_ref, w_ref, o_ref):
        o_ref[...] = w_ref[...].astype(o_ref.dtype)

    grid_spec = pltpu.PrefetchScalarGridSpec(
        num_scalar_prefetch=1, grid=(2, r // tr),
        in_specs=[pl.BlockSpec((None, None, tr, c), lambda h, i, me: (l, h, i, 0))],
        out_specs=pl.BlockSpec((None, None, tr, c), lambda h, i, me: (me[0], h, i, 0)))
    return pl.pallas_call(body, name=name, grid_spec=grid_spec, out_shape=_sds((N_CHIPS, 2, r, c), out_dtype),
                          compiler_params=_cparams(("parallel", "parallel")))(me_idx, w)


HBM = pl.BlockSpec(memory_space=pltpu.HBM)
SEM = pl.BlockSpec(memory_space=pltpu.SEMAPHORE)
TOKEN = (SUBLANES, LANES)


def _comm_call(name, body, hbm, sems_in=(), after=None, sems_out=(), token=False):
    n, k = len(hbm), len(sems_out)
    ins = [pltpu.with_memory_space_constraint(a, pltpu.HBM) for a in hbm] + list(sems_in)
    in_specs = [HBM] * n + [SEM] * len(sems_in)
    if after is not None:
        ins.append(after)
        in_specs.append(ANY)
    out_shape = [pltpu.SemaphoreType.DMA((s,)) for s in sems_out] + [pltpu.HBM(a.shape, a.dtype) for a in hbm]
    out_specs = [SEM] * k + [HBM] * n
    if token:
        out_shape.append(_sds(TOKEN))
        out_specs.append(pl.BlockSpec(memory_space=pltpu.VMEM))
    res = pl.pallas_call(
        body, name=name, in_specs=in_specs, out_specs=out_specs, out_shape=out_shape,
        input_output_aliases={i: k + i for i in range(n)},
        compiler_params=pltpu.CompilerParams(has_side_effects=pltpu.SideEffectType.DATAFLOW_SIDE_EFFECTING))(*ins)
    return list(res[:k]), list(res[k:k + n]), (res[k + n] if token else None)


def _remote(src, dst, send, recv, to):
    return pltpu.make_async_remote_copy(src_ref=src, dst_ref=dst, send_sem=send, recv_sem=recv, device_id=to,
                                        device_id_type=MESH)


def _gather_start(name, bufs, after=None, pair=False):
    n = len(bufs)
    o = n + (after is not None)

    def body(*refs):
        ins, (send, recv), token = refs[:n], refs[o:o + 2], refs[-1]
        x, y, c, chips = _place()
        for i in range(n):
            mine = ins[i].at[2 * x + y, c]
            for k, chip in enumerate(chips):
                _remote(mine, mine, send.at[3 * i + k], recv.at[3 * i + k], (*chip, c)).start()
            if pair:
                _remote(mine, mine, send.at[3 * n + i], recv.at[3 * n + i], (x, y, 1 - c)).start()
        token[...] = jnp.zeros(TOKEN, F32)

    n_sems = (3 + pair) * n
    return _comm_call(name, body, bufs, after=after, sems_out=(n_sems, n_sems), token=True)


def _gather_forward(name, bufs, sems, after, pair=False):
    n = len(bufs)
    o = n + 2 + (after is not None)

    def body(*refs):
        ins, (send, recv), (send2, recv2), token = refs[:n], refs[n:n + 2], refs[o:o + 2], refs[-1]
        x, y, c, chips = _place()
        for i in range(n):
            mine = ins[i].at[2 * x + y, c]
            for k, chip in enumerate(chips):
                land = ins[i].at[2 * chip[0] + chip[1], c]
                first = _remote(mine, land, send.at[3 * i + k], recv.at[3 * i + k], (*chip, c))
                first.wait_send()
                first.wait_recv()
                _remote(land, land, send2.at[3 * i + k], recv2.at[3 * i + k], (x, y, 1 - c)).start()
            if pair:
                own = _remote(mine, ins[i].at[2 * x + y, 1 - c], send.at[3 * n + i], recv.at[3 * n + i], (x, y, 1 - c))
                own.wait_send()
                own.wait_recv()
        token[...] = jnp.zeros(TOKEN, F32)

    return _comm_call(name, body, bufs, sems_in=sems, after=after, sems_out=(3 * n, 3 * n), token=True)


def _gather_finish(name, bufs, sems, after):
    n = len(bufs)

    def body(*refs):
        ins, (send, recv) = refs[:n], refs[n:n + 2]
        x, y, c, chips = _place()
        for i in range(n):
            for k, chip in enumerate(chips):
                idx = 2 * chip[0] + chip[1]
                cp = _remote(ins[i].at[idx, c], ins[i].at[idx, 1 - c], send.at[3 * i + k], recv.at[3 * i + k], (x, y, 1 - c))
                cp.wait_send()
                cp.wait_recv()

    return _comm_call(name, body, bufs, sems_in=sems, after=after)[1]


def _swap_start(name, grads):
    n = len(grads)
    lands = [lax.empty((g.shape[0],) + g.shape[2:], g.dtype) for g in grads]

    def body(*refs):
        ins, lnd, (send, recv), token = refs[:n], refs[n:2 * n], refs[2 * n:2 * n + 2], refs[-1]
        x, y, c, _ = _place()
        for i in range(n):
            _remote(ins[i].at[:, 1 - c], lnd[i], send.at[i], recv.at[i], (x, y, 1 - c)).start()
        token[...] = jnp.zeros(TOKEN, F32)

    return _comm_call(name, body, list(grads) + lands, sems_out=(n, n), token=True)


def _swap_wait(name, hbm, sems, after):
    n = len(hbm) // 2

    def body(*refs):
        ins, lnd, (send, recv) = refs[:n], refs[n:2 * n], refs[2 * n:2 * n + 2]
        x, y, c, _ = _place()
        for i in range(n):
            cp = _remote(ins[i].at[:, 1 - c], lnd[i], send.at[i], recv.at[i], (x, y, 1 - c))
            cp.wait_send()
            cp.wait_recv()

    out = _comm_call(name, body, hbm, sems_in=sems, after=after)[1]
    return out[:n], out[n:]


def _pair_add(name, g, got, cm_idx):
    nk, _, r, c = g.shape
    tr = _tile(r, max(2 * SUBLANES, COPY_BLOCK_BYTES // (4 * c)), 2 * SUBLANES)

    def body(cm_ref, g_ref, x_ref, o_ref, land_ref):
        s = (g_ref[...] + x_ref[...]).astype(o_ref.dtype)
        o_ref[...] = s

        @pl.when(pl.program_id(1) == cm_ref[1])
        def _():
            land_ref[...] = s

    grid_spec = pltpu.PrefetchScalarGridSpec(
        num_scalar_prefetch=1, grid=(r // tr, nk),
        in_specs=[pl.BlockSpec((None, None, tr, c), lambda i, k, cm: (k, cm[0], i, 0)),
                  pl.BlockSpec((None, tr, c), lambda i, k, cm: (k, i, 0))],
        out_specs=[pl.BlockSpec((None, tr, c), lambda i, k, cm: (k, i, 0)),
                   pl.BlockSpec((None, tr, c), lambda i, k, cm: (cm[1], i, 0))])
    return pl.pallas_call(body, name=name, grid_spec=grid_spec, out_shape=[_sds((nk, r, c), BF16)] * 2,
                          compiler_params=_cparams(("parallel", "arbitrary")))(cm_idx, g, got)


def _scatter_start(name, parts, lands):
    n = len(parts)

    def body(*refs):
        ins, lnd, (send, recv), token = refs[:n], refs[n:2 * n], refs[2 * n:2 * n + 2], refs[-1]
        x, y, c, chips = _place()
        for i in range(n):
            for k, chip in enumerate(chips):
                _remote(ins[i].at[2 * chip[0] + chip[1]], lnd[i].at[2 * x + y], send.at[3 * i + k], recv.at[3 * i + k],
                        (*chip, c)).start()
        token[...] = jnp.zeros(TOKEN, F32)

    return _comm_call(name, body, list(parts) + list(lands), sems_out=(3 * n, 3 * n), token=True)


def _scatter_wait(name, hbm, sems, after):
    n = len(hbm) // 2

    def body(*refs):
        ins, lnd, (send, recv) = refs[:n], refs[n:2 * n], refs[2 * n:2 * n + 2]
        x, y, c, chips = _place()
        for i in range(n):
            for k, chip in enumerate(chips):
                idx = 2 * chip[0] + chip[1]
                cp = _remote(ins[i].at[idx], lnd[i].at[idx], send.at[3 * i + k], recv.at[3 * i + k], (*chip, c))
                cp.wait_send()
                cp.wait_recv()

    out = _comm_call(name, body, hbm, sems_in=sems, after=after)[1]
    return out[:n], out[n:]


def _sum_leading(name, x, out_dtype=F32):
    nk, r, c = x.shape
    tc = _tile(c, ELEMENTWISE_COLS)
    tr = _tile(r, max(2 * SUBLANES, COPY_BLOCK_BYTES // (nk * tc * x.dtype.itemsize)), 2 * SUBLANES)

    def body(x_ref, o_ref):
        acc = x_ref[0].astype(F32)
        for k in range(1, nk):
            acc = acc + x_ref[k].astype(F32)
        o_ref[...] = acc.astype(o_ref.dtype)

    return pl.pallas_call(body, name=name, grid=(r // tr, c // tc),
                          in_specs=[pl.BlockSpec((nk, tr, tc), lambda i, j: (0, i, j))],
                          out_specs=pl.BlockSpec((tr, tc), lambda i, j: (i, j)), out_shape=_sds((r, c), out_dtype),
                          compiler_params=_cparams(("parallel", "parallel")))(x)


def _exchange_start(name, halves):
    n = len(halves)
    lands = [lax.empty(h.shape, h.dtype) for h in halves]

    def body(*refs):
        ins, lnd, (send, recv), token = refs[:n], refs[n:2 * n], refs[2 * n:2 * n + 2], refs[-1]
        x, y, c, _ = _place()
        for i in range(n):
            _remote(ins[i], lnd[i], send.at[i], recv.at[i], (x, y, 1 - c)).start()
        token[...] = jnp.zeros(TOKEN, F32)

    return _comm_call(name, body, list(halves) + lands, sems_out=(n, n), token=True)


def _exchange_wait(name, hbm, sems, after):
    n = len(hbm) // 2

    def body(*refs):
        ins, lnd, (send, recv) = refs[:n], refs[n:2 * n], refs[2 * n:2 * n + 2]
        x, y, c, _ = _place()
        for i in range(n):
            cp = _remote(ins[i], lnd[i], send.at[i], recv.at[i], (x, y, 1 - c))
            cp.wait_send()
            cp.wait_recv()

    out = _comm_call(name, body, hbm, sems_in=sems, after=after)[1]
    return out[:n], out[n:]


SMALL = ("attn_sinks", "pool_w", "pool_scale", "ssm_lam_re", "ssm_lam_im", "ssm_log_dt", "ssm_b_re", "ssm_b_im",
         "ssm_c_re", "ssm_c_im", "ssm_d", "ln1_g", "ln1_b", "ffn_conv_b", "ln2_g", "ln2_b")
BIG = ("w_in", "ssm_glu_w", "w_out", "ffn_w_up", "ffn_conv_w", "ffn_w_down")
ALL_W = ("w_in", "attn_sinks", "pool_w", "pool_scale", "ssm_lam_re", "ssm_lam_im", "ssm_log_dt", "ssm_b_re", "ssm_b_im",
         "ssm_c_re", "ssm_c_im", "ssm_d", "ssm_glu_w", "w_out", "ln1_g", "ln1_b", "ffn_w_up", "ffn_conv_w", "ffn_conv_b",
         "ffn_w_down", "ln2_g", "ln2_b")
PACK_UNIT = SUBLANES * LANES


def _padded(n):
    return -(-n // PACK_UNIT) * PACK_UNIT


def _pack(arrs):
    cols = []
    for name in SMALL:
        a = arrs[name].reshape(DEPTH, -1)
        cols.append(jnp.pad(a, ((0, 0), (0, _padded(a.shape[1]) - a.shape[1]))))
    return jnp.concatenate(cols, axis=1).reshape(-1, LANES)


def _unpack(packed, shapes):
    flat = packed.reshape(DEPTH, -1)
    out, off = {}, 0
    for name in SMALL:
        n = math.prod(shapes[name][1:])
        out[name] = flat[:, off:off + n].reshape(shapes[name])
        off += _padded(n)
    return out


def _b_rows(b):
    return b.transpose(2, 0, 1).reshape(SSM_GROUP, SSM_CH)


def _b_unrows(b):
    return b.reshape(SSM_GROUP, SSM_N_GROUPS, SSM_STATE).transpose(1, 2, 0)


def _block_diag_in(bb):
    eye = jnp.eye(SSM_N_GROUPS, dtype=F32)
    b3 = bb.reshape(SSM_GROUP, SSM_N_GROUPS, SSM_STATE)
    return jnp.einsum("hgp,gk->ghkp", b3, eye).reshape(SSM_WIDTH, SSM_CH)


def _c_unrows(c):
    return c.reshape(SSM_GROUP, SSM_N_GROUPS, SSM_STATE).transpose(1, 0, 2)


def _block_diag_out(cc):
    eye = jnp.eye(SSM_N_GROUPS, dtype=F32)
    return jnp.einsum("ghp,gk->gpkh", cc, eye).reshape(SSM_CH, SSM_WIDTH)


def _rows_layout(re, im):
    n = re.shape[1]
    return jnp.stack([re.reshape(SCAN_NB, SCAN_CW, n), im.reshape(SCAN_NB, SCAN_CW, n)], axis=1).reshape(2 * SSM_CH, n)


H_POOL0 = ATTN_WIDTH + 2 * KV_WIDTH
H_SSM0 = H_POOL0 + POOL_WIDTH


def _ssm_params(p):
    lr = p["ssm_lam_re"].reshape(1, SSM_CH)
    li = p["ssm_lam_im"].reshape(1, SSM_CH)
    ldt = jnp.repeat(p["ssm_log_dt"], SSM_STATE).reshape(1, SSM_CH)
    return lr, li, ldt, _b_rows(p["ssm_b_re"]), _b_rows(p["ssm_b_im"])


def _layer_fwd(x, xb, p, wg, rope_t, dep, mid):
    cos_t, sin_t = rope_t
    h = _mm_shard_cols("in_proj", xb, wg["w_in"], dep=dep)
    qk = _rope("rope_fwd", h, 0, Q_TILES + KV_TILES, cos_t, sin_t, MM_DTYPE)
    y_attn, y_attn_b = _attn_fwd(qk, h, p["attn_sinks"])
    y_pool = _pool_fwd(h, p["pool_w"], p["pool_scale"].reshape(1, POOL_WIDTH))
    ssm_in = _ssm_params(p)
    ar, ai, bbr, bbi = _ssm_prep(*ssm_in)
    bd = _scan_layout(_block_diag_in(bbr), _block_diag_in(bbi)).astype(MM_DTYPE)
    cc = _rows_layout(_block_diag_out(p["ssm_c_re"]), -_block_diag_out(p["ssm_c_im"])).astype(MM_DTYPE)
    dvec = p["ssm_d"].reshape(1, SSM_WIDTH)
    up = _time_permute(h[:, H_SSM0:])
    xx = _mm_nn("ssm_bu", up, bd, tn=1024)
    ss = _ssm_scan("ssm_scan_fwd", _scan_layout(ar, ai), xx, False)
    yp = _mm_nn("ssm_cs", ss, cc, tk=1024)
    yf, gy = _ssm_gelu(yp, up, dvec)
    ab = _mm_shard_cols("ssm_glu_proj", gy, wg["ssm_glu_w"])
    y_ssm = _time_unpermute(_ssm_glu(ab))
    mix = jnp.concatenate([y_attn_b, y_pool, y_ssm], axis=1)
    mixo = _mm_nn("out_proj", mix, wg["w_out"].reshape(MIX_WIDTH, D_MODEL))
    r1, x1, x1b = _ln_fwd("ln1_fwd", x, mixo, p["ln1_g"].reshape(1, D_MODEL), p["ln1_b"].reshape(1, D_MODEL))
    tokens = mid(x1b)
    hf = _ffn_up(x1b, wg["ffn_w_up"], dep=tokens)
    conv_b = p["ffn_conv_b"].reshape(N_CHIPS, 1, FS)
    act = _ffn_act(hf, wg["ffn_conv_w"], conv_b)
    f = _ffn_down(act, wg["ffn_w_down"].reshape(2, FS, D_MODEL))
    r2, x2, x2b = _ln_fwd("ln2_fwd", x1, f, p["ln2_g"].reshape(1, D_MODEL), p["ln2_b"].reshape(1, D_MODEL))
    saved = dict(xb=xb, h=h, qk=qk, y_attn=y_attn, ssm_in=ssm_in, ar=ar, ai=ai, bd=bd, cc=cc, dvec=dvec, up=up, ss=ss, yf=yf,
                 gy=gy, ab=ab, mix=mix, r1=r1, x1b=x1b, hf=hf, conv_b=conv_b, act=act, r2=r2)
    return x2, x2b, saved


def _layer_bwd(da, db, p, wg, sv, rope_t, run, start):
    cos_t, sin_t = rope_t
    small = {}
    dr2, dr2b, dg, dbeta = _ln_bwd("ln2_bwd" if db is not None else "ln2_bwd_last", sv["r2"], p["ln2_g"].reshape(1, D_MODEL),
                                   p["ln2_b"].reshape(1, D_MODEL), da, db, dep=run("h0", None))
    small["ln2_g"], small["ln2_b"] = dg, dbeta
    w_down = wg["ffn_w_down"].reshape(2, FS, D_MODEL)
    dact = _ffn_down_dact(dr2b, w_down)
    dw_down = _ffn_down_dw(sv["act"], dr2b)
    dhc, dcw, dcb = _ffn_act_bwd(sv["hf"], wg["ffn_conv_w"], sv["conv_b"], dact, dep=run("h1", dw_down))
    dconv_w = dcw.transpose(1, 0, 2, 3).reshape(N_CHIPS, CONV_WIDTH, FS)
    small["ffn_conv_b"] = dcb.transpose(1, 0, 2, 3)
    dh_ffn = _conv_bwd_input(dhc, wg["ffn_conv_w"])
    dw_up = _ffn_up_dw(sv["x1b"], dh_ffn)
    tok = [start("ffn", {"ffn_w_up": dw_up, "ffn_conv_w": dconv_w,
                         "ffn_w_down": dw_down.reshape(N_CHIPS, FS // 2, D_MODEL)})] + run("h2", dw_up)
    dx1_ffn = _ffn_up_dx(dh_ffn, wg["ffn_w_up"], dep=tok)
    dr1, dr1b, dg, dbeta = _ln_bwd("ln1_bwd", sv["r1"], p["ln1_g"].reshape(1, D_MODEL), p["ln1_b"].reshape(1, D_MODEL), dr2,
                                   dx1_ffn, dep=tok)
    small["ln1_g"], small["ln1_b"] = dg, dbeta
    w_out = wg["w_out"].reshape(MIX_WIDTH, D_MODEL)
    dw_out = _mm_tn("out_proj_dw", sv["mix"], dr1b)
    dmix = _mm_nt("out_proj_dx", dr1b, w_out, dep=run("h3", dw_out))
    dq, dkc, dkp, dvc, dvp, dsk = _attn_bwd(sv["qk"], sv["h"], p["attn_sinks"], sv["y_attn"], dmix, 0)
    small["attn_sinks"] = dsk[:, :, 0]
    dh_attn = _attn_dh(dq, dkc, dkp, dvc, dvp, cos_t, -sin_t)
    dh_pool, dpw, dps = _pool_bwd(sv["h"], p["pool_w"], p["pool_scale"].reshape(1, POOL_WIDTH), dmix, ATTN_WIDTH // POOL_WIDTH)
    small["pool_w"], small["pool_scale"] = dpw, dps
    dout_p = _time_permute(dmix[:, ATTN_WIDTH + POOL_WIDTH:])
    dab = _ssm_glu_bwd(sv["ab"], dout_p)
    dgy = _mm_shard_cols_nt("ssm_glu_dx", dab, wg["ssm_glu_w"])
    dw_glu = _mm_shard_cols_tn("ssm_glu_dw", sv["gy"], dab, N_CHIPS)
    dyf, du1, dd = _ssm_gelu_bwd(sv["yf"], dgy, sv["up"], sv["dvec"])
    small["ssm_d"] = dd
    dss = _mm_nt("ssm_cs_dx", dyf, sv["cc"], tn=1024)
    dcre, dcim = _scan_unlayout(_ssm_diag("ssm_c_diag", _mm_tn("ssm_cs_dw", dyf, sv["ss"], tn=1024)))
    small["ssm_c_re"], small["ssm_c_im"] = _c_unrows(dcre), -_c_unrows(dcim)
    gg, da8 = _ssm_scan("ssm_scan_bwd", _scan_layout(sv["ar"], -sv["ai"]), dss, True, sv["ss"])
    du2 = _mm_nt("ssm_bu_dx", gg, sv["bd"], tk=1024)
    dbbr, dbbi = _scan_unlayout(_ssm_diag("ssm_b_diag", _mm_tn("ssm_bu_dw", sv["up"], gg, tn=1024)))
    dar8, dai8 = _scan_unlayout(da8)
    dlr, dli, dldt, dbr, dbi = _ssm_prep_bwd(*sv["ssm_in"], dar8, dai8, dbbr, dbbi)
    small["ssm_lam_re"], small["ssm_lam_im"] = dlr, dli
    small["ssm_log_dt"] = dldt.reshape(SSM_N_GROUPS, SSM_STATE).sum(axis=1)
    small["ssm_b_re"], small["ssm_b_im"] = _b_unrows(dbr), _b_unrows(dbi)
    dh_ssm = _time_unpermute(_add2("ssm_du", du1, du2, MM_DTYPE))
    dh = jnp.concatenate([dh_attn, dh_pool, dh_ssm], axis=1)
    dx_in = _mm_shard_cols_nt("in_proj_dx", dh, wg["w_in"], dep=run("h4", dh))
    dw_in = _mm_shard_cols_tn("in_proj_dw", sv["xb"], dh, N_CHIPS)
    start("mix", {"w_in": dw_in, "ssm_glu_w": dw_glu, "w_out": dw_out.reshape(N_CHIPS, MIX_WIDTH // N_CHIPS, D_MODEL)})
    return dr1, dx_in, small


CONV_PAD = 2 * SUBLANES


def _halved(name, a):
    if name == "ffn_conv_w":
        a = jnp.pad(a, ((0, 0), (0, CONV_PAD - CONV_WIDTH), (0, 0)))
    return a.reshape(a.shape[0], 2, a.shape[1] // 2, a.shape[2])


def _unhalved(name, a):
    a = a.reshape(a.shape[:-3] + (2 * a.shape[-2], a.shape[-1]))
    return a[..., :CONV_WIDTH, :] if name == "ffn_conv_w" else a


class _Reduce:
    def __init__(self, tag, grads, cm_idx):
        self.tag, self.cm_idx, self.names = tag, cm_idx, tuple(grads)
        g4 = [_halved(name, grads[name]) for name in self.names]
        self.sems, self.hbm, self.token = _swap_start("grad_swap_start_" + tag, g4)

    def swapped(self, after):
        g4, got = _swap_wait("grad_swap_wait_" + self.tag, self.hbm, self.sems, after)
        parts, lands = zip(*[_pair_add("grad_pair_add", g, x, self.cm_idx) for g, x in zip(g4, got)])
        self.sems, self.hbm, self.token = _scatter_start("grad_scatter_start_" + self.tag, parts, lands)
        return self.token

    def scattered(self, after):
        _, recv = _scatter_wait("grad_scatter_wait_" + self.tag, self.hbm, self.sems, after)
        halves = [_sum_leading("grad_chip_sum", r) for r in recv]
        self.sems, self.hbm, self.token = _exchange_start("grad_exchange_start_" + self.tag, halves)
        return self.token

    def finish(self, after):
        return _exchange_wait("grad_exchange_wait_" + self.tag, self.hbm, self.sems, after)


def kernel(x, w_in, attn_sinks, pool_w, pool_scale, ssm_lam_re, ssm_lam_im, ssm_log_dt, ssm_b_re, ssm_b_im, ssm_c_re, ssm_c_im, ssm_d, ssm_glu_w, w_out, ln1_g, ln1_b, ffn_w_up, ffn_conv_w, ffn_conv_b, ffn_w_down, ln2_g, ln2_b, loss_target, m_w_in, m_attn_sinks, m_pool_w, m_pool_scale, m_ssm_lam_re, m_ssm_lam_im, m_ssm_log_dt, m_ssm_b_re, m_ssm_b_im, m_ssm_c_re, m_ssm_c_im, m_ssm_d, m_ssm_glu_w, m_w_out, m_ln1_g, m_ln1_b, m_ffn_w_up, m_ffn_conv_w, m_ffn_conv_b, m_ffn_w_down, m_ln2_g, m_ln2_b, v_w_in, v_attn_sinks, v_pool_w, v_pool_scale, v_ssm_lam_re, v_ssm_lam_im, v_ssm_log_dt, v_ssm_b_re, v_ssm_b_im, v_ssm_c_re, v_ssm_c_im, v_ssm_d, v_ssm_glu_w, v_w_out, v_ln1_g, v_ln1_b, v_ffn_w_up, v_ffn_conv_w, v_ffn_conv_b, v_ffn_w_down, v_ln2_g, v_ln2_b):
    w = dict(w_in=w_in, attn_sinks=attn_sinks, pool_w=pool_w, pool_scale=pool_scale, ssm_lam_re=ssm_lam_re,
             ssm_lam_im=ssm_lam_im, ssm_log_dt=ssm_log_dt, ssm_b_re=ssm_b_re, ssm_b_im=ssm_b_im, ssm_c_re=ssm_c_re,
             ssm_c_im=ssm_c_im, ssm_d=ssm_d, ssm_glu_w=ssm_glu_w, w_out=w_out, ln1_g=ln1_g, ln1_b=ln1_b, ffn_w_up=ffn_w_up,
             ffn_conv_w=ffn_conv_w, ffn_conv_b=ffn_conv_b, ffn_w_down=ffn_w_down, ln2_g=ln2_g, ln2_b=ln2_b)
    m = dict(w_in=m_w_in, attn_sinks=m_attn_sinks, pool_w=m_pool_w, pool_scale=m_pool_scale, ssm_lam_re=m_ssm_lam_re,
             ssm_lam_im=m_ssm_lam_im, ssm_log_dt=m_ssm_log_dt, ssm_b_re=m_ssm_b_re, ssm_b_im=m_ssm_b_im, ssm_c_re=m_ssm_c_re,
             ssm_c_im=m_ssm_c_im, ssm_d=m_ssm_d, ssm_glu_w=m_ssm_glu_w, w_out=m_w_out, ln1_g=m_ln1_g, ln1_b=m_ln1_b,
             ffn_w_up=m_ffn_w_up, ffn_conv_w=m_ffn_conv_w, ffn_conv_b=m_ffn_conv_b, ffn_w_down=m_ffn_w_down, ln2_g=m_ln2_g,
             ln2_b=m_ln2_b)
    v = dict(w_in=v_w_in, attn_sinks=v_attn_sinks, pool_w=v_pool_w, pool_scale=v_pool_scale, ssm_lam_re=v_ssm_lam_re,
             ssm_lam_im=v_ssm_lam_im, ssm_log_dt=v_ssm_log_dt, ssm_b_re=v_ssm_b_re, ssm_b_im=v_ssm_b_im, ssm_c_re=v_ssm_c_re,
             ssm_c_im=v_ssm_c_im, ssm_d=v_ssm_d, ssm_glu_w=v_ssm_glu_w, w_out=v_w_out, ln1_g=v_ln1_g, ln1_b=v_ln1_b,
             ffn_w_up=v_ffn_w_up, ffn_conv_w=v_ffn_conv_w, ffn_conv_b=v_ffn_conv_b, ffn_w_down=v_ffn_w_down, ln2_g=v_ln2_g,
             ln2_b=v_ln2_b)
    c_pos = lax.axis_index("c").astype(jnp.int32)
    chip = (2 * lax.axis_index("x") + lax.axis_index("y")).astype(jnp.int32)
    c_idx, chip_idx, cm_idx = c_pos.reshape(1), chip.reshape(1), jnp.stack([c_pos, chip])
    rope_t = _rope_tables()
    xs = x.reshape(SEQ, D_MODEL)
    xb = xs.astype(MM_DTYPE)
    for t in (w, m, v):
        t["ffn_w_up"] = jnp.swapaxes(t["ffn_w_up"], 1, 2)
    wh, mh, vh = ({n: _halved(n, t[n]) for n in BIG} for t in (w, m, v))

    def place(l):
        return [_cast_place("place_" + n, wh[n], l, chip_idx, F32 if n == "ffn_conv_w" else MM_DTYPE) for n in BIG]

    n_mix = BIG.index("ffn_w_up")

    def gather_start(l, after):
        bufs = place(l)
        return (_gather_start("gather_start_%d_mix" % l, bufs[:n_mix], after),
                _gather_start("gather_start_%d_ffn" % l, bufs[n_mix:], after))

    def gather_wait(l, group, started, after):
        sems, bufs, _ = _gather_forward("gather_forward_%d_%s" % (l, group), started[1], started[0], after)
        bufs = _gather_finish("gather_finish_%d_%s" % (l, group), bufs, sems, after)
        names = BIG[:n_mix] if group == "mix" else BIG[n_mix:]
        return bufs, {n: _unhalved(n, g) for n, g in zip(names, bufs)}

    flight = gather_start(0, None)
    gathered, saved = [gather_wait(0, "mix", flight[0], None)[1]], []
    for l in range(DEPTH):
        nxt = {}

        def mid(after):
            bufs, wg_ffn = gather_wait(l, "ffn", flight[1], after)
            gathered[l].update(wg_ffn)
            if l + 1 == DEPTH:
                return []
            nxt["flight"] = gather_start(l + 1, bufs[0])
            return [nxt["flight"][0][2], nxt["flight"][1][2]]

        xs, xb, sv = _layer_fwd(xs, xb, {n: w[n][l] for n in SMALL}, gathered[l], rope_t, None, mid)
        saved.append(sv)
        if l + 1 < DEPTH:
            flight = nxt["flight"]
            gathered.append(gather_wait(l + 1, "mix", flight[0], xb)[1])
    dy, loss_tile = _loss(xs, loss_target.reshape(SEQ, D_MODEL))
    loss = lax.psum(loss_tile[0, 0], ("x", "y", "c"))

    big_out = {n: None for n in BIG}
    small_g = {n: [None] * DEPTH for n in SMALL}
    agenda = {}
    tail = []
    plan = {"ffn": (("h3", 0), ("end", 0), ("h1", -1)), "mix": (("h1", -1), ("h3", -1), ("h4", -1))}
    tail_rank = {("mix", 0): 0, ("ffn", 2): 1, ("mix", 1): 2, ("mix", 2): 3}
    started = []

    def book(l, group, red):
        def update(after):
            names, own, got = red.names, *red.finish(after)
            for n, o, g in zip(names, own, got):
                big_out[n] = _adamw_big("adamw_" + n, l, c_idx, wh[n], mh[n], vh[n], o, g, big_out[n])
            return [big_out[names[-1]][0]] if l == 0 else []

        steps = (lambda a: [red.swapped(a)], lambda a: [red.scattered(a)], update)
        for k, ((hook, dl), step) in enumerate(zip(plan[group], steps)):
            if l + dl >= 0:
                agenda.setdefault((l + dl, hook), []).append(step)
            else:
                tail.append((tail_rank[group, k], step))

    def run_at(l):
        return lambda hook, after: [t for step in agenda.pop((l, hook), []) for t in step(after)]

    def start_at(l):
        def start(group, grads):
            red = _Reduce("%s_%d" % (group, l), grads, cm_idx)
            book(l, group, red)
            started.append(red.token)
            return red.token
        return start

    da, db, carry = dy, None, []
    for l in reversed(range(DEPTH)):
        agenda.setdefault((l, "h0"), []).append(lambda after, carry=carry: carry)
        da, db, small = _layer_bwd(da, db, {n: w[n][l] for n in SMALL}, gathered[l], saved[l], rope_t, run_at(l), start_at(l))
        for n in SMALL:
            small_g[n][l] = small[n].reshape(w[n].shape[1:])
        carry = run_at(l)("end", db) + started[-1:]
    shapes = {n: w[n].shape for n in SMALL}
    part = _pack({n: jnp.stack(small_g[n]) for n in SMALL})
    slots = lax.dynamic_update_slice(jnp.zeros((N_CHIPS, 2) + part.shape, F32), part[None, None], (chip, c_pos, 0, 0))
    small_sems, small_bufs, _ = _gather_start("gather_start_small", [slots], pair=True)
    after = None
    for _, step in sorted(tail, key=lambda rs: rs[0]):
        after = (step(after) or [after])[-1]
    grad_x = _ln_in_grad(da, db).reshape(x.shape)
    small_sems, small_bufs, _ = _gather_forward("gather_forward_small", small_bufs, small_sems, after, pair=True)
    small_bufs = _gather_finish("gather_finish_small", small_bufs, small_sems, grad_x)
    g_small = _sum_leading("small_grad_sum", small_bufs[0].reshape((N_DEV,) + part.shape))
    upd = _adamw("adamw_small", _pack(w), g_small, _pack(m), _pack(v))
    small_out = [_unpack(a, shapes) for a in (g_small,) + tuple(upd)]

    outs = [loss, grad_x]
    for kind in range(4):
        for n in ALL_W:
            if n in SMALL:
                outs.append(small_out[kind][n])
            else:
                o = _unhalved(n, big_out[n][kind])
                outs.append(jnp.swapaxes(o, 1, 2) if n == "ffn_w_up" else o)
    return tuple(outs)


def _ln_in_grad(dr1, dx_in):
    tm = _tile(SEQ, 512)

    def body(a_ref, b_ref, o_ref):
        o_ref[...] = DEEPNORM_ALPHA * a_ref[...] + b_ref[...]

    blk = pl.BlockSpec((tm, D_MODEL), lambda i: (i, 0))
    return pl.pallas_call(body, name="grad_x", grid=(SEQ // tm,), in_specs=[blk, blk], out_specs=blk,
                          out_shape=_sds((SEQ, D_MODEL)), compiler_params=_cparams(("parallel",)))(dr1, dx_in)
```

```python
import functools
import math

import jax
import jax.numpy as jnp
from jax import lax
from jax.experimental import pallas as pl
from jax.experimental.pallas import tpu as pltpu

F32 = jnp.float32
BF16 = jnp.bfloat16
MM_DTYPE = BF16

D_MODEL = 2048
SEQ = 2048
DEPTH = 4
D_FF = 5504
HEAD_DIM = 64
N_Q_HEADS = D_MODEL // 2 // HEAD_DIM
N_KV_HEADS = N_Q_HEADS // 4
ATTN_WIDTH = N_Q_HEADS * HEAD_DIM
KV_WIDTH = N_KV_HEADS * HEAD_DIM
ATTN_BLOCK = 128
ROPE_THETA = 10000.0
POOL_WINDOWS = (2, 4, 8, 16)
POOL_WIDTH = D_MODEL // 4
POOL_GROUP = POOL_WIDTH // len(POOL_WINDOWS)
SSM_WIDTH = D_MODEL // 4
SSM_GROUP = 16
SSM_N_GROUPS = SSM_WIDTH // SSM_GROUP
SSM_STATE = 64
SSM_CH = SSM_N_GROUPS * SSM_STATE
MIX_WIDTH = ATTN_WIDTH + POOL_WIDTH + SSM_WIDTH
IN_WIDTH = ATTN_WIDTH + 2 * KV_WIDTH + POOL_WIDTH + SSM_WIDTH
CONV_WIDTH = 3
LN_EPS = 1e-5
DEEPNORM_ALPHA = (2 * DEPTH) ** 0.25
ADAM_LR = 0.001
ADAM_B1 = 0.9
ADAM_B2 = 0.999
ADAM_EPS = 1e-08
ADAM_WD = 0.01
ADAM_STEP = 10

N_CHIPS = 4
N_DEV = 8
FS = 2 * D_FF // N_CHIPS
IN_S = IN_WIDTH // N_CHIPS
GLU_S = 2 * SSM_WIDTH // N_CHIPS
LANES = 128
SUBLANES = 8
SCAN_CW = 256
SCAN_NB = SSM_CH // SCAN_CW
VMEM_LIMIT = 56 * 1024 * 1024
COPY_BLOCK_BYTES = 6 * 1024 * 1024
NEG = -1e30

NN = (((1,), (0,)), ((), ()))
NT = (((1,), (1,)), ((), ()))
TN = (((0,), (0,)), ((), ()))
MESH = pl.DeviceIdType.MESH


def _tile(n, pref, mult=LANES):
    best = None
    for t in range(mult, min(n, pref) + 1, mult):
        if n % t == 0:
            best = t
    return n if best is None else best


def _cparams(sem):
    return pltpu.CompilerParams(dimension_semantics=sem, vmem_limit_bytes=VMEM_LIMIT)


def _sds(shape, dtype=F32):
    return jax.ShapeDtypeStruct(tuple(shape), dtype)


def _as_list(x):
    return [] if x is None else list(x) if isinstance(x, (list, tuple)) else [x]


def _mm(name, a, b, out_shape, grid, a_spec, b_spec, o_spec, dims, acc_shape, out_dtype=F32, dep=None):
    nk = grid[2]
    deps = _as_list(dep)

    def product(a_ref, b_ref):
        return lax.dot_general(a_ref[...].astype(MM_DTYPE), b_ref[...].astype(MM_DTYPE), dims, preferred_element_type=F32)

    def body_one(a_ref, b_ref, *rest):
        rest[-1][...] = product(a_ref, b_ref).astype(rest[-1].dtype)

    def body(a_ref, b_ref, *rest):
        o_ref, acc_ref = rest[-2:]
        k = pl.program_id(2)

        @pl.when(k == 0)
        def _():
            acc_ref[...] = product(a_ref, b_ref)

        @pl.when(k > 0)
        def _():
            acc_ref[...] += product(a_ref, b_ref)

        @pl.when(k == nk - 1)
        def _():
            o_ref[...] = acc_ref[...].astype(o_ref.dtype)

    return pl.pallas_call(
        body_one if nk == 1 else body, name=name, grid=grid, in_specs=[a_spec, b_spec] + [ANY] * len(deps),
        out_specs=o_spec, out_shape=_sds(out_shape, out_dtype),
        scratch_shapes=[] if nk == 1 else [pltpu.VMEM(acc_shape, F32)],
        compiler_params=_cparams(("parallel", "parallel", "arbitrary")))(a, b, *deps)


def _mm_nn(name, a, b, tm=2048, tn=512, tk=2048, out_dtype=F32):
    m, kk = a.shape
    n = b.shape[1]
    tm, tn, tk = _tile(m, tm), _tile(n, tn), _tile(kk, tk)
    return _mm(name, a, b, (m, n), (m // tm, n // tn, kk // tk),
               pl.BlockSpec((tm, tk), lambda i, j, k: (i, k)), pl.BlockSpec((tk, tn), lambda i, j, k: (k, j)),
               pl.BlockSpec((tm, tn), lambda i, j, k: (i, j)), NN, (tm, tn), out_dtype)


def _mm_nt(name, a, b, tm=2048, tn=512, tk=2048, dep=None):
    m, kk = a.shape
    n = b.shape[0]
    tm, tn, tk = _tile(m, tm), _tile(n, tn), _tile(kk, tk)
    return _mm(name, a, b, (m, n), (m // tm, n // tn, kk // tk),
               pl.BlockSpec((tm, tk), lambda i, j, k: (i, k)), pl.BlockSpec((tn, tk), lambda i, j, k: (j, k)),
               pl.BlockSpec((tm, tn), lambda i, j, k: (i, j)), NT, (tm, tn), dep=dep)


def _mm_tn(name, a, b, tm=1024, tn=1024, ts=2048):
    s, m = a.shape
    n = b.shape[1]
    tm, tn, ts = _tile(m, tm), _tile(n, tn), _tile(s, ts)
    return _mm(name, a, b, (m, n), (m // tm, n // tn, s // ts),
               pl.BlockSpec((ts, tm), lambda i, j, k: (k, i)), pl.BlockSpec((ts, tn), lambda i, j, k: (k, j)),
               pl.BlockSpec((tm, tn), lambda i, j, k: (i, j)), TN, (tm, tn))


def _mm_shard_cols(name, a, w, tm=2048, tk=2048, dep=None):
    m, kk = a.shape
    nj, _, c = w.shape
    tm, tk = _tile(m, tm), _tile(kk, tk)
    return _mm(name, a, w, (m, nj * c), (m // tm, nj, kk // tk),
               pl.BlockSpec((tm, tk), lambda i, j, k: (i, k)), pl.BlockSpec((None, tk, c), lambda i, j, k: (j, k, 0)),
               pl.BlockSpec((tm, c), lambda i, j, k: (i, j)), NN, (tm, c), dep=dep)


def _mm_shard_cols_nt(name, d, w, tm=2048, tn=512, dep=None):
    m = d.shape[0]
    nj, n, c = w.shape
    tm, tn = _tile(m, tm), _tile(n, tn)
    return _mm(name, d, w, (m, n), (m // tm, n // tn, nj),
               pl.BlockSpec((tm, c), lambda i, j, k: (i, k)), pl.BlockSpec((None, tn, c), lambda i, j, k: (k, j, 0)),
               pl.BlockSpec((tm, tn), lambda i, j, k: (i, j)), NT, (tm, tn), dep=dep)


def _mm_shard_cols_tn(name, a, d, nj, tm=1024, ts=2048):
    s, m = a.shape
    c = d.shape[1] // nj
    tm, ts = _tile(m, tm), _tile(s, ts)
    return _mm(name, a, d, (nj, m, c), (nj, m // tm, s // ts),
               pl.BlockSpec((ts, tm), lambda j, i, k: (k, i)), pl.BlockSpec((ts, c), lambda j, i, k: (k, j)),
               pl.BlockSpec((None, tm, c), lambda j, i, k: (j, i, 0)), TN, (tm, c))


def _ffn_up(x1, w_up_t, tm=512, tk=2048, dep=None):
    s, d = x1.shape
    tm, tk = _tile(s, tm), _tile(d, tk)
    return _mm("ffn_up", x1, w_up_t, (N_CHIPS, s, FS), (N_CHIPS, s // tm, d // tk),
               pl.BlockSpec((tm, tk), lambda j, i, k: (i, k)), pl.BlockSpec((None, FS, tk), lambda j, i, k: (j, 0, k)),
               pl.BlockSpec((None, tm, FS), lambda j, i, k: (j, i, 0)), NT, (tm, FS), dep=dep)


def _ffn_down(act, w_down, tm=1024, tn=512):
    _, s, _ = act.shape
    d = w_down.shape[2]
    tm, tn = _tile(s, tm), _tile(d, tn)
    return _mm("ffn_down", act, w_down, (s, d), (s // tm, d // tn, 2),
               pl.BlockSpec((None, tm, FS), lambda i, j, k: (k, i, 0)), pl.BlockSpec((None, FS, tn), lambda i, j, k: (k, 0, j)),
               pl.BlockSpec((tm, tn), lambda i, j, k: (i, j)), NN, (tm, tn))


def _ffn_down_dact(df, w_down, tm=512, tk=2048):
    s, d = df.shape
    tm, tk = _tile(s, tm), _tile(d, tk)
    return _mm("ffn_down_dact", df, w_down, (2, s, FS), (2, s // tm, d // tk),
               pl.BlockSpec((tm, tk), lambda j, i, k: (i, k)), pl.BlockSpec((None, FS, tk), lambda j, i, k: (j, 0, k)),
               pl.BlockSpec((None, tm, FS), lambda j, i, k: (j, i, 0)), NT, (tm, FS))


def _ffn_down_dw(act, df, tn=512, ts=2048):
    _, s, _ = act.shape
    d = df.shape[1]
    tn, ts = _tile(d, tn), _tile(s, ts)
    return _mm("ffn_down_dw", act, df, (2, FS, d), (2, d // tn, s // ts),
               pl.BlockSpec((None, ts, FS), lambda p, j, k: (p, k, 0)), pl.BlockSpec((ts, tn), lambda p, j, k: (k, j)),
               pl.BlockSpec((None, FS, tn), lambda p, j, k: (p, 0, j)), TN, (FS, tn))


def _ffn_up_dx(dh, w_up_t, tm=1024, tn=1024, dep=None):
    s = dh.shape[2]
    d = w_up_t.shape[2]
    tm, tn = _tile(s, tm), _tile(d, tn)
    return _mm("ffn_up_dx", dh, w_up_t, (s, d), (s // tm, d // tn, N_CHIPS),
               pl.BlockSpec((None, None, tm, FS), lambda i, j, k: (k % 2, k // 2, i, 0)),
               pl.BlockSpec((None, FS, tn), lambda i, j, k: (k, 0, j)),
               pl.BlockSpec((tm, tn), lambda i, j, k: (i, j)), NN, (tm, tn), dep=dep)


def _ffn_up_dw(x1, dh, tn=512, ts=2048):
    s, d = x1.shape
    tn, ts = _tile(d, tn), _tile(s, ts)
    return _mm("ffn_up_dw", dh, x1, (N_CHIPS, FS, d), (N_CHIPS, d // tn, s // ts),
               pl.BlockSpec((None, None, ts, FS), lambda j, i, k: (j % 2, j // 2, k, 0)),
               pl.BlockSpec((ts, tn), lambda j, i, k: (k, i)),
               pl.BlockSpec((None, FS, tn), lambda j, i, k: (j, 0, i)), TN, (FS, tn))


def _rope_tables():
    half = HEAD_DIM // 2
    inv = ROPE_THETA ** (-jnp.arange(half, dtype=F32) / half)
    ang = jnp.arange(SEQ).astype(F32)[:, None] * inv[None, :]
    cos, sin = jnp.cos(ang), jnp.sin(ang)
    cos_t = jnp.tile(cos, (1, LANES // half))
    sin_t = jnp.tile(jnp.concatenate([-sin, sin], axis=1), (1, LANES // HEAD_DIM))
    return cos_t, sin_t


def _rotate_half(t):
    lane = lax.broadcasted_iota(jnp.int32, t.shape, 1)
    first = (lane % HEAD_DIM) < (HEAD_DIM // 2)
    return jnp.where(first, pltpu.roll(t, LANES - HEAD_DIM // 2, 1), pltpu.roll(t, HEAD_DIM // 2, 1))


def _rope(name, src, col_tile0, n_tiles, cos_t, sin_t, out_dtype):
    tm = _tile(SEQ, 512)

    def body(x_ref, c_ref, s_ref, o_ref):
        t = x_ref[...].astype(F32)
        o_ref[...] = (t * c_ref[...] + _rotate_half(t) * s_ref[...]).astype(o_ref.dtype)

    return pl.pallas_call(
        body, name=name, grid=(SEQ // tm, n_tiles),
        in_specs=[pl.BlockSpec((tm, LANES), lambda i, j: (i, j + col_tile0)),
                  pl.BlockSpec((tm, LANES), lambda i, j: (i, 0)), pl.BlockSpec((tm, LANES), lambda i, j: (i, 0))],
        out_specs=pl.BlockSpec((tm, LANES), lambda i, j: (i, j)),
        out_shape=_sds((SEQ, n_tiles * LANES), out_dtype),
        compiler_params=_cparams(("parallel", "parallel")))(src, cos_t, sin_t)


Q_TILES = ATTN_WIDTH // LANES
KV_TILES = KV_WIDTH // LANES
Q_PER_KV_TILE = Q_TILES // KV_TILES
HEADS_PER_KV_TILE = N_Q_HEADS // KV_TILES
K_TILE0 = ATTN_WIDTH // LANES
V_TILE0 = (ATTN_WIDTH + KV_WIDTH) // LANES
N_QBLK = SEQ // ATTN_BLOCK


def _dup_half(t, which):
    lane = lax.broadcasted_iota(jnp.int32, t.shape, 1)
    r = pltpu.roll(t, HEAD_DIM, 1)
    lo = lane < HEAD_DIM
    return jnp.where(lo, t, r) if which == 0 else jnp.where(lo, r, t)


def _attn_masks(n):
    row = lax.broadcasted_iota(jnp.int32, (ATTN_BLOCK, ATTN_BLOCK), 0)
    col = lax.broadcasted_iota(jnp.int32, (ATTN_BLOCK, ATTN_BLOCK), 1)
    return col <= row, (col > row) & (n > 0), col < HEAD_DIM


def _attn_probs(qm, k2c, k2p, cur_ok, prev_ok, sink):
    scale = HEAD_DIM ** -0.5
    sc = lax.dot_general(qm, k2c, NT, preferred_element_type=F32) * scale
    sp = lax.dot_general(qm, k2p, NT, preferred_element_type=F32) * scale
    sc = jnp.where(cur_ok, sc, NEG)
    sp = jnp.where(prev_ok, sp, NEG)
    m = jnp.maximum(jnp.maximum(sc.max(1, keepdims=True), sp.max(1, keepdims=True)), sink)
    pc, pp = jnp.exp(sc - m), jnp.exp(sp - m)
    esink = jnp.exp(sink - m)
    inv = 1.0 / (pc.sum(1, keepdims=True) + pp.sum(1, keepdims=True) + esink)
    return pc * inv, pp * inv, esink * inv


def _attn_specs():
    blk = (ATTN_BLOCK, LANES)
    wide = (ATTN_BLOCK, Q_PER_KV_TILE * LANES)
    prev = lambda n: jnp.maximum(n - 1, 0)
    q_spec = pl.BlockSpec(wide, lambda t, n: (n, t))
    kc = pl.BlockSpec(blk, lambda t, n: (n, K_TILE0 + t))
    kp = pl.BlockSpec(blk, lambda t, n: (prev(n), K_TILE0 + t))
    vc = pl.BlockSpec(blk, lambda t, n: (n, V_TILE0 + t))
    vp = pl.BlockSpec(blk, lambda t, n: (prev(n), V_TILE0 + t))
    return q_spec, kc, kp, vc, vp, pl.BlockSpec(memory_space=pltpu.SMEM)


def _attn_fwd(qk, h, sinks):
    q_spec, kc_s, kp_s, vc_s, vp_s, smem = _attn_specs()

    def body(sink_ref, q_ref, kc_ref, kp_ref, vc_ref, vp_ref, o_ref, ob_ref):
        t, n = pl.program_id(0), pl.program_id(1)
        cur_ok, prev_ok, lo = _attn_masks(n)
        kc, kp = kc_ref[...].astype(F32), kp_ref[...].astype(F32)
        vc, vp = vc_ref[...], vp_ref[...]
        for kvl in range(2):
            k2c, k2p = _dup_half(kc, kvl).astype(MM_DTYPE), _dup_half(kp, kvl).astype(MM_DTYPE)
            v2c, v2p = _dup_half(vc, kvl).astype(MM_DTYPE), _dup_half(vp, kvl).astype(MM_DTYPE)
            for a in (2 * kvl, 2 * kvl + 1):
                qt = q_ref[:, a * LANES:(a + 1) * LANES].astype(F32)
                outs = []
                for hs in range(2):
                    qm = jnp.where(lo == (hs == 0), qt, 0.0).astype(MM_DTYPE)
                    sink = sink_ref[t * HEADS_PER_KV_TILE + 2 * a + hs]
                    pc, pp, _ = _attn_probs(qm, k2c, k2p, cur_ok, prev_ok, sink)
                    outs.append(lax.dot_general(pc.astype(MM_DTYPE), v2c, NN, preferred_element_type=F32)
                                + lax.dot_general(pp.astype(MM_DTYPE), v2p, NN, preferred_element_type=F32))
                o = jnp.where(lo, outs[0], outs[1])
                o_ref[:, a * LANES:(a + 1) * LANES] = o
                ob_ref[:, a * LANES:(a + 1) * LANES] = o.astype(ob_ref.dtype)

    return pl.pallas_call(
        body, name="attn_fwd", grid=(KV_TILES, N_QBLK),
        in_specs=[smem, q_spec, kc_s, kp_s, vc_s, vp_s], out_specs=[q_spec, q_spec],
        out_shape=[_sds((SEQ, ATTN_WIDTH)), _sds((SEQ, ATTN_WIDTH), MM_DTYPE)],
        compiler_params=_cparams(("parallel", "parallel")))(sinks, qk, qk, qk, h, h)


def _attn_bwd(qk, h, sinks, y, dy, dy_tile0, dep=None):
    deps = _as_list(dep)
    q_spec, kc_s, kp_s, vc_s, vp_s, smem = _attn_specs()
    blk = (ATTN_BLOCK, LANES)
    wide = (ATTN_BLOCK, Q_PER_KV_TILE * LANES)
    kv_out = pl.BlockSpec(blk, lambda t, n: (n, t))
    dy_spec = pl.BlockSpec(wide, lambda t, n: (n, t + dy_tile0))

    def body(sink_ref, q_ref, kc_ref, kp_ref, vc_ref, vp_ref, y_ref, dy_ref, *rest):
        dq_ref, dkc_ref, dkp_ref, dvc_ref, dvp_ref, dsk_ref = rest[-6:]
        t, n = pl.program_id(0), pl.program_id(1)
        cur_ok, prev_ok, lo = _attn_masks(n)
        scale = HEAD_DIM ** -0.5
        kc, kp = kc_ref[...].astype(F32), kp_ref[...].astype(F32)
        vc, vp = vc_ref[...], vp_ref[...]
        hrow = lax.broadcasted_iota(jnp.int32, (HEADS_PER_KV_TILE, LANES), 0)
        dsk = jnp.zeros((HEADS_PER_KV_TILE, LANES), F32)
        folded = []
        for kvl in range(2):
            k2c, k2p = _dup_half(kc, kvl).astype(MM_DTYPE), _dup_half(kp, kvl).astype(MM_DTYPE)
            v2c, v2p = _dup_half(vc, kvl).astype(MM_DTYPE), _dup_half(vp, kvl).astype(MM_DTYPE)
            acc = [jnp.zeros(blk, F32) for _ in range(4)]
            for a in (2 * kvl, 2 * kvl + 1):
                sl = slice(a * LANES, (a + 1) * LANES)
                qt = q_ref[:, sl].astype(F32)
                dot_, yt = dy_ref[:, sl], y_ref[:, sl]
                dqs = []
                for hs in range(2):
                    hm = lo == (hs == 0)
                    qm = jnp.where(hm, qt, 0.0).astype(MM_DTYPE)
                    dom = jnp.where(hm, dot_, 0.0).astype(MM_DTYPE)
                    hl = 2 * a + hs
                    sink = sink_ref[t * HEADS_PER_KV_TILE + hl]
                    pc, pp, psink = _attn_probs(qm, k2c, k2p, cur_ok, prev_ok, sink)
                    delta = jnp.sum(jnp.where(hm, dot_ * yt, 0.0), axis=1, keepdims=True)
                    dpc = lax.dot_general(dom, v2c, NT, preferred_element_type=F32)
                    dpp = lax.dot_general(dom, v2p, NT, preferred_element_type=F32)
                    dsc = (pc * (dpc - delta) * scale).astype(MM_DTYPE)
                    dsp = (pp * (dpp - delta) * scale).astype(MM_DTYPE)
                    dqs.append(lax.dot_general(dsc, k2c, NN, preferred_element_type=F32)
                               + lax.dot_general(dsp, k2p, NN, preferred_element_type=F32))
                    acc[0] += lax.dot_general(dsc, qm, TN, preferred_element_type=F32)
                    acc[1] += lax.dot_general(dsp, qm, TN, preferred_element_type=F32)
                    acc[2] += lax.dot_general(pc.astype(MM_DTYPE), dom, TN, preferred_element_type=F32)
                    acc[3] += lax.dot_general(pp.astype(MM_DTYPE), dom, TN, preferred_element_type=F32)
                    dsk = dsk + jnp.where(hrow == hl, -jnp.sum(psink * delta), 0.0)
                dq_ref[:, sl] = jnp.where(lo, dqs[0], dqs[1])
            folded.append([x + pltpu.roll(x, HEAD_DIM, 1) for x in acc])
        for o_ref, i in ((dkc_ref, 0), (dkp_ref, 1), (dvc_ref, 2), (dvp_ref, 3)):
            o_ref[...] = jnp.where(lo, folded[0][i], folded[1][i])

        @pl.when(n == 0)
        def _():
            dsk_ref[...] = jnp.zeros_like(dsk_ref)

        dsk_ref[...] += dsk

    kv_shape = _sds((SEQ, KV_WIDTH))
    return pl.pallas_call(
        body, name="attn_bwd", grid=(KV_TILES, N_QBLK),
        in_specs=[smem, q_spec, kc_s, kp_s, vc_s, vp_s, q_spec, dy_spec] + [ANY] * len(deps),
        out_specs=[q_spec, kv_out, kv_out, kv_out, kv_out,
                   pl.BlockSpec((None, HEADS_PER_KV_TILE, LANES), lambda t, n: (t, 0, 0))],
        out_shape=[_sds((SEQ, ATTN_WIDTH)), kv_shape, kv_shape, kv_shape, kv_shape,
                   _sds((KV_TILES, HEADS_PER_KV_TILE, LANES))],
        compiler_params=_cparams(("parallel", "arbitrary")))(sinks, qk, qk, qk, h, h, y, dy, *deps)


def _attn_dh(dq, dkc, dkp, dvc, dvp, cos_t, nsin_t):
    n_tiles = Q_TILES + 2 * KV_TILES
    nxt = lambda n: jnp.minimum(n + 1, N_QBLK - 1)

    def body(dq_ref, kc_ref, kp_ref, vc_ref, vp_ref, c_ref, s_ref, o_ref):
        has_next = pl.program_id(0) < N_QBLK - 1
        cos, sin = c_ref[...], s_ref[...]

        def unrope(t):
            return t * cos + _rotate_half(t) * sin

        for j in range(Q_TILES):
            sl = slice(j * LANES, (j + 1) * LANES)
            o_ref[:, sl] = unrope(dq_ref[:, sl]).astype(o_ref.dtype)
        for j in range(KV_TILES):
            sl = slice(j * LANES, (j + 1) * LANES)
            t = kc_ref[:, sl] + jnp.where(has_next, kp_ref[:, sl], 0.0)
            o_ref[:, ATTN_WIDTH + j * LANES:ATTN_WIDTH + (j + 1) * LANES] = unrope(t).astype(o_ref.dtype)
        o_ref[:, ATTN_WIDTH + KV_WIDTH:] = (vc_ref[...] + jnp.where(has_next, vp_ref[...], 0.0)).astype(o_ref.dtype)

    qb, kb, tb = (ATTN_BLOCK, ATTN_WIDTH), (ATTN_BLOCK, KV_WIDTH), (ATTN_BLOCK, LANES)
    return pl.pallas_call(
        body, name="attn_dh", grid=(N_QBLK,),
        in_specs=[pl.BlockSpec(qb, lambda n: (n, 0)),
                  pl.BlockSpec(kb, lambda n: (n, 0)), pl.BlockSpec(kb, lambda n: (nxt(n), 0)),
                  pl.BlockSpec(kb, lambda n: (n, 0)), pl.BlockSpec(kb, lambda n: (nxt(n), 0)),
                  pl.BlockSpec(tb, lambda n: (n, 0)), pl.BlockSpec(tb, lambda n: (n, 0))],
        out_specs=pl.BlockSpec((ATTN_BLOCK, n_tiles * LANES), lambda n: (n, 0)),
        out_shape=_sds((SEQ, n_tiles * LANES), MM_DTYPE),
        compiler_params=_cparams(("parallel",)))(dq, dkc, dkp, dvc, dvp, cos_t, nsin_t)


POOL_TILE0 = (ATTN_WIDTH + 2 * KV_WIDTH) // POOL_WIDTH


def _shift_rows(x, d, down):
    n = x.shape[0]
    row = lax.broadcasted_iota(jnp.int32, x.shape, 0)
    if down:
        return jnp.where(row >= d, pltpu.roll(x, d, 0), 0.0)
    return jnp.where(row < n - d, pltpu.roll(x, n - d, 0), 0.0)


def _window_sum(x, w, down):
    d = 1
    while d < w:
        x = x + _shift_rows(x, d, down)
        d *= 2
    return x


def _pool_z(u, w):
    t = lax.broadcasted_iota(jnp.int32, u.shape, 0).astype(F32)
    cnt = jnp.minimum(t + 1.0, float(w))
    return _window_sum(u, w, True) / cnt - u, cnt


def _pool_fwd(h, pool_w, pool_scale):
    def body(u_ref, w_ref, s_ref, o_ref):
        for gi, w in enumerate(POOL_WINDOWS):
            sl = slice(gi * POOL_GROUP, (gi + 1) * POOL_GROUP)
            z, _ = _pool_z(u_ref[:, sl], w)
            o_ref[:, sl] = (lax.dot_general(z.astype(MM_DTYPE), w_ref[gi].astype(MM_DTYPE), NN,
                                            preferred_element_type=F32) * s_ref[:, sl]).astype(o_ref.dtype)

    return pl.pallas_call(
        body, name="pool_fwd", grid=(1,),
        in_specs=[pl.BlockSpec((SEQ, POOL_WIDTH), lambda i: (0, POOL_TILE0)),
                  pl.BlockSpec(pool_w.shape, lambda i: (0, 0, 0)), pl.BlockSpec((1, POOL_WIDTH), lambda i: (0, 0))],
        out_specs=pl.BlockSpec((SEQ, POOL_WIDTH), lambda i: (0, 0)),
        out_shape=_sds((SEQ, POOL_WIDTH), MM_DTYPE), compiler_params=_cparams(("arbitrary",)))(h, pool_w, pool_scale)


def _pool_bwd(h, pool_w, pool_scale, dmix, dy_tile0):
    def body(u_ref, w_ref, s_ref, dy_ref, du_ref, dw_ref, ds_ref):
        for gi, w in enumerate(POOL_WINDOWS):
            sl = slice(gi * POOL_GROUP, (gi + 1) * POOL_GROUP)
            z, cnt = _pool_z(u_ref[:, sl], w)
            zb, wb = z.astype(MM_DTYPE), w_ref[gi].astype(MM_DTYPE)
            dy = dy_ref[:, sl]
            zp = lax.dot_general(zb, wb, NN, preferred_element_type=F32)
            ds_ref[:, sl] = jnp.sum(dy * zp, axis=0, keepdims=True)
            dyo = (dy * s_ref[:, sl]).astype(MM_DTYPE)
            dw_ref[gi] = lax.dot_general(zb, dyo, TN, preferred_element_type=F32)
            dz = lax.dot_general(dyo, wb, NT, preferred_element_type=F32)
            du_ref[:, sl] = (_window_sum(dz / cnt, w, False) - dz).astype(du_ref.dtype)

    return pl.pallas_call(
        body, name="pool_bwd", grid=(1,),
        in_specs=[pl.BlockSpec((SEQ, POOL_WIDTH), lambda i: (0, POOL_TILE0)),
                  pl.BlockSpec(pool_w.shape, lambda i: (0, 0, 0)), pl.BlockSpec((1, POOL_WIDTH), lambda i: (0, 0)),
                  pl.BlockSpec((SEQ, POOL_WIDTH), lambda i: (0, dy_tile0))],
        out_specs=[pl.BlockSpec((SEQ, POOL_WIDTH), lambda i: (0, 0)), pl.BlockSpec(pool_w.shape, lambda i: (0, 0, 0)),
                   pl.BlockSpec((1, POOL_WIDTH), lambda i: (0, 0))],
        out_shape=[_sds((SEQ, POOL_WIDTH), MM_DTYPE), _sds(pool_w.shape), _sds((1, POOL_WIDTH))],
        compiler_params=_cparams(("arbitrary",)))(h, pool_w, pool_scale, dmix)


def _ssm_discretize(lr, li, ldt, br, bi):
    dt = jnp.exp(ldt)
    mag = jnp.exp(lr * dt)
    ar, ai = mag * jnp.cos(li * dt), mag * jnp.sin(li * dt)
    nr, ni = ar - 1.0, ai
    den = lr * lr + li * li
    zr = (nr * lr + ni * li) / den
    zi = (ni * lr - nr * li) / den
    return ar, ai, zr * br - zi * bi, zr * bi + zi * br


def _ssm_prep(lr, li, ldt, br, bi):
    def body(lr_ref, li_ref, ldt_ref, br_ref, bi_ref, ar_ref, ai_ref, bbr_ref, bbi_ref):
        outs = _ssm_discretize(lr_ref[...], li_ref[...], ldt_ref[...], br_ref[...], bi_ref[...])
        for o, v in zip((ar_ref, ai_ref, bbr_ref, bbi_ref), outs):
            o[...] = v

    row, mat = _sds((1, SSM_CH)), _sds((SSM_GROUP, SSM_CH))
    return pl.pallas_call(body, name="ssm_prep", out_shape=[row, row, mat, mat])(lr, li, ldt, br, bi)


def _ssm_prep_bwd(lr, li, ldt, br, bi, dar8, dai8, dbbr, dbbi):
    def body(lr_ref, li_ref, ldt_ref, br_ref, bi_ref, dar_ref, dai_ref, dbbr_ref, dbbi_ref, *outs):
        args = (lr_ref[...], li_ref[...], ldt_ref[...], br_ref[...], bi_ref[...])
        _, vjp = jax.vjp(_ssm_discretize, *args)
        cot = (jnp.sum(dar_ref[...], axis=0, keepdims=True), jnp.sum(dai_ref[...], axis=0, keepdims=True),
               dbbr_ref[...], dbbi_ref[...])
        for o, v in zip(outs, vjp(cot)):
            o[...] = v

    row, mat = _sds((1, SSM_CH)), _sds((SSM_GROUP, SSM_CH))
    return pl.pallas_call(body, name="ssm_prep_bwd", out_shape=[row, row, row, mat, mat])(
        lr, li, ldt, br, bi, dar8, dai8, dbbr, dbbi)


def _ssm_diag(name, full):
    tiles = SCAN_CW // LANES
    groups = LANES // SSM_STATE
    col = lambda t, part: (t // tiles) * 2 * tiles + part * tiles + t % tiles

    def body(x_ref, o_ref):
        lane = lax.broadcasted_iota(jnp.int32, (SSM_GROUP, LANES), 1)
        out = x_ref[0:SSM_GROUP, :]
        for k in range(1, groups):
            out = jnp.where(lane >= k * SSM_STATE, x_ref[k * SSM_GROUP:(k + 1) * SSM_GROUP, :], out)
        o_ref[...] = out

    return pl.pallas_call(
        body, name=name, grid=(SSM_CH // LANES, 2),
        in_specs=[pl.BlockSpec((groups * SSM_GROUP, LANES), lambda t, part: (t, col(t, part)))],
        out_specs=pl.BlockSpec((SSM_GROUP, LANES), lambda t, part: (0, col(t, part))),
        out_shape=_sds((SSM_GROUP, 2 * SSM_CH)), compiler_params=_cparams(("parallel", "parallel")))(full)


def _scan_layout(re, im):
    r = re.shape[0]
    return jnp.stack([re.reshape(r, SCAN_NB, SCAN_CW), im.reshape(r, SCAN_NB, SCAN_CW)], axis=2).reshape(r, 2 * SSM_CH)


def _scan_unlayout(x):
    r = x.shape[0]
    x = x.reshape(r, SCAN_NB, 2, SCAN_CW)
    return x[:, :, 0].reshape(r, SSM_CH), x[:, :, 1].reshape(r, SSM_CH)


def _time_permute(u):
    s, c = u.shape
    return u.reshape(SUBLANES, s // SUBLANES, c).transpose(1, 0, 2).reshape(s, c)


def _time_unpermute(u):
    s, c = u.shape
    return u.reshape(s // SUBLANES, SUBLANES, c).transpose(1, 0, 2).reshape(s, c)


def _ssm_scan(name, a_vec, x, reverse, s_prev=None):
    nsteps = SEQ // SUBLANES
    cw = SCAN_CW
    with_da = s_prev is not None

    def body(a_ref, x_ref, *rest):
        if with_da:
            s_ref, o_ref, da_ref = rest
        else:
            o_ref, = rest
        ar = jnp.broadcast_to(a_ref[:, :cw], (SUBLANES, cw))
        ai = jnp.broadcast_to(a_ref[:, cw:], (SUBLANES, cw))
        seg = lax.broadcasted_iota(jnp.int32, (SUBLANES, cw), 0)

        def toward(v):
            if reverse:
                return jnp.where(seg < SUBLANES - 1, pltpu.roll(v, SUBLANES - 1, 0), 0.0)
            return jnp.where(seg >= 1, pltpu.roll(v, 1, 0), 0.0)

        def rows(j):
            jj = nsteps - 1 - j if reverse else j
            return pl.ds(pl.multiple_of(jj * SUBLANES, SUBLANES), SUBLANES)

        def cmul(pr, pi, qr, qi):
            return pr * qr - pi * qi, pr * qi + pi * qr

        def local(j, c):
            sr, si = c
            r = rows(j)
            mr, mi = cmul(ar, ai, sr, si)
            return mr + x_ref[r, :cw], mi + x_ref[r, cw:]

        zero = jnp.zeros((SUBLANES, cw), F32)
        fr, fi = lax.fori_loop(0, nsteps, local, (zero, zero))

        def power(_, c):
            return cmul(ar, ai, *c)

        pr, pi = lax.fori_loop(0, nsteps - 1, power, (ar, ai))
        tr, ti = fr, fi
        for _ in range(SUBLANES - 1):
            mr, mi = cmul(pr, pi, toward(tr), toward(ti))
            tr, ti = fr + mr, fi + mi
        init = (toward(tr), toward(ti))

        def full(j, c):
            r = rows(j)
            if with_da:
                sr, si, dar, dai = c
            else:
                sr, si = c
            mr, mi = cmul(ar, ai, sr, si)
            sr, si = mr + x_ref[r, :cw], mi + x_ref[r, cw:]
            o_ref[r, :cw] = sr
            o_ref[r, cw:] = si
            if not with_da:
                return sr, si
            jj = nsteps - 1 - j
            rp = pl.ds(pl.multiple_of(jnp.maximum(jj - 1, 0) * SUBLANES, SUBLANES), SUBLANES)
            last = pl.ds((nsteps - 1) * SUBLANES, SUBLANES)
            first = jj == 0
            spr = jnp.where(first, jnp.where(seg >= 1, pltpu.roll(s_ref[last, :cw], 1, 0), 0.0), s_ref[rp, :cw])
            spi = jnp.where(first, jnp.where(seg >= 1, pltpu.roll(s_ref[last, cw:], 1, 0), 0.0), s_ref[rp, cw:])
            return sr, si, dar + sr * spr + si * spi, dai + si * spr - sr * spi

        if with_da:
            _, _, dar, dai = lax.fori_loop(0, nsteps, full, init + (zero, zero))
            da_ref[:, :cw] = dar
            da_ref[:, cw:] = dai
        else:
            lax.fori_loop(0, nsteps, full, init)

    blk = pl.BlockSpec((SEQ, 2 * cw), lambda b: (0, b))
    a_spec = pl.BlockSpec((1, 2 * cw), lambda b: (0, b))
    in_specs, args = [a_spec, blk], [a_vec, x]
    out_specs, out_shape = blk, _sds((SEQ, 2 * SSM_CH))
    if with_da:
        in_specs, args = in_specs + [blk], args + [s_prev]
        out_specs = [blk, pl.BlockSpec((SUBLANES, 2 * cw), lambda b: (0, b))]
        out_shape = [out_shape, _sds((SUBLANES, 2 * SSM_CH))]
    return pl.pallas_call(body, name=name, grid=(SCAN_NB,), in_specs=in_specs, out_specs=out_specs,
                          out_shape=out_shape, compiler_params=_cparams(("parallel",)))(*args)


def _ssm_gelu(yp, up, dvec):
    tm = _tile(SEQ, 512)

    def body(y_ref, u_ref, d_ref, yf_ref, g_ref):
        yf = y_ref[...] + d_ref[...] * u_ref[...]
        yf_ref[...] = yf
        g_ref[...] = jax.nn.gelu(yf).astype(g_ref.dtype)

    blk = pl.BlockSpec((tm, SSM_WIDTH), lambda i: (i, 0))
    row = pl.BlockSpec((1, SSM_WIDTH), lambda i: (0, 0))
    return pl.pallas_call(body, name="ssm_gelu", grid=(SEQ // tm,), in_specs=[blk, blk, row], out_specs=[blk, blk],
                          out_shape=[_sds((SEQ, SSM_WIDTH)), _sds((SEQ, SSM_WIDTH), MM_DTYPE)],
                          compiler_params=_cparams(("parallel",)))(yp, up, dvec)


def _ssm_gelu_bwd(yf, dgy, up, dvec):
    tm = _tile(SEQ, 512)

    def body(yf_ref, dg_ref, u_ref, d_ref, dyf_ref, du_ref, dd_ref):
        _, vjp = jax.vjp(jax.nn.gelu, yf_ref[...])
        dyf, = vjp(dg_ref[...])
        dyf_ref[...] = dyf.astype(dyf_ref.dtype)
        du_ref[...] = d_ref[...] * dyf

        @pl.when(pl.program_id(0) == 0)
        def _():
            dd_ref[...] = jnp.zeros_like(dd_ref)

        dd_ref[...] += jnp.sum(dyf * u_ref[...], axis=0, keepdims=True)

    blk = pl.BlockSpec((tm, SSM_WIDTH), lambda i: (i, 0))
    row = pl.BlockSpec((1, SSM_WIDTH), lambda i: (0, 0))
    return pl.pallas_call(body, name="ssm_gelu_bwd", grid=(SEQ // tm,), in_specs=[blk, blk, blk, row],
                          out_specs=[blk, blk, row],
                          out_shape=[_sds((SEQ, SSM_WIDTH), MM_DTYPE), _sds((SEQ, SSM_WIDTH)), _sds((1, SSM_WIDTH))],
                          compiler_params=_cparams(("arbitrary",)))(yf, dgy, up, dvec)


def _glu(ab):
    return ab[:, :SSM_WIDTH] * jax.nn.sigmoid(ab[:, SSM_WIDTH:])


def _ssm_glu(ab):
    tm = _tile(SEQ, 512)

    def body(ab_ref, o_ref):
        o_ref[...] = _glu(ab_ref[...]).astype(o_ref.dtype)

    return pl.pallas_call(body, name="ssm_glu", grid=(SEQ // tm,),
                          in_specs=[pl.BlockSpec((tm, 2 * SSM_WIDTH), lambda i: (i, 0))],
                          out_specs=pl.BlockSpec((tm, SSM_WIDTH), lambda i: (i, 0)),
                          out_shape=_sds((SEQ, SSM_WIDTH), MM_DTYPE), compiler_params=_cparams(("parallel",)))(ab)


def _ssm_glu_bwd(ab, dout):
    tm = _tile(SEQ, 512)

    def body(ab_ref, do_ref, dab_ref):
        _, vjp = jax.vjp(_glu, ab_ref[...])
        dab, = vjp(do_ref[...])
        dab_ref[...] = dab.astype(dab_ref.dtype)

    return pl.pallas_call(body, name="ssm_glu_bwd", grid=(SEQ // tm,),
                          in_specs=[pl.BlockSpec((tm, 2 * SSM_WIDTH), lambda i: (i, 0)),
                                    pl.BlockSpec((tm, SSM_WIDTH), lambda i: (i, 0))],
                          out_specs=pl.BlockSpec((tm, 2 * SSM_WIDTH), lambda i: (i, 0)),
                          out_shape=_sds((SEQ, 2 * SSM_WIDTH), MM_DTYPE), compiler_params=_cparams(("parallel",)))(ab, dout)


def _add2(name, a, b, out_dtype):
    tm = _tile(a.shape[0], 512)

    def body(a_ref, b_ref, o_ref):
        o_ref[...] = (a_ref[...] + b_ref[...]).astype(o_ref.dtype)

    blk = pl.BlockSpec((tm, a.shape[1]), lambda i: (i, 0))
    return pl.pallas_call(body, name=name, grid=(a.shape[0] // tm,), in_specs=[blk, blk], out_specs=blk,
                          out_shape=_sds(a.shape, out_dtype), compiler_params=_cparams(("parallel",)))(a, b)


def _layer_norm(r, g, b):
    mu = r.mean(-1, keepdims=True)
    var = jnp.square(r - mu).mean(-1, keepdims=True)
    return (r - mu) * lax.rsqrt(var + LN_EPS) * g + b


def _ln_fwd(name, x, y, g, b):
    tm = _tile(SEQ, 256)

    def body(x_ref, y_ref, g_ref, b_ref, r_ref, o_ref, ob_ref):
        r = DEEPNORM_ALPHA * x_ref[...] + y_ref[...]
        r_ref[...] = r
        o = _layer_norm(r, g_ref[...], b_ref[...])
        o_ref[...] = o
        ob_ref[...] = o.astype(ob_ref.dtype)

    blk = pl.BlockSpec((tm, D_MODEL), lambda i: (i, 0))
    row = pl.BlockSpec((1, D_MODEL), lambda i: (0, 0))
    return pl.pallas_call(body, name=name, grid=(SEQ // tm,), in_specs=[blk, blk, row, row], out_specs=[blk, blk, blk],
                          out_shape=[_sds((SEQ, D_MODEL))] * 2 + [_sds((SEQ, D_MODEL), MM_DTYPE)],
                          compiler_params=_cparams(("parallel",)))(x, y, g, b)


def _ln_bwd(name, r, g, b, da, db=None, dep=None):
    tm = _tile(SEQ, 256)
    two = db is not None
    deps = _as_list(dep)

    def body(r_ref, g_ref, b_ref, da_ref, *rest):
        dr_ref, drb_ref, dg_ref, dbeta_ref = rest[-4:]
        dout = DEEPNORM_ALPHA * da_ref[...] + rest[0][...] if two else da_ref[...]
        _, vjp = jax.vjp(_layer_norm, r_ref[...], g_ref[...], b_ref[...])
        dr, dg, dbeta = vjp(dout)
        dr_ref[...] = dr
        drb_ref[...] = dr.astype(drb_ref.dtype)

        @pl.when(pl.program_id(0) == 0)
        def _():
            dg_ref[...] = jnp.zeros_like(dg_ref)
            dbeta_ref[...] = jnp.zeros_like(dbeta_ref)

        dg_ref[...] += dg
        dbeta_ref[...] += dbeta

    blk = pl.BlockSpec((tm, D_MODEL), lambda i: (i, 0))
    row = pl.BlockSpec((1, D_MODEL), lambda i: (0, 0))
    args = [r, g, b, da] + ([db] if two else []) + deps
    return pl.pallas_call(body, name=name, grid=(SEQ // tm,),
                          in_specs=[blk, row, row, blk] + ([blk] if two else []) + [ANY] * len(deps),
                          out_specs=[blk, blk, row, row],
                          out_shape=[_sds((SEQ, D_MODEL)), _sds((SEQ, D_MODEL), MM_DTYPE), _sds((1, D_MODEL)), _sds((1, D_MODEL))],
                          compiler_params=_cparams(("arbitrary",)))(*args)


FFN_TM = 128
HALO = SUBLANES


def _conv_taps(cur, halo):
    row = lax.broadcasted_iota(jnp.int32, cur.shape, 0)
    h1 = jnp.where(row == 0, halo[HALO - 1:HALO, :], pltpu.roll(cur, 1, 0))
    h2 = jnp.where(row == 0, halo[HALO - 2:HALO - 1, :], jnp.where(row == 1, halo[HALO - 1:HALO, :], pltpu.roll(cur, 2, 0)))
    return h1, h2


def _conv_fwd(cur, halo, w_ref, b_ref):
    h1, h2 = _conv_taps(cur, halo)
    return b_ref[...] + h2 * w_ref[0:1, :] + h1 * w_ref[1:2, :] + cur * w_ref[2:3, :], h1, h2


def _gate(val, gate):
    return jax.nn.silu(gate) * val


def _ffn_specs(tm):
    nb = tm // HALO
    cur = lambda off: pl.BlockSpec((None, tm, FS), lambda p, i: (p + off, i, 0))
    halo = lambda off: pl.BlockSpec((None, HALO, FS), lambda p, i: (p + off, jnp.maximum(i * nb - 1, 0), 0))
    cw = lambda off: pl.BlockSpec((None, CONV_WIDTH, FS), lambda p, i: (p + off, 0, 0))
    cb = lambda off: pl.BlockSpec((None, 1, FS), lambda p, i: (p + off, 0, 0))
    return cur, halo, cw, cb


def _ffn_act(hf, conv_w, conv_b):
    tm = _tile(SEQ, FFN_TM, SUBLANES)
    cur, halo, cw, cb = _ffn_specs(tm)

    def body(v_ref, vh_ref, g_ref, gh_ref, wv_ref, wg_ref, bv_ref, bg_ref, o_ref):
        live = pl.program_id(1) > 0
        vh = jnp.where(live, vh_ref[...], 0.0)
        gh = jnp.where(live, gh_ref[...], 0.0)
        val, _, _ = _conv_fwd(v_ref[...], vh, wv_ref, bv_ref)
        gate, _, _ = _conv_fwd(g_ref[...], gh, wg_ref, bg_ref)
        o_ref[...] = _gate(val, gate).astype(o_ref.dtype)

    return pl.pallas_call(
        body, name="ffn_act", grid=(2, SEQ // tm),
        in_specs=[cur(0), halo(0), cur(2), halo(2), cw(0), cw(2), cb(0), cb(2)],
        out_specs=pl.BlockSpec((None, tm, FS), lambda p, i: (p, i, 0)),
        out_shape=_sds((2, SEQ, FS), MM_DTYPE), compiler_params=_cparams(("parallel", "parallel")))(
            hf, hf, hf, hf, conv_w, conv_w, conv_b, conv_b)


def _ffn_act_bwd(hf, conv_w, conv_b, dact, dep=None):
    tm = _tile(SEQ, FFN_TM, SUBLANES)
    nb, nblk = tm // HALO, SEQ // tm
    cur, halo, cw, cb = _ffn_specs(tm)
    nxt = lambda off: pl.BlockSpec((None, HALO, FS), lambda p, i: (p + off, jnp.minimum((i + 1) * nb, SEQ // HALO - 1), 0))
    deps = _as_list(dep)

    def body(v_ref, vh_ref, vn_ref, g_ref, gh_ref, gn_ref, wv_ref, wg_ref, bv_ref, bg_ref, da_ref, dan_ref, *rest):
        dh_ref, dw_ref, dbias_ref = rest[-3:]
        dwv_ref, dwg_ref = dw_ref.at[0], dw_ref.at[1]
        dbv_ref, dbg_ref = dbias_ref.at[0], dbias_ref.at[1]
        i = pl.program_id(1)
        live, more = i > 0, i < nblk - 1
        vh = jnp.where(live, vh_ref[...], 0.0)
        gh = jnp.where(live, gh_ref[...], 0.0)
        vcur, gcur = v_ref[...], g_ref[...]
        val, v1, v2 = _conv_fwd(vcur, vh, wv_ref, bv_ref)
        gate, g1, g2 = _conv_fwd(gcur, gh, wg_ref, bg_ref)
        _, vjp = jax.vjp(_gate, val, gate)
        dval, dgate = vjp(da_ref[...])
        val_n, _, _ = _conv_fwd(vn_ref[...], vcur[tm - HALO:, :], wv_ref, bv_ref)
        gate_n, _, _ = _conv_fwd(gn_ref[...], gcur[tm - HALO:, :], wg_ref, bg_ref)
        _, vjp_n = jax.vjp(_gate, val_n, gate_n)
        dval_n, dgate_n = vjp_n(dan_ref[...])
        row = lax.broadcasted_iota(jnp.int32, dval.shape, 0)
        for k, (d, dn, w_ref) in enumerate(((dval, dval_n, wv_ref), (dgate, dgate_n, wg_ref))):
            dn = jnp.where(more, dn, 0.0)
            d1 = jnp.where(row == tm - 1, dn[0:1, :], pltpu.roll(d, tm - 1, 0))
            d2 = jnp.where(row == tm - 1, dn[1:2, :], jnp.where(row == tm - 2, dn[0:1, :], pltpu.roll(d, tm - 2, 0)))
            dh_ref[k] = (d * w_ref[2:3, :] + d1 * w_ref[1:2, :] + d2 * w_ref[0:1, :]).astype(dh_ref.dtype)

        @pl.when(i == 0)
        def _():
            dw_ref[...] = jnp.zeros_like(dw_ref)
            dbias_ref[...] = jnp.zeros_like(dbias_ref)

        for d, taps, dwk_ref, dbk_ref in ((dval, (v2, v1, vcur), dwv_ref, dbv_ref), (dgate, (g2, g1, gcur), dwg_ref, dbg_ref)):
            for k in range(CONV_WIDTH):
                dwk_ref[k:k + 1, :] += jnp.sum(d * taps[k], axis=0, keepdims=True)
            dbk_ref[...] += jnp.sum(d, axis=0, keepdims=True)

    return pl.pallas_call(
        body, name="ffn_act_bwd", grid=(2, SEQ // tm),
        in_specs=[cur(0), halo(0), nxt(0), cur(2), halo(2), nxt(2), cw(0), cw(2), cb(0), cb(2),
                  pl.BlockSpec((None, tm, FS), lambda p, i: (p, i, 0)), nxt(0)] + [ANY] * len(deps),
        out_specs=[pl.BlockSpec((None, 2, tm, FS), lambda p, i: (p, 0, i, 0)),
                   pl.BlockSpec((None, 2, CONV_WIDTH, FS), lambda p, i: (p, 0, 0, 0)),
                   pl.BlockSpec((None, 2, 1, FS), lambda p, i: (p, 0, 0, 0))],
        out_shape=[_sds((2, 2, SEQ, FS), MM_DTYPE), _sds((2, 2, CONV_WIDTH, FS)), _sds((2, 2, 1, FS))],
        compiler_params=_cparams(("parallel", "arbitrary")))(hf, hf, hf, hf, hf, hf, conv_w, conv_w, conv_b, conv_b, dact,
                                                              dact, *deps)


def _loss(y, target):
    tm = _tile(SEQ, 256)

    def body(y_ref, t_ref, dy_ref, l_ref):
        err = y_ref[...] - t_ref[...]
        dy_ref[...] = err * (1.0 / D_MODEL)

        @pl.when(pl.program_id(0) == 0)
        def _():
            l_ref[...] = jnp.zeros_like(l_ref)

        l_ref[...] += 0.5 * jnp.sum(jnp.mean(jnp.square(err), axis=-1))

    blk = pl.BlockSpec((tm, D_MODEL), lambda i: (i, 0))
    return pl.pallas_call(body, name="loss", grid=(SEQ // tm,), in_specs=[blk, blk],
                          out_specs=[blk, pl.BlockSpec((SUBLANES, LANES), lambda i: (0, 0))],
                          out_shape=[_sds((SEQ, D_MODEL)), _sds((SUBLANES, LANES))],
                          compiler_params=_cparams(("arbitrary",)))(y, target)


ADAM_BLOCK_BYTES = 3 << 19
ELEMENTWISE_COLS = 1024


def _adamw_math(w, g, m, v):
    nm = ADAM_B1 * m + (1.0 - ADAM_B1) * g
    nv = ADAM_B2 * v + (1.0 - ADAM_B2) * jnp.square(g)
    m_hat = nm / (1.0 - ADAM_B1 ** ADAM_STEP)
    v_hat = nv / (1.0 - ADAM_B2 ** ADAM_STEP)
    return -ADAM_LR * (m_hat / (jnp.sqrt(v_hat) + ADAM_EPS) + ADAM_WD * w), nm, nv


def _adamw(name, w, g, m, v):
    r, c = w.shape
    tr = _tile(r, max(SUBLANES, ADAM_BLOCK_BYTES // (4 * c)), SUBLANES)

    def body(w_ref, g_ref, m_ref, v_ref, d_ref, nm_ref, nv_ref):
        d_ref[...], nm_ref[...], nv_ref[...] = _adamw_math(w_ref[...], g_ref[...], m_ref[...], v_ref[...])

    blk = pl.BlockSpec((tr, c), lambda i: (i, 0))
    return pl.pallas_call(body, name=name, grid=(r // tr,), in_specs=[blk] * 4, out_specs=[blk] * 3,
                          out_shape=[_sds((r, c))] * 3, compiler_params=_cparams(("parallel",)))(w, g, m, v)


def _adamw_big(name, l, c_idx, w, m, v, g_own, g_got, prev):
    depth, _, r, c = w.shape
    tc = _tile(c, ELEMENTWISE_COLS)
    tr = _tile(r, max(SUBLANES, ADAM_BLOCK_BYTES // (4 * tc)), SUBLANES)

    def body(c_ref, w_ref, m_ref, v_ref, own_ref, got_ref, *rest):
        g_ref, d_ref, nm_ref, nv_ref = rest[-4:]
        g = jnp.where(pl.program_id(0) == c_ref[0], own_ref[...], got_ref[...])
        g_ref[...] = g
        d_ref[...], nm_ref[...], nv_ref[...] = _adamw_math(w_ref[...], g, m_ref[...], v_ref[...])

    stacked = pl.BlockSpec((None, None, tr, tc), lambda h, i, j, cr: (l, h, i, j))
    own = pl.BlockSpec((tr, tc), lambda h, i, j, cr: (jnp.where(h == cr[0], i, 0), jnp.where(h == cr[0], j, 0)))
    got = pl.BlockSpec((tr, tc), lambda h, i, j, cr: (jnp.where(h == cr[0], 0, i), jnp.where(h == cr[0], 0, j)))
    grid_spec = pltpu.PrefetchScalarGridSpec(
        num_scalar_prefetch=1, grid=(2, r // tr, c // tc),
        in_specs=[stacked] * 3 + [own, got] + ([ANY] * 4 if prev else []), out_specs=[stacked] * 4)
    return pl.pallas_call(
        body, name=name, grid_spec=grid_spec, out_shape=[_sds((depth, 2, r, c))] * 4,
        input_output_aliases={6 + k: k for k in range(4)} if prev else {},
        compiler_params=_cparams(("arbitrary", "arbitrary", "arbitrary")))(c_idx, w, m, v, g_own, g_got, *(prev or ()))


ANY = pl.BlockSpec(memory_space=pl.ANY)


def _place():
    x, y, c = lax.axis_index("x"), lax.axis_index("y"), lax.axis_index("c")
    chips = [(1 - x, y), (x, 1 - y), (1 - x, 1 - y)]
    return x, y, c, chips


def _cast_place(name, w, l, me_idx, out_dtype):
    _, _, r, c = w.shape
    tr = _tile(r, max(2 * SUBLANES, COPY_BLOCK_BYTES // (4 * c)), 2 * SUBLANES)

    def body(me_ref, w_ref, o_ref):
        o_ref[...] = w_ref[...].astype(o_ref.dtype)

    grid_spec = pltpu.PrefetchScalarGridSpec(
        num_scalar_prefetch=1, grid=(2, r // tr),
        in_specs=[pl.BlockSpec((None, None, tr, c), lambda h, i, me: (l, h, i, 0))],
        out_specs=pl.BlockSpec((None, None, tr, c), lambda h, i, me: (me[0], h, i, 0)))
    return pl.pallas_call(body, name=name, grid_spec=grid_spec, out_shape=_sds((N_CHIPS, 2, r, c), out_dtype),
                          compiler_params=_cparams(("parallel", "parallel")))(me_idx, w)


HBM = pl.BlockSpec(memory_space=pltpu.HBM)
SEM = pl.BlockSpec(memory_space=pltpu.SEMAPHORE)
TOKEN = (SUBLANES, LANES)


def _comm_call(name, body, hbm, sems_in=(), after=None, sems_out=(), token=False):
    n, k = len(hbm), len(sems_out)
    ins = [pltpu.with_memory_space_constraint(a, pltpu.HBM) for a in hbm] + list(sems_in)
    in_specs = [HBM] * n + [SEM] * len(sems_in)
    if after is not None:
        ins.append(after)
        in_specs.append(ANY)
    out_shape = [pltpu.SemaphoreType.DMA((s,)) for s in sems_out] + [pltpu.HBM(a.shape, a.dtype) for a in hbm]
    out_specs = [SEM] * k + [HBM] * n
    if token:
        out_shape.append(_sds(TOKEN))
        out_specs.append(pl.BlockSpec(memory_space=pltpu.VMEM))
    res = pl.pallas_call(
        body, name=name, in_specs=in_specs, out_specs=out_specs, out_shape=out_shape,
        input_output_aliases={i: k + i for i in range(n)},
        compiler_params=pltpu.CompilerParams(has_side_effects=pltpu.SideEffectType.DATAFLOW_SIDE_EFFECTING))(*ins)
    return list(res[:k]), list(res[k:k + n]), (res[k + n] if token else None)


def _remote(src, dst, send, recv, to):
    return pltpu.make_async_remote_copy(src_ref=src, dst_ref=dst, send_sem=send, recv_sem=recv, device_id=to,
                                        device_id_type=MESH)


def _gather_start(name, bufs, after=None, pair=False):
    n = len(bufs)
    o = n + (after is not None)

    def body(*refs):
        ins, (send, recv), token = refs[:n], refs[o:o + 2], refs[-1]
        x, y, c, chips = _place()
        for i in range(n):
            mine = ins[i].at[2 * x + y, c]
            for k, chip in enumerate(chips):
                _remote(mine, mine, send.at[3 * i + k], recv.at[3 * i + k], (*chip, c)).start()
            if pair:
                _remote(mine, mine, send.at[3 * n + i], recv.at[3 * n + i], (x, y, 1 - c)).start()
        token[...] = jnp.zeros(TOKEN, F32)

    n_sems = (3 + pair) * n
    return _comm_call(name, body, bufs, after=after, sems_out=(n_sems, n_sems), token=True)


def _gather_forward(name, bufs, sems, after, pair=False):
    n = len(bufs)
    o = n + 2 + (after is not None)

    def body(*refs):
        ins, (send, recv), (send2, recv2), token = refs[:n], refs[n:n + 2], refs[o:o + 2], refs[-1]
        x, y, c, chips = _place()
        for i in range(n):
            mine = ins[i].at[2 * x + y, c]
            for k, chip in enumerate(chips):
                land = ins[i].at[2 * chip[0] + chip[1], c]
                first = _remote(mine, land, send.at[3 * i + k], recv.at[3 * i + k], (*chip, c))
                first.wait_send()
                first.wait_recv()
                _remote(land, land, send2.at[3 * i + k], recv2.at[3 * i + k], (x, y, 1 - c)).start()
            if pair:
                own = _remote(mine, ins[i].at[2 * x + y, 1 - c], send.at[3 * n + i], recv.at[3 * n + i], (x, y, 1 - c))
                own.wait_send()
                own.wait_recv()
        token[...] = jnp.zeros(TOKEN, F32)

    return _comm_call(name, body, bufs, sems_in=sems, after=after, sems_out=(3 * n, 3 * n), token=True)


def _gather_finish(name, bufs, sems, after):
    n = len(bufs)

    def body(*refs):
        ins, (send, recv) = refs[:n], refs[n:n + 2]
        x, y, c, chips = _place()
        for i in range(n):
            for k, chip in enumerate(chips):
                idx = 2 * chip[0] + chip[1]
                cp = _remote(ins[i].at[idx, c], ins[i].at[idx, 1 - c], send.at[3 * i + k], recv.at[3 * i + k], (x, y, 1 - c))
                cp.wait_send()
                cp.wait_recv()

    return _comm_call(name, body, bufs, sems_in=sems, after=after)[1]


def _swap_start(name, grads):
    n = len(grads)
    lands = [lax.empty((g.shape[0],) + g.shape[2:], g.dtype) for g in grads]

    def body(*refs):
        ins, lnd, (send, recv), token = refs[:n], refs[n:2 * n], refs[2 * n:2 * n + 2], refs[-1]
        x, y, c, _ = _place()
        for i in range(n):
            _remote(ins[i].at[:, 1 - c], lnd[i], send.at[i], recv.at[i], (x, y, 1 - c)).start()
        token[...] = jnp.zeros(TOKEN, F32)

    return _comm_call(name, body, list(grads) + lands, sems_out=(n, n), token=True)


def _swap_wait(name, hbm, sems, after):
    n = len(hbm) // 2

    def body(*refs):
        ins, lnd, (send, recv) = refs[:n], refs[n:2 * n], refs[2 * n:2 * n + 2]
        x, y, c, _ = _place()
        for i in range(n):
            cp = _remote(ins[i].at[:, 1 - c], lnd[i], send.at[i], recv.at[i], (x, y, 1 - c))
            cp.wait_send()
            cp.wait_recv()

    out = _comm_call(name, body, hbm, sems_in=sems, after=after)[1]
    return out[:n], out[n:]


def _pair_add(name, g, got, cm_idx):
    nk, _, r, c = g.shape
    tr = _tile(r, max(2 * SUBLANES, COPY_BLOCK_BYTES // (4 * c)), 2 * SUBLANES)

    def body(cm_ref, g_ref, x_ref, o_ref, land_ref):
        s = (g_ref[...] + x_ref[...]).astype(o_ref.dtype)
        o_ref[...] = s

        @pl.when(pl.program_id(1) == cm_ref[1])
        def _():
            land_ref[...] = s

    grid_spec = pltpu.PrefetchScalarGridSpec(
        num_scalar_prefetch=1, grid=(r // tr, nk),
        in_specs=[pl.BlockSpec((None, None, tr, c), lambda i, k, cm: (k, cm[0], i, 0)),
                  pl.BlockSpec((None, tr, c), lambda i, k, cm: (k, i, 0))],
        out_specs=[pl.BlockSpec((None, tr, c), lambda i, k, cm: (k, i, 0)),
                   pl.BlockSpec((None, tr, c), lambda i, k, cm: (cm[1], i, 0))])
    return pl.pallas_call(body, name=name, grid_spec=grid_spec, out_shape=[_sds((nk, r, c), BF16)] * 2,
                          compiler_params=_cparams(("parallel", "arbitrary")))(cm_idx, g, got)


def _scatter_start(name, parts, lands):
    n = len(parts)

    def body(*refs):
        ins, lnd, (send, recv), token = refs[:n], refs[n:2 * n], refs[2 * n:2 * n + 2], refs[-1]
        x, y, c, chips = _place()
        for i in range(n):
            for k, chip in enumerate(chips):
                _remote(ins[i].at[2 * chip[0] + chip[1]], lnd[i].at[2 * x + y], send.at[3 * i + k], recv.at[3 * i + k],
                        (*chip, c)).start()
        token[...] = jnp.zeros(TOKEN, F32)

    return _comm_call(name, body, list(parts) + list(lands), sems_out=(3 * n, 3 * n), token=True)


def _scatter_wait(name, hbm, sems, after):
    n = len(hbm) // 2

    def body(*refs):
        ins, lnd, (send, recv) = refs[:n], refs[n:2 * n], refs[2 * n:2 * n + 2]
        x, y, c, chips = _place()
        for i in range(n):
            for k, chip in enumerate(chips):
                idx = 2 * chip[0] + chip[1]
                cp = _remote(ins[i].at[idx], lnd[i].at[idx], send.at[3 * i + k], recv.at[3 * i + k], (*chip, c))
                cp.wait_send()
                cp.wait_recv()

    out = _comm_call(name, body, hbm, sems_in=sems, after=after)[1]
    return out[:n], out[n:]


def _sum_leading(name, x, out_dtype=F32):
    nk, r, c = x.shape
    tc = _tile(c, ELEMENTWISE_COLS)
    tr = _tile(r, max(2 * SUBLANES, COPY_BLOCK_BYTES // (nk * tc * x.dtype.itemsize)), 2 * SUBLANES)

    def body(x_ref, o_ref):
        acc = x_ref[0].astype(F32)
        for k in range(1, nk):
            acc = acc + x_ref[k].astype(F32)
        o_ref[...] = acc.astype(o_ref.dtype)

    return pl.pallas_call(body, name=name, grid=(r // tr, c // tc),
                          in_specs=[pl.BlockSpec((nk, tr, tc), lambda i, j: (0, i, j))],
                          out_specs=pl.BlockSpec((tr, tc), lambda i, j: (i, j)), out_shape=_sds((r, c), out_dtype),
                          compiler_params=_cparams(("parallel", "parallel")))(x)


def _exchange_start(name, halves):
    n = len(halves)
    lands = [lax.empty(h.shape, h.dtype) for h in halves]

    def body(*refs):
        ins, lnd, (send, recv), token = refs[:n], refs[n:2 * n], refs[2 * n:2 * n + 2], refs[-1]
        x, y, c, _ = _place()
        for i in range(n):
            _remote(ins[i], lnd[i], send.at[i], recv.at[i], (x, y, 1 - c)).start()
        token[...] = jnp.zeros(TOKEN, F32)

    return _comm_call(name, body, list(halves) + lands, sems_out=(n, n), token=True)


def _exchange_wait(name, hbm, sems, after):
    n = len(hbm) // 2

    def body(*refs):
        ins, lnd, (send, recv) = refs[:n], refs[n:2 * n], refs[2 * n:2 * n + 2]
        x, y, c, _ = _place()
        for i in range(n):
            cp = _remote(ins[i], lnd[i], send.at[i], recv.at[i], (x, y, 1 - c))
            cp.wait_send()
            cp.wait_recv()

    out = _comm_call(name, body, hbm, sems_in=sems, after=after)[1]
    return out[:n], out[n:]


SMALL = ("attn_sinks", "pool_w", "pool_scale", "ssm_lam_re", "ssm_lam_im", "ssm_log_dt", "ssm_b_re", "ssm_b_im",
         "ssm_c_re", "ssm_c_im", "ssm_d", "ln1_g", "ln1_b", "ffn_conv_b", "ln2_g", "ln2_b")
BIG = ("w_in", "ssm_glu_w", "w_out", "ffn_w_up", "ffn_conv_w", "ffn_w_down")
ALL_W = ("w_in", "attn_sinks", "pool_w", "pool_scale", "ssm_lam_re", "ssm_lam_im", "ssm_log_dt", "ssm_b_re", "ssm_b_im",
         "ssm_c_re", "ssm_c_im", "ssm_d", "ssm_glu_w", "w_out", "ln1_g", "ln1_b", "ffn_w_up", "ffn_conv_w", "ffn_conv_b",
         "ffn_w_down", "ln2_g", "ln2_b")
PACK_UNIT = SUBLANES * LANES


def _padded(n):
    return -(-n // PACK_UNIT) * PACK_UNIT


def _pack(arrs):
    cols = []
    for name in SMALL:
        a = arrs[name].reshape(DEPTH, -1)
        cols.append(jnp.pad(a, ((0, 0), (0, _padded(a.shape[1]) - a.shape[1]))))
    return jnp.concatenate(cols, axis=1).reshape(-1, LANES)


def _unpack(packed, shapes):
    flat = packed.reshape(DEPTH, -1)
    out, off = {}, 0
    for name in SMALL:
        n = math.prod(shapes[name][1:])
        out[name] = flat[:, off:off + n].reshape(shapes[name])
        off += _padded(n)
    return out


def _b_rows(b):
    return b.transpose(2, 0, 1).reshape(SSM_GROUP, SSM_CH)


def _b_unrows(b):
    return b.reshape(SSM_GROUP, SSM_N_GROUPS, SSM_STATE).transpose(1, 2, 0)


def _block_diag_in(bb):
    eye = jnp.eye(SSM_N_GROUPS, dtype=F32)
    b3 = bb.reshape(SSM_GROUP, SSM_N_GROUPS, SSM_STATE)
    return jnp.einsum("hgp,gk->ghkp", b3, eye).reshape(SSM_WIDTH, SSM_CH)


def _c_unrows(c):
    return c.reshape(SSM_GROUP, SSM_N_GROUPS, SSM_STATE).transpose(1, 0, 2)


def _block_diag_out(cc):
    eye = jnp.eye(SSM_N_GROUPS, dtype=F32)
    return jnp.einsum("ghp,gk->gpkh", cc, eye).reshape(SSM_CH, SSM_WIDTH)


def _rows_layout(re, im):
    n = re.shape[1]
    return jnp.stack([re.reshape(SCAN_NB, SCAN_CW, n), im.reshape(SCAN_NB, SCAN_CW, n)], axis=1).reshape(2 * SSM_CH, n)


H_POOL0 = ATTN_WIDTH + 2 * KV_WIDTH
H_SSM0 = H_POOL0 + POOL_WIDTH


def _ssm_params(p):
    lr = p["ssm_lam_re"].reshape(1, SSM_CH)
    li = p["ssm_lam_im"].reshape(1, SSM_CH)
    ldt = jnp.repeat(p["ssm_log_dt"], SSM_STATE).reshape(1, SSM_CH)
    return lr, li, ldt, _b_rows(p["ssm_b_re"]), _b_rows(p["ssm_b_im"])


def _layer_fwd(x, xb, p, wg, rope_t, dep, mid):
    cos_t, sin_t = rope_t
    h = _mm_shard_cols("in_proj", xb, wg["w_in"], dep=dep)
    qk = _rope("rope_fwd", h, 0, Q_TILES + KV_TILES, cos_t, sin_t, MM_DTYPE)
    y_attn, y_attn_b = _attn_fwd(qk, h, p["attn_sinks"])
    y_pool = _pool_fwd(h, p["pool_w"], p["pool_scale"].reshape(1, POOL_WIDTH))
    ssm_in = _ssm_params(p)
    ar, ai, bbr, bbi = _ssm_prep(*ssm_in)
    bd = _scan_layout(_block_diag_in(bbr), _block_diag_in(bbi)).astype(MM_DTYPE)
    cc = _rows_layout(_block_diag_out(p["ssm_c_re"]), -_block_diag_out(p["ssm_c_im"])).astype(MM_DTYPE)
    dvec = p["ssm_d"].reshape(1, SSM_WIDTH)
    up = _time_permute(h[:, H_SSM0:])
    xx = _mm_nn("ssm_bu", up, bd, tn=1024)
    ss = _ssm_scan("ssm_scan_fwd", _scan_layout(ar, ai), xx, False)
    yp = _mm_nn("ssm_cs", ss, cc, tk=1024)
    yf, gy = _ssm_gelu(yp, up, dvec)
    ab = _mm_shard_cols("ssm_glu_proj", gy, wg["ssm_glu_w"])
    y_ssm = _time_unpermute(_ssm_glu(ab))
    mix = jnp.concatenate([y_attn_b, y_pool, y_ssm], axis=1)
    mixo = _mm_nn("out_proj", mix, wg["w_out"].reshape(MIX_WIDTH, D_MODEL))
    r1, x1, x1b = _ln_fwd("ln1_fwd", x, mixo, p["ln1_g"].reshape(1, D_MODEL), p["ln1_b"].reshape(1, D_MODEL))
    tokens = mid(x1b)
    hf = _ffn_up(x1b, wg["ffn_w_up"], dep=tokens)
    conv_b = p["ffn_conv_b"].reshape(N_CHIPS, 1, FS)
    act = _ffn_act(hf, wg["ffn_conv_w"], conv_b)
    f = _ffn_down(act, wg["ffn_w_down"].reshape(2, FS, D_MODEL))
    r2, x2, x2b = _ln_fwd("ln2_fwd", x1, f, p["ln2_g"].reshape(1, D_MODEL), p["ln2_b"].reshape(1, D_MODEL))
    saved = dict(xb=xb, h=h, qk=qk, y_attn=y_attn, ssm_in=ssm_in, ar=ar, ai=ai, bd=bd, cc=cc, dvec=dvec, up=up, ss=ss, yf=yf,
                 gy=gy, ab=ab, mix=mix, r1=r1, x1b=x1b, hf=hf, conv_b=conv_b, act=act, r2=r2)
    return x2, x2b, saved


def _layer_bwd(da, db, p, wg, sv, rope_t, run, start):
    cos_t, sin_t = rope_t
    small = {}
    dr2, dr2b, dg, dbeta = _ln_bwd("ln2_bwd" if db is not None else "ln2_bwd_last", sv["r2"], p["ln2_g"].reshape(1, D_MODEL),
                                   p["ln2_b"].reshape(1, D_MODEL), da, db, dep=run("h0", None))
    small["ln2_g"], small["ln2_b"] = dg, dbeta
    w_down = wg["ffn_w_down"].reshape(2, FS, D_MODEL)
    dact = _ffn_down_dact(dr2b, w_down)
    dw_down = _ffn_down_dw(sv["act"], dr2b)
    dh_ffn, dcw, dcb = _ffn_act_bwd(sv["hf"], wg["ffn_conv_w"], sv["conv_b"], dact, dep=run("h1", dw_down))
    dconv_w = dcw.transpose(1, 0, 2, 3).reshape(N_CHIPS, CONV_WIDTH, FS)
    small["ffn_conv_b"] = dcb.transpose(1, 0, 2, 3)
    dw_up = _ffn_up_dw(sv["x1b"], dh_ffn)
    tok = [start("ffn", {"ffn_w_up": dw_up, "ffn_conv_w": dconv_w,
                         "ffn_w_down": dw_down.reshape(N_CHIPS, FS // 2, D_MODEL)})] + run("h2", dw_up)
    dx1_ffn = _ffn_up_dx(dh_ffn, wg["ffn_w_up"], dep=tok)
    dr1, dr1b, dg, dbeta = _ln_bwd("ln1_bwd", sv["r1"], p["ln1_g"].reshape(1, D_MODEL), p["ln1_b"].reshape(1, D_MODEL), dr2,
                                   dx1_ffn, dep=tok)
    small["ln1_g"], small["ln1_b"] = dg, dbeta
    w_out = wg["w_out"].reshape(MIX_WIDTH, D_MODEL)
    dw_out = _mm_tn("out_proj_dw", sv["mix"], dr1b)
    dmix = _mm_nt("out_proj_dx", dr1b, w_out, dep=run("h3", dw_out))
    dq, dkc, dkp, dvc, dvp, dsk = _attn_bwd(sv["qk"], sv["h"], p["attn_sinks"], sv["y_attn"], dmix, 0)
    small["attn_sinks"] = dsk[:, :, 0]
    dh_attn = _attn_dh(dq, dkc, dkp, dvc, dvp, cos_t, -sin_t)
    dh_pool, dpw, dps = _pool_bwd(sv["h"], p["pool_w"], p["pool_scale"].reshape(1, POOL_WIDTH), dmix, ATTN_WIDTH // POOL_WIDTH)
    small["pool_w"], small["pool_scale"] = dpw, dps
    dout_p = _time_permute(dmix[:, ATTN_WIDTH + POOL_WIDTH:])
    dab = _ssm_glu_bwd(sv["ab"], dout_p)
    dgy = _mm_shard_cols_nt("ssm_glu_dx", dab, wg["ssm_glu_w"])
    dw_glu = _mm_shard_cols_tn("ssm_glu_dw", sv["gy"], dab, N_CHIPS)
    dyf, du1, dd = _ssm_gelu_bwd(sv["yf"], dgy, sv["up"], sv["dvec"])
    small["ssm_d"] = dd
    dss = _mm_nt("ssm_cs_dx", dyf, sv["cc"], tn=1024)
    dcre, dcim = _scan_unlayout(_ssm_diag("ssm_c_diag", _mm_tn("ssm_cs_dw", dyf, sv["ss"], tn=1024)))
    small["ssm_c_re"], small["ssm_c_im"] = _c_unrows(dcre), -_c_unrows(dcim)
    gg, da8 = _ssm_scan("ssm_scan_bwd", _scan_layout(sv["ar"], -sv["ai"]), dss, True, sv["ss"])
    du2 = _mm_nt("ssm_bu_dx", gg, sv["bd"], tk=1024)
    dbbr, dbbi = _scan_unlayout(_ssm_diag("ssm_b_diag", _mm_tn("ssm_bu_dw", sv["up"], gg, tn=1024)))
    dar8, dai8 = _scan_unlayout(da8)
    dlr, dli, dldt, dbr, dbi = _ssm_prep_bwd(*sv["ssm_in"], dar8, dai8, dbbr, dbbi)
    small["ssm_lam_re"], small["ssm_lam_im"] = dlr, dli
    small["ssm_log_dt"] = dldt.reshape(SSM_N_GROUPS, SSM_STATE).sum(axis=1)
    small["ssm_b_re"], small["ssm_b_im"] = _b_unrows(dbr), _b_unrows(dbi)
    dh_ssm = _time_unpermute(_add2("ssm_du", du1, du2, MM_DTYPE))
    dh = jnp.concatenate([dh_attn, dh_pool, dh_ssm], axis=1)
    dx_in = _mm_shard_cols_nt("in_proj_dx", dh, wg["w_in"], dep=run("h4", dh))
    dw_in = _mm_shard_cols_tn("in_proj_dw", sv["xb"], dh, N_CHIPS)
    start("mix", {"w_in": dw_in, "ssm_glu_w": dw_glu, "w_out": dw_out.reshape(N_CHIPS, MIX_WIDTH // N_CHIPS, D_MODEL)})
    return dr1, dx_in, small


CONV_PAD = 2 * SUBLANES


def _halved(name, a):
    if name == "ffn_conv_w":
        a = jnp.pad(a, ((0, 0), (0, CONV_PAD - CONV_WIDTH), (0, 0)))
    return a.reshape(a.shape[0], 2, a.shape[1] // 2, a.shape[2])


def _unhalved(name, a):
    a = a.reshape(a.shape[:-3] + (2 * a.shape[-2], a.shape[-1]))
    return a[..., :CONV_WIDTH, :] if name == "ffn_conv_w" else a


class _Reduce:
    def __init__(self, tag, grads, cm_idx):
        self.tag, self.cm_idx, self.names = tag, cm_idx, tuple(grads)
        g4 = [_halved(name, grads[name]) for name in self.names]
        self.sems, self.hbm, self.token = _swap_start("grad_swap_start_" + tag, g4)

    def swapped(self, after):
        g4, got = _swap_wait("grad_swap_wait_" + self.tag, self.hbm, self.sems, after)
        parts, lands = zip(*[_pair_add("grad_pair_add", g, x, self.cm_idx) for g, x in zip(g4, got)])
        self.sems, self.hbm, self.token = _scatter_start("grad_scatter_start_" + self.tag, parts, lands)
        return self.token

    def scattered(self, after):
        _, recv = _scatter_wait("grad_scatter_wait_" + self.tag, self.hbm, self.sems, after)
        halves = [_sum_leading("grad_chip_sum", r) for r in recv]
        self.sems, self.hbm, self.token = _exchange_start("grad_exchange_start_" + self.tag, halves)
        return self.token

    def finish(self, after):
        return _exchange_wait("grad_exchange_wait_" + self.tag, self.hbm, self.sems, after)


def kernel(x, w_in, attn_sinks, pool_w, pool_scale, ssm_lam_re, ssm_lam_im, ssm_log_dt, ssm_b_re, ssm_b_im, ssm_c_re, ssm_c_im, ssm_d, ssm_glu_w, w_out, ln1_g, ln1_b, ffn_w_up, ffn_conv_w, ffn_conv_b, ffn_w_down, ln2_g, ln2_b, loss_target, m_w_in, m_attn_sinks, m_pool_w, m_pool_scale, m_ssm_lam_re, m_ssm_lam_im, m_ssm_log_dt, m_ssm_b_re, m_ssm_b_im, m_ssm_c_re, m_ssm_c_im, m_ssm_d, m_ssm_glu_w, m_w_out, m_ln1_g, m_ln1_b, m_ffn_w_up, m_ffn_conv_w, m_ffn_conv_b, m_ffn_w_down, m_ln2_g, m_ln2_b, v_w_in, v_attn_sinks, v_pool_w, v_pool_scale, v_ssm_lam_re, v_ssm_lam_im, v_ssm_log_dt, v_ssm_b_re, v_ssm_b_im, v_ssm_c_re, v_ssm_c_im, v_ssm_d, v_ssm_glu_w, v_w_out, v_ln1_g, v_ln1_b, v_ffn_w_up, v_ffn_conv_w, v_ffn_conv_b, v_ffn_w_down, v_ln2_g, v_ln2_b):
    w = dict(w_in=w_in, attn_sinks=attn_sinks, pool_w=pool_w, pool_scale=pool_scale, ssm_lam_re=ssm_lam_re,
             ssm_lam_im=ssm_lam_im, ssm_log_dt=ssm_log_dt, ssm_b_re=ssm_b_re, ssm_b_im=ssm_b_im, ssm_c_re=ssm_c_re,
             ssm_c_im=ssm_c_im, ssm_d=ssm_d, ssm_glu_w=ssm_glu_w, w_out=w_out, ln1_g=ln1_g, ln1_b=ln1_b, ffn_w_up=ffn_w_up,
             ffn_conv_w=ffn_conv_w, ffn_conv_b=ffn_conv_b, ffn_w_down=ffn_w_down, ln2_g=ln2_g, ln2_b=ln2_b)
    m = dict(w_in=m_w_in, attn_sinks=m_attn_sinks, pool_w=m_pool_w, pool_scale=m_pool_scale, ssm_lam_re=m_ssm_lam_re,
             ssm_lam_im=m_ssm_lam_im, ssm_log_dt=m_ssm_log_dt, ssm_b_re=m_ssm_b_re, ssm_b_im=m_ssm_b_im, ssm_c_re=m_ssm_c_re,
             ssm_c_im=m_ssm_c_im, ssm_d=m_ssm_d, ssm_glu_w=m_ssm_glu_w, w_out=m_w_out, ln1_g=m_ln1_g, ln1_b=m_ln1_b,
             ffn_w_up=m_ffn_w_up, ffn_conv_w=m_ffn_conv_w, ffn_conv_b=m_ffn_conv_b, ffn_w_down=m_ffn_w_down, ln2_g=m_ln2_g,
             ln2_b=m_ln2_b)
    v = dict(w_in=v_w_in, attn_sinks=v_attn_sinks, pool_w=v_pool_w, pool_scale=v_pool_scale, ssm_lam_re=v_ssm_lam_re,
             ssm_lam_im=v_ssm_lam_im, ssm_log_dt=v_ssm_log_dt, ssm_b_re=v_ssm_b_re, ssm_b_im=v_ssm_b_im, ssm_c_re=v_ssm_c_re,
             ssm_c_im=v_ssm_c_im, ssm_d=v_ssm_d, ssm_glu_w=v_ssm_glu_w, w_out=v_w_out, ln1_g=v_ln1_g, ln1_b=v_ln1_b,
             ffn_w_up=v_ffn_w_up, ffn_conv_w=v_ffn_conv_w, ffn_conv_b=v_ffn_conv_b, ffn_w_down=v_ffn_w_down, ln2_g=v_ln2_g,
             ln2_b=v_ln2_b)
    c_pos = lax.axis_index("c").astype(jnp.int32)
    chip = (2 * lax.axis_index("x") + lax.axis_index("y")).astype(jnp.int32)
    c_idx, chip_idx, cm_idx = c_pos.reshape(1), chip.reshape(1), jnp.stack([c_pos, chip])
    rope_t = _rope_tables()
    xs = x.reshape(SEQ, D_MODEL)
    xb = xs.astype(MM_DTYPE)
    for t in (w, m, v):
        t["ffn_w_up"] = jnp.swapaxes(t["ffn_w_up"], 1, 2)
    wh, mh, vh = ({n: _halved(n, t[n]) for n in BIG} for t in (w, m, v))

    def place(l):
        return [_cast_place("place_" + n, wh[n], l, chip_idx, F32 if n == "ffn_conv_w" else MM_DTYPE) for n in BIG]

    n_mix = BIG.index("ffn_w_up")

    def gather_start(l, after):
        bufs = place(l)
        return (_gather_start("gather_start_%d_mix" % l, bufs[:n_mix], after),
                _gather_start("gather_start_%d_ffn" % l, bufs[n_mix:], after))

    def gather_wait(l, group, started, after):
        sems, bufs, _ = _gather_forward("gather_forward_%d_%s" % (l, group), started[1], started[0], after)
        bufs = _gather_finish("gather_finish_%d_%s" % (l, group), bufs, sems, after)
        names = BIG[:n_mix] if group == "mix" else BIG[n_mix:]
        return bufs, {n: _unhalved(n, g) for n, g in zip(names, bufs)}

    flight = gather_start(0, None)
    gathered, saved = [gather_wait(0, "mix", flight[0], None)[1]], []
    for l in range(DEPTH):
        nxt = {}

        def mid(after):
            bufs, wg_ffn = gather_wait(l, "ffn", flight[1], after)
            gathered[l].update(wg_ffn)
            if l + 1 == DEPTH:
                return []
            nxt["flight"] = gather_start(l + 1, bufs[0])
            return [nxt["flight"][0][2], nxt["flight"][1][2]]

        xs, xb, sv = _layer_fwd(xs, xb, {n: w[n][l] for n in SMALL}, gathered[l], rope_t, flight[1][2] if l == 0 else None, mid)
        saved.append(sv)
        if l + 1 < DEPTH:
            flight = nxt["flight"]
            gathered.append(gather_wait(l + 1, "mix", flight[0], xb)[1])
    dy, loss_tile = _loss(xs, loss_target.reshape(SEQ, D_MODEL))
    loss = lax.psum(loss_tile[0, 0], ("x", "y", "c"))

    big_out = {n: None for n in BIG}
    small_g = {n: [None] * DEPTH for n in SMALL}
    agenda = {}
    tail = []
    plan = {"ffn": (("h3", 0), ("end", 0), ("h1", -1)), "mix": (("h1", -1), ("h3", -1), ("h4", -1))}
    tail_rank = {("mix", 0): 0, ("ffn", 2): 1, ("mix", 1): 2, ("mix", 2): 3}
    started = []

    def book(l, group, red):
        def update(after):
            names, own, got = red.names, *red.finish(after)
            for n, o, g in zip(names, own, got):
                big_out[n] = _adamw_big("adamw_" + n, l, c_idx, wh[n], mh[n], vh[n], o, g, big_out[n])
            return [big_out[names[-1]][0]] if l == 0 else []

        steps = (lambda a: [red.swapped(a)], lambda a: [red.scattered(a)], update)
        for k, ((hook, dl), step) in enumerate(zip(plan[group], steps)):
            if l + dl >= 0:
                agenda.setdefault((l + dl, hook), []).append(step)
            else:
                tail.append((tail_rank[group, k], step))

    def run_at(l):
        return lambda hook, after: [t for step in agenda.pop((l, hook), []) for t in step(after)]

    def start_at(l):
        def start(group, grads):
            red = _Reduce("%s_%d" % (group, l), grads, cm_idx)
            book(l, group, red)
            started.append(red.token)
            return red.token
        return start

    da, db, carry = dy, None, []
    for l in reversed(range(DEPTH)):
        agenda.setdefault((l, "h0"), []).append(lambda after, carry=carry: carry)
        da, db, small = _layer_bwd(da, db, {n: w[n][l] for n in SMALL}, gathered[l], saved[l], rope_t, run_at(l), start_at(l))
        for n in SMALL:
            small_g[n][l] = small[n].reshape(w[n].shape[1:])
        carry = run_at(l)("end", db) + started[-1:]
    shapes = {n: w[n].shape for n in SMALL}
    part = _pack({n: jnp.stack(small_g[n]) for n in SMALL})
    slots = lax.dynamic_update_slice(jnp.zeros((N_CHIPS, 2) + part.shape, F32), part[None, None], (chip, c_pos, 0, 0))
    small_sems, small_bufs, after = _gather_start("gather_start_small", [slots], pair=True)
    for _, step in sorted(tail, key=lambda rs: rs[0]):
        after = (step(after) or [after])[-1]
    grad_x = _ln_in_grad(da, db).reshape(x.shape)
    small_sems, small_bufs, _ = _gather_forward("gather_forward_small", small_bufs, small_sems, after, pair=True)
    small_bufs = _gather_finish("gather_finish_small", small_bufs, small_sems, grad_x)
    g_small = _sum_leading("small_grad_sum", small_bufs[0].reshape((N_DEV,) + part.shape))
    upd = _adamw("adamw_small", _pack(w), g_small, _pack(m), _pack(v))
    small_out = [_unpack(a, shapes) for a in (g_small,) + tuple(upd)]

    outs = [loss, grad_x]
    for kind in range(4):
        for n in ALL_W:
            if n in SMALL:
                outs.append(small_out[kind][n])
            else:
                o = _unhalved(n, big_out[n][kind])
                outs.append(jnp.swapaxes(o, 1, 2) if n == "ffn_w_up" else o)
    return tuple(outs)


def _ln_in_grad(dr1, dx_in):
    tm = _tile(SEQ, 512)

    def body(a_ref, b_ref, o_ref):
        o_ref[...] = DEEPNORM_ALPHA * a_ref[...] + b_ref[...]

    blk = pl.BlockSpec((tm, D_MODEL), lambda i: (i, 0))
    return pl.pallas_call(body, name="grad_x", grid=(SEQ // tm,), in_specs=[blk, blk], out_specs=blk,
                          out_shape=_sds((SEQ, D_MODEL)), compiler_params=_cparams(("parallel",)))(dr1, dx_in)
```

```python
import functools
import math

import jax
import jax.numpy as jnp
from jax import lax
from jax.experimental import pallas as pl
from jax.experimental.pallas import tpu as pltpu

F32 = jnp.float32
BF16 = jnp.bfloat16
MM_DTYPE = BF16

D_MODEL = 2048
SEQ = 2048
DEPTH = 4
D_FF = 5504
HEAD_DIM = 64
N_Q_HEADS = D_MODEL // 2 // HEAD_DIM
N_KV_HEADS = N_Q_HEADS // 4
ATTN_WIDTH = N_Q_HEADS * HEAD_DIM
KV_WIDTH = N_KV_HEADS * HEAD_DIM
ATTN_BLOCK = 128
ROPE_THETA = 10000.0
POOL_WINDOWS = (2, 4, 8, 16)
POOL_WIDTH = D_MODEL // 4
POOL_GROUP = POOL_WIDTH // len(POOL_WINDOWS)
SSM_WIDTH = D_MODEL // 4
SSM_GROUP = 16
SSM_N_GROUPS = SSM_WIDTH // SSM_GROUP
SSM_STATE = 64
SSM_CH = SSM_N_GROUPS * SSM_STATE
MIX_WIDTH = ATTN_WIDTH + POOL_WIDTH + SSM_WIDTH
IN_WIDTH = ATTN_WIDTH + 2 * KV_WIDTH + POOL_WIDTH + SSM_WIDTH
CONV_WIDTH = 3
LN_EPS = 1e-5
DEEPNORM_ALPHA = (2 * DEPTH) ** 0.25
ADAM_LR = 0.001
ADAM_B1 = 0.9
ADAM_B2 = 0.999
ADAM_EPS = 1e-08
ADAM_WD = 0.01
ADAM_STEP = 10

N_CHIPS = 4
N_DEV = 8
FS = 2 * D_FF // N_CHIPS
IN_S = IN_WIDTH // N_CHIPS
GLU_S = 2 * SSM_WIDTH // N_CHIPS
LANES = 128
SUBLANES = 8
SCAN_CW = 256
SCAN_NB = SSM_CH // SCAN_CW
VMEM_LIMIT = 56 * 1024 * 1024
COPY_BLOCK_BYTES = 6 * 1024 * 1024
NEG = -1e30

NN = (((1,), (0,)), ((), ()))
NT = (((1,), (1,)), ((), ()))
TN = (((0,), (0,)), ((), ()))
MESH = pl.DeviceIdType.MESH


def _tile(n, pref, mult=LANES):
    best = None
    for t in range(mult, min(n, pref) + 1, mult):
        if n % t == 0:
            best = t
    return n if best is None else best


def _cparams(sem):
    return pltpu.CompilerParams(dimension_semantics=sem, vmem_limit_bytes=VMEM_LIMIT)


def _sds(shape, dtype=F32):
    return jax.ShapeDtypeStruct(tuple(shape), dtype)


def _as_list(x):
    return [] if x is None else list(x) if isinstance(x, (list, tuple)) else [x]


def _mm(name, a, b, out_shape, grid, a_spec, b_spec, o_spec, dims, acc_shape, out_dtype=F32, dep=None):
    nk = grid[2]
    deps = _as_list(dep)

    def product(a_ref, b_ref):
        return lax.dot_general(a_ref[...].astype(MM_DTYPE), b_ref[...].astype(MM_DTYPE), dims, preferred_element_type=F32)

    def body_one(a_ref, b_ref, *rest):
        rest[-1][...] = product(a_ref, b_ref).astype(rest[-1].dtype)

    def body(a_ref, b_ref, *rest):
        o_ref, acc_ref = rest[-2:]
        k = pl.program_id(2)

        @pl.when(k == 0)
        def _():
            acc_ref[...] = product(a_ref, b_ref)

        @pl.when(k > 0)
        def _():
            acc_ref[...] += product(a_ref, b_ref)

        @pl.when(k == nk - 1)
        def _():
            o_ref[...] = acc_ref[...].astype(o_ref.dtype)

    return pl.pallas_call(
        body_one if nk == 1 else body, name=name, grid=grid, in_specs=[a_spec, b_spec] + [ANY] * len(deps),
        out_specs=o_spec, out_shape=_sds(out_shape, out_dtype),
        scratch_shapes=[] if nk == 1 else [pltpu.VMEM(acc_shape, F32)],
        compiler_params=_cparams(("parallel", "parallel", "arbitrary")))(a, b, *deps)


def _mm_nn(name, a, b, tm=2048, tn=512, tk=2048, out_dtype=F32):
    m, kk = a.shape
    n = b.shape[1]
    tm, tn, tk = _tile(m, tm), _tile(n, tn), _tile(kk, tk)
    return _mm(name, a, b, (m, n), (m // tm, n // tn, kk // tk),
               pl.BlockSpec((tm, tk), lambda i, j, k: (i, k)), pl.BlockSpec((tk, tn), lambda i, j, k: (k, j)),
               pl.BlockSpec((tm, tn), lambda i, j, k: (i, j)), NN, (tm, tn), out_dtype)


def _mm_nt(name, a, b, tm=2048, tn=512, tk=2048, dep=None):
    m, kk = a.shape
    n = b.shape[0]
    tm, tn, tk = _tile(m, tm), _tile(n, tn), _tile(kk, tk)
    return _mm(name, a, b, (m, n), (m // tm, n // tn, kk // tk),
               pl.BlockSpec((tm, tk), lambda i, j, k: (i, k)), pl.BlockSpec((tn, tk), lambda i, j, k: (j, k)),
               pl.BlockSpec((tm, tn), lambda i, j, k: (i, j)), NT, (tm, tn), dep=dep)


def _mm_tn(name, a, b, tm=1024, tn=1024, ts=2048):
    s, m = a.shape
    n = b.shape[1]
    tm, tn, ts = _tile(m, tm), _tile(n, tn), _tile(s, ts)
    return _mm(name, a, b, (m, n), (m // tm, n // tn, s // ts),
               pl.BlockSpec((ts, tm), lambda i, j, k: (k, i)), pl.BlockSpec((ts, tn), lambda i, j, k: (k, j)),
               pl.BlockSpec((tm, tn), lambda i, j, k: (i, j)), TN, (tm, tn))


def _mm_shard_cols(name, a, w, tm=2048, tk=2048, dep=None):
    m, kk = a.shape
    nj, _, c = w.shape
    tm, tk = _tile(m, tm), _tile(kk, tk)
    return _mm(name, a, w, (m, nj * c), (m // tm, nj, kk // tk),
               pl.BlockSpec((tm, tk), lambda i, j, k: (i, k)), pl.BlockSpec((None, tk, c), lambda i, j, k: (j, k, 0)),
               pl.BlockSpec((tm, c), lambda i, j, k: (i, j)), NN, (tm, c), dep=dep)


def _mm_shard_cols_nt(name, d, w, tm=2048, tn=512, dep=None):
    m = d.shape[0]
    nj, n, c = w.shape
    tm, tn = _tile(m, tm), _tile(n, tn)
    return _mm(name, d, w, (m, n), (m // tm, n // tn, nj),
               pl.BlockSpec((tm, c), lambda i, j, k: (i, k)), pl.BlockSpec((None, tn, c), lambda i, j, k: (k, j, 0)),
               pl.BlockSpec((tm, tn), lambda i, j, k: (i, j)), NT, (tm, tn), dep=dep)


def _mm_shard_cols_tn(name, a, d, nj, tm=1024, ts=2048):
    s, m = a.shape
    c = d.shape[1] // nj
    tm, ts = _tile(m, tm), _tile(s, ts)
    return _mm(name, a, d, (nj, m, c), (nj, m // tm, s // ts),
               pl.BlockSpec((ts, tm), lambda j, i, k: (k, i)), pl.BlockSpec((ts, c), lambda j, i, k: (k, j)),
               pl.BlockSpec((None, tm, c), lambda j, i, k: (j, i, 0)), TN, (tm, c))


def _ffn_up(x1, w_up_t, tm=512, tk=2048, dep=None):
    s, d = x1.shape
    tm, tk = _tile(s, tm), _tile(d, tk)
    return _mm("ffn_up", x1, w_up_t, (N_CHIPS, s, FS), (N_CHIPS, s // tm, d // tk),
               pl.BlockSpec((tm, tk), lambda j, i, k: (i, k)), pl.BlockSpec((None, FS, tk), lambda j, i, k: (j, 0, k)),
               pl.BlockSpec((None, tm, FS), lambda j, i, k: (j, i, 0)), NT, (tm, FS), dep=dep)


def _ffn_down(act, w_down, tm=1024, tn=512):
    _, s, _ = act.shape
    d = w_down.shape[2]
    tm, tn = _tile(s, tm), _tile(d, tn)
    return _mm("ffn_down", act, w_down, (s, d), (s // tm, d // tn, 2),
               pl.BlockSpec((None, tm, FS), lambda i, j, k: (k, i, 0)), pl.BlockSpec((None, FS, tn), lambda i, j, k: (k, 0, j)),
               pl.BlockSpec((tm, tn), lambda i, j, k: (i, j)), NN, (tm, tn))


def _ffn_down_dact(df, w_down, tm=512, tk=2048):
    s, d = df.shape
    tm, tk = _tile(s, tm), _tile(d, tk)
    return _mm("ffn_down_dact", df, w_down, (2, s, FS), (2, s // tm, d // tk),
               pl.BlockSpec((tm, tk), lambda j, i, k: (i, k)), pl.BlockSpec((None, FS, tk), lambda j, i, k: (j, 0, k)),
               pl.BlockSpec((None, tm, FS), lambda j, i, k: (j, i, 0)), NT, (tm, FS))


def _ffn_down_dw(act, df, tn=512, ts=2048):
    _, s, _ = act.shape
    d = df.shape[1]
    tn, ts = _tile(d, tn), _tile(s, ts)
    return _mm("ffn_down_dw", act, df, (2, FS, d), (2, d // tn, s // ts),
               pl.BlockSpec((None, ts, FS), lambda p, j, k: (p, k, 0)), pl.BlockSpec((ts, tn), lambda p, j, k: (k, j)),
               pl.BlockSpec((None, FS, tn), lambda p, j, k: (p, 0, j)), TN, (FS, tn))


def _ffn_up_dx(dh, w_up_t, tm=1024, tn=1024, dep=None):
    s = dh.shape[2]
    d = w_up_t.shape[2]
    tm, tn = _tile(s, tm), _tile(d, tn)
    return _mm("ffn_up_dx", dh, w_up_t, (s, d), (s // tm, d // tn, N_CHIPS),
               pl.BlockSpec((None, None, tm, FS), lambda i, j, k: (k % 2, k // 2, i, 0)),
               pl.BlockSpec((None, FS, tn), lambda i, j, k: (k, 0, j)),
               pl.BlockSpec((tm, tn), lambda i, j, k: (i, j)), NN, (tm, tn), dep=dep)


def _ffn_up_dw(x1, dh, tn=512, ts=2048):
    s, d = x1.shape
    tn, ts = _tile(d, tn), _tile(s, ts)
    return _mm("ffn_up_dw", dh, x1, (N_CHIPS, FS, d), (N_CHIPS, d // tn, s // ts),
               pl.BlockSpec((None, None, ts, FS), lambda j, i, k: (j % 2, j // 2, k, 0)),
               pl.BlockSpec((ts, tn), lambda j, i, k: (k, i)),
               pl.BlockSpec((None, FS, tn), lambda j, i, k: (j, 0, i)), TN, (FS, tn))


def _rope_tables():
    half = HEAD_DIM // 2
    inv = ROPE_THETA ** (-jnp.arange(half, dtype=F32) / half)
    ang = jnp.arange(SEQ).astype(F32)[:, None] * inv[None, :]
    cos, sin = jnp.cos(ang), jnp.sin(ang)
    cos_t = jnp.tile(cos, (1, LANES // half))
    sin_t = jnp.tile(jnp.concatenate([-sin, sin], axis=1), (1, LANES // HEAD_DIM))
    return cos_t, sin_t


def _rotate_half(t):
    lane = lax.broadcasted_iota(jnp.int32, t.shape, 1)
    first = (lane % HEAD_DIM) < (HEAD_DIM // 2)
    return jnp.where(first, pltpu.roll(t, LANES - HEAD_DIM // 2, 1), pltpu.roll(t, HEAD_DIM // 2, 1))


def _rope(name, src, col_tile0, n_tiles, cos_t, sin_t, out_dtype):
    tm = _tile(SEQ, 512)

    def body(x_ref, c_ref, s_ref, o_ref):
        t = x_ref[...].astype(F32)
        o_ref[...] = (t * c_ref[...] + _rotate_half(t) * s_ref[...]).astype(o_ref.dtype)

    return pl.pallas_call(
        body, name=name, grid=(SEQ // tm, n_tiles),
        in_specs=[pl.BlockSpec((tm, LANES), lambda i, j: (i, j + col_tile0)),
                  pl.BlockSpec((tm, LANES), lambda i, j: (i, 0)), pl.BlockSpec((tm, LANES), lambda i, j: (i, 0))],
        out_specs=pl.BlockSpec((tm, LANES), lambda i, j: (i, j)),
        out_shape=_sds((SEQ, n_tiles * LANES), out_dtype),
        compiler_params=_cparams(("parallel", "parallel")))(src, cos_t, sin_t)


Q_TILES = ATTN_WIDTH // LANES
KV_TILES = KV_WIDTH // LANES
Q_PER_KV_TILE = Q_TILES // KV_TILES
HEADS_PER_KV_TILE = N_Q_HEADS // KV_TILES
K_TILE0 = ATTN_WIDTH // LANES
V_TILE0 = (ATTN_WIDTH + KV_WIDTH) // LANES
N_QBLK = SEQ // ATTN_BLOCK


def _dup_half(t, which):
    lane = lax.broadcasted_iota(jnp.int32, t.shape, 1)
    r = pltpu.roll(t, HEAD_DIM, 1)
    lo = lane < HEAD_DIM
    return jnp.where(lo, t, r) if which == 0 else jnp.where(lo, r, t)


GROUP_HEADS = N_Q_HEADS // N_KV_HEADS
GROUP_ROWS = GROUP_HEADS * ATTN_BLOCK


def _attn_stack(tiles, lo):
    return jnp.concatenate([jnp.where(lo == (hs == 0), t, 0.0) for t in tiles for hs in range(2)], axis=0)


def _attn_unstack(x, j, lo):
    r = 2 * j * ATTN_BLOCK
    return jnp.where(lo, x[r:r + ATTN_BLOCK], x[r + ATTN_BLOCK:r + 2 * ATTN_BLOCK])


def _attn_group_consts(n, sink_ref, head0):
    shape = (GROUP_ROWS, 2 * ATTN_BLOCK)
    qi = lax.broadcasted_iota(jnp.int32, shape, 0) % ATTN_BLOCK
    col = lax.broadcasted_iota(jnp.int32, shape, 1)
    valid = ((col < ATTN_BLOCK) & (col <= qi)) | ((col >= ATTN_BLOCK) & (col - ATTN_BLOCK > qi) & (n > 0))
    head = lax.broadcasted_iota(jnp.int32, (GROUP_ROWS, 1), 0) // ATTN_BLOCK
    sinks = jnp.zeros((GROUP_ROWS, 1), F32)
    for i in range(GROUP_HEADS):
        sinks = jnp.where(head == i, sink_ref[head0 + i], sinks)
    return valid, sinks, head


def _attn_probs(qs, k2, valid, sinks):
    s = lax.dot_general(qs, k2, NT, preferred_element_type=F32) * HEAD_DIM ** -0.5
    s = jnp.where(valid, s, NEG)
    m = jnp.maximum(s.max(1, keepdims=True), sinks)
    p = jnp.exp(s - m)
    esink = jnp.exp(sinks - m)
    inv = 1.0 / (p.sum(1, keepdims=True) + esink)
    return p * inv, esink * inv


def _attn_kv(cur, prev, kvl):
    return jnp.concatenate([_dup_half(cur, kvl), _dup_half(prev, kvl)], axis=0).astype(MM_DTYPE)


def _attn_specs():
    blk = (ATTN_BLOCK, LANES)
    wide = (ATTN_BLOCK, Q_PER_KV_TILE * LANES)
    prev = lambda n: jnp.maximum(n - 1, 0)
    q_spec = pl.BlockSpec(wide, lambda t, n: (n, t))
    kc = pl.BlockSpec(blk, lambda t, n: (n, K_TILE0 + t))
    kp = pl.BlockSpec(blk, lambda t, n: (prev(n), K_TILE0 + t))
    vc = pl.BlockSpec(blk, lambda t, n: (n, V_TILE0 + t))
    vp = pl.BlockSpec(blk, lambda t, n: (prev(n), V_TILE0 + t))
    return q_spec, kc, kp, vc, vp, pl.BlockSpec(memory_space=pltpu.SMEM)


def _attn_fwd(qk, h, sinks):
    q_spec, kc_s, kp_s, vc_s, vp_s, smem = _attn_specs()

    def body(sink_ref, q_ref, kc_ref, kp_ref, vc_ref, vp_ref, o_ref, ob_ref):
        t, n = pl.program_id(0), pl.program_id(1)
        lo = lax.broadcasted_iota(jnp.int32, (ATTN_BLOCK, LANES), 1) < HEAD_DIM
        kc, kp = kc_ref[...].astype(F32), kp_ref[...].astype(F32)
        vc, vp = vc_ref[...], vp_ref[...]
        for kvl in range(2):
            valid, sink_rows, _ = _attn_group_consts(n, sink_ref, t * HEADS_PER_KV_TILE + kvl * GROUP_HEADS)
            tiles = [q_ref[:, a * LANES:(a + 1) * LANES].astype(F32) for a in (2 * kvl, 2 * kvl + 1)]
            qs = _attn_stack(tiles, lo).astype(MM_DTYPE)
            pn, _ = _attn_probs(qs, _attn_kv(kc, kp, kvl), valid, sink_rows)
            os_ = lax.dot_general(pn.astype(MM_DTYPE), _attn_kv(vc, vp, kvl), NN, preferred_element_type=F32)
            for j in range(2):
                a = 2 * kvl + j
                o = _attn_unstack(os_, j, lo)
                o_ref[:, a * LANES:(a + 1) * LANES] = o
                ob_ref[:, a * LANES:(a + 1) * LANES] = o.astype(ob_ref.dtype)

    return pl.pallas_call(
        body, name="attn_fwd", grid=(KV_TILES, N_QBLK),
        in_specs=[smem, q_spec, kc_s, kp_s, vc_s, vp_s], out_specs=[q_spec, q_spec],
        out_shape=[_sds((SEQ, ATTN_WIDTH)), _sds((SEQ, ATTN_WIDTH), MM_DTYPE)],
        compiler_params=_cparams(("parallel", "parallel")))(sinks, qk, qk, qk, h, h)


def _attn_bwd(qk, h, sinks, y, dy, dy_tile0, dep=None):
    deps = _as_list(dep)
    q_spec, kc_s, kp_s, vc_s, vp_s, smem = _attn_specs()
    blk = (ATTN_BLOCK, LANES)
    wide = (ATTN_BLOCK, Q_PER_KV_TILE * LANES)
    kv_out = pl.BlockSpec(blk, lambda t, n: (n, t))
    dy_spec = pl.BlockSpec(wide, lambda t, n: (n, t + dy_tile0))

    def body(sink_ref, q_ref, kc_ref, kp_ref, vc_ref, vp_ref, y_ref, dy_ref, *rest):
        dq_ref, dkc_ref, dkp_ref, dvc_ref, dvp_ref, dsk_ref = rest[-6:]
        t, n = pl.program_id(0), pl.program_id(1)
        lo = lax.broadcasted_iota(jnp.int32, blk, 1) < HEAD_DIM
        kc, kp = kc_ref[...].astype(F32), kp_ref[...].astype(F32)
        vc, vp = vc_ref[...], vp_ref[...]
        hrow = lax.broadcasted_iota(jnp.int32, (HEADS_PER_KV_TILE, LANES), 0)
        dsk = jnp.zeros((HEADS_PER_KV_TILE, LANES), F32)
        dk2, dv2 = [], []
        for kvl in range(2):
            valid, sink_rows, head = _attn_group_consts(n, sink_ref, t * HEADS_PER_KV_TILE + kvl * GROUP_HEADS)
            sls = [slice(a * LANES, (a + 1) * LANES) for a in (2 * kvl, 2 * kvl + 1)]
            qs = _attn_stack([q_ref[:, sl].astype(F32) for sl in sls], lo).astype(MM_DTYPE)
            dos = _attn_stack([dy_ref[:, sl] for sl in sls], lo)
            delta = _attn_stack([dy_ref[:, sl] * y_ref[:, sl] for sl in sls], lo).sum(1, keepdims=True)
            dos = dos.astype(MM_DTYPE)
            k2, v2 = _attn_kv(kc, kp, kvl), _attn_kv(vc, vp, kvl)
            pn, psink = _attn_probs(qs, k2, valid, sink_rows)
            dp = lax.dot_general(dos, v2, NT, preferred_element_type=F32)
            ds = (pn * (dp - delta) * HEAD_DIM ** -0.5).astype(MM_DTYPE)
            dqs = lax.dot_general(ds, k2, NN, preferred_element_type=F32)
            for j, sl in enumerate(sls):
                dq_ref[:, sl] = _attn_unstack(dqs, j, lo)
            for acc, x in ((dk2, lax.dot_general(ds, qs, TN, preferred_element_type=F32)),
                           (dv2, lax.dot_general(pn.astype(MM_DTYPE), dos, TN, preferred_element_type=F32))):
                acc.append(x + pltpu.roll(x, HEAD_DIM, 1))
            dsink = psink * delta
            for i in range(GROUP_HEADS):
                dsk = dsk + jnp.where(hrow == kvl * GROUP_HEADS + i, -jnp.sum(jnp.where(head == i, dsink, 0.0)), 0.0)
        for o_ref, src, r in ((dkc_ref, dk2, 0), (dkp_ref, dk2, ATTN_BLOCK), (dvc_ref, dv2, 0), (dvp_ref, dv2, ATTN_BLOCK)):
            o_ref[...] = jnp.where(lo, src[0][r:r + ATTN_BLOCK], src[1][r:r + ATTN_BLOCK])

        @pl.when(n == 0)
        def _():
            dsk_ref[...] = jnp.zeros_like(dsk_ref)

        dsk_ref[...] += dsk

    kv_shape = _sds((SEQ, KV_WIDTH))
    return pl.pallas_call(
        body, name="attn_bwd", grid=(KV_TILES, N_QBLK),
        in_specs=[smem, q_spec, kc_s, kp_s, vc_s, vp_s, q_spec, dy_spec] + [ANY] * len(deps),
        out_specs=[q_spec, kv_out, kv_out, kv_out, kv_out,
                   pl.BlockSpec((None, HEADS_PER_KV_TILE, LANES), lambda t, n: (t, 0, 0))],
        out_shape=[_sds((SEQ, ATTN_WIDTH)), kv_shape, kv_shape, kv_shape, kv_shape,
                   _sds((KV_TILES, HEADS_PER_KV_TILE, LANES))],
        compiler_params=_cparams(("parallel", "arbitrary")))(sinks, qk, qk, qk, h, h, y, dy, *deps)


def _attn_dh(dq, dkc, dkp, dvc, dvp, cos_t, nsin_t):
    n_tiles = Q_TILES + 2 * KV_TILES
    nxt = lambda n: jnp.minimum(n + 1, N_QBLK - 1)

    def body(dq_ref, kc_ref, kp_ref, vc_ref, vp_ref, c_ref, s_ref, o_ref):
        has_next = pl.program_id(0) < N_QBLK - 1
        cos, sin = c_ref[...], s_ref[...]

        def unrope(t):
            return t * cos + _rotate_half(t) * sin

        for j in range(Q_TILES):
            sl = slice(j * LANES, (j + 1) * LANES)
            o_ref[:, sl] = unrope(dq_ref[:, sl]).astype(o_ref.dtype)
        for j in range(KV_TILES):
            sl = slice(j * LANES, (j + 1) * LANES)
            t = kc_ref[:, sl] + jnp.where(has_next, kp_ref[:, sl], 0.0)
            o_ref[:, ATTN_WIDTH + j * LANES:ATTN_WIDTH + (j + 1) * LANES] = unrope(t).astype(o_ref.dtype)
        o_ref[:, ATTN_WIDTH + KV_WIDTH:] = (vc_ref[...] + jnp.where(has_next, vp_ref[...], 0.0)).astype(o_ref.dtype)

    qb, kb, tb = (ATTN_BLOCK, ATTN_WIDTH), (ATTN_BLOCK, KV_WIDTH), (ATTN_BLOCK, LANES)
    return pl.pallas_call(
        body, name="attn_dh", grid=(N_QBLK,),
        in_specs=[pl.BlockSpec(qb, lambda n: (n, 0)),
                  pl.BlockSpec(kb, lambda n: (n, 0)), pl.BlockSpec(kb, lambda n: (nxt(n), 0)),
                  pl.BlockSpec(kb, lambda n: (n, 0)), pl.BlockSpec(kb, lambda n: (nxt(n), 0)),
                  pl.BlockSpec(tb, lambda n: (n, 0)), pl.BlockSpec(tb, lambda n: (n, 0))],
        out_specs=pl.BlockSpec((ATTN_BLOCK, n_tiles * LANES), lambda n: (n, 0)),
        out_shape=_sds((SEQ, n_tiles * LANES), MM_DTYPE),
        compiler_params=_cparams(("parallel",)))(dq, dkc, dkp, dvc, dvp, cos_t, nsin_t)


POOL_TILE0 = (ATTN_WIDTH + 2 * KV_WIDTH) // POOL_WIDTH


def _shift_rows(x, d, down):
    n = x.shape[0]
    row = lax.broadcasted_iota(jnp.int32, x.shape, 0)
    if down:
        return jnp.where(row >= d, pltpu.roll(x, d, 0), 0.0)
    return jnp.where(row < n - d, pltpu.roll(x, n - d, 0), 0.0)


def _window_sum(x, w, down):
    d = 1
    while d < w:
        x = x + _shift_rows(x, d, down)
        d *= 2
    return x


def _pool_z(u, w):
    t = lax.broadcasted_iota(jnp.int32, u.shape, 0).astype(F32)
    cnt = jnp.minimum(t + 1.0, float(w))
    return _window_sum(u, w, True) / cnt - u, cnt


def _pool_fwd(h, pool_w, pool_scale):
    def body(u_ref, w_ref, s_ref, o_ref):
        for gi, w in enumerate(POOL_WINDOWS):
            sl = slice(gi * POOL_GROUP, (gi + 1) * POOL_GROUP)
            z, _ = _pool_z(u_ref[:, sl], w)
            o_ref[:, sl] = (lax.dot_general(z.astype(MM_DTYPE), w_ref[gi].astype(MM_DTYPE), NN,
                                            preferred_element_type=F32) * s_ref[:, sl]).astype(o_ref.dtype)

    return pl.pallas_call(
        body, name="pool_fwd", grid=(1,),
        in_specs=[pl.BlockSpec((SEQ, POOL_WIDTH), lambda i: (0, POOL_TILE0)),
                  pl.BlockSpec(pool_w.shape, lambda i: (0, 0, 0)), pl.BlockSpec((1, POOL_WIDTH), lambda i: (0, 0))],
        out_specs=pl.BlockSpec((SEQ, POOL_WIDTH), lambda i: (0, 0)),
        out_shape=_sds((SEQ, POOL_WIDTH), MM_DTYPE), compiler_params=_cparams(("arbitrary",)))(h, pool_w, pool_scale)


def _pool_bwd(h, pool_w, pool_scale, dmix, dy_tile0):
    def body(u_ref, w_ref, s_ref, dy_ref, du_ref, dw_ref, ds_ref):
        for gi, w in enumerate(POOL_WINDOWS):
            sl = slice(gi * POOL_GROUP, (gi + 1) * POOL_GROUP)
            z, cnt = _pool_z(u_ref[:, sl], w)
            zb, wb = z.astype(MM_DTYPE), w_ref[gi].astype(MM_DTYPE)
            dy = dy_ref[:, sl]
            zp = lax.dot_general(zb, wb, NN, preferred_element_type=F32)
            ds_ref[:, sl] = jnp.sum(dy * zp, axis=0, keepdims=True)
            dyo = (dy * s_ref[:, sl]).astype(MM_DTYPE)
            dw_ref[gi] = lax.dot_general(zb, dyo, TN, preferred_element_type=F32)
            dz = lax.dot_general(dyo, wb, NT, preferred_element_type=F32)
            du_ref[:, sl] = (_window_sum(dz / cnt, w, False) - dz).astype(du_ref.dtype)

    return pl.pallas_call(
        body, name="pool_bwd", grid=(1,),
        in_specs=[pl.BlockSpec((SEQ, POOL_WIDTH), lambda i: (0, POOL_TILE0)),
                  pl.BlockSpec(pool_w.shape, lambda i: (0, 0, 0)), pl.BlockSpec((1, POOL_WIDTH), lambda i: (0, 0)),
                  pl.BlockSpec((SEQ, POOL_WIDTH), lambda i: (0, dy_tile0))],
        out_specs=[pl.BlockSpec((SEQ, POOL_WIDTH), lambda i: (0, 0)), pl.BlockSpec(pool_w.shape, lambda i: (0, 0, 0)),
                   pl.BlockSpec((1, POOL_WIDTH), lambda i: (0, 0))],
        out_shape=[_sds((SEQ, POOL_WIDTH), MM_DTYPE), _sds(pool_w.shape), _sds((1, POOL_WIDTH))],
        compiler_params=_cparams(("arbitrary",)))(h, pool_w, pool_scale, dmix)


def _ssm_discretize(lr, li, ldt, br, bi):
    dt = jnp.exp(ldt)
    mag = jnp.exp(lr * dt)
    ar, ai = mag * jnp.cos(li * dt), mag * jnp.sin(li * dt)
    nr, ni = ar - 1.0, ai
    den = lr * lr + li * li
    zr = (nr * lr + ni * li) / den
    zi = (ni * lr - nr * li) / den
    return ar, ai, zr * br - zi * bi, zr * bi + zi * br


def _ssm_prep(lr, li, ldt, br, bi):
    def body(lr_ref, li_ref, ldt_ref, br_ref, bi_ref, ar_ref, ai_ref, bbr_ref, bbi_ref):
        outs = _ssm_discretize(lr_ref[...], li_ref[...], ldt_ref[...], br_ref[...], bi_ref[...])
        for o, v in zip((ar_ref, ai_ref, bbr_ref, bbi_ref), outs):
            o[...] = v

    row, mat = _sds((1, SSM_CH)), _sds((SSM_GROUP, SSM_CH))
    return pl.pallas_call(body, name="ssm_prep", out_shape=[row, row, mat, mat])(lr, li, ldt, br, bi)


def _ssm_prep_bwd(lr, li, ldt, br, bi, dar8, dai8, dbbr, dbbi):
    def body(lr_ref, li_ref, ldt_ref, br_ref, bi_ref, dar_ref, dai_ref, dbbr_ref, dbbi_ref, *outs):
        args = (lr_ref[...], li_ref[...], ldt_ref[...], br_ref[...], bi_ref[...])
        _, vjp = jax.vjp(_ssm_discretize, *args)
        cot = (jnp.sum(dar_ref[...], axis=0, keepdims=True), jnp.sum(dai_ref[...], axis=0, keepdims=True),
               dbbr_ref[...], dbbi_ref[...])
        for o, v in zip(outs, vjp(cot)):
            o[...] = v

    row, mat = _sds((1, SSM_CH)), _sds((SSM_GROUP, SSM_CH))
    return pl.pallas_call(body, name="ssm_prep_bwd", out_shape=[row, row, row, mat, mat])(
        lr, li, ldt, br, bi, dar8, dai8, dbbr, dbbi)


def _ssm_diag(name, full):
    tiles = SCAN_CW // LANES
    groups = LANES // SSM_STATE
    col = lambda t, part: (t // tiles) * 2 * tiles + part * tiles + t % tiles

    def body(x_ref, o_ref):
        lane = lax.broadcasted_iota(jnp.int32, (SSM_GROUP, LANES), 1)
        out = x_ref[0:SSM_GROUP, :]
        for k in range(1, groups):
            out = jnp.where(lane >= k * SSM_STATE, x_ref[k * SSM_GROUP:(k + 1) * SSM_GROUP, :], out)
        o_ref[...] = out

    return pl.pallas_call(
        body, name=name, grid=(SSM_CH // LANES, 2),
        in_specs=[pl.BlockSpec((groups * SSM_GROUP, LANES), lambda t, part: (t, col(t, part)))],
        out_specs=pl.BlockSpec((SSM_GROUP, LANES), lambda t, part: (0, col(t, part))),
        out_shape=_sds((SSM_GROUP, 2 * SSM_CH)), compiler_params=_cparams(("parallel", "parallel")))(full)


def _scan_layout(re, im):
    r = re.shape[0]
    return jnp.stack([re.reshape(r, SCAN_NB, SCAN_CW), im.reshape(r, SCAN_NB, SCAN_CW)], axis=2).reshape(r, 2 * SSM_CH)


def _scan_unlayout(x):
    r = x.shape[0]
    x = x.reshape(r, SCAN_NB, 2, SCAN_CW)
    return x[:, :, 0].reshape(r, SSM_CH), x[:, :, 1].reshape(r, SSM_CH)


def _time_permute(u):
    s, c = u.shape
    return u.reshape(SUBLANES, s // SUBLANES, c).transpose(1, 0, 2).reshape(s, c)


def _time_unpermute(u):
    s, c = u.shape
    return u.reshape(s // SUBLANES, SUBLANES, c).transpose(1, 0, 2).reshape(s, c)


def _ssm_scan(name, a_vec, x, reverse, s_prev=None):
    nsteps = SEQ // SUBLANES
    cw = SCAN_CW
    with_da = s_prev is not None

    def body(a_ref, x_ref, *rest):
        if with_da:
            s_ref, o_ref, da_ref = rest
        else:
            o_ref, = rest
        ar = jnp.broadcast_to(a_ref[:, :cw], (SUBLANES, cw))
        ai = jnp.broadcast_to(a_ref[:, cw:], (SUBLANES, cw))
        seg = lax.broadcasted_iota(jnp.int32, (SUBLANES, cw), 0)

        def toward(v):
            if reverse:
                return jnp.where(seg < SUBLANES - 1, pltpu.roll(v, SUBLANES - 1, 0), 0.0)
            return jnp.where(seg >= 1, pltpu.roll(v, 1, 0), 0.0)

        def rows(j):
            jj = nsteps - 1 - j if reverse else j
            return pl.ds(pl.multiple_of(jj * SUBLANES, SUBLANES), SUBLANES)

        def cmul(pr, pi, qr, qi):
            return pr * qr - pi * qi, pr * qi + pi * qr

        def local(j, c):
            sr, si = c
            r = rows(j)
            mr, mi = cmul(ar, ai, sr, si)
            return mr + x_ref[r, :cw], mi + x_ref[r, cw:]

        zero = jnp.zeros((SUBLANES, cw), F32)
        fr, fi = lax.fori_loop(0, nsteps, local, (zero, zero))

        def power(_, c):
            return cmul(ar, ai, *c)

        pr, pi = lax.fori_loop(0, nsteps - 1, power, (ar, ai))
        tr, ti = fr, fi
        for _ in range(SUBLANES - 1):
            mr, mi = cmul(pr, pi, toward(tr), toward(ti))
            tr, ti = fr + mr, fi + mi
        init = (toward(tr), toward(ti))

        def full(j, c):
            r = rows(j)
            if with_da:
                sr, si, dar, dai = c
            else:
                sr, si = c
            mr, mi = cmul(ar, ai, sr, si)
            sr, si = mr + x_ref[r, :cw], mi + x_ref[r, cw:]
            o_ref[r, :cw] = sr
            o_ref[r, cw:] = si
            if not with_da:
                return sr, si
            jj = nsteps - 1 - j
            rp = pl.ds(pl.multiple_of(jnp.maximum(jj - 1, 0) * SUBLANES, SUBLANES), SUBLANES)
            last = pl.ds((nsteps - 1) * SUBLANES, SUBLANES)
            first = jj == 0
            spr = jnp.where(first, jnp.where(seg >= 1, pltpu.roll(s_ref[last, :cw], 1, 0), 0.0), s_ref[rp, :cw])
            spi = jnp.where(first, jnp.where(seg >= 1, pltpu.roll(s_ref[last, cw:], 1, 0), 0.0), s_ref[rp, cw:])
            return sr, si, dar + sr * spr + si * spi, dai + si * spr - sr * spi

        if with_da:
            _, _, dar, dai = lax.fori_loop(0, nsteps, full, init + (zero, zero))
            da_ref[:, :cw] = dar
            da_ref[:, cw:] = dai
        else:
            lax.fori_loop(0, nsteps, full, init)

    blk = pl.BlockSpec((SEQ, 2 * cw), lambda b: (0, b))
    a_spec = pl.BlockSpec((1, 2 * cw), lambda b: (0, b))
    in_specs, args = [a_spec, blk], [a_vec, x]
    out_specs, out_shape = blk, _sds((SEQ, 2 * SSM_CH))
    if with_da:
        in_specs, args = in_specs + [blk], args + [s_prev]
        out_specs = [blk, pl.BlockSpec((SUBLANES, 2 * cw), lambda b: (0, b))]
        out_shape = [out_shape, _sds((SUBLANES, 2 * SSM_CH))]
    return pl.pallas_call(body, name=name, grid=(SCAN_NB,), in_specs=in_specs, out_specs=out_specs,
                          out_shape=out_shape, compiler_params=_cparams(("parallel",)))(*args)


def _ssm_gelu(yp, up, dvec):
    tm = _tile(SEQ, 512)

    def body(y_ref, u_ref, d_ref, yf_ref, g_ref):
        yf = y_ref[...] + d_ref[...] * u_ref[...]
        yf_ref[...] = yf
        g_ref[...] = jax.nn.gelu(yf).astype(g_ref.dtype)

    blk = pl.BlockSpec((tm, SSM_WIDTH), lambda i: (i, 0))
    row = pl.BlockSpec((1, SSM_WIDTH), lambda i: (0, 0))
    return pl.pallas_call(body, name="ssm_gelu", grid=(SEQ // tm,), in_specs=[blk, blk, row], out_specs=[blk, blk],
                          out_shape=[_sds((SEQ, SSM_WIDTH)), _sds((SEQ, SSM_WIDTH), MM_DTYPE)],
                          compiler_params=_cparams(("parallel",)))(yp, up, dvec)


def _ssm_gelu_bwd(yf, dgy, up, dvec):
    tm = _tile(SEQ, 512)

    def body(yf_ref, dg_ref, u_ref, d_ref, dyf_ref, du_ref, dd_ref):
        _, vjp = jax.vjp(jax.nn.gelu, yf_ref[...])
        dyf, = vjp(dg_ref[...])
        dyf_ref[...] = dyf.astype(dyf_ref.dtype)
        du_ref[...] = d_ref[...] * dyf

        @pl.when(pl.program_id(0) == 0)
        def _():
            dd_ref[...] = jnp.zeros_like(dd_ref)

        dd_ref[...] += jnp.sum(dyf * u_ref[...], axis=0, keepdims=True)

    blk = pl.BlockSpec((tm, SSM_WIDTH), lambda i: (i, 0))
    row = pl.BlockSpec((1, SSM_WIDTH), lambda i: (0, 0))
    return pl.pallas_call(body, name="ssm_gelu_bwd", grid=(SEQ // tm,), in_specs=[blk, blk, blk, row],
                          out_specs=[blk, blk, row],
                          out_shape=[_sds((SEQ, SSM_WIDTH), MM_DTYPE), _sds((SEQ, SSM_WIDTH)), _sds((1, SSM_WIDTH))],
                          compiler_params=_cparams(("arbitrary",)))(yf, dgy, up, dvec)


def _glu(ab):
    return ab[:, :SSM_WIDTH] * jax.nn.sigmoid(ab[:, SSM_WIDTH:])


def _ssm_glu(ab):
    tm = _tile(SEQ, 512)

    def body(ab_ref, o_ref):
        o_ref[...] = _glu(ab_ref[...]).astype(o_ref.dtype)

    return pl.pallas_call(body, name="ssm_glu", grid=(SEQ // tm,),
                          in_specs=[pl.BlockSpec((tm, 2 * SSM_WIDTH), lambda i: (i, 0))],
                          out_specs=pl.BlockSpec((tm, SSM_WIDTH), lambda i: (i, 0)),
                          out_shape=_sds((SEQ, SSM_WIDTH), MM_DTYPE), compiler_params=_cparams(("parallel",)))(ab)


def _ssm_glu_bwd(ab, dout):
    tm = _tile(SEQ, 512)

    def body(ab_ref, do_ref, dab_ref):
        _, vjp = jax.vjp(_glu, ab_ref[...])
        dab, = vjp(do_ref[...])
        dab_ref[...] = dab.astype(dab_ref.dtype)

    return pl.pallas_call(body, name="ssm_glu_bwd", grid=(SEQ // tm,),
                          in_specs=[pl.BlockSpec((tm, 2 * SSM_WIDTH), lambda i: (i, 0)),
                                    pl.BlockSpec((tm, SSM_WIDTH), lambda i: (i, 0))],
                          out_specs=pl.BlockSpec((tm, 2 * SSM_WIDTH), lambda i: (i, 0)),
                          out_shape=_sds((SEQ, 2 * SSM_WIDTH), MM_DTYPE), compiler_params=_cparams(("parallel",)))(ab, dout)


def _add2(name, a, b, out_dtype):
    tm = _tile(a.shape[0], 512)

    def body(a_ref, b_ref, o_ref):
        o_ref[...] = (a_ref[...] + b_ref[...]).astype(o_ref.dtype)

    blk = pl.BlockSpec((tm, a.shape[1]), lambda i: (i, 0))
    return pl.pallas_call(body, name=name, grid=(a.shape[0] // tm,), in_specs=[blk, blk], out_specs=blk,
                          out_shape=_sds(a.shape, out_dtype), compiler_params=_cparams(("parallel",)))(a, b)


def _layer_norm(r, g, b):
    mu = r.mean(-1, keepdims=True)
    var = jnp.square(r - mu).mean(-1, keepdims=True)
    return (r - mu) * lax.rsqrt(var + LN_EPS) * g + b


def _ln_fwd(name, x, y, g, b):
    tm = _tile(SEQ, 256)

    def body(x_ref, y_ref, g_ref, b_ref, r_ref, o_ref, ob_ref):
        r = DEEPNORM_ALPHA * x_ref[...] + y_ref[...]
        r_ref[...] = r
        o = _layer_norm(r, g_ref[...], b_ref[...])
        o_ref[...] = o
        ob_ref[...] = o.astype(ob_ref.dtype)

    blk = pl.BlockSpec((tm, D_MODEL), lambda i: (i, 0))
    row = pl.BlockSpec((1, D_MODEL), lambda i: (0, 0))
    return pl.pallas_call(body, name=name, grid=(SEQ // tm,), in_specs=[blk, blk, row, row], out_specs=[blk, blk, blk],
                          out_shape=[_sds((SEQ, D_MODEL))] * 2 + [_sds((SEQ, D_MODEL), MM_DTYPE)],
                          compiler_params=_cparams(("parallel",)))(x, y, g, b)


def _ln_bwd(name, r, g, b, da, db=None, dep=None):
    tm = _tile(SEQ, 256)
    two = db is not None
    deps = _as_list(dep)

    def body(r_ref, g_ref, b_ref, da_ref, *rest):
        dr_ref, drb_ref, dg_ref, dbeta_ref = rest[-4:]
        dout = DEEPNORM_ALPHA * da_ref[...] + rest[0][...] if two else da_ref[...]
        _, vjp = jax.vjp(_layer_norm, r_ref[...], g_ref[...], b_ref[...])
        dr, dg, dbeta = vjp(dout)
        dr_ref[...] = dr
        drb_ref[...] = dr.astype(drb_ref.dtype)

        @pl.when(pl.program_id(0) == 0)
        def _():
            dg_ref[...] = jnp.zeros_like(dg_ref)
            dbeta_ref[...] = jnp.zeros_like(dbeta_ref)

        dg_ref[...] += dg
        dbeta_ref[...] += dbeta

    blk = pl.BlockSpec((tm, D_MODEL), lambda i: (i, 0))
    row = pl.BlockSpec((1, D_MODEL), lambda i: (0, 0))
    args = [r, g, b, da] + ([db] if two else []) + deps
    return pl.pallas_call(body, name=name, grid=(SEQ // tm,),
                          in_specs=[blk, row, row, blk] + ([blk] if two else []) + [ANY] * len(deps),
                          out_specs=[blk, blk, row, row],
                          out_shape=[_sds((SEQ, D_MODEL)), _sds((SEQ, D_MODEL), MM_DTYPE), _sds((1, D_MODEL)), _sds((1, D_MODEL))],
                          compiler_params=_cparams(("arbitrary",)))(*args)


FFN_TM = 128
HALO = SUBLANES


def _conv_taps(cur, halo):
    row = lax.broadcasted_iota(jnp.int32, cur.shape, 0)
    h1 = jnp.where(row == 0, halo[HALO - 1:HALO, :], pltpu.roll(cur, 1, 0))
    h2 = jnp.where(row == 0, halo[HALO - 2:HALO - 1, :], jnp.where(row == 1, halo[HALO - 1:HALO, :], pltpu.roll(cur, 2, 0)))
    return h1, h2


def _conv_fwd(cur, halo, w_ref, b_ref):
    h1, h2 = _conv_taps(cur, halo)
    return b_ref[...] + h2 * w_ref[0:1, :] + h1 * w_ref[1:2, :] + cur * w_ref[2:3, :], h1, h2


def _gate(val, gate):
    return jax.nn.silu(gate) * val


def _ffn_specs(tm):
    nb = tm // HALO
    cur = lambda off: pl.BlockSpec((None, tm, FS), lambda p, i: (p + off, i, 0))
    halo = lambda off: pl.BlockSpec((None, HALO, FS), lambda p, i: (p + off, jnp.maximum(i * nb - 1, 0), 0))
    cw = lambda off: pl.BlockSpec((None, CONV_WIDTH, FS), lambda p, i: (p + off, 0, 0))
    cb = lambda off: pl.BlockSpec((None, 1, FS), lambda p, i: (p + off, 0, 0))
    return cur, halo, cw, cb


def _ffn_act(hf, conv_w, conv_b):
    tm = _tile(SEQ, FFN_TM, SUBLANES)
    cur, halo, cw, cb = _ffn_specs(tm)

    def body(v_ref, vh_ref, g_ref, gh_ref, wv_ref, wg_ref, bv_ref, bg_ref, o_ref):
        live = pl.program_id(1) > 0
        vh = jnp.where(live, vh_ref[...], 0.0)
        gh = jnp.where(live, gh_ref[...], 0.0)
        val, _, _ = _conv_fwd(v_ref[...], vh, wv_ref, bv_ref)
        gate, _, _ = _conv_fwd(g_ref[...], gh, wg_ref, bg_ref)
        o_ref[...] = _gate(val, gate).astype(o_ref.dtype)

    return pl.pallas_call(
        body, name="ffn_act", grid=(2, SEQ // tm),
        in_specs=[cur(0), halo(0), cur(2), halo(2), cw(0), cw(2), cb(0), cb(2)],
        out_specs=pl.BlockSpec((None, tm, FS), lambda p, i: (p, i, 0)),
        out_shape=_sds((2, SEQ, FS), MM_DTYPE), compiler_params=_cparams(("parallel", "parallel")))(
            hf, hf, hf, hf, conv_w, conv_w, conv_b, conv_b)


def _ffn_act_bwd(hf, conv_w, conv_b, dact, dep=None):
    tm = _tile(SEQ, FFN_TM, SUBLANES)
    nb, nblk = tm // HALO, SEQ // tm
    cur, halo, cw, cb = _ffn_specs(tm)
    nxt = lambda off: pl.BlockSpec((None, HALO, FS), lambda p, i: (p + off, jnp.minimum((i + 1) * nb, SEQ // HALO - 1), 0))
    deps = _as_list(dep)

    def body(v_ref, vh_ref, vn_ref, g_ref, gh_ref, gn_ref, wv_ref, wg_ref, bv_ref, bg_ref, da_ref, dan_ref, *rest):
        dh_ref, dw_ref, dbias_ref = rest[-3:]
        dwv_ref, dwg_ref = dw_ref.at[0], dw_ref.at[1]
        dbv_ref, dbg_ref = dbias_ref.at[0], dbias_ref.at[1]
        i = pl.program_id(1)
        live, more = i > 0, i < nblk - 1
        vh = jnp.where(live, vh_ref[...], 0.0)
        gh = jnp.where(live, gh_ref[...], 0.0)
        vcur, gcur = v_ref[...], g_ref[...]
        val, v1, v2 = _conv_fwd(vcur, vh, wv_ref, bv_ref)
        gate, g1, g2 = _conv_fwd(gcur, gh, wg_ref, bg_ref)
        _, vjp = jax.vjp(_gate, val, gate)
        dval, dgate = vjp(da_ref[...])
        val_n, _, _ = _conv_fwd(vn_ref[...], vcur[tm - HALO:, :], wv_ref, bv_ref)
        gate_n, _, _ = _conv_fwd(gn_ref[...], gcur[tm - HALO:, :], wg_ref, bg_ref)
        _, vjp_n = jax.vjp(_gate, val_n, gate_n)
        dval_n, dgate_n = vjp_n(dan_ref[...])
        row = lax.broadcasted_iota(jnp.int32, dval.shape, 0)
        for k, (d, dn, w_ref) in enumerate(((dval, dval_n, wv_ref), (dgate, dgate_n, wg_ref))):
            dn = jnp.where(more, dn, 0.0)
            d1 = jnp.where(row == tm - 1, dn[0:1, :], pltpu.roll(d, tm - 1, 0))
            d2 = jnp.where(row == tm - 1, dn[1:2, :], jnp.where(row == tm - 2, dn[0:1, :], pltpu.roll(d, tm - 2, 0)))
            dh_ref[k] = (d * w_ref[2:3, :] + d1 * w_ref[1:2, :] + d2 * w_ref[0:1, :]).astype(dh_ref.dtype)

        @pl.when(i == 0)
        def _():
            dw_ref[...] = jnp.zeros_like(dw_ref)
            dbias_ref[...] = jnp.zeros_like(dbias_ref)

        for d, taps, dwk_ref, dbk_ref in ((dval, (v2, v1, vcur), dwv_ref, dbv_ref), (dgate, (g2, g1, gcur), dwg_ref, dbg_ref)):
            for k in range(CONV_WIDTH):
                dwk_ref[k:k + 1, :] += jnp.sum(d * taps[k], axis=0, keepdims=True)
            dbk_ref[...] += jnp.sum(d, axis=0, keepdims=True)

    return pl.pallas_call(
        body, name="ffn_act_bwd", grid=(2, SEQ // tm),
        in_specs=[cur(0), halo(0), nxt(0), cur(2), halo(2), nxt(2), cw(0), cw(2), cb(0), cb(2),
                  pl.BlockSpec((None, tm, FS), lambda p, i: (p, i, 0)), nxt(0)] + [ANY] * len(deps),
        out_specs=[pl.BlockSpec((None, 2, tm, FS), lambda p, i: (p, 0, i, 0)),
                   pl.BlockSpec((None, 2, CONV_WIDTH, FS), lambda p, i: (p, 0, 0, 0)),
                   pl.BlockSpec((None, 2, 1, FS), lambda p, i: (p, 0, 0, 0))],
        out_shape=[_sds((2, 2, SEQ, FS), MM_DTYPE), _sds((2, 2, CONV_WIDTH, FS)), _sds((2, 2, 1, FS))],
        compiler_params=_cparams(("parallel", "arbitrary")))(hf, hf, hf, hf, hf, hf, conv_w, conv_w, conv_b, conv_b, dact,
                                                              dact, *deps)


def _loss(y, target):
    tm = _tile(SEQ, 256)

    def body(y_ref, t_ref, dy_ref, l_ref):
        err = y_ref[...] - t_ref[...]
        dy_ref[...] = err * (1.0 / D_MODEL)

        @pl.when(pl.program_id(0) == 0)
        def _():
            l_ref[...] = jnp.zeros_like(l_ref)

        l_ref[...] += 0.5 * jnp.sum(jnp.mean(jnp.square(err), axis=-1))

    blk = pl.BlockSpec((tm, D_MODEL), lambda i: (i, 0))
    return pl.pallas_call(body, name="loss", grid=(SEQ // tm,), in_specs=[blk, blk],
                          out_specs=[blk, pl.BlockSpec((SUBLANES, LANES), lambda i: (0, 0))],
                          out_shape=[_sds((SEQ, D_MODEL)), _sds((SUBLANES, LANES))],
                          compiler_params=_cparams(("arbitrary",)))(y, target)


ADAM_BLOCK_BYTES = 3 << 19
ELEMENTWISE_COLS = 1024


def _adamw_math(w, g, m, v):
    nm = ADAM_B1 * m + (1.0 - ADAM_B1) * g
    nv = ADAM_B2 * v + (1.0 - ADAM_B2) * jnp.square(g)
    m_hat = nm / (1.0 - ADAM_B1 ** ADAM_STEP)
    v_hat = nv / (1.0 - ADAM_B2 ** ADAM_STEP)
    return -ADAM_LR * (m_hat / (jnp.sqrt(v_hat) + ADAM_EPS) + ADAM_WD * w), nm, nv


def _adamw(name, w, g, m, v):
    r, c = w.shape
    tr = _tile(r, max(SUBLANES, ADAM_BLOCK_BYTES // (4 * c)), SUBLANES)

    def body(w_ref, g_ref, m_ref, v_ref, d_ref, nm_ref, nv_ref):
        d_ref[...], nm_ref[...], nv_ref[...] = _adamw_math(w_ref[...], g_ref[...], m_ref[...], v_ref[...])

    blk = pl.BlockSpec((tr, c), lambda i: (i, 0))
    return pl.pallas_call(body, name=name, grid=(r // tr,), in_specs=[blk] * 4, out_specs=[blk] * 3,
                          out_shape=[_sds((r, c))] * 3, compiler_params=_cparams(("parallel",)))(w, g, m, v)


def _adamw_big(name, l, c_idx, w, m, v, g_own, g_got, prev):
    depth, _, r, c = w.shape
    tc = _tile(c, ELEMENTWISE_COLS)
    tr = _tile(r, max(SUBLANES, ADAM_BLOCK_BYTES // (4 * tc)), SUBLANES)

    def body(c_ref, w_ref, m_ref, v_ref, own_ref, got_ref, *rest):
        g_ref, d_ref, nm_ref, nv_ref = rest[-4:]
        g = jnp.where(pl.program_id(0) == c_ref[0], own_ref[...], got_ref[...])
        g_ref[...] = g
        d_ref[...], nm_ref[...], nv_ref[...] = _adamw_math(w_ref[...], g, m_ref[...], v_ref[...])

    stacked = pl.BlockSpec((None, None, tr, tc), lambda h, i, j, cr: (l, h, i, j))
    own = pl.BlockSpec((tr, tc), lambda h, i, j, cr: (jnp.where(h == cr[0], i, 0), jnp.where(h == cr[0], j, 0)))
    got = pl.BlockSpec((tr, tc), lambda h, i, j, cr: (jnp.where(h == cr[0], 0, i), jnp.where(h == cr[0], 0, j)))
    grid_spec = pltpu.PrefetchScalarGridSpec(
        num_scalar_prefetch=1, grid=(2, r // tr, c // tc),
        in_specs=[stacked] * 3 + [own, got] + ([ANY] * 4 if prev else []), out_specs=[stacked] * 4)
    return pl.pallas_call(
        body, name=name, grid_spec=grid_spec, out_shape=[_sds((depth, 2, r, c))] * 4,
        input_output_aliases={6 + k: k for k in range(4)} if prev else {},
        compiler_params=_cparams(("arbitrary", "arbitrary", "arbitrary")))(c_idx, w, m, v, g_own, g_got, *(prev or ()))


ANY = pl.BlockSpec(memory_space=pl.ANY)


def _place():
    x, y, c = lax.axis_index("x"), lax.axis_index("y"), lax.axis_index("c")
    chips = [(1 - x, y), (x, 1 - y), (1 - x, 1 - y)]
    return x, y, c, chips


def _cast_place(name, w, l, me_idx, out_dtype):
    _, _, r, c = w.shape
    tr = _tile(r, max(2 * SUBLANES, COPY_BLOCK_BYTES // (4 * c)), 2 * SUBLANES)

    def body(me_ref, w_ref, o_ref):
        o_ref[...] = w_ref[...].astype(o_ref.dtype)

    grid_spec = pltpu.PrefetchScalarGridSpec(
        num_scalar_prefetch=1, grid=(2, r // tr),
        in_specs=[pl.BlockSpec((None, None, tr, c), lambda h, i, me: (l, h, i, 0))],
        out_specs=pl.BlockSpec((None, None, tr, c), lambda h, i, me: (me[0], h, i, 0)))
    return pl.pallas_call(body, name=name, grid_spec=grid_spec, out_shape=_sds((N_CHIPS, 2, r, c), out_dtype),
                          compiler_params=_cparams(("parallel", "parallel")))(me_idx, w)


HBM = pl.BlockSpec(memory_space=pltpu.HBM)
SEM = pl.BlockSpec(memory_space=pltpu.SEMAPHORE)
TOKEN = (SUBLANES, LANES)


def _comm_call(name, body, hbm, sems_in=(), after=None, sems_out=(), token=False):
    n, k = len(hbm), len(sems_out)
    ins = [pltpu.with_memory_space_constraint(a, pltpu.HBM) for a in hbm] + list(sems_in)
    in_specs = [HBM] * n + [SEM] * len(sems_in)
    if after is not None:
        ins.append(after)
        in_specs.append(ANY)
    out_shape = [pltpu.SemaphoreType.DMA((s,)) for s in sems_out] + [pltpu.HBM(a.shape, a.dtype) for a in hbm]
    out_specs = [SEM] * k + [HBM] * n
    if token:
        out_shape.append(_sds(TOKEN))
        out_specs.append(pl.BlockSpec(memory_space=pltpu.VMEM))
    res = pl.pallas_call(
        body, name=name, in_specs=in_specs, out_specs=out_specs, out_shape=out_shape,
        input_output_aliases={i: k + i for i in range(n)},
        compiler_params=pltpu.CompilerParams(has_side_effects=pltpu.SideEffectType.DATAFLOW_SIDE_EFFECTING))(*ins)
    return list(res[:k]), list(res[k:k + n]), (res[k + n] if token else None)


def _remote(src, dst, send, recv, to):
    return pltpu.make_async_remote_copy(src_ref=src, dst_ref=dst, send_sem=send, recv_sem=recv, device_id=to,
                                        device_id_type=MESH)


def _gather_start(name, bufs, after=None, pair=False):
    n = len(bufs)
    o = n + (after is not None)

    def body(*refs):
        ins, (send, recv), token = refs[:n], refs[o:o + 2], refs[-1]
        x, y, c, chips = _place()
        for i in range(n):
            mine = ins[i].at[2 * x + y, c]
            for k, chip in enumerate(chips):
                _remote(mine, mine, send.at[3 * i + k], recv.at[3 * i + k], (*chip, c)).start()
            if pair:
                _remote(mine, mine, send.at[3 * n + i], recv.at[3 * n + i], (x, y, 1 - c)).start()
        token[...] = jnp.zeros(TOKEN, F32)

    n_sems = (3 + pair) * n
    return _comm_call(name, body, bufs, after=after, sems_out=(n_sems, n_sems), token=True)


def _gather_forward(name, bufs, sems, after, pair=False):
    n = len(bufs)
    o = n + 2 + (after is not None)

    def body(*refs):
        ins, (send, recv), (send2, recv2), token = refs[:n], refs[n:n + 2], refs[o:o + 2], refs[-1]
        x, y, c, chips = _place()
        for i in range(n):
            mine = ins[i].at[2 * x + y, c]
            for k, chip in enumerate(chips):
                land = ins[i].at[2 * chip[0] + chip[1], c]
                first = _remote(mine, land, send.at[3 * i + k], recv.at[3 * i + k], (*chip, c))
                first.wait_send()
                first.wait_recv()
                _remote(land, land, send2.at[3 * i + k], recv2.at[3 * i + k], (x, y, 1 - c)).start()
            if pair:
                own = _remote(mine, ins[i].at[2 * x + y, 1 - c], send.at[3 * n + i], recv.at[3 * n + i], (x, y, 1 - c))
                own.wait_send()
                own.wait_recv()
        token[...] = jnp.zeros(TOKEN, F32)

    return _comm_call(name, body, bufs, sems_in=sems, after=after, sems_out=(3 * n, 3 * n), token=True)


def _gather_finish(name, bufs, sems, after):
    n = len(bufs)

    def body(*refs):
        ins, (send, recv) = refs[:n], refs[n:n + 2]
        x, y, c, chips = _place()
        for i in range(n):
            for k, chip in enumerate(chips):
                idx = 2 * chip[0] + chip[1]
                cp = _remote(ins[i].at[idx, c], ins[i].at[idx, 1 - c], send.at[3 * i + k], recv.at[3 * i + k], (x, y, 1 - c))
                cp.wait_send()
                cp.wait_recv()

    return _comm_call(name, body, bufs, sems_in=sems, after=after)[1]


def _swap_start(name, grads):
    n = len(grads)
    lands = [lax.empty((g.shape[0],) + g.shape[2:], g.dtype) for g in grads]

    def body(*refs):
        ins, lnd, (send, recv), token = refs[:n], refs[n:2 * n], refs[2 * n:2 * n + 2], refs[-1]
        x, y, c, _ = _place()
        for i in range(n):
            _remote(ins[i].at[:, 1 - c], lnd[i], send.at[i], recv.at[i], (x, y, 1 - c)).start()
        token[...] = jnp.zeros(TOKEN, F32)

    return _comm_call(name, body, list(grads) + lands, sems_out=(n, n), token=True)


def _swap_wait(name, hbm, sems, after):
    n = len(hbm) // 2

    def body(*refs):
        ins, lnd, (send, recv) = refs[:n], refs[n:2 * n], refs[2 * n:2 * n + 2]
        x, y, c, _ = _place()
        for i in range(n):
            cp = _remote(ins[i].at[:, 1 - c], lnd[i], send.at[i], recv.at[i], (x, y, 1 - c))
            cp.wait_send()
            cp.wait_recv()

    out = _comm_call(name, body, hbm, sems_in=sems, after=after)[1]
    return out[:n], out[n:]


def _pair_add(name, g, got, cm_idx):
    nk, _, r, c = g.shape
    tr = _tile(r, max(2 * SUBLANES, COPY_BLOCK_BYTES // (4 * c)), 2 * SUBLANES)

    def body(cm_ref, g_ref, x_ref, o_ref, land_ref):
        s = (g_ref[...] + x_ref[...]).astype(o_ref.dtype)
        o_ref[...] = s

        @pl.when(pl.program_id(1) == cm_ref[1])
        def _():
            land_ref[...] = s

    grid_spec = pltpu.PrefetchScalarGridSpec(
        num_scalar_prefetch=1, grid=(r // tr, nk),
        in_specs=[pl.BlockSpec((None, None, tr, c), lambda i, k, cm: (k, cm[0], i, 0)),
                  pl.BlockSpec((None, tr, c), lambda i, k, cm: (k, i, 0))],
        out_specs=[pl.BlockSpec((None, tr, c), lambda i, k, cm: (k, i, 0)),
                   pl.BlockSpec((None, tr, c), lambda i, k, cm: (cm[1], i, 0))])
    return pl.pallas_call(body, name=name, grid_spec=grid_spec, out_shape=[_sds((nk, r, c), BF16)] * 2,
                          compiler_params=_cparams(("parallel", "arbitrary")))(cm_idx, g, got)


def _scatter_start(name, parts, lands):
    n = len(parts)

    def body(*refs):
        ins, lnd, (send, recv), token = refs[:n], refs[n:2 * n], refs[2 * n:2 * n + 2], refs[-1]
        x, y, c, chips = _place()
        for i in range(n):
            for k, chip in enumerate(chips):
                _remote(ins[i].at[2 * chip[0] + chip[1]], lnd[i].at[2 * x + y], send.at[3 * i + k], recv.at[3 * i + k],
                        (*chip, c)).start()
        token[...] = jnp.zeros(TOKEN, F32)

    return _comm_call(name, body, list(parts) + list(lands), sems_out=(3 * n, 3 * n), token=True)


def _scatter_wait(name, hbm, sems, after):
    n = len(hbm) // 2

    def body(*refs):
        ins, lnd, (send, recv) = refs[:n], refs[n:2 * n], refs[2 * n:2 * n + 2]
        x, y, c, chips = _place()
        for i in range(n):
            for k, chip in enumerate(chips):
                idx = 2 * chip[0] + chip[1]
                cp = _remote(ins[i].at[idx], lnd[i].at[idx], send.at[3 * i + k], recv.at[3 * i + k], (*chip, c))
                cp.wait_send()
                cp.wait_recv()

    out = _comm_call(name, body, hbm, sems_in=sems, after=after)[1]
    return out[:n], out[n:]


def _sum_leading(name, x, out_dtype=F32):
    nk, r, c = x.shape
    tc = _tile(c, ELEMENTWISE_COLS)
    tr = _tile(r, max(2 * SUBLANES, COPY_BLOCK_BYTES // (nk * tc * x.dtype.itemsize)), 2 * SUBLANES)

    def body(x_ref, o_ref):
        acc = x_ref[0].astype(F32)
        for k in range(1, nk):
            acc = acc + x_ref[k].astype(F32)
        o_ref[...] = acc.astype(o_ref.dtype)

    return pl.pallas_call(body, name=name, grid=(r // tr, c // tc),
                          in_specs=[pl.BlockSpec((nk, tr, tc), lambda i, j: (0, i, j))],
                          out_specs=pl.BlockSpec((tr, tc), lambda i, j: (i, j)), out_shape=_sds((r, c), out_dtype),
                          compiler_params=_cparams(("parallel", "parallel")))(x)


def _exchange_start(name, halves):
    n = len(halves)
    lands = [lax.empty(h.shape, h.dtype) for h in halves]

    def body(*refs):
        ins, lnd, (send, recv), token = refs[:n], refs[n:2 * n], refs[2 * n:2 * n + 2], refs[-1]
        x, y, c, _ = _place()
        for i in range(n):
            _remote(ins[i], lnd[i], send.at[i], recv.at[i], (x, y, 1 - c)).start()
        token[...] = jnp.zeros(TOKEN, F32)

    return _comm_call(name, body, list(halves) + lands, sems_out=(n, n), token=True)


def _exchange_wait(name, hbm, sems, after):
    n = len(hbm) // 2

    def body(*refs):
        ins, lnd, (send, recv) = refs[:n], refs[n:2 * n], refs[2 * n:2 * n + 2]
        x, y, c, _ = _place()
        for i in range(n):
            cp = _remote(ins[i], lnd[i], send.at[i], recv.at[i], (x, y, 1 - c))
            cp.wait_send()
            cp.wait_recv()

    out = _comm_call(name, body, hbm, sems_in=sems, after=after)[1]
    return out[:n], out[n:]


SMALL = ("attn_sinks", "pool_w", "pool_scale", "ssm_lam_re", "ssm_lam_im", "ssm_log_dt", "ssm_b_re", "ssm_b_im",
         "ssm_c_re", "ssm_c_im", "ssm_d", "ln1_g", "ln1_b", "ffn_conv_b", "ln2_g", "ln2_b")
BIG = ("w_in", "ssm_glu_w", "w_out", "ffn_w_up", "ffn_conv_w", "ffn_w_down")
ALL_W = ("w_in", "attn_sinks", "pool_w", "pool_scale", "ssm_lam_re", "ssm_lam_im", "ssm_log_dt", "ssm_b_re", "ssm_b_im",
         "ssm_c_re", "ssm_c_im", "ssm_d", "ssm_glu_w", "w_out", "ln1_g", "ln1_b", "ffn_w_up", "ffn_conv_w", "ffn_conv_b",
         "ffn_w_down", "ln2_g", "ln2_b")
PACK_UNIT = SUBLANES * LANES


def _padded(n):
    return -(-n // PACK_UNIT) * PACK_UNIT


def _pack(arrs):
    cols = []
    for name in SMALL:
        a = arrs[name].reshape(DEPTH, -1)
        cols.append(jnp.pad(a, ((0, 0), (0, _padded(a.shape[1]) - a.shape[1]))))
    return jnp.concatenate(cols, axis=1).reshape(-1, LANES)


def _unpack(packed, shapes):
    flat = packed.reshape(DEPTH, -1)
    out, off = {}, 0
    for name in SMALL:
        n = math.prod(shapes[name][1:])
        out[name] = flat[:, off:off + n].reshape(shapes[name])
        off += _padded(n)
    return out


def _b_rows(b):
    return b.transpose(2, 0, 1).reshape(SSM_GROUP, SSM_CH)


def _b_unrows(b):
    return b.reshape(SSM_GROUP, SSM_N_GROUPS, SSM_STATE).transpose(1, 2, 0)


def _block_diag_in(bb):
    eye = jnp.eye(SSM_N_GROUPS, dtype=F32)
    b3 = bb.reshape(SSM_GROUP, SSM_N_GROUPS, SSM_STATE)
    return jnp.einsum("hgp,gk->ghkp", b3, eye).reshape(SSM_WIDTH, SSM_CH)


def _c_unrows(c):
    return c.reshape(SSM_GROUP, SSM_N_GROUPS, SSM_STATE).transpose(1, 0, 2)


def _block_diag_out(cc):
    eye = jnp.eye(SSM_N_GROUPS, dtype=F32)
    return jnp.einsum("ghp,gk->gpkh", cc, eye).reshape(SSM_CH, SSM_WIDTH)


def _rows_layout(re, im):
    n = re.shape[1]
    return jnp.stack([re.reshape(SCAN_NB, SCAN_CW, n), im.reshape(SCAN_NB, SCAN_CW, n)], axis=1).reshape(2 * SSM_CH, n)


H_POOL0 = ATTN_WIDTH + 2 * KV_WIDTH
H_SSM0 = H_POOL0 + POOL_WIDTH


def _ssm_params(p):
    lr = p["ssm_lam_re"].reshape(1, SSM_CH)
    li = p["ssm_lam_im"].reshape(1, SSM_CH)
    ldt = jnp.repeat(p["ssm_log_dt"], SSM_STATE).reshape(1, SSM_CH)
    return lr, li, ldt, _b_rows(p["ssm_b_re"]), _b_rows(p["ssm_b_im"])


def _layer_fwd(x, xb, p, wg, rope_t, dep, mid):
    cos_t, sin_t = rope_t
    h = _mm_shard_cols("in_proj", xb, wg["w_in"], dep=dep)
    qk = _rope("rope_fwd", h, 0, Q_TILES + KV_TILES, cos_t, sin_t, MM_DTYPE)
    y_attn, y_attn_b = _attn_fwd(qk, h, p["attn_sinks"])
    y_pool = _pool_fwd(h, p["pool_w"], p["pool_scale"].reshape(1, POOL_WIDTH))
    ssm_in = _ssm_params(p)
    ar, ai, bbr, bbi = _ssm_prep(*ssm_in)
    bd = _scan_layout(_block_diag_in(bbr), _block_diag_in(bbi)).astype(MM_DTYPE)
    cc = _rows_layout(_block_diag_out(p["ssm_c_re"]), -_block_diag_out(p["ssm_c_im"])).astype(MM_DTYPE)
    dvec = p["ssm_d"].reshape(1, SSM_WIDTH)
    up = _time_permute(h[:, H_SSM0:])
    xx = _mm_nn("ssm_bu", up, bd, tn=1024)
    ss = _ssm_scan("ssm_scan_fwd", _scan_layout(ar, ai), xx, False)
    yp = _mm_nn("ssm_cs", ss, cc, tk=1024)
    yf, gy = _ssm_gelu(yp, up, dvec)
    ab = _mm_shard_cols("ssm_glu_proj", gy, wg["ssm_glu_w"])
    y_ssm = _time_unpermute(_ssm_glu(ab))
    mix = jnp.concatenate([y_attn_b, y_pool, y_ssm], axis=1)
    mixo = _mm_nn("out_proj", mix, wg["w_out"].reshape(MIX_WIDTH, D_MODEL))
    r1, x1, x1b = _ln_fwd("ln1_fwd", x, mixo, p["ln1_g"].reshape(1, D_MODEL), p["ln1_b"].reshape(1, D_MODEL))
    tokens = mid(x1b)
    hf = _ffn_up(x1b, wg["ffn_w_up"], dep=tokens)
    conv_b = p["ffn_conv_b"].reshape(N_CHIPS, 1, FS)
    act = _ffn_act(hf, wg["ffn_conv_w"], conv_b)
    f = _ffn_down(act, wg["ffn_w_down"].reshape(2, FS, D_MODEL))
    r2, x2, x2b = _ln_fwd("ln2_fwd", x1, f, p["ln2_g"].reshape(1, D_MODEL), p["ln2_b"].reshape(1, D_MODEL))
    saved = dict(xb=xb, h=h, qk=qk, y_attn=y_attn, ssm_in=ssm_in, ar=ar, ai=ai, bd=bd, cc=cc, dvec=dvec, up=up, ss=ss, yf=yf,
                 gy=gy, ab=ab, mix=mix, r1=r1, x1b=x1b, hf=hf, conv_b=conv_b, act=act, r2=r2)
    return x2, x2b, saved


def _layer_bwd(da, db, p, wg, sv, rope_t, run, start):
    cos_t, sin_t = rope_t
    small = {}
    dr2, dr2b, dg, dbeta = _ln_bwd("ln2_bwd" if db is not None else "ln2_bwd_last", sv["r2"], p["ln2_g"].reshape(1, D_MODEL),
                                   p["ln2_b"].reshape(1, D_MODEL), da, db, dep=run("h0", None))
    small["ln2_g"], small["ln2_b"] = dg, dbeta
    w_down = wg["ffn_w_down"].reshape(2, FS, D_MODEL)
    dact = _ffn_down_dact(dr2b, w_down)
    dw_down = _ffn_down_dw(sv["act"], dr2b)
    dh_ffn, dcw, dcb = _ffn_act_bwd(sv["hf"], wg["ffn_conv_w"], sv["conv_b"], dact, dep=run("h1", dw_down))
    dconv_w = dcw.transpose(1, 0, 2, 3).reshape(N_CHIPS, CONV_WIDTH, FS)
    small["ffn_conv_b"] = dcb.transpose(1, 0, 2, 3)
    dw_up = _ffn_up_dw(sv["x1b"], dh_ffn)
    tok = [start("ffn", {"ffn_w_up": dw_up, "ffn_conv_w": dconv_w,
                         "ffn_w_down": dw_down.reshape(N_CHIPS, FS // 2, D_MODEL)})] + run("h2", dw_up)
    dx1_ffn = _ffn_up_dx(dh_ffn, wg["ffn_w_up"], dep=tok)
    dr1, dr1b, dg, dbeta = _ln_bwd("ln1_bwd", sv["r1"], p["ln1_g"].reshape(1, D_MODEL), p["ln1_b"].reshape(1, D_MODEL), dr2,
                                   dx1_ffn, dep=tok)
    small["ln1_g"], small["ln1_b"] = dg, dbeta
    w_out = wg["w_out"].reshape(MIX_WIDTH, D_MODEL)
    dw_out = _mm_tn("out_proj_dw", sv["mix"], dr1b)
    dmix = _mm_nt("out_proj_dx", dr1b, w_out, dep=run("h3", dw_out))
    dq, dkc, dkp, dvc, dvp, dsk = _attn_bwd(sv["qk"], sv["h"], p["attn_sinks"], sv["y_attn"], dmix, 0)
    small["attn_sinks"] = dsk[:, :, 0]
    dh_attn = _attn_dh(dq, dkc, dkp, dvc, dvp, cos_t, -sin_t)
    dh_pool, dpw, dps = _pool_bwd(sv["h"], p["pool_w"], p["pool_scale"].reshape(1, POOL_WIDTH), dmix, ATTN_WIDTH // POOL_WIDTH)
    small["pool_w"], small["pool_scale"] = dpw, dps
    dout_p = _time_permute(dmix[:, ATTN_WIDTH + POOL_WIDTH:])
    dab = _ssm_glu_bwd(sv["ab"], dout_p)
    dgy = _mm_shard_cols_nt("ssm_glu_dx", dab, wg["ssm_glu_w"])
    dw_glu = _mm_shard_cols_tn("ssm_glu_dw", sv["gy"], dab, N_CHIPS)
    dyf, du1, dd = _ssm_gelu_bwd(sv["yf"], dgy, sv["up"], sv["dvec"])
    small["ssm_d"] = dd
    dss = _mm_nt("ssm_cs_dx", dyf, sv["cc"], tn=1024)
    dcre, dcim = _scan_unlayout(_ssm_diag("ssm_c_diag", _mm_tn("ssm_cs_dw", dyf, sv["ss"], tn=1024)))
    small["ssm_c_re"], small["ssm_c_im"] = _c_unrows(dcre), -_c_unrows(dcim)
    gg, da8 = _ssm_scan("ssm_scan_bwd", _scan_layout(sv["ar"], -sv["ai"]), dss, True, sv["ss"])
    du2 = _mm_nt("ssm_bu_dx", gg, sv["bd"], tk=1024)
    dbbr, dbbi = _scan_unlayout(_ssm_diag("ssm_b_diag", _mm_tn("ssm_bu_dw", sv["up"], gg, tn=1024)))
    dar8, dai8 = _scan_unlayout(da8)
    dlr, dli, dldt, dbr, dbi = _ssm_prep_bwd(*sv["ssm_in"], dar8, dai8, dbbr, dbbi)
    small["ssm_lam_re"], small["ssm_lam_im"] = dlr, dli
    small["ssm_log_dt"] = dldt.reshape(SSM_N_GROUPS, SSM_STATE).sum(axis=1)
    small["ssm_b_re"], small["ssm_b_im"] = _b_unrows(dbr), _b_unrows(dbi)
    dh_ssm = _time_unpermute(_add2("ssm_du", du1, du2, MM_DTYPE))
    dh = jnp.concatenate([dh_attn, dh_pool, dh_ssm], axis=1)
    dx_in = _mm_shard_cols_nt("in_proj_dx", dh, wg["w_in"], dep=run("h4", dh))
    dw_in = _mm_shard_cols_tn("in_proj_dw", sv["xb"], dh, N_CHIPS)
    start("mix", {"w_in": dw_in, "ssm_glu_w": dw_glu, "w_out": dw_out.reshape(N_CHIPS, MIX_WIDTH // N_CHIPS, D_MODEL)})
    return dr1, dx_in, small


CONV_PAD = 2 * SUBLANES


def _halved(name, a):
    if name == "ffn_conv_w":
        a = jnp.pad(a, ((0, 0), (0, CONV_PAD - CONV_WIDTH), (0, 0)))
    return a.reshape(a.shape[0], 2, a.shape[1] // 2, a.shape[2])


def _unhalved(name, a):
    a = a.reshape(a.shape[:-3] + (2 * a.shape[-2], a.shape[-1]))
    return a[..., :CONV_WIDTH, :] if name == "ffn_conv_w" else a


class _Reduce:
    def __init__(self, tag, grads, cm_idx):
        self.tag, self.cm_idx, self.names = tag, cm_idx, tuple(grads)
        g4 = [_halved(name, grads[name]) for name in self.names]
        self.sems, self.hbm, self.token = _swap_start("grad_swap_start_" + tag, g4)

    def swapped(self, after):
        g4, got = _swap_wait("grad_swap_wait_" + self.tag, self.hbm, self.sems, after)
        parts, lands = zip(*[_pair_add("grad_pair_add", g, x, self.cm_idx) for g, x in zip(g4, got)])
        self.sems, self.hbm, self.token = _scatter_start("grad_scatter_start_" + self.tag, parts, lands)
        return self.token

    def scattered(self, after):
        _, recv = _scatter_wait("grad_scatter_wait_" + self.tag, self.hbm, self.sems, after)
        halves = [_sum_leading("grad_chip_sum", r) for r in recv]
        self.sems, self.hbm, self.token = _exchange_start("grad_exchange_start_" + self.tag, halves)
        return self.token

    def finish(self, after):
        return _exchange_wait("grad_exchange_wait_" + self.tag, self.hbm, self.sems, after)


def kernel(x, w_in, attn_sinks, pool_w, pool_scale, ssm_lam_re, ssm_lam_im, ssm_log_dt, ssm_b_re, ssm_b_im, ssm_c_re, ssm_c_im, ssm_d, ssm_glu_w, w_out, ln1_g, ln1_b, ffn_w_up, ffn_conv_w, ffn_conv_b, ffn_w_down, ln2_g, ln2_b, loss_target, m_w_in, m_attn_sinks, m_pool_w, m_pool_scale, m_ssm_lam_re, m_ssm_lam_im, m_ssm_log_dt, m_ssm_b_re, m_ssm_b_im, m_ssm_c_re, m_ssm_c_im, m_ssm_d, m_ssm_glu_w, m_w_out, m_ln1_g, m_ln1_b, m_ffn_w_up, m_ffn_conv_w, m_ffn_conv_b, m_ffn_w_down, m_ln2_g, m_ln2_b, v_w_in, v_attn_sinks, v_pool_w, v_pool_scale, v_ssm_lam_re, v_ssm_lam_im, v_ssm_log_dt, v_ssm_b_re, v_ssm_b_im, v_ssm_c_re, v_ssm_c_im, v_ssm_d, v_ssm_glu_w, v_w_out, v_ln1_g, v_ln1_b, v_ffn_w_up, v_ffn_conv_w, v_ffn_conv_b, v_ffn_w_down, v_ln2_g, v_ln2_b):
    w = dict(w_in=w_in, attn_sinks=attn_sinks, pool_w=pool_w, pool_scale=pool_scale, ssm_lam_re=ssm_lam_re,
             ssm_lam_im=ssm_lam_im, ssm_log_dt=ssm_log_dt, ssm_b_re=ssm_b_re, ssm_b_im=ssm_b_im, ssm_c_re=ssm_c_re,
             ssm_c_im=ssm_c_im, ssm_d=ssm_d, ssm_glu_w=ssm_glu_w, w_out=w_out, ln1_g=ln1_g, ln1_b=ln1_b, ffn_w_up=ffn_w_up,
             ffn_conv_w=ffn_conv_w, ffn_conv_b=ffn_conv_b, ffn_w_down=ffn_w_down, ln2_g=ln2_g, ln2_b=ln2_b)
    m = dict(w_in=m_w_in, attn_sinks=m_attn_sinks, pool_w=m_pool_w, pool_scale=m_pool_scale, ssm_lam_re=m_ssm_lam_re,
             ssm_lam_im=m_ssm_lam_im, ssm_log_dt=m_ssm_log_dt, ssm_b_re=m_ssm_b_re, ssm_b_im=m_ssm_b_im, ssm_c_re=m_ssm_c_re,
             ssm_c_im=m_ssm_c_im, ssm_d=m_ssm_d, ssm_glu_w=m_ssm_glu_w, w_out=m_w_out, ln1_g=m_ln1_g, ln1_b=m_ln1_b,
             ffn_w_up=m_ffn_w_up, ffn_conv_w=m_ffn_conv_w, ffn_conv_b=m_ffn_conv_b, ffn_w_down=m_ffn_w_down, ln2_g=m_ln2_g,
             ln2_b=m_ln2_b)
    v = dict(w_in=v_w_in, attn_sinks=v_attn_sinks, pool_w=v_pool_w, pool_scale=v_pool_scale, ssm_lam_re=v_ssm_lam_re,
             ssm_lam_im=v_ssm_lam_im, ssm_log_dt=v_ssm_log_dt, ssm_b_re=v_ssm_b_re, ssm_b_im=v_ssm_b_im, ssm_c_re=v_ssm_c_re,
             ssm_c_im=v_ssm_c_im, ssm_d=v_ssm_d, ssm_glu_w=v_ssm_glu_w, w_out=v_w_out, ln1_g=v_ln1_g, ln1_b=v_ln1_b,
             ffn_w_up=v_ffn_w_up, ffn_conv_w=v_ffn_conv_w, ffn_conv_b=v_ffn_conv_b, ffn_w_down=v_ffn_w_down, ln2_g=v_ln2_g,
             ln2_b=v_ln2_b)
    c_pos = lax.axis_index("c").astype(jnp.int32)
    chip = (2 * lax.axis_index("x") + lax.axis_index("y")).astype(jnp.int32)
    c_idx, chip_idx, cm_idx = c_pos.reshape(1), chip.reshape(1), jnp.stack([c_pos, chip])
    rope_t = _rope_tables()
    xs = x.reshape(SEQ, D_MODEL)
    xb = xs.astype(MM_DTYPE)
    for t in (w, m, v):
        t["ffn_w_up"] = jnp.swapaxes(t["ffn_w_up"], 1, 2)
    wh, mh, vh = ({n: _halved(n, t[n]) for n in BIG} for t in (w, m, v))

    def place(l):
        return [_cast_place("place_" + n, wh[n], l, chip_idx, F32 if n == "ffn_conv_w" else MM_DTYPE) for n in BIG]

    n_mix = BIG.index("ffn_w_up")

    def gather_start(l, after):
        bufs = place(l)
        return (_gather_start("gather_start_%d_mix" % l, bufs[:n_mix], after),
                _gather_start("gather_start_%d_ffn" % l, bufs[n_mix:], after))

    def gather_wait(l, group, started, after):
        sems, bufs, _ = _gather_forward("gather_forward_%d_%s" % (l, group), started[1], started[0], after)
        bufs = _gather_finish("gather_finish_%d_%s" % (l, group), bufs, sems, after)
        names = BIG[:n_mix] if group == "mix" else BIG[n_mix:]
        return bufs, {n: _unhalved(n, g) for n, g in zip(names, bufs)}

    flight = gather_start(0, None)
    gathered, saved = [gather_wait(0, "mix", flight[0], None)[1]], []
    for l in range(DEPTH):
        nxt = {}

        def mid(after):
            bufs, wg_ffn = gather_wait(l, "ffn", flight[1], after)
            gathered[l].update(wg_ffn)
            if l + 1 == DEPTH:
                return []
            nxt["flight"] = gather_start(l + 1, bufs[0])
            return [nxt["flight"][0][2], nxt["flight"][1][2]]

        xs, xb, sv = _layer_fwd(xs, xb, {n: w[n][l] for n in SMALL}, gathered[l], rope_t, flight[1][2] if l == 0 else None, mid)
        saved.append(sv)
        if l + 1 < DEPTH:
            flight = nxt["flight"]
            gathered.append(gather_wait(l + 1, "mix", flight[0], xb)[1])
    dy, loss_tile = _loss(xs, loss_target.reshape(SEQ, D_MODEL))
    loss = lax.psum(loss_tile[0, 0], ("x", "y", "c"))

    big_out = {n: None for n in BIG}
    small_g = {n: [None] * DEPTH for n in SMALL}
    agenda = {}
    tail = []
    plan = {"ffn": (("h3", 0), ("end", 0), ("h1", -1)), "mix": (("h1", -1), ("h3", -1), ("h4", -1))}
    tail_rank = {("mix", 0): 0, ("ffn", 2): 1, ("mix", 1): 2, ("mix", 2): 3}
    started = []

    def book(l, group, red):
        def update(after):
            names, own, got = red.names, *red.finish(after)
            for n, o, g in zip(names, own, got):
                big_out[n] = _adamw_big("adamw_" + n, l, c_idx, wh[n], mh[n], vh[n], o, g, big_out[n])
            return [big_out[names[-1]][0]] if l == 0 else []

        steps = (lambda a: [red.swapped(a)], lambda a: [red.scattered(a)], update)
        for k, ((hook, dl), step) in enumerate(zip(plan[group], steps)):
            if l + dl >= 0:
                agenda.setdefault((l + dl, hook), []).append(step)
            else:
                tail.append((tail_rank[group, k], step))

    def run_at(l):
        return lambda hook, after: [t for step in agenda.pop((l, hook), []) for t in step(after)]

    def start_at(l):
        def start(group, grads):
            red = _Reduce("%s_%d" % (group, l), grads, cm_idx)
            book(l, group, red)
            started.append(red.token)
            return red.token
        return start

    da, db, carry = dy, None, []
    for l in reversed(range(DEPTH)):
        agenda.setdefault((l, "h0"), []).append(lambda after, carry=carry: carry)
        da, db, small = _layer_bwd(da, db, {n: w[n][l] for n in SMALL}, gathered[l], saved[l], rope_t, run_at(l), start_at(l))
        for n in SMALL:
            small_g[n][l] = small[n].reshape(w[n].shape[1:])
        carry = run_at(l)("end", db) + started[-1:]
    shapes = {n: w[n].shape for n in SMALL}
    part = _pack({n: jnp.stack(small_g[n]) for n in SMALL})
    slots = lax.dynamic_update_slice(jnp.zeros((N_CHIPS, 2) + part.shape, F32), part[None, None], (chip, c_pos, 0, 0))
    small_sems, small_bufs, after = _gather_start("gather_start_small", [slots], pair=True)
    for _, step in sorted(tail, key=lambda rs: rs[0]):
        after = (step(after) or [after])[-1]
    grad_x = _ln_in_grad(da, db).reshape(x.shape)
    small_sems, small_bufs, _ = _gather_forward("gather_forward_small", small_bufs, small_sems, after, pair=True)
    small_bufs = _gather_finish("gather_finish_small", small_bufs, small_sems, grad_x)
    g_small = _sum_leading("small_grad_sum", small_bufs[0].reshape((N_DEV,) + part.shape))
    upd = _adamw("adamw_small", _pack(w), g_small, _pack(m), _pack(v))
    small_out = [_unpack(a, shapes) for a in (g_small,) + tuple(upd)]

    outs = [loss, grad_x]
    for kind in range(4):
        for n in ALL_W:
            if n in SMALL:
                outs.append(small_out[kind][n])
            else:
                o = _unhalved(n, big_out[n][kind])
                outs.append(jnp.swapaxes(o, 1, 2) if n == "ffn_w_up" else o)
    return tuple(outs)


def _ln_in_grad(dr1, dx_in):
    tm = _tile(SEQ, 512)

    def body(a_ref, b_ref, o_ref):
        o_ref[...] = DEEPNORM_ALPHA * a_ref[...] + b_ref[...]

    blk = pl.BlockSpec((tm, D_MODEL), lambda i: (i, 0))
    return pl.pallas_call(body, name="grad_x", grid=(SEQ // tm,), in_specs=[blk, blk], out_specs=blk,
                          out_shape=_sds((SEQ, D_MODEL)), compiler_params=_cparams(("parallel",)))(dr1, dx_in)
```

```python
import functools
import math

import jax
import jax.numpy as jnp
from jax import lax
from jax.experimental import pallas as pl
from jax.experimental.pallas import tpu as pltpu

F32 = jnp.float32
BF16 = jnp.bfloat16
MM_DTYPE = BF16

D_MODEL = 2048
SEQ = 2048
DEPTH = 4
D_FF = 5504
HEAD_DIM = 64
N_Q_HEADS = D_MODEL // 2 // HEAD_DIM
N_KV_HEADS = N_Q_HEADS // 4
ATTN_WIDTH = N_Q_HEADS * HEAD_DIM
KV_WIDTH = N_KV_HEADS * HEAD_DIM
ATTN_BLOCK = 128
ROPE_THETA = 10000.0
POOL_WINDOWS = (2, 4, 8, 16)
POOL_WIDTH = D_MODEL // 4
POOL_GROUP = POOL_WIDTH // len(POOL_WINDOWS)
SSM_WIDTH = D_MODEL // 4
SSM_GROUP = 16
SSM_N_GROUPS = SSM_WIDTH // SSM_GROUP
SSM_STATE = 64
SSM_CH = SSM_N_GROUPS * SSM_STATE
MIX_WIDTH = ATTN_WIDTH + POOL_WIDTH + SSM_WIDTH
IN_WIDTH = ATTN_WIDTH + 2 * KV_WIDTH + POOL_WIDTH + SSM_WIDTH
CONV_WIDTH = 3
LN_EPS = 1e-5
DEEPNORM_ALPHA = (2 * DEPTH) ** 0.25
ADAM_LR = 0.001
ADAM_B1 = 0.9
ADAM_B2 = 0.999
ADAM_EPS = 1e-08
ADAM_WD = 0.01
ADAM_STEP = 10

N_CHIPS = 4
N_DEV = 8
FS = 2 * D_FF // N_CHIPS
IN_S = IN_WIDTH // N_CHIPS
GLU_S = 2 * SSM_WIDTH // N_CHIPS
LANES = 128
SUBLANES = 8
SCAN_CW = 256
SCAN_NB = SSM_CH // SCAN_CW
SCAN_UNROLL = 4
VMEM_LIMIT = 56 * 1024 * 1024
COPY_BLOCK_BYTES = 6 * 1024 * 1024
NEG = -1e30

NN = (((1,), (0,)), ((), ()))
NT = (((1,), (1,)), ((), ()))
TN = (((0,), (0,)), ((), ()))
MESH = pl.DeviceIdType.MESH


def _tile(n, pref, mult=LANES):
    best = None
    for t in range(mult, min(n, pref) + 1, mult):
        if n % t == 0:
            best = t
    return n if best is None else best


def _cparams(sem):
    return pltpu.CompilerParams(dimension_semantics=sem, vmem_limit_bytes=VMEM_LIMIT)


def _sds(shape, dtype=F32):
    return jax.ShapeDtypeStruct(tuple(shape), dtype)


def _as_list(x):
    return [] if x is None else list(x) if isinstance(x, (list, tuple)) else [x]


def _mm(name, a, b, out_shape, grid, a_spec, b_spec, o_spec, dims, acc_shape, out_dtype=F32, dep=None):
    nk = grid[2]
    deps = _as_list(dep)

    def product(a_ref, b_ref):
        return lax.dot_general(a_ref[...].astype(MM_DTYPE), b_ref[...].astype(MM_DTYPE), dims, preferred_element_type=F32)

    def body_one(a_ref, b_ref, *rest):
        rest[-1][...] = product(a_ref, b_ref).astype(rest[-1].dtype)

    def body(a_ref, b_ref, *rest):
        o_ref, acc_ref = rest[-2:]
        k = pl.program_id(2)

        @pl.when(k == 0)
        def _():
            acc_ref[...] = product(a_ref, b_ref)

        @pl.when(k > 0)
        def _():
            acc_ref[...] += product(a_ref, b_ref)

        @pl.when(k == nk - 1)
        def _():
            o_ref[...] = acc_ref[...].astype(o_ref.dtype)

    return pl.pallas_call(
        body_one if nk == 1 else body, name=name, grid=grid, in_specs=[a_spec, b_spec] + [ANY] * len(deps),
        out_specs=o_spec, out_shape=_sds(out_shape, out_dtype),
        scratch_shapes=[] if nk == 1 else [pltpu.VMEM(acc_shape, F32)],
        compiler_params=_cparams(("parallel", "parallel", "arbitrary")))(a, b, *deps)


def _mm_nn(name, a, b, tm=2048, tn=512, tk=2048, out_dtype=F32, dep=None):
    m, kk = a.shape
    n = b.shape[1]
    tm, tn, tk = _tile(m, tm), _tile(n, tn), _tile(kk, tk)
    return _mm(name, a, b, (m, n), (m // tm, n // tn, kk // tk),
               pl.BlockSpec((tm, tk), lambda i, j, k: (i, k)), pl.BlockSpec((tk, tn), lambda i, j, k: (k, j)),
               pl.BlockSpec((tm, tn), lambda i, j, k: (i, j)), NN, (tm, tn), out_dtype, dep=dep)


def _mm_nt(name, a, b, tm=2048, tn=512, tk=2048, dep=None):
    m, kk = a.shape
    n = b.shape[0]
    tm, tn, tk = _tile(m, tm), _tile(n, tn), _tile(kk, tk)
    return _mm(name, a, b, (m, n), (m // tm, n // tn, kk // tk),
               pl.BlockSpec((tm, tk), lambda i, j, k: (i, k)), pl.BlockSpec((tn, tk), lambda i, j, k: (j, k)),
               pl.BlockSpec((tm, tn), lambda i, j, k: (i, j)), NT, (tm, tn), dep=dep)


def _mm_tn(name, a, b, tm=1024, tn=1024, ts=2048):
    s, m = a.shape
    n = b.shape[1]
    tm, tn, ts = _tile(m, tm), _tile(n, tn), _tile(s, ts)
    return _mm(name, a, b, (m, n), (m // tm, n // tn, s // ts),
               pl.BlockSpec((ts, tm), lambda i, j, k: (k, i)), pl.BlockSpec((ts, tn), lambda i, j, k: (k, j)),
               pl.BlockSpec((tm, tn), lambda i, j, k: (i, j)), TN, (tm, tn))


def _mm_shard_cols(name, a, w, tm=2048, tk=2048, dep=None):
    m, kk = a.shape
    nj, _, c = w.shape
    tm, tk = _tile(m, tm), _tile(kk, tk)
    return _mm(name, a, w, (m, nj * c), (m // tm, nj, kk // tk),
               pl.BlockSpec((tm, tk), lambda i, j, k: (i, k)), pl.BlockSpec((None, tk, c), lambda i, j, k: (j, k, 0)),
               pl.BlockSpec((tm, c), lambda i, j, k: (i, j)), NN, (tm, c), dep=dep)


def _mm_shard_cols_nt(name, d, w, tm=2048, tn=512, dep=None):
    m = d.shape[0]
    nj, n, c = w.shape
    tm, tn = _tile(m, tm), _tile(n, tn)
    return _mm(name, d, w, (m, n), (m // tm, n // tn, nj),
               pl.BlockSpec((tm, c), lambda i, j, k: (i, k)), pl.BlockSpec((None, tn, c), lambda i, j, k: (k, j, 0)),
               pl.BlockSpec((tm, tn), lambda i, j, k: (i, j)), NT, (tm, tn), dep=dep)


def _mm_shard_cols_tn(name, a, d, nj, tm=1024, ts=2048):
    s, m = a.shape
    c = d.shape[1] // nj
    tm, ts = _tile(m, tm), _tile(s, ts)
    return _mm(name, a, d, (nj, m, c), (nj, m // tm, s // ts),
               pl.BlockSpec((ts, tm), lambda j, i, k: (k, i)), pl.BlockSpec((ts, c), lambda j, i, k: (k, j)),
               pl.BlockSpec((None, tm, c), lambda j, i, k: (j, i, 0)), TN, (tm, c))


def _ffn_up(x1, w_up_t, tm=512, tk=2048, dep=None):
    s, d = x1.shape
    tm, tk = _tile(s, tm), _tile(d, tk)
    return _mm("ffn_up", x1, w_up_t, (N_CHIPS, s, FS), (N_CHIPS, s // tm, d // tk),
               pl.BlockSpec((tm, tk), lambda j, i, k: (i, k)), pl.BlockSpec((None, FS, tk), lambda j, i, k: (j, 0, k)),
               pl.BlockSpec((None, tm, FS), lambda j, i, k: (j, i, 0)), NT, (tm, FS), dep=dep)


def _ffn_down(act, w_down, tm=1024, tn=512):
    _, s, _ = act.shape
    d = w_down.shape[2]
    tm, tn = _tile(s, tm), _tile(d, tn)
    return _mm("ffn_down", act, w_down, (s, d), (s // tm, d // tn, 2),
               pl.BlockSpec((None, tm, FS), lambda i, j, k: (k, i, 0)), pl.BlockSpec((None, FS, tn), lambda i, j, k: (k, 0, j)),
               pl.BlockSpec((tm, tn), lambda i, j, k: (i, j)), NN, (tm, tn))


def _ffn_down_dact(df, w_down, tm=512, tk=2048):
    s, d = df.shape
    tm, tk = _tile(s, tm), _tile(d, tk)
    return _mm("ffn_down_dact", df, w_down, (2, s, FS), (2, s // tm, d // tk),
               pl.BlockSpec((tm, tk), lambda j, i, k: (i, k)), pl.BlockSpec((None, FS, tk), lambda j, i, k: (j, 0, k)),
               pl.BlockSpec((None, tm, FS), lambda j, i, k: (j, i, 0)), NT, (tm, FS))


def _ffn_down_dw(act, df, tn=512, ts=2048):
    _, s, _ = act.shape
    d = df.shape[1]
    tn, ts = _tile(d, tn), _tile(s, ts)
    return _mm("ffn_down_dw", act, df, (2, FS, d), (2, d // tn, s // ts),
               pl.BlockSpec((None, ts, FS), lambda p, j, k: (p, k, 0)), pl.BlockSpec((ts, tn), lambda p, j, k: (k, j)),
               pl.BlockSpec((None, FS, tn), lambda p, j, k: (p, 0, j)), TN, (FS, tn))


def _ffn_up_dx(dh, w_up_t, tm=1024, tn=1024, dep=None):
    s = dh.shape[2]
    d = w_up_t.shape[2]
    tm, tn = _tile(s, tm), _tile(d, tn)
    return _mm("ffn_up_dx", dh, w_up_t, (s, d), (s // tm, d // tn, N_CHIPS),
               pl.BlockSpec((None, None, tm, FS), lambda i, j, k: (k % 2, k // 2, i, 0)),
               pl.BlockSpec((None, FS, tn), lambda i, j, k: (k, 0, j)),
               pl.BlockSpec((tm, tn), lambda i, j, k: (i, j)), NN, (tm, tn), dep=dep)


def _ffn_up_dw(x1, dh, tn=512, ts=2048):
    s, d = x1.shape
    tn, ts = _tile(d, tn), _tile(s, ts)
    return _mm("ffn_up_dw", dh, x1, (N_CHIPS, FS, d), (N_CHIPS, d // tn, s // ts),
               pl.BlockSpec((None, None, ts, FS), lambda j, i, k: (j % 2, j // 2, k, 0)),
               pl.BlockSpec((ts, tn), lambda j, i, k: (k, i)),
               pl.BlockSpec((None, FS, tn), lambda j, i, k: (j, 0, i)), TN, (FS, tn))


def _rope_tables():
    half = HEAD_DIM // 2
    inv = ROPE_THETA ** (-jnp.arange(half, dtype=F32) / half)
    ang = jnp.arange(SEQ).astype(F32)[:, None] * inv[None, :]
    cos, sin = jnp.cos(ang), jnp.sin(ang)
    cos_t = jnp.tile(cos, (1, LANES // half))
    sin_t = jnp.tile(jnp.concatenate([-sin, sin], axis=1), (1, LANES // HEAD_DIM))
    return cos_t, sin_t


def _rotate_half(t):
    lane = lax.broadcasted_iota(jnp.int32, t.shape, 1)
    first = (lane % HEAD_DIM) < (HEAD_DIM // 2)
    return jnp.where(first, pltpu.roll(t, LANES - HEAD_DIM // 2, 1), pltpu.roll(t, HEAD_DIM // 2, 1))


def _rope(name, src, col_tile0, n_tiles, cos_t, sin_t, out_dtype):
    tm = _tile(SEQ, 512)
    assert col_tile0 % n_tiles == 0

    def body(x_ref, c_ref, s_ref, o_ref):
        cos, sin = c_ref[...], s_ref[...]
        for j in range(n_tiles):
            sl = slice(j * LANES, (j + 1) * LANES)
            t = x_ref[:, sl].astype(F32)
            o_ref[:, sl] = (t * cos + _rotate_half(t) * sin).astype(o_ref.dtype)

    wide = n_tiles * LANES
    return pl.pallas_call(
        body, name=name, grid=(SEQ // tm,),
        in_specs=[pl.BlockSpec((tm, wide), lambda i: (i, col_tile0 // n_tiles)),
                  pl.BlockSpec((tm, LANES), lambda i: (i, 0)), pl.BlockSpec((tm, LANES), lambda i: (i, 0))],
        out_specs=pl.BlockSpec((tm, wide), lambda i: (i, 0)),
        out_shape=_sds((SEQ, wide), out_dtype),
        compiler_params=_cparams(("parallel",)))(src, cos_t, sin_t)


Q_TILES = ATTN_WIDTH // LANES
KV_TILES = KV_WIDTH // LANES
Q_PER_KV_TILE = Q_TILES // KV_TILES
HEADS_PER_KV_TILE = N_Q_HEADS // KV_TILES
K_TILE0 = ATTN_WIDTH // LANES
V_TILE0 = (ATTN_WIDTH + KV_WIDTH) // LANES
N_QBLK = SEQ // ATTN_BLOCK


def _dup_half(t, which):
    lane = lax.broadcasted_iota(jnp.int32, t.shape, 1)
    r = pltpu.roll(t, HEAD_DIM, 1)
    lo = lane < HEAD_DIM
    return jnp.where(lo, t, r) if which == 0 else jnp.where(lo, r, t)


GROUP_HEADS = N_Q_HEADS // N_KV_HEADS
GROUP_ROWS = GROUP_HEADS * ATTN_BLOCK


def _attn_stack(tiles, lo):
    return jnp.concatenate([jnp.where(lo == (hs == 0), t, 0.0) for t in tiles for hs in range(2)], axis=0)


def _attn_unstack(x, j, lo):
    r = 2 * j * ATTN_BLOCK
    return jnp.where(lo, x[r:r + ATTN_BLOCK], x[r + ATTN_BLOCK:r + 2 * ATTN_BLOCK])


def _attn_group_consts(n, sink_ref, head0):
    shape = (GROUP_ROWS, 2 * ATTN_BLOCK)
    qi = lax.broadcasted_iota(jnp.int32, shape, 0) % ATTN_BLOCK
    col = lax.broadcasted_iota(jnp.int32, shape, 1)
    valid = ((col < ATTN_BLOCK) & (col <= qi)) | ((col >= ATTN_BLOCK) & (col - ATTN_BLOCK > qi) & (n > 0))
    head = lax.broadcasted_iota(jnp.int32, (GROUP_ROWS, 1), 0) // ATTN_BLOCK
    sinks = jnp.zeros((GROUP_ROWS, 1), F32)
    for i in range(GROUP_HEADS):
        sinks = jnp.where(head == i, sink_ref[head0 + i], sinks)
    return valid, sinks, head


def _attn_probs(qs, k2, valid, sinks):
    s = lax.dot_general(qs, k2, NT, preferred_element_type=F32) * HEAD_DIM ** -0.5
    s = jnp.where(valid, s, NEG)
    m = jnp.maximum(s.max(1, keepdims=True), sinks)
    p = jnp.exp(s - m)
    esink = jnp.exp(sinks - m)
    inv = 1.0 / (p.sum(1, keepdims=True) + esink)
    return p * inv, esink * inv


def _attn_kv(cur, prev, kvl):
    return jnp.concatenate([_dup_half(cur, kvl), _dup_half(prev, kvl)], axis=0).astype(MM_DTYPE)


def _attn_specs():
    blk = (ATTN_BLOCK, LANES)
    wide = (ATTN_BLOCK, Q_PER_KV_TILE * LANES)
    prev = lambda n: jnp.maximum(n - 1, 0)
    q_spec = pl.BlockSpec(wide, lambda t, n: (n, t))
    kc = pl.BlockSpec(blk, lambda t, n: (n, K_TILE0 + t))
    kp = pl.BlockSpec(blk, lambda t, n: (prev(n), K_TILE0 + t))
    vc = pl.BlockSpec(blk, lambda t, n: (n, V_TILE0 + t))
    vp = pl.BlockSpec(blk, lambda t, n: (prev(n), V_TILE0 + t))
    return q_spec, kc, kp, vc, vp, pl.BlockSpec(memory_space=pltpu.SMEM)


def _attn_fwd(qk, h, sinks):
    q_spec, kc_s, kp_s, vc_s, vp_s, smem = _attn_specs()

    def body(sink_ref, q_ref, kc_ref, kp_ref, vc_ref, vp_ref, o_ref, ob_ref):
        t, n = pl.program_id(0), pl.program_id(1)
        lo = lax.broadcasted_iota(jnp.int32, (ATTN_BLOCK, LANES), 1) < HEAD_DIM
        kc, kp = kc_ref[...].astype(F32), kp_ref[...].astype(F32)
        vc, vp = vc_ref[...], vp_ref[...]
        for kvl in range(2):
            valid, sink_rows, _ = _attn_group_consts(n, sink_ref, t * HEADS_PER_KV_TILE + kvl * GROUP_HEADS)
            tiles = [q_ref[:, a * LANES:(a + 1) * LANES].astype(F32) for a in (2 * kvl, 2 * kvl + 1)]
            qs = _attn_stack(tiles, lo).astype(MM_DTYPE)
            pn, _ = _attn_probs(qs, _attn_kv(kc, kp, kvl), valid, sink_rows)
            os_ = lax.dot_general(pn.astype(MM_DTYPE), _attn_kv(vc, vp, kvl), NN, preferred_element_type=F32)
            for j in range(2):
                a = 2 * kvl + j
                o = _attn_unstack(os_, j, lo)
                o_ref[:, a * LANES:(a + 1) * LANES] = o
                ob_ref[:, a * LANES:(a + 1) * LANES] = o.astype(ob_ref.dtype)

    return pl.pallas_call(
        body, name="attn_fwd", grid=(KV_TILES, N_QBLK),
        in_specs=[smem, q_spec, kc_s, kp_s, vc_s, vp_s], out_specs=[q_spec, q_spec],
        out_shape=[_sds((SEQ, ATTN_WIDTH)), _sds((SEQ, ATTN_WIDTH), MM_DTYPE)],
        compiler_params=_cparams(("parallel", "parallel")))(sinks, qk, qk, qk, h, h)


def _attn_bwd(qk, h, sinks, y, dy, dy_tile0, dep=None):
    deps = _as_list(dep)
    q_spec, kc_s, kp_s, vc_s, vp_s, smem = _attn_specs()
    blk = (ATTN_BLOCK, LANES)
    wide = (ATTN_BLOCK, Q_PER_KV_TILE * LANES)
    kv_out = pl.BlockSpec(blk, lambda t, n: (n, t))
    dy_spec = pl.BlockSpec(wide, lambda t, n: (n, t + dy_tile0))

    def body(sink_ref, q_ref, kc_ref, kp_ref, vc_ref, vp_ref, y_ref, dy_ref, *rest):
        dq_ref, dkc_ref, dkp_ref, dvc_ref, dvp_ref, dsk_ref = rest[-6:]
        t, n = pl.program_id(0), pl.program_id(1)
        lo = lax.broadcasted_iota(jnp.int32, blk, 1) < HEAD_DIM
        kc, kp = kc_ref[...].astype(F32), kp_ref[...].astype(F32)
        vc, vp = vc_ref[...], vp_ref[...]
        hrow = lax.broadcasted_iota(jnp.int32, (HEADS_PER_KV_TILE, LANES), 0)
        dsk = jnp.zeros((HEADS_PER_KV_TILE, LANES), F32)
        dk2, dv2 = [], []
        for kvl in range(2):
            valid, sink_rows, head = _attn_group_consts(n, sink_ref, t * HEADS_PER_KV_TILE + kvl * GROUP_HEADS)
            sls = [slice(a * LANES, (a + 1) * LANES) for a in (2 * kvl, 2 * kvl + 1)]
            qs = _attn_stack([q_ref[:, sl].astype(F32) for sl in sls], lo).astype(MM_DTYPE)
            dos = _attn_stack([dy_ref[:, sl] for sl in sls], lo)
            delta = _attn_stack([dy_ref[:, sl] * y_ref[:, sl] for sl in sls], lo).sum(1, keepdims=True)
            dos = dos.astype(MM_DTYPE)
            k2, v2 = _attn_kv(kc, kp, kvl), _attn_kv(vc, vp, kvl)
            pn, psink = _attn_probs(qs, k2, valid, sink_rows)
            dp = lax.dot_general(dos, v2, NT, preferred_element_type=F32)
            ds = (pn * (dp - delta) * HEAD_DIM ** -0.5).astype(MM_DTYPE)
            dqs = lax.dot_general(ds, k2, NN, preferred_element_type=F32)
            for j, sl in enumerate(sls):
                dq_ref[:, sl] = _attn_unstack(dqs, j, lo)
            for acc, x in ((dk2, lax.dot_general(ds, qs, TN, preferred_element_type=F32)),
                           (dv2, lax.dot_general(pn.astype(MM_DTYPE), dos, TN, preferred_element_type=F32))):
                acc.append(x + pltpu.roll(x, HEAD_DIM, 1))
            dsink = psink * delta
            for i in range(GROUP_HEADS):
                dsk = dsk + jnp.where(hrow == kvl * GROUP_HEADS + i, -jnp.sum(jnp.where(head == i, dsink, 0.0)), 0.0)
        for o_ref, src, r in ((dkc_ref, dk2, 0), (dkp_ref, dk2, ATTN_BLOCK), (dvc_ref, dv2, 0), (dvp_ref, dv2, ATTN_BLOCK)):
            o_ref[...] = jnp.where(lo, src[0][r:r + ATTN_BLOCK], src[1][r:r + ATTN_BLOCK])

        @pl.when(n == 0)
        def _():
            dsk_ref[...] = jnp.zeros_like(dsk_ref)

        dsk_ref[...] += dsk

    kv_shape = _sds((SEQ, KV_WIDTH))
    return pl.pallas_call(
        body, name="attn_bwd", grid=(KV_TILES, N_QBLK),
        in_specs=[smem, q_spec, kc_s, kp_s, vc_s, vp_s, q_spec, dy_spec] + [ANY] * len(deps),
        out_specs=[q_spec, kv_out, kv_out, kv_out, kv_out,
                   pl.BlockSpec((None, HEADS_PER_KV_TILE, LANES), lambda t, n: (t, 0, 0))],
        out_shape=[_sds((SEQ, ATTN_WIDTH)), kv_shape, kv_shape, kv_shape, kv_shape,
                   _sds((KV_TILES, HEADS_PER_KV_TILE, LANES))],
        compiler_params=_cparams(("parallel", "arbitrary")))(sinks, qk, qk, qk, h, h, y, dy, *deps)


def _attn_dh(dq, dkc, dkp, dvc, dvp, cos_t, nsin_t):
    n_tiles = Q_TILES + 2 * KV_TILES
    nxt = lambda n: jnp.minimum(n + 1, N_QBLK - 1)

    def body(dq_ref, kc_ref, kp_ref, vc_ref, vp_ref, c_ref, s_ref, o_ref):
        has_next = pl.program_id(0) < N_QBLK - 1
        cos, sin = c_ref[...], s_ref[...]

        def unrope(t):
            return t * cos + _rotate_half(t) * sin

        for j in range(Q_TILES):
            sl = slice(j * LANES, (j + 1) * LANES)
            o_ref[:, sl] = unrope(dq_ref[:, sl]).astype(o_ref.dtype)
        for j in range(KV_TILES):
            sl = slice(j * LANES, (j + 1) * LANES)
            t = kc_ref[:, sl] + jnp.where(has_next, kp_ref[:, sl], 0.0)
            o_ref[:, ATTN_WIDTH + j * LANES:ATTN_WIDTH + (j + 1) * LANES] = unrope(t).astype(o_ref.dtype)
        o_ref[:, ATTN_WIDTH + KV_WIDTH:] = (vc_ref[...] + jnp.where(has_next, vp_ref[...], 0.0)).astype(o_ref.dtype)

    qb, kb, tb = (ATTN_BLOCK, ATTN_WIDTH), (ATTN_BLOCK, KV_WIDTH), (ATTN_BLOCK, LANES)
    return pl.pallas_call(
        body, name="attn_dh", grid=(N_QBLK,),
        in_specs=[pl.BlockSpec(qb, lambda n: (n, 0)),
                  pl.BlockSpec(kb, lambda n: (n, 0)), pl.BlockSpec(kb, lambda n: (nxt(n), 0)),
                  pl.BlockSpec(kb, lambda n: (n, 0)), pl.BlockSpec(kb, lambda n: (nxt(n), 0)),
                  pl.BlockSpec(tb, lambda n: (n, 0)), pl.BlockSpec(tb, lambda n: (n, 0))],
        out_specs=pl.BlockSpec((ATTN_BLOCK, n_tiles * LANES), lambda n: (n, 0)),
        out_shape=_sds((SEQ, n_tiles * LANES), MM_DTYPE),
        compiler_params=_cparams(("parallel",)))(dq, dkc, dkp, dvc, dvp, cos_t, nsin_t)


POOL_TILE0 = (ATTN_WIDTH + 2 * KV_WIDTH) // POOL_WIDTH


def _shift_rows(x, d, down):
    n = x.shape[0]
    row = lax.broadcasted_iota(jnp.int32, x.shape, 0)
    if down:
        return jnp.where(row >= d, pltpu.roll(x, d, 0), 0.0)
    return jnp.where(row < n - d, pltpu.roll(x, n - d, 0), 0.0)


def _window_sum(x, w, down):
    d = 1
    while d < w:
        x = x + _shift_rows(x, d, down)
        d *= 2
    return x


def _pool_z(u, w):
    t = lax.broadcasted_iota(jnp.int32, u.shape, 0).astype(F32)
    cnt = jnp.minimum(t + 1.0, float(w))
    return _window_sum(u, w, True) / cnt - u, cnt


def _pool_fwd(h, pool_w, pool_scale):
    def body(u_ref, w_ref, s_ref, o_ref):
        for gi, w in enumerate(POOL_WINDOWS):
            sl = slice(gi * POOL_GROUP, (gi + 1) * POOL_GROUP)
            z, _ = _pool_z(u_ref[:, sl], w)
            o_ref[:, sl] = (lax.dot_general(z.astype(MM_DTYPE), w_ref[gi].astype(MM_DTYPE), NN,
                                            preferred_element_type=F32) * s_ref[:, sl]).astype(o_ref.dtype)

    return pl.pallas_call(
        body, name="pool_fwd", grid=(1,),
        in_specs=[pl.BlockSpec((SEQ, POOL_WIDTH), lambda i: (0, POOL_TILE0)),
                  pl.BlockSpec(pool_w.shape, lambda i: (0, 0, 0)), pl.BlockSpec((1, POOL_WIDTH), lambda i: (0, 0))],
        out_specs=pl.BlockSpec((SEQ, POOL_WIDTH), lambda i: (0, 0)),
        out_shape=_sds((SEQ, POOL_WIDTH), MM_DTYPE), compiler_params=_cparams(("arbitrary",)))(h, pool_w, pool_scale)


def _pool_bwd(h, pool_w, pool_scale, dmix, dy_tile0):
    def body(u_ref, w_ref, s_ref, dy_ref, du_ref, dw_ref, ds_ref):
        for gi, w in enumerate(POOL_WINDOWS):
            sl = slice(gi * POOL_GROUP, (gi + 1) * POOL_GROUP)
            z, cnt = _pool_z(u_ref[:, sl], w)
            zb, wb = z.astype(MM_DTYPE), w_ref[gi].astype(MM_DTYPE)
            dy = dy_ref[:, sl]
            zp = lax.dot_general(zb, wb, NN, preferred_element_type=F32)
            ds_ref[:, sl] = jnp.sum(dy * zp, axis=0, keepdims=True)
            dyo = (dy * s_ref[:, sl]).astype(MM_DTYPE)
            dw_ref[gi] = lax.dot_general(zb, dyo, TN, preferred_element_type=F32)
            dz = lax.dot_general(dyo, wb, NT, preferred_element_type=F32)
            du_ref[:, sl] = (_window_sum(dz / cnt, w, False) - dz).astype(du_ref.dtype)

    return pl.pallas_call(
        body, name="pool_bwd", grid=(1,),
        in_specs=[pl.BlockSpec((SEQ, POOL_WIDTH), lambda i: (0, POOL_TILE0)),
                  pl.BlockSpec(pool_w.shape, lambda i: (0, 0, 0)), pl.BlockSpec((1, POOL_WIDTH), lambda i: (0, 0)),
                  pl.BlockSpec((SEQ, POOL_WIDTH), lambda i: (0, dy_tile0))],
        out_specs=[pl.BlockSpec((SEQ, POOL_WIDTH), lambda i: (0, 0)), pl.BlockSpec(pool_w.shape, lambda i: (0, 0, 0)),
                   pl.BlockSpec((1, POOL_WIDTH), lambda i: (0, 0))],
        out_shape=[_sds((SEQ, POOL_WIDTH), MM_DTYPE), _sds(pool_w.shape), _sds((1, POOL_WIDTH))],
        compiler_params=_cparams(("arbitrary",)))(h, pool_w, pool_scale, dmix)


def _ssm_discretize(lr, li, ldt, br, bi):
    dt = jnp.exp(ldt)
    mag = jnp.exp(lr * dt)
    ar, ai = mag * jnp.cos(li * dt), mag * jnp.sin(li * dt)
    nr, ni = ar - 1.0, ai
    den = lr * lr + li * li
    zr = (nr * lr + ni * li) / den
    zi = (ni * lr - nr * li) / den
    return ar, ai, zr * br - zi * bi, zr * bi + zi * br


def _ssm_prep(lr, li, ldt, br, bi):
    def body(lr_ref, li_ref, ldt_ref, br_ref, bi_ref, ar_ref, ai_ref, bbr_ref, bbi_ref):
        outs = _ssm_discretize(lr_ref[...], li_ref[...], ldt_ref[...], br_ref[...], bi_ref[...])
        for o, v in zip((ar_ref, ai_ref, bbr_ref, bbi_ref), outs):
            o[...] = v

    row, mat = _sds((1, SSM_CH)), _sds((SSM_GROUP, SSM_CH))
    return pl.pallas_call(body, name="ssm_prep", out_shape=[row, row, mat, mat])(lr, li, ldt, br, bi)


def _ssm_prep_bwd(lr, li, ldt, br, bi, dar8, dai8, dbbr, dbbi):
    def body(lr_ref, li_ref, ldt_ref, br_ref, bi_ref, dar_ref, dai_ref, dbbr_ref, dbbi_ref, *outs):
        args = (lr_ref[...], li_ref[...], ldt_ref[...], br_ref[...], bi_ref[...])
        _, vjp = jax.vjp(_ssm_discretize, *args)
        cot = (jnp.sum(dar_ref[...], axis=0, keepdims=True), jnp.sum(dai_ref[...], axis=0, keepdims=True),
               dbbr_ref[...], dbbi_ref[...])
        for o, v in zip(outs, vjp(cot)):
            o[...] = v

    row, mat = _sds((1, SSM_CH)), _sds((SSM_GROUP, SSM_CH))
    return pl.pallas_call(body, name="ssm_prep_bwd", out_shape=[row, row, row, mat, mat])(
        lr, li, ldt, br, bi, dar8, dai8, dbbr, dbbi)


def _ssm_diag(name, full):
    tiles = SCAN_CW // LANES
    groups = LANES // SSM_STATE
    rows = tiles * groups * SSM_GROUP

    def body(x_ref, o_ref):
        lane = lax.broadcasted_iota(jnp.int32, (SSM_GROUP, LANES), 1)
        for q in range(2 * tiles):
            sl = slice(q * LANES, (q + 1) * LANES)
            r0 = (q % tiles) * groups * SSM_GROUP
            out = x_ref[r0:r0 + SSM_GROUP, sl]
            for k in range(1, groups):
                out = jnp.where(lane >= k * SSM_STATE, x_ref[r0 + k * SSM_GROUP:r0 + (k + 1) * SSM_GROUP, sl], out)
            o_ref[:, sl] = out

    return pl.pallas_call(
        body, name=name, grid=(SCAN_NB,),
        in_specs=[pl.BlockSpec((rows, 2 * SCAN_CW), lambda b: (b, b))],
        out_specs=pl.BlockSpec((SSM_GROUP, 2 * SCAN_CW), lambda b: (0, b)),
        out_shape=_sds((SSM_GROUP, 2 * SSM_CH)), compiler_params=_cparams(("parallel",)))(full)


def _scan_layout(re, im):
    r = re.shape[0]
    return jnp.stack([re.reshape(r, SCAN_NB, SCAN_CW), im.reshape(r, SCAN_NB, SCAN_CW)], axis=2).reshape(r, 2 * SSM_CH)


def _scan_unlayout(x):
    r = x.shape[0]
    x = x.reshape(r, SCAN_NB, 2, SCAN_CW)
    return x[:, :, 0].reshape(r, SSM_CH), x[:, :, 1].reshape(r, SSM_CH)


def _time_permute(u):
    s, c = u.shape
    return u.reshape(SUBLANES, s // SUBLANES, c).transpose(1, 0, 2).reshape(s, c)


def _time_unpermute(u):
    s, c = u.shape
    return u.reshape(s // SUBLANES, SUBLANES, c).transpose(1, 0, 2).reshape(s, c)


def _ssm_scan(name, a_vec, x, reverse, s_prev=None):
    nsteps = SEQ // SUBLANES
    cw = SCAN_CW
    with_da = s_prev is not None

    def body(a_ref, x_ref, *rest):
        if with_da:
            s_ref, o_ref, da_ref = rest
        else:
            o_ref, = rest
        ar = jnp.broadcast_to(a_ref[:, :cw], (SUBLANES, cw))
        ai = jnp.broadcast_to(a_ref[:, cw:], (SUBLANES, cw))
        seg = lax.broadcasted_iota(jnp.int32, (SUBLANES, cw), 0)

        def toward(v):
            if reverse:
                return jnp.where(seg < SUBLANES - 1, pltpu.roll(v, SUBLANES - 1, 0), 0.0)
            return jnp.where(seg >= 1, pltpu.roll(v, 1, 0), 0.0)

        def rows(j):
            jj = nsteps - 1 - j if reverse else j
            return pl.ds(pl.multiple_of(jj * SUBLANES, SUBLANES), SUBLANES)

        def cmul(pr, pi, qr, qi):
            return pr * qr - pi * qi, pr * qi + pi * qr

        def local(j, c):
            sr, si = c
            r = rows(j)
            mr, mi = cmul(ar, ai, sr, si)
            return mr + x_ref[r, :cw], mi + x_ref[r, cw:]

        zero = jnp.zeros((SUBLANES, cw), F32)
        fr, fi = lax.fori_loop(0, nsteps, local, (zero, zero), unroll=SCAN_UNROLL)

        def power(_, c):
            return cmul(ar, ai, *c)

        pr, pi = lax.fori_loop(0, nsteps - 1, power, (ar, ai))
        tr, ti = fr, fi
        for _ in range(SUBLANES - 1):
            mr, mi = cmul(pr, pi, toward(tr), toward(ti))
            tr, ti = fr + mr, fi + mi
        init = (toward(tr), toward(ti))

        def advance(j, sr, si):
            r = rows(j)
            mr, mi = cmul(ar, ai, sr, si)
            sr, si = mr + x_ref[r, :cw], mi + x_ref[r, cw:]
            o_ref[r, :cw] = sr
            o_ref[r, cw:] = si
            return sr, si

        def full(j, c):
            return advance(j, *c)

        def full_da(j, c):
            sr, si = advance(j, c[0], c[1])
            rp = pl.ds(pl.multiple_of((nsteps - 2 - j) * SUBLANES, SUBLANES), SUBLANES)
            spr, spi = s_ref[rp, :cw], s_ref[rp, cw:]
            return sr, si, c[2] + sr * spr + si * spi, c[3] + si * spr - sr * spi

        if with_da:
            sr, si, dar, dai = lax.fori_loop(0, nsteps - 1, full_da, init + (zero, zero), unroll=SCAN_UNROLL)
            sr, si = advance(nsteps - 1, sr, si)
            last = pl.ds((nsteps - 1) * SUBLANES, SUBLANES)
            spr = jnp.where(seg >= 1, pltpu.roll(s_ref[last, :cw], 1, 0), 0.0)
            spi = jnp.where(seg >= 1, pltpu.roll(s_ref[last, cw:], 1, 0), 0.0)
            da_ref[:, :cw] = dar + sr * spr + si * spi
            da_ref[:, cw:] = dai + si * spr - sr * spi
        else:
            lax.fori_loop(0, nsteps, full, init, unroll=SCAN_UNROLL)

    blk = pl.BlockSpec((SEQ, 2 * cw), lambda b: (0, b))
    a_spec = pl.BlockSpec((1, 2 * cw), lambda b: (0, b))
    in_specs, args = [a_spec, blk], [a_vec, x]
    out_specs, out_shape = blk, _sds((SEQ, 2 * SSM_CH))
    if with_da:
        in_specs, args = in_specs + [blk], args + [s_prev]
        out_specs = [blk, pl.BlockSpec((SUBLANES, 2 * cw), lambda b: (0, b))]
        out_shape = [out_shape, _sds((SUBLANES, 2 * SSM_CH))]
    return pl.pallas_call(body, name=name, grid=(SCAN_NB,), in_specs=in_specs, out_specs=out_specs,
                          out_shape=out_shape, compiler_params=_cparams(("parallel",)))(*args)


def _ssm_gelu(yp, up, dvec):
    tm = _tile(SEQ, 512)

    def body(y_ref, u_ref, d_ref, yf_ref, g_ref):
        yf = y_ref[...] + d_ref[...] * u_ref[...]
        yf_ref[...] = yf
        g_ref[...] = jax.nn.gelu(yf).astype(g_ref.dtype)

    blk = pl.BlockSpec((tm, SSM_WIDTH), lambda i: (i, 0))
    row = pl.BlockSpec((1, SSM_WIDTH), lambda i: (0, 0))
    return pl.pallas_call(body, name="ssm_gelu", grid=(SEQ // tm,), in_specs=[blk, blk, row], out_specs=[blk, blk],
                          out_shape=[_sds((SEQ, SSM_WIDTH)), _sds((SEQ, SSM_WIDTH), MM_DTYPE)],
                          compiler_params=_cparams(("parallel",)))(yp, up, dvec)


def _ssm_gelu_bwd(yf, dgy, up, dvec):
    tm = _tile(SEQ, 512)

    def body(yf_ref, dg_ref, u_ref, d_ref, dyf_ref, du_ref, dd_ref):
        _, vjp = jax.vjp(jax.nn.gelu, yf_ref[...])
        dyf, = vjp(dg_ref[...])
        dyf_ref[...] = dyf.astype(dyf_ref.dtype)
        du_ref[...] = d_ref[...] * dyf

        @pl.when(pl.program_id(0) == 0)
        def _():
            dd_ref[...] = jnp.zeros_like(dd_ref)

        dd_ref[...] += jnp.sum(dyf * u_ref[...], axis=0, keepdims=True)

    blk = pl.BlockSpec((tm, SSM_WIDTH), lambda i: (i, 0))
    row = pl.BlockSpec((1, SSM_WIDTH), lambda i: (0, 0))
    return pl.pallas_call(body, name="ssm_gelu_bwd", grid=(SEQ // tm,), in_specs=[blk, blk, blk, row],
                          out_specs=[blk, blk, row],
                          out_shape=[_sds((SEQ, SSM_WIDTH), MM_DTYPE), _sds((SEQ, SSM_WIDTH)), _sds((1, SSM_WIDTH))],
                          compiler_params=_cparams(("arbitrary",)))(yf, dgy, up, dvec)


def _glu(ab):
    return ab[:, :SSM_WIDTH] * jax.nn.sigmoid(ab[:, SSM_WIDTH:])


def _ssm_glu(ab):
    tm = _tile(SEQ, 512)

    def body(ab_ref, o_ref):
        o_ref[...] = _glu(ab_ref[...]).astype(o_ref.dtype)

    return pl.pallas_call(body, name="ssm_glu", grid=(SEQ // tm,),
                          in_specs=[pl.BlockSpec((tm, 2 * SSM_WIDTH), lambda i: (i, 0))],
                          out_specs=pl.BlockSpec((tm, SSM_WIDTH), lambda i: (i, 0)),
                          out_shape=_sds((SEQ, SSM_WIDTH), MM_DTYPE), compiler_params=_cparams(("parallel",)))(ab)


def _ssm_glu_bwd(ab, dout):
    tm = _tile(SEQ, 512)

    def body(ab_ref, do_ref, dab_ref):
        _, vjp = jax.vjp(_glu, ab_ref[...])
        dab, = vjp(do_ref[...])
        dab_ref[...] = dab.astype(dab_ref.dtype)

    return pl.pallas_call(body, name="ssm_glu_bwd", grid=(SEQ // tm,),
                          in_specs=[pl.BlockSpec((tm, 2 * SSM_WIDTH), lambda i: (i, 0)),
                                    pl.BlockSpec((tm, SSM_WIDTH), lambda i: (i, 0))],
                          out_specs=pl.BlockSpec((tm, 2 * SSM_WIDTH), lambda i: (i, 0)),
                          out_shape=_sds((SEQ, 2 * SSM_WIDTH), MM_DTYPE), compiler_params=_cparams(("parallel",)))(ab, dout)


def _add2(name, a, b, out_dtype):
    tm = _tile(a.shape[0], 512)

    def body(a_ref, b_ref, o_ref):
        o_ref[...] = (a_ref[...] + b_ref[...]).astype(o_ref.dtype)

    blk = pl.BlockSpec((tm, a.shape[1]), lambda i: (i, 0))
    return pl.pallas_call(body, name=name, grid=(a.shape[0] // tm,), in_specs=[blk, blk], out_specs=blk,
                          out_shape=_sds(a.shape, out_dtype), compiler_params=_cparams(("parallel",)))(a, b)


def _layer_norm(r, g, b):
    mu = r.mean(-1, keepdims=True)
    var = jnp.square(r - mu).mean(-1, keepdims=True)
    return (r - mu) * lax.rsqrt(var + LN_EPS) * g + b


def _ln_fwd(name, x, y, g, b):
    tm = _tile(SEQ, 256)

    def body(x_ref, y_ref, g_ref, b_ref, r_ref, o_ref, ob_ref):
        r = DEEPNORM_ALPHA * x_ref[...] + y_ref[...]
        r_ref[...] = r
        o = _layer_norm(r, g_ref[...], b_ref[...])
        o_ref[...] = o
        ob_ref[...] = o.astype(ob_ref.dtype)

    blk = pl.BlockSpec((tm, D_MODEL), lambda i: (i, 0))
    row = pl.BlockSpec((1, D_MODEL), lambda i: (0, 0))
    return pl.pallas_call(body, name=name, grid=(SEQ // tm,), in_specs=[blk, blk, row, row], out_specs=[blk, blk, blk],
                          out_shape=[_sds((SEQ, D_MODEL))] * 2 + [_sds((SEQ, D_MODEL), MM_DTYPE)],
                          compiler_params=_cparams(("parallel",)))(x, y, g, b)


def _ln_bwd(name, r, g, b, da, db=None, dep=None):
    tm = _tile(SEQ, 256)
    two = db is not None
    deps = _as_list(dep)

    def body(r_ref, g_ref, b_ref, da_ref, *rest):
        dr_ref, drb_ref, dg_ref, dbeta_ref = rest[-4:]
        dout = DEEPNORM_ALPHA * da_ref[...] + rest[0][...] if two else da_ref[...]
        _, vjp = jax.vjp(_layer_norm, r_ref[...], g_ref[...], b_ref[...])
        dr, dg, dbeta = vjp(dout)
        dr_ref[...] = dr
        drb_ref[...] = dr.astype(drb_ref.dtype)

        @pl.when(pl.program_id(0) == 0)
        def _():
            dg_ref[...] = jnp.zeros_like(dg_ref)
            dbeta_ref[...] = jnp.zeros_like(dbeta_ref)

        dg_ref[...] += dg
        dbeta_ref[...] += dbeta

    blk = pl.BlockSpec((tm, D_MODEL), lambda i: (i, 0))
    row = pl.BlockSpec((1, D_MODEL), lambda i: (0, 0))
    args = [r, g, b, da] + ([db] if two else []) + deps
    return pl.pallas_call(body, name=name, grid=(SEQ // tm,),
                          in_specs=[blk, row, row, blk] + ([blk] if two else []) + [ANY] * len(deps),
                          out_specs=[blk, blk, row, row],
                          out_shape=[_sds((SEQ, D_MODEL)), _sds((SEQ, D_MODEL), MM_DTYPE), _sds((1, D_MODEL)), _sds((1, D_MODEL))],
                          compiler_params=_cparams(("arbitrary",)))(*args)


FFN_TM = 128
HALO = SUBLANES


def _conv_taps(cur, halo):
    row = lax.broadcasted_iota(jnp.int32, cur.shape, 0)
    h1 = jnp.where(row == 0, halo[HALO - 1:HALO, :], pltpu.roll(cur, 1, 0))
    h2 = jnp.where(row == 0, halo[HALO - 2:HALO - 1, :], jnp.where(row == 1, halo[HALO - 1:HALO, :], pltpu.roll(cur, 2, 0)))
    return h1, h2


def _conv_fwd(cur, halo, w_ref, b_ref):
    h1, h2 = _conv_taps(cur, halo)
    return b_ref[...] + h2 * w_ref[0:1, :] + h1 * w_ref[1:2, :] + cur * w_ref[2:3, :], h1, h2


def _gate(val, gate):
    return jax.nn.silu(gate) * val


def _ffn_specs(tm):
    nb = tm // HALO
    cur = lambda off: pl.BlockSpec((None, tm, FS), lambda p, i: (p + off, i, 0))
    halo = lambda off: pl.BlockSpec((None, HALO, FS), lambda p, i: (p + off, jnp.maximum(i * nb - 1, 0), 0))
    cw = lambda off: pl.BlockSpec((None, CONV_WIDTH, FS), lambda p, i: (p + off, 0, 0))
    cb = lambda off: pl.BlockSpec((None, 1, FS), lambda p, i: (p + off, 0, 0))
    return cur, halo, cw, cb


def _ffn_act(hf, conv_w, conv_b):
    tm = _tile(SEQ, FFN_TM, SUBLANES)
    cur, halo, cw, cb = _ffn_specs(tm)

    def body(v_ref, vh_ref, g_ref, gh_ref, wv_ref, wg_ref, bv_ref, bg_ref, o_ref):
        live = pl.program_id(1) > 0
        vh = jnp.where(live, vh_ref[...], 0.0)
        gh = jnp.where(live, gh_ref[...], 0.0)
        val, _, _ = _conv_fwd(v_ref[...], vh, wv_ref, bv_ref)
        gate, _, _ = _conv_fwd(g_ref[...], gh, wg_ref, bg_ref)
        o_ref[...] = _gate(val, gate).astype(o_ref.dtype)

    return pl.pallas_call(
        body, name="ffn_act", grid=(2, SEQ // tm),
        in_specs=[cur(0), halo(0), cur(2), halo(2), cw(0), cw(2), cb(0), cb(2)],
        out_specs=pl.BlockSpec((None, tm, FS), lambda p, i: (p, i, 0)),
        out_shape=_sds((2, SEQ, FS), MM_DTYPE), compiler_params=_cparams(("parallel", "parallel")))(
            hf, hf, hf, hf, conv_w, conv_w, conv_b, conv_b)


def _ffn_act_bwd(hf, conv_w, conv_b, dact, dep=None):
    tm = _tile(SEQ, FFN_TM, SUBLANES)
    nb, nblk = tm // HALO, SEQ // tm
    cur, halo, cw, cb = _ffn_specs(tm)
    nxt = lambda off: pl.BlockSpec((None, HALO, FS), lambda p, i: (p + off, jnp.minimum((i + 1) * nb, SEQ // HALO - 1), 0))
    deps = _as_list(dep)

    def body(v_ref, vh_ref, vn_ref, g_ref, gh_ref, gn_ref, wv_ref, wg_ref, bv_ref, bg_ref, da_ref, dan_ref, *rest):
        dh_ref, dw_ref, dbias_ref = rest[-3:]
        dwv_ref, dwg_ref = dw_ref.at[0], dw_ref.at[1]
        dbv_ref, dbg_ref = dbias_ref.at[0], dbias_ref.at[1]
        i = pl.program_id(1)
        live, more = i > 0, i < nblk - 1
        vh = jnp.where(live, vh_ref[...], 0.0)
        gh = jnp.where(live, gh_ref[...], 0.0)
        vcur, gcur = v_ref[...], g_ref[...]
        val, v1, v2 = _conv_fwd(vcur, vh, wv_ref, bv_ref)
        gate, g1, g2 = _conv_fwd(gcur, gh, wg_ref, bg_ref)
        _, vjp = jax.vjp(_gate, val, gate)
        dval, dgate = vjp(da_ref[...])
        val_n, _, _ = _conv_fwd(vn_ref[...], vcur[tm - HALO:, :], wv_ref, bv_ref)
        gate_n, _, _ = _conv_fwd(gn_ref[...], gcur[tm - HALO:, :], wg_ref, bg_ref)
        _, vjp_n = jax.vjp(_gate, val_n, gate_n)
        dval_n, dgate_n = vjp_n(dan_ref[...])
        row = lax.broadcasted_iota(jnp.int32, dval.shape, 0)
        for k, (d, dn, w_ref) in enumerate(((dval, dval_n, wv_ref), (dgate, dgate_n, wg_ref))):
            dn = jnp.where(more, dn, 0.0)
            d1 = jnp.where(row == tm - 1, dn[0:1, :], pltpu.roll(d, tm - 1, 0))
            d2 = jnp.where(row == tm - 1, dn[1:2, :], jnp.where(row == tm - 2, dn[0:1, :], pltpu.roll(d, tm - 2, 0)))
            dh_ref[k] = (d * w_ref[2:3, :] + d1 * w_ref[1:2, :] + d2 * w_ref[0:1, :]).astype(dh_ref.dtype)

        @pl.when(i == 0)
        def _():
            dw_ref[...] = jnp.zeros_like(dw_ref)
            dbias_ref[...] = jnp.zeros_like(dbias_ref)

        for d, taps, dwk_ref, dbk_ref in ((dval, (v2, v1, vcur), dwv_ref, dbv_ref), (dgate, (g2, g1, gcur), dwg_ref, dbg_ref)):
            for k in range(CONV_WIDTH):
                dwk_ref[k:k + 1, :] += jnp.sum(d * taps[k], axis=0, keepdims=True)
            dbk_ref[...] += jnp.sum(d, axis=0, keepdims=True)

    return pl.pallas_call(
        body, name="ffn_act_bwd", grid=(2, SEQ // tm),
        in_specs=[cur(0), halo(0), nxt(0), cur(2), halo(2), nxt(2), cw(0), cw(2), cb(0), cb(2),
                  pl.BlockSpec((None, tm, FS), lambda p, i: (p, i, 0)), nxt(0)] + [ANY] * len(deps),
        out_specs=[pl.BlockSpec((None, 2, tm, FS), lambda p, i: (p, 0, i, 0)),
                   pl.BlockSpec((None, 2, CONV_WIDTH, FS), lambda p, i: (p, 0, 0, 0)),
                   pl.BlockSpec((None, 2, 1, FS), lambda p, i: (p, 0, 0, 0))],
        out_shape=[_sds((2, 2, SEQ, FS), MM_DTYPE), _sds((2, 2, CONV_WIDTH, FS)), _sds((2, 2, 1, FS))],
        compiler_params=_cparams(("parallel", "arbitrary")))(hf, hf, hf, hf, hf, hf, conv_w, conv_w, conv_b, conv_b, dact,
                                                              dact, *deps)


def _loss(y, target):
    tm = _tile(SEQ, 256)

    def body(y_ref, t_ref, dy_ref, l_ref):
        err = y_ref[...] - t_ref[...]
        dy_ref[...] = err * (1.0 / D_MODEL)

        @pl.when(pl.program_id(0) == 0)
        def _():
            l_ref[...] = jnp.zeros_like(l_ref)

        l_ref[...] += 0.5 * jnp.sum(jnp.mean(jnp.square(err), axis=-1))

    blk = pl.BlockSpec((tm, D_MODEL), lambda i: (i, 0))
    return pl.pallas_call(body, name="loss", grid=(SEQ // tm,), in_specs=[blk, blk],
                          out_specs=[blk, pl.BlockSpec((SUBLANES, LANES), lambda i: (0, 0))],
                          out_shape=[_sds((SEQ, D_MODEL)), _sds((SUBLANES, LANES))],
                          compiler_params=_cparams(("arbitrary",)))(y, target)


ADAM_BLOCK_BYTES = 3 << 19
ELEMENTWISE_COLS = 1024


def _adamw_math(w, g, m, v):
    nm = ADAM_B1 * m + (1.0 - ADAM_B1) * g
    nv = ADAM_B2 * v + (1.0 - ADAM_B2) * jnp.square(g)
    m_hat = nm / (1.0 - ADAM_B1 ** ADAM_STEP)
    v_hat = nv / (1.0 - ADAM_B2 ** ADAM_STEP)
    return -ADAM_LR * (m_hat / (jnp.sqrt(v_hat) + ADAM_EPS) + ADAM_WD * w), nm, nv


def _adamw(name, w, g, m, v):
    r, c = w.shape
    tr = _tile(r, max(SUBLANES, ADAM_BLOCK_BYTES // (4 * c)), SUBLANES)

    def body(w_ref, g_ref, m_ref, v_ref, d_ref, nm_ref, nv_ref):
        d_ref[...], nm_ref[...], nv_ref[...] = _adamw_math(w_ref[...], g_ref[...], m_ref[...], v_ref[...])

    blk = pl.BlockSpec((tr, c), lambda i: (i, 0))
    return pl.pallas_call(body, name=name, grid=(r // tr,), in_specs=[blk] * 4, out_specs=[blk] * 3,
                          out_shape=[_sds((r, c))] * 3, compiler_params=_cparams(("parallel",)))(w, g, m, v)


def _adamw_big(name, l, c_idx, w, m, v, g_own, g_got, prev):
    depth, _, r, c = w.shape
    tc = _tile(c, ELEMENTWISE_COLS)
    tr = _tile(r, max(SUBLANES, ADAM_BLOCK_BYTES // (4 * tc)), SUBLANES)

    def body(c_ref, w_ref, m_ref, v_ref, own_ref, got_ref, *rest):
        g_ref, d_ref, nm_ref, nv_ref = rest[-4:]
        g = jnp.where(pl.program_id(0) == c_ref[0], own_ref[...], got_ref[...])
        g_ref[...] = g
        d_ref[...], nm_ref[...], nv_ref[...] = _adamw_math(w_ref[...], g, m_ref[...], v_ref[...])

    stacked = pl.BlockSpec((None, None, tr, tc), lambda h, i, j, cr: (l, h, i, j))
    own = pl.BlockSpec((tr, tc), lambda h, i, j, cr: (jnp.where(h == cr[0], i, 0), jnp.where(h == cr[0], j, 0)))
    got = pl.BlockSpec((tr, tc), lambda h, i, j, cr: (jnp.where(h == cr[0], 0, i), jnp.where(h == cr[0], 0, j)))
    grid_spec = pltpu.PrefetchScalarGridSpec(
        num_scalar_prefetch=1, grid=(2, r // tr, c // tc),
        in_specs=[stacked] * 3 + [own, got] + ([ANY] * 4 if prev else []), out_specs=[stacked] * 4)
    return pl.pallas_call(
        body, name=name, grid_spec=grid_spec, out_shape=[_sds((depth, 2, r, c))] * 4,
        input_output_aliases={6 + k: k for k in range(4)} if prev else {},
        compiler_params=_cparams(("arbitrary", "arbitrary", "arbitrary")))(c_idx, w, m, v, g_own, g_got, *(prev or ()))


ANY = pl.BlockSpec(memory_space=pl.ANY)


def _place():
    x, y, c = lax.axis_index("x"), lax.axis_index("y"), lax.axis_index("c")
    chips = [(1 - x, y), (x, 1 - y), (1 - x, 1 - y)]
    return x, y, c, chips


def _cast_place(name, w, l, me_idx, out_dtype):
    _, _, r, c = w.shape
    tr = _tile(r, max(2 * SUBLANES, COPY_BLOCK_BYTES // (4 * c)), 2 * SUBLANES)

    def body(me_ref, w_ref, o_ref):
        o_ref[...] = w_ref[...].astype(o_ref.dtype)

    grid_spec = pltpu.PrefetchScalarGridSpec(
        num_scalar_prefetch=1, grid=(2, r // tr),
        in_specs=[pl.BlockSpec((None, None, tr, c), lambda h, i, me: (l, h, i, 0))],
        out_specs=pl.BlockSpec((None, None, tr, c), lambda h, i, me: (me[0], h, i, 0)))
    return pl.pallas_call(body, name=name, grid_spec=grid_spec, out_shape=_sds((N_CHIPS, 2, r, c), out_dtype),
                          compiler_params=_cparams(("parallel", "parallel")))(me_idx, w)


HBM = pl.BlockSpec(memory_space=pltpu.HBM)
SEM = pl.BlockSpec(memory_space=pltpu.SEMAPHORE)
TOKEN = (SUBLANES, LANES)


def _comm_call(name, body, hbm, sems_in=(), after=None, sems_out=(), token=False):
    n, k = len(hbm), len(sems_out)
    ins = [pltpu.with_memory_space_constraint(a, pltpu.HBM) for a in hbm] + list(sems_in)
    in_specs = [HBM] * n + [SEM] * len(sems_in)
    if after is not None:
        ins.append(after)
        in_specs.append(ANY)
    out_shape = [pltpu.SemaphoreType.DMA((s,)) for s in sems_out] + [pltpu.HBM(a.shape, a.dtype) for a in hbm]
    out_specs = [SEM] * k + [HBM] * n
    if token:
        out_shape.append(_sds(TOKEN))
        out_specs.append(pl.BlockSpec(memory_space=pltpu.VMEM))
    res = pl.pallas_call(
        body, name=name, in_specs=in_specs, out_specs=out_specs, out_shape=out_shape,
        input_output_aliases={i: k + i for i in range(n)},
        compiler_params=pltpu.CompilerParams(has_side_effects=pltpu.SideEffectType.DATAFLOW_SIDE_EFFECTING))(*ins)
    return list(res[:k]), list(res[k:k + n]), (res[k + n] if token else None)


def _remote(src, dst, send, recv, to):
    return pltpu.make_async_remote_copy(src_ref=src, dst_ref=dst, send_sem=send, recv_sem=recv, device_id=to,
                                        device_id_type=MESH)


def _gather_start(name, bufs, after=None, pair=False):
    n = len(bufs)
    o = n + (after is not None)

    def body(*refs):
        ins, (send, recv), token = refs[:n], refs[o:o + 2], refs[-1]
        x, y, c, chips = _place()
        for i in range(n):
            mine = ins[i].at[2 * x + y, c]
            for k, chip in enumerate(chips):
                _remote(mine, mine, send.at[3 * i + k], recv.at[3 * i + k], (*chip, c)).start()
            if pair:
                _remote(mine, mine, send.at[3 * n + i], recv.at[3 * n + i], (x, y, 1 - c)).start()
        token[...] = jnp.zeros(TOKEN, F32)

    n_sems = (3 + pair) * n
    return _comm_call(name, body, bufs, after=after, sems_out=(n_sems, n_sems), token=True)


def _gather_forward(name, bufs, sems, after, pair=False):
    n = len(bufs)
    o = n + 2 + (after is not None)

    def body(*refs):
        ins, (send, recv), (send2, recv2), token = refs[:n], refs[n:n + 2], refs[o:o + 2], refs[-1]
        x, y, c, chips = _place()
        for i in range(n):
            mine = ins[i].at[2 * x + y, c]
            for k, chip in enumerate(chips):
                land = ins[i].at[2 * chip[0] + chip[1], c]
                first = _remote(mine, land, send.at[3 * i + k], recv.at[3 * i + k], (*chip, c))
                first.wait_send()
                first.wait_recv()
                _remote(land, land, send2.at[3 * i + k], recv2.at[3 * i + k], (x, y, 1 - c)).start()
            if pair:
                own = _remote(mine, ins[i].at[2 * x + y, 1 - c], send.at[3 * n + i], recv.at[3 * n + i], (x, y, 1 - c))
                own.wait_send()
                own.wait_recv()
        token[...] = jnp.zeros(TOKEN, F32)

    return _comm_call(name, body, bufs, sems_in=sems, after=after, sems_out=(3 * n, 3 * n), token=True)


def _gather_finish(name, bufs, sems, after):
    n = len(bufs)

    def body(*refs):
        ins, (send, recv) = refs[:n], refs[n:n + 2]
        x, y, c, chips = _place()
        for i in range(n):
            for k, chip in enumerate(chips):
                idx = 2 * chip[0] + chip[1]
                cp = _remote(ins[i].at[idx, c], ins[i].at[idx, 1 - c], send.at[3 * i + k], recv.at[3 * i + k], (x, y, 1 - c))
                cp.wait_send()
                cp.wait_recv()

    return _comm_call(name, body, bufs, sems_in=sems, after=after)[1]


def _swap_start(name, grads):
    n = len(grads)
    lands = [lax.empty((g.shape[0],) + g.shape[2:], g.dtype) for g in grads]

    def body(*refs):
        ins, lnd, (send, recv), token = refs[:n], refs[n:2 * n], refs[2 * n:2 * n + 2], refs[-1]
        x, y, c, _ = _place()
        for i in range(n):
            _remote(ins[i].at[:, 1 - c], lnd[i], send.at[i], recv.at[i], (x, y, 1 - c)).start()
        token[...] = jnp.zeros(TOKEN, F32)

    return _comm_call(name, body, list(grads) + lands, sems_out=(n, n), token=True)


def _swap_wait(name, hbm, sems, after):
    n = len(hbm) // 2

    def body(*refs):
        ins, lnd, (send, recv) = refs[:n], refs[n:2 * n], refs[2 * n:2 * n + 2]
        x, y, c, _ = _place()
        for i in range(n):
            cp = _remote(ins[i].at[:, 1 - c], lnd[i], send.at[i], recv.at[i], (x, y, 1 - c))
            cp.wait_send()
            cp.wait_recv()

    out = _comm_call(name, body, hbm, sems_in=sems, after=after)[1]
    return out[:n], out[n:]


def _pair_add(name, g, got, cm_idx):
    nk, _, r, c = g.shape
    tr = _tile(r, max(2 * SUBLANES, COPY_BLOCK_BYTES // (4 * c)), 2 * SUBLANES)

    def body(cm_ref, g_ref, x_ref, o_ref, land_ref):
        s = (g_ref[...] + x_ref[...]).astype(o_ref.dtype)
        o_ref[...] = s

        @pl.when(pl.program_id(1) == cm_ref[1])
        def _():
            land_ref[...] = s

    grid_spec = pltpu.PrefetchScalarGridSpec(
        num_scalar_prefetch=1, grid=(r // tr, nk),
        in_specs=[pl.BlockSpec((None, None, tr, c), lambda i, k, cm: (k, cm[0], i, 0)),
                  pl.BlockSpec((None, tr, c), lambda i, k, cm: (k, i, 0))],
        out_specs=[pl.BlockSpec((None, tr, c), lambda i, k, cm: (k, i, 0)),
                   pl.BlockSpec((None, tr, c), lambda i, k, cm: (cm[1], i, 0))])
    return pl.pallas_call(body, name=name, grid_spec=grid_spec, out_shape=[_sds((nk, r, c), BF16)] * 2,
                          compiler_params=_cparams(("parallel", "arbitrary")))(cm_idx, g, got)


def _scatter_start(name, parts, lands):
    n = len(parts)

    def body(*refs):
        ins, lnd, (send, recv), token = refs[:n], refs[n:2 * n], refs[2 * n:2 * n + 2], refs[-1]
        x, y, c, chips = _place()
        for i in range(n):
            for k, chip in enumerate(chips):
                _remote(ins[i].at[2 * chip[0] + chip[1]], lnd[i].at[2 * x + y], send.at[3 * i + k], recv.at[3 * i + k],
                        (*chip, c)).start()
        token[...] = jnp.zeros(TOKEN, F32)

    return _comm_call(name, body, list(parts) + list(lands), sems_out=(3 * n, 3 * n), token=True)


def _scatter_wait(name, hbm, sems, after):
    n = len(hbm) // 2

    def body(*refs):
        ins, lnd, (send, recv) = refs[:n], refs[n:2 * n], refs[2 * n:2 * n + 2]
        x, y, c, chips = _place()
        for i in range(n):
            for k, chip in enumerate(chips):
                idx = 2 * chip[0] + chip[1]
                cp = _remote(ins[i].at[idx], lnd[i].at[idx], send.at[3 * i + k], recv.at[3 * i + k], (*chip, c))
                cp.wait_send()
                cp.wait_recv()

    out = _comm_call(name, body, hbm, sems_in=sems, after=after)[1]
    return out[:n], out[n:]


def _sum_leading(name, x, out_dtype=F32):
    nk, r, c = x.shape
    tc = _tile(c, ELEMENTWISE_COLS)
    tr = _tile(r, max(2 * SUBLANES, COPY_BLOCK_BYTES // (nk * tc * x.dtype.itemsize)), 2 * SUBLANES)

    def body(x_ref, o_ref):
        acc = x_ref[0].astype(F32)
        for k in range(1, nk):
            acc = acc + x_ref[k].astype(F32)
        o_ref[...] = acc.astype(o_ref.dtype)

    return pl.pallas_call(body, name=name, grid=(r // tr, c // tc),
                          in_specs=[pl.BlockSpec((nk, tr, tc), lambda i, j: (0, i, j))],
                          out_specs=pl.BlockSpec((tr, tc), lambda i, j: (i, j)), out_shape=_sds((r, c), out_dtype),
                          compiler_params=_cparams(("parallel", "parallel")))(x)


def _exchange_start(name, halves):
    n = len(halves)
    lands = [lax.empty(h.shape, h.dtype) for h in halves]

    def body(*refs):
        ins, lnd, (send, recv), token = refs[:n], refs[n:2 * n], refs[2 * n:2 * n + 2], refs[-1]
        x, y, c, _ = _place()
        for i in range(n):
            _remote(ins[i], lnd[i], send.at[i], recv.at[i], (x, y, 1 - c)).start()
        token[...] = jnp.zeros(TOKEN, F32)

    return _comm_call(name, body, list(halves) + lands, sems_out=(n, n), token=True)


def _exchange_wait(name, hbm, sems, after):
    n = len(hbm) // 2

    def body(*refs):
        ins, lnd, (send, recv) = refs[:n], refs[n:2 * n], refs[2 * n:2 * n + 2]
        x, y, c, _ = _place()
        for i in range(n):
            cp = _remote(ins[i], lnd[i], send.at[i], recv.at[i], (x, y, 1 - c))
            cp.wait_send()
            cp.wait_recv()

    out = _comm_call(name, body, hbm, sems_in=sems, after=after)[1]
    return out[:n], out[n:]


SMALL = ("attn_sinks", "pool_w", "pool_scale", "ssm_lam_re", "ssm_lam_im", "ssm_log_dt", "ssm_b_re", "ssm_b_im",
         "ssm_c_re", "ssm_c_im", "ssm_d", "ln1_g", "ln1_b", "ffn_conv_b", "ln2_g", "ln2_b")
BIG = ("w_in", "ssm_glu_w", "w_out", "ffn_w_up", "ffn_conv_w", "ffn_w_down")
ALL_W = ("w_in", "attn_sinks", "pool_w", "pool_scale", "ssm_lam_re", "ssm_lam_im", "ssm_log_dt", "ssm_b_re", "ssm_b_im",
         "ssm_c_re", "ssm_c_im", "ssm_d", "ssm_glu_w", "w_out", "ln1_g", "ln1_b", "ffn_w_up", "ffn_conv_w", "ffn_conv_b",
         "ffn_w_down", "ln2_g", "ln2_b")
PACK_UNIT = SUBLANES * LANES


def _padded(n):
    return -(-n // PACK_UNIT) * PACK_UNIT


def _pack(arrs):
    cols = []
    for name in SMALL:
        a = arrs[name].reshape(DEPTH, -1)
        cols.append(jnp.pad(a, ((0, 0), (0, _padded(a.shape[1]) - a.shape[1]))))
    return jnp.concatenate(cols, axis=1).reshape(-1, LANES)


def _unpack(packed, shapes):
    flat = packed.reshape(DEPTH, -1)
    out, off = {}, 0
    for name in SMALL:
        n = math.prod(shapes[name][1:])
        out[name] = flat[:, off:off + n].reshape(shapes[name])
        off += _padded(n)
    return out


def _b_rows(b):
    return b.transpose(2, 0, 1).reshape(SSM_GROUP, SSM_CH)


def _b_unrows(b):
    return b.reshape(SSM_GROUP, SSM_N_GROUPS, SSM_STATE).transpose(1, 2, 0)


def _block_diag_in(bb):
    eye = jnp.eye(SSM_N_GROUPS, dtype=F32)
    b3 = bb.reshape(SSM_GROUP, SSM_N_GROUPS, SSM_STATE)
    return jnp.einsum("hgp,gk->ghkp", b3, eye).reshape(SSM_WIDTH, SSM_CH)


def _c_unrows(c):
    return c.reshape(SSM_GROUP, SSM_N_GROUPS, SSM_STATE).transpose(1, 0, 2)


def _block_diag_out(cc):
    eye = jnp.eye(SSM_N_GROUPS, dtype=F32)
    return jnp.einsum("ghp,gk->gpkh", cc, eye).reshape(SSM_CH, SSM_WIDTH)


def _rows_layout(re, im):
    n = re.shape[1]
    return jnp.stack([re.reshape(SCAN_NB, SCAN_CW, n), im.reshape(SCAN_NB, SCAN_CW, n)], axis=1).reshape(2 * SSM_CH, n)


H_POOL0 = ATTN_WIDTH + 2 * KV_WIDTH
H_SSM0 = H_POOL0 + POOL_WIDTH


def _ssm_params(p):
    lr = p["ssm_lam_re"].reshape(1, SSM_CH)
    li = p["ssm_lam_im"].reshape(1, SSM_CH)
    ldt = jnp.repeat(p["ssm_log_dt"], SSM_STATE).reshape(1, SSM_CH)
    return lr, li, ldt, _b_rows(p["ssm_b_re"]), _b_rows(p["ssm_b_im"])


def _layer_fwd(x, xb, p, wg, rope_t, dep, pre, mid):
    cos_t, sin_t = rope_t
    h = _mm_shard_cols("in_proj", xb, wg["w_in"], dep=dep)
    qk = _rope("rope_fwd", h, 0, Q_TILES + KV_TILES, cos_t, sin_t, MM_DTYPE)
    y_attn, y_attn_b = _attn_fwd(qk, h, p["attn_sinks"])
    y_pool = _pool_fwd(h, p["pool_w"], p["pool_scale"].reshape(1, POOL_WIDTH))
    ssm_in = _ssm_params(p)
    ar, ai, bbr, bbi = _ssm_prep(*ssm_in)
    bd = _scan_layout(_block_diag_in(bbr), _block_diag_in(bbi)).astype(MM_DTYPE)
    cc = _rows_layout(_block_diag_out(p["ssm_c_re"]), -_block_diag_out(p["ssm_c_im"])).astype(MM_DTYPE)
    dvec = p["ssm_d"].reshape(1, SSM_WIDTH)
    up = _time_permute(h[:, H_SSM0:])
    xx = _mm_nn("ssm_bu", up, bd, tn=1024)
    ss = _ssm_scan("ssm_scan_fwd", _scan_layout(ar, ai), xx, False)
    yp = _mm_nn("ssm_cs", ss, cc, tk=1024)
    yf, gy = _ssm_gelu(yp, up, dvec)
    ab = _mm_shard_cols("ssm_glu_proj", gy, wg["ssm_glu_w"])
    y_ssm = _time_unpermute(_ssm_glu(ab))
    mix = jnp.concatenate([y_attn_b, y_pool, y_ssm], axis=1)
    mixo = _mm_nn("out_proj", mix, wg["w_out"].reshape(MIX_WIDTH, D_MODEL), dep=pre(mix))
    r1, x1, x1b = _ln_fwd("ln1_fwd", x, mixo, p["ln1_g"].reshape(1, D_MODEL), p["ln1_b"].reshape(1, D_MODEL))
    tokens = mid(x1b)
    hf = _ffn_up(x1b, wg["ffn_w_up"], dep=tokens)
    conv_b = p["ffn_conv_b"].reshape(N_CHIPS, 1, FS)
    act = _ffn_act(hf, wg["ffn_conv_w"], conv_b)
    f = _ffn_down(act, wg["ffn_w_down"].reshape(2, FS, D_MODEL))
    r2, x2, x2b = _ln_fwd("ln2_fwd", x1, f, p["ln2_g"].reshape(1, D_MODEL), p["ln2_b"].reshape(1, D_MODEL))
    saved = dict(xb=xb, h=h, qk=qk, y_attn=y_attn, ssm_in=ssm_in, ar=ar, ai=ai, bd=bd, cc=cc, dvec=dvec, up=up, ss=ss, yf=yf,
                 gy=gy, ab=ab, mix=mix, r1=r1, x1b=x1b, hf=hf, conv_b=conv_b, act=act, r2=r2)
    return x2, x2b, saved


def _layer_bwd(da, db, p, wg, sv, rope_t, run, start):
    cos_t, sin_t = rope_t
    small = {}
    dr2, dr2b, dg, dbeta = _ln_bwd("ln2_bwd" if db is not None else "ln2_bwd_last", sv["r2"], p["ln2_g"].reshape(1, D_MODEL),
                                   p["ln2_b"].reshape(1, D_MODEL), da, db, dep=run("h0", None))
    small["ln2_g"], small["ln2_b"] = dg, dbeta
    w_down = wg["ffn_w_down"].reshape(2, FS, D_MODEL)
    dact = _ffn_down_dact(dr2b, w_down)
    dw_down = _ffn_down_dw(sv["act"], dr2b)
    dh_ffn, dcw, dcb = _ffn_act_bwd(sv["hf"], wg["ffn_conv_w"], sv["conv_b"], dact, dep=run("h1", dw_down))
    dconv_w = dcw.transpose(1, 0, 2, 3).reshape(N_CHIPS, CONV_WIDTH, FS)
    small["ffn_conv_b"] = dcb.transpose(1, 0, 2, 3)
    dw_up = _ffn_up_dw(sv["x1b"], dh_ffn)
    tok = [start("ffn", {"ffn_w_up": dw_up, "ffn_conv_w": dconv_w,
                         "ffn_w_down": dw_down.reshape(N_CHIPS, FS // 2, D_MODEL)})] + run("h2", dw_up)
    dx1_ffn = _ffn_up_dx(dh_ffn, wg["ffn_w_up"], dep=tok)
    dr1, dr1b, dg, dbeta = _ln_bwd("ln1_bwd", sv["r1"], p["ln1_g"].reshape(1, D_MODEL), p["ln1_b"].reshape(1, D_MODEL), dr2,
                                   dx1_ffn, dep=tok)
    small["ln1_g"], small["ln1_b"] = dg, dbeta
    w_out = wg["w_out"].reshape(MIX_WIDTH, D_MODEL)
    dw_out = _mm_tn("out_proj_dw", sv["mix"], dr1b)
    dmix = _mm_nt("out_proj_dx", dr1b, w_out, dep=run("h3", dw_out))
    dq, dkc, dkp, dvc, dvp, dsk = _attn_bwd(sv["qk"], sv["h"], p["attn_sinks"], sv["y_attn"], dmix, 0)
    small["attn_sinks"] = dsk[:, :, 0]
    dh_attn = _attn_dh(dq, dkc, dkp, dvc, dvp, cos_t, -sin_t)
    dh_pool, dpw, dps = _pool_bwd(sv["h"], p["pool_w"], p["pool_scale"].reshape(1, POOL_WIDTH), dmix, ATTN_WIDTH // POOL_WIDTH)
    small["pool_w"], small["pool_scale"] = dpw, dps
    dout_p = _time_permute(dmix[:, ATTN_WIDTH + POOL_WIDTH:])
    dab = _ssm_glu_bwd(sv["ab"], dout_p)
    dgy = _mm_shard_cols_nt("ssm_glu_dx", dab, wg["ssm_glu_w"])
    dw_glu = _mm_shard_cols_tn("ssm_glu_dw", sv["gy"], dab, N_CHIPS)
    dyf, du1, dd = _ssm_gelu_bwd(sv["yf"], dgy, sv["up"], sv["dvec"])
    small["ssm_d"] = dd
    dss = _mm_nt("ssm_cs_dx", dyf, sv["cc"], tn=1024)
    dcre, dcim = _scan_unlayout(_ssm_diag("ssm_c_diag", _mm_tn("ssm_cs_dw", dyf, sv["ss"], tn=1024)))
    small["ssm_c_re"], small["ssm_c_im"] = _c_unrows(dcre), -_c_unrows(dcim)
    gg, da8 = _ssm_scan("ssm_scan_bwd", _scan_layout(sv["ar"], -sv["ai"]), dss, True, sv["ss"])
    du2 = _mm_nt("ssm_bu_dx", gg, sv["bd"], tk=1024)
    dbbr, dbbi = _scan_unlayout(_ssm_diag("ssm_b_diag", _mm_tn("ssm_bu_dw", sv["up"], gg, tn=1024)))
    dar8, dai8 = _scan_unlayout(da8)
    dlr, dli, dldt, dbr, dbi = _ssm_prep_bwd(*sv["ssm_in"], dar8, dai8, dbbr, dbbi)
    small["ssm_lam_re"], small["ssm_lam_im"] = dlr, dli
    small["ssm_log_dt"] = dldt.reshape(SSM_N_GROUPS, SSM_STATE).sum(axis=1)
    small["ssm_b_re"], small["ssm_b_im"] = _b_unrows(dbr), _b_unrows(dbi)
    dh_ssm = _time_unpermute(_add2("ssm_du", du1, du2, MM_DTYPE))
    dh = jnp.concatenate([dh_attn, dh_pool, dh_ssm], axis=1)
    dx_in = _mm_shard_cols_nt("in_proj_dx", dh, wg["w_in"], dep=run("h4", dh))
    dw_in = _mm_shard_cols_tn("in_proj_dw", sv["xb"], dh, N_CHIPS)
    start("mix", {"w_in": dw_in, "ssm_glu_w": dw_glu, "w_out": dw_out.reshape(N_CHIPS, MIX_WIDTH // N_CHIPS, D_MODEL)})
    return dr1, dx_in, small


CONV_PAD = 2 * SUBLANES


def _halved(name, a):
    if name == "ffn_conv_w":
        a = jnp.pad(a, ((0, 0), (0, CONV_PAD - CONV_WIDTH), (0, 0)))
    return a.reshape(a.shape[0], 2, a.shape[1] // 2, a.shape[2])


def _unhalved(name, a):
    a = a.reshape(a.shape[:-3] + (2 * a.shape[-2], a.shape[-1]))
    return a[..., :CONV_WIDTH, :] if name == "ffn_conv_w" else a


class _Reduce:
    def __init__(self, tag, grads, cm_idx):
        self.tag, self.cm_idx, self.names = tag, cm_idx, tuple(grads)
        g4 = [_halved(name, grads[name]) for name in self.names]
        self.sems, self.hbm, self.token = _swap_start("grad_swap_start_" + tag, g4)

    def swapped(self, after):
        g4, got = _swap_wait("grad_swap_wait_" + self.tag, self.hbm, self.sems, after)
        parts, lands = zip(*[_pair_add("grad_pair_add", g, x, self.cm_idx) for g, x in zip(g4, got)])
        self.sems, self.hbm, self.token = _scatter_start("grad_scatter_start_" + self.tag, parts, lands)
        return self.token

    def scattered(self, after):
        _, recv = _scatter_wait("grad_scatter_wait_" + self.tag, self.hbm, self.sems, after)
        halves = [_sum_leading("grad_chip_sum", r) for r in recv]
        self.sems, self.hbm, self.token = _exchange_start("grad_exchange_start_" + self.tag, halves)
        return self.token

    def finish(self, after):
        return _exchange_wait("grad_exchange_wait_" + self.tag, self.hbm, self.sems, after)


def kernel(x, w_in, attn_sinks, pool_w, pool_scale, ssm_lam_re, ssm_lam_im, ssm_log_dt, ssm_b_re, ssm_b_im, ssm_c_re, ssm_c_im, ssm_d, ssm_glu_w, w_out, ln1_g, ln1_b, ffn_w_up, ffn_conv_w, ffn_conv_b, ffn_w_down, ln2_g, ln2_b, loss_target, m_w_in, m_attn_sinks, m_pool_w, m_pool_scale, m_ssm_lam_re, m_ssm_lam_im, m_ssm_log_dt, m_ssm_b_re, m_ssm_b_im, m_ssm_c_re, m_ssm_c_im, m_ssm_d, m_ssm_glu_w, m_w_out, m_ln1_g, m_ln1_b, m_ffn_w_up, m_ffn_conv_w, m_ffn_conv_b, m_ffn_w_down, m_ln2_g, m_ln2_b, v_w_in, v_attn_sinks, v_pool_w, v_pool_scale, v_ssm_lam_re, v_ssm_lam_im, v_ssm_log_dt, v_ssm_b_re, v_ssm_b_im, v_ssm_c_re, v_ssm_c_im, v_ssm_d, v_ssm_glu_w, v_w_out, v_ln1_g, v_ln1_b, v_ffn_w_up, v_ffn_conv_w, v_ffn_conv_b, v_ffn_w_down, v_ln2_g, v_ln2_b):
    w = dict(w_in=w_in, attn_sinks=attn_sinks, pool_w=pool_w, pool_scale=pool_scale, ssm_lam_re=ssm_lam_re,
             ssm_lam_im=ssm_lam_im, ssm_log_dt=ssm_log_dt, ssm_b_re=ssm_b_re, ssm_b_im=ssm_b_im, ssm_c_re=ssm_c_re,
             ssm_c_im=ssm_c_im, ssm_d=ssm_d, ssm_glu_w=ssm_glu_w, w_out=w_out, ln1_g=ln1_g, ln1_b=ln1_b, ffn_w_up=ffn_w_up,
             ffn_conv_w=ffn_conv_w, ffn_conv_b=ffn_conv_b, ffn_w_down=ffn_w_down, ln2_g=ln2_g, ln2_b=ln2_b)
    m = dict(w_in=m_w_in, attn_sinks=m_attn_sinks, pool_w=m_pool_w, pool_scale=m_pool_scale, ssm_lam_re=m_ssm_lam_re,
             ssm_lam_im=m_ssm_lam_im, ssm_log_dt=m_ssm_log_dt, ssm_b_re=m_ssm_b_re, ssm_b_im=m_ssm_b_im, ssm_c_re=m_ssm_c_re,
             ssm_c_im=m_ssm_c_im, ssm_d=m_ssm_d, ssm_glu_w=m_ssm_glu_w, w_out=m_w_out, ln1_g=m_ln1_g, ln1_b=m_ln1_b,
             ffn_w_up=m_ffn_w_up, ffn_conv_w=m_ffn_conv_w, ffn_conv_b=m_ffn_conv_b, ffn_w_down=m_ffn_w_down, ln2_g=m_ln2_g,
             ln2_b=m_ln2_b)
    v = dict(w_in=v_w_in, attn_sinks=v_attn_sinks, pool_w=v_pool_w, pool_scale=v_pool_scale, ssm_lam_re=v_ssm_lam_re,
             ssm_lam_im=v_ssm_lam_im, ssm_log_dt=v_ssm_log_dt, ssm_b_re=v_ssm_b_re, ssm_b_im=v_ssm_b_im, ssm_c_re=v_ssm_c_re,
             ssm_c_im=v_ssm_c_im, ssm_d=v_ssm_d, ssm_glu_w=v_ssm_glu_w, w_out=v_w_out, ln1_g=v_ln1_g, ln1_b=v_ln1_b,
             ffn_w_up=v_ffn_w_up, ffn_conv_w=v_ffn_conv_w, ffn_conv_b=v_ffn_conv_b, ffn_w_down=v_ffn_w_down, ln2_g=v_ln2_g,
             ln2_b=v_ln2_b)
    c_pos = lax.axis_index("c").astype(jnp.int32)
    chip = (2 * lax.axis_index("x") + lax.axis_index("y")).astype(jnp.int32)
    c_idx, chip_idx, cm_idx = c_pos.reshape(1), chip.reshape(1), jnp.stack([c_pos, chip])
    rope_t = _rope_tables()
    xs = x.reshape(SEQ, D_MODEL)
    xb = xs.astype(MM_DTYPE)
    for t in (w, m, v):
        t["ffn_w_up"] = jnp.swapaxes(t["ffn_w_up"], 1, 2)
    wh, mh, vh = ({n: _halved(n, t[n]) for n in BIG} for t in (w, m, v))

    def place(l):
        return [_cast_place("place_" + n, wh[n], l, chip_idx, F32 if n == "ffn_conv_w" else MM_DTYPE) for n in BIG]

    n_mix = BIG.index("ffn_w_up")

    def gather_start(l, after):
        bufs = place(l)
        return (_gather_start("gather_start_%d_mix" % l, bufs[:n_mix], after),
                _gather_start("gather_start_%d_ffn" % l, bufs[n_mix:], after))

    def gather_forward(l, group, started, after):
        return _gather_forward("gather_forward_%d_%s" % (l, group), started[1], started[0], after)

    def gather_finish(l, group, forwarded, after):
        bufs = _gather_finish("gather_finish_%d_%s" % (l, group), forwarded[1], forwarded[0], after)
        names = BIG[:n_mix] if group == "mix" else BIG[n_mix:]
        return bufs, {n: _unhalved(n, g) for n, g in zip(names, bufs)}

    def gather_wait(l, group, started, after):
        return gather_finish(l, group, gather_forward(l, group, started, after), after)

    flight = gather_start(0, None)
    gathered, saved = [gather_wait(0, "mix", flight[0], None)[1]], []
    for l in range(DEPTH):
        nxt = {}

        def pre(after):
            if l == 0:
                return []
            nxt["forwarded"] = gather_forward(l, "ffn", flight[1], after)
            return [nxt["forwarded"][2]]

        def mid(after):
            forwarded = gather_forward(l, "ffn", flight[1], after) if l == 0 else nxt["forwarded"]
            bufs, wg_ffn = gather_finish(l, "ffn", forwarded, after)
            gathered[l].update(wg_ffn)
            if l + 1 == DEPTH:
                return []
            nxt["flight"] = gather_start(l + 1, bufs[0])
            return [nxt["flight"][0][2], nxt["flight"][1][2]]

        xs, xb, sv = _layer_fwd(xs, xb, {n: w[n][l] for n in SMALL}, gathered[l], rope_t, flight[1][2] if l == 0 else None, pre,
                                mid)
        saved.append(sv)
        if l + 1 < DEPTH:
            flight = nxt["flight"]
            gathered.append(gather_wait(l + 1, "mix", flight[0], xb)[1])
    dy, loss_tile = _loss(xs, loss_target.reshape(SEQ, D_MODEL))
    loss = lax.psum(loss_tile[0, 0], ("x", "y", "c"))

    big_out = {n: None for n in BIG}
    small_g = {n: [None] * DEPTH for n in SMALL}
    agenda = {}
    tail = []
    plan = {"ffn": (("h3", 0), ("h1", -1), ("h2", -1)), "mix": (("h1", -1), ("h3", -1), ("h4", -1))}
    tail_rank = {("ffn", 1): 0, ("mix", 0): 1, ("ffn", 2): 2, ("mix", 1): 3, ("mix", 2): 4}
    started = []

    def book(l, group, red):
        def update(after):
            names, own, got = red.names, *red.finish(after)
            for n, o, g in zip(names, own, got):
                big_out[n] = _adamw_big("adamw_" + n, l, c_idx, wh[n], mh[n], vh[n], o, g, big_out[n])
            return [big_out[names[-1]][0]] if l == 0 else []

        steps = (lambda a: [red.swapped(a)], lambda a: [red.scattered(a)], update)
        for k, ((hook, dl), step) in enumerate(zip(plan[group], steps)):
            if l + dl >= 0:
                agenda.setdefault((l + dl, hook), []).append(step)
            else:
                tail.append((tail_rank[group, k], step))

    def run_at(l):
        return lambda hook, after: [t for step in agenda.pop((l, hook), []) for t in step(after)]

    def start_at(l):
        def start(group, grads):
            red = _Reduce("%s_%d" % (group, l), grads, cm_idx)
            book(l, group, red)
            started.append(red.token)
            return red.token
        return start

    da, db, carry = dy, None, []
    for l in reversed(range(DEPTH)):
        agenda.setdefault((l, "h0"), []).append(lambda after, carry=carry: carry)
        da, db, small = _layer_bwd(da, db, {n: w[n][l] for n in SMALL}, gathered[l], saved[l], rope_t, run_at(l), start_at(l))
        for n in SMALL:
            small_g[n][l] = small[n].reshape(w[n].shape[1:])
        carry = run_at(l)("end", db) + started[-1:]
    shapes = {n: w[n].shape for n in SMALL}
    part = _pack({n: jnp.stack(small_g[n]) for n in SMALL})
    slots = lax.dynamic_update_slice(jnp.zeros((N_CHIPS, 2) + part.shape, F32), part[None, None], (chip, c_pos, 0, 0))
    small_sems, small_bufs, after = _gather_start("gather_start_small", [slots], pair=True)
    for _, step in sorted(tail, key=lambda rs: rs[0]):
        after = (step(after) or [after])[-1]
    grad_x = _ln_in_grad(da, db).reshape(x.shape)
    small_sems, small_bufs, _ = _gather_forward("gather_forward_small", small_bufs, small_sems, after, pair=True)
    small_bufs = _gather_finish("gather_finish_small", small_bufs, small_sems, grad_x)
    g_small = _sum_leading("small_grad_sum", small_bufs[0].reshape((N_DEV,) + part.shape))
    upd = _adamw("adamw_small", _pack(w), g_small, _pack(m), _pack(v))
    small_out = [_unpack(a, shapes) for a in (g_small,) + tuple(upd)]

    outs = [loss, grad_x]
    for kind in range(4):
        for n in ALL_W:
            if n in SMALL:
                outs.append(small_out[kind][n])
            else:
                o = _unhalved(n, big_out[n][kind])
                outs.append(jnp.swapaxes(o, 1, 2) if n == "ffn_w_up" else o)
    return tuple(outs)


def _ln_in_grad(dr1, dx_in):
    tm = _tile(SEQ, 512)

    def body(a_ref, b_ref, o_ref):
        o_ref[...] = DEEPNORM_ALPHA * a_ref[...] + b_ref[...]

    blk = pl.BlockSpec((tm, D_MODEL), lambda i: (i, 0))
    return pl.pallas_call(body, name="grad_x", grid=(SEQ // tm,), in_specs=[blk, blk], out_specs=blk,
                          out_shape=_sds((SEQ, D_MODEL)), compiler_params=_cparams(("parallel",)))(dr1, dx_in)
```

```python
import functools
import math

import jax
import jax.numpy as jnp
from jax import lax
from jax.experimental import pallas as pl
from jax.experimental.pallas import tpu as pltpu

F32 = jnp.float32
BF16 = jnp.bfloat16
MM_DTYPE = BF16

D_MODEL = 2048
SEQ = 2048
DEPTH = 4
D_FF = 5504
HEAD_DIM = 64
N_Q_HEADS = D_MODEL // 2 // HEAD_DIM
N_KV_HEADS = N_Q_HEADS // 4
ATTN_WIDTH = N_Q_HEADS * HEAD_DIM
KV_WIDTH = N_KV_HEADS * HEAD_DIM
ATTN_BLOCK = 128
ROPE_THETA = 10000.0
POOL_WINDOWS = (2, 4, 8, 16)
POOL_WIDTH = D_MODEL // 4
POOL_GROUP = POOL_WIDTH // len(POOL_WINDOWS)
SSM_WIDTH = D_MODEL // 4
SSM_GROUP = 16
SSM_N_GROUPS = SSM_WIDTH // SSM_GROUP
SSM_STATE = 64
SSM_CH = SSM_N_GROUPS * SSM_STATE
MIX_WIDTH = ATTN_WIDTH + POOL_WIDTH + SSM_WIDTH
IN_WIDTH = ATTN_WIDTH + 2 * KV_WIDTH + POOL_WIDTH + SSM_WIDTH
CONV_WIDTH = 3
LN_EPS = 1e-5
DEEPNORM_ALPHA = (2 * DEPTH) ** 0.25
ADAM_LR = 0.001
ADAM_B1 = 0.9
ADAM_B2 = 0.999
ADAM_EPS = 1e-08
ADAM_WD = 0.01
ADAM_STEP = 10

N_CHIPS = 4
N_DEV = 8
FS = 2 * D_FF // N_CHIPS
IN_S = IN_WIDTH // N_CHIPS
GLU_S = 2 * SSM_WIDTH // N_CHIPS
LANES = 128
SUBLANES = 8
SCAN_CW = 512
SCAN_NB = SSM_CH // SCAN_CW
SCAN_UNROLL = 4
VMEM_LIMIT = 56 * 1024 * 1024
COPY_BLOCK_BYTES = 6 * 1024 * 1024
NEG = -1e30

NN = (((1,), (0,)), ((), ()))
NT = (((1,), (1,)), ((), ()))
TN = (((0,), (0,)), ((), ()))
MESH = pl.DeviceIdType.MESH


def _tile(n, pref, mult=LANES):
    best = None
    for t in range(mult, min(n, pref) + 1, mult):
        if n % t == 0:
            best = t
    return n if best is None else best


def _cparams(sem):
    return pltpu.CompilerParams(dimension_semantics=sem, vmem_limit_bytes=VMEM_LIMIT)


def _sds(shape, dtype=F32):
    return jax.ShapeDtypeStruct(tuple(shape), dtype)


def _as_list(x):
    return [] if x is None else list(x) if isinstance(x, (list, tuple)) else [x]


def _mm(name, a, b, out_shape, grid, a_spec, b_spec, o_spec, dims, acc_shape, out_dtype=F32, dep=None):
    nk = grid[2]
    deps = _as_list(dep)

    def product(a_ref, b_ref):
        return lax.dot_general(a_ref[...].astype(MM_DTYPE), b_ref[...].astype(MM_DTYPE), dims, preferred_element_type=F32)

    def body_one(a_ref, b_ref, *rest):
        rest[-1][...] = product(a_ref, b_ref).astype(rest[-1].dtype)

    def body(a_ref, b_ref, *rest):
        o_ref, acc_ref = rest[-2:]
        k = pl.program_id(2)

        @pl.when(k == 0)
        def _():
            acc_ref[...] = product(a_ref, b_ref)

        @pl.when(k > 0)
        def _():
            acc_ref[...] += product(a_ref, b_ref)

        @pl.when(k == nk - 1)
        def _():
            o_ref[...] = acc_ref[...].astype(o_ref.dtype)

    return pl.pallas_call(
        body_one if nk == 1 else body, name=name, grid=grid, in_specs=[a_spec, b_spec] + [ANY] * len(deps),
        out_specs=o_spec, out_shape=_sds(out_shape, out_dtype),
        scratch_shapes=[] if nk == 1 else [pltpu.VMEM(acc_shape, F32)],
        compiler_params=_cparams(("parallel", "parallel", "arbitrary")))(a, b, *deps)


def _mm_nn(name, a, b, tm=2048, tn=512, tk=2048, out_dtype=F32, dep=None):
    m, kk = a.shape
    n = b.shape[1]
    tm, tn, tk = _tile(m, tm), _tile(n, tn), _tile(kk, tk)
    return _mm(name, a, b, (m, n), (m // tm, n // tn, kk // tk),
               pl.BlockSpec((tm, tk), lambda i, j, k: (i, k)), pl.BlockSpec((tk, tn), lambda i, j, k: (k, j)),
               pl.BlockSpec((tm, tn), lambda i, j, k: (i, j)), NN, (tm, tn), out_dtype, dep=dep)


def _mm_nt(name, a, b, tm=2048, tn=512, tk=2048, dep=None):
    m, kk = a.shape
    n = b.shape[0]
    tm, tn, tk = _tile(m, tm), _tile(n, tn), _tile(kk, tk)
    return _mm(name, a, b, (m, n), (m // tm, n // tn, kk // tk),
               pl.BlockSpec((tm, tk), lambda i, j, k: (i, k)), pl.BlockSpec((tn, tk), lambda i, j, k: (j, k)),
               pl.BlockSpec((tm, tn), lambda i, j, k: (i, j)), NT, (tm, tn), dep=dep)


def _mm_tn(name, a, b, tm=1024, tn=1024, ts=2048):
    s, m = a.shape
    n = b.shape[1]
    tm, tn, ts = _tile(m, tm), _tile(n, tn), _tile(s, ts)
    return _mm(name, a, b, (m, n), (m // tm, n // tn, s // ts),
               pl.BlockSpec((ts, tm), lambda i, j, k: (k, i)), pl.BlockSpec((ts, tn), lambda i, j, k: (k, j)),
               pl.BlockSpec((tm, tn), lambda i, j, k: (i, j)), TN, (tm, tn))


def _mm_shard_cols(name, a, w, tm=2048, tk=2048, dep=None):
    m, kk = a.shape
    nj, _, c = w.shape
    tm, tk = _tile(m, tm), _tile(kk, tk)
    return _mm(name, a, w, (m, nj * c), (m // tm, nj, kk // tk),
               pl.BlockSpec((tm, tk), lambda i, j, k: (i, k)), pl.BlockSpec((None, tk, c), lambda i, j, k: (j, k, 0)),
               pl.BlockSpec((tm, c), lambda i, j, k: (i, j)), NN, (tm, c), dep=dep)


def _mm_shard_cols_nt(name, d, w, tm=2048, tn=512, dep=None):
    m = d.shape[0]
    nj, n, c = w.shape
    tm, tn = _tile(m, tm), _tile(n, tn)
    return _mm(name, d, w, (m, n), (m // tm, n // tn, nj),
               pl.BlockSpec((tm, c), lambda i, j, k: (i, k)), pl.BlockSpec((None, tn, c), lambda i, j, k: (k, j, 0)),
               pl.BlockSpec((tm, tn), lambda i, j, k: (i, j)), NT, (tm, tn), dep=dep)


def _mm_shard_cols_tn(name, a, d, nj, tm=1024, ts=2048):
    s, m = a.shape
    c = d.shape[1] // nj
    tm, ts = _tile(m, tm), _tile(s, ts)
    return _mm(name, a, d, (nj, m, c), (nj, m // tm, s // ts),
               pl.BlockSpec((ts, tm), lambda j, i, k: (k, i)), pl.BlockSpec((ts, c), lambda j, i, k: (k, j)),
               pl.BlockSpec((None, tm, c), lambda j, i, k: (j, i, 0)), TN, (tm, c))


def _ffn_up(x1, w_up_t, tm=512, tk=2048, dep=None):
    s, d = x1.shape
    tm, tk = _tile(s, tm), _tile(d, tk)
    return _mm("ffn_up", x1, w_up_t, (N_CHIPS, s, FS), (N_CHIPS, s // tm, d // tk),
               pl.BlockSpec((tm, tk), lambda j, i, k: (i, k)), pl.BlockSpec((None, FS, tk), lambda j, i, k: (j, 0, k)),
               pl.BlockSpec((None, tm, FS), lambda j, i, k: (j, i, 0)), NT, (tm, FS), dep=dep)


def _ffn_down(act, w_down, tm=1024, tn=512):
    _, s, _ = act.shape
    d = w_down.shape[2]
    tm, tn = _tile(s, tm), _tile(d, tn)
    return _mm("ffn_down", act, w_down, (s, d), (s // tm, d // tn, 2),
               pl.BlockSpec((None, tm, FS), lambda i, j, k: (k, i, 0)), pl.BlockSpec((None, FS, tn), lambda i, j, k: (k, 0, j)),
               pl.BlockSpec((tm, tn), lambda i, j, k: (i, j)), NN, (tm, tn))


def _ffn_down_dact(df, w_down, tm=512, tk=2048):
    s, d = df.shape
    tm, tk = _tile(s, tm), _tile(d, tk)
    return _mm("ffn_down_dact", df, w_down, (2, s, FS), (2, s // tm, d // tk),
               pl.BlockSpec((tm, tk), lambda j, i, k: (i, k)), pl.BlockSpec((None, FS, tk), lambda j, i, k: (j, 0, k)),
               pl.BlockSpec((None, tm, FS), lambda j, i, k: (j, i, 0)), NT, (tm, FS))


def _ffn_down_dw(act, df, tn=512, ts=2048):
    _, s, _ = act.shape
    d = df.shape[1]
    tn, ts = _tile(d, tn), _tile(s, ts)
    return _mm("ffn_down_dw", act, df, (2, FS, d), (2, d // tn, s // ts),
               pl.BlockSpec((None, ts, FS), lambda p, j, k: (p, k, 0)), pl.BlockSpec((ts, tn), lambda p, j, k: (k, j)),
               pl.BlockSpec((None, FS, tn), lambda p, j, k: (p, 0, j)), TN, (FS, tn))


def _ffn_up_dx(dh, w_up_t, tm=1024, tn=1024, dep=None):
    s = dh.shape[2]
    d = w_up_t.shape[2]
    tm, tn = _tile(s, tm), _tile(d, tn)
    return _mm("ffn_up_dx", dh, w_up_t, (s, d), (s // tm, d // tn, N_CHIPS),
               pl.BlockSpec((None, None, tm, FS), lambda i, j, k: (k % 2, k // 2, i, 0)),
               pl.BlockSpec((None, FS, tn), lambda i, j, k: (k, 0, j)),
               pl.BlockSpec((tm, tn), lambda i, j, k: (i, j)), NN, (tm, tn), dep=dep)


def _ffn_up_dw(x1, dh, tn=512, ts=2048):
    s, d = x1.shape
    tn, ts = _tile(d, tn), _tile(s, ts)
    return _mm("ffn_up_dw", dh, x1, (N_CHIPS, FS, d), (N_CHIPS, d // tn, s // ts),
               pl.BlockSpec((None, None, ts, FS), lambda j, i, k: (j % 2, j // 2, k, 0)),
               pl.BlockSpec((ts, tn), lambda j, i, k: (k, i)),
               pl.BlockSpec((None, FS, tn), lambda j, i, k: (j, 0, i)), TN, (FS, tn))


def _rope_tables():
    half = HEAD_DIM // 2
    inv = ROPE_THETA ** (-jnp.arange(half, dtype=F32) / half)
    ang = jnp.arange(SEQ).astype(F32)[:, None] * inv[None, :]
    cos, sin = jnp.cos(ang), jnp.sin(ang)
    cos_t = jnp.tile(cos, (1, LANES // half))
    sin_t = jnp.tile(jnp.concatenate([-sin, sin], axis=1), (1, LANES // HEAD_DIM))
    return cos_t, sin_t


def _rotate_half(t):
    lane = lax.broadcasted_iota(jnp.int32, t.shape, 1)
    first = (lane % HEAD_DIM) < (HEAD_DIM // 2)
    return jnp.where(first, pltpu.roll(t, LANES - HEAD_DIM // 2, 1), pltpu.roll(t, HEAD_DIM // 2, 1))


def _rope(name, src, col_tile0, n_tiles, cos_t, sin_t, out_dtype):
    tm = _tile(SEQ, 512)
    assert col_tile0 % n_tiles == 0

    def body(x_ref, c_ref, s_ref, o_ref):
        cos, sin = c_ref[...], s_ref[...]
        for j in range(n_tiles):
            sl = slice(j * LANES, (j + 1) * LANES)
            t = x_ref[:, sl].astype(F32)
            o_ref[:, sl] = (t * cos + _rotate_half(t) * sin).astype(o_ref.dtype)

    wide = n_tiles * LANES
    return pl.pallas_call(
        body, name=name, grid=(SEQ // tm,),
        in_specs=[pl.BlockSpec((tm, wide), lambda i: (i, col_tile0 // n_tiles)),
                  pl.BlockSpec((tm, LANES), lambda i: (i, 0)), pl.BlockSpec((tm, LANES), lambda i: (i, 0))],
        out_specs=pl.BlockSpec((tm, wide), lambda i: (i, 0)),
        out_shape=_sds((SEQ, wide), out_dtype),
        compiler_params=_cparams(("parallel",)))(src, cos_t, sin_t)


Q_TILES = ATTN_WIDTH // LANES
KV_TILES = KV_WIDTH // LANES
Q_PER_KV_TILE = Q_TILES // KV_TILES
HEADS_PER_KV_TILE = N_Q_HEADS // KV_TILES
K_TILE0 = ATTN_WIDTH // LANES
V_TILE0 = (ATTN_WIDTH + KV_WIDTH) // LANES
N_QBLK = SEQ // ATTN_BLOCK


def _dup_half(t, which):
    lane = lax.broadcasted_iota(jnp.int32, t.shape, 1)
    r = pltpu.roll(t, HEAD_DIM, 1)
    lo = lane < HEAD_DIM
    return jnp.where(lo, t, r) if which == 0 else jnp.where(lo, r, t)


GROUP_HEADS = N_Q_HEADS // N_KV_HEADS
GROUP_ROWS = GROUP_HEADS * ATTN_BLOCK


def _attn_stack(tiles, lo):
    return jnp.concatenate([jnp.where(lo == (hs == 0), t, 0.0) for t in tiles for hs in range(2)], axis=0)


def _attn_unstack(x, j, lo):
    r = 2 * j * ATTN_BLOCK
    return jnp.where(lo, x[r:r + ATTN_BLOCK], x[r + ATTN_BLOCK:r + 2 * ATTN_BLOCK])


def _attn_group_consts(n, sink_ref, head0):
    shape = (GROUP_ROWS, 2 * ATTN_BLOCK)
    qi = lax.broadcasted_iota(jnp.int32, shape, 0) % ATTN_BLOCK
    col = lax.broadcasted_iota(jnp.int32, shape, 1)
    valid = ((col < ATTN_BLOCK) & (col <= qi)) | ((col >= ATTN_BLOCK) & (col - ATTN_BLOCK > qi) & (n > 0))
    head = lax.broadcasted_iota(jnp.int32, (GROUP_ROWS, 1), 0) // ATTN_BLOCK
    sinks = jnp.zeros((GROUP_ROWS, 1), F32)
    for i in range(GROUP_HEADS):
        sinks = jnp.where(head == i, sink_ref[head0 + i], sinks)
    return valid, sinks, head


def _attn_probs(qs, k2, valid, sinks):
    s = lax.dot_general(qs, k2, NT, preferred_element_type=F32) * HEAD_DIM ** -0.5
    s = jnp.where(valid, s, NEG)
    m = jnp.maximum(s.max(1, keepdims=True), sinks)
    p = jnp.exp(s - m)
    esink = jnp.exp(sinks - m)
    inv = 1.0 / (p.sum(1, keepdims=True) + esink)
    return p * inv, esink * inv


def _attn_kv(cur, prev, kvl):
    return jnp.concatenate([_dup_half(cur, kvl), _dup_half(prev, kvl)], axis=0).astype(MM_DTYPE)


def _attn_specs():
    blk = (ATTN_BLOCK, LANES)
    wide = (ATTN_BLOCK, Q_PER_KV_TILE * LANES)
    prev = lambda n: jnp.maximum(n - 1, 0)
    q_spec = pl.BlockSpec(wide, lambda t, n: (n, t))
    kc = pl.BlockSpec(blk, lambda t, n: (n, K_TILE0 + t))
    kp = pl.BlockSpec(blk, lambda t, n: (prev(n), K_TILE0 + t))
    vc = pl.BlockSpec(blk, lambda t, n: (n, V_TILE0 + t))
    vp = pl.BlockSpec(blk, lambda t, n: (prev(n), V_TILE0 + t))
    return q_spec, kc, kp, vc, vp, pl.BlockSpec(memory_space=pltpu.SMEM)


def _attn_fwd(qk, h, sinks):
    q_spec, kc_s, kp_s, vc_s, vp_s, smem = _attn_specs()

    def body(sink_ref, q_ref, kc_ref, kp_ref, vc_ref, vp_ref, o_ref, ob_ref):
        t, n = pl.program_id(0), pl.program_id(1)
        lo = lax.broadcasted_iota(jnp.int32, (ATTN_BLOCK, LANES), 1) < HEAD_DIM
        kc, kp = kc_ref[...].astype(F32), kp_ref[...].astype(F32)
        vc, vp = vc_ref[...], vp_ref[...]
        for kvl in range(2):
            valid, sink_rows, _ = _attn_group_consts(n, sink_ref, t * HEADS_PER_KV_TILE + kvl * GROUP_HEADS)
            tiles = [q_ref[:, a * LANES:(a + 1) * LANES].astype(F32) for a in (2 * kvl, 2 * kvl + 1)]
            qs = _attn_stack(tiles, lo).astype(MM_DTYPE)
            pn, _ = _attn_probs(qs, _attn_kv(kc, kp, kvl), valid, sink_rows)
            os_ = lax.dot_general(pn.astype(MM_DTYPE), _attn_kv(vc, vp, kvl), NN, preferred_element_type=F32)
            for j in range(2):
                a = 2 * kvl + j
                o = _attn_unstack(os_, j, lo)
                o_ref[:, a * LANES:(a + 1) * LANES] = o
                ob_ref[:, a * LANES:(a + 1) * LANES] = o.astype(ob_ref.dtype)

    return pl.pallas_call(
        body, name="attn_fwd", grid=(KV_TILES, N_QBLK),
        in_specs=[smem, q_spec, kc_s, kp_s, vc_s, vp_s], out_specs=[q_spec, q_spec],
        out_shape=[_sds((SEQ, ATTN_WIDTH)), _sds((SEQ, ATTN_WIDTH), MM_DTYPE)],
        compiler_params=_cparams(("parallel", "parallel")))(sinks, qk, qk, qk, h, h)


def _attn_bwd(qk, h, sinks, y, dy, dy_tile0, dep=None):
    deps = _as_list(dep)
    q_spec, kc_s, kp_s, vc_s, vp_s, smem = _attn_specs()
    blk = (ATTN_BLOCK, LANES)
    wide = (ATTN_BLOCK, Q_PER_KV_TILE * LANES)
    kv_out = pl.BlockSpec(blk, lambda t, n: (n, t))
    dy_spec = pl.BlockSpec(wide, lambda t, n: (n, t + dy_tile0))

    def body(sink_ref, q_ref, kc_ref, kp_ref, vc_ref, vp_ref, y_ref, dy_ref, *rest):
        dq_ref, dkc_ref, dkp_ref, dvc_ref, dvp_ref, dsk_ref = rest[-6:]
        t, n = pl.program_id(0), pl.program_id(1)
        lo = lax.broadcasted_iota(jnp.int32, blk, 1) < HEAD_DIM
        kc, kp = kc_ref[...].astype(F32), kp_ref[...].astype(F32)
        vc, vp = vc_ref[...], vp_ref[...]
        hrow = lax.broadcasted_iota(jnp.int32, (HEADS_PER_KV_TILE, LANES), 0)
        dsk = jnp.zeros((HEADS_PER_KV_TILE, LANES), F32)
        dk2, dv2 = [], []
        for kvl in range(2):
            valid, sink_rows, head = _attn_group_consts(n, sink_ref, t * HEADS_PER_KV_TILE + kvl * GROUP_HEADS)
            sls = [slice(a * LANES, (a + 1) * LANES) for a in (2 * kvl, 2 * kvl + 1)]
            qs = _attn_stack([q_ref[:, sl].astype(F32) for sl in sls], lo).astype(MM_DTYPE)
            dos = _attn_stack([dy_ref[:, sl] for sl in sls], lo)
            delta = _attn_stack([dy_ref[:, sl] * y_ref[:, sl] for sl in sls], lo).sum(1, keepdims=True)
            dos = dos.astype(MM_DTYPE)
            k2, v2 = _attn_kv(kc, kp, kvl), _attn_kv(vc, vp, kvl)
            pn, psink = _attn_probs(qs, k2, valid, sink_rows)
            dp = lax.dot_general(dos, v2, NT, preferred_element_type=F32)
            ds = (pn * (dp - delta) * HEAD_DIM ** -0.5).astype(MM_DTYPE)
            dqs = lax.dot_general(ds, k2, NN, preferred_element_type=F32)
            for j, sl in enumerate(sls):
                dq_ref[:, sl] = _attn_unstack(dqs, j, lo)
            for acc, x in ((dk2, lax.dot_general(ds, qs, TN, preferred_element_type=F32)),
                           (dv2, lax.dot_general(pn.astype(MM_DTYPE), dos, TN, preferred_element_type=F32))):
                acc.append(x + pltpu.roll(x, HEAD_DIM, 1))
            dsink = psink * delta
            for i in range(GROUP_HEADS):
                dsk = dsk + jnp.where(hrow == kvl * GROUP_HEADS + i, -jnp.sum(jnp.where(head == i, dsink, 0.0)), 0.0)
        for o_ref, src, r in ((dkc_ref, dk2, 0), (dkp_ref, dk2, ATTN_BLOCK), (dvc_ref, dv2, 0), (dvp_ref, dv2, ATTN_BLOCK)):
            o_ref[...] = jnp.where(lo, src[0][r:r + ATTN_BLOCK], src[1][r:r + ATTN_BLOCK])

        @pl.when(n == 0)
        def _():
            dsk_ref[...] = jnp.zeros_like(dsk_ref)

        dsk_ref[...] += dsk

    kv_shape = _sds((SEQ, KV_WIDTH))
    return pl.pallas_call(
        body, name="attn_bwd", grid=(KV_TILES, N_QBLK),
        in_specs=[smem, q_spec, kc_s, kp_s, vc_s, vp_s, q_spec, dy_spec] + [ANY] * len(deps),
        out_specs=[q_spec, kv_out, kv_out, kv_out, kv_out,
                   pl.BlockSpec((None, HEADS_PER_KV_TILE, LANES), lambda t, n: (t, 0, 0))],
        out_shape=[_sds((SEQ, ATTN_WIDTH)), kv_shape, kv_shape, kv_shape, kv_shape,
                   _sds((KV_TILES, HEADS_PER_KV_TILE, LANES))],
        compiler_params=_cparams(("parallel", "arbitrary")))(sinks, qk, qk, qk, h, h, y, dy, *deps)


def _attn_dh(dq, dkc, dkp, dvc, dvp, cos_t, nsin_t):
    n_tiles = Q_TILES + 2 * KV_TILES
    nxt = lambda n: jnp.minimum(n + 1, N_QBLK - 1)

    def body(dq_ref, kc_ref, kp_ref, vc_ref, vp_ref, c_ref, s_ref, o_ref):
        has_next = pl.program_id(0) < N_QBLK - 1
        cos, sin = c_ref[...], s_ref[...]

        def unrope(t):
            return t * cos + _rotate_half(t) * sin

        for j in range(Q_TILES):
            sl = slice(j * LANES, (j + 1) * LANES)
            o_ref[:, sl] = unrope(dq_ref[:, sl]).astype(o_ref.dtype)
        for j in range(KV_TILES):
            sl = slice(j * LANES, (j + 1) * LANES)
            t = kc_ref[:, sl] + jnp.where(has_next, kp_ref[:, sl], 0.0)
            o_ref[:, ATTN_WIDTH + j * LANES:ATTN_WIDTH + (j + 1) * LANES] = unrope(t).astype(o_ref.dtype)
        o_ref[:, ATTN_WIDTH + KV_WIDTH:] = (vc_ref[...] + jnp.where(has_next, vp_ref[...], 0.0)).astype(o_ref.dtype)

    qb, kb, tb = (ATTN_BLOCK, ATTN_WIDTH), (ATTN_BLOCK, KV_WIDTH), (ATTN_BLOCK, LANES)
    return pl.pallas_call(
        body, name="attn_dh", grid=(N_QBLK,),
        in_specs=[pl.BlockSpec(qb, lambda n: (n, 0)),
                  pl.BlockSpec(kb, lambda n: (n, 0)), pl.BlockSpec(kb, lambda n: (nxt(n), 0)),
                  pl.BlockSpec(kb, lambda n: (n, 0)), pl.BlockSpec(kb, lambda n: (nxt(n), 0)),
                  pl.BlockSpec(tb, lambda n: (n, 0)), pl.BlockSpec(tb, lambda n: (n, 0))],
        out_specs=pl.BlockSpec((ATTN_BLOCK, n_tiles * LANES), lambda n: (n, 0)),
        out_shape=_sds((SEQ, n_tiles * LANES), MM_DTYPE),
        compiler_params=_cparams(("parallel",)))(dq, dkc, dkp, dvc, dvp, cos_t, nsin_t)


POOL_TILE0 = (ATTN_WIDTH + 2 * KV_WIDTH) // POOL_WIDTH


def _shift_rows(x, d, down):
    n = x.shape[0]
    row = lax.broadcasted_iota(jnp.int32, x.shape, 0)
    if down:
        return jnp.where(row >= d, pltpu.roll(x, d, 0), 0.0)
    return jnp.where(row < n - d, pltpu.roll(x, n - d, 0), 0.0)


def _window_sum(x, w, down):
    d = 1
    while d < w:
        x = x + _shift_rows(x, d, down)
        d *= 2
    return x


def _pool_z(u, w):
    t = lax.broadcasted_iota(jnp.int32, u.shape, 0).astype(F32)
    cnt = jnp.minimum(t + 1.0, float(w))
    return _window_sum(u, w, True) / cnt - u, cnt


def _pool_fwd(h, pool_w, pool_scale):
    def body(u_ref, w_ref, s_ref, o_ref):
        for gi, w in enumerate(POOL_WINDOWS):
            sl = slice(gi * POOL_GROUP, (gi + 1) * POOL_GROUP)
            z, _ = _pool_z(u_ref[:, sl], w)
            o_ref[:, sl] = (lax.dot_general(z.astype(MM_DTYPE), w_ref[gi].astype(MM_DTYPE), NN,
                                            preferred_element_type=F32) * s_ref[:, sl]).astype(o_ref.dtype)

    return pl.pallas_call(
        body, name="pool_fwd", grid=(1,),
        in_specs=[pl.BlockSpec((SEQ, POOL_WIDTH), lambda i: (0, POOL_TILE0)),
                  pl.BlockSpec(pool_w.shape, lambda i: (0, 0, 0)), pl.BlockSpec((1, POOL_WIDTH), lambda i: (0, 0))],
        out_specs=pl.BlockSpec((SEQ, POOL_WIDTH), lambda i: (0, 0)),
        out_shape=_sds((SEQ, POOL_WIDTH), MM_DTYPE), compiler_params=_cparams(("arbitrary",)))(h, pool_w, pool_scale)


def _pool_bwd(h, pool_w, pool_scale, dmix, dy_tile0):
    def body(u_ref, w_ref, s_ref, dy_ref, du_ref, dw_ref, ds_ref):
        for gi, w in enumerate(POOL_WINDOWS):
            sl = slice(gi * POOL_GROUP, (gi + 1) * POOL_GROUP)
            z, cnt = _pool_z(u_ref[:, sl], w)
            zb, wb = z.astype(MM_DTYPE), w_ref[gi].astype(MM_DTYPE)
            dy = dy_ref[:, sl]
            zp = lax.dot_general(zb, wb, NN, preferred_element_type=F32)
            ds_ref[:, sl] = jnp.sum(dy * zp, axis=0, keepdims=True)
            dyo = (dy * s_ref[:, sl]).astype(MM_DTYPE)
            dw_ref[gi] = lax.dot_general(zb, dyo, TN, preferred_element_type=F32)
            dz = lax.dot_general(dyo, wb, NT, preferred_element_type=F32)
            du_ref[:, sl] = (_window_sum(dz / cnt, w, False) - dz).astype(du_ref.dtype)

    return pl.pallas_call(
        body, name="pool_bwd", grid=(1,),
        in_specs=[pl.BlockSpec((SEQ, POOL_WIDTH), lambda i: (0, POOL_TILE0)),
                  pl.BlockSpec(pool_w.shape, lambda i: (0, 0, 0)), pl.BlockSpec((1, POOL_WIDTH), lambda i: (0, 0)),
                  pl.BlockSpec((SEQ, POOL_WIDTH), lambda i: (0, dy_tile0))],
        out_specs=[pl.BlockSpec((SEQ, POOL_WIDTH), lambda i: (0, 0)), pl.BlockSpec(pool_w.shape, lambda i: (0, 0, 0)),
                   pl.BlockSpec((1, POOL_WIDTH), lambda i: (0, 0))],
        out_shape=[_sds((SEQ, POOL_WIDTH), MM_DTYPE), _sds(pool_w.shape), _sds((1, POOL_WIDTH))],
        compiler_params=_cparams(("arbitrary",)))(h, pool_w, pool_scale, dmix)


def _ssm_discretize(lr, li, ldt, br, bi):
    dt = jnp.exp(ldt)
    mag = jnp.exp(lr * dt)
    ar, ai = mag * jnp.cos(li * dt), mag * jnp.sin(li * dt)
    nr, ni = ar - 1.0, ai
    den = lr * lr + li * li
    zr = (nr * lr + ni * li) / den
    zi = (ni * lr - nr * li) / den
    return ar, ai, zr * br - zi * bi, zr * bi + zi * br


def _ssm_prep(lr, li, ldt, br, bi):
    def body(lr_ref, li_ref, ldt_ref, br_ref, bi_ref, ar_ref, ai_ref, bbr_ref, bbi_ref):
        outs = _ssm_discretize(lr_ref[...], li_ref[...], ldt_ref[...], br_ref[...], bi_ref[...])
        for o, v in zip((ar_ref, ai_ref, bbr_ref, bbi_ref), outs):
            o[...] = v

    row, mat = _sds((1, SSM_CH)), _sds((SSM_GROUP, SSM_CH))
    return pl.pallas_call(body, name="ssm_prep", out_shape=[row, row, mat, mat])(lr, li, ldt, br, bi)


def _ssm_prep_bwd(lr, li, ldt, br, bi, dar8, dai8, dbbr, dbbi):
    def body(lr_ref, li_ref, ldt_ref, br_ref, bi_ref, dar_ref, dai_ref, dbbr_ref, dbbi_ref, *outs):
        args = (lr_ref[...], li_ref[...], ldt_ref[...], br_ref[...], bi_ref[...])
        _, vjp = jax.vjp(_ssm_discretize, *args)
        cot = (jnp.sum(dar_ref[...], axis=0, keepdims=True), jnp.sum(dai_ref[...], axis=0, keepdims=True),
               dbbr_ref[...], dbbi_ref[...])
        for o, v in zip(outs, vjp(cot)):
            o[...] = v

    row, mat = _sds((1, SSM_CH)), _sds((SSM_GROUP, SSM_CH))
    return pl.pallas_call(body, name="ssm_prep_bwd", out_shape=[row, row, row, mat, mat])(
        lr, li, ldt, br, bi, dar8, dai8, dbbr, dbbi)


def _ssm_diag(name, full):
    tiles = SCAN_CW // LANES
    groups = LANES // SSM_STATE
    rows = tiles * groups * SSM_GROUP

    def body(x_ref, o_ref):
        lane = lax.broadcasted_iota(jnp.int32, (SSM_GROUP, LANES), 1)
        for q in range(2 * tiles):
            sl = slice(q * LANES, (q + 1) * LANES)
            r0 = (q % tiles) * groups * SSM_GROUP
            out = x_ref[r0:r0 + SSM_GROUP, sl]
            for k in range(1, groups):
                out = jnp.where(lane >= k * SSM_STATE, x_ref[r0 + k * SSM_GROUP:r0 + (k + 1) * SSM_GROUP, sl], out)
            o_ref[:, sl] = out

    return pl.pallas_call(
        body, name=name, grid=(SCAN_NB,),
        in_specs=[pl.BlockSpec((rows, 2 * SCAN_CW), lambda b: (b, b))],
        out_specs=pl.BlockSpec((SSM_GROUP, 2 * SCAN_CW), lambda b: (0, b)),
        out_shape=_sds((SSM_GROUP, 2 * SSM_CH)), compiler_params=_cparams(("parallel",)))(full)


def _scan_layout(re, im):
    r = re.shape[0]
    return jnp.stack([re.reshape(r, SCAN_NB, SCAN_CW), im.reshape(r, SCAN_NB, SCAN_CW)], axis=2).reshape(r, 2 * SSM_CH)


def _scan_unlayout(x):
    r = x.shape[0]
    x = x.reshape(r, SCAN_NB, 2, SCAN_CW)
    return x[:, :, 0].reshape(r, SSM_CH), x[:, :, 1].reshape(r, SSM_CH)


def _time_permute(u):
    s, c = u.shape
    return u.reshape(SUBLANES, s // SUBLANES, c).transpose(1, 0, 2).reshape(s, c)


def _time_unpermute(u):
    s, c = u.shape
    return u.reshape(s // SUBLANES, SUBLANES, c).transpose(1, 0, 2).reshape(s, c)


def _ssm_scan(name, a_vec, x, reverse, s_prev=None):
    nsteps = SEQ // SUBLANES
    cw = SCAN_CW
    with_da = s_prev is not None

    def body(a_ref, x_ref, *rest):
        if with_da:
            s_ref, o_ref, da_ref = rest
        else:
            o_ref, = rest
        ar = jnp.broadcast_to(a_ref[:, :cw], (SUBLANES, cw))
        ai = jnp.broadcast_to(a_ref[:, cw:], (SUBLANES, cw))
        seg = lax.broadcasted_iota(jnp.int32, (SUBLANES, cw), 0)

        def toward(v):
            if reverse:
                return jnp.where(seg < SUBLANES - 1, pltpu.roll(v, SUBLANES - 1, 0), 0.0)
            return jnp.where(seg >= 1, pltpu.roll(v, 1, 0), 0.0)

        def rows(j):
            jj = nsteps - 1 - j if reverse else j
            return pl.ds(pl.multiple_of(jj * SUBLANES, SUBLANES), SUBLANES)

        def cmul(pr, pi, qr, qi):
            return pr * qr - pi * qi, pr * qi + pi * qr

        def local(j, c):
            sr, si = c
            r = rows(j)
            mr, mi = cmul(ar, ai, sr, si)
            return mr + x_ref[r, :cw], mi + x_ref[r, cw:]

        zero = jnp.zeros((SUBLANES, cw), F32)
        fr, fi = lax.fori_loop(0, nsteps, local, (zero, zero), unroll=SCAN_UNROLL)

        def power(_, c):
            return cmul(ar, ai, *c)

        pr, pi = lax.fori_loop(0, nsteps - 1, power, (ar, ai))
        tr, ti = fr, fi
        for _ in range(SUBLANES - 1):
            mr, mi = cmul(pr, pi, toward(tr), toward(ti))
            tr, ti = fr + mr, fi + mi
        init = (toward(tr), toward(ti))

        def advance(j, sr, si):
            r = rows(j)
            mr, mi = cmul(ar, ai, sr, si)
            sr, si = mr + x_ref[r, :cw], mi + x_ref[r, cw:]
            o_ref[r, :cw] = sr
            o_ref[r, cw:] = si
            return sr, si

        def full(j, c):
            return advance(j, *c)

        def full_da(j, c):
            sr, si = advance(j, c[0], c[1])
            rp = pl.ds(pl.multiple_of((nsteps - 2 - j) * SUBLANES, SUBLANES), SUBLANES)
            spr, spi = s_ref[rp, :cw], s_ref[rp, cw:]
            return sr, si, c[2] + sr * spr + si * spi, c[3] + si * spr - sr * spi

        if with_da:
            sr, si, dar, dai = lax.fori_loop(0, nsteps - 1, full_da, init + (zero, zero), unroll=SCAN_UNROLL)
            sr, si = advance(nsteps - 1, sr, si)
            last = pl.ds((nsteps - 1) * SUBLANES, SUBLANES)
            spr = jnp.where(seg >= 1, pltpu.roll(s_ref[last, :cw], 1, 0), 0.0)
            spi = jnp.where(seg >= 1, pltpu.roll(s_ref[last, cw:], 1, 0), 0.0)
            da_ref[:, :cw] = dar + sr * spr + si * spi
            da_ref[:, cw:] = dai + si * spr - sr * spi
        else:
            lax.fori_loop(0, nsteps, full, init, unroll=SCAN_UNROLL)

    blk = pl.BlockSpec((SEQ, 2 * cw), lambda b: (0, b))
    a_spec = pl.BlockSpec((1, 2 * cw), lambda b: (0, b))
    in_specs, args = [a_spec, blk], [a_vec, x]
    out_specs, out_shape = blk, _sds((SEQ, 2 * SSM_CH))
    if with_da:
        in_specs, args = in_specs + [blk], args + [s_prev]
        out_specs = [blk, pl.BlockSpec((SUBLANES, 2 * cw), lambda b: (0, b))]
        out_shape = [out_shape, _sds((SUBLANES, 2 * SSM_CH))]
    return pl.pallas_call(body, name=name, grid=(SCAN_NB,), in_specs=in_specs, out_specs=out_specs,
                          out_shape=out_shape, compiler_params=_cparams(("parallel",)))(*args)


def _ssm_gelu(yp, up, dvec):
    tm = _tile(SEQ, 512)

    def body(y_ref, u_ref, d_ref, yf_ref, g_ref):
        yf = y_ref[...] + d_ref[...] * u_ref[...]
        yf_ref[...] = yf
        g_ref[...] = jax.nn.gelu(yf).astype(g_ref.dtype)

    blk = pl.BlockSpec((tm, SSM_WIDTH), lambda i: (i, 0))
    row = pl.BlockSpec((1, SSM_WIDTH), lambda i: (0, 0))
    return pl.pallas_call(body, name="ssm_gelu", grid=(SEQ // tm,), in_specs=[blk, blk, row], out_specs=[blk, blk],
                          out_shape=[_sds((SEQ, SSM_WIDTH)), _sds((SEQ, SSM_WIDTH), MM_DTYPE)],
                          compiler_params=_cparams(("parallel",)))(yp, up, dvec)


def _ssm_gelu_bwd(yf, dgy, up, dvec):
    tm = _tile(SEQ, 512)

    def body(yf_ref, dg_ref, u_ref, d_ref, dyf_ref, du_ref, dd_ref):
        _, vjp = jax.vjp(jax.nn.gelu, yf_ref[...])
        dyf, = vjp(dg_ref[...])
        dyf_ref[...] = dyf.astype(dyf_ref.dtype)
        du_ref[...] = d_ref[...] * dyf

        @pl.when(pl.program_id(0) == 0)
        def _():
            dd_ref[...] = jnp.zeros_like(dd_ref)

        dd_ref[...] += jnp.sum(dyf * u_ref[...], axis=0, keepdims=True)

    blk = pl.BlockSpec((tm, SSM_WIDTH), lambda i: (i, 0))
    row = pl.BlockSpec((1, SSM_WIDTH), lambda i: (0, 0))
    return pl.pallas_call(body, name="ssm_gelu_bwd", grid=(SEQ // tm,), in_specs=[blk, blk, blk, row],
                          out_specs=[blk, blk, row],
                          out_shape=[_sds((SEQ, SSM_WIDTH), MM_DTYPE), _sds((SEQ, SSM_WIDTH)), _sds((1, SSM_WIDTH))],
                          compiler_params=_cparams(("arbitrary",)))(yf, dgy, up, dvec)


def _glu(ab):
    return ab[:, :SSM_WIDTH] * jax.nn.sigmoid(ab[:, SSM_WIDTH:])


def _ssm_glu(ab):
    tm = _tile(SEQ, 512)

    def body(ab_ref, o_ref):
        o_ref[...] = _glu(ab_ref[...]).astype(o_ref.dtype)

    return pl.pallas_call(body, name="ssm_glu", grid=(SEQ // tm,),
                          in_specs=[pl.BlockSpec((tm, 2 * SSM_WIDTH), lambda i: (i, 0))],
                          out_specs=pl.BlockSpec((tm, SSM_WIDTH), lambda i: (i, 0)),
                          out_shape=_sds((SEQ, SSM_WIDTH), MM_DTYPE), compiler_params=_cparams(("parallel",)))(ab)


def _ssm_glu_bwd(ab, dout):
    tm = _tile(SEQ, 512)

    def body(ab_ref, do_ref, dab_ref):
        _, vjp = jax.vjp(_glu, ab_ref[...])
        dab, = vjp(do_ref[...])
        dab_ref[...] = dab.astype(dab_ref.dtype)

    return pl.pallas_call(body, name="ssm_glu_bwd", grid=(SEQ // tm,),
                          in_specs=[pl.BlockSpec((tm, 2 * SSM_WIDTH), lambda i: (i, 0)),
                                    pl.BlockSpec((tm, SSM_WIDTH), lambda i: (i, 0))],
                          out_specs=pl.BlockSpec((tm, 2 * SSM_WIDTH), lambda i: (i, 0)),
                          out_shape=_sds((SEQ, 2 * SSM_WIDTH), MM_DTYPE), compiler_params=_cparams(("parallel",)))(ab, dout)


def _add2(name, a, b, out_dtype):
    tm = _tile(a.shape[0], 512)

    def body(a_ref, b_ref, o_ref):
        o_ref[...] = (a_ref[...] + b_ref[...]).astype(o_ref.dtype)

    blk = pl.BlockSpec((tm, a.shape[1]), lambda i: (i, 0))
    return pl.pallas_call(body, name=name, grid=(a.shape[0] // tm,), in_specs=[blk, blk], out_specs=blk,
                          out_shape=_sds(a.shape, out_dtype), compiler_params=_cparams(("parallel",)))(a, b)


def _layer_norm(r, g, b):
    mu = r.mean(-1, keepdims=True)
    var = jnp.square(r - mu).mean(-1, keepdims=True)
    return (r - mu) * lax.rsqrt(var + LN_EPS) * g + b


def _ln_fwd(name, x, y, g, b):
    tm = _tile(SEQ, 256)

    def body(x_ref, y_ref, g_ref, b_ref, r_ref, o_ref, ob_ref):
        r = DEEPNORM_ALPHA * x_ref[...] + y_ref[...]
        r_ref[...] = r
        o = _layer_norm(r, g_ref[...], b_ref[...])
        o_ref[...] = o
        ob_ref[...] = o.astype(ob_ref.dtype)

    blk = pl.BlockSpec((tm, D_MODEL), lambda i: (i, 0))
    row = pl.BlockSpec((1, D_MODEL), lambda i: (0, 0))
    return pl.pallas_call(body, name=name, grid=(SEQ // tm,), in_specs=[blk, blk, row, row], out_specs=[blk, blk, blk],
                          out_shape=[_sds((SEQ, D_MODEL))] * 2 + [_sds((SEQ, D_MODEL), MM_DTYPE)],
                          compiler_params=_cparams(("parallel",)))(x, y, g, b)


def _ln_bwd(name, r, g, b, da, db=None, dep=None):
    tm = _tile(SEQ, 256)
    two = db is not None
    deps = _as_list(dep)

    def body(r_ref, g_ref, b_ref, da_ref, *rest):
        dr_ref, drb_ref, dg_ref, dbeta_ref = rest[-4:]
        dout = DEEPNORM_ALPHA * da_ref[...] + rest[0][...] if two else da_ref[...]
        _, vjp = jax.vjp(_layer_norm, r_ref[...], g_ref[...], b_ref[...])
        dr, dg, dbeta = vjp(dout)
        dr_ref[...] = dr
        drb_ref[...] = dr.astype(drb_ref.dtype)

        @pl.when(pl.program_id(0) == 0)
        def _():
            dg_ref[...] = jnp.zeros_like(dg_ref)
            dbeta_ref[...] = jnp.zeros_like(dbeta_ref)

        dg_ref[...] += dg
        dbeta_ref[...] += dbeta

    blk = pl.BlockSpec((tm, D_MODEL), lambda i: (i, 0))
    row = pl.BlockSpec((1, D_MODEL), lambda i: (0, 0))
    args = [r, g, b, da] + ([db] if two else []) + deps
    return pl.pallas_call(body, name=name, grid=(SEQ // tm,),
                          in_specs=[blk, row, row, blk] + ([blk] if two else []) + [ANY] * len(deps),
                          out_specs=[blk, blk, row, row],
                          out_shape=[_sds((SEQ, D_MODEL)), _sds((SEQ, D_MODEL), MM_DTYPE), _sds((1, D_MODEL)), _sds((1, D_MODEL))],
                          compiler_params=_cparams(("arbitrary",)))(*args)


FFN_TM = 128
HALO = SUBLANES


def _conv_taps(cur, halo):
    row = lax.broadcasted_iota(jnp.int32, cur.shape, 0)
    h1 = jnp.where(row == 0, halo[HALO - 1:HALO, :], pltpu.roll(cur, 1, 0))
    h2 = jnp.where(row == 0, halo[HALO - 2:HALO - 1, :], jnp.where(row == 1, halo[HALO - 1:HALO, :], pltpu.roll(cur, 2, 0)))
    return h1, h2


def _conv_fwd(cur, halo, w_ref, b_ref):
    h1, h2 = _conv_taps(cur, halo)
    return b_ref[...] + h2 * w_ref[0:1, :] + h1 * w_ref[1:2, :] + cur * w_ref[2:3, :], h1, h2


def _gate(val, gate):
    return jax.nn.silu(gate) * val


def _ffn_specs(tm):
    nb = tm // HALO
    cur = lambda off: pl.BlockSpec((None, tm, FS), lambda p, i: (p + off, i, 0))
    halo = lambda off: pl.BlockSpec((None, HALO, FS), lambda p, i: (p + off, jnp.maximum(i * nb - 1, 0), 0))
    cw = lambda off: pl.BlockSpec((None, CONV_WIDTH, FS), lambda p, i: (p + off, 0, 0))
    cb = lambda off: pl.BlockSpec((None, 1, FS), lambda p, i: (p + off, 0, 0))
    return cur, halo, cw, cb


def _ffn_act(hf, conv_w, conv_b):
    tm = _tile(SEQ, FFN_TM, SUBLANES)
    cur, halo, cw, cb = _ffn_specs(tm)

    def body(v_ref, vh_ref, g_ref, gh_ref, wv_ref, wg_ref, bv_ref, bg_ref, o_ref):
        live = pl.program_id(1) > 0
        vh = jnp.where(live, vh_ref[...], 0.0)
        gh = jnp.where(live, gh_ref[...], 0.0)
        val, _, _ = _conv_fwd(v_ref[...], vh, wv_ref, bv_ref)
        gate, _, _ = _conv_fwd(g_ref[...], gh, wg_ref, bg_ref)
        o_ref[...] = _gate(val, gate).astype(o_ref.dtype)

    return pl.pallas_call(
        body, name="ffn_act", grid=(2, SEQ // tm),
        in_specs=[cur(0), halo(0), cur(2), halo(2), cw(0), cw(2), cb(0), cb(2)],
        out_specs=pl.BlockSpec((None, tm, FS), lambda p, i: (p, i, 0)),
        out_shape=_sds((2, SEQ, FS), MM_DTYPE), compiler_params=_cparams(("parallel", "parallel")))(
            hf, hf, hf, hf, conv_w, conv_w, conv_b, conv_b)


def _ffn_act_bwd(hf, conv_w, conv_b, dact, dep=None):
    tm = _tile(SEQ, FFN_TM, SUBLANES)
    nb, nblk = tm // HALO, SEQ // tm
    cur, halo, cw, cb = _ffn_specs(tm)
    nxt = lambda off: pl.BlockSpec((None, HALO, FS), lambda p, i: (p + off, jnp.minimum((i + 1) * nb, SEQ // HALO - 1), 0))
    deps = _as_list(dep)

    def body(v_ref, vh_ref, vn_ref, g_ref, gh_ref, gn_ref, wv_ref, wg_ref, bv_ref, bg_ref, da_ref, dan_ref, *rest):
        dh_ref, dw_ref, dbias_ref = rest[-3:]
        dwv_ref, dwg_ref = dw_ref.at[0], dw_ref.at[1]
        dbv_ref, dbg_ref = dbias_ref.at[0], dbias_ref.at[1]
        i = pl.program_id(1)
        live, more = i > 0, i < nblk - 1
        vh = jnp.where(live, vh_ref[...], 0.0)
        gh = jnp.where(live, gh_ref[...], 0.0)
        vcur, gcur = v_ref[...], g_ref[...]
        val, v1, v2 = _conv_fwd(vcur, vh, wv_ref, bv_ref)
        gate, g1, g2 = _conv_fwd(gcur, gh, wg_ref, bg_ref)
        _, vjp = jax.vjp(_gate, val, gate)
        dval, dgate = vjp(da_ref[...])
        val_n, _, _ = _conv_fwd(vn_ref[...], vcur[tm - HALO:, :], wv_ref, bv_ref)
        gate_n, _, _ = _conv_fwd(gn_ref[...], gcur[tm - HALO:, :], wg_ref, bg_ref)
        _, vjp_n = jax.vjp(_gate, val_n, gate_n)
        dval_n, dgate_n = vjp_n(dan_ref[...])
        row = lax.broadcasted_iota(jnp.int32, dval.shape, 0)
        for k, (d, dn, w_ref) in enumerate(((dval, dval_n, wv_ref), (dgate, dgate_n, wg_ref))):
            dn = jnp.where(more, dn, 0.0)
            d1 = jnp.where(row == tm - 1, dn[0:1, :], pltpu.roll(d, tm - 1, 0))
            d2 = jnp.where(row == tm - 1, dn[1:2, :], jnp.where(row == tm - 2, dn[0:1, :], pltpu.roll(d, tm - 2, 0)))
            dh_ref[k] = (d * w_ref[2:3, :] + d1 * w_ref[1:2, :] + d2 * w_ref[0:1, :]).astype(dh_ref.dtype)

        @pl.when(i == 0)
        def _():
            dw_ref[...] = jnp.zeros_like(dw_ref)
            dbias_ref[...] = jnp.zeros_like(dbias_ref)

        for d, taps, dwk_ref, dbk_ref in ((dval, (v2, v1, vcur), dwv_ref, dbv_ref), (dgate, (g2, g1, gcur), dwg_ref, dbg_ref)):
            for k in range(CONV_WIDTH):
                dwk_ref[k:k + 1, :] += jnp.sum(d * taps[k], axis=0, keepdims=True)
            dbk_ref[...] += jnp.sum(d, axis=0, keepdims=True)

    return pl.pallas_call(
        body, name="ffn_act_bwd", grid=(2, SEQ // tm),
        in_specs=[cur(0), halo(0), nxt(0), cur(2), halo(2), nxt(2), cw(0), cw(2), cb(0), cb(2),
                  pl.BlockSpec((None, tm, FS), lambda p, i: (p, i, 0)), nxt(0)] + [ANY] * len(deps),
        out_specs=[pl.BlockSpec((None, 2, tm, FS), lambda p, i: (p, 0, i, 0)),
                   pl.BlockSpec((None, 2, CONV_WIDTH, FS), lambda p, i: (p, 0, 0, 0)),
                   pl.BlockSpec((None, 2, 1, FS), lambda p, i: (p, 0, 0, 0))],
        out_shape=[_sds((2, 2, SEQ, FS), MM_DTYPE), _sds((2, 2, CONV_WIDTH, FS)), _sds((2, 2, 1, FS))],
        compiler_params=_cparams(("parallel", "arbitrary")))(hf, hf, hf, hf, hf, hf, conv_w, conv_w, conv_b, conv_b, dact,
                                                              dact, *deps)


def _loss(y, target):
    tm = _tile(SEQ, 256)

    def body(y_ref, t_ref, dy_ref, l_ref):
        err = y_ref[...] - t_ref[...]
        dy_ref[...] = err * (1.0 / D_MODEL)

        @pl.when(pl.program_id(0) == 0)
        def _():
            l_ref[...] = jnp.zeros_like(l_ref)

        l_ref[...] += 0.5 * jnp.sum(jnp.mean(jnp.square(err), axis=-1))

    blk = pl.BlockSpec((tm, D_MODEL), lambda i: (i, 0))
    return pl.pallas_call(body, name="loss", grid=(SEQ // tm,), in_specs=[blk, blk],
                          out_specs=[blk, pl.BlockSpec((SUBLANES, LANES), lambda i: (0, 0))],
                          out_shape=[_sds((SEQ, D_MODEL)), _sds((SUBLANES, LANES))],
                          compiler_params=_cparams(("arbitrary",)))(y, target)


ADAM_BLOCK_BYTES = 3 << 19
ELEMENTWISE_COLS = 1024


def _adamw_math(w, g, m, v):
    nm = ADAM_B1 * m + (1.0 - ADAM_B1) * g
    nv = ADAM_B2 * v + (1.0 - ADAM_B2) * jnp.square(g)
    m_hat = nm / (1.0 - ADAM_B1 ** ADAM_STEP)
    v_hat = nv / (1.0 - ADAM_B2 ** ADAM_STEP)
    return -ADAM_LR * (m_hat / (jnp.sqrt(v_hat) + ADAM_EPS) + ADAM_WD * w), nm, nv


def _adamw(name, w, g, m, v):
    r, c = w.shape
    tr = _tile(r, max(SUBLANES, ADAM_BLOCK_BYTES // (4 * c)), SUBLANES)

    def body(w_ref, g_ref, m_ref, v_ref, d_ref, nm_ref, nv_ref):
        d_ref[...], nm_ref[...], nv_ref[...] = _adamw_math(w_ref[...], g_ref[...], m_ref[...], v_ref[...])

    blk = pl.BlockSpec((tr, c), lambda i: (i, 0))
    return pl.pallas_call(body, name=name, grid=(r // tr,), in_specs=[blk] * 4, out_specs=[blk] * 3,
                          out_shape=[_sds((r, c))] * 3, compiler_params=_cparams(("parallel",)))(w, g, m, v)


def _adamw_big(name, l, c_idx, w, m, v, g_own, g_got, prev):
    depth, _, r, c = w.shape
    tc = _tile(c, ELEMENTWISE_COLS)
    tr = _tile(r, max(SUBLANES, ADAM_BLOCK_BYTES // (4 * tc)), SUBLANES)

    def body(c_ref, w_ref, m_ref, v_ref, own_ref, got_ref, *rest):
        g_ref, d_ref, nm_ref, nv_ref = rest[-4:]
        g = jnp.where(pl.program_id(0) == c_ref[0], own_ref[...], got_ref[...])
        g_ref[...] = g
        d_ref[...], nm_ref[...], nv_ref[...] = _adamw_math(w_ref[...], g, m_ref[...], v_ref[...])

    stacked = pl.BlockSpec((None, None, tr, tc), lambda h, i, j, cr: (l, h, i, j))
    own = pl.BlockSpec((tr, tc), lambda h, i, j, cr: (jnp.where(h == cr[0], i, 0), jnp.where(h == cr[0], j, 0)))
    got = pl.BlockSpec((tr, tc), lambda h, i, j, cr: (jnp.where(h == cr[0], 0, i), jnp.where(h == cr[0], 0, j)))
    grid_spec = pltpu.PrefetchScalarGridSpec(
        num_scalar_prefetch=1, grid=(2, r // tr, c // tc),
        in_specs=[stacked] * 3 + [own, got] + ([ANY] * 4 if prev else []), out_specs=[stacked] * 4)
    return pl.pallas_call(
        body, name=name, grid_spec=grid_spec, out_shape=[_sds((depth, 2, r, c))] * 4,
        input_output_aliases={6 + k: k for k in range(4)} if prev else {},
        compiler_params=_cparams(("arbitrary", "arbitrary", "arbitrary")))(c_idx, w, m, v, g_own, g_got, *(prev or ()))


ANY = pl.BlockSpec(memory_space=pl.ANY)


def _place():
    x, y, c = lax.axis_index("x"), lax.axis_index("y"), lax.axis_index("c")
    chips = [(1 - x, y), (x, 1 - y), (1 - x, 1 - y)]
    return x, y, c, chips


def _cast_place(name, w, l, me_idx, out_dtype):
    _, _, r, c = w.shape
    tr = _tile(r, max(2 * SUBLANES, COPY_BLOCK_BYTES // (4 * c)), 2 * SUBLANES)

    def body(me_ref, w_ref, o_ref):
        o_ref[...] = w_ref[...].astype(o_ref.dtype)

    grid_spec = pltpu.PrefetchScalarGridSpec(
        num_scalar_prefetch=1, grid=(2, r // tr),
        in_specs=[pl.BlockSpec((None, None, tr, c), lambda h, i, me: (l, h, i, 0))],
        out_specs=pl.BlockSpec((None, None, tr, c), lambda h, i, me: (me[0], h, i, 0)))
    return pl.pallas_call(body, name=name, grid_spec=grid_spec, out_shape=_sds((N_CHIPS, 2, r, c), out_dtype),
                          compiler_params=_cparams(("parallel", "parallel")))(me_idx, w)


HBM = pl.BlockSpec(memory_space=pltpu.HBM)
SEM = pl.BlockSpec(memory_space=pltpu.SEMAPHORE)
TOKEN = (SUBLANES, LANES)


def _comm_call(name, body, hbm, sems_in=(), after=None, sems_out=(), token=False):
    n, k = len(hbm), len(sems_out)
    ins = [pltpu.with_memory_space_constraint(a, pltpu.HBM) for a in hbm] + list(sems_in)
    in_specs = [HBM] * n + [SEM] * len(sems_in)
    if after is not None:
        ins.append(after)
        in_specs.append(ANY)
    out_shape = [pltpu.SemaphoreType.DMA((s,)) for s in sems_out] + [pltpu.HBM(a.shape, a.dtype) for a in hbm]
    out_specs = [SEM] * k + [HBM] * n
    if token:
        out_shape.append(_sds(TOKEN))
        out_specs.append(pl.BlockSpec(memory_space=pltpu.VMEM))
    res = pl.pallas_call(
        body, name=name, in_specs=in_specs, out_specs=out_specs, out_shape=out_shape,
        input_output_aliases={i: k + i for i in range(n)},
        compiler_params=pltpu.CompilerParams(has_side_effects=pltpu.SideEffectType.DATAFLOW_SIDE_EFFECTING))(*ins)
    return list(res[:k]), list(res[k:k + n]), (res[k + n] if token else None)


def _remote(src, dst, send, recv, to):
    return pltpu.make_async_remote_copy(src_ref=src, dst_ref=dst, send_sem=send, recv_sem=recv, device_id=to,
                                        device_id_type=MESH)


def _gather_start(name, bufs, after=None, pair=False):
    n = len(bufs)
    o = n + (after is not None)

    def body(*refs):
        ins, (send, recv), token = refs[:n], refs[o:o + 2], refs[-1]
        x, y, c, chips = _place()
        for i in range(n):
            mine = ins[i].at[2 * x + y, c]
            for k, chip in enumerate(chips):
                _remote(mine, mine, send.at[3 * i + k], recv.at[3 * i + k], (*chip, c)).start()
            if pair:
                _remote(mine, mine, send.at[3 * n + i], recv.at[3 * n + i], (x, y, 1 - c)).start()
        token[...] = jnp.zeros(TOKEN, F32)

    n_sems = (3 + pair) * n
    return _comm_call(name, body, bufs, after=after, sems_out=(n_sems, n_sems), token=True)


def _gather_forward(name, bufs, sems, after, pair=False):
    n = len(bufs)
    o = n + 2 + (after is not None)

    def body(*refs):
        ins, (send, recv), (send2, recv2), token = refs[:n], refs[n:n + 2], refs[o:o + 2], refs[-1]
        x, y, c, chips = _place()
        for i in range(n):
            mine = ins[i].at[2 * x + y, c]
            for k, chip in enumerate(chips):
                land = ins[i].at[2 * chip[0] + chip[1], c]
                first = _remote(mine, land, send.at[3 * i + k], recv.at[3 * i + k], (*chip, c))
                first.wait_send()
                first.wait_recv()
                _remote(land, land, send2.at[3 * i + k], recv2.at[3 * i + k], (x, y, 1 - c)).start()
            if pair:
                own = _remote(mine, ins[i].at[2 * x + y, 1 - c], send.at[3 * n + i], recv.at[3 * n + i], (x, y, 1 - c))
                own.wait_send()
                own.wait_recv()
        token[...] = jnp.zeros(TOKEN, F32)

    return _comm_call(name, body, bufs, sems_in=sems, after=after, sems_out=(3 * n, 3 * n), token=True)


def _gather_finish(name, bufs, sems, after):
    n = len(bufs)

    def body(*refs):
        ins, (send, recv) = refs[:n], refs[n:n + 2]
        x, y, c, chips = _place()
        for i in range(n):
            for k, chip in enumerate(chips):
                idx = 2 * chip[0] + chip[1]
                cp = _remote(ins[i].at[idx, c], ins[i].at[idx, 1 - c], send.at[3 * i + k], recv.at[3 * i + k], (x, y, 1 - c))
                cp.wait_send()
                cp.wait_recv()

    return _comm_call(name, body, bufs, sems_in=sems, after=after)[1]


def _swap_start(name, grads):
    n = len(grads)
    lands = [lax.empty((g.shape[0],) + g.shape[2:], g.dtype) for g in grads]

    def body(*refs):
        ins, lnd, (send, recv), token = refs[:n], refs[n:2 * n], refs[2 * n:2 * n + 2], refs[-1]
        x, y, c, _ = _place()
        for i in range(n):
            _remote(ins[i].at[:, 1 - c], lnd[i], send.at[i], recv.at[i], (x, y, 1 - c)).start()
        token[...] = jnp.zeros(TOKEN, F32)

    return _comm_call(name, body, list(grads) + lands, sems_out=(n, n), token=True)


def _swap_wait(name, hbm, sems, after):
    n = len(hbm) // 2

    def body(*refs):
        ins, lnd, (send, recv) = refs[:n], refs[n:2 * n], refs[2 * n:2 * n + 2]
        x, y, c, _ = _place()
        for i in range(n):
            cp = _remote(ins[i].at[:, 1 - c], lnd[i], send.at[i], recv.at[i], (x, y, 1 - c))
            cp.wait_send()
            cp.wait_recv()

    out = _comm_call(name, body, hbm, sems_in=sems, after=after)[1]
    return out[:n], out[n:]


def _pair_add(name, g, got, cm_idx):
    nk, _, r, c = g.shape
    tr = _tile(r, max(2 * SUBLANES, COPY_BLOCK_BYTES // (4 * c)), 2 * SUBLANES)

    def body(cm_ref, g_ref, x_ref, o_ref, land_ref):
        s = (g_ref[...] + x_ref[...]).astype(o_ref.dtype)
        o_ref[...] = s

        @pl.when(pl.program_id(1) == cm_ref[1])
        def _():
            land_ref[...] = s

    grid_spec = pltpu.PrefetchScalarGridSpec(
        num_scalar_prefetch=1, grid=(r // tr, nk),
        in_specs=[pl.BlockSpec((None, None, tr, c), lambda i, k, cm: (k, cm[0], i, 0)),
                  pl.BlockSpec((None, tr, c), lambda i, k, cm: (k, i, 0))],
        out_specs=[pl.BlockSpec((None, tr, c), lambda i, k, cm: (k, i, 0)),
                   pl.BlockSpec((None, tr, c), lambda i, k, cm: (cm[1], i, 0))])
    return pl.pallas_call(body, name=name, grid_spec=grid_spec, out_shape=[_sds((nk, r, c), BF16)] * 2,
                          compiler_params=_cparams(("parallel", "arbitrary")))(cm_idx, g, got)


def _scatter_start(name, parts, lands):
    n = len(parts)

    def body(*refs):
        ins, lnd, (send, recv), token = refs[:n], refs[n:2 * n], refs[2 * n:2 * n + 2], refs[-1]
        x, y, c, chips = _place()
        for i in range(n):
            for k, chip in enumerate(chips):
                _remote(ins[i].at[2 * chip[0] + chip[1]], lnd[i].at[2 * x + y], send.at[3 * i + k], recv.at[3 * i + k],
                        (*chip, c)).start()
        token[...] = jnp.zeros(TOKEN, F32)

    return _comm_call(name, body, list(parts) + list(lands), sems_out=(3 * n, 3 * n), token=True)


def _scatter_wait(name, hbm, sems, after):
    n = len(hbm) // 2

    def body(*refs):
        ins, lnd, (send, recv) = refs[:n], refs[n:2 * n], refs[2 * n:2 * n + 2]
        x, y, c, chips = _place()
        for i in range(n):
            for k, chip in enumerate(chips):
                idx = 2 * chip[0] + chip[1]
                cp = _remote(ins[i].at[idx], lnd[i].at[idx], send.at[3 * i + k], recv.at[3 * i + k], (*chip, c))
                cp.wait_send()
                cp.wait_recv()

    out = _comm_call(name, body, hbm, sems_in=sems, after=after)[1]
    return out[:n], out[n:]


def _sum_leading(name, x, out_dtype=F32):
    nk, r, c = x.shape
    tc = _tile(c, ELEMENTWISE_COLS)
    tr = _tile(r, max(2 * SUBLANES, COPY_BLOCK_BYTES // (nk * tc * x.dtype.itemsize)), 2 * SUBLANES)

    def body(x_ref, o_ref):
        acc = x_ref[0].astype(F32)
        for k in range(1, nk):
            acc = acc + x_ref[k].astype(F32)
        o_ref[...] = acc.astype(o_ref.dtype)

    return pl.pallas_call(body, name=name, grid=(r // tr, c // tc),
                          in_specs=[pl.BlockSpec((nk, tr, tc), lambda i, j: (0, i, j))],
                          out_specs=pl.BlockSpec((tr, tc), lambda i, j: (i, j)), out_shape=_sds((r, c), out_dtype),
                          compiler_params=_cparams(("parallel", "parallel")))(x)


def _exchange_start(name, halves):
    n = len(halves)
    lands = [lax.empty(h.shape, h.dtype) for h in halves]

    def body(*refs):
        ins, lnd, (send, recv), token = refs[:n], refs[n:2 * n], refs[2 * n:2 * n + 2], refs[-1]
        x, y, c, _ = _place()
        for i in range(n):
            _remote(ins[i], lnd[i], send.at[i], recv.at[i], (x, y, 1 - c)).start()
        token[...] = jnp.zeros(TOKEN, F32)

    return _comm_call(name, body, list(halves) + lands, sems_out=(n, n), token=True)


def _exchange_wait(name, hbm, sems, after):
    n = len(hbm) // 2

    def body(*refs):
        ins, lnd, (send, recv) = refs[:n], refs[n:2 * n], refs[2 * n:2 * n + 2]
        x, y, c, _ = _place()
        for i in range(n):
            cp = _remote(ins[i], lnd[i], send.at[i], recv.at[i], (x, y, 1 - c))
            cp.wait_send()
            cp.wait_recv()

    out = _comm_call(name, body, hbm, sems_in=sems, after=after)[1]
    return out[:n], out[n:]


SMALL = ("attn_sinks", "pool_w", "pool_scale", "ssm_lam_re", "ssm_lam_im", "ssm_log_dt", "ssm_b_re", "ssm_b_im",
         "ssm_c_re", "ssm_c_im", "ssm_d", "ln1_g", "ln1_b", "ffn_conv_b", "ln2_g", "ln2_b")
BIG = ("w_in", "ssm_glu_w", "w_out", "ffn_w_up", "ffn_conv_w", "ffn_w_down")
ALL_W = ("w_in", "attn_sinks", "pool_w", "pool_scale", "ssm_lam_re", "ssm_lam_im", "ssm_log_dt", "ssm_b_re", "ssm_b_im",
         "ssm_c_re", "ssm_c_im", "ssm_d", "ssm_glu_w", "w_out", "ln1_g", "ln1_b", "ffn_w_up", "ffn_conv_w", "ffn_conv_b",
         "ffn_w_down", "ln2_g", "ln2_b")
PACK_UNIT = SUBLANES * LANES


def _padded(n):
    return -(-n // PACK_UNIT) * PACK_UNIT


def _pack(arrs):
    cols = []
    for name in SMALL:
        a = arrs[name].reshape(DEPTH, -1)
        cols.append(jnp.pad(a, ((0, 0), (0, _padded(a.shape[1]) - a.shape[1]))))
    return jnp.concatenate(cols, axis=1).reshape(-1, LANES)


def _unpack(packed, shapes):
    flat = packed.reshape(DEPTH, -1)
    out, off = {}, 0
    for name in SMALL:
        n = math.prod(shapes[name][1:])
        out[name] = flat[:, off:off + n].reshape(shapes[name])
        off += _padded(n)
    return out


def _b_rows(b):
    return b.transpose(2, 0, 1).reshape(SSM_GROUP, SSM_CH)


def _b_unrows(b):
    return b.reshape(SSM_GROUP, SSM_N_GROUPS, SSM_STATE).transpose(1, 2, 0)


def _block_diag_in(bb):
    eye = jnp.eye(SSM_N_GROUPS, dtype=F32)
    b3 = bb.reshape(SSM_GROUP, SSM_N_GROUPS, SSM_STATE)
    return jnp.einsum("hgp,gk->ghkp", b3, eye).reshape(SSM_WIDTH, SSM_CH)


def _c_unrows(c):
    return c.reshape(SSM_GROUP, SSM_N_GROUPS, SSM_STATE).transpose(1, 0, 2)


def _block_diag_out(cc):
    eye = jnp.eye(SSM_N_GROUPS, dtype=F32)
    return jnp.einsum("ghp,gk->gpkh", cc, eye).reshape(SSM_CH, SSM_WIDTH)


def _rows_layout(re, im):
    n = re.shape[1]
    return jnp.stack([re.reshape(SCAN_NB, SCAN_CW, n), im.reshape(SCAN_NB, SCAN_CW, n)], axis=1).reshape(2 * SSM_CH, n)


H_POOL0 = ATTN_WIDTH + 2 * KV_WIDTH
H_SSM0 = H_POOL0 + POOL_WIDTH


def _ssm_params(p):
    lr = p["ssm_lam_re"].reshape(1, SSM_CH)
    li = p["ssm_lam_im"].reshape(1, SSM_CH)
    ldt = jnp.repeat(p["ssm_log_dt"], SSM_STATE).reshape(1, SSM_CH)
    return lr, li, ldt, _b_rows(p["ssm_b_re"]), _b_rows(p["ssm_b_im"])


def _layer_fwd(x, xb, p, wg, rope_t, dep, pre, mid):
    cos_t, sin_t = rope_t
    h = _mm_shard_cols("in_proj", xb, wg["w_in"], dep=dep)
    qk = _rope("rope_fwd", h, 0, Q_TILES + KV_TILES, cos_t, sin_t, MM_DTYPE)
    y_attn, y_attn_b = _attn_fwd(qk, h, p["attn_sinks"])
    y_pool = _pool_fwd(h, p["pool_w"], p["pool_scale"].reshape(1, POOL_WIDTH))
    ssm_in = _ssm_params(p)
    ar, ai, bbr, bbi = _ssm_prep(*ssm_in)
    bd = _scan_layout(_block_diag_in(bbr), _block_diag_in(bbi)).astype(MM_DTYPE)
    cc = _rows_layout(_block_diag_out(p["ssm_c_re"]), -_block_diag_out(p["ssm_c_im"])).astype(MM_DTYPE)
    dvec = p["ssm_d"].reshape(1, SSM_WIDTH)
    up = _time_permute(h[:, H_SSM0:])
    xx = _mm_nn("ssm_bu", up, bd, tn=1024)
    ss = _ssm_scan("ssm_scan_fwd", _scan_layout(ar, ai), xx, False)
    yp = _mm_nn("ssm_cs", ss, cc, tk=1024)
    yf, gy = _ssm_gelu(yp, up, dvec)
    ab = _mm_shard_cols("ssm_glu_proj", gy, wg["ssm_glu_w"])
    y_ssm = _time_unpermute(_ssm_glu(ab))
    mix = jnp.concatenate([y_attn_b, y_pool, y_ssm], axis=1)
    mixo = _mm_nn("out_proj", mix, wg["w_out"].reshape(MIX_WIDTH, D_MODEL), dep=pre(mix))
    r1, x1, x1b = _ln_fwd("ln1_fwd", x, mixo, p["ln1_g"].reshape(1, D_MODEL), p["ln1_b"].reshape(1, D_MODEL))
    tokens = mid(x1b)
    hf = _ffn_up(x1b, wg["ffn_w_up"], dep=tokens)
    conv_b = p["ffn_conv_b"].reshape(N_CHIPS, 1, FS)
    act = _ffn_act(hf, wg["ffn_conv_w"], conv_b)
    f = _ffn_down(act, wg["ffn_w_down"].reshape(2, FS, D_MODEL))
    r2, x2, x2b = _ln_fwd("ln2_fwd", x1, f, p["ln2_g"].reshape(1, D_MODEL), p["ln2_b"].reshape(1, D_MODEL))
    saved = dict(xb=xb, h=h, qk=qk, y_attn=y_attn, ssm_in=ssm_in, ar=ar, ai=ai, bd=bd, cc=cc, dvec=dvec, up=up, ss=ss, yf=yf,
                 gy=gy, ab=ab, mix=mix, r1=r1, x1b=x1b, hf=hf, conv_b=conv_b, act=act, r2=r2)
    return x2, x2b, saved


def _layer_bwd(da, db, p, wg, sv, rope_t, run, start):
    cos_t, sin_t = rope_t
    small = {}
    dr2, dr2b, dg, dbeta = _ln_bwd("ln2_bwd" if db is not None else "ln2_bwd_last", sv["r2"], p["ln2_g"].reshape(1, D_MODEL),
                                   p["ln2_b"].reshape(1, D_MODEL), da, db, dep=run("h0", None))
    small["ln2_g"], small["ln2_b"] = dg, dbeta
    w_down = wg["ffn_w_down"].reshape(2, FS, D_MODEL)
    dact = _ffn_down_dact(dr2b, w_down)
    dw_down = _ffn_down_dw(sv["act"], dr2b)
    dh_ffn, dcw, dcb = _ffn_act_bwd(sv["hf"], wg["ffn_conv_w"], sv["conv_b"], dact, dep=run("h1", dw_down))
    dconv_w = dcw.transpose(1, 0, 2, 3).reshape(N_CHIPS, CONV_WIDTH, FS)
    small["ffn_conv_b"] = dcb.transpose(1, 0, 2, 3)
    dw_up = _ffn_up_dw(sv["x1b"], dh_ffn)
    tok = [start("ffn", {"ffn_w_up": dw_up, "ffn_conv_w": dconv_w,
                         "ffn_w_down": dw_down.reshape(N_CHIPS, FS // 2, D_MODEL)})] + run("h2", dw_up)
    dx1_ffn = _ffn_up_dx(dh_ffn, wg["ffn_w_up"], dep=tok)
    dr1, dr1b, dg, dbeta = _ln_bwd("ln1_bwd", sv["r1"], p["ln1_g"].reshape(1, D_MODEL), p["ln1_b"].reshape(1, D_MODEL), dr2,
                                   dx1_ffn, dep=tok)
    small["ln1_g"], small["ln1_b"] = dg, dbeta
    w_out = wg["w_out"].reshape(MIX_WIDTH, D_MODEL)
    dw_out = _mm_tn("out_proj_dw", sv["mix"], dr1b)
    dmix = _mm_nt("out_proj_dx", dr1b, w_out, dep=run("h3", dw_out))
    dq, dkc, dkp, dvc, dvp, dsk = _attn_bwd(sv["qk"], sv["h"], p["attn_sinks"], sv["y_attn"], dmix, 0)
    small["attn_sinks"] = dsk[:, :, 0]
    dh_attn = _attn_dh(dq, dkc, dkp, dvc, dvp, cos_t, -sin_t)
    dh_pool, dpw, dps = _pool_bwd(sv["h"], p["pool_w"], p["pool_scale"].reshape(1, POOL_WIDTH), dmix, ATTN_WIDTH // POOL_WIDTH)
    small["pool_w"], small["pool_scale"] = dpw, dps
    dout_p = _time_permute(dmix[:, ATTN_WIDTH + POOL_WIDTH:])
    dab = _ssm_glu_bwd(sv["ab"], dout_p)
    dgy = _mm_shard_cols_nt("ssm_glu_dx", dab, wg["ssm_glu_w"])
    dw_glu = _mm_shard_cols_tn("ssm_glu_dw", sv["gy"], dab, N_CHIPS)
    dyf, du1, dd = _ssm_gelu_bwd(sv["yf"], dgy, sv["up"], sv["dvec"])
    small["ssm_d"] = dd
    dss = _mm_nt("ssm_cs_dx", dyf, sv["cc"], tn=1024)
    dcre, dcim = _scan_unlayout(_ssm_diag("ssm_c_diag", _mm_tn("ssm_cs_dw", dyf, sv["ss"], tn=1024)))
    small["ssm_c_re"], small["ssm_c_im"] = _c_unrows(dcre), -_c_unrows(dcim)
    gg, da8 = _ssm_scan("ssm_scan_bwd", _scan_layout(sv["ar"], -sv["ai"]), dss, True, sv["ss"])
    du2 = _mm_nt("ssm_bu_dx", gg, sv["bd"], tk=1024)
    dbbr, dbbi = _scan_unlayout(_ssm_diag("ssm_b_diag", _mm_tn("ssm_bu_dw", sv["up"], gg, tn=1024)))
    dar8, dai8 = _scan_unlayout(da8)
    dlr, dli, dldt, dbr, dbi = _ssm_prep_bwd(*sv["ssm_in"], dar8, dai8, dbbr, dbbi)
    small["ssm_lam_re"], small["ssm_lam_im"] = dlr, dli
    small["ssm_log_dt"] = dldt.reshape(SSM_N_GROUPS, SSM_STATE).sum(axis=1)
    small["ssm_b_re"], small["ssm_b_im"] = _b_unrows(dbr), _b_unrows(dbi)
    dh_ssm = _time_unpermute(_add2("ssm_du", du1, du2, MM_DTYPE))
    dh = jnp.concatenate([dh_attn, dh_pool, dh_ssm], axis=1)
    dx_in = _mm_shard_cols_nt("in_proj_dx", dh, wg["w_in"], dep=run("h4", dh))
    dw_in = _mm_shard_cols_tn("in_proj_dw", sv["xb"], dh, N_CHIPS)
    start("mix", {"w_in": dw_in, "ssm_glu_w": dw_glu, "w_out": dw_out.reshape(N_CHIPS, MIX_WIDTH // N_CHIPS, D_MODEL)})
    return dr1, dx_in, small


CONV_PAD = 2 * SUBLANES


def _halved(name, a):
    if name == "ffn_conv_w":
        a = jnp.pad(a, ((0, 0), (0, CONV_PAD - CONV_WIDTH), (0, 0)))
    return a.reshape(a.shape[0], 2, a.shape[1] // 2, a.shape[2])


def _unhalved(name, a):
    a = a.reshape(a.shape[:-3] + (2 * a.shape[-2], a.shape[-1]))
    return a[..., :CONV_WIDTH, :] if name == "ffn_conv_w" else a


class _Reduce:
    def __init__(self, tag, grads, cm_idx):
        self.tag, self.cm_idx, self.names = tag, cm_idx, tuple(grads)
        g4 = [_halved(name, grads[name]) for name in self.names]
        self.sems, self.hbm, self.token = _swap_start("grad_swap_start_" + tag, g4)

    def swapped(self, after):
        g4, got = _swap_wait("grad_swap_wait_" + self.tag, self.hbm, self.sems, after)
        parts, lands = zip(*[_pair_add("grad_pair_add", g, x, self.cm_idx) for g, x in zip(g4, got)])
        self.sems, self.hbm, self.token = _scatter_start("grad_scatter_start_" + self.tag, parts, lands)
        return self.token

    def scattered(self, after):
        _, recv = _scatter_wait("grad_scatter_wait_" + self.tag, self.hbm, self.sems, after)
        halves = [_sum_leading("grad_chip_sum", r) for r in recv]
        self.sems, self.hbm, self.token = _exchange_start("grad_exchange_start_" + self.tag, halves)
        return self.token

    def finish(self, after):
        return _exchange_wait("grad_exchange_wait_" + self.tag, self.hbm, self.sems, after)


def kernel(x, w_in, attn_sinks, pool_w, pool_scale, ssm_lam_re, ssm_lam_im, ssm_log_dt, ssm_b_re, ssm_b_im, ssm_c_re, ssm_c_im, ssm_d, ssm_glu_w, w_out, ln1_g, ln1_b, ffn_w_up, ffn_conv_w, ffn_conv_b, ffn_w_down, ln2_g, ln2_b, loss_target, m_w_in, m_attn_sinks, m_pool_w, m_pool_scale, m_ssm_lam_re, m_ssm_lam_im, m_ssm_log_dt, m_ssm_b_re, m_ssm_b_im, m_ssm_c_re, m_ssm_c_im, m_ssm_d, m_ssm_glu_w, m_w_out, m_ln1_g, m_ln1_b, m_ffn_w_up, m_ffn_conv_w, m_ffn_conv_b, m_ffn_w_down, m_ln2_g, m_ln2_b, v_w_in, v_attn_sinks, v_pool_w, v_pool_scale, v_ssm_lam_re, v_ssm_lam_im, v_ssm_log_dt, v_ssm_b_re, v_ssm_b_im, v_ssm_c_re, v_ssm_c_im, v_ssm_d, v_ssm_glu_w, v_w_out, v_ln1_g, v_ln1_b, v_ffn_w_up, v_ffn_conv_w, v_ffn_conv_b, v_ffn_w_down, v_ln2_g, v_ln2_b):
    w = dict(w_in=w_in, attn_sinks=attn_sinks, pool_w=pool_w, pool_scale=pool_scale, ssm_lam_re=ssm_lam_re,
             ssm_lam_im=ssm_lam_im, ssm_log_dt=ssm_log_dt, ssm_b_re=ssm_b_re, ssm_b_im=ssm_b_im, ssm_c_re=ssm_c_re,
             ssm_c_im=ssm_c_im, ssm_d=ssm_d, ssm_glu_w=ssm_glu_w, w_out=w_out, ln1_g=ln1_g, ln1_b=ln1_b, ffn_w_up=ffn_w_up,
             ffn_conv_w=ffn_conv_w, ffn_conv_b=ffn_conv_b, ffn_w_down=ffn_w_down, ln2_g=ln2_g, ln2_b=ln2_b)
    m = dict(w_in=m_w_in, attn_sinks=m_attn_sinks, pool_w=m_pool_w, pool_scale=m_pool_scale, ssm_lam_re=m_ssm_lam_re,
             ssm_lam_im=m_ssm_lam_im, ssm_log_dt=m_ssm_log_dt, ssm_b_re=m_ssm_b_re, ssm_b_im=m_ssm_b_im, ssm_c_re=m_ssm_c_re,
             ssm_c_im=m_ssm_c_im, ssm_d=m_ssm_d, ssm_glu_w=m_ssm_glu_w, w_out=m_w_out, ln1_g=m_ln1_g, ln1_b=m_ln1_b,
             ffn_w_up=m_ffn_w_up, ffn_conv_w=m_ffn_conv_w, ffn_conv_b=m_ffn_conv_b, ffn_w_down=m_ffn_w_down, ln2_g=m_ln2_g,
             ln2_b=m_ln2_b)
    v = dict(w_in=v_w_in, attn_sinks=v_attn_sinks, pool_w=v_pool_w, pool_scale=v_pool_scale, ssm_lam_re=v_ssm_lam_re,
             ssm_lam_im=v_ssm_lam_im, ssm_log_dt=v_ssm_log_dt, ssm_b_re=v_ssm_b_re, ssm_b_im=v_ssm_b_im, ssm_c_re=v_ssm_c_re,
             ssm_c_im=v_ssm_c_im, ssm_d=v_ssm_d, ssm_glu_w=v_ssm_glu_w, w_out=v_w_out, ln1_g=v_ln1_g, ln1_b=v_ln1_b,
             ffn_w_up=v_ffn_w_up, ffn_conv_w=v_ffn_conv_w, ffn_conv_b=v_ffn_conv_b, ffn_w_down=v_ffn_w_down, ln2_g=v_ln2_g,
             ln2_b=v_ln2_b)
    c_pos = lax.axis_index("c").astype(jnp.int32)
    chip = (2 * lax.axis_index("x") + lax.axis_index("y")).astype(jnp.int32)
    c_idx, chip_idx, cm_idx = c_pos.reshape(1), chip.reshape(1), jnp.stack([c_pos, chip])
    rope_t = _rope_tables()
    xs = x.reshape(SEQ, D_MODEL)
    xb = xs.astype(MM_DTYPE)
    for t in (w, m, v):
        t["ffn_w_up"] = jnp.swapaxes(t["ffn_w_up"], 1, 2)
    wh, mh, vh = ({n: _halved(n, t[n]) for n in BIG} for t in (w, m, v))

    def place(l):
        return [_cast_place("place_" + n, wh[n], l, chip_idx, F32 if n == "ffn_conv_w" else MM_DTYPE) for n in BIG]

    n_mix = BIG.index("ffn_w_up")

    def gather_start(l, after):
        bufs = place(l)
        return (_gather_start("gather_start_%d_mix" % l, bufs[:n_mix], after),
                _gather_start("gather_start_%d_ffn" % l, bufs[n_mix:], after))

    def gather_forward(l, group, started, after):
        return _gather_forward("gather_forward_%d_%s" % (l, group), started[1], started[0], after)

    def gather_finish(l, group, forwarded, after):
        bufs = _gather_finish("gather_finish_%d_%s" % (l, group), forwarded[1], forwarded[0], after)
        names = BIG[:n_mix] if group == "mix" else BIG[n_mix:]
        return bufs, {n: _unhalved(n, g) for n, g in zip(names, bufs)}

    def gather_wait(l, group, started, after):
        return gather_finish(l, group, gather_forward(l, group, started, after), after)

    flight = gather_start(0, None)
    gathered, saved = [gather_wait(0, "mix", flight[0], None)[1]], []
    for l in range(DEPTH):
        nxt = {}

        def pre(after):
            if l == 0:
                return []
            nxt["forwarded"] = gather_forward(l, "ffn", flight[1], after)
            return [nxt["forwarded"][2]]

        def mid(after):
            forwarded = gather_forward(l, "ffn", flight[1], after) if l == 0 else nxt["forwarded"]
            bufs, wg_ffn = gather_finish(l, "ffn", forwarded, after)
            gathered[l].update(wg_ffn)
            if l + 1 == DEPTH:
                return []
            nxt["flight"] = gather_start(l + 1, bufs[0])
            return [nxt["flight"][0][2], nxt["flight"][1][2]]

        xs, xb, sv = _layer_fwd(xs, xb, {n: w[n][l] for n in SMALL}, gathered[l], rope_t, flight[1][2] if l == 0 else None, pre,
                                mid)
        saved.append(sv)
        if l + 1 < DEPTH:
            flight = nxt["flight"]
            gathered.append(gather_wait(l + 1, "mix", flight[0], xb)[1])
    dy, loss_tile = _loss(xs, loss_target.reshape(SEQ, D_MODEL))
    loss = lax.psum(loss_tile[0, 0], ("x", "y", "c"))

    big_out = {n: None for n in BIG}
    small_g = {n: [None] * DEPTH for n in SMALL}
    agenda = {}
    tail = []
    plan = {"ffn": (("h3", 0), ("h1", -1), ("h2", -1)), "mix": (("h1", -1), ("h3", -1), ("h4", -1))}
    tail_rank = {("ffn", 1): 0, ("mix", 0): 1, ("ffn", 2): 2, ("mix", 1): 3, ("mix", 2): 4}
    started = []

    def book(l, group, red):
        def update(after):
            names, own, got = red.names, *red.finish(after)
            for n, o, g in zip(names, own, got):
                big_out[n] = _adamw_big("adamw_" + n, l, c_idx, wh[n], mh[n], vh[n], o, g, big_out[n])
            return [big_out[names[-1]][0]] if l == 0 else []

        steps = (lambda a: [red.swapped(a)], lambda a: [red.scattered(a)], update)
        for k, ((hook, dl), step) in enumerate(zip(plan[group], steps)):
            if l + dl >= 0:
                agenda.setdefault((l + dl, hook), []).append(step)
            else:
                tail.append((tail_rank[group, k], step))

    def run_at(l):
        return lambda hook, after: [t for step in agenda.pop((l, hook), []) for t in step(after)]

    def start_at(l):
        def start(group, grads):
            red = _Reduce("%s_%d" % (group, l), grads, cm_idx)
            book(l, group, red)
            started.append(red.token)
            return red.token
        return start

    da, db, carry = dy, None, []
    for l in reversed(range(DEPTH)):
        agenda.setdefault((l, "h0"), []).append(lambda after, carry=carry: carry)
        da, db, small = _layer_bwd(da, db, {n: w[n][l] for n in SMALL}, gathered[l], saved[l], rope_t, run_at(l), start_at(l))
        for n in SMALL:
            small_g[n][l] = small[n].reshape(w[n].shape[1:])
        carry = run_at(l)("end", db) + started[-1:]
    shapes = {n: w[n].shape for n in SMALL}
    part = _pack({n: jnp.stack(small_g[n]) for n in SMALL})
    slots = lax.dynamic_update_slice(jnp.zeros((N_CHIPS, 2) + part.shape, F32), part[None, None], (chip, c_pos, 0, 0))
    small_sems, small_bufs, after = _gather_start("gather_start_small", [slots], pair=True)
    for _, step in sorted(tail, key=lambda rs: rs[0]):
        after = (step(after) or [after])[-1]
    grad_x = _ln_in_grad(da, db).reshape(x.shape)
    small_sems, small_bufs, _ = _gather_forward("gather_forward_small", small_bufs, small_sems, after, pair=True)
    small_bufs = _gather_finish("gather_finish_small", small_bufs, small_sems, grad_x)
    g_small = _sum_leading("small_grad_sum", small_bufs[0].reshape((N_DEV,) + part.shape))
    upd = _adamw("adamw_small", _pack(w), g_small, _pack(m), _pack(v))
    small_out = [_unpack(a, shapes) for a in (g_small,) + tuple(upd)]

    outs = [loss, grad_x]
    for kind in range(4):
        for n in ALL_W:
            if n in SMALL:
                outs.append(small_out[kind][n])
            else:
                o = _unhalved(n, big_out[n][kind])
                outs.append(jnp.swapaxes(o, 1, 2) if n == "ffn_w_up" else o)
    return tuple(outs)


def _ln_in_grad(dr1, dx_in):
    tm = _tile(SEQ, 512)

    def body(a_ref, b_ref, o_ref):
        o_ref[...] = DEEPNORM_ALPHA * a_ref[...] + b_ref[...]

    blk = pl.BlockSpec((tm, D_MODEL), lambda i: (i, 0))
    return pl.pallas_call(body, name="grad_x", grid=(SEQ // tm,), in_specs=[blk, blk], out_specs=blk,
                          out_shape=_sds((SEQ, D_MODEL)), compiler_params=_cparams(("parallel",)))(dr1, dx_in)
```

```python
import functools
import math

import jax
import jax.numpy as jnp
from jax import lax
from jax.experimental import pallas as pl
from jax.experimental.pallas import tpu as pltpu

F32 = jnp.float32
BF16 = jnp.bfloat16
MM_DTYPE = BF16

D_MODEL = 2048
SEQ = 2048
DEPTH = 4
D_FF = 5504
HEAD_DIM = 64
N_Q_HEADS = D_MODEL // 2 // HEAD_DIM
N_KV_HEADS = N_Q_HEADS // 4
ATTN_WIDTH = N_Q_HEADS * HEAD_DIM
KV_WIDTH = N_KV_HEADS * HEAD_DIM
ATTN_BLOCK = 128
ROPE_THETA = 10000.0
POOL_WINDOWS = (2, 4, 8, 16)
POOL_WIDTH = D_MODEL // 4
POOL_GROUP = POOL_WIDTH // len(POOL_WINDOWS)
SSM_WIDTH = D_MODEL // 4
SSM_GROUP = 16
SSM_N_GROUPS = SSM_WIDTH // SSM_GROUP
SSM_STATE = 64
SSM_CH = SSM_N_GROUPS * SSM_STATE
MIX_WIDTH = ATTN_WIDTH + POOL_WIDTH + SSM_WIDTH
IN_WIDTH = ATTN_WIDTH + 2 * KV_WIDTH + POOL_WIDTH + SSM_WIDTH
CONV_WIDTH = 3
LN_EPS = 1e-5
DEEPNORM_ALPHA = (2 * DEPTH) ** 0.25
ADAM_LR = 0.001
ADAM_B1 = 0.9
ADAM_B2 = 0.999
ADAM_EPS = 1e-08
ADAM_WD = 0.01
ADAM_STEP = 10

N_CHIPS = 4
N_DEV = 8
FS = 2 * D_FF // N_CHIPS
IN_S = IN_WIDTH // N_CHIPS
GLU_S = 2 * SSM_WIDTH // N_CHIPS
LANES = 128
SUBLANES = 8
SCAN_CW = 512
SCAN_NB = SSM_CH // SCAN_CW
SCAN_UNROLL = 4
VMEM_LIMIT = 56 * 1024 * 1024
COPY_BLOCK_BYTES = 6 * 1024 * 1024
NEG = -1e30

NN = (((1,), (0,)), ((), ()))
NT = (((1,), (1,)), ((), ()))
TN = (((0,), (0,)), ((), ()))
MESH = pl.DeviceIdType.MESH


def _tile(n, pref, mult=LANES):
    best = None
    for t in range(mult, min(n, pref) + 1, mult):
        if n % t == 0:
            best = t
    return n if best is None else best


def _cparams(sem):
    return pltpu.CompilerParams(dimension_semantics=sem, vmem_limit_bytes=VMEM_LIMIT)


def _sds(shape, dtype=F32):
    return jax.ShapeDtypeStruct(tuple(shape), dtype)


def _as_list(x):
    return [] if x is None else list(x) if isinstance(x, (list, tuple)) else [x]


def _mm(name, a, b, out_shape, grid, a_spec, b_spec, o_spec, dims, acc_shape, out_dtype=F32, dep=None):
    nk = grid[2]
    deps = _as_list(dep)

    def product(a_ref, b_ref):
        return lax.dot_general(a_ref[...].astype(MM_DTYPE), b_ref[...].astype(MM_DTYPE), dims, preferred_element_type=F32)

    def body_one(a_ref, b_ref, *rest):
        rest[-1][...] = product(a_ref, b_ref).astype(rest[-1].dtype)

    def body(a_ref, b_ref, *rest):
        o_ref, acc_ref = rest[-2:]
        k = pl.program_id(2)

        @pl.when(k == 0)
        def _():
            acc_ref[...] = product(a_ref, b_ref)

        @pl.when(k > 0)
        def _():
            acc_ref[...] += product(a_ref, b_ref)

        @pl.when(k == nk - 1)
        def _():
            o_ref[...] = acc_ref[...].astype(o_ref.dtype)

    return pl.pallas_call(
        body_one if nk == 1 else body, name=name, grid=grid, in_specs=[a_spec, b_spec] + [ANY] * len(deps),
        out_specs=o_spec, out_shape=_sds(out_shape, out_dtype),
        scratch_shapes=[] if nk == 1 else [pltpu.VMEM(acc_shape, F32)],
        compiler_params=_cparams(("parallel", "parallel", "arbitrary")))(a, b, *deps)


def _mm_nn(name, a, b, tm=2048, tn=512, tk=2048, out_dtype=F32, dep=None):
    m, kk = a.shape
    n = b.shape[1]
    tm, tn, tk = _tile(m, tm), _tile(n, tn), _tile(kk, tk)
    return _mm(name, a, b, (m, n), (m // tm, n // tn, kk // tk),
               pl.BlockSpec((tm, tk), lambda i, j, k: (i, k)), pl.BlockSpec((tk, tn), lambda i, j, k: (k, j)),
               pl.BlockSpec((tm, tn), lambda i, j, k: (i, j)), NN, (tm, tn), out_dtype, dep=dep)


def _mm_nt(name, a, b, tm=2048, tn=512, tk=2048, dep=None):
    m, kk = a.shape
    n = b.shape[0]
    tm, tn, tk = _tile(m, tm), _tile(n, tn), _tile(kk, tk)
    return _mm(name, a, b, (m, n), (m // tm, n // tn, kk // tk),
               pl.BlockSpec((tm, tk), lambda i, j, k: (i, k)), pl.BlockSpec((tn, tk), lambda i, j, k: (j, k)),
               pl.BlockSpec((tm, tn), lambda i, j, k: (i, j)), NT, (tm, tn), dep=dep)


def _mm_tn(name, a, b, tm=1024, tn=1024, ts=2048):
    s, m = a.shape
    n = b.shape[1]
    tm, tn, ts = _tile(m, tm), _tile(n, tn), _tile(s, ts)
    return _mm(name, a, b, (m, n), (m // tm, n // tn, s // ts),
               pl.BlockSpec((ts, tm), lambda i, j, k: (k, i)), pl.BlockSpec((ts, tn), lambda i, j, k: (k, j)),
               pl.BlockSpec((tm, tn), lambda i, j, k: (i, j)), TN, (tm, tn))


def _mm_shard_cols(name, a, w, tm=2048, tk=2048, dep=None):
    m, kk = a.shape
    nj, _, c = w.shape
    tm, tk = _tile(m, tm), _tile(kk, tk)
    return _mm(name, a, w, (m, nj * c), (m // tm, nj, kk // tk),
               pl.BlockSpec((tm, tk), lambda i, j, k: (i, k)), pl.BlockSpec((None, tk, c), lambda i, j, k: (j, k, 0)),
               pl.BlockSpec((tm, c), lambda i, j, k: (i, j)), NN, (tm, c), dep=dep)


def _mm_shard_cols_nt(name, d, w, tm=2048, tn=512, dep=None):
    m = d.shape[0]
    nj, n, c = w.shape
    tm, tn = _tile(m, tm), _tile(n, tn)
    return _mm(name, d, w, (m, n), (m // tm, n // tn, nj),
               pl.BlockSpec((tm, c), lambda i, j, k: (i, k)), pl.BlockSpec((None, tn, c), lambda i, j, k: (k, j, 0)),
               pl.BlockSpec((tm, tn), lambda i, j, k: (i, j)), NT, (tm, tn), dep=dep)


def _mm_shard_cols_tn(name, a, d, nj, tm=1024, ts=2048):
    s, m = a.shape
    c = d.shape[1] // nj
    tm, ts = _tile(m, tm), _tile(s, ts)
    return _mm(name, a, d, (nj, m, c), (nj, m // tm, s // ts),
               pl.BlockSpec((ts, tm), lambda j, i, k: (k, i)), pl.BlockSpec((ts, c), lambda j, i, k: (k, j)),
               pl.BlockSpec((None, tm, c), lambda j, i, k: (j, i, 0)), TN, (tm, c))


def _ffn_up(x1, w_up_t, tm=512, tk=2048, dep=None):
    s, d = x1.shape
    tm, tk = _tile(s, tm), _tile(d, tk)
    return _mm("ffn_up", x1, w_up_t, (N_CHIPS, s, FS), (N_CHIPS, s // tm, d // tk),
               pl.BlockSpec((tm, tk), lambda j, i, k: (i, k)), pl.BlockSpec((None, FS, tk), lambda j, i, k: (j, 0, k)),
               pl.BlockSpec((None, tm, FS), lambda j, i, k: (j, i, 0)), NT, (tm, FS), dep=dep)


def _ffn_down(act, w_down, tm=1024, tn=512):
    _, s, _ = act.shape
    d = w_down.shape[2]
    tm, tn = _tile(s, tm), _tile(d, tn)
    return _mm("ffn_down", act, w_down, (s, d), (s // tm, d // tn, 2),
               pl.BlockSpec((None, tm, FS), lambda i, j, k: (k, i, 0)), pl.BlockSpec((None, FS, tn), lambda i, j, k: (k, 0, j)),
               pl.BlockSpec((tm, tn), lambda i, j, k: (i, j)), NN, (tm, tn))


def _ffn_down_dact(df, w_down, tm=512, tk=2048):
    s, d = df.shape
    tm, tk = _tile(s, tm), _tile(d, tk)
    return _mm("ffn_down_dact", df, w_down, (2, s, FS), (2, s // tm, d // tk),
               pl.BlockSpec((tm, tk), lambda j, i, k: (i, k)), pl.BlockSpec((None, FS, tk), lambda j, i, k: (j, 0, k)),
               pl.BlockSpec((None, tm, FS), lambda j, i, k: (j, i, 0)), NT, (tm, FS))


def _ffn_down_dw(act, df, tn=512, ts=2048):
    _, s, _ = act.shape
    d = df.shape[1]
    tn, ts = _tile(d, tn), _tile(s, ts)
    return _mm("ffn_down_dw", act, df, (2, FS, d), (2, d // tn, s // ts),
               pl.BlockSpec((None, ts, FS), lambda p, j, k: (p, k, 0)), pl.BlockSpec((ts, tn), lambda p, j, k: (k, j)),
               pl.BlockSpec((None, FS, tn), lambda p, j, k: (p, 0, j)), TN, (FS, tn))


def _ffn_up_dx(dh, w_up_t, tm=1024, tn=1024, dep=None):
    s = dh.shape[2]
    d = w_up_t.shape[2]
    tm, tn = _tile(s, tm), _tile(d, tn)
    return _mm("ffn_up_dx", dh, w_up_t, (s, d), (s // tm, d // tn, N_CHIPS),
               pl.BlockSpec((None, None, tm, FS), lambda i, j, k: (k % 2, k // 2, i, 0)),
               pl.BlockSpec((None, FS, tn), lambda i, j, k: (k, 0, j)),
               pl.BlockSpec((tm, tn), lambda i, j, k: (i, j)), NN, (tm, tn), dep=dep)


def _ffn_up_dw(x1, dh, tn=512, ts=2048):
    s, d = x1.shape
    tn, ts = _tile(d, tn), _tile(s, ts)
    return _mm("ffn_up_dw", dh, x1, (N_CHIPS, FS, d), (N_CHIPS, d // tn, s // ts),
               pl.BlockSpec((None, None, ts, FS), lambda j, i, k: (j % 2, j // 2, k, 0)),
               pl.BlockSpec((ts, tn), lambda j, i, k: (k, i)),
               pl.BlockSpec((None, FS, tn), lambda j, i, k: (j, 0, i)), TN, (FS, tn))


def _rope_tables():
    half = HEAD_DIM // 2
    inv = ROPE_THETA ** (-jnp.arange(half, dtype=F32) / half)
    ang = jnp.arange(SEQ).astype(F32)[:, None] * inv[None, :]
    cos, sin = jnp.cos(ang), jnp.sin(ang)
    cos_t = jnp.tile(cos, (1, LANES // half))
    sin_t = jnp.tile(jnp.concatenate([-sin, sin], axis=1), (1, LANES // HEAD_DIM))
    return cos_t, sin_t


def _rotate_half(t):
    lane = lax.broadcasted_iota(jnp.int32, t.shape, 1)
    first = (lane % HEAD_DIM) < (HEAD_DIM // 2)
    return jnp.where(first, pltpu.roll(t, LANES - HEAD_DIM // 2, 1), pltpu.roll(t, HEAD_DIM // 2, 1))


def _rope(name, src, col_tile0, n_tiles, cos_t, sin_t, out_dtype):
    tm = _tile(SEQ, 512)
    assert col_tile0 % n_tiles == 0

    def body(x_ref, c_ref, s_ref, o_ref):
        cos, sin = c_ref[...], s_ref[...]
        for j in range(n_tiles):
            sl = slice(j * LANES, (j + 1) * LANES)
            t = x_ref[:, sl].astype(F32)
            o_ref[:, sl] = (t * cos + _rotate_half(t) * sin).astype(o_ref.dtype)

    wide = n_tiles * LANES
    return pl.pallas_call(
        body, name=name, grid=(SEQ // tm,),
        in_specs=[pl.BlockSpec((tm, wide), lambda i: (i, col_tile0 // n_tiles)),
                  pl.BlockSpec((tm, LANES), lambda i: (i, 0)), pl.BlockSpec((tm, LANES), lambda i: (i, 0))],
        out_specs=pl.BlockSpec((tm, wide), lambda i: (i, 0)),
        out_shape=_sds((SEQ, wide), out_dtype),
        compiler_params=_cparams(("parallel",)))(src, cos_t, sin_t)


Q_TILES = ATTN_WIDTH // LANES
KV_TILES = KV_WIDTH // LANES
Q_PER_KV_TILE = Q_TILES // KV_TILES
HEADS_PER_KV_TILE = N_Q_HEADS // KV_TILES
K_TILE0 = ATTN_WIDTH // LANES
V_TILE0 = (ATTN_WIDTH + KV_WIDTH) // LANES
N_QBLK = SEQ // ATTN_BLOCK


def _dup_half(t, which):
    lane = lax.broadcasted_iota(jnp.int32, t.shape, 1)
    r = pltpu.roll(t, HEAD_DIM, 1)
    lo = lane < HEAD_DIM
    return jnp.where(lo, t, r) if which == 0 else jnp.where(lo, r, t)


GROUP_HEADS = N_Q_HEADS // N_KV_HEADS
GROUP_ROWS = GROUP_HEADS * ATTN_BLOCK


def _attn_stack(tiles, lo):
    return jnp.concatenate([jnp.where(lo == (hs == 0), t, 0.0) for t in tiles for hs in range(2)], axis=0)


def _attn_unstack(x, j, lo):
    r = 2 * j * ATTN_BLOCK
    return jnp.where(lo, x[r:r + ATTN_BLOCK], x[r + ATTN_BLOCK:r + 2 * ATTN_BLOCK])


def _attn_group_consts(n, sink_ref, head0):
    shape = (GROUP_ROWS, 2 * ATTN_BLOCK)
    qi = lax.broadcasted_iota(jnp.int32, shape, 0) % ATTN_BLOCK
    col = lax.broadcasted_iota(jnp.int32, shape, 1)
    valid = ((col < ATTN_BLOCK) & (col <= qi)) | ((col >= ATTN_BLOCK) & (col - ATTN_BLOCK > qi) & (n > 0))
    head = lax.broadcasted_iota(jnp.int32, (GROUP_ROWS, 1), 0) // ATTN_BLOCK
    sinks = jnp.zeros((GROUP_ROWS, 1), F32)
    for i in range(GROUP_HEADS):
        sinks = jnp.where(head == i, sink_ref[head0 + i], sinks)
    return valid, sinks, head


def _attn_probs(qs, k2, valid, sinks):
    s = lax.dot_general(qs, k2, NT, preferred_element_type=F32) * HEAD_DIM ** -0.5
    s = jnp.where(valid, s, NEG)
    m = jnp.maximum(s.max(1, keepdims=True), sinks)
    p = jnp.exp(s - m)
    esink = jnp.exp(sinks - m)
    inv = 1.0 / (p.sum(1, keepdims=True) + esink)
    return p * inv, esink * inv


def _attn_kv(cur, prev, kvl):
    return jnp.concatenate([_dup_half(cur, kvl), _dup_half(prev, kvl)], axis=0).astype(MM_DTYPE)


def _attn_specs():
    blk = (ATTN_BLOCK, LANES)
    wide = (ATTN_BLOCK, Q_PER_KV_TILE * LANES)
    prev = lambda n: jnp.maximum(n - 1, 0)
    q_spec = pl.BlockSpec(wide, lambda t, n: (n, t))
    kc = pl.BlockSpec(blk, lambda t, n: (n, K_TILE0 + t))
    kp = pl.BlockSpec(blk, lambda t, n: (prev(n), K_TILE0 + t))
    vc = pl.BlockSpec(blk, lambda t, n: (n, V_TILE0 + t))
    vp = pl.BlockSpec(blk, lambda t, n: (prev(n), V_TILE0 + t))
    return q_spec, kc, kp, vc, vp, pl.BlockSpec(memory_space=pltpu.SMEM)


def _attn_fwd(qk, h, sinks):
    q_spec, kc_s, kp_s, vc_s, vp_s, smem = _attn_specs()

    def body(sink_ref, q_ref, kc_ref, kp_ref, vc_ref, vp_ref, o_ref, ob_ref):
        t, n = pl.program_id(0), pl.program_id(1)
        lo = lax.broadcasted_iota(jnp.int32, (ATTN_BLOCK, LANES), 1) < HEAD_DIM
        kc, kp = kc_ref[...].astype(F32), kp_ref[...].astype(F32)
        vc, vp = vc_ref[...], vp_ref[...]
        for kvl in range(2):
            valid, sink_rows, _ = _attn_group_consts(n, sink_ref, t * HEADS_PER_KV_TILE + kvl * GROUP_HEADS)
            tiles = [q_ref[:, a * LANES:(a + 1) * LANES].astype(F32) for a in (2 * kvl, 2 * kvl + 1)]
            qs = _attn_stack(tiles, lo).astype(MM_DTYPE)
            pn, _ = _attn_probs(qs, _attn_kv(kc, kp, kvl), valid, sink_rows)
            os_ = lax.dot_general(pn.astype(MM_DTYPE), _attn_kv(vc, vp, kvl), NN, preferred_element_type=F32)
            for j in range(2):
                a = 2 * kvl + j
                o = _attn_unstack(os_, j, lo)
                o_ref[:, a * LANES:(a + 1) * LANES] = o
                ob_ref[:, a * LANES:(a + 1) * LANES] = o.astype(ob_ref.dtype)

    return pl.pallas_call(
        body, name="attn_fwd", grid=(KV_TILES, N_QBLK),
        in_specs=[smem, q_spec, kc_s, kp_s, vc_s, vp_s], out_specs=[q_spec, q_spec],
        out_shape=[_sds((SEQ, ATTN_WIDTH)), _sds((SEQ, ATTN_WIDTH), MM_DTYPE)],
        compiler_params=_cparams(("parallel", "parallel")))(sinks, qk, qk, qk, h, h)


def _attn_bwd(qk, h, sinks, y, dy, dy_tile0, dep=None):
    deps = _as_list(dep)
    q_spec, kc_s, kp_s, vc_s, vp_s, smem = _attn_specs()
    blk = (ATTN_BLOCK, LANES)
    wide = (ATTN_BLOCK, Q_PER_KV_TILE * LANES)
    kv_out = pl.BlockSpec(blk, lambda t, n: (n, t))
    dy_spec = pl.BlockSpec(wide, lambda t, n: (n, t + dy_tile0))

    def body(sink_ref, q_ref, kc_ref, kp_ref, vc_ref, vp_ref, y_ref, dy_ref, *rest):
        dq_ref, dkc_ref, dkp_ref, dvc_ref, dvp_ref, dsk_ref = rest[-6:]
        t, n = pl.program_id(0), pl.program_id(1)
        lo = lax.broadcasted_iota(jnp.int32, blk, 1) < HEAD_DIM
        kc, kp = kc_ref[...].astype(F32), kp_ref[...].astype(F32)
        vc, vp = vc_ref[...], vp_ref[...]
        hrow = lax.broadcasted_iota(jnp.int32, (HEADS_PER_KV_TILE, LANES), 0)
        dsk = jnp.zeros((HEADS_PER_KV_TILE, LANES), F32)
        dk2, dv2 = [], []
        for kvl in range(2):
            valid, sink_rows, head = _attn_group_consts(n, sink_ref, t * HEADS_PER_KV_TILE + kvl * GROUP_HEADS)
            sls = [slice(a * LANES, (a + 1) * LANES) for a in (2 * kvl, 2 * kvl + 1)]
            qs = _attn_stack([q_ref[:, sl].astype(F32) for sl in sls], lo).astype(MM_DTYPE)
            dos = _attn_stack([dy_ref[:, sl] for sl in sls], lo)
            delta = _attn_stack([dy_ref[:, sl] * y_ref[:, sl] for sl in sls], lo).sum(1, keepdims=True)
            dos = dos.astype(MM_DTYPE)
            k2, v2 = _attn_kv(kc, kp, kvl), _attn_kv(vc, vp, kvl)
            pn, psink = _attn_probs(qs, k2, valid, sink_rows)
            dp = lax.dot_general(dos, v2, NT, preferred_element_type=F32)
            ds = (pn * (dp - delta) * HEAD_DIM ** -0.5).astype(MM_DTYPE)
            dqs = lax.dot_general(ds, k2, NN, preferred_element_type=F32)
            for j, sl in enumerate(sls):
                dq_ref[:, sl] = _attn_unstack(dqs, j, lo)
            for acc, x in ((dk2, lax.dot_general(ds, qs, TN, preferred_element_type=F32)),
                           (dv2, lax.dot_general(pn.astype(MM_DTYPE), dos, TN, preferred_element_type=F32))):
                acc.append(x + pltpu.roll(x, HEAD_DIM, 1))
            dsink = psink * delta
            for i in range(GROUP_HEADS):
                dsk = dsk + jnp.where(hrow == kvl * GROUP_HEADS + i, -jnp.sum(jnp.where(head == i, dsink, 0.0)), 0.0)
        for o_ref, src, r in ((dkc_ref, dk2, 0), (dkp_ref, dk2, ATTN_BLOCK), (dvc_ref, dv2, 0), (dvp_ref, dv2, ATTN_BLOCK)):
            o_ref[...] = jnp.where(lo, src[0][r:r + ATTN_BLOCK], src[1][r:r + ATTN_BLOCK])

        @pl.when(n == 0)
        def _():
            dsk_ref[...] = jnp.zeros_like(dsk_ref)

        dsk_ref[...] += dsk

    kv_shape = _sds((SEQ, KV_WIDTH))
    return pl.pallas_call(
        body, name="attn_bwd", grid=(KV_TILES, N_QBLK),
        in_specs=[smem, q_spec, kc_s, kp_s, vc_s, vp_s, q_spec, dy_spec] + [ANY] * len(deps),
        out_specs=[q_spec, kv_out, kv_out, kv_out, kv_out,
                   pl.BlockSpec((None, HEADS_PER_KV_TILE, LANES), lambda t, n: (t, 0, 0))],
        out_shape=[_sds((SEQ, ATTN_WIDTH)), kv_shape, kv_shape, kv_shape, kv_shape,
                   _sds((KV_TILES, HEADS_PER_KV_TILE, LANES))],
        compiler_params=_cparams(("parallel", "arbitrary")))(sinks, qk, qk, qk, h, h, y, dy, *deps)


def _attn_dh(dq, dkc, dkp, dvc, dvp, cos_t, nsin_t):
    n_tiles = Q_TILES + 2 * KV_TILES
    nxt = lambda n: jnp.minimum(n + 1, N_QBLK - 1)

    def body(dq_ref, kc_ref, kp_ref, vc_ref, vp_ref, c_ref, s_ref, o_ref):
        has_next = pl.program_id(0) < N_QBLK - 1
        cos, sin = c_ref[...], s_ref[...]

        def unrope(t):
            return t * cos + _rotate_half(t) * sin

        for j in range(Q_TILES):
            sl = slice(j * LANES, (j + 1) * LANES)
            o_ref[:, sl] = unrope(dq_ref[:, sl]).astype(o_ref.dtype)
        for j in range(KV_TILES):
            sl = slice(j * LANES, (j + 1) * LANES)
            t = kc_ref[:, sl] + jnp.where(has_next, kp_ref[:, sl], 0.0)
            o_ref[:, ATTN_WIDTH + j * LANES:ATTN_WIDTH + (j + 1) * LANES] = unrope(t).astype(o_ref.dtype)
        o_ref[:, ATTN_WIDTH + KV_WIDTH:] = (vc_ref[...] + jnp.where(has_next, vp_ref[...], 0.0)).astype(o_ref.dtype)

    qb, kb, tb = (ATTN_BLOCK, ATTN_WIDTH), (ATTN_BLOCK, KV_WIDTH), (ATTN_BLOCK, LANES)
    return pl.pallas_call(
        body, name="attn_dh", grid=(N_QBLK,),
        in_specs=[pl.BlockSpec(qb, lambda n: (n, 0)),
                  pl.BlockSpec(kb, lambda n: (n, 0)), pl.BlockSpec(kb, lambda n: (nxt(n), 0)),
                  pl.BlockSpec(kb, lambda n: (n, 0)), pl.BlockSpec(kb, lambda n: (nxt(n), 0)),
                  pl.BlockSpec(tb, lambda n: (n, 0)), pl.BlockSpec(tb, lambda n: (n, 0))],
        out_specs=pl.BlockSpec((ATTN_BLOCK, n_tiles * LANES), lambda n: (n, 0)),
        out_shape=_sds((SEQ, n_tiles * LANES), MM_DTYPE),
        compiler_params=_cparams(("parallel",)))(dq, dkc, dkp, dvc, dvp, cos_t, nsin_t)


POOL_TILE0 = (ATTN_WIDTH + 2 * KV_WIDTH) // POOL_WIDTH


def _shift_rows(x, d, down):
    n = x.shape[0]
    row = lax.broadcasted_iota(jnp.int32, x.shape, 0)
    if down:
        return jnp.where(row >= d, pltpu.roll(x, d, 0), 0.0)
    return jnp.where(row < n - d, pltpu.roll(x, n - d, 0), 0.0)


def _window_sum(x, w, down):
    d = 1
    while d < w:
        x = x + _shift_rows(x, d, down)
        d *= 2
    return x


def _pool_z(u, w):
    t = lax.broadcasted_iota(jnp.int32, u.shape, 0).astype(F32)
    cnt = jnp.minimum(t + 1.0, float(w))
    return _window_sum(u, w, True) / cnt - u, cnt


def _pool_fwd(h, pool_w, pool_scale):
    def body(u_ref, w_ref, s_ref, o_ref):
        for gi, w in enumerate(POOL_WINDOWS):
            sl = slice(gi * POOL_GROUP, (gi + 1) * POOL_GROUP)
            z, _ = _pool_z(u_ref[:, sl], w)
            o_ref[:, sl] = (lax.dot_general(z.astype(MM_DTYPE), w_ref[gi].astype(MM_DTYPE), NN,
                                            preferred_element_type=F32) * s_ref[:, sl]).astype(o_ref.dtype)

    return pl.pallas_call(
        body, name="pool_fwd", grid=(1,),
        in_specs=[pl.BlockSpec((SEQ, POOL_WIDTH), lambda i: (0, POOL_TILE0)),
                  pl.BlockSpec(pool_w.shape, lambda i: (0, 0, 0)), pl.BlockSpec((1, POOL_WIDTH), lambda i: (0, 0))],
        out_specs=pl.BlockSpec((SEQ, POOL_WIDTH), lambda i: (0, 0)),
        out_shape=_sds((SEQ, POOL_WIDTH), MM_DTYPE), compiler_params=_cparams(("arbitrary",)))(h, pool_w, pool_scale)


def _pool_bwd(h, pool_w, pool_scale, dmix, dy_tile0):
    def body(u_ref, w_ref, s_ref, dy_ref, du_ref, dw_ref, ds_ref):
        for gi, w in enumerate(POOL_WINDOWS):
            sl = slice(gi * POOL_GROUP, (gi + 1) * POOL_GROUP)
            z, cnt = _pool_z(u_ref[:, sl], w)
            zb, wb = z.astype(MM_DTYPE), w_ref[gi].astype(MM_DTYPE)
            dy = dy_ref[:, sl]
            zp = lax.dot_general(zb, wb, NN, preferred_element_type=F32)
            ds_ref[:, sl] = jnp.sum(dy * zp, axis=0, keepdims=True)
            dyo = (dy * s_ref[:, sl]).astype(MM_DTYPE)
            dw_ref[gi] = lax.dot_general(zb, dyo, TN, preferred_element_type=F32)
            dz = lax.dot_general(dyo, wb, NT, preferred_element_type=F32)
            du_ref[:, sl] = (_window_sum(dz / cnt, w, False) - dz).astype(du_ref.dtype)

    return pl.pallas_call(
        body, name="pool_bwd", grid=(1,),
        in_specs=[pl.BlockSpec((SEQ, POOL_WIDTH), lambda i: (0, POOL_TILE0)),
                  pl.BlockSpec(pool_w.shape, lambda i: (0, 0, 0)), pl.BlockSpec((1, POOL_WIDTH), lambda i: (0, 0)),
                  pl.BlockSpec((SEQ, POOL_WIDTH), lambda i: (0, dy_tile0))],
        out_specs=[pl.BlockSpec((SEQ, POOL_WIDTH), lambda i: (0, 0)), pl.BlockSpec(pool_w.shape, lambda i: (0, 0, 0)),
                   pl.BlockSpec((1, POOL_WIDTH), lambda i: (0, 0))],
        out_shape=[_sds((SEQ, POOL_WIDTH), MM_DTYPE), _sds(pool_w.shape), _sds((1, POOL_WIDTH))],
        compiler_params=_cparams(("arbitrary",)))(h, pool_w, pool_scale, dmix)


def _ssm_discretize(lr, li, ldt, br, bi):
    dt = jnp.exp(ldt)
    mag = jnp.exp(lr * dt)
    ar, ai = mag * jnp.cos(li * dt), mag * jnp.sin(li * dt)
    nr, ni = ar - 1.0, ai
    den = lr * lr + li * li
    zr = (nr * lr + ni * li) / den
    zi = (ni * lr - nr * li) / den
    return ar, ai, zr * br - zi * bi, zr * bi + zi * br


def _ssm_prep(lr, li, ldt, br, bi):
    def body(lr_ref, li_ref, ldt_ref, br_ref, bi_ref, ar_ref, ai_ref, bbr_ref, bbi_ref):
        outs = _ssm_discretize(lr_ref[...], li_ref[...], ldt_ref[...], br_ref[...], bi_ref[...])
        for o, v in zip((ar_ref, ai_ref, bbr_ref, bbi_ref), outs):
            o[...] = v

    row, mat = _sds((1, SSM_CH)), _sds((SSM_GROUP, SSM_CH))
    return pl.pallas_call(body, name="ssm_prep", out_shape=[row, row, mat, mat])(lr, li, ldt, br, bi)


def _ssm_prep_bwd(lr, li, ldt, br, bi, dar8, dai8, dbbr, dbbi):
    def body(lr_ref, li_ref, ldt_ref, br_ref, bi_ref, dar_ref, dai_ref, dbbr_ref, dbbi_ref, *outs):
        args = (lr_ref[...], li_ref[...], ldt_ref[...], br_ref[...], bi_ref[...])
        _, vjp = jax.vjp(_ssm_discretize, *args)
        cot = (jnp.sum(dar_ref[...], axis=0, keepdims=True), jnp.sum(dai_ref[...], axis=0, keepdims=True),
               dbbr_ref[...], dbbi_ref[...])
        for o, v in zip(outs, vjp(cot)):
            o[...] = v

    row, mat = _sds((1, SSM_CH)), _sds((SSM_GROUP, SSM_CH))
    return pl.pallas_call(body, name="ssm_prep_bwd", out_shape=[row, row, row, mat, mat])(
        lr, li, ldt, br, bi, dar8, dai8, dbbr, dbbi)


def _ssm_diag(name, full):
    tiles = SCAN_CW // LANES
    groups = LANES // SSM_STATE
    rows = tiles * groups * SSM_GROUP

    def body(x_ref, o_ref):
        lane = lax.broadcasted_iota(jnp.int32, (SSM_GROUP, LANES), 1)
        for q in range(2 * tiles):
            sl = slice(q * LANES, (q + 1) * LANES)
            r0 = (q % tiles) * groups * SSM_GROUP
            out = x_ref[r0:r0 + SSM_GROUP, sl]
            for k in range(1, groups):
                out = jnp.where(lane >= k * SSM_STATE, x_ref[r0 + k * SSM_GROUP:r0 + (k + 1) * SSM_GROUP, sl], out)
            o_ref[:, sl] = out

    return pl.pallas_call(
        body, name=name, grid=(SCAN_NB,),
        in_specs=[pl.BlockSpec((rows, 2 * SCAN_CW), lambda b: (b, b))],
        out_specs=pl.BlockSpec((SSM_GROUP, 2 * SCAN_CW), lambda b: (0, b)),
        out_shape=_sds((SSM_GROUP, 2 * SSM_CH)), compiler_params=_cparams(("parallel",)))(full)


def _scan_layout(re, im):
    r = re.shape[0]
    return jnp.stack([re.reshape(r, SCAN_NB, SCAN_CW), im.reshape(r, SCAN_NB, SCAN_CW)], axis=2).reshape(r, 2 * SSM_CH)


def _scan_unlayout(x):
    r = x.shape[0]
    x = x.reshape(r, SCAN_NB, 2, SCAN_CW)
    return x[:, :, 0].reshape(r, SSM_CH), x[:, :, 1].reshape(r, SSM_CH)


def _time_permute(u):
    s, c = u.shape
    return u.reshape(SUBLANES, s // SUBLANES, c).transpose(1, 0, 2).reshape(s, c)


def _time_unpermute(u):
    s, c = u.shape
    return u.reshape(s // SUBLANES, SUBLANES, c).transpose(1, 0, 2).reshape(s, c)


def _ssm_scan(name, a_vec, x, reverse, s_prev=None):
    nsteps = SEQ // SUBLANES
    cw = SCAN_CW
    with_da = s_prev is not None

    def body(a_ref, x_ref, *rest):
        if with_da:
            s_ref, o_ref, da_ref = rest
        else:
            o_ref, = rest
        ar = jnp.broadcast_to(a_ref[:, :cw], (SUBLANES, cw))
        ai = jnp.broadcast_to(a_ref[:, cw:], (SUBLANES, cw))
        seg = lax.broadcasted_iota(jnp.int32, (SUBLANES, cw), 0)

        def toward(v):
            if reverse:
                return jnp.where(seg < SUBLANES - 1, pltpu.roll(v, SUBLANES - 1, 0), 0.0)
            return jnp.where(seg >= 1, pltpu.roll(v, 1, 0), 0.0)

        def rows(j):
            jj = nsteps - 1 - j if reverse else j
            return pl.ds(pl.multiple_of(jj * SUBLANES, SUBLANES), SUBLANES)

        def cmul(pr, pi, qr, qi):
            return pr * qr - pi * qi, pr * qi + pi * qr

        def local(j, c):
            sr, si = c
            r = rows(j)
            mr, mi = cmul(ar, ai, sr, si)
            return mr + x_ref[r, :cw], mi + x_ref[r, cw:]

        zero = jnp.zeros((SUBLANES, cw), F32)
        fr, fi = lax.fori_loop(0, nsteps, local, (zero, zero), unroll=SCAN_UNROLL)

        def power(_, c):
            return cmul(ar, ai, *c)

        pr, pi = lax.fori_loop(0, nsteps - 1, power, (ar, ai))
        tr, ti = fr, fi
        for _ in range(SUBLANES - 1):
            mr, mi = cmul(pr, pi, toward(tr), toward(ti))
            tr, ti = fr + mr, fi + mi
        init = (toward(tr), toward(ti))

        def advance(j, sr, si):
            r = rows(j)
            mr, mi = cmul(ar, ai, sr, si)
            sr, si = mr + x_ref[r, :cw], mi + x_ref[r, cw:]
            o_ref[r, :cw] = sr
            o_ref[r, cw:] = si
            return sr, si

        def full(j, c):
            return advance(j, *c)

        def full_da(j, c):
            sr, si = advance(j, c[0], c[1])
            rp = pl.ds(pl.multiple_of((nsteps - 2 - j) * SUBLANES, SUBLANES), SUBLANES)
            spr, spi = s_ref[rp, :cw], s_ref[rp, cw:]
            return sr, si, c[2] + sr * spr + si * spi, c[3] + si * spr - sr * spi

        if with_da:
            sr, si, dar, dai = lax.fori_loop(0, nsteps - 1, full_da, init + (zero, zero), unroll=SCAN_UNROLL)
            sr, si = advance(nsteps - 1, sr, si)
            last = pl.ds((nsteps - 1) * SUBLANES, SUBLANES)
            spr = jnp.where(seg >= 1, pltpu.roll(s_ref[last, :cw], 1, 0), 0.0)
            spi = jnp.where(seg >= 1, pltpu.roll(s_ref[last, cw:], 1, 0), 0.0)
            da_ref[:, :cw] = dar + sr * spr + si * spi
            da_ref[:, cw:] = dai + si * spr - sr * spi
        else:
            lax.fori_loop(0, nsteps, full, init, unroll=SCAN_UNROLL)

    blk = pl.BlockSpec((SEQ, 2 * cw), lambda b: (0, b))
    a_spec = pl.BlockSpec((1, 2 * cw), lambda b: (0, b))
    in_specs, args = [a_spec, blk], [a_vec, x]
    out_specs, out_shape = blk, _sds((SEQ, 2 * SSM_CH))
    if with_da:
        in_specs, args = in_specs + [blk], args + [s_prev]
        out_specs = [blk, pl.BlockSpec((SUBLANES, 2 * cw), lambda b: (0, b))]
        out_shape = [out_shape, _sds((SUBLANES, 2 * SSM_CH))]
    return pl.pallas_call(body, name=name, grid=(SCAN_NB,), in_specs=in_specs, out_specs=out_specs,
                          out_shape=out_shape, compiler_params=_cparams(("parallel",)))(*args)


def _ssm_gelu(yp, up, dvec):
    tm = _tile(SEQ, 512)

    def body(y_ref, u_ref, d_ref, yf_ref, g_ref):
        yf = y_ref[...] + d_ref[...] * u_ref[...]
        yf_ref[...] = yf
        g_ref[...] = jax.nn.gelu(yf).astype(g_ref.dtype)

    blk = pl.BlockSpec((tm, SSM_WIDTH), lambda i: (i, 0))
    row = pl.BlockSpec((1, SSM_WIDTH), lambda i: (0, 0))
    return pl.pallas_call(body, name="ssm_gelu", grid=(SEQ // tm,), in_specs=[blk, blk, row], out_specs=[blk, blk],
                          out_shape=[_sds((SEQ, SSM_WIDTH)), _sds((SEQ, SSM_WIDTH), MM_DTYPE)],
                          compiler_params=_cparams(("parallel",)))(yp, up, dvec)


def _ssm_gelu_bwd(yf, dgy, up, dvec):
    tm = _tile(SEQ, 512)

    def body(yf_ref, dg_ref, u_ref, d_ref, dyf_ref, du_ref, dd_ref):
        _, vjp = jax.vjp(jax.nn.gelu, yf_ref[...])
        dyf, = vjp(dg_ref[...])
        dyf_ref[...] = dyf.astype(dyf_ref.dtype)
        du_ref[...] = d_ref[...] * dyf

        @pl.when(pl.program_id(0) == 0)
        def _():
            dd_ref[...] = jnp.zeros_like(dd_ref)

        dd_ref[...] += jnp.sum(dyf * u_ref[...], axis=0, keepdims=True)

    blk = pl.BlockSpec((tm, SSM_WIDTH), lambda i: (i, 0))
    row = pl.BlockSpec((1, SSM_WIDTH), lambda i: (0, 0))
    return pl.pallas_call(body, name="ssm_gelu_bwd", grid=(SEQ // tm,), in_specs=[blk, blk, blk, row],
                          out_specs=[blk, blk, row],
                          out_shape=[_sds((SEQ, SSM_WIDTH), MM_DTYPE), _sds((SEQ, SSM_WIDTH)), _sds((1, SSM_WIDTH))],
                          compiler_params=_cparams(("arbitrary",)))(yf, dgy, up, dvec)


def _glu(ab):
    return ab[:, :SSM_WIDTH] * jax.nn.sigmoid(ab[:, SSM_WIDTH:])


def _ssm_glu(ab):
    tm = _tile(SEQ, 512)

    def body(ab_ref, o_ref):
        o_ref[...] = _glu(ab_ref[...]).astype(o_ref.dtype)

    return pl.pallas_call(body, name="ssm_glu", grid=(SEQ // tm,),
                          in_specs=[pl.BlockSpec((tm, 2 * SSM_WIDTH), lambda i: (i, 0))],
                          out_specs=pl.BlockSpec((tm, SSM_WIDTH), lambda i: (i, 0)),
                          out_shape=_sds((SEQ, SSM_WIDTH), MM_DTYPE), compiler_params=_cparams(("parallel",)))(ab)


def _ssm_glu_bwd(ab, dout):
    tm = _tile(SEQ, 512)

    def body(ab_ref, do_ref, dab_ref):
        _, vjp = jax.vjp(_glu, ab_ref[...])
        dab, = vjp(do_ref[...])
        dab_ref[...] = dab.astype(dab_ref.dtype)

    return pl.pallas_call(body, name="ssm_glu_bwd", grid=(SEQ // tm,),
                          in_specs=[pl.BlockSpec((tm, 2 * SSM_WIDTH), lambda i: (i, 0)),
                                    pl.BlockSpec((tm, SSM_WIDTH), lambda i: (i, 0))],
                          out_specs=pl.BlockSpec((tm, 2 * SSM_WIDTH), lambda i: (i, 0)),
                          out_shape=_sds((SEQ, 2 * SSM_WIDTH), MM_DTYPE), compiler_params=_cparams(("parallel",)))(ab, dout)


def _add2(name, a, b, out_dtype):
    tm = _tile(a.shape[0], 512)

    def body(a_ref, b_ref, o_ref):
        o_ref[...] = (a_ref[...] + b_ref[...]).astype(o_ref.dtype)

    blk = pl.BlockSpec((tm, a.shape[1]), lambda i: (i, 0))
    return pl.pallas_call(body, name=name, grid=(a.shape[0] // tm,), in_specs=[blk, blk], out_specs=blk,
                          out_shape=_sds(a.shape, out_dtype), compiler_params=_cparams(("parallel",)))(a, b)


def _layer_norm(r, g, b):
    mu = r.mean(-1, keepdims=True)
    var = jnp.square(r - mu).mean(-1, keepdims=True)
    return (r - mu) * lax.rsqrt(var + LN_EPS) * g + b


def _ln_fwd(name, x, y, g, b):
    tm = _tile(SEQ, 256)

    def body(x_ref, y_ref, g_ref, b_ref, r_ref, o_ref, ob_ref):
        r = DEEPNORM_ALPHA * x_ref[...] + y_ref[...]
        r_ref[...] = r
        o = _layer_norm(r, g_ref[...], b_ref[...])
        o_ref[...] = o
        ob_ref[...] = o.astype(ob_ref.dtype)

    blk = pl.BlockSpec((tm, D_MODEL), lambda i: (i, 0))
    row = pl.BlockSpec((1, D_MODEL), lambda i: (0, 0))
    return pl.pallas_call(body, name=name, grid=(SEQ // tm,), in_specs=[blk, blk, row, row], out_specs=[blk, blk, blk],
                          out_shape=[_sds((SEQ, D_MODEL))] * 2 + [_sds((SEQ, D_MODEL), MM_DTYPE)],
                          compiler_params=_cparams(("parallel",)))(x, y, g, b)


def _ln_bwd(name, r, g, b, da, db=None, dep=None):
    tm = _tile(SEQ, 256)
    two = db is not None
    deps = _as_list(dep)

    def body(r_ref, g_ref, b_ref, da_ref, *rest):
        dr_ref, drb_ref, dg_ref, dbeta_ref = rest[-4:]
        dout = DEEPNORM_ALPHA * da_ref[...] + rest[0][...] if two else da_ref[...]
        _, vjp = jax.vjp(_layer_norm, r_ref[...], g_ref[...], b_ref[...])
        dr, dg, dbeta = vjp(dout)
        dr_ref[...] = dr
        drb_ref[...] = dr.astype(drb_ref.dtype)

        @pl.when(pl.program_id(0) == 0)
        def _():
            dg_ref[...] = jnp.zeros_like(dg_ref)
            dbeta_ref[...] = jnp.zeros_like(dbeta_ref)

        dg_ref[...] += dg
        dbeta_ref[...] += dbeta

    blk = pl.BlockSpec((tm, D_MODEL), lambda i: (i, 0))
    row = pl.BlockSpec((1, D_MODEL), lambda i: (0, 0))
    args = [r, g, b, da] + ([db] if two else []) + deps
    return pl.pallas_call(body, name=name, grid=(SEQ // tm,),
                          in_specs=[blk, row, row, blk] + ([blk] if two else []) + [ANY] * len(deps),
                          out_specs=[blk, blk, row, row],
                          out_shape=[_sds((SEQ, D_MODEL)), _sds((SEQ, D_MODEL), MM_DTYPE), _sds((1, D_MODEL)), _sds((1, D_MODEL))],
                          compiler_params=_cparams(("arbitrary",)))(*args)


FFN_TM = 256
HALO = SUBLANES


def _conv_taps(cur, halo):
    row = lax.broadcasted_iota(jnp.int32, cur.shape, 0)
    h1 = jnp.where(row == 0, halo[HALO - 1:HALO, :], pltpu.roll(cur, 1, 0))
    h2 = jnp.where(row == 0, halo[HALO - 2:HALO - 1, :], jnp.where(row == 1, halo[HALO - 1:HALO, :], pltpu.roll(cur, 2, 0)))
    return h1, h2


def _conv_fwd(cur, halo, w_ref, b_ref):
    h1, h2 = _conv_taps(cur, halo)
    return b_ref[...] + h2 * w_ref[0:1, :] + h1 * w_ref[1:2, :] + cur * w_ref[2:3, :], h1, h2


def _gate(val, gate):
    return jax.nn.silu(gate) * val


def _ffn_specs(tm):
    nb = tm // HALO
    cur = lambda off: pl.BlockSpec((None, tm, FS), lambda p, i: (p + off, i, 0))
    halo = lambda off: pl.BlockSpec((None, HALO, FS), lambda p, i: (p + off, jnp.maximum(i * nb - 1, 0), 0))
    cw = lambda off: pl.BlockSpec((None, CONV_WIDTH, FS), lambda p, i: (p + off, 0, 0))
    cb = lambda off: pl.BlockSpec((None, 1, FS), lambda p, i: (p + off, 0, 0))
    return cur, halo, cw, cb


def _ffn_act(hf, conv_w, conv_b):
    tm = _tile(SEQ, FFN_TM, SUBLANES)
    cur, halo, cw, cb = _ffn_specs(tm)

    def body(v_ref, vh_ref, g_ref, gh_ref, wv_ref, wg_ref, bv_ref, bg_ref, o_ref):
        live = pl.program_id(1) > 0
        vh = jnp.where(live, vh_ref[...], 0.0)
        gh = jnp.where(live, gh_ref[...], 0.0)
        val, _, _ = _conv_fwd(v_ref[...], vh, wv_ref, bv_ref)
        gate, _, _ = _conv_fwd(g_ref[...], gh, wg_ref, bg_ref)
        o_ref[...] = _gate(val, gate).astype(o_ref.dtype)

    return pl.pallas_call(
        body, name="ffn_act", grid=(2, SEQ // tm),
        in_specs=[cur(0), halo(0), cur(2), halo(2), cw(0), cw(2), cb(0), cb(2)],
        out_specs=pl.BlockSpec((None, tm, FS), lambda p, i: (p, i, 0)),
        out_shape=_sds((2, SEQ, FS), MM_DTYPE), compiler_params=_cparams(("parallel", "parallel")))(
            hf, hf, hf, hf, conv_w, conv_w, conv_b, conv_b)


def _ffn_act_bwd(hf, conv_w, conv_b, dact, dep=None):
    tm = _tile(SEQ, FFN_TM, SUBLANES)
    nb, nblk = tm // HALO, SEQ // tm
    cur, halo, cw, cb = _ffn_specs(tm)
    nxt = lambda off: pl.BlockSpec((None, HALO, FS), lambda p, i: (p + off, jnp.minimum((i + 1) * nb, SEQ // HALO - 1), 0))
    deps = _as_list(dep)

    def body(v_ref, vh_ref, vn_ref, g_ref, gh_ref, gn_ref, wv_ref, wg_ref, bv_ref, bg_ref, da_ref, dan_ref, *rest):
        dh_ref, dw_ref, dbias_ref = rest[-3:]
        dwv_ref, dwg_ref = dw_ref.at[0], dw_ref.at[1]
        dbv_ref, dbg_ref = dbias_ref.at[0], dbias_ref.at[1]
        i = pl.program_id(1)
        live, more = i > 0, i < nblk - 1
        vh = jnp.where(live, vh_ref[...], 0.0)
        gh = jnp.where(live, gh_ref[...], 0.0)
        vcur, gcur = v_ref[...], g_ref[...]
        val, v1, v2 = _conv_fwd(vcur, vh, wv_ref, bv_ref)
        gate, g1, g2 = _conv_fwd(gcur, gh, wg_ref, bg_ref)
        _, vjp = jax.vjp(_gate, val, gate)
        dval, dgate = vjp(da_ref[...])
        val_n, _, _ = _conv_fwd(vn_ref[...], vcur[tm - HALO:, :], wv_ref, bv_ref)
        gate_n, _, _ = _conv_fwd(gn_ref[...], gcur[tm - HALO:, :], wg_ref, bg_ref)
        _, vjp_n = jax.vjp(_gate, val_n, gate_n)
        dval_n, dgate_n = vjp_n(dan_ref[...])
        row = lax.broadcasted_iota(jnp.int32, dval.shape, 0)
        for k, (d, dn, w_ref) in enumerate(((dval, dval_n, wv_ref), (dgate, dgate_n, wg_ref))):
            dn = jnp.where(more, dn, 0.0)
            d1 = jnp.where(row == tm - 1, dn[0:1, :], pltpu.roll(d, tm - 1, 0))
            d2 = jnp.where(row == tm - 1, dn[1:2, :], jnp.where(row == tm - 2, dn[0:1, :], pltpu.roll(d, tm - 2, 0)))
            dh_ref[k] = (d * w_ref[2:3, :] + d1 * w_ref[1:2, :] + d2 * w_ref[0:1, :]).astype(dh_ref.dtype)

        @pl.when(i == 0)
        def _():
            dw_ref[...] = jnp.zeros_like(dw_ref)
            dbias_ref[...] = jnp.zeros_like(dbias_ref)

        for d, taps, dwk_ref, dbk_ref in ((dval, (v2, v1, vcur), dwv_ref, dbv_ref), (dgate, (g2, g1, gcur), dwg_ref, dbg_ref)):
            for k in range(CONV_WIDTH):
                dwk_ref[k:k + 1, :] += jnp.sum(d * taps[k], axis=0, keepdims=True)
            dbk_ref[...] += jnp.sum(d, axis=0, keepdims=True)

    return pl.pallas_call(
        body, name="ffn_act_bwd", grid=(2, SEQ // tm),
        in_specs=[cur(0), halo(0), nxt(0), cur(2), halo(2), nxt(2), cw(0), cw(2), cb(0), cb(2),
                  pl.BlockSpec((None, tm, FS), lambda p, i: (p, i, 0)), nxt(0)] + [ANY] * len(deps),
        out_specs=[pl.BlockSpec((None, 2, tm, FS), lambda p, i: (p, 0, i, 0)),
                   pl.BlockSpec((None, 2, CONV_WIDTH, FS), lambda p, i: (p, 0, 0, 0)),
                   pl.BlockSpec((None, 2, 1, FS), lambda p, i: (p, 0, 0, 0))],
        out_shape=[_sds((2, 2, SEQ, FS), MM_DTYPE), _sds((2, 2, CONV_WIDTH, FS)), _sds((2, 2, 1, FS))],
        compiler_params=_cparams(("parallel", "arbitrary")))(hf, hf, hf, hf, hf, hf, conv_w, conv_w, conv_b, conv_b, dact,
                                                              dact, *deps)


def _loss(y, target):
    tm = _tile(SEQ, 256)

    def body(y_ref, t_ref, dy_ref, l_ref):
        err = y_ref[...] - t_ref[...]
        dy_ref[...] = err * (1.0 / D_MODEL)

        @pl.when(pl.program_id(0) == 0)
        def _():
            l_ref[...] = jnp.zeros_like(l_ref)

        l_ref[...] += 0.5 * jnp.sum(jnp.mean(jnp.square(err), axis=-1))

    blk = pl.BlockSpec((tm, D_MODEL), lambda i: (i, 0))
    return pl.pallas_call(body, name="loss", grid=(SEQ // tm,), in_specs=[blk, blk],
                          out_specs=[blk, pl.BlockSpec((SUBLANES, LANES), lambda i: (0, 0))],
                          out_shape=[_sds((SEQ, D_MODEL)), _sds((SUBLANES, LANES))],
                          compiler_params=_cparams(("arbitrary",)))(y, target)


ADAM_BLOCK_BYTES = 3 << 19
ELEMENTWISE_COLS = 1024


def _adamw_math(w, g, m, v):
    nm = ADAM_B1 * m + (1.0 - ADAM_B1) * g
    nv = ADAM_B2 * v + (1.0 - ADAM_B2) * jnp.square(g)
    m_hat = nm / (1.0 - ADAM_B1 ** ADAM_STEP)
    v_hat = nv / (1.0 - ADAM_B2 ** ADAM_STEP)
    return -ADAM_LR * (m_hat / (jnp.sqrt(v_hat) + ADAM_EPS) + ADAM_WD * w), nm, nv


def _adamw(name, w, g, m, v):
    r, c = w.shape
    tr = _tile(r, max(SUBLANES, ADAM_BLOCK_BYTES // (4 * c)), SUBLANES)

    def body(w_ref, g_ref, m_ref, v_ref, d_ref, nm_ref, nv_ref):
        d_ref[...], nm_ref[...], nv_ref[...] = _adamw_math(w_ref[...], g_ref[...], m_ref[...], v_ref[...])

    blk = pl.BlockSpec((tr, c), lambda i: (i, 0))
    return pl.pallas_call(body, name=name, grid=(r // tr,), in_specs=[blk] * 4, out_specs=[blk] * 3,
                          out_shape=[_sds((r, c))] * 3, compiler_params=_cparams(("parallel",)))(w, g, m, v)


def _adamw_big(name, l, c_idx, w, m, v, g_own, g_got, prev):
    depth, _, r, c = w.shape
    tc = _tile(c, ELEMENTWISE_COLS)
    tr = _tile(r, max(SUBLANES, ADAM_BLOCK_BYTES // (4 * tc)), SUBLANES)

    def body(c_ref, w_ref, m_ref, v_ref, own_ref, got_ref, *rest):
        g_ref, d_ref, nm_ref, nv_ref = rest[-4:]
        g = jnp.where(pl.program_id(0) == c_ref[0], own_ref[...], got_ref[...])
        g_ref[...] = g
        d_ref[...], nm_ref[...], nv_ref[...] = _adamw_math(w_ref[...], g, m_ref[...], v_ref[...])

    stacked = pl.BlockSpec((None, None, tr, tc), lambda h, i, j, cr: (l, h, i, j))
    own = pl.BlockSpec((tr, tc), lambda h, i, j, cr: (jnp.where(h == cr[0], i, 0), jnp.where(h == cr[0], j, 0)))
    got = pl.BlockSpec((tr, tc), lambda h, i, j, cr: (jnp.where(h == cr[0], 0, i), jnp.where(h == cr[0], 0, j)))
    grid_spec = pltpu.PrefetchScalarGridSpec(
        num_scalar_prefetch=1, grid=(2, r // tr, c // tc),
        in_specs=[stacked] * 3 + [own, got] + ([ANY] * 4 if prev else []), out_specs=[stacked] * 4)
    return pl.pallas_call(
        body, name=name, grid_spec=grid_spec, out_shape=[_sds((depth, 2, r, c))] * 4,
        input_output_aliases={6 + k: k for k in range(4)} if prev else {},
        compiler_params=_cparams(("arbitrary", "arbitrary", "arbitrary")))(c_idx, w, m, v, g_own, g_got, *(prev or ()))


ANY = pl.BlockSpec(memory_space=pl.ANY)


def _place():
    x, y, c = lax.axis_index("x"), lax.axis_index("y"), lax.axis_index("c")
    chips = [(1 - x, y), (x, 1 - y), (1 - x, 1 - y)]
    return x, y, c, chips


def _cast_place(name, w, l, me_idx, out_dtype):
    _, _, r, c = w.shape
    tr = _tile(r, max(2 * SUBLANES, COPY_BLOCK_BYTES // (4 * c)), 2 * SUBLANES)

    def body(me_ref, w_ref, o_ref):
        o_ref[...] = w_ref[...].astype(o_ref.dtype)

    grid_spec = pltpu.PrefetchScalarGridSpec(
        num_scalar_prefetch=1, grid=(2, r // tr),
        in_specs=[pl.BlockSpec((None, None, tr, c), lambda h, i, me: (l, h, i, 0))],
        out_specs=pl.BlockSpec((None, None, tr, c), lambda h, i, me: (me[0], h, i, 0)))
    return pl.pallas_call(body, name=name, grid_spec=grid_spec, out_shape=_sds((N_CHIPS, 2, r, c), out_dtype),
                          compiler_params=_cparams(("parallel", "parallel")))(me_idx, w)


HBM = pl.BlockSpec(memory_space=pltpu.HBM)
SEM = pl.BlockSpec(memory_space=pltpu.SEMAPHORE)
TOKEN = (SUBLANES, LANES)


def _comm_call(name, body, hbm, sems_in=(), after=None, sems_out=(), token=False):
    n, k = len(hbm), len(sems_out)
    ins = [pltpu.with_memory_space_constraint(a, pltpu.HBM) for a in hbm] + list(sems_in)
    in_specs = [HBM] * n + [SEM] * len(sems_in)
    if after is not None:
        ins.append(after)
        in_specs.append(ANY)
    out_shape = [pltpu.SemaphoreType.DMA((s,)) for s in sems_out] + [pltpu.HBM(a.shape, a.dtype) for a in hbm]
    out_specs = [SEM] * k + [HBM] * n
    if token:
        out_shape.append(_sds(TOKEN))
        out_specs.append(pl.BlockSpec(memory_space=pltpu.VMEM))
    res = pl.pallas_call(
        body, name=name, in_specs=in_specs, out_specs=out_specs, out_shape=out_shape,
        input_output_aliases={i: k + i for i in range(n)},
        compiler_params=pltpu.CompilerParams(has_side_effects=pltpu.SideEffectType.DATAFLOW_SIDE_EFFECTING))(*ins)
    return list(res[:k]), list(res[k:k + n]), (res[k + n] if token else None)


def _remote(src, dst, send, recv, to):
    return pltpu.make_async_remote_copy(src_ref=src, dst_ref=dst, send_sem=send, recv_sem=recv, device_id=to,
                                        device_id_type=MESH)


def _gather_start(name, bufs, after=None, pair=False):
    n = len(bufs)
    o = n + (after is not None)

    def body(*refs):
        ins, (send, recv), token = refs[:n], refs[o:o + 2], refs[-1]
        x, y, c, chips = _place()
        for i in range(n):
            mine = ins[i].at[2 * x + y, c]
            for k, chip in enumerate(chips):
                _remote(mine, mine, send.at[3 * i + k], recv.at[3 * i + k], (*chip, c)).start()
            if pair:
                _remote(mine, mine, send.at[3 * n + i], recv.at[3 * n + i], (x, y, 1 - c)).start()
        token[...] = jnp.zeros(TOKEN, F32)

    n_sems = (3 + pair) * n
    return _comm_call(name, body, bufs, after=after, sems_out=(n_sems, n_sems), token=True)


def _gather_forward(name, bufs, sems, after, pair=False):
    n = len(bufs)
    o = n + 2 + (after is not None)

    def body(*refs):
        ins, (send, recv), (send2, recv2), token = refs[:n], refs[n:n + 2], refs[o:o + 2], refs[-1]
        x, y, c, chips = _place()
        for i in range(n):
            mine = ins[i].at[2 * x + y, c]
            for k, chip in enumerate(chips):
                land = ins[i].at[2 * chip[0] + chip[1], c]
                first = _remote(mine, land, send.at[3 * i + k], recv.at[3 * i + k], (*chip, c))
                first.wait_send()
                first.wait_recv()
                _remote(land, land, send2.at[3 * i + k], recv2.at[3 * i + k], (x, y, 1 - c)).start()
            if pair:
                own = _remote(mine, ins[i].at[2 * x + y, 1 - c], send.at[3 * n + i], recv.at[3 * n + i], (x, y, 1 - c))
                own.wait_send()
                own.wait_recv()
        token[...] = jnp.zeros(TOKEN, F32)

    return _comm_call(name, body, bufs, sems_in=sems, after=after, sems_out=(3 * n, 3 * n), token=True)


def _gather_finish(name, bufs, sems, after):
    n = len(bufs)

    def body(*refs):
        ins, (send, recv) = refs[:n], refs[n:n + 2]
        x, y, c, chips = _place()
        for i in range(n):
            for k, chip in enumerate(chips):
                idx = 2 * chip[0] + chip[1]
                cp = _remote(ins[i].at[idx, c], ins[i].at[idx, 1 - c], send.at[3 * i + k], recv.at[3 * i + k], (x, y, 1 - c))
                cp.wait_send()
                cp.wait_recv()

    return _comm_call(name, body, bufs, sems_in=sems, after=after)[1]


def _swap_start(name, grads):
    n = len(grads)
    lands = [lax.empty((g.shape[0],) + g.shape[2:], g.dtype) for g in grads]

    def body(*refs):
        ins, lnd, (send, recv), token = refs[:n], refs[n:2 * n], refs[2 * n:2 * n + 2], refs[-1]
        x, y, c, _ = _place()
        for i in range(n):
            _remote(ins[i].at[:, 1 - c], lnd[i], send.at[i], recv.at[i], (x, y, 1 - c)).start()
        token[...] = jnp.zeros(TOKEN, F32)

    return _comm_call(name, body, list(grads) + lands, sems_out=(n, n), token=True)


def _swap_wait(name, hbm, sems, after):
    n = len(hbm) // 2

    def body(*refs):
        ins, lnd, (send, recv) = refs[:n], refs[n:2 * n], refs[2 * n:2 * n + 2]
        x, y, c, _ = _place()
        for i in range(n):
            cp = _remote(ins[i].at[:, 1 - c], lnd[i], send.at[i], recv.at[i], (x, y, 1 - c))
            cp.wait_send()
            cp.wait_recv()

    out = _comm_call(name, body, hbm, sems_in=sems, after=after)[1]
    return out[:n], out[n:]


def _pair_add(name, g, got, cm_idx):
    nk, _, r, c = g.shape
    tr = _tile(r, max(2 * SUBLANES, COPY_BLOCK_BYTES // (4 * c)), 2 * SUBLANES)

    def body(cm_ref, g_ref, x_ref, o_ref, land_ref):
        s = (g_ref[...] + x_ref[...]).astype(o_ref.dtype)
        o_ref[...] = s

        @pl.when(pl.program_id(1) == cm_ref[1])
        def _():
            land_ref[...] = s

    grid_spec = pltpu.PrefetchScalarGridSpec(
        num_scalar_prefetch=1, grid=(r // tr, nk),
        in_specs=[pl.BlockSpec((None, None, tr, c), lambda i, k, cm: (k, cm[0], i, 0)),
                  pl.BlockSpec((None, tr, c), lambda i, k, cm: (k, i, 0))],
        out_specs=[pl.BlockSpec((None, tr, c), lambda i, k, cm: (k, i, 0)),
                   pl.BlockSpec((None, tr, c), lambda i, k, cm: (cm[1], i, 0))])
    return pl.pallas_call(body, name=name, grid_spec=grid_spec, out_shape=[_sds((nk, r, c), BF16)] * 2,
                          compiler_params=_cparams(("parallel", "arbitrary")))(cm_idx, g, got)


def _scatter_start(name, parts, lands):
    n = len(parts)

    def body(*refs):
        ins, lnd, (send, recv), token = refs[:n], refs[n:2 * n], refs[2 * n:2 * n + 2], refs[-1]
        x, y, c, chips = _place()
        for i in range(n):
            for k, chip in enumerate(chips):
                _remote(ins[i].at[2 * chip[0] + chip[1]], lnd[i].at[2 * x + y], send.at[3 * i + k], recv.at[3 * i + k],
                        (*chip, c)).start()
        token[...] = jnp.zeros(TOKEN, F32)

    return _comm_call(name, body, list(parts) + list(lands), sems_out=(3 * n, 3 * n), token=True)


def _scatter_wait(name, hbm, sems, after):
    n = len(hbm) // 2

    def body(*refs):
        ins, lnd, (send, recv) = refs[:n], refs[n:2 * n], refs[2 * n:2 * n + 2]
        x, y, c, chips = _place()
        for i in range(n):
            for k, chip in enumerate(chips):
                idx = 2 * chip[0] + chip[1]
                cp = _remote(ins[i].at[idx], lnd[i].at[idx], send.at[3 * i + k], recv.at[3 * i + k], (*chip, c))
                cp.wait_send()
                cp.wait_recv()

    out = _comm_call(name, body, hbm, sems_in=sems, after=after)[1]
    return out[:n], out[n:]


def _sum_leading(name, x, out_dtype=F32):
    nk, r, c = x.shape
    tc = _tile(c, ELEMENTWISE_COLS)
    tr = _tile(r, max(2 * SUBLANES, COPY_BLOCK_BYTES // (nk * tc * x.dtype.itemsize)), 2 * SUBLANES)

    def body(x_ref, o_ref):
        acc = x_ref[0].astype(F32)
        for k in range(1, nk):
            acc = acc + x_ref[k].astype(F32)
        o_ref[...] = acc.astype(o_ref.dtype)

    return pl.pallas_call(body, name=name, grid=(r // tr, c // tc),
                          in_specs=[pl.BlockSpec((nk, tr, tc), lambda i, j: (0, i, j))],
                          out_specs=pl.BlockSpec((tr, tc), lambda i, j: (i, j)), out_shape=_sds((r, c), out_dtype),
                          compiler_params=_cparams(("parallel", "parallel")))(x)


def _exchange_start(name, halves):
    n = len(halves)
    lands = [lax.empty(h.shape, h.dtype) for h in halves]

    def body(*refs):
        ins, lnd, (send, recv), token = refs[:n], refs[n:2 * n], refs[2 * n:2 * n + 2], refs[-1]
        x, y, c, _ = _place()
        for i in range(n):
            _remote(ins[i], lnd[i], send.at[i], recv.at[i], (x, y, 1 - c)).start()
        token[...] = jnp.zeros(TOKEN, F32)

    return _comm_call(name, body, list(halves) + lands, sems_out=(n, n), token=True)


def _exchange_wait(name, hbm, sems, after):
    n = len(hbm) // 2

    def body(*refs):
        ins, lnd, (send, recv) = refs[:n], refs[n:2 * n], refs[2 * n:2 * n + 2]
        x, y, c, _ = _place()
        for i in range(n):
            cp = _remote(ins[i], lnd[i], send.at[i], recv.at[i], (x, y, 1 - c))
            cp.wait_send()
            cp.wait_recv()

    out = _comm_call(name, body, hbm, sems_in=sems, after=after)[1]
    return out[:n], out[n:]


SMALL = ("attn_sinks", "pool_w", "pool_scale", "ssm_lam_re", "ssm_lam_im", "ssm_log_dt", "ssm_b_re", "ssm_b_im",
         "ssm_c_re", "ssm_c_im", "ssm_d", "ln1_g", "ln1_b", "ffn_conv_b", "ln2_g", "ln2_b")
BIG = ("w_in", "ssm_glu_w", "w_out", "ffn_w_up", "ffn_conv_w", "ffn_w_down")
ALL_W = ("w_in", "attn_sinks", "pool_w", "pool_scale", "ssm_lam_re", "ssm_lam_im", "ssm_log_dt", "ssm_b_re", "ssm_b_im",
         "ssm_c_re", "ssm_c_im", "ssm_d", "ssm_glu_w", "w_out", "ln1_g", "ln1_b", "ffn_w_up", "ffn_conv_w", "ffn_conv_b",
         "ffn_w_down", "ln2_g", "ln2_b")
PACK_UNIT = SUBLANES * LANES


def _padded(n):
    return -(-n // PACK_UNIT) * PACK_UNIT


def _pack(arrs):
    cols = []
    for name in SMALL:
        a = arrs[name].reshape(DEPTH, -1)
        cols.append(jnp.pad(a, ((0, 0), (0, _padded(a.shape[1]) - a.shape[1]))))
    return jnp.concatenate(cols, axis=1).reshape(-1, LANES)


def _unpack(packed, shapes):
    flat = packed.reshape(DEPTH, -1)
    out, off = {}, 0
    for name in SMALL:
        n = math.prod(shapes[name][1:])
        out[name] = flat[:, off:off + n].reshape(shapes[name])
        off += _padded(n)
    return out


def _b_rows(b):
    return b.transpose(2, 0, 1).reshape(SSM_GROUP, SSM_CH)


def _b_unrows(b):
    return b.reshape(SSM_GROUP, SSM_N_GROUPS, SSM_STATE).transpose(1, 2, 0)


def _block_diag_in(bb):
    eye = jnp.eye(SSM_N_GROUPS, dtype=F32)
    b3 = bb.reshape(SSM_GROUP, SSM_N_GROUPS, SSM_STATE)
    return jnp.einsum("hgp,gk->ghkp", b3, eye).reshape(SSM_WIDTH, SSM_CH)


def _c_unrows(c):
    return c.reshape(SSM_GROUP, SSM_N_GROUPS, SSM_STATE).transpose(1, 0, 2)


def _block_diag_out(cc):
    eye = jnp.eye(SSM_N_GROUPS, dtype=F32)
    return jnp.einsum("ghp,gk->gpkh", cc, eye).reshape(SSM_CH, SSM_WIDTH)


def _rows_layout(re, im):
    n = re.shape[1]
    return jnp.stack([re.reshape(SCAN_NB, SCAN_CW, n), im.reshape(SCAN_NB, SCAN_CW, n)], axis=1).reshape(2 * SSM_CH, n)


H_POOL0 = ATTN_WIDTH + 2 * KV_WIDTH
H_SSM0 = H_POOL0 + POOL_WIDTH


def _ssm_params(p):
    lr = p["ssm_lam_re"].reshape(1, SSM_CH)
    li = p["ssm_lam_im"].reshape(1, SSM_CH)
    ldt = jnp.repeat(p["ssm_log_dt"], SSM_STATE).reshape(1, SSM_CH)
    return lr, li, ldt, _b_rows(p["ssm_b_re"]), _b_rows(p["ssm_b_im"])


def _layer_fwd(x, xb, p, wg, rope_t, dep, pre, mid):
    cos_t, sin_t = rope_t
    h = _mm_shard_cols("in_proj", xb, wg["w_in"], dep=dep)
    qk = _rope("rope_fwd", h, 0, Q_TILES + KV_TILES, cos_t, sin_t, MM_DTYPE)
    y_attn, y_attn_b = _attn_fwd(qk, h, p["attn_sinks"])
    y_pool = _pool_fwd(h, p["pool_w"], p["pool_scale"].reshape(1, POOL_WIDTH))
    ssm_in = _ssm_params(p)
    ar, ai, bbr, bbi = _ssm_prep(*ssm_in)
    bd = _scan_layout(_block_diag_in(bbr), _block_diag_in(bbi)).astype(MM_DTYPE)
    cc = _rows_layout(_block_diag_out(p["ssm_c_re"]), -_block_diag_out(p["ssm_c_im"])).astype(MM_DTYPE)
    dvec = p["ssm_d"].reshape(1, SSM_WIDTH)
    up = _time_permute(h[:, H_SSM0:])
    xx = _mm_nn("ssm_bu", up, bd, tn=1024)
    ss = _ssm_scan("ssm_scan_fwd", _scan_layout(ar, ai), xx, False)
    yp = _mm_nn("ssm_cs", ss, cc, tk=1024)
    yf, gy = _ssm_gelu(yp, up, dvec)
    ab = _mm_shard_cols("ssm_glu_proj", gy, wg["ssm_glu_w"])
    y_ssm = _time_unpermute(_ssm_glu(ab))
    mix = jnp.concatenate([y_attn_b, y_pool, y_ssm], axis=1)
    mixo = _mm_nn("out_proj", mix, wg["w_out"].reshape(MIX_WIDTH, D_MODEL), dep=pre(mix))
    r1, x1, x1b = _ln_fwd("ln1_fwd", x, mixo, p["ln1_g"].reshape(1, D_MODEL), p["ln1_b"].reshape(1, D_MODEL))
    tokens = mid(x1b)
    hf = _ffn_up(x1b, wg["ffn_w_up"], dep=tokens)
    conv_b = p["ffn_conv_b"].reshape(N_CHIPS, 1, FS)
    act = _ffn_act(hf, wg["ffn_conv_w"], conv_b)
    f = _ffn_down(act, wg["ffn_w_down"].reshape(2, FS, D_MODEL))
    r2, x2, x2b = _ln_fwd("ln2_fwd", x1, f, p["ln2_g"].reshape(1, D_MODEL), p["ln2_b"].reshape(1, D_MODEL))
    saved = dict(xb=xb, h=h, qk=qk, y_attn=y_attn, ssm_in=ssm_in, ar=ar, ai=ai, bd=bd, cc=cc, dvec=dvec, up=up, ss=ss, yf=yf,
                 gy=gy, ab=ab, mix=mix, r1=r1, x1b=x1b, hf=hf, conv_b=conv_b, act=act, r2=r2)
    return x2, x2b, saved


def _layer_bwd(da, db, p, wg, sv, rope_t, run, start):
    cos_t, sin_t = rope_t
    small = {}
    dr2, dr2b, dg, dbeta = _ln_bwd("ln2_bwd" if db is not None else "ln2_bwd_last", sv["r2"], p["ln2_g"].reshape(1, D_MODEL),
                                   p["ln2_b"].reshape(1, D_MODEL), da, db, dep=run("h0", None))
    small["ln2_g"], small["ln2_b"] = dg, dbeta
    w_down = wg["ffn_w_down"].reshape(2, FS, D_MODEL)
    dact = _ffn_down_dact(dr2b, w_down)
    dw_down = _ffn_down_dw(sv["act"], dr2b)
    dh_ffn, dcw, dcb = _ffn_act_bwd(sv["hf"], wg["ffn_conv_w"], sv["conv_b"], dact, dep=run("h1", dw_down))
    dconv_w = dcw.transpose(1, 0, 2, 3).reshape(N_CHIPS, CONV_WIDTH, FS)
    small["ffn_conv_b"] = dcb.transpose(1, 0, 2, 3)
    dw_up = _ffn_up_dw(sv["x1b"], dh_ffn)
    tok = [start("ffn", {"ffn_w_up": dw_up, "ffn_conv_w": dconv_w,
                         "ffn_w_down": dw_down.reshape(N_CHIPS, FS // 2, D_MODEL)})] + run("h2", dw_up)
    dx1_ffn = _ffn_up_dx(dh_ffn, wg["ffn_w_up"], dep=tok)
    dr1, dr1b, dg, dbeta = _ln_bwd("ln1_bwd", sv["r1"], p["ln1_g"].reshape(1, D_MODEL), p["ln1_b"].reshape(1, D_MODEL), dr2,
                                   dx1_ffn, dep=tok)
    small["ln1_g"], small["ln1_b"] = dg, dbeta
    w_out = wg["w_out"].reshape(MIX_WIDTH, D_MODEL)
    dw_out = _mm_tn("out_proj_dw", sv["mix"], dr1b)
    dmix = _mm_nt("out_proj_dx", dr1b, w_out, dep=run("h3", dw_out))
    dq, dkc, dkp, dvc, dvp, dsk = _attn_bwd(sv["qk"], sv["h"], p["attn_sinks"], sv["y_attn"], dmix, 0)
    small["attn_sinks"] = dsk[:, :, 0]
    dh_attn = _attn_dh(dq, dkc, dkp, dvc, dvp, cos_t, -sin_t)
    dh_pool, dpw, dps = _pool_bwd(sv["h"], p["pool_w"], p["pool_scale"].reshape(1, POOL_WIDTH), dmix, ATTN_WIDTH // POOL_WIDTH)
    small["pool_w"], small["pool_scale"] = dpw, dps
    dout_p = _time_permute(dmix[:, ATTN_WIDTH + POOL_WIDTH:])
    dab = _ssm_glu_bwd(sv["ab"], dout_p)
    dgy = _mm_shard_cols_nt("ssm_glu_dx", dab, wg["ssm_glu_w"])
    dw_glu = _mm_shard_cols_tn("ssm_glu_dw", sv["gy"], dab, N_CHIPS)
    dyf, du1, dd = _ssm_gelu_bwd(sv["yf"], dgy, sv["up"], sv["dvec"])
    small["ssm_d"] = dd
    dss = _mm_nt("ssm_cs_dx", dyf, sv["cc"], tn=1024)
    dcre, dcim = _scan_unlayout(_ssm_diag("ssm_c_diag", _mm_tn("ssm_cs_dw", dyf, sv["ss"], tn=1024)))
    small["ssm_c_re"], small["ssm_c_im"] = _c_unrows(dcre), -_c_unrows(dcim)
    gg, da8 = _ssm_scan("ssm_scan_bwd", _scan_layout(sv["ar"], -sv["ai"]), dss, True, sv["ss"])
    du2 = _mm_nt("ssm_bu_dx", gg, sv["bd"], tk=1024)
    dbbr, dbbi = _scan_unlayout(_ssm_diag("ssm_b_diag", _mm_tn("ssm_bu_dw", sv["up"], gg, tn=1024)))
    dar8, dai8 = _scan_unlayout(da8)
    dlr, dli, dldt, dbr, dbi = _ssm_prep_bwd(*sv["ssm_in"], dar8, dai8, dbbr, dbbi)
    small["ssm_lam_re"], small["ssm_lam_im"] = dlr, dli
    small["ssm_log_dt"] = dldt.reshape(SSM_N_GROUPS, SSM_STATE).sum(axis=1)
    small["ssm_b_re"], small["ssm_b_im"] = _b_unrows(dbr), _b_unrows(dbi)
    dh_ssm = _time_unpermute(_add2("ssm_du", du1, du2, MM_DTYPE))
    dh = jnp.concatenate([dh_attn, dh_pool, dh_ssm], axis=1)
    dx_in = _mm_shard_cols_nt("in_proj_dx", dh, wg["w_in"], dep=run("h4", dh))
    dw_in = _mm_shard_cols_tn("in_proj_dw", sv["xb"], dh, N_CHIPS)
    start("mix", {"w_in": dw_in, "ssm_glu_w": dw_glu, "w_out": dw_out.reshape(N_CHIPS, MIX_WIDTH // N_CHIPS, D_MODEL)})
    return dr1, dx_in, small


CONV_PAD = 2 * SUBLANES


def _halved(name, a):
    if name == "ffn_conv_w":
        a = jnp.pad(a, ((0, 0), (0, CONV_PAD - CONV_WIDTH), (0, 0)))
    return a.reshape(a.shape[0], 2, a.shape[1] // 2, a.shape[2])


def _unhalved(name, a):
    a = a.reshape(a.shape[:-3] + (2 * a.shape[-2], a.shape[-1]))
    return a[..., :CONV_WIDTH, :] if name == "ffn_conv_w" else a


class _Reduce:
    def __init__(self, tag, grads, cm_idx):
        self.tag, self.cm_idx, self.names = tag, cm_idx, tuple(grads)
        g4 = [_halved(name, grads[name]) for name in self.names]
        self.sems, self.hbm, self.token = _swap_start("grad_swap_start_" + tag, g4)

    def swapped(self, after):
        g4, got = _swap_wait("grad_swap_wait_" + self.tag, self.hbm, self.sems, after)
        parts, lands = zip(*[_pair_add("grad_pair_add", g, x, self.cm_idx) for g, x in zip(g4, got)])
        self.sems, self.hbm, self.token = _scatter_start("grad_scatter_start_" + self.tag, parts, lands)
        return self.token

    def scattered(self, after):
        _, recv = _scatter_wait("grad_scatter_wait_" + self.tag, self.hbm, self.sems, after)
        halves = [_sum_leading("grad_chip_sum", r) for r in recv]
        self.sems, self.hbm, self.token = _exchange_start("grad_exchange_start_" + self.tag, halves)
        return self.token

    def finish(self, after):
        return _exchange_wait("grad_exchange_wait_" + self.tag, self.hbm, self.sems, after)


def kernel(x, w_in, attn_sinks, pool_w, pool_scale, ssm_lam_re, ssm_lam_im, ssm_log_dt, ssm_b_re, ssm_b_im, ssm_c_re, ssm_c_im, ssm_d, ssm_glu_w, w_out, ln1_g, ln1_b, ffn_w_up, ffn_conv_w, ffn_conv_b, ffn_w_down, ln2_g, ln2_b, loss_target, m_w_in, m_attn_sinks, m_pool_w, m_pool_scale, m_ssm_lam_re, m_ssm_lam_im, m_ssm_log_dt, m_ssm_b_re, m_ssm_b_im, m_ssm_c_re, m_ssm_c_im, m_ssm_d, m_ssm_glu_w, m_w_out, m_ln1_g, m_ln1_b, m_ffn_w_up, m_ffn_conv_w, m_ffn_conv_b, m_ffn_w_down, m_ln2_g, m_ln2_b, v_w_in, v_attn_sinks, v_pool_w, v_pool_scale, v_ssm_lam_re, v_ssm_lam_im, v_ssm_log_dt, v_ssm_b_re, v_ssm_b_im, v_ssm_c_re, v_ssm_c_im, v_ssm_d, v_ssm_glu_w, v_w_out, v_ln1_g, v_ln1_b, v_ffn_w_up, v_ffn_conv_w, v_ffn_conv_b, v_ffn_w_down, v_ln2_g, v_ln2_b):
    w = dict(w_in=w_in, attn_sinks=attn_sinks, pool_w=pool_w, pool_scale=pool_scale, ssm_lam_re=ssm_lam_re,
             ssm_lam_im=ssm_lam_im, ssm_log_dt=ssm_log_dt, ssm_b_re=ssm_b_re, ssm_b_im=ssm_b_im, ssm_c_re=ssm_c_re,
             ssm_c_im=ssm_c_im, ssm_d=ssm_d, ssm_glu_w=ssm_glu_w, w_out=w_out, ln1_g=ln1_g, ln1_b=ln1_b, ffn_w_up=ffn_w_up,
             ffn_conv_w=ffn_conv_w, ffn_conv_b=ffn_conv_b, ffn_w_down=ffn_w_down, ln2_g=ln2_g, ln2_b=ln2_b)
    m = dict(w_in=m_w_in, attn_sinks=m_attn_sinks, pool_w=m_pool_w, pool_scale=m_pool_scale, ssm_lam_re=m_ssm_lam_re,
             ssm_lam_im=m_ssm_lam_im, ssm_log_dt=m_ssm_log_dt, ssm_b_re=m_ssm_b_re, ssm_b_im=m_ssm_b_im, ssm_c_re=m_ssm_c_re,
             ssm_c_im=m_ssm_c_im, ssm_d=m_ssm_d, ssm_glu_w=m_ssm_glu_w, w_out=m_w_out, ln1_g=m_ln1_g, ln1_b=m_ln1_b,
             ffn_w_up=m_ffn_w_up, ffn_conv_w=m_ffn_conv_w, ffn_conv_b=m_ffn_conv_b, ffn_w_down=m_ffn_w_down, ln2_g=m_ln2_g,
             ln2_b=m_ln2_b)
    v = dict(w_in=v_w_in, attn_sinks=v_attn_sinks, pool_w=v_pool_w, pool_scale=v_pool_scale, ssm_lam_re=v_ssm_lam_re,
             ssm_lam_im=v_ssm_lam_im, ssm_log_dt=v_ssm_log_dt, ssm_b_re=v_ssm_b_re, ssm_b_im=v_ssm_b_im, ssm_c_re=v_ssm_c_re,
             ssm_c_im=v_ssm_c_im, ssm_d=v_ssm_d, ssm_glu_w=v_ssm_glu_w, w_out=v_w_out, ln1_g=v_ln1_g, ln1_b=v_ln1_b,
             ffn_w_up=v_ffn_w_up, ffn_conv_w=v_ffn_conv_w, ffn_conv_b=v_ffn_conv_b, ffn_w_down=v_ffn_w_down, ln2_g=v_ln2_g,
             ln2_b=v_ln2_b)
    c_pos = lax.axis_index("c").astype(jnp.int32)
    chip = (2 * lax.axis_index("x") + lax.axis_index("y")).astype(jnp.int32)
    c_idx, chip_idx, cm_idx = c_pos.reshape(1), chip.reshape(1), jnp.stack([c_pos, chip])
    rope_t = _rope_tables()
    xs = x.reshape(SEQ, D_MODEL)
    xb = xs.astype(MM_DTYPE)
    for t in (w, m, v):
        t["ffn_w_up"] = jnp.swapaxes(t["ffn_w_up"], 1, 2)
    wh, mh, vh = ({n: _halved(n, t[n]) for n in BIG} for t in (w, m, v))

    def place(l):
        return [_cast_place("place_" + n, wh[n], l, chip_idx, F32 if n == "ffn_conv_w" else MM_DTYPE) for n in BIG]

    n_mix = BIG.index("ffn_w_up")

    def gather_start(l, after):
        bufs = place(l)
        return (_gather_start("gather_start_%d_mix" % l, bufs[:n_mix], after),
                _gather_start("gather_start_%d_ffn" % l, bufs[n_mix:], after))

    def gather_forward(l, group, started, after):
        return _gather_forward("gather_forward_%d_%s" % (l, group), started[1], started[0], after)

    def gather_finish(l, group, forwarded, after):
        bufs = _gather_finish("gather_finish_%d_%s" % (l, group), forwarded[1], forwarded[0], after)
        names = BIG[:n_mix] if group == "mix" else BIG[n_mix:]
        return bufs, {n: _unhalved(n, g) for n, g in zip(names, bufs)}

    def gather_wait(l, group, started, after):
        return gather_finish(l, group, gather_forward(l, group, started, after), after)

    flight = gather_start(0, None)
    gathered, saved = [gather_wait(0, "mix", flight[0], None)[1]], []
    for l in range(DEPTH):
        nxt = {}

        def pre(after):
            if l == 0:
                return []
            nxt["forwarded"] = gather_forward(l, "ffn", flight[1], after)
            return [nxt["forwarded"][2]]

        def mid(after):
            forwarded = gather_forward(l, "ffn", flight[1], after) if l == 0 else nxt["forwarded"]
            bufs, wg_ffn = gather_finish(l, "ffn", forwarded, after)
            gathered[l].update(wg_ffn)
            if l + 1 == DEPTH:
                return []
            nxt["flight"] = gather_start(l + 1, bufs[0])
            return [nxt["flight"][0][2], nxt["flight"][1][2]]

        xs, xb, sv = _layer_fwd(xs, xb, {n: w[n][l] for n in SMALL}, gathered[l], rope_t, flight[1][2] if l == 0 else None, pre,
                                mid)
        saved.append(sv)
        if l + 1 < DEPTH:
            flight = nxt["flight"]
            gathered.append(gather_wait(l + 1, "mix", flight[0], xb)[1])
    dy, loss_tile = _loss(xs, loss_target.reshape(SEQ, D_MODEL))
    loss = lax.psum(loss_tile[0, 0], ("x", "y", "c"))

    big_out = {n: None for n in BIG}
    small_g = {n: [None] * DEPTH for n in SMALL}
    agenda = {}
    tail = []
    plan = {"ffn": (("h3", 0), ("h1", -1), ("h2", -1)), "mix": (("h1", -1), ("h3", -1), ("h4", -1))}
    tail_rank = {("ffn", 1): 0, ("mix", 0): 1, ("ffn", 2): 2, ("mix", 1): 3, ("mix", 2): 4}
    started = []

    def book(l, group, red):
        def update(after):
            names, own, got = red.names, *red.finish(after)
            for n, o, g in zip(names, own, got):
                big_out[n] = _adamw_big("adamw_" + n, l, c_idx, wh[n], mh[n], vh[n], o, g, big_out[n])
            return [big_out[names[-1]][0]] if l == 0 else []

        steps = (lambda a: [red.swapped(a)], lambda a: [red.scattered(a)], update)
        for k, ((hook, dl), step) in enumerate(zip(plan[group], steps)):
            if l + dl >= 0:
                agenda.setdefault((l + dl, hook), []).append(step)
            else:
                tail.append((tail_rank[group, k], step))

    def run_at(l):
        return lambda hook, after: [t for step in agenda.pop((l, hook), []) for t in step(after)]

    def start_at(l):
        def start(group, grads):
            red = _Reduce("%s_%d" % (group, l), grads, cm_idx)
            book(l, group, red)
            started.append(red.token)
            return red.token
        return start

    da, db, carry = dy, None, []
    for l in reversed(range(DEPTH)):
        agenda.setdefault((l, "h0"), []).append(lambda after, carry=carry: carry)
        da, db, small = _layer_bwd(da, db, {n: w[n][l] for n in SMALL}, gathered[l], saved[l], rope_t, run_at(l), start_at(l))
        for n in SMALL:
            small_g[n][l] = small[n].reshape(w[n].shape[1:])
        carry = run_at(l)("end", db) + started[-1:]
    shapes = {n: w[n].shape for n in SMALL}
    part = _pack({n: jnp.stack(small_g[n]) for n in SMALL})
    slots = lax.dynamic_update_slice(jnp.zeros((N_CHIPS, 2) + part.shape, F32), part[None, None], (chip, c_pos, 0, 0))
    small_sems, small_bufs, after = _gather_start("gather_start_small", [slots], pair=True)
    for _, step in sorted(tail, key=lambda rs: rs[0]):
        after = (step(after) or [after])[-1]
    grad_x = _ln_in_grad(da, db).reshape(x.shape)
    small_sems, small_bufs, _ = _gather_forward("gather_forward_small", small_bufs, small_sems, after, pair=True)
    small_bufs = _gather_finish("gather_finish_small", small_bufs, small_sems, grad_x)
    g_small = _sum_leading("small_grad_sum", small_bufs[0].reshape((N_DEV,) + part.shape))
    upd = _adamw("adamw_small", _pack(w), g_small, _pack(m), _pack(v))
    small_out = [_unpack(a, shapes) for a in (g_small,) + tuple(upd)]

    outs = [loss, grad_x]
    for kind in range(4):
        for n in ALL_W:
            if n in SMALL:
                outs.append(small_out[kind][n])
            else:
                o = _unhalved(n, big_out[n][kind])
                outs.append(jnp.swapaxes(o, 1, 2) if n == "ffn_w_up" else o)
    return tuple(outs)


def _ln_in_grad(dr1, dx_in):
    tm = _tile(SEQ, 512)

    def body(a_ref, b_ref, o_ref):
        o_ref[...] = DEEPNORM_ALPHA * a_ref[...] + b_ref[...]

    blk = pl.BlockSpec((tm, D_MODEL), lambda i: (i, 0))
    return pl.pallas_call(body, name="grad_x", grid=(SEQ // tm,), in_specs=[blk, blk], out_specs=blk,
                          out_shape=_sds((SEQ, D_MODEL)), compiler_params=_cparams(("parallel",)))(dr1, dx_in)
```

```python
import functools
import math

import jax
import jax.numpy as jnp
from jax import lax
from jax.experimental import pallas as pl
from jax.experimental.pallas import tpu as pltpu

F32 = jnp.float32
BF16 = jnp.bfloat16
MM_DTYPE = BF16

D_MODEL = 2048
SEQ = 2048
DEPTH = 4
D_FF = 5504
HEAD_DIM = 64
N_Q_HEADS = D_MODEL // 2 // HEAD_DIM
N_KV_HEADS = N_Q_HEADS // 4
ATTN_WIDTH = N_Q_HEADS * HEAD_DIM
KV_WIDTH = N_KV_HEADS * HEAD_DIM
ATTN_BLOCK = 128
ROPE_THETA = 10000.0
POOL_WINDOWS = (2, 4, 8, 16)
POOL_WIDTH = D_MODEL // 4
POOL_GROUP = POOL_WIDTH // len(POOL_WINDOWS)
SSM_WIDTH = D_MODEL // 4
SSM_GROUP = 16
SSM_N_GROUPS = SSM_WIDTH // SSM_GROUP
SSM_STATE = 64
SSM_CH = SSM_N_GROUPS * SSM_STATE
MIX_WIDTH = ATTN_WIDTH + POOL_WIDTH + SSM_WIDTH
IN_WIDTH = ATTN_WIDTH + 2 * KV_WIDTH + POOL_WIDTH + SSM_WIDTH
CONV_WIDTH = 3
LN_EPS = 1e-5
DEEPNORM_ALPHA = (2 * DEPTH) ** 0.25
ADAM_LR = 0.001
ADAM_B1 = 0.9
ADAM_B2 = 0.999
ADAM_EPS = 1e-08
ADAM_WD = 0.01
ADAM_STEP = 10

N_CHIPS = 4
N_DEV = 8
FS = 2 * D_FF // N_CHIPS
IN_S = IN_WIDTH // N_CHIPS
GLU_S = 2 * SSM_WIDTH // N_CHIPS
LANES = 128
SUBLANES = 8
SCAN_CW = 512
SCAN_NB = SSM_CH // SCAN_CW
SCAN_UNROLL = 4
VMEM_LIMIT = 56 * 1024 * 1024
COPY_BLOCK_BYTES = 6 * 1024 * 1024
NEG = -1e30

NN = (((1,), (0,)), ((), ()))
NT = (((1,), (1,)), ((), ()))
TN = (((0,), (0,)), ((), ()))
MESH = pl.DeviceIdType.MESH


def _tile(n, pref, mult=LANES):
    best = None
    for t in range(mult, min(n, pref) + 1, mult):
        if n % t == 0:
            best = t
    return n if best is None else best


def _cparams(sem):
    return pltpu.CompilerParams(dimension_semantics=sem, vmem_limit_bytes=VMEM_LIMIT)


def _sds(shape, dtype=F32):
    return jax.ShapeDtypeStruct(tuple(shape), dtype)


def _as_list(x):
    return [] if x is None else list(x) if isinstance(x, (list, tuple)) else [x]


def _mm(name, a, b, out_shape, grid, a_spec, b_spec, o_spec, dims, acc_shape, out_dtype=F32, dep=None):
    nk = grid[2]
    deps = _as_list(dep)

    def product(a_ref, b_ref):
        return lax.dot_general(a_ref[...].astype(MM_DTYPE), b_ref[...].astype(MM_DTYPE), dims, preferred_element_type=F32)

    def body_one(a_ref, b_ref, *rest):
        rest[-1][...] = product(a_ref, b_ref).astype(rest[-1].dtype)

    def body(a_ref, b_ref, *rest):
        o_ref, acc_ref = rest[-2:]
        k = pl.program_id(2)

        @pl.when(k == 0)
        def _():
            acc_ref[...] = product(a_ref, b_ref)

        @pl.when(k > 0)
        def _():
            acc_ref[...] += product(a_ref, b_ref)

        @pl.when(k == nk - 1)
        def _():
            o_ref[...] = acc_ref[...].astype(o_ref.dtype)

    return pl.pallas_call(
        body_one if nk == 1 else body, name=name, grid=grid, in_specs=[a_spec, b_spec] + [ANY] * len(deps),
        out_specs=o_spec, out_shape=_sds(out_shape, out_dtype),
        scratch_shapes=[] if nk == 1 else [pltpu.VMEM(acc_shape, F32)],
        compiler_params=_cparams(("parallel", "parallel", "arbitrary")))(a, b, *deps)


def _mm_nn(name, a, b, tm=2048, tn=512, tk=2048, out_dtype=F32, dep=None):
    m, kk = a.shape
    n = b.shape[1]
    tm, tn, tk = _tile(m, tm), _tile(n, tn), _tile(kk, tk)
    return _mm(name, a, b, (m, n), (m // tm, n // tn, kk // tk),
               pl.BlockSpec((tm, tk), lambda i, j, k: (i, k)), pl.BlockSpec((tk, tn), lambda i, j, k: (k, j)),
               pl.BlockSpec((tm, tn), lambda i, j, k: (i, j)), NN, (tm, tn), out_dtype, dep=dep)


def _mm_nt(name, a, b, tm=2048, tn=512, tk=2048, dep=None):
    m, kk = a.shape
    n = b.shape[0]
    tm, tn, tk = _tile(m, tm), _tile(n, tn), _tile(kk, tk)
    return _mm(name, a, b, (m, n), (m // tm, n // tn, kk // tk),
               pl.BlockSpec((tm, tk), lambda i, j, k: (i, k)), pl.BlockSpec((tn, tk), lambda i, j, k: (j, k)),
               pl.BlockSpec((tm, tn), lambda i, j, k: (i, j)), NT, (tm, tn), dep=dep)


def _mm_tn(name, a, b, tm=1024, tn=1024, ts=2048):
    s, m = a.shape
    n = b.shape[1]
    tm, tn, ts = _tile(m, tm), _tile(n, tn), _tile(s, ts)
    return _mm(name, a, b, (m, n), (m // tm, n // tn, s // ts),
               pl.BlockSpec((ts, tm), lambda i, j, k: (k, i)), pl.BlockSpec((ts, tn), lambda i, j, k: (k, j)),
               pl.BlockSpec((tm, tn), lambda i, j, k: (i, j)), TN, (tm, tn))


def _mm_shard_cols(name, a, w, tm=2048, tk=2048, dep=None):
    m, kk = a.shape
    nj, _, c = w.shape
    tm, tk = _tile(m, tm), _tile(kk, tk)
    return _mm(name, a, w, (m, nj * c), (m // tm, nj, kk // tk),
               pl.BlockSpec((tm, tk), lambda i, j, k: (i, k)), pl.BlockSpec((None, tk, c), lambda i, j, k: (j, k, 0)),
               pl.BlockSpec((tm, c), lambda i, j, k: (i, j)), NN, (tm, c), dep=dep)


def _mm_shard_cols_nt(name, d, w, tm=2048, tn=512, dep=None):
    m = d.shape[0]
    nj, n, c = w.shape
    tm, tn = _tile(m, tm), _tile(n, tn)
    return _mm(name, d, w, (m, n), (m // tm, n // tn, nj),
               pl.BlockSpec((tm, c), lambda i, j, k: (i, k)), pl.BlockSpec((None, tn, c), lambda i, j, k: (k, j, 0)),
               pl.BlockSpec((tm, tn), lambda i, j, k: (i, j)), NT, (tm, tn), dep=dep)


def _mm_shard_cols_tn(name, a, d, nj, tm=1024, ts=2048):
    s, m = a.shape
    c = d.shape[1] // nj
    tm, ts = _tile(m, tm), _tile(s, ts)
    return _mm(name, a, d, (nj, m, c), (nj, m // tm, s // ts),
               pl.BlockSpec((ts, tm), lambda j, i, k: (k, i)), pl.BlockSpec((ts, c), lambda j, i, k: (k, j)),
               pl.BlockSpec((None, tm, c), lambda j, i, k: (j, i, 0)), TN, (tm, c))


def _ffn_up(x1, w_up_t, tm=512, tk=2048, dep=None):
    s, d = x1.shape
    tm, tk = _tile(s, tm), _tile(d, tk)
    return _mm("ffn_up", x1, w_up_t, (N_CHIPS, s, FS), (N_CHIPS, s // tm, d // tk),
               pl.BlockSpec((tm, tk), lambda j, i, k: (i, k)), pl.BlockSpec((None, FS, tk), lambda j, i, k: (j, 0, k)),
               pl.BlockSpec((None, tm, FS), lambda j, i, k: (j, i, 0)), NT, (tm, FS), dep=dep)


def _ffn_down(act, w_down, tm=1024, tn=512):
    _, s, _ = act.shape
    d = w_down.shape[2]
    tm, tn = _tile(s, tm), _tile(d, tn)
    return _mm("ffn_down", act, w_down, (s, d), (s // tm, d // tn, 2),
               pl.BlockSpec((None, tm, FS), lambda i, j, k: (k, i, 0)), pl.BlockSpec((None, FS, tn), lambda i, j, k: (k, 0, j)),
               pl.BlockSpec((tm, tn), lambda i, j, k: (i, j)), NN, (tm, tn))


def _ffn_down_dact(df, w_down, tm=512, tk=2048):
    s, d = df.shape
    tm, tk = _tile(s, tm), _tile(d, tk)
    return _mm("ffn_down_dact", df, w_down, (2, s, FS), (2, s // tm, d // tk),
               pl.BlockSpec((tm, tk), lambda j, i, k: (i, k)), pl.BlockSpec((None, FS, tk), lambda j, i, k: (j, 0, k)),
               pl.BlockSpec((None, tm, FS), lambda j, i, k: (j, i, 0)), NT, (tm, FS))


def _ffn_down_dw(act, df, tn=512, ts=2048):
    _, s, _ = act.shape
    d = df.shape[1]
    tn, ts = _tile(d, tn), _tile(s, ts)
    return _mm("ffn_down_dw", act, df, (2, FS, d), (2, d // tn, s // ts),
               pl.BlockSpec((None, ts, FS), lambda p, j, k: (p, k, 0)), pl.BlockSpec((ts, tn), lambda p, j, k: (k, j)),
               pl.BlockSpec((None, FS, tn), lambda p, j, k: (p, 0, j)), TN, (FS, tn))


def _ffn_up_dx(dh, w_up_t, tm=1024, tn=1024, dep=None):
    s = dh.shape[2]
    d = w_up_t.shape[2]
    tm, tn = _tile(s, tm), _tile(d, tn)
    return _mm("ffn_up_dx", dh, w_up_t, (s, d), (s // tm, d // tn, N_CHIPS),
               pl.BlockSpec((None, None, tm, FS), lambda i, j, k: (k % 2, k // 2, i, 0)),
               pl.BlockSpec((None, FS, tn), lambda i, j, k: (k, 0, j)),
               pl.BlockSpec((tm, tn), lambda i, j, k: (i, j)), NN, (tm, tn), dep=dep)


def _ffn_up_dw(x1, dh, tn=512, ts=2048):
    s, d = x1.shape
    tn, ts = _tile(d, tn), _tile(s, ts)
    return _mm("ffn_up_dw", dh, x1, (N_CHIPS, FS, d), (N_CHIPS, d // tn, s // ts),
               pl.BlockSpec((None, None, ts, FS), lambda j, i, k: (j % 2, j // 2, k, 0)),
               pl.BlockSpec((ts, tn), lambda j, i, k: (k, i)),
               pl.BlockSpec((None, FS, tn), lambda j, i, k: (j, 0, i)), TN, (FS, tn))


def _rope_tables():
    half = HEAD_DIM // 2
    inv = ROPE_THETA ** (-jnp.arange(half, dtype=F32) / half)
    ang = jnp.arange(SEQ).astype(F32)[:, None] * inv[None, :]
    cos, sin = jnp.cos(ang), jnp.sin(ang)
    cos_t = jnp.tile(cos, (1, LANES // half))
    sin_t = jnp.tile(jnp.concatenate([-sin, sin], axis=1), (1, LANES // HEAD_DIM))
    return cos_t, sin_t


def _rotate_half(t):
    lane = lax.broadcasted_iota(jnp.int32, t.shape, 1)
    first = (lane % HEAD_DIM) < (HEAD_DIM // 2)
    return jnp.where(first, pltpu.roll(t, LANES - HEAD_DIM // 2, 1), pltpu.roll(t, HEAD_DIM // 2, 1))


def _rope(name, src, col_tile0, n_tiles, cos_t, sin_t, out_dtype):
    tm = _tile(SEQ, 512)
    assert col_tile0 % n_tiles == 0

    def body(x_ref, c_ref, s_ref, o_ref):
        cos, sin = c_ref[...], s_ref[...]
        for j in range(n_tiles):
            sl = slice(j * LANES, (j + 1) * LANES)
            t = x_ref[:, sl].astype(F32)
            o_ref[:, sl] = (t * cos + _rotate_half(t) * sin).astype(o_ref.dtype)

    wide = n_tiles * LANES
    return pl.pallas_call(
        body, name=name, grid=(SEQ // tm,),
        in_specs=[pl.BlockSpec((tm, wide), lambda i: (i, col_tile0 // n_tiles)),
                  pl.BlockSpec((tm, LANES), lambda i: (i, 0)), pl.BlockSpec((tm, LANES), lambda i: (i, 0))],
        out_specs=pl.BlockSpec((tm, wide), lambda i: (i, 0)),
        out_shape=_sds((SEQ, wide), out_dtype),
        compiler_params=_cparams(("parallel",)))(src, cos_t, sin_t)


Q_TILES = ATTN_WIDTH // LANES
KV_TILES = KV_WIDTH // LANES
Q_PER_KV_TILE = Q_TILES // KV_TILES
HEADS_PER_KV_TILE = N_Q_HEADS // KV_TILES
K_TILE0 = ATTN_WIDTH // LANES
V_TILE0 = (ATTN_WIDTH + KV_WIDTH) // LANES
N_QBLK = SEQ // ATTN_BLOCK


def _dup_half(t, which):
    lane = lax.broadcasted_iota(jnp.int32, t.shape, 1)
    r = pltpu.roll(t, HEAD_DIM, 1)
    lo = lane < HEAD_DIM
    return jnp.where(lo, t, r) if which == 0 else jnp.where(lo, r, t)


GROUP_HEADS = N_Q_HEADS // N_KV_HEADS
GROUP_ROWS = GROUP_HEADS * ATTN_BLOCK


def _attn_stack(tiles, lo):
    return jnp.concatenate([jnp.where(lo == (hs == 0), t, 0.0) for t in tiles for hs in range(2)], axis=0)


def _attn_unstack(x, j, lo):
    r = 2 * j * ATTN_BLOCK
    return jnp.where(lo, x[r:r + ATTN_BLOCK], x[r + ATTN_BLOCK:r + 2 * ATTN_BLOCK])


def _attn_group_consts(n, sink_ref, head0):
    shape = (GROUP_ROWS, 2 * ATTN_BLOCK)
    qi = lax.broadcasted_iota(jnp.int32, shape, 0) % ATTN_BLOCK
    col = lax.broadcasted_iota(jnp.int32, shape, 1)
    valid = ((col < ATTN_BLOCK) & (col <= qi)) | ((col >= ATTN_BLOCK) & (col - ATTN_BLOCK > qi) & (n > 0))
    head = lax.broadcasted_iota(jnp.int32, (GROUP_ROWS, 1), 0) // ATTN_BLOCK
    sinks = jnp.zeros((GROUP_ROWS, 1), F32)
    for i in range(GROUP_HEADS):
        sinks = jnp.where(head == i, sink_ref[head0 + i], sinks)
    return valid, sinks, head


def _attn_probs(qs, k2, valid, sinks):
    s = lax.dot_general(qs, k2, NT, preferred_element_type=F32) * HEAD_DIM ** -0.5
    s = jnp.where(valid, s, NEG)
    m = jnp.maximum(s.max(1, keepdims=True), sinks)
    p = jnp.exp(s - m)
    esink = jnp.exp(sinks - m)
    inv = 1.0 / (p.sum(1, keepdims=True) + esink)
    return p * inv, esink * inv


def _attn_kv(cur, prev, kvl):
    return jnp.concatenate([_dup_half(cur, kvl), _dup_half(prev, kvl)], axis=0).astype(MM_DTYPE)


def _attn_specs():
    blk = (ATTN_BLOCK, LANES)
    wide = (ATTN_BLOCK, Q_PER_KV_TILE * LANES)
    prev = lambda n: jnp.maximum(n - 1, 0)
    q_spec = pl.BlockSpec(wide, lambda t, n: (n, t))
    kc = pl.BlockSpec(blk, lambda t, n: (n, K_TILE0 + t))
    kp = pl.BlockSpec(blk, lambda t, n: (prev(n), K_TILE0 + t))
    vc = pl.BlockSpec(blk, lambda t, n: (n, V_TILE0 + t))
    vp = pl.BlockSpec(blk, lambda t, n: (prev(n), V_TILE0 + t))
    return q_spec, kc, kp, vc, vp, pl.BlockSpec(memory_space=pltpu.SMEM)


def _attn_fwd(qk, h, sinks):
    q_spec, kc_s, kp_s, vc_s, vp_s, smem = _attn_specs()

    def body(sink_ref, q_ref, kc_ref, kp_ref, vc_ref, vp_ref, o_ref, ob_ref):
        t, n = pl.program_id(0), pl.program_id(1)
        lo = lax.broadcasted_iota(jnp.int32, (ATTN_BLOCK, LANES), 1) < HEAD_DIM
        kc, kp = kc_ref[...].astype(F32), kp_ref[...].astype(F32)
        vc, vp = vc_ref[...], vp_ref[...]
        for kvl in range(2):
            valid, sink_rows, _ = _attn_group_consts(n, sink_ref, t * HEADS_PER_KV_TILE + kvl * GROUP_HEADS)
            tiles = [q_ref[:, a * LANES:(a + 1) * LANES].astype(F32) for a in (2 * kvl, 2 * kvl + 1)]
            qs = _attn_stack(tiles, lo).astype(MM_DTYPE)
            pn, _ = _attn_probs(qs, _attn_kv(kc, kp, kvl), valid, sink_rows)
            os_ = lax.dot_general(pn.astype(MM_DTYPE), _attn_kv(vc, vp, kvl), NN, preferred_element_type=F32)
            for j in range(2):
                a = 2 * kvl + j
                o = _attn_unstack(os_, j, lo)
                o_ref[:, a * LANES:(a + 1) * LANES] = o
                ob_ref[:, a * LANES:(a + 1) * LANES] = o.astype(ob_ref.dtype)

    return pl.pallas_call(
        body, name="attn_fwd", grid=(KV_TILES, N_QBLK),
        in_specs=[smem, q_spec, kc_s, kp_s, vc_s, vp_s], out_specs=[q_spec, q_spec],
        out_shape=[_sds((SEQ, ATTN_WIDTH)), _sds((SEQ, ATTN_WIDTH), MM_DTYPE)],
        compiler_params=_cparams(("parallel", "parallel")))(sinks, qk, qk, qk, h, h)


def _attn_bwd(qk, h, sinks, y, dy, dy_tile0, dep=None):
    deps = _as_list(dep)
    q_spec, kc_s, kp_s, vc_s, vp_s, smem = _attn_specs()
    blk = (ATTN_BLOCK, LANES)
    wide = (ATTN_BLOCK, Q_PER_KV_TILE * LANES)
    kv_out = pl.BlockSpec(blk, lambda t, n: (n, t))
    dy_spec = pl.BlockSpec(wide, lambda t, n: (n, t + dy_tile0))

    def body(sink_ref, q_ref, kc_ref, kp_ref, vc_ref, vp_ref, y_ref, dy_ref, *rest):
        dq_ref, dkc_ref, dkp_ref, dvc_ref, dvp_ref, dsk_ref = rest[-6:]
        t, n = pl.program_id(0), pl.program_id(1)
        lo = lax.broadcasted_iota(jnp.int32, blk, 1) < HEAD_DIM
        kc, kp = kc_ref[...].astype(F32), kp_ref[...].astype(F32)
        vc, vp = vc_ref[...], vp_ref[...]
        hrow = lax.broadcasted_iota(jnp.int32, (HEADS_PER_KV_TILE, LANES), 0)
        dsk = jnp.zeros((HEADS_PER_KV_TILE, LANES), F32)
        dk2, dv2 = [], []
        for kvl in range(2):
            valid, sink_rows, head = _attn_group_consts(n, sink_ref, t * HEADS_PER_KV_TILE + kvl * GROUP_HEADS)
            sls = [slice(a * LANES, (a + 1) * LANES) for a in (2 * kvl, 2 * kvl + 1)]
            qs = _attn_stack([q_ref[:, sl].astype(F32) for sl in sls], lo).astype(MM_DTYPE)
            dos = _attn_stack([dy_ref[:, sl] for sl in sls], lo)
            delta = _attn_stack([dy_ref[:, sl] * y_ref[:, sl] for sl in sls], lo).sum(1, keepdims=True)
            dos = dos.astype(MM_DTYPE)
            k2, v2 = _attn_kv(kc, kp, kvl), _attn_kv(vc, vp, kvl)
            pn, psink = _attn_probs(qs, k2, valid, sink_rows)
            dp = lax.dot_general(dos, v2, NT, preferred_element_type=F32)
            ds = (pn * (dp - delta) * HEAD_DIM ** -0.5).astype(MM_DTYPE)
            dqs = lax.dot_general(ds, k2, NN, preferred_element_type=F32)
            for j, sl in enumerate(sls):
                dq_ref[:, sl] = _attn_unstack(dqs, j, lo)
            for acc, x in ((dk2, lax.dot_general(ds, qs, TN, preferred_element_type=F32)),
                           (dv2, lax.dot_general(pn.astype(MM_DTYPE), dos, TN, preferred_element_type=F32))):
                acc.append(x + pltpu.roll(x, HEAD_DIM, 1))
            dsink = psink * delta
            for i in range(GROUP_HEADS):
                dsk = dsk + jnp.where(hrow == kvl * GROUP_HEADS + i, -jnp.sum(jnp.where(head == i, dsink, 0.0)), 0.0)
        for o_ref, src, r in ((dkc_ref, dk2, 0), (dkp_ref, dk2, ATTN_BLOCK), (dvc_ref, dv2, 0), (dvp_ref, dv2, ATTN_BLOCK)):
            o_ref[...] = jnp.where(lo, src[0][r:r + ATTN_BLOCK], src[1][r:r + ATTN_BLOCK])

        @pl.when(n == 0)
        def _():
            dsk_ref[...] = jnp.zeros_like(dsk_ref)

        dsk_ref[...] += dsk

    kv_shape = _sds((SEQ, KV_WIDTH))
    return pl.pallas_call(
        body, name="attn_bwd", grid=(KV_TILES, N_QBLK),
        in_specs=[smem, q_spec, kc_s, kp_s, vc_s, vp_s, q_spec, dy_spec] + [ANY] * len(deps),
        out_specs=[q_spec, kv_out, kv_out, kv_out, kv_out,
                   pl.BlockSpec((None, HEADS_PER_KV_TILE, LANES), lambda t, n: (t, 0, 0))],
        out_shape=[_sds((SEQ, ATTN_WIDTH)), kv_shape, kv_shape, kv_shape, kv_shape,
                   _sds((KV_TILES, HEADS_PER_KV_TILE, LANES))],
        compiler_params=_cparams(("parallel", "arbitrary")))(sinks, qk, qk, qk, h, h, y, dy, *deps)


def _attn_dh(dq, dkc, dkp, dvc, dvp, cos_t, nsin_t):
    n_tiles = Q_TILES + 2 * KV_TILES
    nxt = lambda n: jnp.minimum(n + 1, N_QBLK - 1)

    def body(dq_ref, kc_ref, kp_ref, vc_ref, vp_ref, c_ref, s_ref, o_ref):
        has_next = pl.program_id(0) < N_QBLK - 1
        cos, sin = c_ref[...], s_ref[...]

        def unrope(t):
            return t * cos + _rotate_half(t) * sin

        for j in range(Q_TILES):
            sl = slice(j * LANES, (j + 1) * LANES)
            o_ref[:, sl] = unrope(dq_ref[:, sl]).astype(o_ref.dtype)
        for j in range(KV_TILES):
            sl = slice(j * LANES, (j + 1) * LANES)
            t = kc_ref[:, sl] + jnp.where(has_next, kp_ref[:, sl], 0.0)
            o_ref[:, ATTN_WIDTH + j * LANES:ATTN_WIDTH + (j + 1) * LANES] = unrope(t).astype(o_ref.dtype)
        o_ref[:, ATTN_WIDTH + KV_WIDTH:] = (vc_ref[...] + jnp.where(has_next, vp_ref[...], 0.0)).astype(o_ref.dtype)

    qb, kb, tb = (ATTN_BLOCK, ATTN_WIDTH), (ATTN_BLOCK, KV_WIDTH), (ATTN_BLOCK, LANES)
    return pl.pallas_call(
        body, name="attn_dh", grid=(N_QBLK,),
        in_specs=[pl.BlockSpec(qb, lambda n: (n, 0)),
                  pl.BlockSpec(kb, lambda n: (n, 0)), pl.BlockSpec(kb, lambda n: (nxt(n), 0)),
                  pl.BlockSpec(kb, lambda n: (n, 0)), pl.BlockSpec(kb, lambda n: (nxt(n), 0)),
                  pl.BlockSpec(tb, lambda n: (n, 0)), pl.BlockSpec(tb, lambda n: (n, 0))],
        out_specs=pl.BlockSpec((ATTN_BLOCK, n_tiles * LANES), lambda n: (n, 0)),
        out_shape=_sds((SEQ, n_tiles * LANES), MM_DTYPE),
        compiler_params=_cparams(("parallel",)))(dq, dkc, dkp, dvc, dvp, cos_t, nsin_t)


POOL_TILE0 = (ATTN_WIDTH + 2 * KV_WIDTH) // POOL_WIDTH


def _shift_rows(x, d, down):
    n = x.shape[0]
    row = lax.broadcasted_iota(jnp.int32, x.shape, 0)
    if down:
        return jnp.where(row >= d, pltpu.roll(x, d, 0), 0.0)
    return jnp.where(row < n - d, pltpu.roll(x, n - d, 0), 0.0)


def _window_sum(x, w, down):
    d = 1
    while d < w:
        x = x + _shift_rows(x, d, down)
        d *= 2
    return x


def _pool_z(u, w):
    t = lax.broadcasted_iota(jnp.int32, u.shape, 0).astype(F32)
    cnt = jnp.minimum(t + 1.0, float(w))
    return _window_sum(u, w, True) / cnt - u, cnt


def _pool_fwd(h, pool_w, pool_scale):
    def body(u_ref, w_ref, s_ref, o_ref):
        for gi, w in enumerate(POOL_WINDOWS):
            sl = slice(gi * POOL_GROUP, (gi + 1) * POOL_GROUP)
            z, _ = _pool_z(u_ref[:, sl], w)
            o_ref[:, sl] = (lax.dot_general(z.astype(MM_DTYPE), w_ref[gi].astype(MM_DTYPE), NN,
                                            preferred_element_type=F32) * s_ref[:, sl]).astype(o_ref.dtype)

    return pl.pallas_call(
        body, name="pool_fwd", grid=(1,),
        in_specs=[pl.BlockSpec((SEQ, POOL_WIDTH), lambda i: (0, POOL_TILE0)),
                  pl.BlockSpec(pool_w.shape, lambda i: (0, 0, 0)), pl.BlockSpec((1, POOL_WIDTH), lambda i: (0, 0))],
        out_specs=pl.BlockSpec((SEQ, POOL_WIDTH), lambda i: (0, 0)),
        out_shape=_sds((SEQ, POOL_WIDTH), MM_DTYPE), compiler_params=_cparams(("arbitrary",)))(h, pool_w, pool_scale)


def _pool_bwd(h, pool_w, pool_scale, dmix, dy_tile0):
    def body(u_ref, w_ref, s_ref, dy_ref, du_ref, dw_ref, ds_ref):
        for gi, w in enumerate(POOL_WINDOWS):
            sl = slice(gi * POOL_GROUP, (gi + 1) * POOL_GROUP)
            z, cnt = _pool_z(u_ref[:, sl], w)
            zb, wb = z.astype(MM_DTYPE), w_ref[gi].astype(MM_DTYPE)
            dy = dy_ref[:, sl]
            zp = lax.dot_general(zb, wb, NN, preferred_element_type=F32)
            ds_ref[:, sl] = jnp.sum(dy * zp, axis=0, keepdims=True)
            dyo = (dy * s_ref[:, sl]).astype(MM_DTYPE)
            dw_ref[gi] = lax.dot_general(zb, dyo, TN, preferred_element_type=F32)
            dz = lax.dot_general(dyo, wb, NT, preferred_element_type=F32)
            du_ref[:, sl] = (_window_sum(dz / cnt, w, False) - dz).astype(du_ref.dtype)

    return pl.pallas_call(
        body, name="pool_bwd", grid=(1,),
        in_specs=[pl.BlockSpec((SEQ, POOL_WIDTH), lambda i: (0, POOL_TILE0)),
                  pl.BlockSpec(pool_w.shape, lambda i: (0, 0, 0)), pl.BlockSpec((1, POOL_WIDTH), lambda i: (0, 0)),
                  pl.BlockSpec((SEQ, POOL_WIDTH), lambda i: (0, dy_tile0))],
        out_specs=[pl.BlockSpec((SEQ, POOL_WIDTH), lambda i: (0, 0)), pl.BlockSpec(pool_w.shape, lambda i: (0, 0, 0)),
                   pl.BlockSpec((1, POOL_WIDTH), lambda i: (0, 0))],
        out_shape=[_sds((SEQ, POOL_WIDTH), MM_DTYPE), _sds(pool_w.shape), _sds((1, POOL_WIDTH))],
        compiler_params=_cparams(("arbitrary",)))(h, pool_w, pool_scale, dmix)


def _ssm_discretize(lr, li, ldt, br, bi):
    dt = jnp.exp(ldt)
    mag = jnp.exp(lr * dt)
    ar, ai = mag * jnp.cos(li * dt), mag * jnp.sin(li * dt)
    nr, ni = ar - 1.0, ai
    den = lr * lr + li * li
    zr = (nr * lr + ni * li) / den
    zi = (ni * lr - nr * li) / den
    return ar, ai, zr * br - zi * bi, zr * bi + zi * br


def _ssm_prep(lr, li, ldt, br, bi):
    def body(lr_ref, li_ref, ldt_ref, br_ref, bi_ref, ar_ref, ai_ref, bbr_ref, bbi_ref):
        outs = _ssm_discretize(lr_ref[...], li_ref[...], ldt_ref[...], br_ref[...], bi_ref[...])
        for o, v in zip((ar_ref, ai_ref, bbr_ref, bbi_ref), outs):
            o[...] = v

    row, mat = _sds((1, SSM_CH)), _sds((SSM_GROUP, SSM_CH))
    return pl.pallas_call(body, name="ssm_prep", out_shape=[row, row, mat, mat])(lr, li, ldt, br, bi)


def _ssm_prep_bwd(lr, li, ldt, br, bi, dar8, dai8, dbbr, dbbi):
    def body(lr_ref, li_ref, ldt_ref, br_ref, bi_ref, dar_ref, dai_ref, dbbr_ref, dbbi_ref, *outs):
        args = (lr_ref[...], li_ref[...], ldt_ref[...], br_ref[...], bi_ref[...])
        _, vjp = jax.vjp(_ssm_discretize, *args)
        cot = (jnp.sum(dar_ref[...], axis=0, keepdims=True), jnp.sum(dai_ref[...], axis=0, keepdims=True),
               dbbr_ref[...], dbbi_ref[...])
        for o, v in zip(outs, vjp(cot)):
            o[...] = v

    row, mat = _sds((1, SSM_CH)), _sds((SSM_GROUP, SSM_CH))
    return pl.pallas_call(body, name="ssm_prep_bwd", out_shape=[row, row, row, mat, mat])(
        lr, li, ldt, br, bi, dar8, dai8, dbbr, dbbi)


def _ssm_diag(name, full):
    tiles = SCAN_CW // LANES
    groups = LANES // SSM_STATE
    rows = tiles * groups * SSM_GROUP

    def body(x_ref, o_ref):
        lane = lax.broadcasted_iota(jnp.int32, (SSM_GROUP, LANES), 1)
        for q in range(2 * tiles):
            sl = slice(q * LANES, (q + 1) * LANES)
            r0 = (q % tiles) * groups * SSM_GROUP
            out = x_ref[r0:r0 + SSM_GROUP, sl]
            for k in range(1, groups):
                out = jnp.where(lane >= k * SSM_STATE, x_ref[r0 + k * SSM_GROUP:r0 + (k + 1) * SSM_GROUP, sl], out)
            o_ref[:, sl] = out

    return pl.pallas_call(
        body, name=name, grid=(SCAN_NB,),
        in_specs=[pl.BlockSpec((rows, 2 * SCAN_CW), lambda b: (b, b))],
        out_specs=pl.BlockSpec((SSM_GROUP, 2 * SCAN_CW), lambda b: (0, b)),
        out_shape=_sds((SSM_GROUP, 2 * SSM_CH)), compiler_params=_cparams(("parallel",)))(full)


def _scan_layout(re, im):
    r = re.shape[0]
    return jnp.stack([re.reshape(r, SCAN_NB, SCAN_CW), im.reshape(r, SCAN_NB, SCAN_CW)], axis=2).reshape(r, 2 * SSM_CH)


def _scan_unlayout(x):
    r = x.shape[0]
    x = x.reshape(r, SCAN_NB, 2, SCAN_CW)
    return x[:, :, 0].reshape(r, SSM_CH), x[:, :, 1].reshape(r, SSM_CH)


def _time_permute(u):
    s, c = u.shape
    return u.reshape(SUBLANES, s // SUBLANES, c).transpose(1, 0, 2).reshape(s, c)


def _time_unpermute(u):
    s, c = u.shape
    return u.reshape(s // SUBLANES, SUBLANES, c).transpose(1, 0, 2).reshape(s, c)


def _ssm_scan(name, a_vec, x, reverse, s_prev=None):
    nsteps = SEQ // SUBLANES
    cw = SCAN_CW
    with_da = s_prev is not None

    def body(a_ref, x_ref, *rest):
        if with_da:
            s_ref, o_ref, da_ref = rest
        else:
            o_ref, = rest
        ar = jnp.broadcast_to(a_ref[:, :cw], (SUBLANES, cw))
        ai = jnp.broadcast_to(a_ref[:, cw:], (SUBLANES, cw))
        seg = lax.broadcasted_iota(jnp.int32, (SUBLANES, cw), 0)

        def toward(v):
            if reverse:
                return jnp.where(seg < SUBLANES - 1, pltpu.roll(v, SUBLANES - 1, 0), 0.0)
            return jnp.where(seg >= 1, pltpu.roll(v, 1, 0), 0.0)

        def rows(j):
            jj = nsteps - 1 - j if reverse else j
            return pl.ds(pl.multiple_of(jj * SUBLANES, SUBLANES), SUBLANES)

        def cmul(pr, pi, qr, qi):
            return pr * qr - pi * qi, pr * qi + pi * qr

        def local(j, c):
            sr, si = c
            r = rows(j)
            mr, mi = cmul(ar, ai, sr, si)
            return mr + x_ref[r, :cw], mi + x_ref[r, cw:]

        zero = jnp.zeros((SUBLANES, cw), F32)
        fr, fi = lax.fori_loop(0, nsteps, local, (zero, zero), unroll=SCAN_UNROLL)

        def power(_, c):
            return cmul(ar, ai, *c)

        pr, pi = lax.fori_loop(0, nsteps - 1, power, (ar, ai))
        tr, ti = fr, fi
        for _ in range(SUBLANES - 1):
            mr, mi = cmul(pr, pi, toward(tr), toward(ti))
            tr, ti = fr + mr, fi + mi
        init = (toward(tr), toward(ti))

        def advance(j, sr, si):
            r = rows(j)
            mr, mi = cmul(ar, ai, sr, si)
            sr, si = mr + x_ref[r, :cw], mi + x_ref[r, cw:]
            o_ref[r, :cw] = sr
            o_ref[r, cw:] = si
            return sr, si

        def full(j, c):
            return advance(j, *c)

        def full_da(j, c):
            sr, si = advance(j, c[0], c[1])
            rp = pl.ds(pl.multiple_of((nsteps - 2 - j) * SUBLANES, SUBLANES), SUBLANES)
            spr, spi = s_ref[rp, :cw], s_ref[rp, cw:]
            return sr, si, c[2] + sr * spr + si * spi, c[3] + si * spr - sr * spi

        if with_da:
            sr, si, dar, dai = lax.fori_loop(0, nsteps - 1, full_da, init + (zero, zero), unroll=SCAN_UNROLL)
            sr, si = advance(nsteps - 1, sr, si)
            last = pl.ds((nsteps - 1) * SUBLANES, SUBLANES)
            spr = jnp.where(seg >= 1, pltpu.roll(s_ref[last, :cw], 1, 0), 0.0)
            spi = jnp.where(seg >= 1, pltpu.roll(s_ref[last, cw:], 1, 0), 0.0)
            da_ref[:, :cw] = dar + sr * spr + si * spi
            da_ref[:, cw:] = dai + si * spr - sr * spi
        else:
            lax.fori_loop(0, nsteps, full, init, unroll=SCAN_UNROLL)

    blk = pl.BlockSpec((SEQ, 2 * cw), lambda b: (0, b))
    a_spec = pl.BlockSpec((1, 2 * cw), lambda b: (0, b))
    in_specs, args = [a_spec, blk], [a_vec, x]
    out_specs, out_shape = blk, _sds((SEQ, 2 * SSM_CH))
    if with_da:
        in_specs, args = in_specs + [blk], args + [s_prev]
        out_specs = [blk, pl.BlockSpec((SUBLANES, 2 * cw), lambda b: (0, b))]
        out_shape = [out_shape, _sds((SUBLANES, 2 * SSM_CH))]
    return pl.pallas_call(body, name=name, grid=(SCAN_NB,), in_specs=in_specs, out_specs=out_specs,
                          out_shape=out_shape, compiler_params=_cparams(("parallel",)))(*args)


def _ssm_gelu(yp, up, dvec):
    tm = _tile(SEQ, 512)

    def body(y_ref, u_ref, d_ref, yf_ref, g_ref):
        yf = y_ref[...] + d_ref[...] * u_ref[...]
        yf_ref[...] = yf
        g_ref[...] = jax.nn.gelu(yf).astype(g_ref.dtype)

    blk = pl.BlockSpec((tm, SSM_WIDTH), lambda i: (i, 0))
    row = pl.BlockSpec((1, SSM_WIDTH), lambda i: (0, 0))
    return pl.pallas_call(body, name="ssm_gelu", grid=(SEQ // tm,), in_specs=[blk, blk, row], out_specs=[blk, blk],
                          out_shape=[_sds((SEQ, SSM_WIDTH)), _sds((SEQ, SSM_WIDTH), MM_DTYPE)],
                          compiler_params=_cparams(("parallel",)))(yp, up, dvec)


def _ssm_gelu_bwd(yf, dgy, up, dvec):
    tm = _tile(SEQ, 512)

    def body(yf_ref, dg_ref, u_ref, d_ref, dyf_ref, du_ref, dd_ref):
        _, vjp = jax.vjp(jax.nn.gelu, yf_ref[...])
        dyf, = vjp(dg_ref[...])
        dyf_ref[...] = dyf.astype(dyf_ref.dtype)
        du_ref[...] = d_ref[...] * dyf

        @pl.when(pl.program_id(0) == 0)
        def _():
            dd_ref[...] = jnp.zeros_like(dd_ref)

        dd_ref[...] += jnp.sum(dyf * u_ref[...], axis=0, keepdims=True)

    blk = pl.BlockSpec((tm, SSM_WIDTH), lambda i: (i, 0))
    row = pl.BlockSpec((1, SSM_WIDTH), lambda i: (0, 0))
    return pl.pallas_call(body, name="ssm_gelu_bwd", grid=(SEQ // tm,), in_specs=[blk, blk, blk, row],
                          out_specs=[blk, blk, row],
                          out_shape=[_sds((SEQ, SSM_WIDTH), MM_DTYPE), _sds((SEQ, SSM_WIDTH)), _sds((1, SSM_WIDTH))],
                          compiler_params=_cparams(("arbitrary",)))(yf, dgy, up, dvec)


def _glu(ab):
    return ab[:, :SSM_WIDTH] * jax.nn.sigmoid(ab[:, SSM_WIDTH:])


def _ssm_glu(ab):
    tm = _tile(SEQ, 512)

    def body(ab_ref, o_ref):
        o_ref[...] = _glu(ab_ref[...]).astype(o_ref.dtype)

    return pl.pallas_call(body, name="ssm_glu", grid=(SEQ // tm,),
                          in_specs=[pl.BlockSpec((tm, 2 * SSM_WIDTH), lambda i: (i, 0))],
                          out_specs=pl.BlockSpec((tm, SSM_WIDTH), lambda i: (i, 0)),
                          out_shape=_sds((SEQ, SSM_WIDTH), MM_DTYPE), compiler_params=_cparams(("parallel",)))(ab)


def _ssm_glu_bwd(ab, dout):
    tm = _tile(SEQ, 512)

    def body(ab_ref, do_ref, dab_ref):
        _, vjp = jax.vjp(_glu, ab_ref[...])
        dab, = vjp(do_ref[...])
        dab_ref[...] = dab.astype(dab_ref.dtype)

    return pl.pallas_call(body, name="ssm_glu_bwd", grid=(SEQ // tm,),
                          in_specs=[pl.BlockSpec((tm, 2 * SSM_WIDTH), lambda i: (i, 0)),
                                    pl.BlockSpec((tm, SSM_WIDTH), lambda i: (i, 0))],
                          out_specs=pl.BlockSpec((tm, 2 * SSM_WIDTH), lambda i: (i, 0)),
                          out_shape=_sds((SEQ, 2 * SSM_WIDTH), MM_DTYPE), compiler_params=_cparams(("parallel",)))(ab, dout)


def _add2(name, a, b, out_dtype):
    tm = _tile(a.shape[0], 512)

    def body(a_ref, b_ref, o_ref):
        o_ref[...] = (a_ref[...] + b_ref[...]).astype(o_ref.dtype)

    blk = pl.BlockSpec((tm, a.shape[1]), lambda i: (i, 0))
    return pl.pallas_call(body, name=name, grid=(a.shape[0] // tm,), in_specs=[blk, blk], out_specs=blk,
                          out_shape=_sds(a.shape, out_dtype), compiler_params=_cparams(("parallel",)))(a, b)


def _layer_norm(r, g, b):
    mu = r.mean(-1, keepdims=True)
    var = jnp.square(r - mu).mean(-1, keepdims=True)
    return (r - mu) * lax.rsqrt(var + LN_EPS) * g + b


def _ln_fwd(name, x, y, g, b):
    tm = _tile(SEQ, 256)

    def body(x_ref, y_ref, g_ref, b_ref, r_ref, o_ref, ob_ref):
        r = DEEPNORM_ALPHA * x_ref[...] + y_ref[...]
        r_ref[...] = r
        o = _layer_norm(r, g_ref[...], b_ref[...])
        o_ref[...] = o
        ob_ref[...] = o.astype(ob_ref.dtype)

    blk = pl.BlockSpec((tm, D_MODEL), lambda i: (i, 0))
    row = pl.BlockSpec((1, D_MODEL), lambda i: (0, 0))
    return pl.pallas_call(body, name=name, grid=(SEQ // tm,), in_specs=[blk, blk, row, row], out_specs=[blk, blk, blk],
                          out_shape=[_sds((SEQ, D_MODEL))] * 2 + [_sds((SEQ, D_MODEL), MM_DTYPE)],
                          compiler_params=_cparams(("parallel",)))(x, y, g, b)


def _ln_bwd(name, r, g, b, da, db=None, dep=None):
    tm = _tile(SEQ, 256)
    two = db is not None
    deps = _as_list(dep)

    def body(r_ref, g_ref, b_ref, da_ref, *rest):
        dr_ref, drb_ref, dg_ref, dbeta_ref = rest[-4:]
        dout = DEEPNORM_ALPHA * da_ref[...] + rest[0][...] if two else da_ref[...]
        _, vjp = jax.vjp(_layer_norm, r_ref[...], g_ref[...], b_ref[...])
        dr, dg, dbeta = vjp(dout)
        dr_ref[...] = dr
        drb_ref[...] = dr.astype(drb_ref.dtype)

        @pl.when(pl.program_id(0) == 0)
        def _():
            dg_ref[...] = jnp.zeros_like(dg_ref)
            dbeta_ref[...] = jnp.zeros_like(dbeta_ref)

        dg_ref[...] += dg
        dbeta_ref[...] += dbeta

    blk = pl.BlockSpec((tm, D_MODEL), lambda i: (i, 0))
    row = pl.BlockSpec((1, D_MODEL), lambda i: (0, 0))
    args = [r, g, b, da] + ([db] if two else []) + deps
    return pl.pallas_call(body, name=name, grid=(SEQ // tm,),
                          in_specs=[blk, row, row, blk] + ([blk] if two else []) + [ANY] * len(deps),
                          out_specs=[blk, blk, row, row],
                          out_shape=[_sds((SEQ, D_MODEL)), _sds((SEQ, D_MODEL), MM_DTYPE), _sds((1, D_MODEL)), _sds((1, D_MODEL))],
                          compiler_params=_cparams(("arbitrary",)))(*args)


FFN_TM = 128
HALO = SUBLANES


def _conv_taps(cur, halo):
    row = lax.broadcasted_iota(jnp.int32, cur.shape, 0)
    h1 = jnp.where(row == 0, halo[HALO - 1:HALO, :], pltpu.roll(cur, 1, 0))
    h2 = jnp.where(row == 0, halo[HALO - 2:HALO - 1, :], jnp.where(row == 1, halo[HALO - 1:HALO, :], pltpu.roll(cur, 2, 0)))
    return h1, h2


def _conv_fwd(cur, halo, w_ref, b_ref):
    h1, h2 = _conv_taps(cur, halo)
    return b_ref[...] + h2 * w_ref[0:1, :] + h1 * w_ref[1:2, :] + cur * w_ref[2:3, :], h1, h2


def _gate_bwd(val, gate, da):
    sig = jax.nn.sigmoid(gate)
    sg = gate * sig
    return da * sg, da * val * (sig + sg * (1.0 - sig))


def _gate(val, gate):
    return jax.nn.silu(gate) * val


def _ffn_specs(tm):
    nb = tm // HALO
    cur = lambda off: pl.BlockSpec((None, tm, FS), lambda p, i: (p + off, i, 0))
    halo = lambda off: pl.BlockSpec((None, HALO, FS), lambda p, i: (p + off, jnp.maximum(i * nb - 1, 0), 0))
    cw = lambda off: pl.BlockSpec((None, CONV_WIDTH, FS), lambda p, i: (p + off, 0, 0))
    cb = lambda off: pl.BlockSpec((None, 1, FS), lambda p, i: (p + off, 0, 0))
    return cur, halo, cw, cb


def _ffn_act(hf, conv_w, conv_b):
    tm = _tile(SEQ, FFN_TM, SUBLANES)
    cur, halo, cw, cb = _ffn_specs(tm)

    def body(v_ref, vh_ref, g_ref, gh_ref, wv_ref, wg_ref, bv_ref, bg_ref, o_ref):
        live = pl.program_id(1) > 0
        vh = jnp.where(live, vh_ref[...], 0.0)
        gh = jnp.where(live, gh_ref[...], 0.0)
        val, _, _ = _conv_fwd(v_ref[...], vh, wv_ref, bv_ref)
        gate, _, _ = _conv_fwd(g_ref[...], gh, wg_ref, bg_ref)
        o_ref[...] = _gate(val, gate).astype(o_ref.dtype)

    return pl.pallas_call(
        body, name="ffn_act", grid=(2, SEQ // tm),
        in_specs=[cur(0), halo(0), cur(2), halo(2), cw(0), cw(2), cb(0), cb(2)],
        out_specs=pl.BlockSpec((None, tm, FS), lambda p, i: (p, i, 0)),
        out_shape=_sds((2, SEQ, FS), MM_DTYPE), compiler_params=_cparams(("parallel", "parallel")))(
            hf, hf, hf, hf, conv_w, conv_w, conv_b, conv_b)


def _ffn_act_bwd(hf, conv_w, conv_b, dact, dep=None):
    tm = _tile(SEQ, FFN_TM, SUBLANES)
    nb, nblk = tm // HALO, SEQ // tm
    cur, halo, cw, cb = _ffn_specs(tm)
    nxt = lambda off: pl.BlockSpec((None, HALO, FS), lambda p, i: (p + off, jnp.minimum((i + 1) * nb, SEQ // HALO - 1), 0))
    deps = _as_list(dep)

    def body(v_ref, vh_ref, vn_ref, g_ref, gh_ref, gn_ref, wv_ref, wg_ref, bv_ref, bg_ref, da_ref, dan_ref, *rest):
        dh_ref, dw_ref, dbias_ref = rest[-3:]
        dwv_ref, dwg_ref = dw_ref.at[0], dw_ref.at[1]
        dbv_ref, dbg_ref = dbias_ref.at[0], dbias_ref.at[1]
        i = pl.program_id(1)
        live, more = i > 0, i < nblk - 1
        vh = jnp.where(live, vh_ref[...], 0.0)
        gh = jnp.where(live, gh_ref[...], 0.0)
        vcur, gcur = v_ref[...], g_ref[...]
        val, v1, v2 = _conv_fwd(vcur, vh, wv_ref, bv_ref)
        gate, g1, g2 = _conv_fwd(gcur, gh, wg_ref, bg_ref)
        dval, dgate = _gate_bwd(val, gate, da_ref[...])
        val_n, _, _ = _conv_fwd(vn_ref[...], vcur[tm - HALO:, :], wv_ref, bv_ref)
        gate_n, _, _ = _conv_fwd(gn_ref[...], gcur[tm - HALO:, :], wg_ref, bg_ref)
        dval_n, dgate_n = _gate_bwd(val_n, gate_n, dan_ref[...])
        row = lax.broadcasted_iota(jnp.int32, dval.shape, 0)
        for k, (d, dn, w_ref) in enumerate(((dval, dval_n, wv_ref), (dgate, dgate_n, wg_ref))):
            dn = jnp.where(more, dn, 0.0)
            d1 = jnp.where(row == tm - 1, dn[0:1, :], pltpu.roll(d, tm - 1, 0))
            d2 = jnp.where(row == tm - 1, dn[1:2, :], jnp.where(row == tm - 2, dn[0:1, :], pltpu.roll(d, tm - 2, 0)))
            dh_ref[k] = (d * w_ref[2:3, :] + d1 * w_ref[1:2, :] + d2 * w_ref[0:1, :]).astype(dh_ref.dtype)

        @pl.when(i == 0)
        def _():
            dw_ref[...] = jnp.zeros_like(dw_ref)
            dbias_ref[...] = jnp.zeros_like(dbias_ref)

        for d, taps, dwk_ref, dbk_ref in ((dval, (v2, v1, vcur), dwv_ref, dbv_ref), (dgate, (g2, g1, gcur), dwg_ref, dbg_ref)):
            for k in range(CONV_WIDTH):
                dwk_ref[k:k + 1, :] += jnp.sum(d * taps[k], axis=0, keepdims=True)
            dbk_ref[...] += jnp.sum(d, axis=0, keepdims=True)

    return pl.pallas_call(
        body, name="ffn_act_bwd", grid=(2, SEQ // tm),
        in_specs=[cur(0), halo(0), nxt(0), cur(2), halo(2), nxt(2), cw(0), cw(2), cb(0), cb(2),
                  pl.BlockSpec((None, tm, FS), lambda p, i: (p, i, 0)), nxt(0)] + [ANY] * len(deps),
        out_specs=[pl.BlockSpec((None, 2, tm, FS), lambda p, i: (p, 0, i, 0)),
                   pl.BlockSpec((None, 2, CONV_WIDTH, FS), lambda p, i: (p, 0, 0, 0)),
                   pl.BlockSpec((None, 2, 1, FS), lambda p, i: (p, 0, 0, 0))],
        out_shape=[_sds((2, 2, SEQ, FS), MM_DTYPE), _sds((2, 2, CONV_WIDTH, FS)), _sds((2, 2, 1, FS))],
        compiler_params=_cparams(("parallel", "arbitrary")))(hf, hf, hf, hf, hf, hf, conv_w, conv_w, conv_b, conv_b, dact,
                                                              dact, *deps)


def _loss(y, target):
    tm = _tile(SEQ, 256)

    def body(y_ref, t_ref, dy_ref, l_ref):
        err = y_ref[...] - t_ref[...]
        dy_ref[...] = err * (1.0 / D_MODEL)

        @pl.when(pl.program_id(0) == 0)
        def _():
            l_ref[...] = jnp.zeros_like(l_ref)

        l_ref[...] += 0.5 * jnp.sum(jnp.mean(jnp.square(err), axis=-1))

    blk = pl.BlockSpec((tm, D_MODEL), lambda i: (i, 0))
    return pl.pallas_call(body, name="loss", grid=(SEQ // tm,), in_specs=[blk, blk],
                          out_specs=[blk, pl.BlockSpec((SUBLANES, LANES), lambda i: (0, 0))],
                          out_shape=[_sds((SEQ, D_MODEL)), _sds((SUBLANES, LANES))],
                          compiler_params=_cparams(("arbitrary",)))(y, target)


ADAM_BLOCK_BYTES = 3 << 19
ELEMENTWISE_COLS = 1024


def _adamw_math(w, g, m, v):
    nm = ADAM_B1 * m + (1.0 - ADAM_B1) * g
    nv = ADAM_B2 * v + (1.0 - ADAM_B2) * jnp.square(g)
    m_hat = nm / (1.0 - ADAM_B1 ** ADAM_STEP)
    v_hat = nv / (1.0 - ADAM_B2 ** ADAM_STEP)
    return -ADAM_LR * (m_hat / (jnp.sqrt(v_hat) + ADAM_EPS) + ADAM_WD * w), nm, nv


def _adamw(name, w, g, m, v):
    r, c = w.shape
    tr = _tile(r, max(SUBLANES, ADAM_BLOCK_BYTES // (4 * c)), SUBLANES)

    def body(w_ref, g_ref, m_ref, v_ref, d_ref, nm_ref, nv_ref):
        d_ref[...], nm_ref[...], nv_ref[...] = _adamw_math(w_ref[...], g_ref[...], m_ref[...], v_ref[...])

    blk = pl.BlockSpec((tr, c), lambda i: (i, 0))
    return pl.pallas_call(body, name=name, grid=(r // tr,), in_specs=[blk] * 4, out_specs=[blk] * 3,
                          out_shape=[_sds((r, c))] * 3, compiler_params=_cparams(("parallel",)))(w, g, m, v)


def _adamw_big(name, l, c_idx, w, m, v, g_own, g_got, prev):
    depth, _, r, c = w.shape
    tc = _tile(c, ELEMENTWISE_COLS)
    tr = _tile(r, max(SUBLANES, ADAM_BLOCK_BYTES // (4 * tc)), SUBLANES)

    def body(c_ref, w_ref, m_ref, v_ref, own_ref, got_ref, *rest):
        g_ref, d_ref, nm_ref, nv_ref = rest[-4:]
        g = jnp.where(pl.program_id(0) == c_ref[0], own_ref[...], got_ref[...])
        g_ref[...] = g
        d_ref[...], nm_ref[...], nv_ref[...] = _adamw_math(w_ref[...], g, m_ref[...], v_ref[...])

    stacked = pl.BlockSpec((None, None, tr, tc), lambda h, i, j, cr: (l, h, i, j))
    own = pl.BlockSpec((tr, tc), lambda h, i, j, cr: (jnp.where(h == cr[0], i, 0), jnp.where(h == cr[0], j, 0)))
    got = pl.BlockSpec((tr, tc), lambda h, i, j, cr: (jnp.where(h == cr[0], 0, i), jnp.where(h == cr[0], 0, j)))
    grid_spec = pltpu.PrefetchScalarGridSpec(
        num_scalar_prefetch=1, grid=(2, r // tr, c // tc),
        in_specs=[stacked] * 3 + [own, got] + ([ANY] * 4 if prev else []), out_specs=[stacked] * 4)
    return pl.pallas_call(
        body, name=name, grid_spec=grid_spec, out_shape=[_sds((depth, 2, r, c))] * 4,
        input_output_aliases={6 + k: k for k in range(4)} if prev else {},
        compiler_params=_cparams(("arbitrary", "arbitrary", "arbitrary")))(c_idx, w, m, v, g_own, g_got, *(prev or ()))


ANY = pl.BlockSpec(memory_space=pl.ANY)


def _place():
    x, y, c = lax.axis_index("x"), lax.axis_index("y"), lax.axis_index("c")
    chips = [(1 - x, y), (x, 1 - y), (1 - x, 1 - y)]
    return x, y, c, chips


def _cast_place(name, w, l, me_idx, out_dtype):
    _, _, r, c = w.shape
    tr = _tile(r, max(2 * SUBLANES, COPY_BLOCK_BYTES // (4 * c)), 2 * SUBLANES)

    def body(me_ref, w_ref, o_ref):
        o_ref[...] = w_ref[...].astype(o_ref.dtype)

    grid_spec = pltpu.PrefetchScalarGridSpec(
        num_scalar_prefetch=1, grid=(2, r // tr),
        in_specs=[pl.BlockSpec((None, None, tr, c), lambda h, i, me: (l, h, i, 0))],
        out_specs=pl.BlockSpec((None, None, tr, c), lambda h, i, me: (me[0], h, i, 0)))
    return pl.pallas_call(body, name=name, grid_spec=grid_spec, out_shape=_sds((N_CHIPS, 2, r, c), out_dtype),
                          compiler_params=_cparams(("parallel", "parallel")))(me_idx, w)


HBM = pl.BlockSpec(memory_space=pltpu.HBM)
SEM = pl.BlockSpec(memory_space=pltpu.SEMAPHORE)
TOKEN = (SUBLANES, LANES)


def _comm_call(name, body, hbm, sems_in=(), after=None, sems_out=(), token=False):
    n, k = len(hbm), len(sems_out)
    ins = [pltpu.with_memory_space_constraint(a, pltpu.HBM) for a in hbm] + list(sems_in)
    in_specs = [HBM] * n + [SEM] * len(sems_in)
    if after is not None:
        ins.append(after)
        in_specs.append(ANY)
    out_shape = [pltpu.SemaphoreType.DMA((s,)) for s in sems_out] + [pltpu.HBM(a.shape, a.dtype) for a in hbm]
    out_specs = [SEM] * k + [HBM] * n
    if token:
        out_shape.append(_sds(TOKEN))
        out_specs.append(pl.BlockSpec(memory_space=pltpu.VMEM))
    res = pl.pallas_call(
        body, name=name, in_specs=in_specs, out_specs=out_specs, out_shape=out_shape,
        input_output_aliases={i: k + i for i in range(n)},
        compiler_params=pltpu.CompilerParams(has_side_effects=pltpu.SideEffectType.DATAFLOW_SIDE_EFFECTING))(*ins)
    return list(res[:k]), list(res[k:k + n]), (res[k + n] if token else None)


def _remote(src, dst, send, recv, to):
    return pltpu.make_async_remote_copy(src_ref=src, dst_ref=dst, send_sem=send, recv_sem=recv, device_id=to,
                                        device_id_type=MESH)


def _gather_start(name, bufs, after=None, pair=False):
    n = len(bufs)
    o = n + (after is not None)

    def body(*refs):
        ins, (send, recv), token = refs[:n], refs[o:o + 2], refs[-1]
        x, y, c, chips = _place()
        for i in range(n):
            mine = ins[i].at[2 * x + y, c]
            for k, chip in enumerate(chips):
                _remote(mine, mine, send.at[3 * i + k], recv.at[3 * i + k], (*chip, c)).start()
            if pair:
                _remote(mine, mine, send.at[3 * n + i], recv.at[3 * n + i], (x, y, 1 - c)).start()
        token[...] = jnp.zeros(TOKEN, F32)

    n_sems = (3 + pair) * n
    return _comm_call(name, body, bufs, after=after, sems_out=(n_sems, n_sems), token=True)


def _gather_forward(name, bufs, sems, after, pair=False):
    n = len(bufs)
    o = n + 2 + (after is not None)

    def body(*refs):
        ins, (send, recv), (send2, recv2), token = refs[:n], refs[n:n + 2], refs[o:o + 2], refs[-1]
        x, y, c, chips = _place()
        for i in range(n):
            mine = ins[i].at[2 * x + y, c]
            for k, chip in enumerate(chips):
                land = ins[i].at[2 * chip[0] + chip[1], c]
                first = _remote(mine, land, send.at[3 * i + k], recv.at[3 * i + k], (*chip, c))
                first.wait_send()
                first.wait_recv()
                _remote(land, land, send2.at[3 * i + k], recv2.at[3 * i + k], (x, y, 1 - c)).start()
            if pair:
                own = _remote(mine, ins[i].at[2 * x + y, 1 - c], send.at[3 * n + i], recv.at[3 * n + i], (x, y, 1 - c))
                own.wait_send()
                own.wait_recv()
        token[...] = jnp.zeros(TOKEN, F32)

    return _comm_call(name, body, bufs, sems_in=sems, after=after, sems_out=(3 * n, 3 * n), token=True)


def _gather_finish(name, bufs, sems, after):
    n = len(bufs)

    def body(*refs):
        ins, (send, recv) = refs[:n], refs[n:n + 2]
        x, y, c, chips = _place()
        for i in range(n):
            for k, chip in enumerate(chips):
                idx = 2 * chip[0] + chip[1]
                cp = _remote(ins[i].at[idx, c], ins[i].at[idx, 1 - c], send.at[3 * i + k], recv.at[3 * i + k], (x, y, 1 - c))
                cp.wait_send()
                cp.wait_recv()

    return _comm_call(name, body, bufs, sems_in=sems, after=after)[1]


def _swap_start(name, grads):
    n = len(grads)
    lands = [lax.empty((g.shape[0],) + g.shape[2:], g.dtype) for g in grads]

    def body(*refs):
        ins, lnd, (send, recv), token = refs[:n], refs[n:2 * n], refs[2 * n:2 * n + 2], refs[-1]
        x, y, c, _ = _place()
        for i in range(n):
            _remote(ins[i].at[:, 1 - c], lnd[i], send.at[i], recv.at[i], (x, y, 1 - c)).start()
        token[...] = jnp.zeros(TOKEN, F32)

    return _comm_call(name, body, list(grads) + lands, sems_out=(n, n), token=True)


def _swap_wait(name, hbm, sems, after):
    n = len(hbm) // 2

    def body(*refs):
        ins, lnd, (send, recv) = refs[:n], refs[n:2 * n], refs[2 * n:2 * n + 2]
        x, y, c, _ = _place()
        for i in range(n):
            cp = _remote(ins[i].at[:, 1 - c], lnd[i], send.at[i], recv.at[i], (x, y, 1 - c))
            cp.wait_send()
            cp.wait_recv()

    out = _comm_call(name, body, hbm, sems_in=sems, after=after)[1]
    return out[:n], out[n:]


def _pair_add(name, g, got, cm_idx):
    nk, _, r, c = g.shape
    tr = _tile(r, max(2 * SUBLANES, COPY_BLOCK_BYTES // (4 * c)), 2 * SUBLANES)

    def body(cm_ref, g_ref, x_ref, o_ref, land_ref):
        s = (g_ref[...] + x_ref[...]).astype(o_ref.dtype)
        o_ref[...] = s

        @pl.when(pl.program_id(1) == cm_ref[1])
        def _():
            land_ref[...] = s

    grid_spec = pltpu.PrefetchScalarGridSpec(
        num_scalar_prefetch=1, grid=(r // tr, nk),
        in_specs=[pl.BlockSpec((None, None, tr, c), lambda i, k, cm: (k, cm[0], i, 0)),
                  pl.BlockSpec((None, tr, c), lambda i, k, cm: (k, i, 0))],
        out_specs=[pl.BlockSpec((None, tr, c), lambda i, k, cm: (k, i, 0)),
                   pl.BlockSpec((None, tr, c), lambda i, k, cm: (cm[1], i, 0))])
    return pl.pallas_call(body, name=name, grid_spec=grid_spec, out_shape=[_sds((nk, r, c), BF16)] * 2,
                          compiler_params=_cparams(("parallel", "arbitrary")))(cm_idx, g, got)


def _scatter_start(name, parts, lands):
    n = len(parts)

    def body(*refs):
        ins, lnd, (send, recv), token = refs[:n], refs[n:2 * n], refs[2 * n:2 * n + 2], refs[-1]
        x, y, c, chips = _place()
        for i in range(n):
            for k, chip in enumerate(chips):
                _remote(ins[i].at[2 * chip[0] + chip[1]], lnd[i].at[2 * x + y], send.at[3 * i + k], recv.at[3 * i + k],
                        (*chip, c)).start()
        token[...] = jnp.zeros(TOKEN, F32)

    return _comm_call(name, body, list(parts) + list(lands), sems_out=(3 * n, 3 * n), token=True)


def _scatter_wait(name, hbm, sems, after):
    n = len(hbm) // 2

    def body(*refs):
        ins, lnd, (send, recv) = refs[:n], refs[n:2 * n], refs[2 * n:2 * n + 2]
        x, y, c, chips = _place()
        for i in range(n):
            for k, chip in enumerate(chips):
                idx = 2 * chip[0] + chip[1]
                cp = _remote(ins[i].at[idx], lnd[i].at[idx], send.at[3 * i + k], recv.at[3 * i + k], (*chip, c))
                cp.wait_send()
                cp.wait_recv()

    out = _comm_call(name, body, hbm, sems_in=sems, after=after)[1]
    return out[:n], out[n:]


def _sum_leading(name, x, out_dtype=F32):
    nk, r, c = x.shape
    tc = _tile(c, ELEMENTWISE_COLS)
    tr = _tile(r, max(2 * SUBLANES, COPY_BLOCK_BYTES // (nk * tc * x.dtype.itemsize)), 2 * SUBLANES)

    def body(x_ref, o_ref):
        acc = x_ref[0].astype(F32)
        for k in range(1, nk):
            acc = acc + x_ref[k].astype(F32)
        o_ref[...] = acc.astype(o_ref.dtype)

    return pl.pallas_call(body, name=name, grid=(r // tr, c // tc),
                          in_specs=[pl.BlockSpec((nk, tr, tc), lambda i, j: (0, i, j))],
                          out_specs=pl.BlockSpec((tr, tc), lambda i, j: (i, j)), out_shape=_sds((r, c), out_dtype),
                          compiler_params=_cparams(("parallel", "parallel")))(x)


def _exchange_start(name, halves):
    n = len(halves)
    lands = [lax.empty(h.shape, h.dtype) for h in halves]

    def body(*refs):
        ins, lnd, (send, recv), token = refs[:n], refs[n:2 * n], refs[2 * n:2 * n + 2], refs[-1]
        x, y, c, _ = _place()
        for i in range(n):
            _remote(ins[i], lnd[i], send.at[i], recv.at[i], (x, y, 1 - c)).start()
        token[...] = jnp.zeros(TOKEN, F32)

    return _comm_call(name, body, list(halves) + lands, sems_out=(n, n), token=True)


def _exchange_wait(name, hbm, sems, after):
    n = len(hbm) // 2

    def body(*refs):
        ins, lnd, (send, recv) = refs[:n], refs[n:2 * n], refs[2 * n:2 * n + 2]
        x, y, c, _ = _place()
        for i in range(n):
            cp = _remote(ins[i], lnd[i], send.at[i], recv.at[i], (x, y, 1 - c))
            cp.wait_send()
            cp.wait_recv()

    out = _comm_call(name, body, hbm, sems_in=sems, after=after)[1]
    return out[:n], out[n:]


SMALL = ("attn_sinks", "pool_w", "pool_scale", "ssm_lam_re", "ssm_lam_im", "ssm_log_dt", "ssm_b_re", "ssm_b_im",
         "ssm_c_re", "ssm_c_im", "ssm_d", "ln1_g", "ln1_b", "ffn_conv_b", "ln2_g", "ln2_b")
BIG = ("w_in", "ssm_glu_w", "w_out", "ffn_w_up", "ffn_conv_w", "ffn_w_down")
ALL_W = ("w_in", "attn_sinks", "pool_w", "pool_scale", "ssm_lam_re", "ssm_lam_im", "ssm_log_dt", "ssm_b_re", "ssm_b_im",
         "ssm_c_re", "ssm_c_im", "ssm_d", "ssm_glu_w", "w_out", "ln1_g", "ln1_b", "ffn_w_up", "ffn_conv_w", "ffn_conv_b",
         "ffn_w_down", "ln2_g", "ln2_b")
PACK_UNIT = SUBLANES * LANES


def _padded(n):
    return -(-n // PACK_UNIT) * PACK_UNIT


def _pack(arrs):
    cols = []
    for name in SMALL:
        a = arrs[name].reshape(DEPTH, -1)
        cols.append(jnp.pad(a, ((0, 0), (0, _padded(a.shape[1]) - a.shape[1]))))
    return jnp.concatenate(cols, axis=1).reshape(-1, LANES)


def _unpack(packed, shapes):
    flat = packed.reshape(DEPTH, -1)
    out, off = {}, 0
    for name in SMALL:
        n = math.prod(shapes[name][1:])
        out[name] = flat[:, off:off + n].reshape(shapes[name])
        off += _padded(n)
    return out


def _b_rows(b):
    return b.transpose(2, 0, 1).reshape(SSM_GROUP, SSM_CH)


def _b_unrows(b):
    return b.reshape(SSM_GROUP, SSM_N_GROUPS, SSM_STATE).transpose(1, 2, 0)


def _block_diag_in(bb):
    eye = jnp.eye(SSM_N_GROUPS, dtype=F32)
    b3 = bb.reshape(SSM_GROUP, SSM_N_GROUPS, SSM_STATE)
    return jnp.einsum("hgp,gk->ghkp", b3, eye).reshape(SSM_WIDTH, SSM_CH)


def _c_unrows(c):
    return c.reshape(SSM_GROUP, SSM_N_GROUPS, SSM_STATE).transpose(1, 0, 2)


def _block_diag_out(cc):
    eye = jnp.eye(SSM_N_GROUPS, dtype=F32)
    return jnp.einsum("ghp,gk->gpkh", cc, eye).reshape(SSM_CH, SSM_WIDTH)


def _rows_layout(re, im):
    n = re.shape[1]
    return jnp.stack([re.reshape(SCAN_NB, SCAN_CW, n), im.reshape(SCAN_NB, SCAN_CW, n)], axis=1).reshape(2 * SSM_CH, n)


H_POOL0 = ATTN_WIDTH + 2 * KV_WIDTH
H_SSM0 = H_POOL0 + POOL_WIDTH


def _ssm_params(p):
    lr = p["ssm_lam_re"].reshape(1, SSM_CH)
    li = p["ssm_lam_im"].reshape(1, SSM_CH)
    ldt = jnp.repeat(p["ssm_log_dt"], SSM_STATE).reshape(1, SSM_CH)
    return lr, li, ldt, _b_rows(p["ssm_b_re"]), _b_rows(p["ssm_b_im"])


def _layer_fwd(x, xb, p, wg, rope_t, dep, pre, mid):
    cos_t, sin_t = rope_t
    h = _mm_shard_cols("in_proj", xb, wg["w_in"], dep=dep)
    qk = _rope("rope_fwd", h, 0, Q_TILES + KV_TILES, cos_t, sin_t, MM_DTYPE)
    y_attn, y_attn_b = _attn_fwd(qk, h, p["attn_sinks"])
    y_pool = _pool_fwd(h, p["pool_w"], p["pool_scale"].reshape(1, POOL_WIDTH))
    ssm_in = _ssm_params(p)
    ar, ai, bbr, bbi = _ssm_prep(*ssm_in)
    bd = _scan_layout(_block_diag_in(bbr), _block_diag_in(bbi)).astype(MM_DTYPE)
    cc = _rows_layout(_block_diag_out(p["ssm_c_re"]), -_block_diag_out(p["ssm_c_im"])).astype(MM_DTYPE)
    dvec = p["ssm_d"].reshape(1, SSM_WIDTH)
    up = _time_permute(h[:, H_SSM0:])
    xx = _mm_nn("ssm_bu", up, bd, tn=1024)
    ss = _ssm_scan("ssm_scan_fwd", _scan_layout(ar, ai), xx, False)
    yp = _mm_nn("ssm_cs", ss, cc, tk=1024)
    yf, gy = _ssm_gelu(yp, up, dvec)
    ab = _mm_shard_cols("ssm_glu_proj", gy, wg["ssm_glu_w"])
    y_ssm = _time_unpermute(_ssm_glu(ab))
    mix = jnp.concatenate([y_attn_b, y_pool, y_ssm], axis=1)
    mixo = _mm_nn("out_proj", mix, wg["w_out"].reshape(MIX_WIDTH, D_MODEL), dep=pre(mix))
    r1, x1, x1b = _ln_fwd("ln1_fwd", x, mixo, p["ln1_g"].reshape(1, D_MODEL), p["ln1_b"].reshape(1, D_MODEL))
    tokens = mid(x1b)
    hf = _ffn_up(x1b, wg["ffn_w_up"], dep=tokens)
    conv_b = p["ffn_conv_b"].reshape(N_CHIPS, 1, FS)
    act = _ffn_act(hf, wg["ffn_conv_w"], conv_b)
    f = _ffn_down(act, wg["ffn_w_down"].reshape(2, FS, D_MODEL))
    r2, x2, x2b = _ln_fwd("ln2_fwd", x1, f, p["ln2_g"].reshape(1, D_MODEL), p["ln2_b"].reshape(1, D_MODEL))
    saved = dict(xb=xb, h=h, qk=qk, y_attn=y_attn, ssm_in=ssm_in, ar=ar, ai=ai, bd=bd, cc=cc, dvec=dvec, up=up, ss=ss, yf=yf,
                 gy=gy, ab=ab, mix=mix, r1=r1, x1b=x1b, hf=hf, conv_b=conv_b, act=act, r2=r2)
    return x2, x2b, saved


def _layer_bwd(da, db, p, wg, sv, rope_t, run, start):
    cos_t, sin_t = rope_t
    small = {}
    dr2, dr2b, dg, dbeta = _ln_bwd("ln2_bwd" if db is not None else "ln2_bwd_last", sv["r2"], p["ln2_g"].reshape(1, D_MODEL),
                                   p["ln2_b"].reshape(1, D_MODEL), da, db, dep=run("h0", None))
    small["ln2_g"], small["ln2_b"] = dg, dbeta
    w_down = wg["ffn_w_down"].reshape(2, FS, D_MODEL)
    dact = _ffn_down_dact(dr2b, w_down)
    dw_down = _ffn_down_dw(sv["act"], dr2b)
    dh_ffn, dcw, dcb = _ffn_act_bwd(sv["hf"], wg["ffn_conv_w"], sv["conv_b"], dact, dep=run("h1", dw_down))
    dconv_w = dcw.transpose(1, 0, 2, 3).reshape(N_CHIPS, CONV_WIDTH, FS)
    small["ffn_conv_b"] = dcb.transpose(1, 0, 2, 3)
    dw_up = _ffn_up_dw(sv["x1b"], dh_ffn)
    tok = [start("ffn", {"ffn_w_up": dw_up, "ffn_conv_w": dconv_w,
                         "ffn_w_down": dw_down.reshape(N_CHIPS, FS // 2, D_MODEL)})] + run("h2", dw_up)
    dx1_ffn = _ffn_up_dx(dh_ffn, wg["ffn_w_up"], dep=tok)
    dr1, dr1b, dg, dbeta = _ln_bwd("ln1_bwd", sv["r1"], p["ln1_g"].reshape(1, D_MODEL), p["ln1_b"].reshape(1, D_MODEL), dr2,
                                   dx1_ffn, dep=tok)
    small["ln1_g"], small["ln1_b"] = dg, dbeta
    w_out = wg["w_out"].reshape(MIX_WIDTH, D_MODEL)
    dw_out = _mm_tn("out_proj_dw", sv["mix"], dr1b)
    dmix = _mm_nt("out_proj_dx", dr1b, w_out, dep=run("h3", dw_out))
    dq, dkc, dkp, dvc, dvp, dsk = _attn_bwd(sv["qk"], sv["h"], p["attn_sinks"], sv["y_attn"], dmix, 0)
    small["attn_sinks"] = dsk[:, :, 0]
    dh_attn = _attn_dh(dq, dkc, dkp, dvc, dvp, cos_t, -sin_t)
    dh_pool, dpw, dps = _pool_bwd(sv["h"], p["pool_w"], p["pool_scale"].reshape(1, POOL_WIDTH), dmix, ATTN_WIDTH // POOL_WIDTH)
    small["pool_w"], small["pool_scale"] = dpw, dps
    dout_p = _time_permute(dmix[:, ATTN_WIDTH + POOL_WIDTH:])
    dab = _ssm_glu_bwd(sv["ab"], dout_p)
    dgy = _mm_shard_cols_nt("ssm_glu_dx", dab, wg["ssm_glu_w"])
    dw_glu = _mm_shard_cols_tn("ssm_glu_dw", sv["gy"], dab, N_CHIPS)
    dyf, du1, dd = _ssm_gelu_bwd(sv["yf"], dgy, sv["up"], sv["dvec"])
    small["ssm_d"] = dd
    dss = _mm_nt("ssm_cs_dx", dyf, sv["cc"], tn=1024)
    dcre, dcim = _scan_unlayout(_ssm_diag("ssm_c_diag", _mm_tn("ssm_cs_dw", dyf, sv["ss"], tn=1024)))
    small["ssm_c_re"], small["ssm_c_im"] = _c_unrows(dcre), -_c_unrows(dcim)
    gg, da8 = _ssm_scan("ssm_scan_bwd", _scan_layout(sv["ar"], -sv["ai"]), dss, True, sv["ss"])
    du2 = _mm_nt("ssm_bu_dx", gg, sv["bd"], tk=1024)
    dbbr, dbbi = _scan_unlayout(_ssm_diag("ssm_b_diag", _mm_tn("ssm_bu_dw", sv["up"], gg, tn=1024)))
    dar8, dai8 = _scan_unlayout(da8)
    dlr, dli, dldt, dbr, dbi = _ssm_prep_bwd(*sv["ssm_in"], dar8, dai8, dbbr, dbbi)
    small["ssm_lam_re"], small["ssm_lam_im"] = dlr, dli
    small["ssm_log_dt"] = dldt.reshape(SSM_N_GROUPS, SSM_STATE).sum(axis=1)
    small["ssm_b_re"], small["ssm_b_im"] = _b_unrows(dbr), _b_unrows(dbi)
    dh_ssm = _time_unpermute(_add2("ssm_du", du1, du2, MM_DTYPE))
    dh = jnp.concatenate([dh_attn, dh_pool, dh_ssm], axis=1)
    dx_in = _mm_shard_cols_nt("in_proj_dx", dh, wg["w_in"], dep=run("h4", dh))
    dw_in = _mm_shard_cols_tn("in_proj_dw", sv["xb"], dh, N_CHIPS)
    start("mix", {"w_in": dw_in, "ssm_glu_w": dw_glu, "w_out": dw_out.reshape(N_CHIPS, MIX_WIDTH // N_CHIPS, D_MODEL)})
    return dr1, dx_in, small


CONV_PAD = 2 * SUBLANES


def _halved(name, a):
    if name == "ffn_conv_w":
        a = jnp.pad(a, ((0, 0), (0, CONV_PAD - CONV_WIDTH), (0, 0)))
    return a.reshape(a.shape[0], 2, a.shape[1] // 2, a.shape[2])


def _unhalved(name, a):
    a = a.reshape(a.shape[:-3] + (2 * a.shape[-2], a.shape[-1]))
    return a[..., :CONV_WIDTH, :] if name == "ffn_conv_w" else a


class _Reduce:
    def __init__(self, tag, grads, cm_idx):
        self.tag, self.cm_idx, self.names = tag, cm_idx, tuple(grads)
        g4 = [_halved(name, grads[name]) for name in self.names]
        self.sems, self.hbm, self.token = _swap_start("grad_swap_start_" + tag, g4)

    def swapped(self, after):
        g4, got = _swap_wait("grad_swap_wait_" + self.tag, self.hbm, self.sems, after)
        parts, lands = zip(*[_pair_add("grad_pair_add", g, x, self.cm_idx) for g, x in zip(g4, got)])
        self.sems, self.hbm, self.token = _scatter_start("grad_scatter_start_" + self.tag, parts, lands)
        return self.token

    def scattered(self, after):
        _, recv = _scatter_wait("grad_scatter_wait_" + self.tag, self.hbm, self.sems, after)
        halves = [_sum_leading("grad_chip_sum", r) for r in recv]
        self.sems, self.hbm, self.token = _exchange_start("grad_exchange_start_" + self.tag, halves)
        return self.token

    def finish(self, after):
        return _exchange_wait("grad_exchange_wait_" + self.tag, self.hbm, self.sems, after)


def kernel(x, w_in, attn_sinks, pool_w, pool_scale, ssm_lam_re, ssm_lam_im, ssm_log_dt, ssm_b_re, ssm_b_im, ssm_c_re, ssm_c_im, ssm_d, ssm_glu_w, w_out, ln1_g, ln1_b, ffn_w_up, ffn_conv_w, ffn_conv_b, ffn_w_down, ln2_g, ln2_b, loss_target, m_w_in, m_attn_sinks, m_pool_w, m_pool_scale, m_ssm_lam_re, m_ssm_lam_im, m_ssm_log_dt, m_ssm_b_re, m_ssm_b_im, m_ssm_c_re, m_ssm_c_im, m_ssm_d, m_ssm_glu_w, m_w_out, m_ln1_g, m_ln1_b, m_ffn_w_up, m_ffn_conv_w, m_ffn_conv_b, m_ffn_w_down, m_ln2_g, m_ln2_b, v_w_in, v_attn_sinks, v_pool_w, v_pool_scale, v_ssm_lam_re, v_ssm_lam_im, v_ssm_log_dt, v_ssm_b_re, v_ssm_b_im, v_ssm_c_re, v_ssm_c_im, v_ssm_d, v_ssm_glu_w, v_w_out, v_ln1_g, v_ln1_b, v_ffn_w_up, v_ffn_conv_w, v_ffn_conv_b, v_ffn_w_down, v_ln2_g, v_ln2_b):
    w = dict(w_in=w_in, attn_sinks=attn_sinks, pool_w=pool_w, pool_scale=pool_scale, ssm_lam_re=ssm_lam_re,
             ssm_lam_im=ssm_lam_im, ssm_log_dt=ssm_log_dt, ssm_b_re=ssm_b_re, ssm_b_im=ssm_b_im, ssm_c_re=ssm_c_re,
             ssm_c_im=ssm_c_im, ssm_d=ssm_d, ssm_glu_w=ssm_glu_w, w_out=w_out, ln1_g=ln1_g, ln1_b=ln1_b, ffn_w_up=ffn_w_up,
             ffn_conv_w=ffn_conv_w, ffn_conv_b=ffn_conv_b, ffn_w_down=ffn_w_down, ln2_g=ln2_g, ln2_b=ln2_b)
    m = dict(w_in=m_w_in, attn_sinks=m_attn_sinks, pool_w=m_pool_w, pool_scale=m_pool_scale, ssm_lam_re=m_ssm_lam_re,
             ssm_lam_im=m_ssm_lam_im, ssm_log_dt=m_ssm_log_dt, ssm_b_re=m_ssm_b_re, ssm_b_im=m_ssm_b_im, ssm_c_re=m_ssm_c_re,
             ssm_c_im=m_ssm_c_im, ssm_d=m_ssm_d, ssm_glu_w=m_ssm_glu_w, w_out=m_w_out, ln1_g=m_ln1_g, ln1_b=m_ln1_b,
             ffn_w_up=m_ffn_w_up, ffn_conv_w=m_ffn_conv_w, ffn_conv_b=m_ffn_conv_b, ffn_w_down=m_ffn_w_down, ln2_g=m_ln2_g,
             ln2_b=m_ln2_b)
    v = dict(w_in=v_w_in, attn_sinks=v_attn_sinks, pool_w=v_pool_w, pool_scale=v_pool_scale, ssm_lam_re=v_ssm_lam_re,
             ssm_lam_im=v_ssm_lam_im, ssm_log_dt=v_ssm_log_dt, ssm_b_re=v_ssm_b_re, ssm_b_im=v_ssm_b_im, ssm_c_re=v_ssm_c_re,
             ssm_c_im=v_ssm_c_im, ssm_d=v_ssm_d, ssm_glu_w=v_ssm_glu_w, w_out=v_w_out, ln1_g=v_ln1_g, ln1_b=v_ln1_b,
             ffn_w_up=v_ffn_w_up, ffn_conv_w=v_ffn_conv_w, ffn_conv_b=v_ffn_conv_b, ffn_w_down=v_ffn_w_down, ln2_g=v_ln2_g,
             ln2_b=v_ln2_b)
    c_pos = lax.axis_index("c").astype(jnp.int32)
    chip = (2 * lax.axis_index("x") + lax.axis_index("y")).astype(jnp.int32)
    c_idx, chip_idx, cm_idx = c_pos.reshape(1), chip.reshape(1), jnp.stack([c_pos, chip])
    rope_t = _rope_tables()
    xs = x.reshape(SEQ, D_MODEL)
    xb = xs.astype(MM_DTYPE)
    for t in (w, m, v):
        t["ffn_w_up"] = jnp.swapaxes(t["ffn_w_up"], 1, 2)
    wh, mh, vh = ({n: _halved(n, t[n]) for n in BIG} for t in (w, m, v))

    def place(l):
        return [_cast_place("place_" + n, wh[n], l, chip_idx, F32 if n == "ffn_conv_w" else MM_DTYPE) for n in BIG]

    n_mix = BIG.index("ffn_w_up")

    def gather_start(l, after):
        bufs = place(l)
        return (_gather_start("gather_start_%d_mix" % l, bufs[:n_mix], after),
                _gather_start("gather_start_%d_ffn" % l, bufs[n_mix:], after))

    def gather_forward(l, group, started, after):
        return _gather_forward("gather_forward_%d_%s" % (l, group), started[1], started[0], after)

    def gather_finish(l, group, forwarded, after):
        bufs = _gather_finish("gather_finish_%d_%s" % (l, group), forwarded[1], forwarded[0], after)
        names = BIG[:n_mix] if group == "mix" else BIG[n_mix:]
        return bufs, {n: _unhalved(n, g) for n, g in zip(names, bufs)}

    def gather_wait(l, group, started, after):
        return gather_finish(l, group, gather_forward(l, group, started, after), after)

    flight = gather_start(0, None)
    gathered, saved = [gather_wait(0, "mix", flight[0], None)[1]], []
    for l in range(DEPTH):
        nxt = {}

        def pre(after):
            if l == 0:
                return []
            nxt["forwarded"] = gather_forward(l, "ffn", flight[1], after)
            return [nxt["forwarded"][2]]

        def mid(after):
            forwarded = gather_forward(l, "ffn", flight[1], after) if l == 0 else nxt["forwarded"]
            bufs, wg_ffn = gather_finish(l, "ffn", forwarded, after)
            gathered[l].update(wg_ffn)
            if l + 1 == DEPTH:
                return []
            nxt["flight"] = gather_start(l + 1, bufs[0])
            return [nxt["flight"][0][2], nxt["flight"][1][2]]

        xs, xb, sv = _layer_fwd(xs, xb, {n: w[n][l] for n in SMALL}, gathered[l], rope_t, flight[1][2] if l == 0 else None, pre,
                                mid)
        saved.append(sv)
        if l + 1 < DEPTH:
            flight = nxt["flight"]
            gathered.append(gather_wait(l + 1, "mix", flight[0], xb)[1])
    dy, loss_tile = _loss(xs, loss_target.reshape(SEQ, D_MODEL))
    loss = lax.psum(loss_tile[0, 0], ("x", "y", "c"))

    big_out = {n: None for n in BIG}
    small_g = {n: [None] * DEPTH for n in SMALL}
    agenda = {}
    tail = []
    plan = {"ffn": (("h3", 0), ("h1", -1), ("h2", -1)), "mix": (("h1", -1), ("h3", -1), ("h4", -1))}
    tail_rank = {("ffn", 1): 0, ("mix", 0): 1, ("ffn", 2): 2, ("mix", 1): 3, ("mix", 2): 4}
    started = []

    def book(l, group, red):
        def update(after):
            names, own, got = red.names, *red.finish(after)
            for n, o, g in zip(names, own, got):
                big_out[n] = _adamw_big("adamw_" + n, l, c_idx, wh[n], mh[n], vh[n], o, g, big_out[n])
            return [big_out[names[-1]][0]] if l == 0 else []

        steps = (lambda a: [red.swapped(a)], lambda a: [red.scattered(a)], update)
        for k, ((hook, dl), step) in enumerate(zip(plan[group], steps)):
            if l + dl >= 0:
                agenda.setdefault((l + dl, hook), []).append(step)
            else:
                tail.append((tail_rank[group, k], step))

    def run_at(l):
        return lambda hook, after: [t for step in agenda.pop((l, hook), []) for t in step(after)]

    def start_at(l):
        def start(group, grads):
            red = _Reduce("%s_%d" % (group, l), grads, cm_idx)
            book(l, group, red)
            started.append(red.token)
            return red.token
        return start

    da, db, carry = dy, None, []
    for l in reversed(range(DEPTH)):
        agenda.setdefault((l, "h0"), []).append(lambda after, carry=carry: carry)
        da, db, small = _layer_bwd(da, db, {n: w[n][l] for n in SMALL}, gathered[l], saved[l], rope_t, run_at(l), start_at(l))
        for n in SMALL:
            small_g[n][l] = small[n].reshape(w[n].shape[1:])
        carry = run_at(l)("end", db) + started[-1:]
    shapes = {n: w[n].shape for n in SMALL}
    part = _pack({n: jnp.stack(small_g[n]) for n in SMALL})
    slots = lax.dynamic_update_slice(jnp.zeros((N_CHIPS, 2) + part.shape, F32), part[None, None], (chip, c_pos, 0, 0))
    small_sems, small_bufs, after = _gather_start("gather_start_small", [slots], pair=True)
    for _, step in sorted(tail, key=lambda rs: rs[0]):
        after = (step(after) or [after])[-1]
    grad_x = _ln_in_grad(da, db).reshape(x.shape)
    small_sems, small_bufs, _ = _gather_forward("gather_forward_small", small_bufs, small_sems, after, pair=True)
    small_bufs = _gather_finish("gather_finish_small", small_bufs, small_sems, grad_x)
    g_small = _sum_leading("small_grad_sum", small_bufs[0].reshape((N_DEV,) + part.shape))
    upd = _adamw("adamw_small", _pack(w), g_small, _pack(m), _pack(v))
    small_out = [_unpack(a, shapes) for a in (g_small,) + tuple(upd)]

    outs = [loss, grad_x]
    for kind in range(4):
        for n in ALL_W:
            if n in SMALL:
                outs.append(small_out[kind][n])
            else:
                o = _unhalved(n, big_out[n][kind])
                outs.append(jnp.swapaxes(o, 1, 2) if n == "ffn_w_up" else o)
    return tuple(outs)


def _ln_in_grad(dr1, dx_in):
    tm = _tile(SEQ, 512)

    def body(a_ref, b_ref, o_ref):
        o_ref[...] = DEEPNORM_ALPHA * a_ref[...] + b_ref[...]

    blk = pl.BlockSpec((tm, D_MODEL), lambda i: (i, 0))
    return pl.pallas_call(body, name="grad_x", grid=(SEQ // tm,), in_specs=[blk, blk], out_specs=blk,
                          out_shape=_sds((SEQ, D_MODEL)), compiler_params=_cparams(("parallel",)))(dr1, dx_in)
```
